```python
import math
import jax, jax.numpy as jnp
from jax import lax
import numpy as np

D_MODEL = 1024
BATCH = 16
SEQ = 4096
DEPTH = 2

CONV_WIDTH = D_MODEL // 2
CONV_K = 31
DN_DK = 128
DN_DV = 128
DN_HEADS = D_MODEL // 128
DN_SHORT_K = 4
DN_CHUNK = 64
SG_WIDTH = D_MODEL // 2
SG_GROUPS = 4
SG_CHUNK = 128
N_BRANCH = 3
NORM_EPS = 1e-6

SPLIT_SIZES = (
    CONV_WIDTH,
    CONV_WIDTH,
    CONV_WIDTH,
    DN_HEADS * (2 * DN_DK + DN_DV),
    DN_HEADS * DN_DV,
    DN_HEADS,
    DN_HEADS,
    SG_WIDTH,
    SG_WIDTH,
    SG_WIDTH,
    N_BRANCH * D_MODEL,
)
N_IN_COLS = sum(SPLIT_SIZES)

kernel_name = "hybrid_conv_deltanet_gmlp_gated_merge"


def _rmsnorm(x, g):
    xf = x.astype(jnp.float32)
    y = xf * lax.rsqrt(jnp.mean(xf * xf, axis=-1, keepdims=True) + NORM_EPS)
    return (y * g.astype(jnp.float32)).astype(x.dtype)


def _layernorm(x, g, b):
    xf = x.astype(jnp.float32)
    mu = jnp.mean(xf, axis=-1, keepdims=True)
    var = jnp.mean(jnp.square(xf - mu), axis=-1, keepdims=True)
    y = (xf - mu) * lax.rsqrt(var + NORM_EPS)
    return (y * g.astype(jnp.float32) + b.astype(jnp.float32)).astype(x.dtype)


def _l2norm(x):
    xf = x.astype(jnp.float32)
    return xf * lax.rsqrt(jnp.sum(xf * xf, axis=-1, keepdims=True) + NORM_EPS)


def _causal_dwconv(x, w):
    k, c = w.shape
    xp = jnp.pad(x, ((0, 0), (k - 1, 0), (0, 0)))
    return lax.conv_general_dilated(
        xp, w.astype(x.dtype)[:, None, :], window_strides=(1,), padding="VALID",
        dimension_numbers=("NWC", "WIO", "NWC"), feature_group_count=c)


def _gated_delta_rule(q, k, v, beta, g):
    bsz, t, h, dk = q.shape
    dv = v.shape[-1]
    c = DN_CHUNK
    n = t // c

    def chunks(a):
        a = a.reshape((bsz, n, c, h) + a.shape[3:])
        return jnp.moveaxis(a, 3, 1)

    q, k, v, beta, g = chunks(q), chunks(k), chunks(v), chunks(beta), chunks(g)
    gc = jnp.cumsum(g, axis=-1)
    diff = gc[..., :, None] - gc[..., None, :]
    incl = jnp.tril(jnp.ones((c, c), dtype=bool))
    strict = jnp.tril(jnp.ones((c, c), dtype=bool), k=-1)
    gamma_incl = jnp.exp(jnp.where(incl, diff, -jnp.inf))
    gamma_strict = jnp.exp(jnp.where(strict, diff, -jnp.inf))

    kk = jnp.einsum("bhnid,bhnjd->bhnij", k, k)
    a_mat = jnp.eye(c, dtype=jnp.float32) + beta[..., :, None] * kk * gamma_strict
    rhs = jnp.concatenate([v * beta[..., None],
                           k * (beta * jnp.exp(gc))[..., None]], axis=-1)
    sol = lax.linalg.triangular_solve(a_mat, rhs, left_side=True, lower=True,
                                      unit_diagonal=True)
    u, w = sol[..., :dv], sol[..., dv:]

    qk = jnp.einsum("bhnid,bhnjd->bhnij", q, k) * gamma_incl
    q_dec = q * jnp.exp(gc)[..., None]
    k_dec = k * jnp.exp(gc[..., -1:] - gc)[..., None]
    d_last = jnp.exp(gc[..., -1])

    def step(s, xs):
        u_c, w_c, qd_c, qk_c, kd_c, dl_c = xs
        v_new = u_c - jnp.einsum("bhcd,bhde->bhce", w_c, s)
        o_c = (jnp.einsum("bhcd,bhde->bhce", qd_c, s)
               + jnp.einsum("bhij,bhje->bhie", qk_c, v_new))
        s = dl_c[..., None, None] * s + jnp.einsum("bhcd,bhce->bhde", kd_c, v_new)
        return s, o_c

    xs = tuple(jnp.moveaxis(a, 2, 0) for a in (u, w, q_dec, qk, k_dec, d_last))
    s0 = jnp.zeros((bsz, h, dk, dv), jnp.float32)
    _, o = lax.scan(step, s0, xs)
    o = jnp.transpose(o, (1, 0, 3, 2, 4))
    return o.reshape(bsz, t, h, dv)


def _layer(x, norm_g, w_in, a_dw, a_dw_b, a_ln_g, a_ln_b, a_proj,
           b_conv, b_a_log, b_dt_bias, b_onorm_g, b_proj,
           c_ln_g, c_ln_b, c_ws, c_bs, c_proj, w_out):
    bsz, t, _ = x.shape
    h = _rmsnorm(x, norm_g)
    proj = h @ w_in
    idx = [int(i) for i in np.cumsum(SPLIT_SIZES)[:-1]]
    (a_val, a_glu, a_z, qkv, b_z, b_beta, b_alpha,
     c_u, c_v, c_z, gate_logits) = jnp.split(proj, idx, axis=-1)

    a = a_val * jax.nn.sigmoid(a_glu)
    a = _causal_dwconv(a, a_dw) + a_dw_b
    a = _layernorm(a, a_ln_g, a_ln_b)
    y_a = jax.nn.silu(a) * jax.nn.silu(a_z)

    qkv = jax.nn.silu(_causal_dwconv(qkv, b_conv))
    q, k, v = jnp.split(qkv, [DN_HEADS * DN_DK, 2 * DN_HEADS * DN_DK], axis=-1)
    q = _l2norm(q.reshape(bsz, t, DN_HEADS, DN_DK)) * (DN_DK ** -0.5)
    k = _l2norm(k.reshape(bsz, t, DN_HEADS, DN_DK))
    v = v.reshape(bsz, t, DN_HEADS, DN_DV).astype(jnp.float32)
    beta = jax.nn.sigmoid(b_beta.astype(jnp.float32))
    g = -jnp.exp(b_a_log.astype(jnp.float32)) * jax.nn.softplus(
        b_alpha.astype(jnp.float32) + b_dt_bias.astype(jnp.float32))
    o = _gated_delta_rule(q, k, v, beta, g)
    o = _rmsnorm(o, b_onorm_g)
    z = jax.nn.silu(b_z.reshape(bsz, t, DN_HEADS, DN_DV).astype(jnp.float32))
    y_b = (o * z).reshape(bsz, t, DN_HEADS * DN_DV).astype(x.dtype)

    u = jax.nn.gelu(c_u)
    vs = _layernorm(jax.nn.gelu(c_v), c_ln_g, c_ln_b)
    vs = vs.reshape(bsz, t // SG_CHUNK, SG_CHUNK, SG_GROUPS, SG_WIDTH // SG_GROUPS)
    ws = jnp.tril(c_ws)
    mixed = jnp.einsum("gij,bnjgc->bnigc", ws, vs) + c_bs.T[:, :, None]
    y_c = u * mixed.reshape(bsz, t, SG_WIDTH) * jax.nn.silu(c_z)

    gates = jax.nn.sigmoid(gate_logits).reshape(bsz, t, N_BRANCH, D_MODEL)
    merged = (gates[..., 0, :] * (y_a @ a_proj)
              + gates[..., 1, :] * (y_b @ b_proj)
              + gates[..., 2, :] * (y_c @ c_proj))
    return x + merged @ w_out


def _fwd_setup_inputs(seed: int = 0) -> dict:
    key = jax.random.key(seed)
    ks = jax.random.split(key, 24)
    f32 = jnp.float32
    L = DEPTH

    def nrm(k, shape, fan_in):
        return jax.random.normal(k, shape, f32) * (fan_in ** -0.5)

    def gain(k, shape):
        return 1.0 + 0.02 * jax.random.normal(k, shape, f32)

    def bias(k, shape):
        return 0.02 * jax.random.normal(k, shape, f32)

    dt = jnp.exp(jax.random.uniform(ks[10], (L, DN_HEADS), f32,
                                    math.log(1e-3), math.log(1e-1)))
    return {
        "x": jax.random.normal(ks[0], (BATCH, SEQ, D_MODEL), f32),
        "norm_g": gain(ks[1], (L, D_MODEL)),
        "w_in": nrm(ks[2], (L, D_MODEL, N_IN_COLS), D_MODEL),
        "a_dw": nrm(ks[3], (L, CONV_K, CONV_WIDTH), CONV_K),
        "a_dw_b": bias(ks[4], (L, CONV_WIDTH)),
        "a_ln_g": gain(ks[5], (L, CONV_WIDTH)),
        "a_ln_b": bias(ks[6], (L, CONV_WIDTH)),
        "a_proj": nrm(ks[7], (L, CONV_WIDTH, D_MODEL), CONV_WIDTH),
        "b_conv": nrm(ks[8], (L, DN_SHORT_K, DN_HEADS * (2 * DN_DK + DN_DV)), DN_SHORT_K),
        "b_a_log": jnp.log(jax.random.uniform(ks[9], (L, DN_HEADS), f32, 1.0, 16.0)),
        "b_dt_bias": dt + jnp.log(-jnp.expm1(-dt)),
        "b_onorm_g": gain(ks[11], (L, DN_DV)),
        "b_proj": nrm(ks[12], (L, DN_HEADS * DN_DV, D_MODEL), DN_HEADS * DN_DV),
        "c_ln_g": gain(ks[13], (L, SG_WIDTH)),
        "c_ln_b": bias(ks[14], (L, SG_WIDTH)),
        "c_ws": nrm(ks[15], (L, SG_GROUPS, SG_CHUNK, SG_CHUNK), SG_CHUNK),
        "c_bs": gain(ks[16], (L, SG_GROUPS, SG_CHUNK)),
        "c_proj": nrm(ks[17], (L, SG_WIDTH, D_MODEL), SG_WIDTH),
        "w_out": nrm(ks[18], (L, D_MODEL, D_MODEL), D_MODEL),
        "final_g": gain(ks[19], (D_MODEL,)),
    }


def _fwd_reference(x, norm_g, w_in, a_dw, a_dw_b, a_ln_g, a_ln_b, a_proj,
              b_conv, b_a_log, b_dt_bias, b_onorm_g, b_proj,
              c_ln_g, c_ln_b, c_ws, c_bs, c_proj, w_out, final_g):
    for l in range(DEPTH):
        x = _layer(x, norm_g[l], w_in[l], a_dw[l], a_dw_b[l], a_ln_g[l], a_ln_b[l],
                   a_proj[l], b_conv[l], b_a_log[l], b_dt_bias[l], b_onorm_g[l],
                   b_proj[l], c_ln_g[l], c_ln_b[l], c_ws[l], c_bs[l], c_proj[l],
                   w_out[l])
    return _rmsnorm(x, final_g)


import jax as _jax
import jax.numpy as _jnp

TWIN_FORMAT = 'train_step'
FWD_PARAMS = ['x', 'norm_g', 'w_in', 'a_dw', 'a_dw_b', 'a_ln_g', 'a_ln_b', 'a_proj', 'b_conv', 'b_a_log', 'b_dt_bias', 'b_onorm_g', 'b_proj', 'c_ln_g', 'c_ln_b', 'c_ws', 'c_bs', 'c_proj', 'w_out', 'final_g']
TWIN_WEIGHTS = ['norm_g', 'w_in', 'a_dw', 'a_dw_b', 'a_ln_g', 'a_ln_b', 'a_proj', 'b_conv', 'b_a_log', 'b_dt_bias', 'b_onorm_g', 'b_proj', 'c_ln_g', 'c_ln_b', 'c_ws', 'c_bs', 'c_proj', 'w_out', 'final_g']
TWIN_DIFF_INPUT = 'x'
TWIN_INPUTS = ['x', 'norm_g', 'w_in', 'a_dw', 'a_dw_b', 'a_ln_g', 'a_ln_b', 'a_proj', 'b_conv', 'b_a_log', 'b_dt_bias', 'b_onorm_g', 'b_proj', 'c_ln_g', 'c_ln_b', 'c_ws', 'c_bs', 'c_proj', 'w_out', 'final_g', 'loss_target', 'm_norm_g', 'm_w_in', 'm_a_dw', 'm_a_dw_b', 'm_a_ln_g', 'm_a_ln_b', 'm_a_proj', 'm_b_conv', 'm_b_a_log', 'm_b_dt_bias', 'm_b_onorm_g', 'm_b_proj', 'm_c_ln_g', 'm_c_ln_b', 'm_c_ws', 'm_c_bs', 'm_c_proj', 'm_w_out', 'm_final_g', 'v_norm_g', 'v_w_in', 'v_a_dw', 'v_a_dw_b', 'v_a_ln_g', 'v_a_ln_b', 'v_a_proj', 'v_b_conv', 'v_b_a_log', 'v_b_dt_bias', 'v_b_onorm_g', 'v_b_proj', 'v_c_ln_g', 'v_c_ln_b', 'v_c_ws', 'v_c_bs', 'v_c_proj', 'v_w_out', 'v_final_g']
TWIN_OUTPUTS = ['loss', 'grad_x', 'grad_norm_g', 'grad_w_in', 'grad_a_dw', 'grad_a_dw_b', 'grad_a_ln_g', 'grad_a_ln_b', 'grad_a_proj', 'grad_b_conv', 'grad_b_a_log', 'grad_b_dt_bias', 'grad_b_onorm_g', 'grad_b_proj', 'grad_c_ln_g', 'grad_c_ln_b', 'grad_c_ws', 'grad_c_bs', 'grad_c_proj', 'grad_w_out', 'grad_final_g', 'delta_norm_g', 'delta_w_in', 'delta_a_dw', 'delta_a_dw_b', 'delta_a_ln_g', 'delta_a_ln_b', 'delta_a_proj', 'delta_b_conv', 'delta_b_a_log', 'delta_b_dt_bias', 'delta_b_onorm_g', 'delta_b_proj', 'delta_c_ln_g', 'delta_c_ln_b', 'delta_c_ws', 'delta_c_bs', 'delta_c_proj', 'delta_w_out', 'delta_final_g', 'new_m_norm_g', 'new_m_w_in', 'new_m_a_dw', 'new_m_a_dw_b', 'new_m_a_ln_g', 'new_m_a_ln_b', 'new_m_a_proj', 'new_m_b_conv', 'new_m_b_a_log', 'new_m_b_dt_bias', 'new_m_b_onorm_g', 'new_m_b_proj', 'new_m_c_ln_g', 'new_m_c_ln_b', 'new_m_c_ws', 'new_m_c_bs', 'new_m_c_proj', 'new_m_w_out', 'new_m_final_g', 'new_v_norm_g', 'new_v_w_in', 'new_v_a_dw', 'new_v_a_dw_b', 'new_v_a_ln_g', 'new_v_a_ln_b', 'new_v_a_proj', 'new_v_b_conv', 'new_v_b_a_log', 'new_v_b_dt_bias', 'new_v_b_onorm_g', 'new_v_b_proj', 'new_v_c_ln_g', 'new_v_c_ln_b', 'new_v_c_ws', 'new_v_c_bs', 'new_v_c_proj', 'new_v_w_out', 'new_v_final_g']
TWIN_LEAF_KINDS = {'loss': 'loss', 'grad_x': 'grad_x', 'grad_norm_g': 'grad_w', 'grad_w_in': 'grad_w', 'grad_a_dw': 'grad_w', 'grad_a_dw_b': 'grad_w', 'grad_a_ln_g': 'grad_w', 'grad_a_ln_b': 'grad_w', 'grad_a_proj': 'grad_w', 'grad_b_conv': 'grad_w', 'grad_b_a_log': 'grad_w', 'grad_b_dt_bias': 'grad_w', 'grad_b_onorm_g': 'grad_w', 'grad_b_proj': 'grad_w', 'grad_c_ln_g': 'grad_w', 'grad_c_ln_b': 'grad_w', 'grad_c_ws': 'grad_w', 'grad_c_bs': 'grad_w', 'grad_c_proj': 'grad_w', 'grad_w_out': 'grad_w', 'grad_final_g': 'grad_w', 'delta_norm_g': 'delta_w', 'delta_w_in': 'delta_w', 'delta_a_dw': 'delta_w', 'delta_a_dw_b': 'delta_w', 'delta_a_ln_g': 'delta_w', 'delta_a_ln_b': 'delta_w', 'delta_a_proj': 'delta_w', 'delta_b_conv': 'delta_w', 'delta_b_a_log': 'delta_w', 'delta_b_dt_bias': 'delta_w', 'delta_b_onorm_g': 'delta_w', 'delta_b_proj': 'delta_w', 'delta_c_ln_g': 'delta_w', 'delta_c_ln_b': 'delta_w', 'delta_c_ws': 'delta_w', 'delta_c_bs': 'delta_w', 'delta_c_proj': 'delta_w', 'delta_w_out': 'delta_w', 'delta_final_g': 'delta_w', 'new_m_norm_g': 'new_m', 'new_m_w_in': 'new_m', 'new_m_a_dw': 'new_m', 'new_m_a_dw_b': 'new_m', 'new_m_a_ln_g': 'new_m', 'new_m_a_ln_b': 'new_m', 'new_m_a_proj': 'new_m', 'new_m_b_conv': 'new_m', 'new_m_b_a_log': 'new_m', 'new_m_b_dt_bias': 'new_m', 'new_m_b_onorm_g': 'new_m', 'new_m_b_proj': 'new_m', 'new_m_c_ln_g': 'new_m', 'new_m_c_ln_b': 'new_m', 'new_m_c_ws': 'new_m', 'new_m_c_bs': 'new_m', 'new_m_c_proj': 'new_m', 'new_m_w_out': 'new_m', 'new_m_final_g': 'new_m', 'new_v_norm_g': 'new_v', 'new_v_w_in': 'new_v', 'new_v_a_dw': 'new_v', 'new_v_a_dw_b': 'new_v', 'new_v_a_ln_g': 'new_v', 'new_v_a_ln_b': 'new_v', 'new_v_a_proj': 'new_v', 'new_v_b_conv': 'new_v', 'new_v_b_a_log': 'new_v', 'new_v_b_dt_bias': 'new_v', 'new_v_b_onorm_g': 'new_v', 'new_v_b_proj': 'new_v', 'new_v_c_ln_g': 'new_v', 'new_v_c_ln_b': 'new_v', 'new_v_c_ws': 'new_v', 'new_v_c_bs': 'new_v', 'new_v_c_proj': 'new_v', 'new_v_w_out': 'new_v', 'new_v_final_g': 'new_v'}


def _forward(args):
    return _fwd_reference(*[args[k] for k in FWD_PARAMS])


def _output_shape():
    out = _jax.eval_shape(lambda: _forward(_fwd_setup_inputs(0)))
    return out.shape, out.dtype

N_MICROBATCH = 1
ADAM_LR = 0.001
ADAM_B1 = 0.9
ADAM_B2 = 0.999
ADAM_EPS = 1e-08
ADAM_WD = 0.01
ADAM_STEP = 10
PER_EXAMPLE_BATCH_AXIS = {'x': 0, 'loss_target': 0}
SHARED_INPUTS = []
_WEIGHT_DTYPES = {'norm_g': _jnp.float32, 'w_in': _jnp.float32, 'a_dw': _jnp.float32, 'a_dw_b': _jnp.float32, 'a_ln_g': _jnp.float32, 'a_ln_b': _jnp.float32, 'a_proj': _jnp.float32, 'b_conv': _jnp.float32, 'b_a_log': _jnp.float32, 'b_dt_bias': _jnp.float32, 'b_onorm_g': _jnp.float32, 'b_proj': _jnp.float32, 'c_ln_g': _jnp.float32, 'c_ln_b': _jnp.float32, 'c_ws': _jnp.float32, 'c_bs': _jnp.float32, 'c_proj': _jnp.float32, 'w_out': _jnp.float32, 'final_g': _jnp.float32}
MOMENT_SCALE = {'norm_g': 1.819327e-01, 'w_in': 5.661789e-02, 'a_dw': 6.693408e-02, 'a_dw_b': 1.415624e-01, 'a_ln_g': 7.757913e-02, 'a_ln_b': 6.587659e-02, 'a_proj': 4.586922e-02, 'b_conv': 5.867728e-02, 'b_a_log': 4.985112e-01, 'b_dt_bias': 5.055141e-01, 'b_onorm_g': 2.251133e-01, 'b_proj': 7.533396e-02, 'c_ln_g': 4.820125e-02, 'c_ln_b': 4.825568e-02, 'c_ws': 5.031645e-02, 'c_bs': 7.129350e-02, 'c_proj': 6.025514e-02, 'w_out': 1.065465e-01, 'final_g': 6.393578e+01}


def _to_microbatches(a, axis):
    t = _jnp.moveaxis(a, axis, 0)
    t = t.reshape((N_MICROBATCH, t.shape[0] // N_MICROBATCH) + t.shape[1:])
    return _jnp.moveaxis(t, 1, axis + 1)


def setup_inputs(seed: int = 0) -> dict:
    inp = _fwd_setup_inputs(seed)
    key = _jax.random.fold_in(_jax.random.key(seed), 7919)
    shape, _ = _output_shape()
    out = dict(inp)
    out["loss_target"] = _jax.random.normal(_jax.random.fold_in(key, 0), shape, _jnp.float32)
    for i, name in enumerate(TWIN_WEIGHTS):
        w = inp[name].astype(_jnp.float32)
        if MOMENT_SCALE is None:
            s = _jnp.sqrt(_jnp.mean(_jnp.square(w)) + 1e-30)
        else:
            s = MOMENT_SCALE[name]
        km, kv = _jax.random.split(_jax.random.fold_in(key, i + 1))
        out[name] = w
        out["m_" + name] = s * _jax.random.normal(km, w.shape, _jnp.float32)
        out["v_" + name] = (s * s) * _jax.random.uniform(kv, w.shape, _jnp.float32, 0.5, 1.5)
    if N_MICROBATCH > 1:
        for name, axis in PER_EXAMPLE_BATCH_AXIS.items():
            out[name] = _to_microbatches(out[name], axis)
    return {'x': out['x'], 'norm_g': out['norm_g'], 'w_in': out['w_in'], 'a_dw': out['a_dw'], 'a_dw_b': out['a_dw_b'], 'a_ln_g': out['a_ln_g'], 'a_ln_b': out['a_ln_b'], 'a_proj': out['a_proj'], 'b_conv': out['b_conv'], 'b_a_log': out['b_a_log'], 'b_dt_bias': out['b_dt_bias'], 'b_onorm_g': out['b_onorm_g'], 'b_proj': out['b_proj'], 'c_ln_g': out['c_ln_g'], 'c_ln_b': out['c_ln_b'], 'c_ws': out['c_ws'], 'c_bs': out['c_bs'], 'c_proj': out['c_proj'], 'w_out': out['w_out'], 'final_g': out['final_g'], 'loss_target': out['loss_target'], 'm_norm_g': out['m_norm_g'], 'm_w_in': out['m_w_in'], 'm_a_dw': out['m_a_dw'], 'm_a_dw_b': out['m_a_dw_b'], 'm_a_ln_g': out['m_a_ln_g'], 'm_a_ln_b': out['m_a_ln_b'], 'm_a_proj': out['m_a_proj'], 'm_b_conv': out['m_b_conv'], 'm_b_a_log': out['m_b_a_log'], 'm_b_dt_bias': out['m_b_dt_bias'], 'm_b_onorm_g': out['m_b_onorm_g'], 'm_b_proj': out['m_b_proj'], 'm_c_ln_g': out['m_c_ln_g'], 'm_c_ln_b': out['m_c_ln_b'], 'm_c_ws': out['m_c_ws'], 'm_c_bs': out['m_c_bs'], 'm_c_proj': out['m_c_proj'], 'm_w_out': out['m_w_out'], 'm_final_g': out['m_final_g'], 'v_norm_g': out['v_norm_g'], 'v_w_in': out['v_w_in'], 'v_a_dw': out['v_a_dw'], 'v_a_dw_b': out['v_a_dw_b'], 'v_a_ln_g': out['v_a_ln_g'], 'v_a_ln_b': out['v_a_ln_b'], 'v_a_proj': out['v_a_proj'], 'v_b_conv': out['v_b_conv'], 'v_b_a_log': out['v_b_a_log'], 'v_b_dt_bias': out['v_b_dt_bias'], 'v_b_onorm_g': out['v_b_onorm_g'], 'v_b_proj': out['v_b_proj'], 'v_c_ln_g': out['v_c_ln_g'], 'v_c_ln_b': out['v_c_ln_b'], 'v_c_ws': out['v_c_ws'], 'v_c_bs': out['v_c_bs'], 'v_c_proj': out['v_c_proj'], 'v_w_out': out['v_w_out'], 'v_final_g': out['v_final_g']}


def _loss(weights, diff, rest, loss_target):
    with _jax.named_scope("forward"):
        args = {**rest, TWIN_DIFF_INPUT: diff, **{k: w.astype(_WEIGHT_DTYPES[k]) for k, w in weights.items()}}
        y = _forward(args)
    with _jax.named_scope("loss_head"):
        err = _jnp.square(y.astype(_jnp.float32) - loss_target)
        return 0.5 * _jnp.sum(_jnp.mean(err, axis=-1)) if err.ndim else 0.5 * err


def _adamw(w, g, m, v):
    m = ADAM_B1 * m + (1.0 - ADAM_B1) * g
    v = ADAM_B2 * v + (1.0 - ADAM_B2) * _jnp.square(g)
    m_hat = m / (1.0 - ADAM_B1 ** ADAM_STEP)
    v_hat = v / (1.0 - ADAM_B2 ** ADAM_STEP)
    delta = -ADAM_LR * (m_hat / (_jnp.sqrt(v_hat) + ADAM_EPS) + ADAM_WD * w)
    return delta, m, v


def reference(x, norm_g, w_in, a_dw, a_dw_b, a_ln_g, a_ln_b, a_proj, b_conv, b_a_log, b_dt_bias, b_onorm_g, b_proj, c_ln_g, c_ln_b, c_ws, c_bs, c_proj, w_out, final_g, loss_target, m_norm_g, m_w_in, m_a_dw, m_a_dw_b, m_a_ln_g, m_a_ln_b, m_a_proj, m_b_conv, m_b_a_log, m_b_dt_bias, m_b_onorm_g, m_b_proj, m_c_ln_g, m_c_ln_b, m_c_ws, m_c_bs, m_c_proj, m_w_out, m_final_g, v_norm_g, v_w_in, v_a_dw, v_a_dw_b, v_a_ln_g, v_a_ln_b, v_a_proj, v_b_conv, v_b_a_log, v_b_dt_bias, v_b_onorm_g, v_b_proj, v_c_ln_g, v_c_ln_b, v_c_ws, v_c_bs, v_c_proj, v_w_out, v_final_g):
    given = dict(x=x, norm_g=norm_g, w_in=w_in, a_dw=a_dw, a_dw_b=a_dw_b, a_ln_g=a_ln_g, a_ln_b=a_ln_b, a_proj=a_proj, b_conv=b_conv, b_a_log=b_a_log, b_dt_bias=b_dt_bias, b_onorm_g=b_onorm_g, b_proj=b_proj, c_ln_g=c_ln_g, c_ln_b=c_ln_b, c_ws=c_ws, c_bs=c_bs, c_proj=c_proj, w_out=w_out, final_g=final_g, loss_target=loss_target, m_norm_g=m_norm_g, m_w_in=m_w_in, m_a_dw=m_a_dw, m_a_dw_b=m_a_dw_b, m_a_ln_g=m_a_ln_g, m_a_ln_b=m_a_ln_b, m_a_proj=m_a_proj, m_b_conv=m_b_conv, m_b_a_log=m_b_a_log, m_b_dt_bias=m_b_dt_bias, m_b_onorm_g=m_b_onorm_g, m_b_proj=m_b_proj, m_c_ln_g=m_c_ln_g, m_c_ln_b=m_c_ln_b, m_c_ws=m_c_ws, m_c_bs=m_c_bs, m_c_proj=m_c_proj, m_w_out=m_w_out, m_final_g=m_final_g, v_norm_g=v_norm_g, v_w_in=v_w_in, v_a_dw=v_a_dw, v_a_dw_b=v_a_dw_b, v_a_ln_g=v_a_ln_g, v_a_ln_b=v_a_ln_b, v_a_proj=v_a_proj, v_b_conv=v_b_conv, v_b_a_log=v_b_a_log, v_b_dt_bias=v_b_dt_bias, v_b_onorm_g=v_b_onorm_g, v_b_proj=v_b_proj, v_c_ln_g=v_c_ln_g, v_c_ln_b=v_c_ln_b, v_c_ws=v_c_ws, v_c_bs=v_c_bs, v_c_proj=v_c_proj, v_w_out=v_w_out, v_final_g=v_final_g)
    weights = {n: given[n] for n in TWIN_WEIGHTS}
    shared = {n: given[n] for n in SHARED_INPUTS}
    per_example = {n: given[n] for n in ['x']}
    grad_fn = _jax.value_and_grad(_loss, argnums=(0, 1))

    def one_microbatch(ex, loss_target):
        ex = dict(ex)
        diff = ex.pop(TWIN_DIFF_INPUT)
        return grad_fn(weights, diff, {**shared, **ex}, loss_target)

    if N_MICROBATCH == 1:
        loss, (grad_w, grad_x) = one_microbatch(per_example, given["loss_target"])
    else:
        def body(carry, xs):
            loss_sum, grad_sum = carry
            l_k, (gw_k, gx_k) = one_microbatch(xs[0], xs[1])
            with _jax.named_scope("update"):
                return (loss_sum + l_k, _jax.tree.map(_jnp.add, grad_sum, gw_k)), gx_k

        init = (_jnp.zeros((), _jnp.float32), _jax.tree.map(_jnp.zeros_like, weights))
        (loss, grad_w), grad_x = _jax.lax.scan(body, init, (per_example, given["loss_target"]))
    with _jax.named_scope("update"):
        delta_w, new_m, new_v = {}, {}, {}
        for n in TWIN_WEIGHTS:
            delta_w[n], new_m[n], new_v[n] = _adamw(weights[n], grad_w[n], given["m_" + n], given["v_" + n])
    return (loss, grad_x, *[grad_w[n] for n in TWIN_WEIGHTS], *[delta_w[n] for n in TWIN_WEIGHTS],
            *[new_m[n] for n in TWIN_WEIGHTS], *[new_v[n] for n in TWIN_WEIGHTS])
```

```python
import functools
import math

import jax
import jax.numpy as jnp
from jax import lax
from jax.experimental import pallas as pl
from jax.experimental.pallas import tpu as pltpu

F32 = jnp.float32
BF16 = jnp.bfloat16
MESH = pl.DeviceIdType.MESH

D_MODEL = 1024
CONV_W = 512
CONV_K = 31
HEADS = 8
DK = 128
SHORT_K = 4
CHUNK = 64
SG_W = 512
SG_G = 4
SG_C = 128
EPS = 1e-6
N_ORIG = 10256
BA_ORIG = 5632
B_OFF, C_OFF, G_OFF, BA_OFF, N_R = 1536, 5632, 7168, 10240, 10368
TB = 256
HALO_A = 32
HALO_B = 8
VMEM_LIMIT = 56 * 1024 * 1024
ADAM_LR, ADAM_B1, ADAM_B2, ADAM_EPS, ADAM_WD, ADAM_STEP = 0.001, 0.9, 0.999, 1e-08, 0.01, 10
PACK_ROWS = 6688


def _pcall(body, **kw):
    return pl.pallas_call(body, **kw)


def _cparams(sem=None):
    kw = dict(vmem_limit_bytes=VMEM_LIMIT)
    if sem is not None:
        kw["dimension_semantics"] = sem
    return pltpu.CompilerParams(**kw)


def _sig(x):
    return jax.nn.sigmoid(x)


def _silu(x):
    return x * _sig(x)


def _dsilu(x):
    s = _sig(x)
    return s * (1.0 + x * (1.0 - s))


_GELU_C = math.sqrt(2.0 / math.pi)


def _gelu(x):
    return 0.5 * x * (1.0 + jnp.tanh(_GELU_C * (x + 0.044715 * x * x * x)))


def _dgelu(x):
    t = jnp.tanh(_GELU_C * (x + 0.044715 * x * x * x))
    return 0.5 * (1.0 + t) + 0.5 * x * (1.0 - t * t) * _GELU_C * (1.0 + 3 * 0.044715 * x * x)


def _softplus(x):
    return jnp.maximum(x, 0.0) + jnp.log1p(jnp.exp(-jnp.abs(x)))


def _dot(a, b, dims):
    return lax.dot_general(a.astype(BF16), b.astype(BF16), (dims, ((), ())), preferred_element_type=F32)


def _nn(a, b):
    return _dot(a, b, ((1,), (0,)))


def _nt(a, b):
    return _dot(a, b, ((1,), (1,)))


def _tn(a, b):
    return _dot(a, b, ((0,), (0,)))


def _mean(x):
    return jnp.mean(x, axis=-1, keepdims=True)


def _sum0(x):
    return jnp.sum(x, axis=0, keepdims=True)


def _sum1(x):
    return jnp.sum(x, axis=1, keepdims=True)


def _ln_fwd(x, g, b):
    xc = x - _mean(x)
    r = lax.rsqrt(_mean(xc * xc) + EPS)
    xh = xc * r
    return xh * g + b, xh, r


def _ln_bwd(dy, xh, r, g):
    dxh = dy * g
    return r * (dxh - _mean(dxh) - xh * _mean(dxh * xh)), _sum0(dy * xh), _sum0(dy)


def _rms_bwd(dxh, xh, r):
    return r * (dxh - xh * _mean(dxh * xh))


def _lane(x, idx):
    lanes = lax.broadcasted_iota(jnp.int32, x.shape, 1)
    return _sum1(jnp.where(lanes == idx, x, 0.0))


def _rows8(*rows):
    n = rows[0].shape[1]
    return jnp.concatenate(list(rows) + [jnp.zeros((8 - len(rows), n), F32)], axis=0)


def _a_fwd(val, glu, az, val_h, glu_h, w, bias, g, b):
    tb = val.shape[0]
    a_ext = jnp.concatenate([val_h * _sig(glu_h), val * _sig(glu)], axis=0)
    c = a_ext[2:2 + tb] * w[0:1]
    for j in range(1, CONV_K):
        c = c + a_ext[2 + j:2 + j + tb] * w[j:j + 1]
    ln, xh, r = _ln_fwd(c + bias, g, b)
    return _silu(ln) * _silu(az), (a_ext, ln, xh, r)


def _a_bwd(dy, val, glu, az, val_h, glu_h, w, bias, g, b, dc_next):
    tb = val.shape[0]
    _, (a_ext, ln, xh, r) = _a_fwd(val, glu, az, val_h, glu_h, w, bias, g, b)
    dln = dy * _silu(az) * _dsilu(ln)
    daz = dy * _silu(ln) * _dsilu(az)
    dc, dg, db = _ln_bwd(dln, xh, r, g)
    dc_ext = jnp.concatenate([dc, dc_next], axis=0)
    da = dc_ext[30:30 + tb] * w[0:1]
    dw_rows = [_sum0(dc * a_ext[2:2 + tb])]
    for j in range(1, CONV_K):
        da = da + dc_ext[30 - j:30 - j + tb] * w[j:j + 1]
        dw_rows.append(_sum0(dc * a_ext[2 + j:2 + j + tb]))
    dw = jnp.concatenate(dw_rows + [jnp.zeros((1, CONV_W), F32)], axis=0)
    sg = _sig(glu)
    return da * sg, da * val * sg * (1.0 - sg), daz, dw, _sum0(dc), dg, db, dc[:HALO_A]


def _tril(ws):
    ii = lax.broadcasted_iota(jnp.int32, (SG_C, SG_C), 0)
    jj = lax.broadcasted_iota(jnp.int32, (SG_C, SG_C), 1)
    return [jnp.where(jj <= ii, ws[gi], 0.0) for gi in range(SG_G)], jj <= ii


def _c_mix(wt, vs, bias_full):
    tb = vs.shape[0]
    rows = []
    for n in range(tb // SG_C):
        blks = [_nn(wt[gi], vs[n * SG_C:(n + 1) * SG_C, gi * SG_C:(gi + 1) * SG_C]) for gi in range(SG_G)]
        rows.append(jnp.concatenate(blks, axis=1) + bias_full)
    return jnp.concatenate(rows, axis=0)


def _c_fwd(cu, cv, cz, g, b, ws, bias_full):
    wt, _ = _tril(ws)
    vs, xh, r = _ln_fwd(_gelu(cv), g, b)
    mixed = _c_mix(wt, vs, bias_full)
    return _gelu(cu) * mixed * _silu(cz), (wt, vs, xh, r, mixed)


def _c_bwd(dy, cu, cv, cz, g, b, ws, bias_full):
    tb = cu.shape[0]
    _, (wt, vs, xh, r, mixed) = _c_fwd(cu, cv, cz, g, b, ws, bias_full)
    _, low = _tril(ws)
    u, sz = _gelu(cu), _silu(cz)
    dcu = dy * mixed * sz * _dgelu(cu)
    dcz = dy * u * mixed * _dsilu(cz)
    dmixed = dy * u * sz
    dbs = jnp.zeros((SG_C, SG_W), F32)
    dws = [jnp.zeros((SG_C, SG_C), F32) for _ in range(SG_G)]
    rows = []
    for n in range(tb // SG_C):
        dm_n = dmixed[n * SG_C:(n + 1) * SG_C]
        dbs = dbs + dm_n
        blks = []
        for gi in range(SG_G):
            dm = dm_n[:, gi * SG_C:(gi + 1) * SG_C]
            dws[gi] = dws[gi] + _nt(dm, vs[n * SG_C:(n + 1) * SG_C, gi * SG_C:(gi + 1) * SG_C])
            blks.append(_tn(wt[gi], dm))
        rows.append(jnp.concatenate(blks, axis=1))
    dvs = jnp.concatenate(rows, axis=0)
    dgv, dg, db = _ln_bwd(dvs, xh, r, g)
    dws = [jnp.where(low, d, 0.0) for d in dws]
    return dcu, dgv * _dgelu(cv), dcz, dws, dbs, dg, db


def _m_fwd(ya, yb, yc, g0, g1, g2, wa, wb, wc):
    pa, pb, pc = _nn(ya, wa), _nn(yb, wb), _nn(yc, wc)
    s0, s1, s2 = _sig(g0), _sig(g1), _sig(g2)
    return s0 * pa + s1 * pb + s2 * pc, (pa, pb, pc, s0, s1, s2)


def _m_bwd(dout, ya, yb, yc, g0, g1, g2, wa, wb, wc, wo):
    merged, (pa, pb, pc, s0, s1, s2) = _m_fwd(ya, yb, yc, g0, g1, g2, wa, wb, wc)
    dm = _nt(dout, wo)
    dpa, dpb, dpc = dm * s0, dm * s1, dm * s2
    dgs = (dm * pa * s0 * (1.0 - s0), dm * pb * s1 * (1.0 - s1), dm * pc * s2 * (1.0 - s2))
    return (_nt(dpa, wa), _nt(dpb, wb), _nt(dpc, wc)), dgs, merged, (dpa, dpb, dpc)


def _chunk_masks(c):
    ii = lax.broadcasted_iota(jnp.int32, (c, c), 0)
    jj = lax.broadcasted_iota(jnp.int32, (c, c), 1)
    return ii, jj


def _tri_inv(a, ii, jj):
    same = (ii >> 1) == (jj >> 1)
    t = jnp.where(ii == jj, 1.0, 0.0) - jnp.where(same & ((ii & 1) != 0) & ((jj & 1) == 0), a, 0.0)
    b, sh = 2, 2
    while b < a.shape[0]:
        off = ((ii >> sh) == (jj >> sh)) & ((ii & b) != 0) & ((jj & b) == 0)
        t = t - _nn(_nn(t, jnp.where(off, a, 0.0)), t)
        b, sh = 2 * b, sh + 1
    return t


def _chunk_setup(q, k, v, bcol, gcol):
    c = q.shape[0]
    ii, jj = _chunk_masks(c)
    incl, strict, eye = jj <= ii, jj < ii, ii == jj
    grow = _sum0(jnp.where(eye, gcol, 0.0))
    gc_col = _sum1(jnp.where(incl, grow, 0.0))
    gc_row = _sum0(jnp.where(ii <= jj, gcol, 0.0))
    gam_i = jnp.where(incl, jnp.exp(jnp.where(incl, gc_col - gc_row, 0.0)), 0.0)
    gam_s = jnp.where(strict, gam_i, 0.0)
    gl = _sum0(gcol)
    egc, ekd, dl = jnp.exp(gc_col), jnp.exp(gl - gc_col), jnp.exp(gl)
    kk = _nt(k, k)
    a = bcol * kk * gam_s
    t = _tri_inv(a, ii, jj)
    rhs_w = k * (bcol * egc)
    u, w = _nn(t, v * bcol), _nn(t, rhs_w)
    qkr = _nt(q, k)
    return dict(ii=ii, jj=jj, strict=strict, eye=eye, gam_i=gam_i, gam_s=gam_s, egc=egc, ekd=ekd, dl=dl, kk=kk, a=a,
                t=t, rhs_w=rhs_w, u=u, w=w, qk=qkr * gam_i, qd=q * egc, kd=k * ekd)


def _chunk_fwd(q, k, v, bcol, gcol, s):
    st = _chunk_setup(q, k, v, bcol, gcol)
    vnew = st["u"] - _nn(st["w"], s)
    o = _nn(st["qd"], s) + _nn(st["qk"], vnew)
    return o, st["dl"] * s + _tn(st["kd"], vnew)


def _chunk_bwd(q, k, v, bcol, gcol, s, do, ds2):
    st = _chunk_setup(q, k, v, bcol, gcol)
    ii, jj, strict, eye = st["ii"], st["jj"], st["strict"], st["eye"]
    gam_i, gam_s, egc, ekd, dl = st["gam_i"], st["gam_s"], st["egc"], st["ekd"], st["dl"]
    kk, a, t, rhs_w, u, w, qk, qd, kd = (st[n] for n in ("kk", "a", "t", "rhs_w", "u", "w", "qk", "qd", "kd"))
    vnew = u - _nn(w, s)
    dvnew = _tn(qk, do) + _nn(kd, ds2)
    dqk = _nt(do, vnew)
    dqd = _nt(do, s)
    ds = _tn(qd, do) + dl * ds2 - _tn(w, dvnew)
    ddl = _sum0(_sum1(s * ds2))
    dkd = _nt(vnew, ds2)
    dw = -_nt(dvnew, s)
    drhs_u = _tn(t, dvnew)
    drhs_w = _tn(t, dw)
    da = jnp.where(strict, -(_nt(drhs_u, u) + _nt(drhs_w, w)), 0.0)
    dbeta = _sum1(da * kk * gam_s) + _sum1(drhs_u * v) + _sum1(drhs_w * k) * egc
    dkk = da * bcol * gam_s
    e = da * a + dqk * qk
    s_kd = _sum1(dkd * kd)
    dgc_col = _sum1(e) + _sum1(drhs_w * rhs_w) + _sum1(dqd * qd) - s_kd
    dgc_row = _sum0(jnp.where(eye, dgc_col, 0.0)) - _sum0(e)
    dgl = _sum0(s_kd) + ddl * dl
    dg = _sum1(jnp.where(jj >= ii, dgc_row, 0.0)) + dgl
    dqkg = dqk * gam_i
    dq = _nn(dqkg, k) + dqd * egc
    dk = _tn(dqkg, q) + _nn(dkk, k) + _tn(dkk, k) + drhs_w * (bcol * egc) + dkd * ekd
    return dq, dk, drhs_u * bcol, dbeta, dg, ds


def _short_conv(raw, halo, w):
    tb = raw.shape[0]
    ext = jnp.concatenate([halo, raw], axis=0)
    out = ext[5:5 + tb] * w[0:1]
    for j in range(1, SHORT_K):
        out = out + ext[5 + j:5 + j + tb] * w[j:j + 1]
    return out, ext


def _dn_prep(raws, halos, ws, ba, alog, dtb, h):
    pre, ext = zip(*[_short_conv(r, hl, w) for r, hl, w in zip(raws, halos, ws)])
    qc, kc, vc = (_silu(p) for p in pre)
    nq = lax.rsqrt(_sum1(qc * qc) + EPS)
    nk = lax.rsqrt(_sum1(kc * kc) + EPS)
    bcol = _sig(_lane(ba, h))
    xg = _lane(ba, HEADS + h) + _lane(dtb, h)
    ea = jnp.exp(_lane(alog, h))
    gcol = -ea * _softplus(xg)
    return dict(pre=pre, ext=ext, qc=qc, kc=kc, vc=vc, nq=nq, nk=nk, q=qc * nq * (DK ** -0.5), k=kc * nk, bcol=bcol,
                xg=xg, ea=ea, gcol=gcol)


def _dn_out(o, z, og):
    r = lax.rsqrt(_mean(o * o) + EPS)
    xh = o * r
    return xh * og * _silu(z), xh, r


def _sds(shape, dtype):
    return jax.ShapeDtypeStruct(tuple(shape), dtype)


def _matmul(a, b, out_dtype, tm, tn, tk, name):
    m, kd = a.shape
    n = b.shape[1]
    tm, tn, tk = min(tm, m), min(tn, n), min(tk, kd)
    nk = kd // tk

    def body(a_ref, b_ref, o_ref, acc):
        @pl.when(pl.program_id(2) == 0)
        def _():
            acc[...] = jnp.zeros_like(acc)

        acc[...] += jnp.dot(a_ref[...], b_ref[...], preferred_element_type=F32)

        @pl.when(pl.program_id(2) == nk - 1)
        def _():
            o_ref[...] = acc[...].astype(o_ref.dtype)

    return _pcall(
        body, name=name, grid=(m // tm, n // tn, nk),
        in_specs=[pl.BlockSpec((tm, tk), lambda i, j, k: (i, k)), pl.BlockSpec((tk, tn), lambda i, j, k: (k, j))],
        out_specs=pl.BlockSpec((tm, tn), lambda i, j, k: (i, j)), out_shape=_sds((m, n), out_dtype),
        scratch_shapes=[pltpu.VMEM((tm, tn), F32)], compiler_params=_cparams(("parallel", "parallel", "arbitrary")),
    )(a, b)


def _rms_fwd_call(x, g, name):
    m = x.shape[0]
    tm = min(512, m)

    def body(x_ref, g_ref, o_ref):
        xv = x_ref[...]
        o_ref[...] = (xv * lax.rsqrt(_mean(xv * xv) + EPS) * g_ref[...]).astype(BF16)

    return _pcall(
        body, name=name, grid=(m // tm,),
        in_specs=[pl.BlockSpec((tm, D_MODEL), lambda i: (i, 0)), pl.BlockSpec((1, D_MODEL), lambda i: (0, 0))],
        out_specs=pl.BlockSpec((tm, D_MODEL), lambda i: (i, 0)), out_shape=_sds((m, D_MODEL), BF16),
        compiler_params=_cparams(("parallel",)),
    )(x, g)


def _rms_bwd_call(x, dh, g, dres, name):
    m = x.shape[0]
    tm = min(512, m)

    def body(x_ref, dh_ref, g_ref, dr_ref, dx_ref, dg_ref):
        @pl.when(pl.program_id(0) == 0)
        def _():
            dg_ref[...] = jnp.zeros_like(dg_ref)

        xv, dhv = x_ref[...], dh_ref[...]
        r = lax.rsqrt(_mean(xv * xv) + EPS)
        xh = xv * r
        dx_ref[...] = _rms_bwd(dhv * g_ref[...], xh, r) + dr_ref[...]
        dg_ref[...] += _rows8(_sum0(dhv * xh))

    row = pl.BlockSpec((tm, D_MODEL), lambda i: (i, 0))
    return _pcall(
        body, name=name, grid=(m // tm,),
        in_specs=[row, row, pl.BlockSpec((1, D_MODEL), lambda i: (0, 0)), row],
        out_specs=[row, pl.BlockSpec((8, D_MODEL), lambda i: (0, 0))],
        out_shape=[_sds((m, D_MODEL), F32), _sds((8, D_MODEL), F32)], compiler_params=_cparams(("arbitrary",)),
    )(x, dh, g, dres)


def _loss_call(x, tgt, g):
    m = x.shape[0]
    tm = min(512, m)

    def body(x_ref, t_ref, g_ref, dx_ref, dg_ref, l_ref):
        @pl.when(pl.program_id(0) == 0)
        def _():
            dg_ref[...] = jnp.zeros_like(dg_ref)
            l_ref[...] = jnp.zeros_like(l_ref)

        xv = x_ref[...]
        r = lax.rsqrt(_mean(xv * xv) + EPS)
        xh = xv * r
        err = xh * g_ref[...] - t_ref[...]
        dy = err * (1.0 / D_MODEL)
        dx_ref[...] = _rms_bwd(dy * g_ref[...], xh, r)
        dg_ref[...] += _rows8(_sum0(dy * xh))
        l_ref[...] += 0.5 * _sum0(_mean(err * err))

    row = pl.BlockSpec((tm, D_MODEL), lambda i: (i, 0))
    return _pcall(
        body, name="loss_head", grid=(m // tm,),
        in_specs=[row, row, pl.BlockSpec((1, D_MODEL), lambda i: (0, 0))],
        out_specs=[row, pl.BlockSpec((8, D_MODEL), lambda i: (0, 0)), pl.BlockSpec((8, 128), lambda i: (0, 0))],
        out_shape=[_sds((m, D_MODEL), F32), _sds((8, D_MODEL), F32), _sds((8, 128), F32)],
        compiler_params=_cparams(("arbitrary",)),
    )(x, tgt, g)


def _halo_idx(i, rows):
    return jnp.maximum(i * (TB // rows) - 1, 0)


def _a_specs(nt, rev):
    ti = (lambda i: nt - 1 - i) if rev else (lambda i: i)
    tile = [pl.BlockSpec((1, TB, CONV_W), functools.partial(lambda b, i, c: (b, ti(i), c), c=c)) for c in range(3)]
    halo = [pl.BlockSpec((1, HALO_A, CONV_W), functools.partial(lambda b, i, c: (b, _halo_idx(ti(i), HALO_A), c), c=c))
            for c in range(2)]
    par = [pl.BlockSpec((HALO_A, CONV_W), lambda b, i: (0, 0))] + [pl.BlockSpec((1, CONV_W), lambda b, i: (0, 0))] * 3
    return tile + halo + par


def _a_fwd_call(proj, w, bias, g, b, name):
    bsz, t, _ = proj.shape
    nt = t // TB

    def body(val_ref, glu_ref, az_ref, vh_ref, gh_ref, w_ref, bias_ref, g_ref, b_ref, y_ref):
        keep = jnp.where(pl.program_id(1) > 0, 1.0, 0.0)
        y, _ = _a_fwd(val_ref[0], glu_ref[0], az_ref[0], vh_ref[0] * keep, gh_ref[0], w_ref[...], bias_ref[...],
                      g_ref[...], b_ref[...])
        y_ref[0] = y.astype(BF16)

    return _pcall(
        body, name=name, grid=(bsz, nt), in_specs=_a_specs(nt, False),
        out_specs=pl.BlockSpec((1, TB, CONV_W), lambda b, i: (b, i, 0)), out_shape=_sds((bsz, t, CONV_W), BF16),
        compiler_params=_cparams(("parallel", "parallel")),
    )(proj, proj, proj, proj, proj, w, bias, g, b)


def _a_bwd_call(dy, proj, w, bias, g, b, name):
    bsz, t, _ = proj.shape
    nt = t // TB

    def body(dy_ref, val_ref, glu_ref, az_ref, vh_ref, gh_ref, w_ref, bias_ref, g_ref, b_ref, da_ref, dw_ref, ds_ref,
             carry):
        ip = pl.program_id(1)

        @pl.when((pl.program_id(0) == 0) & (ip == 0))
        def _():
            dw_ref[...] = jnp.zeros_like(dw_ref)
            ds_ref[...] = jnp.zeros_like(ds_ref)

        @pl.when(ip == 0)
        def _():
            carry[...] = jnp.zeros_like(carry)

        keep = jnp.where(ip < nt - 1, 1.0, 0.0)
        dval, dglu, daz, dw, dbias, dg, db, head = _a_bwd(
            dy_ref[0], val_ref[0], glu_ref[0], az_ref[0], vh_ref[0] * keep, gh_ref[0], w_ref[...], bias_ref[...],
            g_ref[...], b_ref[...], carry[...])
        carry[...] = head
        da_ref[0] = jnp.concatenate([dval, dglu, daz], axis=1).astype(BF16)
        dw_ref[...] += dw
        ds_ref[...] += _rows8(dbias, dg, db)

    rtile = lambda b, i: (b, nt - 1 - i, 0)
    return _pcall(
        body, name=name, grid=(bsz, nt),
        in_specs=[pl.BlockSpec((1, TB, CONV_W), rtile)] + _a_specs(nt, True),
        out_specs=[pl.BlockSpec((1, TB, 3 * CONV_W), rtile), pl.BlockSpec((HALO_A, CONV_W), lambda b, i: (0, 0)),
                   pl.BlockSpec((8, CONV_W), lambda b, i: (0, 0))],
        out_shape=[_sds((bsz, t, 3 * CONV_W), BF16), _sds((HALO_A, CONV_W), F32), _sds((8, CONV_W), F32)],
        scratch_shapes=[pltpu.VMEM((HALO_A, CONV_W), F32)], compiler_params=_cparams(("arbitrary", "arbitrary")),
    )(dy, proj, proj, proj, proj, proj, w, bias, g, b)


def _c_specs():
    c0 = C_OFF // SG_W
    tile = [pl.BlockSpec((1, TB, SG_W), functools.partial(lambda b, i, c: (b, i, c), c=c0 + c)) for c in range(3)]
    par = [pl.BlockSpec((1, SG_W), lambda b, i: (0, 0))] * 2 + [
        pl.BlockSpec((SG_G, SG_C, SG_C), lambda b, i: (0, 0, 0)), pl.BlockSpec((SG_C, SG_W), lambda b, i: (0, 0))]
    return tile + par


def _c_fwd_call(proj, g, b, ws, bias_full, name):
    bsz, t, _ = proj.shape

    def body(cu_ref, cv_ref, cz_ref, g_ref, b_ref, ws_ref, bf_ref, y_ref):
        y, _ = _c_fwd(cu_ref[0], cv_ref[0], cz_ref[0], g_ref[...], b_ref[...], ws_ref[...], bf_ref[...])
        y_ref[0] = y.astype(BF16)

    return _pcall(
        body, name=name, grid=(bsz, t // TB), in_specs=_c_specs(),
        out_specs=pl.BlockSpec((1, TB, SG_W), lambda b, i: (b, i, 0)), out_shape=_sds((bsz, t, SG_W), BF16),
        compiler_params=_cparams(("parallel", "parallel")),
    )(proj, proj, proj, g, b, ws, bias_full)


def _c_bwd_call(dy, proj, g, b, ws, bias_full, name):
    bsz, t, _ = proj.shape

    def body(dy_ref, cu_ref, cv_ref, cz_ref, g_ref, b_ref, ws_ref, bf_ref, dc_ref, dws_ref, dbs_ref, ds_ref):
        @pl.when((pl.program_id(0) == 0) & (pl.program_id(1) == 0))
        def _():
            dws_ref[...] = jnp.zeros_like(dws_ref)
            dbs_ref[...] = jnp.zeros_like(dbs_ref)
            ds_ref[...] = jnp.zeros_like(ds_ref)

        dcu, dcv, dcz, dws, dbs, dg, db = _c_bwd(dy_ref[0], cu_ref[0], cv_ref[0], cz_ref[0], g_ref[...], b_ref[...],
                                                 ws_ref[...], bf_ref[...])
        dc_ref[0] = jnp.concatenate([dcu, dcv, dcz], axis=1).astype(BF16)
        for gi in range(SG_G):
            dws_ref[gi] += dws[gi]
        dbs_ref[...] += dbs
        ds_ref[...] += _rows8(dg, db)

    tile = lambda b, i: (b, i, 0)
    return _pcall(
        body, name=name, grid=(bsz, t // TB), in_specs=[pl.BlockSpec((1, TB, SG_W), tile)] + _c_specs(),
        out_specs=[pl.BlockSpec((1, TB, 3 * SG_W), tile), pl.BlockSpec((SG_G, SG_C, SG_C), lambda b, i: (0, 0, 0)),
                   pl.BlockSpec((SG_C, SG_W), lambda b, i: (0, 0)), pl.BlockSpec((8, SG_W), lambda b, i: (0, 0))],
        out_shape=[_sds((bsz, t, 3 * SG_W), BF16), _sds((SG_G, SG_C, SG_C), F32), _sds((SG_C, SG_W), F32),
                   _sds((8, SG_W), F32)],
        compiler_params=_cparams(("arbitrary", "arbitrary")),
    )(dy, proj, proj, proj, g, b, ws, bias_full)


def _m_specs():
    g0 = G_OFF // D_MODEL
    y = [pl.BlockSpec((1, TB, n), lambda b, i: (b, i, 0)) for n in (CONV_W, D_MODEL, SG_W)]
    gates = [pl.BlockSpec((1, TB, D_MODEL), functools.partial(lambda b, i, c: (b, i, c), c=g0 + c)) for c in range(3)]
    return y + gates


def _w_specs(*shapes):
    return [pl.BlockSpec(s, lambda b, i: (0, 0)) for s in shapes]


def _m_fwd_call(ya, yb, yc, proj, x, wa, wb, wc, wo, name):
    bsz, t, _ = x.shape

    def body(ya_ref, yb_ref, yc_ref, g0_ref, g1_ref, g2_ref, x_ref, wa_ref, wb_ref, wc_ref, wo_ref, o_ref):
        merged, _ = _m_fwd(ya_ref[0], yb_ref[0], yc_ref[0], g0_ref[0], g1_ref[0], g2_ref[0], wa_ref[...], wb_ref[...],
                           wc_ref[...])
        o_ref[0] = x_ref[0] + _nn(merged, wo_ref[...])

    tile = pl.BlockSpec((1, TB, D_MODEL), lambda b, i: (b, i, 0))
    return _pcall(
        body, name=name, grid=(bsz, t // TB),
        in_specs=_m_specs() + [tile] + _w_specs(wa.shape, wb.shape, wc.shape, wo.shape),
        out_specs=tile, out_shape=_sds(x.shape, F32), compiler_params=_cparams(("parallel", "parallel")),
    )(ya, yb, yc, proj, proj, proj, x, wa, wb, wc, wo)


def _m_bwd_call(dout, ya, yb, yc, proj, wa, wb, wc, wo, name):
    bsz, t, _ = dout.shape

    def body(do_ref, ya_ref, yb_ref, yc_ref, g0_ref, g1_ref, g2_ref, wa_ref, wb_ref, wc_ref, wo_ref, dya_ref, dyb_ref,
             dyc_ref, dg_ref, mg_ref, dp_ref):
        dys, dgs, merged, dps = _m_bwd(do_ref[0], ya_ref[0], yb_ref[0], yc_ref[0], g0_ref[0], g1_ref[0], g2_ref[0],
                                       wa_ref[...], wb_ref[...], wc_ref[...], wo_ref[...])
        dya_ref[0], dyb_ref[0], dyc_ref[0] = dys
        dg_ref[0] = jnp.concatenate(dgs, axis=1).astype(BF16)
        mg_ref[0] = merged.astype(BF16)
        dp_ref[0] = jnp.concatenate(dps, axis=1).astype(BF16)

    tile = lambda n: pl.BlockSpec((1, TB, n), lambda b, i: (b, i, 0))
    widths = (CONV_W, D_MODEL, SG_W, 3 * D_MODEL, D_MODEL, 3 * D_MODEL)
    dts = (F32, F32, F32, BF16, BF16, BF16)
    return _pcall(
        body, name=name, grid=(bsz, t // TB),
        in_specs=[tile(D_MODEL)] + _m_specs() + _w_specs(wa.shape, wb.shape, wc.shape, wo.shape),
        out_specs=[tile(n) for n in widths], out_shape=[_sds((bsz, t, n), d) for n, d in zip(widths, dts)],
        compiler_params=_cparams(("parallel", "parallel")),
    )(dout, ya, yb, yc, proj, proj, proj, wa, wb, wc, wo)


def _m_wgrad_call(ya, yb, yc, merged, dps, dout, name):
    bsz, t, _ = dout.shape

    def body(ya_ref, yb_ref, yc_ref, mg_ref, dp_ref, do_ref, dwa_ref, dwb_ref, dwc_ref, dwo_ref):
        @pl.when((pl.program_id(0) == 0) & (pl.program_id(1) == 0))
        def _():
            for r in (dwa_ref, dwb_ref, dwc_ref, dwo_ref):
                r[...] = jnp.zeros_like(r)

        dp = dp_ref[0]
        dwa_ref[...] += _tn(ya_ref[0], dp[:, :D_MODEL])
        dwb_ref[...] += _tn(yb_ref[0], dp[:, D_MODEL:2 * D_MODEL])
        dwc_ref[...] += _tn(yc_ref[0], dp[:, 2 * D_MODEL:])
        dwo_ref[...] += _tn(mg_ref[0], do_ref[0])

    tile = lambda n: pl.BlockSpec((1, TB, n), lambda b, i: (b, i, 0))
    shapes = ((CONV_W, D_MODEL), (D_MODEL, D_MODEL), (SG_W, D_MODEL), (D_MODEL, D_MODEL))
    return _pcall(
        body, name=name, grid=(bsz, t // TB),
        in_specs=[tile(CONV_W), tile(D_MODEL), tile(SG_W), tile(D_MODEL), tile(3 * D_MODEL), tile(D_MODEL)],
        out_specs=_w_specs(*shapes), out_shape=[_sds(s, F32) for s in shapes],
        compiler_params=_cparams(("arbitrary", "arbitrary")),
    )(ya, yb, yc, merged, dps, dout)


def _dn_specs(nt, rev):
    ti = (lambda i: nt - 1 - i) if rev else (lambda i: i)
    c0 = B_OFF // DK
    tile = [pl.BlockSpec((1, TB, DK), functools.partial(lambda b, i, h, c: (b, ti(i), c + h), c=c0 + HEADS * c))
            for c in range(4)]
    ba = [pl.BlockSpec((1, TB, DK), lambda b, i, h: (b, ti(i), BA_OFF // DK))]
    halo = [pl.BlockSpec((1, HALO_B, DK),
                         functools.partial(lambda b, i, h, c: (b, _halo_idx(ti(i), HALO_B), c + h), c=c0 + HEADS * c))
            for c in range(3)]
    wconv = [pl.BlockSpec((SHORT_K, DK), functools.partial(lambda b, i, h, c: (0, c + h), c=HEADS * c)) for c in range(3)]
    par = [pl.BlockSpec((1, DK), lambda b, i, h: (0, 0))] * 3
    return tile + ba + halo + wconv + par


def _dn_fwd_call(proj, wconv, alog, dtb, og, name):
    bsz, t, _ = proj.shape
    nt, nc = t // TB, TB // CHUNK

    def body(q_ref, k_ref, v_ref, z_ref, ba_ref, qh_ref, kh_ref, vh_ref, wq_ref, wk_ref, wv_ref, al_ref, dt_ref, og_ref,
             y_ref, o_ref, st_ref, s_scr):
        i, h = pl.program_id(1), pl.program_id(2)

        @pl.when(i == 0)
        def _():
            s_scr[h] = jnp.zeros((DK, DK), F32)

        keep = jnp.where(i > 0, 1.0, 0.0)
        p = _dn_prep((q_ref[0], k_ref[0], v_ref[0]), (qh_ref[0] * keep, kh_ref[0] * keep, vh_ref[0] * keep),
                     (wq_ref[...], wk_ref[...], wv_ref[...]), ba_ref[0], al_ref[...], dt_ref[...], h)
        s = s_scr[h]
        outs = []
        for c in range(nc):
            sl = slice(c * CHUNK, (c + 1) * CHUNK)
            st_ref[0, 0, c] = s
            o, s = _chunk_fwd(p["q"][sl], p["k"][sl], p["vc"][sl], p["bcol"][sl], p["gcol"][sl], s)
            outs.append(o)
        s_scr[h] = s
        o = jnp.concatenate(outs, axis=0)
        o_ref[0] = o
        y_ref[0] = _dn_out(o, z_ref[0], og_ref[...])[0].astype(BF16)

    head = pl.BlockSpec((1, TB, DK), lambda b, i, h: (b, i, h))
    return _pcall(
        body, name=name, grid=(bsz, nt, HEADS), in_specs=_dn_specs(nt, False),
        out_specs=[head, head, pl.BlockSpec((1, 1, nc, DK, DK), lambda b, i, h: (b, h, i, 0, 0))],
        out_shape=[_sds((bsz, t, D_MODEL), BF16), _sds((bsz, t, D_MODEL), F32),
                   _sds((bsz, HEADS, t // CHUNK, DK, DK), F32)],
        scratch_shapes=[pltpu.VMEM((HEADS, DK, DK), F32)], compiler_params=_cparams(("arbitrary",) * 3),
    )(proj, proj, proj, proj, proj, proj, proj, proj, wconv, wconv, wconv, alog, dtb, og)


def _dn_bwd_call(dy, o_raw, states, proj, wconv, alog, dtb, og, name):
    bsz, t, _ = proj.shape
    nt, nc = t // TB, TB // CHUNK

    def body(dy_ref, o_ref, st_ref, q_ref, k_ref, v_ref, z_ref, ba_ref, qh_ref, kh_ref, vh_ref, wq_ref, wk_ref, wv_ref,
             al_ref, dt_ref, og_ref, dq_ref, dk_ref, dv_ref, dz_ref, dba_ref, dwc_ref, dsm_ref, ds_scr, dpre_scr):
        ip, h = pl.program_id(1), pl.program_id(2)

        @pl.when((pl.program_id(0) == 0) & (ip == 0) & (h == 0))
        def _():
            dwc_ref[...] = jnp.zeros_like(dwc_ref)
            dsm_ref[...] = jnp.zeros_like(dsm_ref)

        @pl.when(ip == 0)
        def _():
            ds_scr[h] = jnp.zeros((DK, DK), F32)
            for n in range(3):
                dpre_scr[h, n] = jnp.zeros((HALO_B, DK), F32)

        keep = jnp.where(ip < nt - 1, 1.0, 0.0)
        ws = (wq_ref[...], wk_ref[...], wv_ref[...])
        p = _dn_prep((q_ref[0], k_ref[0], v_ref[0]), (qh_ref[0] * keep, kh_ref[0] * keep, vh_ref[0] * keep), ws,
                     ba_ref[0], al_ref[...], dt_ref[...], h)
        z, og_v, dyv = z_ref[0], og_ref[...], dy_ref[0]
        _, xh, r = _dn_out(o_ref[0], z, og_v)
        dz_ref[0] = (dyv * xh * og_v * _dsilu(z)).astype(BF16)
        don = dyv * _silu(z)
        do = _rms_bwd(don * og_v, xh, r)
        ds = ds_scr[h]
        parts = [None] * nc
        for c in reversed(range(nc)):
            sl = slice(c * CHUNK, (c + 1) * CHUNK)
            *parts[c], ds = _chunk_bwd(p["q"][sl], p["k"][sl], p["vc"][sl], p["bcol"][sl], p["gcol"][sl], st_ref[0, 0, c],
                                       do[sl], ds)
        ds_scr[h] = ds
        dq, dk, dv, dbeta, dg = (jnp.concatenate([parts[c][n] for c in range(nc)], axis=0) for n in range(5))
        qc, kc, nq, nk = p["qc"], p["kc"], p["nq"], p["nk"]
        dqc = (DK ** -0.5) * nq * (dq - qc * (nq * nq * _sum1(dq * qc)))
        dkc = nk * (dk - kc * (nk * nk * _sum1(dk * kc)))
        out_refs = (dq_ref, dk_ref, dv_ref)
        for n, dcv in enumerate((dqc, dkc, dv)):
            dpre = dcv * _dsilu(p["pre"][n])
            dext = jnp.concatenate([dpre, dpre_scr[h, n]], axis=0)
            ext, w = p["ext"][n], ws[n]
            draw = dext[3:3 + TB] * w[0:1]
            rows = [_sum0(dpre * ext[5:5 + TB])]
            for j in range(1, SHORT_K):
                draw = draw + dext[3 - j:3 - j + TB] * w[j:j + 1]
                rows.append(_sum0(dpre * ext[5 + j:5 + j + TB]))
            dpre_scr[h, n] = dpre[:HALO_B]
            out_refs[n][0] = draw.astype(BF16)
            col = pl.multiple_of((n * HEADS + h) * DK, DK)
            dwc_ref[:, pl.ds(col, DK)] += _rows8(*rows)
        bcol, gcol = p["bcol"], p["gcol"]
        dbl = dbeta * bcol * (1.0 - bcol)
        dal = dg * (-p["ea"]) * _sig(p["xg"])
        lanes = lax.broadcasted_iota(jnp.int32, (TB, DK), 1)
        dba = jnp.where(lanes == h, dbl, 0.0) + jnp.where(lanes == HEADS + h, dal, 0.0)

        @pl.when(h == 0)
        def _():
            dba_ref[0] = dba

        @pl.when(h > 0)
        def _():
            dba_ref[0] += dba

        lane8 = lax.broadcasted_iota(jnp.int32, (8, DK), 1)
        row8 = lax.broadcasted_iota(jnp.int32, (8, DK), 0)
        dsm_ref[...] += (jnp.where((row8 == 0) & (lane8 == h), _sum0(dg * gcol), 0.0)
                         + jnp.where((row8 == 1) & (lane8 == h), _sum0(dal), 0.0)
                         + jnp.where(row8 == 2, _sum0(don * xh), 0.0))

    rhead = pl.BlockSpec((1, TB, DK), lambda b, i, h: (b, nt - 1 - i, h))
    return _pcall(
        body, name=name, grid=(bsz, nt, HEADS),
        in_specs=[rhead, rhead, pl.BlockSpec((1, 1, nc, DK, DK), lambda b, i, h: (b, h, nt - 1 - i, 0, 0))]
        + _dn_specs(nt, True),
        out_specs=[rhead] * 4 + [pl.BlockSpec((1, TB, DK), lambda b, i, h: (b, nt - 1 - i, 0)),
                                 pl.BlockSpec((8, 3 * D_MODEL), lambda b, i, h: (0, 0)),
                                 pl.BlockSpec((8, DK), lambda b, i, h: (0, 0))],
        out_shape=[_sds((bsz, t, D_MODEL), BF16)] * 4 + [_sds((bsz, t, DK), F32), _sds((8, 3 * D_MODEL), F32),
                                                          _sds((8, DK), F32)],
        scratch_shapes=[pltpu.VMEM((HEADS, DK, DK), F32), pltpu.VMEM((HEADS, 3, HALO_B, DK), F32)],
        compiler_params=_cparams(("arbitrary",) * 3),
    )(dy, o_raw, states, proj, proj, proj, proj, proj, proj, proj, proj, wconv, wconv, wconv, alog, dtb, og)


def _adamw_call(w, g, m, v, name):
    shape = w.shape
    cols = shape[-1]
    rows = math.prod(shape[:-1]) if len(shape) > 1 else 1
    tr = 128 if rows % 128 == 0 else rows
    c1, c2 = 1.0 - ADAM_B1 ** ADAM_STEP, 1.0 - ADAM_B2 ** ADAM_STEP

    def body(w_ref, g_ref, m_ref, v_ref, d_ref, nm_ref, nv_ref):
        gv = g_ref[...]
        nm = ADAM_B1 * m_ref[...] + (1.0 - ADAM_B1) * gv
        nv = ADAM_B2 * v_ref[...] + (1.0 - ADAM_B2) * (gv * gv)
        d_ref[...] = -ADAM_LR * ((nm / c1) / (jnp.sqrt(nv / c2) + ADAM_EPS) + ADAM_WD * w_ref[...])
        nm_ref[...] = nm
        nv_ref[...] = nv

    blk = pl.BlockSpec((tr, cols), lambda i: (i, 0))
    outs = _pcall(
        body, name=name, grid=(rows // tr,), in_specs=[blk] * 4, out_specs=[blk] * 3,
        out_shape=[_sds((rows, cols), F32)] * 3, compiler_params=_cparams(("parallel",)),
    )(*(a.reshape(rows, cols) for a in (w, g, m, v)))
    return tuple(o.reshape(shape) for o in outs)


def _row_tile(rows, cap=512):
    for tr in range(cap, 15, -16):
        if rows % tr == 0:
            return tr
    return rows


def _add_pair_call(a, b, name):
    n, rows, cols = a.shape
    tr = _row_tile(rows)

    def body(a_ref, b_ref, s_ref, sb_ref):
        s = a_ref[...] + b_ref[...].astype(F32)
        s_ref[...] = s
        sb_ref[...] = s.astype(BF16)

    blk = pl.BlockSpec((1, tr, cols), lambda j, i: (j, i, 0))
    return _pcall(
        body, name=name, grid=(n, rows // tr), in_specs=[blk, blk], out_specs=[blk, blk],
        out_shape=[_sds(a.shape, F32), _sds(a.shape, BF16)], compiler_params=_cparams(("parallel", "parallel")),
    )(a, b)


def _add_recv_call(own, recv, name):
    rows, cols = own.shape
    tr = _row_tile(rows)

    def body(o_ref, r_ref, s_ref):
        s_ref[...] = ((o_ref[...] + r_ref[0].astype(F32)) + r_ref[1].astype(F32)) + r_ref[2].astype(F32)

    return _pcall(
        body, name=name, grid=(rows // tr,),
        in_specs=[pl.BlockSpec((tr, cols), lambda i: (i, 0)), pl.BlockSpec((3, tr, cols), lambda i: (0, i, 0))],
        out_specs=pl.BlockSpec((tr, cols), lambda i: (i, 0)), out_shape=_sds(own.shape, F32),
        compiler_params=_cparams(("parallel",)),
    )(own, recv)


def _axes():
    return lax.axis_index("x"), lax.axis_index("y"), lax.axis_index("c")


def _chip_peers(x, y):
    return [(x, 1 - y), (1 - x, y), (1 - x, 1 - y)]


_ANY = pl.BlockSpec(memory_space=pl.ANY)
_VMEM = pl.BlockSpec(memory_space=pltpu.VMEM)


def _remote(src, dst, send_sems, recv_sems, k, dev):
    return pltpu.make_async_remote_copy(src_ref=src, dst_ref=dst, send_sem=send_sems.at[k], recv_sem=recv_sems.at[k],
                                        device_id=dev, device_id_type=MESH)


def _gather_weights_call(packed):
    rows, cols = packed.shape
    half = rows // 2

    def body(x_ref, o_ref, send_sems, recv_sems, local_sem):
        x, y, c = _axes()
        chip, sib, peers = 2 * x + y, (x, y, 1 - c), _chip_peers(x, y)

        def part(cc):
            return pl.ds(pl.multiple_of(cc * half, 16), half)

        def landed(k, cc):
            return o_ref.at[2 * peers[k][0] + peers[k][1], part(cc)]

        mine = pltpu.make_async_copy(x_ref, o_ref.at[chip], local_sem)
        mine.start()
        first = [_remote(x_ref.at[part(c)], o_ref.at[chip, part(c)], send_sems, recv_sems, k, (*peers[k], c))
                 for k in range(3)]
        for cp in first:
            cp.start()
        passed = [_remote(landed(k, c), landed(k, c), send_sems, recv_sems, 3 + k, sib) for k in range(3)]
        for k in range(3):
            _remote(landed(k, c), landed(k, c), send_sems, recv_sems, k, (*peers[k], c)).wait_recv()
            passed[k].start()
        for k in range(3):
            _remote(landed(k, 1 - c), landed(k, 1 - c), send_sems, recv_sems, 3 + k, sib).wait_recv()
        for cp in first + passed:
            cp.wait_send()
        mine.wait()

    return _pcall(
        body, name="gather_weights", in_specs=[_ANY], out_specs=_ANY, out_shape=_sds((4, rows, cols), packed.dtype),
        scratch_shapes=[pltpu.SemaphoreType.DMA((6,)), pltpu.SemaphoreType.DMA((6,)), pltpu.SemaphoreType.DMA],
    )(packed)


def _gather_small_call(x):
    rows, cols = x.shape

    def body(x_ref, o_ref, send_sems, recv_sems):
        xi, yi, c = _axes()
        chip, peers = 2 * xi + yi, _chip_peers(xi, yi)
        o_ref[chip] = x_ref[...]
        sends = [_remote(x_ref, o_ref.at[chip], send_sems, recv_sems, k, (*peers[k], c)) for k in range(3)]
        for cp in sends:
            cp.start()
        for k in range(3):
            _remote(x_ref, o_ref.at[2 * peers[k][0] + peers[k][1]], send_sems, recv_sems, k, (*peers[k], c)).wait_recv()
        for cp in sends:
            cp.wait_send()

    return _pcall(
        body, name="gather_small", in_specs=[_VMEM], out_specs=_VMEM, out_shape=_sds((4, rows, cols), x.dtype),
        scratch_shapes=[pltpu.SemaphoreType.DMA((3,)), pltpu.SemaphoreType.DMA((3,))],
    )(x)


def _swap_sibling_call(v, name):
    def body(v_ref, o_ref, send_sems, recv_sems):
        x, y, c = _axes()
        cp = _remote(v_ref, o_ref, send_sems, recv_sems, 0, (x, y, 1 - c))
        cp.start()
        cp.wait()

    return _pcall(
        body, name=name, in_specs=[_ANY], out_specs=_ANY, out_shape=_sds(v.shape, v.dtype),
        scratch_shapes=[pltpu.SemaphoreType.DMA((1,)), pltpu.SemaphoreType.DMA((1,))],
    )(v)


def _send_chips_call(v):
    _, rows, cols = v.shape

    def body(v_ref, o_ref, send_sems, recv_sems):
        x, y, c = _axes()
        peers = _chip_peers(x, y)
        cps = [_remote(v_ref.at[2 * peers[k][0] + peers[k][1]], o_ref.at[k], send_sems, recv_sems, k, (*peers[k], c))
               for k in range(3)]
        for cp in cps:
            cp.start()
        for cp in cps:
            cp.wait()

    return _pcall(
        body, name="scatter_partials", in_specs=[_ANY], out_specs=_ANY, out_shape=_sds((3, rows, cols), v.dtype),
        scratch_shapes=[pltpu.SemaphoreType.DMA((3,)), pltpu.SemaphoreType.DMA((3,))],
    )(v)


def _allreduce_small_call(v):
    rows, cols = v.shape

    def body(v_ref, o_ref, buf, send_sems, recv_sems):
        x, y, c = _axes()
        me = 4 * x + 2 * y + c
        buf[0] = v_ref[...]
        cps = []
        for m in range(1, 8):
            dev = (1 - x if m & 4 else x, 1 - y if m & 2 else y, 1 - c if m & 1 else c)
            cps.append(_remote(v_ref, buf.at[m], send_sems, recv_sems, m - 1, dev))
        for cp in cps:
            cp.start()
        for cp in cps:
            cp.wait()
        acc = buf[me]
        for d in range(1, 8):
            acc = acc + buf[lax.bitwise_xor(me, d)]
        o_ref[...] = acc

    return _pcall(
        body, name="allreduce_small", in_specs=[_VMEM], out_specs=_VMEM, out_shape=_sds(v.shape, F32),
        scratch_shapes=[pltpu.VMEM((8, rows, cols), F32), pltpu.SemaphoreType.DMA((7,)), pltpu.SemaphoreType.DMA((7,))],
        compiler_params=_cparams(),
    )(v)


def _pack_rows(arrays, dtype, total_rows=None):
    parts = []
    for a in arrays:
        flat = a.astype(dtype).reshape(-1)
        parts.append(jnp.pad(flat, (0, -flat.shape[0] % D_MODEL)).reshape(-1, D_MODEL))
    out = jnp.concatenate(parts, axis=0)
    total_rows = total_rows or out.shape[0] + (-out.shape[0] % 8)
    return jnp.pad(out, ((0, total_rows - out.shape[0]), (0, 0)))


def _unpack_rows(packed, shapes):
    out, r = [], 0
    for s in shapes:
        n = math.prod(s)
        nr = -(-n // D_MODEL)
        out.append(packed[r:r + nr].reshape(-1)[:n].reshape(s))
        r += nr
    return out


def _to_r(w):
    pad = jnp.zeros(w.shape[:-1] + (N_R - N_ORIG,), w.dtype)
    return jnp.concatenate([w[..., :BA_ORIG], w[..., BA_ORIG + 2 * HEADS:], w[..., BA_ORIG:BA_ORIG + 2 * HEADS], pad],
                           axis=-1)


def _from_r(w):
    return jnp.concatenate([w[..., :BA_ORIG], w[..., BA_OFF:BA_OFF + 2 * HEADS], w[..., BA_ORIG:BA_OFF]], axis=-1)


_BIG = ("w_in", "a_proj", "b_proj", "c_proj", "w_out")
_BIG_AXIS = {"w_in": 2, "a_proj": 2, "b_proj": 1, "c_proj": 2, "w_out": 1}
_ORDER = ("norm_g", "w_in", "a_dw", "a_dw_b", "a_ln_g", "a_ln_b", "a_proj", "b_conv", "b_a_log", "b_dt_bias",
          "b_onorm_g", "b_proj", "c_ln_g", "c_ln_b", "c_ws", "c_bs", "c_proj", "w_out", "final_g")


def _local_step(x, tgt, w):
    bsz, t, _ = x.shape
    m = bsz * t
    depth = w["norm_g"].shape[0]
    row = lambda v: v.reshape(1, -1)
    w_r = _to_r(w["w_in"])
    saved, xl = [], x
    for l in range(depth):
        n = f"l{l}_"
        par = dict(
            adw=jnp.pad(w["a_dw"][l], ((0, 1), (0, 0))), adb=row(w["a_dw_b"][l]), alg=row(w["a_ln_g"][l]),
            alb=row(w["a_ln_b"][l]), bconv=w["b_conv"][l], alog=jnp.pad(row(w["b_a_log"][l]), ((0, 0), (0, DK - HEADS))),
            dtb=jnp.pad(row(w["b_dt_bias"][l]), ((0, 0), (0, DK - HEADS))), og=row(w["b_onorm_g"][l]),
            clg=row(w["c_ln_g"][l]), clb=row(w["c_ln_b"][l]), cws=w["c_ws"][l],
            cbias=jnp.repeat(w["c_bs"][l].T, SG_C, axis=1), ng=row(w["norm_g"][l]),
            wa=w["a_proj"][l], wb=w["b_proj"][l], wc=w["c_proj"][l], wo=w["w_out"][l], wr=w_r[l])
        h = _rms_fwd_call(xl.reshape(m, D_MODEL), par["ng"], n + "norm")
        proj = _matmul(h, par["wr"], F32, 1024, 1152, 1024, n + "in_proj").reshape(bsz, t, N_R)
        ya = _a_fwd_call(proj, par["adw"], par["adb"], par["alg"], par["alb"], n + "conv_fwd")
        yb, o_raw, states = _dn_fwd_call(proj, par["bconv"], par["alog"], par["dtb"], par["og"], n + "delta_fwd")
        yc = _c_fwd_call(proj, par["clg"], par["clb"], par["cws"], par["cbias"], n + "gmlp_fwd")
        x_next = _m_fwd_call(ya, yb, yc, proj, xl, par["wa"], par["wb"], par["wc"], par["wo"], n + "merge_fwd")
        saved.append((par, xl, h, proj, ya, yb, yc, o_raw, states))
        xl = x_next
    dout, dfg, loss = _loss_call(xl.reshape(m, D_MODEL), tgt.reshape(m, D_MODEL), row(w["final_g"]))
    dout = dout.reshape(bsz, t, D_MODEL)
    g = {k: [None] * depth for k in _ORDER if k != "final_g"}
    for l in reversed(range(depth)):
        n = f"l{l}_"
        par, xl, h, proj, ya, yb, yc, o_raw, states = saved[l]
        dya, dyb, dyc, dgate, merged, dps = _m_bwd_call(dout, ya, yb, yc, proj, par["wa"], par["wb"], par["wc"],
                                                        par["wo"], n + "merge_bwd")
        g["a_proj"][l], g["b_proj"][l], g["c_proj"][l], g["w_out"][l] = _m_wgrad_call(ya, yb, yc, merged, dps, dout,
                                                                                      n + "merge_wgrad")
        da, dadw, dasm = _a_bwd_call(dya, proj, par["adw"], par["adb"], par["alg"], par["alb"], n + "conv_bwd")
        dq, dk, dv, dz, dba, dbconv, dbsm = _dn_bwd_call(dyb, o_raw, states, proj, par["bconv"], par["alog"], par["dtb"],
                                                         par["og"], n + "delta_bwd")
        dc, dcws, dcbs, dcsm = _c_bwd_call(dyc, proj, par["clg"], par["clb"], par["cws"], par["cbias"], n + "gmlp_bwd")
        dproj = jnp.concatenate([da, dq, dk, dv, dz, dc, dgate, dba.astype(BF16)], axis=-1).reshape(m, N_R)
        dh = _matmul(dproj, par["wr"].T, F32, 1024, 1024, 1152, n + "in_proj_dx")
        g["w_in"][l] = _from_r(_matmul(h.T, dproj, F32, 1024, 1152, 1024, n + "in_proj_dw"))
        dx, dng = _rms_bwd_call(xl.reshape(m, D_MODEL), dh, par["ng"], dout.reshape(m, D_MODEL), n + "norm_bwd")
        dout = dx.reshape(bsz, t, D_MODEL)
        g["norm_g"][l] = dng[0]
        g["a_dw"][l], g["a_dw_b"][l], g["a_ln_g"][l], g["a_ln_b"][l] = dadw[:CONV_K], dasm[0], dasm[1], dasm[2]
        g["b_conv"][l], g["b_a_log"][l], g["b_dt_bias"][l] = dbconv[:SHORT_K], dbsm[0, :HEADS], dbsm[1, :HEADS]
        g["b_onorm_g"][l] = dbsm[2]
        g["c_ln_g"][l], g["c_ln_b"][l], g["c_ws"][l] = dcsm[0], dcsm[1], dcws
        g["c_bs"][l] = dcbs.reshape(SG_C, SG_G, SG_C).sum(-1).T
    grads = {k: jnp.stack(v) for k, v in g.items()}
    grads["final_g"] = dfg[0]
    return loss[0, 0], dout, grads


def kernel(x, norm_g, w_in, a_dw, a_dw_b, a_ln_g, a_ln_b, a_proj, b_conv, b_a_log, b_dt_bias, b_onorm_g, b_proj, c_ln_g, c_ln_b, c_ws, c_bs, c_proj, w_out, final_g, loss_target, m_norm_g, m_w_in, m_a_dw, m_a_dw_b, m_a_ln_g, m_a_ln_b, m_a_proj, m_b_conv, m_b_a_log, m_b_dt_bias, m_b_onorm_g, m_b_proj, m_c_ln_g, m_c_ln_b, m_c_ws, m_c_bs, m_c_proj, m_w_out, m_final_g, v_norm_g, v_w_in, v_a_dw, v_a_dw_b, v_a_ln_g, v_a_ln_b, v_a_proj, v_b_conv, v_b_a_log, v_b_dt_bias, v_b_onorm_g, v_b_proj, v_c_ln_g, v_c_ln_b, v_c_ws, v_c_bs, v_c_proj, v_w_out, v_final_g):
    given = dict(locals())
    ws = {k: given[k] for k in _ORDER}
    xi, yi, ci = _axes()
    chip = 2 * xi + yi

    shard_shapes = [ws[k].shape for k in _BIG]
    gathered = _gather_weights_call(_pack_rows([ws[k] for k in _BIG], BF16, PACK_ROWS))
    full = dict(ws)
    per_chip = [_unpack_rows(gathered[j], shard_shapes) for j in range(4)]
    for i, k in enumerate(_BIG):
        full[k] = jnp.concatenate([per_chip[j][i] for j in range(4)], axis=_BIG_AXIS[k])
    conv_shapes = [a_dw.shape, b_conv.shape]
    small = _gather_small_call(_pack_rows([a_dw, b_conv], F32).reshape(-1, DK))
    per_chip = [_unpack_rows(small[j].reshape(-1, D_MODEL), conv_shapes) for j in range(4)]
    full["a_dw"] = jnp.concatenate([p[0] for p in per_chip], axis=2)
    full["b_conv"] = jnp.concatenate([p[1] for p in per_chip], axis=2)

    loss, grad_x, grads = _local_step(x, loss_target, full)

    def shard_of(a, k, j):
        n = a.shape[_BIG_AXIS[k]] // 4
        return lax.slice_in_dim(a, j * n, (j + 1) * n, axis=_BIG_AXIS[k])

    by_chip = jnp.stack([_pack_rows([shard_of(grads[k], k, j) for k in _BIG], F32, PACK_ROWS) for j in range(4)])
    half = PACK_ROWS // 2
    mine = lax.dynamic_slice_in_dim(by_chip, ci * half, half, axis=1)
    theirs = lax.dynamic_slice_in_dim(by_chip, (1 - ci) * half, half, axis=1).astype(BF16)
    part, part_bf = _add_pair_call(mine, _swap_sibling_call(theirs, "pair_partials"), "pair_sum")
    recv = _send_chips_call(part_bf)
    fin = _add_recv_call(lax.dynamic_index_in_dim(part, chip, 0, keepdims=False), recv, "chip_sum")
    other = _swap_sibling_call(fin, "pair_result")
    reduced = lax.dynamic_update_slice_in_dim(jnp.zeros((PACK_ROWS, D_MODEL), F32), fin, ci * half, axis=0)
    reduced = lax.dynamic_update_slice_in_dim(reduced, other, (1 - ci) * half, axis=0)
    out_g = dict(zip(_BIG, _unpack_rows(reduced, shard_shapes)))

    rest = [k for k in _ORDER if k not in _BIG]
    rest_shapes = [grads[k].shape for k in rest] + [(1,)]
    summed = _unpack_rows(_allreduce_small_call(_pack_rows([grads[k] for k in rest] + [loss.reshape(1)], F32)),
                          rest_shapes)
    out_g.update(zip(rest, summed[:-1]))
    out_g["a_dw"] = lax.dynamic_slice_in_dim(out_g["a_dw"], chip * a_dw.shape[2], a_dw.shape[2], axis=2)
    out_g["b_conv"] = lax.dynamic_slice_in_dim(out_g["b_conv"], chip * b_conv.shape[2], b_conv.shape[2], axis=2)

    upd = {k: _adamw_call(ws[k], out_g[k], given["m_" + k], given["v_" + k], "adamw_" + k) for k in _ORDER}
    return (summed[-1][0], grad_x, *[out_g[k] for k in _ORDER], *[upd[k][0] for k in _ORDER],
            *[upd[k][1] for k in _ORDER], *[upd[k][2] for k in _ORDER])
```

```python
import functools
import math

import jax
import jax.numpy as jnp
from jax import lax
from jax.experimental import pallas as pl
from jax.experimental.pallas import tpu as pltpu

F32 = jnp.float32
BF16 = jnp.bfloat16
MESH = pl.DeviceIdType.MESH

D_MODEL = 1024
CONV_W = 512
CONV_K = 31
HEADS = 8
DK = 128
SHORT_K = 4
CHUNK = 64
SG_W = 512
SG_G = 4
SG_C = 128
EPS = 1e-6
N_ORIG = 10256
BA_ORIG = 5632
A_ORIG_W = 3 * CONV_W
B_W = 4 * D_MODEL
B_OFF, A_OFF, C_OFF, G_OFF, BA_OFF, N_R = 0, 4096, 5632, 7168, 10240, 10368
TB = 256
HALO_A = 32
HALO_B = 8
VMEM_LIMIT = 56 * 1024 * 1024
ADAM_LR, ADAM_B1, ADAM_B2, ADAM_EPS, ADAM_WD, ADAM_STEP = 0.001, 0.9, 0.999, 1e-08, 0.01, 10
PACK_ROWS = 6688


def _pcall(body, **kw):
    return pl.pallas_call(body, **kw)


def _cparams(sem=None):
    kw = dict(vmem_limit_bytes=VMEM_LIMIT)
    if sem is not None:
        kw["dimension_semantics"] = sem
    return pltpu.CompilerParams(**kw)


def _sig(x):
    return jax.nn.sigmoid(x)


def _silu(x):
    return x * _sig(x)


def _dsilu(x):
    s = _sig(x)
    return s * (1.0 + x * (1.0 - s))


_GELU_C = math.sqrt(2.0 / math.pi)


def _gelu(x):
    return 0.5 * x * (1.0 + jnp.tanh(_GELU_C * (x + 0.044715 * x * x * x)))


def _dgelu(x):
    t = jnp.tanh(_GELU_C * (x + 0.044715 * x * x * x))
    return 0.5 * (1.0 + t) + 0.5 * x * (1.0 - t * t) * _GELU_C * (1.0 + 3 * 0.044715 * x * x)


def _softplus(x):
    return jnp.maximum(x, 0.0) + jnp.log1p(jnp.exp(-jnp.abs(x)))


def _dot(a, b, dims):
    return lax.dot_general(a.astype(BF16), b.astype(BF16), (dims, ((), ())), preferred_element_type=F32)


def _nn(a, b):
    return _dot(a, b, ((1,), (0,)))


def _nt(a, b):
    return _dot(a, b, ((1,), (1,)))


def _tn(a, b):
    return _dot(a, b, ((0,), (0,)))


def _mean(x):
    return jnp.mean(x, axis=-1, keepdims=True)


def _sum0(x):
    return jnp.sum(x, axis=0, keepdims=True)


def _sum1(x):
    return jnp.sum(x, axis=1, keepdims=True)


def _ln_fwd(x, g, b):
    xc = x - _mean(x)
    r = lax.rsqrt(_mean(xc * xc) + EPS)
    xh = xc * r
    return xh * g + b, xh, r


def _ln_bwd(dy, xh, r, g):
    dxh = dy * g
    return r * (dxh - _mean(dxh) - xh * _mean(dxh * xh)), _sum0(dy * xh), _sum0(dy)


def _rms_bwd(dxh, xh, r):
    return r * (dxh - xh * _mean(dxh * xh))


def _lane(x, idx):
    lanes = lax.broadcasted_iota(jnp.int32, x.shape, 1)
    return _sum1(jnp.where(lanes == idx, x, 0.0))


def _rows8(*rows):
    n = rows[0].shape[1]
    return jnp.concatenate(list(rows) + [jnp.zeros((8 - len(rows), n), F32)], axis=0)


def _a_fwd(val, glu, az, val_h, glu_h, w, bias, g, b):
    tb = val.shape[0]
    a_ext = jnp.concatenate([val_h * _sig(glu_h), val * _sig(glu)], axis=0)
    c = a_ext[2:2 + tb] * w[0:1]
    for j in range(1, CONV_K):
        c = c + a_ext[2 + j:2 + j + tb] * w[j:j + 1]
    ln, xh, r = _ln_fwd(c + bias, g, b)
    return _silu(ln) * _silu(az), (a_ext, ln, xh, r)


def _a_bwd(dy, val, glu, az, val_h, glu_h, w, bias, g, b, dc_next):
    tb = val.shape[0]
    _, (a_ext, ln, xh, r) = _a_fwd(val, glu, az, val_h, glu_h, w, bias, g, b)
    dln = dy * _silu(az) * _dsilu(ln)
    daz = dy * _silu(ln) * _dsilu(az)
    dc, dg, db = _ln_bwd(dln, xh, r, g)
    dc_ext = jnp.concatenate([dc, dc_next], axis=0)
    da = dc_ext[30:30 + tb] * w[0:1]
    dw_rows = [_sum0(dc * a_ext[2:2 + tb])]
    for j in range(1, CONV_K):
        da = da + dc_ext[30 - j:30 - j + tb] * w[j:j + 1]
        dw_rows.append(_sum0(dc * a_ext[2 + j:2 + j + tb]))
    dw = jnp.concatenate(dw_rows + [jnp.zeros((1, CONV_W), F32)], axis=0)
    sg = _sig(glu)
    return da * sg, da * val * sg * (1.0 - sg), daz, dw, _sum0(dc), dg, db, dc[:HALO_A]


def _tril(ws):
    ii = lax.broadcasted_iota(jnp.int32, (SG_C, SG_C), 0)
    jj = lax.broadcasted_iota(jnp.int32, (SG_C, SG_C), 1)
    return [jnp.where(jj <= ii, ws[gi], 0.0) for gi in range(SG_G)], jj <= ii


def _c_mix(wt, vs, bias_full):
    tb = vs.shape[0]
    rows = []
    for n in range(tb // SG_C):
        blks = [_nn(wt[gi], vs[n * SG_C:(n + 1) * SG_C, gi * SG_C:(gi + 1) * SG_C]) for gi in range(SG_G)]
        rows.append(jnp.concatenate(blks, axis=1) + bias_full)
    return jnp.concatenate(rows, axis=0)


def _c_fwd(cu, cv, cz, g, b, ws, bias_full):
    wt, _ = _tril(ws)
    vs, xh, r = _ln_fwd(_gelu(cv), g, b)
    mixed = _c_mix(wt, vs, bias_full)
    return _gelu(cu) * mixed * _silu(cz), (wt, vs, xh, r, mixed)


def _c_bwd(dy, cu, cv, cz, g, b, ws, bias_full):
    tb = cu.shape[0]
    _, (wt, vs, xh, r, mixed) = _c_fwd(cu, cv, cz, g, b, ws, bias_full)
    _, low = _tril(ws)
    u, sz = _gelu(cu), _silu(cz)
    dcu = dy * mixed * sz * _dgelu(cu)
    dcz = dy * u * mixed * _dsilu(cz)
    dmixed = dy * u * sz
    dbs = jnp.zeros((SG_C, SG_W), F32)
    dws = [jnp.zeros((SG_C, SG_C), F32) for _ in range(SG_G)]
    rows = []
    for n in range(tb // SG_C):
        dm_n = dmixed[n * SG_C:(n + 1) * SG_C]
        dbs = dbs + dm_n
        blks = []
        for gi in range(SG_G):
            dm = dm_n[:, gi * SG_C:(gi + 1) * SG_C]
            dws[gi] = dws[gi] + _nt(dm, vs[n * SG_C:(n + 1) * SG_C, gi * SG_C:(gi + 1) * SG_C])
            blks.append(_tn(wt[gi], dm))
        rows.append(jnp.concatenate(blks, axis=1))
    dvs = jnp.concatenate(rows, axis=0)
    dgv, dg, db = _ln_bwd(dvs, xh, r, g)
    dws = [jnp.where(low, d, 0.0) for d in dws]
    return dcu, dgv * _dgelu(cv), dcz, dws, dbs, dg, db


def _m_fwd(ya, yb, yc, g0, g1, g2, wa, wb, wc):
    pa, pb, pc = _nn(ya, wa), _nn(yb, wb), _nn(yc, wc)
    s0, s1, s2 = _sig(g0), _sig(g1), _sig(g2)
    return s0 * pa + s1 * pb + s2 * pc, (pa, pb, pc, s0, s1, s2)


def _m_bwd(dout, ya, yb, yc, g0, g1, g2, wa, wb, wc, wo):
    merged, (pa, pb, pc, s0, s1, s2) = _m_fwd(ya, yb, yc, g0, g1, g2, wa, wb, wc)
    dm = _nt(dout, wo)
    dpa, dpb, dpc = dm * s0, dm * s1, dm * s2
    dgs = (dm * pa * s0 * (1.0 - s0), dm * pb * s1 * (1.0 - s1), dm * pc * s2 * (1.0 - s2))
    return (_nt(dpa, wa), _nt(dpb, wb), _nt(dpc, wc)), dgs, merged, (dpa, dpb, dpc)


def _chunk_masks(c):
    ii = lax.broadcasted_iota(jnp.int32, (c, c), 0)
    jj = lax.broadcasted_iota(jnp.int32, (c, c), 1)
    return ii, jj


def _dn_decay(items):
    ii, jj = _chunk_masks(CHUNK)
    incl, strict, eye = jj <= ii, jj < ii, ii == jj
    for d in items:
        g = d["g"]
        grow = _sum0(jnp.where(eye, g, 0.0))
        gc_col = _sum1(jnp.where(incl, grow, 0.0))
        gc_row = _sum0(jnp.where(ii <= jj, g, 0.0))
        gam_i = jnp.where(incl, jnp.exp(jnp.where(incl, gc_col - gc_row, 0.0)), 0.0)
        gl = _sum0(g)
        egc = jnp.exp(gc_col)
        d.update(gam_i=gam_i, gam_s=jnp.where(strict, gam_i, 0.0), egc=egc, ekd=jnp.exp(gl - gc_col), dl=jnp.exp(gl),
                 gdiff=gc_col - gc_row, qd=d["q"] * egc, rhs_w=d["k"] * (d["b"] * egc))
        d["kd"] = d["k"] * d["ekd"]
    return ii, jj, strict, eye


def _dn_solve(items, ii, jj, eye):
    off = ((ii >> 1) == (jj >> 1)) & ((ii & 1) != 0) & ((jj & 1) == 0)
    for d in items:
        a1 = jnp.where(off, d["a"], 0.0)
        d["t"] = jnp.where(eye, 1.0, 0.0) - a1
        d["m"] = d["a"] - _nn(a1, d["a"])
    b, sh = 2, 2
    while b < CHUNK:
        off = ((ii >> sh) == (jj >> sh)) & ((ii & b) != 0) & ((jj & b) == 0)
        for d in items:
            mo = jnp.where(off, d["m"], 0.0)
            if 2 * b < CHUNK:
                r = _nn(mo, jnp.concatenate([d["m"], d["t"]], axis=1))
                d["m"], d["t"] = d["m"] - r[:, :CHUNK], d["t"] - r[:, CHUNK:]
            else:
                d["t"] = d["t"] - _nn(mo, d["t"])
        b, sh = 2 * b, sh + 1


def _dn_fwd_chunk(items, ss):
    ii, jj, strict, eye = _dn_decay(items)
    for d in items:
        kq = _nt(jnp.concatenate([d["k"], d["q"]], axis=0), d["k"])
        d["a"] = d["b"] * kq[:CHUNK] * d["gam_s"]
        d["qk"] = kq[CHUNK:] * d["gam_i"]
    _dn_solve(items, ii, jj, eye)
    for d in items:
        d["uw"] = _nn(d["t"], jnp.concatenate([d["v"] * d["b"], d["rhs_w"]], axis=1))
    ws = [_nn(jnp.concatenate([d["uw"][:, DK:], d["qd"]], axis=0), s) for d, s in zip(items, ss)]
    vnew = [d["uw"][:, :DK] - w[:CHUNK] for d, w in zip(items, ws)]
    outs = [w[CHUNK:] + _nn(d["qk"], vn) for d, w, vn in zip(items, ws, vnew)]
    ss = [d["dl"] * s + _tn(d["kd"], vn) for d, s, vn in zip(items, ss, vnew)]
    return outs, ss, vnew


def _dn_bwd_chunk(items, ss, dss):
    ii, jj, strict, eye = _dn_decay(items)
    for d in items:
        kq = jnp.concatenate([d["k"], d["q"]], axis=0)
        gram = _nt(kq, kq)
        d["kk"] = gram[:CHUNK, :CHUNK]
        d["a"] = d["b"] * d["kk"] * d["gam_s"]
        d["qk"] = gram[CHUNK:, :CHUNK] * d["gam_i"]
        d["qkt"] = gram[:CHUNK, CHUNK:] * jnp.where(ii <= jj, jnp.exp(jnp.where(ii <= jj, -d["gdiff"], 0.0)), 0.0)
    for d, s, ds2 in zip(items, ss, dss):
        d["dvnew"] = _nn(jnp.concatenate([d["qkt"], d["kd"]], axis=1), jnp.concatenate([d["do"], ds2], axis=0))
        r = _nt(d["do"], jnp.concatenate([d["vnew"], s], axis=0))
        d["dqk"], d["dqd"] = r[:, :CHUNK], r[:, CHUNK:]
        d["dkd"] = _nt(d["vnew"], ds2)
    new_dss = []
    for d, s, ds2 in zip(items, ss, dss):
        d["dw"] = -_nt(d["dvnew"], s)
        new_dss.append(_tn(jnp.concatenate([d["qd"], -d["uw"][:, DK:]], axis=0),
                           jnp.concatenate([d["do"], d["dvnew"]], axis=0)) + d["dl"] * ds2)
        d["ddl"] = _sum0(_sum1(s * ds2))
    for d in items:
        d["drhs"] = _tn(d["t"], jnp.concatenate([d["dvnew"], d["dw"]], axis=1))
    for d in items:
        d["da"] = jnp.where(strict, -_nt(d["drhs"], d["uw"]), 0.0)
    outs = []
    for d in items:
        q, k, v, b, egc = d["q"], d["k"], d["v"], d["b"], d["egc"]
        drhs_u, drhs_w, da = d["drhs"][:, :DK], d["drhs"][:, DK:], d["da"]
        dbeta = _sum1(da * d["kk"] * d["gam_s"]) + _sum1(drhs_u * v) + _sum1(drhs_w * k) * egc
        dkk = da * b * d["gam_s"]
        e = da * d["a"] + d["dqk"] * d["qk"]
        s_kd = _sum1(d["dkd"] * d["kd"])
        dgc_col = _sum1(e) + _sum1(drhs_w * d["rhs_w"]) + _sum1(d["dqd"] * d["qd"]) - s_kd
        dgc_row = _sum0(jnp.where(eye, dgc_col, 0.0)) - _sum0(e)
        dg = _sum1(jnp.where(jj >= ii, dgc_row, 0.0)) + (_sum0(s_kd) + d["ddl"] * d["dl"])
        x = jnp.concatenate([d["dqk"] * d["gam_i"], dkk], axis=0)
        x1 = _nn(x, k)
        x2 = _tn(x, jnp.concatenate([q, k], axis=0))
        dq = x1[:CHUNK] + d["dqd"] * egc
        dk = x2 + x1[CHUNK:] + drhs_w * (b * egc) + d["dkd"] * d["ekd"]
        outs.append((dq, dk, drhs_u * b, dbeta, dg))
    return outs, new_dss


def _short_conv(raw, halo, w):
    tb = raw.shape[0]
    ext = jnp.concatenate([halo, raw], axis=0)
    out = ext[5:5 + tb] * w[0:1]
    for j in range(1, SHORT_K):
        out = out + ext[5 + j:5 + j + tb] * w[j:j + 1]
    return out, ext


def _dn_prep(raw, halo, w, ba, alog, dtb):
    pre, ext = _short_conv(raw, halo, w)
    act = _silu(pre)
    items = []
    for h in range(HEADS):
        qc, kc, vc = (act[:, n * D_MODEL + h * DK:n * D_MODEL + (h + 1) * DK] for n in range(3))
        nq = lax.rsqrt(_sum1(qc * qc) + EPS)
        nk = lax.rsqrt(_sum1(kc * kc) + EPS)
        xg = ba[:, HEADS + h:HEADS + h + 1] + dtb[:, h:h + 1]
        ea = jnp.exp(alog[:, h:h + 1])
        items.append(dict(q=qc * nq * (DK ** -0.5), k=kc * nk, v=vc, b=_sig(ba[:, h:h + 1]), g=-ea * _softplus(xg),
                          qc=qc, kc=kc, nq=nq, nk=nk, xg=xg, ea=ea))
    return pre, ext, items


def _dn_out(o, z, og):
    r = lax.rsqrt(_mean(o * o) + EPS)
    xh = o * r
    return xh * og * _silu(z), xh, r


def _sds(shape, dtype):
    return jax.ShapeDtypeStruct(tuple(shape), dtype)


def _matmul(a, b, out_dtype, tm, tn, tk, name):
    m, kd = a.shape
    n = b.shape[1]
    tm, tn, tk = min(tm, m), min(tn, n), min(tk, kd)
    nk = kd // tk

    def body(a_ref, b_ref, o_ref, acc):
        @pl.when(pl.program_id(2) == 0)
        def _():
            acc[...] = jnp.zeros_like(acc)

        acc[...] += jnp.dot(a_ref[...], b_ref[...], preferred_element_type=F32)

        @pl.when(pl.program_id(2) == nk - 1)
        def _():
            o_ref[...] = acc[...].astype(o_ref.dtype)

    return _pcall(
        body, name=name, grid=(m // tm, n // tn, nk),
        in_specs=[pl.BlockSpec((tm, tk), lambda i, j, k: (i, k)), pl.BlockSpec((tk, tn), lambda i, j, k: (k, j))],
        out_specs=pl.BlockSpec((tm, tn), lambda i, j, k: (i, j)), out_shape=_sds((m, n), out_dtype),
        scratch_shapes=[pltpu.VMEM((tm, tn), F32)], compiler_params=_cparams(("parallel", "parallel", "arbitrary")),
    )(a, b)


def _rms_fwd_call(x, g, name):
    m = x.shape[0]
    tm = min(512, m)

    def body(x_ref, g_ref, o_ref):
        xv = x_ref[...]
        o_ref[...] = (xv * lax.rsqrt(_mean(xv * xv) + EPS) * g_ref[...]).astype(BF16)

    return _pcall(
        body, name=name, grid=(m // tm,),
        in_specs=[pl.BlockSpec((tm, D_MODEL), lambda i: (i, 0)), pl.BlockSpec((1, D_MODEL), lambda i: (0, 0))],
        out_specs=pl.BlockSpec((tm, D_MODEL), lambda i: (i, 0)), out_shape=_sds((m, D_MODEL), BF16),
        compiler_params=_cparams(("parallel",)),
    )(x, g)


def _rms_bwd_call(x, dh, g, dres, name):
    m = x.shape[0]
    tm = min(512, m)

    def body(x_ref, dh_ref, g_ref, dr_ref, dx_ref, dg_ref):
        @pl.when(pl.program_id(0) == 0)
        def _():
            dg_ref[...] = jnp.zeros_like(dg_ref)

        xv, dhv = x_ref[...], dh_ref[...]
        r = lax.rsqrt(_mean(xv * xv) + EPS)
        xh = xv * r
        dx_ref[...] = _rms_bwd(dhv * g_ref[...], xh, r) + dr_ref[...]
        dg_ref[...] += _rows8(_sum0(dhv * xh))

    row = pl.BlockSpec((tm, D_MODEL), lambda i: (i, 0))
    return _pcall(
        body, name=name, grid=(m // tm,),
        in_specs=[row, row, pl.BlockSpec((1, D_MODEL), lambda i: (0, 0)), row],
        out_specs=[row, pl.BlockSpec((8, D_MODEL), lambda i: (0, 0))],
        out_shape=[_sds((m, D_MODEL), F32), _sds((8, D_MODEL), F32)], compiler_params=_cparams(("arbitrary",)),
    )(x, dh, g, dres)


def _loss_call(x, tgt, g):
    m = x.shape[0]
    tm = min(512, m)

    def body(x_ref, t_ref, g_ref, dx_ref, dg_ref, l_ref):
        @pl.when(pl.program_id(0) == 0)
        def _():
            dg_ref[...] = jnp.zeros_like(dg_ref)
            l_ref[...] = jnp.zeros_like(l_ref)

        xv = x_ref[...]
        r = lax.rsqrt(_mean(xv * xv) + EPS)
        xh = xv * r
        err = xh * g_ref[...] - t_ref[...]
        dy = err * (1.0 / D_MODEL)
        dx_ref[...] = _rms_bwd(dy * g_ref[...], xh, r)
        dg_ref[...] += _rows8(_sum0(dy * xh))
        l_ref[...] += 0.5 * _sum0(_mean(err * err))

    row = pl.BlockSpec((tm, D_MODEL), lambda i: (i, 0))
    return _pcall(
        body, name="loss_head", grid=(m // tm,),
        in_specs=[row, row, pl.BlockSpec((1, D_MODEL), lambda i: (0, 0))],
        out_specs=[row, pl.BlockSpec((8, D_MODEL), lambda i: (0, 0)), pl.BlockSpec((8, 128), lambda i: (0, 0))],
        out_shape=[_sds((m, D_MODEL), F32), _sds((8, D_MODEL), F32), _sds((8, 128), F32)],
        compiler_params=_cparams(("arbitrary",)),
    )(x, tgt, g)


def _halo_idx(i, rows):
    return jnp.maximum(i * (TB // rows) - 1, 0)


def _a_specs(nt, rev):
    ti = (lambda i: nt - 1 - i) if rev else (lambda i: i)
    c0 = A_OFF // CONV_W
    tile = [pl.BlockSpec((1, TB, CONV_W), functools.partial(lambda b, i, c: (b, ti(i), c), c=c0 + c)) for c in range(3)]
    halo = [pl.BlockSpec((1, HALO_A, CONV_W),
                         functools.partial(lambda b, i, c: (b, _halo_idx(ti(i), HALO_A), c), c=c0 + c)) for c in range(2)]
    par = [pl.BlockSpec((HALO_A, CONV_W), lambda b, i: (0, 0))] + [pl.BlockSpec((1, CONV_W), lambda b, i: (0, 0))] * 3
    return tile + halo + par


def _a_fwd_call(proj, w, bias, g, b, name):
    bsz, t, _ = proj.shape
    nt = t // TB

    def body(val_ref, glu_ref, az_ref, vh_ref, gh_ref, w_ref, bias_ref, g_ref, b_ref, y_ref):
        keep = jnp.where(pl.program_id(1) > 0, 1.0, 0.0)
        y, _ = _a_fwd(val_ref[0], glu_ref[0], az_ref[0], vh_ref[0] * keep, gh_ref[0], w_ref[...], bias_ref[...],
                      g_ref[...], b_ref[...])
        y_ref[0] = y.astype(BF16)

    return _pcall(
        body, name=name, grid=(bsz, nt), in_specs=_a_specs(nt, False),
        out_specs=pl.BlockSpec((1, TB, CONV_W), lambda b, i: (b, i, 0)), out_shape=_sds((bsz, t, CONV_W), BF16),
        compiler_params=_cparams(("parallel", "parallel")),
    )(proj, proj, proj, proj, proj, w, bias, g, b)


def _a_bwd_call(dy, proj, w, bias, g, b, name):
    bsz, t, _ = proj.shape
    nt = t // TB

    def body(dy_ref, val_ref, glu_ref, az_ref, vh_ref, gh_ref, w_ref, bias_ref, g_ref, b_ref, da_ref, dw_ref, ds_ref,
             carry):
        ip = pl.program_id(1)

        @pl.when((pl.program_id(0) == 0) & (ip == 0))
        def _():
            dw_ref[...] = jnp.zeros_like(dw_ref)
            ds_ref[...] = jnp.zeros_like(ds_ref)

        @pl.when(ip == 0)
        def _():
            carry[...] = jnp.zeros_like(carry)

        keep = jnp.where(ip < nt - 1, 1.0, 0.0)
        dval, dglu, daz, dw, dbias, dg, db, head = _a_bwd(
            dy_ref[0], val_ref[0], glu_ref[0], az_ref[0], vh_ref[0] * keep, gh_ref[0], w_ref[...], bias_ref[...],
            g_ref[...], b_ref[...], carry[...])
        carry[...] = head
        da_ref[0] = jnp.concatenate([dval, dglu, daz], axis=1).astype(BF16)
        dw_ref[...] += dw
        ds_ref[...] += _rows8(dbias, dg, db)

    rtile = lambda b, i: (b, nt - 1 - i, 0)
    return _pcall(
        body, name=name, grid=(bsz, nt),
        in_specs=[pl.BlockSpec((1, TB, CONV_W), rtile)] + _a_specs(nt, True),
        out_specs=[pl.BlockSpec((1, TB, 3 * CONV_W), rtile), pl.BlockSpec((HALO_A, CONV_W), lambda b, i: (0, 0)),
                   pl.BlockSpec((8, CONV_W), lambda b, i: (0, 0))],
        out_shape=[_sds((bsz, t, 3 * CONV_W), BF16), _sds((HALO_A, CONV_W), F32), _sds((8, CONV_W), F32)],
        scratch_shapes=[pltpu.VMEM((HALO_A, CONV_W), F32)], compiler_params=_cparams(("arbitrary", "arbitrary")),
    )(dy, proj, proj, proj, proj, proj, w, bias, g, b)


def _c_specs():
    c0 = C_OFF // SG_W
    tile = [pl.BlockSpec((1, TB, SG_W), functools.partial(lambda b, i, c: (b, i, c), c=c0 + c)) for c in range(3)]
    par = [pl.BlockSpec((1, SG_W), lambda b, i: (0, 0))] * 2 + [
        pl.BlockSpec((SG_G, SG_C, SG_C), lambda b, i: (0, 0, 0)), pl.BlockSpec((SG_C, SG_W), lambda b, i: (0, 0))]
    return tile + par


def _c_fwd_call(proj, g, b, ws, bias_full, name):
    bsz, t, _ = proj.shape

    def body(cu_ref, cv_ref, cz_ref, g_ref, b_ref, ws_ref, bf_ref, y_ref):
        y, _ = _c_fwd(cu_ref[0], cv_ref[0], cz_ref[0], g_ref[...], b_ref[...], ws_ref[...], bf_ref[...])
        y_ref[0] = y.astype(BF16)

    return _pcall(
        body, name=name, grid=(bsz, t // TB), in_specs=_c_specs(),
        out_specs=pl.BlockSpec((1, TB, SG_W), lambda b, i: (b, i, 0)), out_shape=_sds((bsz, t, SG_W), BF16),
        compiler_params=_cparams(("parallel", "parallel")),
    )(proj, proj, proj, g, b, ws, bias_full)


def _c_bwd_call(dy, proj, g, b, ws, bias_full, name):
    bsz, t, _ = proj.shape

    def body(dy_ref, cu_ref, cv_ref, cz_ref, g_ref, b_ref, ws_ref, bf_ref, dc_ref, dws_ref, dbs_ref, ds_ref):
        @pl.when((pl.program_id(0) == 0) & (pl.program_id(1) == 0))
        def _():
            dws_ref[...] = jnp.zeros_like(dws_ref)
            dbs_ref[...] = jnp.zeros_like(dbs_ref)
            ds_ref[...] = jnp.zeros_like(ds_ref)

        dcu, dcv, dcz, dws, dbs, dg, db = _c_bwd(dy_ref[0], cu_ref[0], cv_ref[0], cz_ref[0], g_ref[...], b_ref[...],
                                                 ws_ref[...], bf_ref[...])
        dc_ref[0] = jnp.concatenate([dcu, dcv, dcz], axis=1).astype(BF16)
        for gi in range(SG_G):
            dws_ref[gi] += dws[gi]
        dbs_ref[...] += dbs
        ds_ref[...] += _rows8(dg, db)

    tile = lambda b, i: (b, i, 0)
    return _pcall(
        body, name=name, grid=(bsz, t // TB), in_specs=[pl.BlockSpec((1, TB, SG_W), tile)] + _c_specs(),
        out_specs=[pl.BlockSpec((1, TB, 3 * SG_W), tile), pl.BlockSpec((SG_G, SG_C, SG_C), lambda b, i: (0, 0, 0)),
                   pl.BlockSpec((SG_C, SG_W), lambda b, i: (0, 0)), pl.BlockSpec((8, SG_W), lambda b, i: (0, 0))],
        out_shape=[_sds((bsz, t, 3 * SG_W), BF16), _sds((SG_G, SG_C, SG_C), F32), _sds((SG_C, SG_W), F32),
                   _sds((8, SG_W), F32)],
        compiler_params=_cparams(("arbitrary", "arbitrary")),
    )(dy, proj, proj, proj, g, b, ws, bias_full)


def _m_specs():
    g0 = G_OFF // D_MODEL
    y = [pl.BlockSpec((1, TB, n), lambda b, i: (b, i, 0)) for n in (CONV_W, D_MODEL, SG_W)]
    gates = [pl.BlockSpec((1, TB, D_MODEL), functools.partial(lambda b, i, c: (b, i, c), c=g0 + c)) for c in range(3)]
    return y + gates


def _w_specs(*shapes):
    return [pl.BlockSpec(s, lambda b, i: (0, 0)) for s in shapes]


def _m_fwd_call(ya, yb, yc, proj, x, wa, wb, wc, wo, name):
    bsz, t, _ = x.shape

    def body(ya_ref, yb_ref, yc_ref, g0_ref, g1_ref, g2_ref, x_ref, wa_ref, wb_ref, wc_ref, wo_ref, o_ref):
        merged, _ = _m_fwd(ya_ref[0], yb_ref[0], yc_ref[0], g0_ref[0], g1_ref[0], g2_ref[0], wa_ref[...], wb_ref[...],
                           wc_ref[...])
        o_ref[0] = x_ref[0] + _nn(merged, wo_ref[...])

    tile = pl.BlockSpec((1, TB, D_MODEL), lambda b, i: (b, i, 0))
    return _pcall(
        body, name=name, grid=(bsz, t // TB),
        in_specs=_m_specs() + [tile] + _w_specs(wa.shape, wb.shape, wc.shape, wo.shape),
        out_specs=tile, out_shape=_sds(x.shape, F32), compiler_params=_cparams(("parallel", "parallel")),
    )(ya, yb, yc, proj, proj, proj, x, wa, wb, wc, wo)


def _m_bwd_call(dout, ya, yb, yc, proj, wa, wb, wc, wo, name):
    bsz, t, _ = dout.shape

    def body(do_ref, ya_ref, yb_ref, yc_ref, g0_ref, g1_ref, g2_ref, wa_ref, wb_ref, wc_ref, wo_ref, dya_ref, dyb_ref,
             dyc_ref, dg_ref, mg_ref, dp_ref):
        dys, dgs, merged, dps = _m_bwd(do_ref[0], ya_ref[0], yb_ref[0], yc_ref[0], g0_ref[0], g1_ref[0], g2_ref[0],
                                       wa_ref[...], wb_ref[...], wc_ref[...], wo_ref[...])
        dya_ref[0], dyb_ref[0], dyc_ref[0] = dys
        dg_ref[0] = jnp.concatenate(dgs, axis=1).astype(BF16)
        mg_ref[0] = merged.astype(BF16)
        dp_ref[0] = jnp.concatenate(dps, axis=1).astype(BF16)

    tile = lambda n: pl.BlockSpec((1, TB, n), lambda b, i: (b, i, 0))
    widths = (CONV_W, D_MODEL, SG_W, 3 * D_MODEL, D_MODEL, 3 * D_MODEL)
    dts = (F32, F32, F32, BF16, BF16, BF16)
    return _pcall(
        body, name=name, grid=(bsz, t // TB),
        in_specs=[tile(D_MODEL)] + _m_specs() + _w_specs(wa.shape, wb.shape, wc.shape, wo.shape),
        out_specs=[tile(n) for n in widths], out_shape=[_sds((bsz, t, n), d) for n, d in zip(widths, dts)],
        compiler_params=_cparams(("parallel", "parallel")),
    )(dout, ya, yb, yc, proj, proj, proj, wa, wb, wc, wo)


def _m_wgrad_call(ya, yb, yc, merged, dps, dout, name):
    bsz, t, _ = dout.shape

    def body(ya_ref, yb_ref, yc_ref, mg_ref, dp_ref, do_ref, dwa_ref, dwb_ref, dwc_ref, dwo_ref):
        @pl.when((pl.program_id(0) == 0) & (pl.program_id(1) == 0))
        def _():
            for r in (dwa_ref, dwb_ref, dwc_ref, dwo_ref):
                r[...] = jnp.zeros_like(r)

        dp = dp_ref[0]
        dwa_ref[...] += _tn(ya_ref[0], dp[:, :D_MODEL])
        dwb_ref[...] += _tn(yb_ref[0], dp[:, D_MODEL:2 * D_MODEL])
        dwc_ref[...] += _tn(yc_ref[0], dp[:, 2 * D_MODEL:])
        dwo_ref[...] += _tn(mg_ref[0], do_ref[0])

    tile = lambda n: pl.BlockSpec((1, TB, n), lambda b, i: (b, i, 0))
    shapes = ((CONV_W, D_MODEL), (D_MODEL, D_MODEL), (SG_W, D_MODEL), (D_MODEL, D_MODEL))
    return _pcall(
        body, name=name, grid=(bsz, t // TB),
        in_specs=[tile(CONV_W), tile(D_MODEL), tile(SG_W), tile(D_MODEL), tile(3 * D_MODEL), tile(D_MODEL)],
        out_specs=_w_specs(*shapes), out_shape=[_sds(s, F32) for s in shapes],
        compiler_params=_cparams(("arbitrary", "arbitrary")),
    )(ya, yb, yc, merged, dps, dout)


def _dn_specs(nc, rev):
    ti = (lambda i: nc - 1 - i) if rev else (lambda i: i)
    return [pl.BlockSpec((1, CHUNK, B_W), lambda b, i: (b, ti(i), 0)),
            pl.BlockSpec((1, HALO_B, 3 * D_MODEL), lambda b, i: (b, jnp.maximum(ti(i) * (CHUNK // HALO_B) - 1, 0), 0)),
            pl.BlockSpec((1, CHUNK, DK), lambda b, i: (b, ti(i), BA_OFF // DK)),
            pl.BlockSpec((SHORT_K, 3 * D_MODEL), lambda b, i: (0, 0))] + [pl.BlockSpec((1, DK), lambda b, i: (0, 0))] * 3


def _dn_saved_specs(nc, rev):
    ti = (lambda i: nc - 1 - i) if rev else (lambda i: i)
    return [pl.BlockSpec((1, HEADS, 1, DK, DK), lambda b, i: (b, 0, ti(i), 0, 0)),
            pl.BlockSpec((1, 1, CHUNK, HEADS * CHUNK), lambda b, i: (b, ti(i), 0, 0)),
            pl.BlockSpec((1, CHUNK, 2 * D_MODEL), lambda b, i: (b, ti(i), 0)),
            pl.BlockSpec((1, CHUNK, D_MODEL), lambda b, i: (b, ti(i), 0))]


def _dn_fwd_call(proj, wconv, alog, dtb, og, name):
    bsz, t, _ = proj.shape
    nc = t // CHUNK

    def body(x_ref, halo_ref, ba_ref, w_ref, al_ref, dt_ref, og_ref, y_ref, o_ref, st_ref, t_ref, uw_ref, vn_ref, s_scr):
        i = pl.program_id(1)

        @pl.when(i == 0)
        def _():
            s_scr[...] = jnp.zeros_like(s_scr)

        x = x_ref[0]
        keep = jnp.where(i > 0, 1.0, 0.0)
        _, _, items = _dn_prep(x[:, :3 * D_MODEL], halo_ref[0] * keep, w_ref[...], ba_ref[0], al_ref[...], dt_ref[...])
        ss = [s_scr[h] for h in range(HEADS)]
        for h in range(HEADS):
            st_ref[0, h, 0] = ss[h]
        outs, ss, vnew = _dn_fwd_chunk(items, ss)
        for h in range(HEADS):
            s_scr[h] = ss[h]
        og_v = og_ref[...]
        ys = [_dn_out(outs[h], x[:, 3 * D_MODEL + h * DK:3 * D_MODEL + (h + 1) * DK], og_v)[0] for h in range(HEADS)]
        o_ref[0] = jnp.concatenate(outs, axis=1)
        y_ref[0] = jnp.concatenate(ys, axis=1).astype(BF16)
        t_ref[0, 0] = jnp.concatenate([d["t"] for d in items], axis=1).astype(BF16)
        uw_ref[0] = jnp.concatenate([d["uw"] for d in items], axis=1).astype(BF16)
        vn_ref[0] = jnp.concatenate(vnew, axis=1).astype(BF16)

    row = pl.BlockSpec((1, CHUNK, D_MODEL), lambda b, i: (b, i, 0))
    return _pcall(
        body, name=name, grid=(bsz, nc), in_specs=_dn_specs(nc, False), out_specs=[row, row] + _dn_saved_specs(nc, False),
        out_shape=[_sds((bsz, t, D_MODEL), BF16), _sds((bsz, t, D_MODEL), F32), _sds((bsz, HEADS, nc, DK, DK), F32),
                   _sds((bsz, nc, CHUNK, HEADS * CHUNK), BF16), _sds((bsz, t, 2 * D_MODEL), BF16),
                   _sds((bsz, t, D_MODEL), BF16)],
        scratch_shapes=[pltpu.VMEM((HEADS, DK, DK), F32)], compiler_params=_cparams(("arbitrary", "arbitrary")),
    )(proj, proj, proj, wconv, alog, dtb, og)


def _dn_bwd_call(dy, o_raw, saved, proj, wconv, alog, dtb, og, name):
    bsz, t, _ = proj.shape
    nc = t // CHUNK

    def body(dy_ref, o_ref, st_ref, t_ref, uw_ref, vn_ref, x_ref, halo_ref, ba_ref, w_ref, al_ref, dt_ref, og_ref,
             dx_ref, dba_ref, dwc_ref, dsm_ref, ds_scr, dpre_scr):
        ip = pl.program_id(1)

        @pl.when((pl.program_id(0) == 0) & (ip == 0))
        def _():
            dwc_ref[...] = jnp.zeros_like(dwc_ref)
            dsm_ref[...] = jnp.zeros_like(dsm_ref)

        @pl.when(ip == 0)
        def _():
            ds_scr[...] = jnp.zeros_like(ds_scr)
            dpre_scr[...] = jnp.zeros_like(dpre_scr)

        x, w = x_ref[0], w_ref[...]
        keep = jnp.where(ip < nc - 1, 1.0, 0.0)
        pre, ext, items = _dn_prep(x[:, :3 * D_MODEL], halo_ref[0] * keep, w, ba_ref[0], al_ref[...], dt_ref[...])
        og_v, dyv, ov, tv, uwv, vnv = og_ref[...], dy_ref[0], o_ref[0], t_ref[0, 0], uw_ref[0], vn_ref[0]
        dzs, dog = [], jnp.zeros((1, DK), F32)
        for h, d in enumerate(items):
            hs = slice(h * DK, (h + 1) * DK)
            z = x[:, 3 * D_MODEL + h * DK:3 * D_MODEL + (h + 1) * DK]
            _, xh, r = _dn_out(ov[:, hs], z, og_v)
            dzs.append(dyv[:, hs] * xh * og_v * _dsilu(z))
            don = dyv[:, hs] * _silu(z)
            dog = dog + _sum0(don * xh)
            d.update(do=_rms_bwd(don * og_v, xh, r), t=tv[:, h * CHUNK:(h + 1) * CHUNK],
                     uw=uwv[:, 2 * h * DK:2 * (h + 1) * DK], vnew=vnv[:, hs])
        grads, dss = _dn_bwd_chunk(items, [st_ref[0, h, 0] for h in range(HEADS)], [ds_scr[h] for h in range(HEADS)])
        for h in range(HEADS):
            ds_scr[h] = dss[h]
        dqc, dkc, dvc = [], [], []
        lanes = lax.broadcasted_iota(jnp.int32, (CHUNK, DK), 1)
        lane8 = lax.broadcasted_iota(jnp.int32, (8, DK), 1)
        row8 = lax.broadcasted_iota(jnp.int32, (8, DK), 0)
        dba = jnp.zeros((CHUNK, DK), F32)
        dsm = jnp.where(row8 == 2, dog, 0.0)
        for h, (d, (dq, dk, dv, dbeta, dg)) in enumerate(zip(items, grads)):
            qc, kc, nq, nk = d["qc"], d["kc"], d["nq"], d["nk"]
            dqc.append((DK ** -0.5) * nq * (dq - qc * (nq * nq * _sum1(dq * qc))))
            dkc.append(nk * (dk - kc * (nk * nk * _sum1(dk * kc))))
            dvc.append(dv)
            dal = dg * (-d["ea"]) * _sig(d["xg"])
            dba = dba + jnp.where(lanes == h, dbeta * d["b"] * (1.0 - d["b"]), 0.0) + jnp.where(lanes == HEADS + h, dal, 0.0)
            dsm = dsm + (jnp.where((row8 == 0) & (lane8 == h), _sum0(dg * d["g"]), 0.0)
                         + jnp.where((row8 == 1) & (lane8 == h), _sum0(dal), 0.0))
        dpre = jnp.concatenate(dqc + dkc + dvc, axis=1) * _dsilu(pre)
        dext = jnp.concatenate([dpre, dpre_scr[...]], axis=0)
        draw = dext[3:3 + CHUNK] * w[0:1]
        rows = [_sum0(dpre * ext[5:5 + CHUNK])]
        for j in range(1, SHORT_K):
            draw = draw + dext[3 - j:3 - j + CHUNK] * w[j:j + 1]
            rows.append(_sum0(dpre * ext[5 + j:5 + j + CHUNK]))
        dpre_scr[...] = dpre[:HALO_B]
        dx_ref[0] = jnp.concatenate([draw] + dzs, axis=1).astype(BF16)
        dba_ref[0] = dba.astype(BF16)
        dwc_ref[...] += _rows8(*rows)
        dsm_ref[...] += dsm

    rrow = lambda n: pl.BlockSpec((1, CHUNK, n), lambda b, i: (b, nc - 1 - i, 0))
    return _pcall(
        body, name=name, grid=(bsz, nc),
        in_specs=[rrow(D_MODEL), rrow(D_MODEL)] + _dn_saved_specs(nc, True) + _dn_specs(nc, True),
        out_specs=[rrow(B_W), rrow(DK), pl.BlockSpec((8, 3 * D_MODEL), lambda b, i: (0, 0)),
                   pl.BlockSpec((8, DK), lambda b, i: (0, 0))],
        out_shape=[_sds((bsz, t, B_W), BF16), _sds((bsz, t, DK), BF16), _sds((8, 3 * D_MODEL), F32), _sds((8, DK), F32)],
        scratch_shapes=[pltpu.VMEM((HEADS, DK, DK), F32), pltpu.VMEM((HALO_B, 3 * D_MODEL), F32)],
        compiler_params=_cparams(("arbitrary", "arbitrary")),
    )(dy, o_raw, *saved, proj, proj, proj, wconv, alog, dtb, og)


def _adamw_call(w, g, m, v, name):
    shape = w.shape
    cols = shape[-1]
    rows = math.prod(shape[:-1]) if len(shape) > 1 else 1
    tr = 128 if rows % 128 == 0 else rows
    c1, c2 = 1.0 - ADAM_B1 ** ADAM_STEP, 1.0 - ADAM_B2 ** ADAM_STEP

    def body(w_ref, g_ref, m_ref, v_ref, d_ref, nm_ref, nv_ref):
        gv = g_ref[...]
        nm = ADAM_B1 * m_ref[...] + (1.0 - ADAM_B1) * gv
        nv = ADAM_B2 * v_ref[...] + (1.0 - ADAM_B2) * (gv * gv)
        d_ref[...] = -ADAM_LR * ((nm / c1) / (jnp.sqrt(nv / c2) + ADAM_EPS) + ADAM_WD * w_ref[...])
        nm_ref[...] = nm
        nv_ref[...] = nv

    blk = pl.BlockSpec((tr, cols), lambda i: (i, 0))
    outs = _pcall(
        body, name=name, grid=(rows // tr,), in_specs=[blk] * 4, out_specs=[blk] * 3,
        out_shape=[_sds((rows, cols), F32)] * 3, compiler_params=_cparams(("parallel",)),
    )(*(a.reshape(rows, cols) for a in (w, g, m, v)))
    return tuple(o.reshape(shape) for o in outs)


def _row_tile(rows, cap=512):
    for tr in range(cap, 15, -16):
        if rows % tr == 0:
            return tr
    return rows


def _add_pair_call(a, b, name):
    n, rows, cols = a.shape
    tr = _row_tile(rows)

    def body(a_ref, b_ref, s_ref, sb_ref):
        s = a_ref[...] + b_ref[...].astype(F32)
        s_ref[...] = s
        sb_ref[...] = s.astype(BF16)

    blk = pl.BlockSpec((1, tr, cols), lambda j, i: (j, i, 0))
    return _pcall(
        body, name=name, grid=(n, rows // tr), in_specs=[blk, blk], out_specs=[blk, blk],
        out_shape=[_sds(a.shape, F32), _sds(a.shape, BF16)], compiler_params=_cparams(("parallel", "parallel")),
    )(a, b)


def _add_recv_call(own, recv, name):
    rows, cols = own.shape
    tr = _row_tile(rows)

    def body(o_ref, r_ref, s_ref):
        s_ref[...] = ((o_ref[...] + r_ref[0].astype(F32)) + r_ref[1].astype(F32)) + r_ref[2].astype(F32)

    return _pcall(
        body, name=name, grid=(rows // tr,),
        in_specs=[pl.BlockSpec((tr, cols), lambda i: (i, 0)), pl.BlockSpec((3, tr, cols), lambda i: (0, i, 0))],
        out_specs=pl.BlockSpec((tr, cols), lambda i: (i, 0)), out_shape=_sds(own.shape, F32),
        compiler_params=_cparams(("parallel",)),
    )(own, recv)


def _axes():
    return lax.axis_index("x"), lax.axis_index("y"), lax.axis_index("c")


def _chip_peers(x, y):
    return [(x, 1 - y), (1 - x, y), (1 - x, 1 - y)]


_ANY = pl.BlockSpec(memory_space=pl.ANY)
_VMEM = pl.BlockSpec(memory_space=pltpu.VMEM)


def _remote(src, dst, send_sems, recv_sems, k, dev):
    return pltpu.make_async_remote_copy(src_ref=src, dst_ref=dst, send_sem=send_sems.at[k], recv_sem=recv_sems.at[k],
                                        device_id=dev, device_id_type=MESH)


def _gather_weights_call(packed):
    rows, cols = packed.shape
    half = rows // 2

    def body(x_ref, o_ref, send_sems, recv_sems, local_sem):
        x, y, c = _axes()
        chip, sib, peers = 2 * x + y, (x, y, 1 - c), _chip_peers(x, y)

        def part(cc):
            return pl.ds(pl.multiple_of(cc * half, 16), half)

        def landed(k, cc):
            return o_ref.at[2 * peers[k][0] + peers[k][1], part(cc)]

        mine = pltpu.make_async_copy(x_ref, o_ref.at[chip], local_sem)
        mine.start()
        first = [_remote(x_ref.at[part(c)], o_ref.at[chip, part(c)], send_sems, recv_sems, k, (*peers[k], c))
                 for k in range(3)]
        for cp in first:
            cp.start()
        passed = [_remote(landed(k, c), landed(k, c), send_sems, recv_sems, 3 + k, sib) for k in range(3)]
        for k in range(3):
            _remote(landed(k, c), landed(k, c), send_sems, recv_sems, k, (*peers[k], c)).wait_recv()
            passed[k].start()
        for k in range(3):
            _remote(landed(k, 1 - c), landed(k, 1 - c), send_sems, recv_sems, 3 + k, sib).wait_recv()
        for cp in first + passed:
            cp.wait_send()
        mine.wait()

    return _pcall(
        body, name="gather_weights", in_specs=[_ANY], out_specs=_ANY, out_shape=_sds((4, rows, cols), packed.dtype),
        scratch_shapes=[pltpu.SemaphoreType.DMA((6,)), pltpu.SemaphoreType.DMA((6,)), pltpu.SemaphoreType.DMA],
    )(packed)


def _gather_small_call(x):
    rows, cols = x.shape

    def body(x_ref, o_ref, send_sems, recv_sems):
        xi, yi, c = _axes()
        chip, peers = 2 * xi + yi, _chip_peers(xi, yi)
        o_ref[chip] = x_ref[...]
        sends = [_remote(x_ref, o_ref.at[chip], send_sems, recv_sems, k, (*peers[k], c)) for k in range(3)]
        for cp in sends:
            cp.start()
        for k in range(3):
            _remote(x_ref, o_ref.at[2 * peers[k][0] + peers[k][1]], send_sems, recv_sems, k, (*peers[k], c)).wait_recv()
        for cp in sends:
            cp.wait_send()

    return _pcall(
        body, name="gather_small", in_specs=[_VMEM], out_specs=_VMEM, out_shape=_sds((4, rows, cols), x.dtype),
        scratch_shapes=[pltpu.SemaphoreType.DMA((3,)), pltpu.SemaphoreType.DMA((3,))],
    )(x)


def _swap_sibling_call(v, name):
    def body(v_ref, o_ref, send_sems, recv_sems):
        x, y, c = _axes()
        cp = _remote(v_ref, o_ref, send_sems, recv_sems, 0, (x, y, 1 - c))
        cp.start()
        cp.wait()

    return _pcall(
        body, name=name, in_specs=[_ANY], out_specs=_ANY, out_shape=_sds(v.shape, v.dtype),
        scratch_shapes=[pltpu.SemaphoreType.DMA((1,)), pltpu.SemaphoreType.DMA((1,))],
    )(v)


def _send_chips_call(v):
    _, rows, cols = v.shape

    def body(v_ref, o_ref, send_sems, recv_sems):
        x, y, c = _axes()
        peers = _chip_peers(x, y)
        cps = [_remote(v_ref.at[2 * peers[k][0] + peers[k][1]], o_ref.at[k], send_sems, recv_sems, k, (*peers[k], c))
               for k in range(3)]
        for cp in cps:
            cp.start()
        for cp in cps:
            cp.wait()

    return _pcall(
        body, name="scatter_partials", in_specs=[_ANY], out_specs=_ANY, out_shape=_sds((3, rows, cols), v.dtype),
        scratch_shapes=[pltpu.SemaphoreType.DMA((3,)), pltpu.SemaphoreType.DMA((3,))],
    )(v)


def _allreduce_small_call(v):
    rows, cols = v.shape

    def body(v_ref, o_ref, buf, send_sems, recv_sems):
        x, y, c = _axes()
        me = 4 * x + 2 * y + c
        buf[0] = v_ref[...]
        cps = []
        for m in range(1, 8):
            dev = (1 - x if m & 4 else x, 1 - y if m & 2 else y, 1 - c if m & 1 else c)
            cps.append(_remote(v_ref, buf.at[m], send_sems, recv_sems, m - 1, dev))
        for cp in cps:
            cp.start()
        for cp in cps:
            cp.wait()
        acc = buf[me]
        for d in range(1, 8):
            acc = acc + buf[lax.bitwise_xor(me, d)]
        o_ref[...] = acc

    return _pcall(
        body, name="allreduce_small", in_specs=[_VMEM], out_specs=_VMEM, out_shape=_sds(v.shape, F32),
        scratch_shapes=[pltpu.VMEM((8, rows, cols), F32), pltpu.SemaphoreType.DMA((7,)), pltpu.SemaphoreType.DMA((7,))],
        compiler_params=_cparams(),
    )(v)


def _pack_rows(arrays, dtype, total_rows=None):
    parts = []
    for a in arrays:
        flat = a.astype(dtype).reshape(-1)
        parts.append(jnp.pad(flat, (0, -flat.shape[0] % D_MODEL)).reshape(-1, D_MODEL))
    out = jnp.concatenate(parts, axis=0)
    total_rows = total_rows or out.shape[0] + (-out.shape[0] % 8)
    return jnp.pad(out, ((0, total_rows - out.shape[0]), (0, 0)))


def _unpack_rows(packed, shapes):
    out, r = [], 0
    for s in shapes:
        n = math.prod(s)
        nr = -(-n // D_MODEL)
        out.append(packed[r:r + nr].reshape(-1)[:n].reshape(s))
        r += nr
    return out


def _to_r(w):
    pad = jnp.zeros(w.shape[:-1] + (N_R - N_ORIG,), w.dtype)
    return jnp.concatenate([w[..., A_ORIG_W:BA_ORIG], w[..., :A_ORIG_W], w[..., BA_ORIG + 2 * HEADS:],
                            w[..., BA_ORIG:BA_ORIG + 2 * HEADS], pad], axis=-1)


def _from_r(w):
    return jnp.concatenate([w[..., A_OFF:C_OFF], w[..., :A_OFF], w[..., BA_OFF:BA_OFF + 2 * HEADS],
                            w[..., C_OFF:BA_OFF]], axis=-1)


_BIG = ("w_in", "a_proj", "b_proj", "c_proj", "w_out")
_BIG_AXIS = {"w_in": 2, "a_proj": 2, "b_proj": 1, "c_proj": 2, "w_out": 1}
_ORDER = ("norm_g", "w_in", "a_dw", "a_dw_b", "a_ln_g", "a_ln_b", "a_proj", "b_conv", "b_a_log", "b_dt_bias",
          "b_onorm_g", "b_proj", "c_ln_g", "c_ln_b", "c_ws", "c_bs", "c_proj", "w_out", "final_g")


def _local_step(x, tgt, w):
    bsz, t, _ = x.shape
    m = bsz * t
    depth = w["norm_g"].shape[0]
    row = lambda v: v.reshape(1, -1)
    w_r = _to_r(w["w_in"])
    saved, xl = [], x
    for l in range(depth):
        n = f"l{l}_"
        par = dict(
            adw=jnp.pad(w["a_dw"][l], ((0, 1), (0, 0))), adb=row(w["a_dw_b"][l]), alg=row(w["a_ln_g"][l]),
            alb=row(w["a_ln_b"][l]), bconv=w["b_conv"][l], alog=jnp.pad(row(w["b_a_log"][l]), ((0, 0), (0, DK - HEADS))),
            dtb=jnp.pad(row(w["b_dt_bias"][l]), ((0, 0), (0, DK - HEADS))), og=row(w["b_onorm_g"][l]),
            clg=row(w["c_ln_g"][l]), clb=row(w["c_ln_b"][l]), cws=w["c_ws"][l],
            cbias=jnp.repeat(w["c_bs"][l].T, SG_C, axis=1), ng=row(w["norm_g"][l]),
            wa=w["a_proj"][l], wb=w["b_proj"][l], wc=w["c_proj"][l], wo=w["w_out"][l], wr=w_r[l])
        h = _rms_fwd_call(xl.reshape(m, D_MODEL), par["ng"], n + "norm")
        proj = _matmul(h, par["wr"], F32, 1024, 1152, 1024, n + "in_proj").reshape(bsz, t, N_R)
        ya = _a_fwd_call(proj, par["adw"], par["adb"], par["alg"], par["alb"], n + "conv_fwd")
        yb, o_raw, *dn_saved = _dn_fwd_call(proj, par["bconv"], par["alog"], par["dtb"], par["og"], n + "delta_fwd")
        yc = _c_fwd_call(proj, par["clg"], par["clb"], par["cws"], par["cbias"], n + "gmlp_fwd")
        x_next = _m_fwd_call(ya, yb, yc, proj, xl, par["wa"], par["wb"], par["wc"], par["wo"], n + "merge_fwd")
        saved.append((par, xl, h, proj, ya, yb, yc, o_raw, dn_saved))
        xl = x_next
    dout, dfg, loss = _loss_call(xl.reshape(m, D_MODEL), tgt.reshape(m, D_MODEL), row(w["final_g"]))
    dout = dout.reshape(bsz, t, D_MODEL)
    g = {k: [None] * depth for k in _ORDER if k != "final_g"}
    for l in reversed(range(depth)):
        n = f"l{l}_"
        par, xl, h, proj, ya, yb, yc, o_raw, dn_saved = saved[l]
        dya, dyb, dyc, dgate, merged, dps = _m_bwd_call(dout, ya, yb, yc, proj, par["wa"], par["wb"], par["wc"],
                                                        par["wo"], n + "merge_bwd")
        g["a_proj"][l], g["b_proj"][l], g["c_proj"][l], g["w_out"][l] = _m_wgrad_call(ya, yb, yc, merged, dps, dout,
                                                                                      n + "merge_wgrad")
        da, dadw, dasm = _a_bwd_call(dya, proj, par["adw"], par["adb"], par["alg"], par["alb"], n + "conv_bwd")
        db, dba, dbconv, dbsm = _dn_bwd_call(dyb, o_raw, dn_saved, proj, par["bconv"], par["alog"], par["dtb"], par["og"],
                                             n + "delta_bwd")
        dc, dcws, dcbs, dcsm = _c_bwd_call(dyc, proj, par["clg"], par["clb"], par["cws"], par["cbias"], n + "gmlp_bwd")
        dproj = jnp.concatenate([db, da, dc, dgate, dba], axis=-1).reshape(m, N_R)
        dh = _matmul(dproj, par["wr"].T, F32, 1024, 1024, 1152, n + "in_proj_dx")
        g["w_in"][l] = _from_r(_matmul(h.T, dproj, F32, 1024, 1152, 1024, n + "in_proj_dw"))
        dx, dng = _rms_bwd_call(xl.reshape(m, D_MODEL), dh, par["ng"], dout.reshape(m, D_MODEL), n + "norm_bwd")
        dout = dx.reshape(bsz, t, D_MODEL)
        g["norm_g"][l] = dng[0]
        g["a_dw"][l], g["a_dw_b"][l], g["a_ln_g"][l], g["a_ln_b"][l] = dadw[:CONV_K], dasm[0], dasm[1], dasm[2]
        g["b_conv"][l], g["b_a_log"][l], g["b_dt_bias"][l] = dbconv[:SHORT_K], dbsm[0, :HEADS], dbsm[1, :HEADS]
        g["b_onorm_g"][l] = dbsm[2]
        g["c_ln_g"][l], g["c_ln_b"][l], g["c_ws"][l] = dcsm[0], dcsm[1], dcws
        g["c_bs"][l] = dcbs.reshape(SG_C, SG_G, SG_C).sum(-1).T
    grads = {k: jnp.stack(v) for k, v in g.items()}
    grads["final_g"] = dfg[0]
    return loss[0, 0], dout, grads


def kernel(x, norm_g, w_in, a_dw, a_dw_b, a_ln_g, a_ln_b, a_proj, b_conv, b_a_log, b_dt_bias, b_onorm_g, b_proj, c_ln_g, c_ln_b, c_ws, c_bs, c_proj, w_out, final_g, loss_target, m_norm_g, m_w_in, m_a_dw, m_a_dw_b, m_a_ln_g, m_a_ln_b, m_a_proj, m_b_conv, m_b_a_log, m_b_dt_bias, m_b_onorm_g, m_b_proj, m_c_ln_g, m_c_ln_b, m_c_ws, m_c_bs, m_c_proj, m_w_out, m_final_g, v_norm_g, v_w_in, v_a_dw, v_a_dw_b, v_a_ln_g, v_a_ln_b, v_a_proj, v_b_conv, v_b_a_log, v_b_dt_bias, v_b_onorm_g, v_b_proj, v_c_ln_g, v_c_ln_b, v_c_ws, v_c_bs, v_c_proj, v_w_out, v_final_g):
    given = dict(locals())
    ws = {k: given[k] for k in _ORDER}
    xi, yi, ci = _axes()
    chip = 2 * xi + yi

    shard_shapes = [ws[k].shape for k in _BIG]
    gathered = _gather_weights_call(_pack_rows([ws[k] for k in _BIG], BF16, PACK_ROWS))
    full = dict(ws)
    per_chip = [_unpack_rows(gathered[j], shard_shapes) for j in range(4)]
    for i, k in enumerate(_BIG):
        full[k] = jnp.concatenate([per_chip[j][i] for j in range(4)], axis=_BIG_AXIS[k])
    conv_shapes = [a_dw.shape, b_conv.shape]
    small = _gather_small_call(_pack_rows([a_dw, b_conv], F32).reshape(-1, DK))
    per_chip = [_unpack_rows(small[j].reshape(-1, D_MODEL), conv_shapes) for j in range(4)]
    full["a_dw"] = jnp.concatenate([p[0] for p in per_chip], axis=2)
    full["b_conv"] = jnp.concatenate([p[1] for p in per_chip], axis=2)

    loss, grad_x, grads = _local_step(x, loss_target, full)

    def shard_of(a, k, j):
        n = a.shape[_BIG_AXIS[k]] // 4
        return lax.slice_in_dim(a, j * n, (j + 1) * n, axis=_BIG_AXIS[k])

    by_chip = jnp.stack([_pack_rows([shard_of(grads[k], k, j) for k in _BIG], F32, PACK_ROWS) for j in range(4)])
    half = PACK_ROWS // 2
    mine = lax.dynamic_slice_in_dim(by_chip, ci * half, half, axis=1)
    theirs = lax.dynamic_slice_in_dim(by_chip, (1 - ci) * half, half, axis=1).astype(BF16)
    part, part_bf = _add_pair_call(mine, _swap_sibling_call(theirs, "pair_partials"), "pair_sum")
    recv = _send_chips_call(part_bf)
    fin = _add_recv_call(lax.dynamic_index_in_dim(part, chip, 0, keepdims=False), recv, "chip_sum")
    other = _swap_sibling_call(fin, "pair_result")
    reduced = lax.dynamic_update_slice_in_dim(jnp.zeros((PACK_ROWS, D_MODEL), F32), fin, ci * half, axis=0)
    reduced = lax.dynamic_update_slice_in_dim(reduced, other, (1 - ci) * half, axis=0)
    out_g = dict(zip(_BIG, _unpack_rows(reduced, shard_shapes)))

    rest = [k for k in _ORDER if k not in _BIG]
    rest_shapes = [grads[k].shape for k in rest] + [(1,)]
    summed = _unpack_rows(_allreduce_small_call(_pack_rows([grads[k] for k in rest] + [loss.reshape(1)], F32)),
                          rest_shapes)
    out_g.update(zip(rest, summed[:-1]))
    out_g["a_dw"] = lax.dynamic_slice_in_dim(out_g["a_dw"], chip * a_dw.shape[2], a_dw.shape[2], axis=2)
    out_g["b_conv"] = lax.dynamic_slice_in_dim(out_g["b_conv"], chip * b_conv.shape[2], b_conv.shape[2], axis=2)

    upd = {k: _adamw_call(ws[k], out_g[k], given["m_" + k], given["v_" + k], "adamw_" + k) for k in _ORDER}
    return (summed[-1][0], grad_x, *[out_g[k] for k in _ORDER], *[upd[k][0] for k in _ORDER],
            *[upd[k][1] for k in _ORDER], *[upd[k][2] for k in _ORDER])
```

```python
import functools
import math

import jax
import jax.numpy as jnp
from jax import lax
from jax.experimental import pallas as pl
from jax.experimental.pallas import tpu as pltpu

F32 = jnp.float32
BF16 = jnp.bfloat16
MESH = pl.DeviceIdType.MESH

D_MODEL = 1024
CONV_W = 512
CONV_K = 31
HEADS = 8
DK = 128
SHORT_K = 4
CHUNK = 64
SG_W = 512
SG_G = 4
SG_C = 128
EPS = 1e-6
N_ORIG = 10256
BA_ORIG = 5632
A_ORIG_W = 3 * CONV_W
B_W = 4 * D_MODEL
B_OFF, A_OFF, C_OFF, G_OFF, BA_OFF, N_R = 0, 4096, 5632, 7168, 10240, 10368
TB = 256
HALO_A = 32
HALO_B = 8
VMEM_LIMIT = 56 * 1024 * 1024
ADAM_LR, ADAM_B1, ADAM_B2, ADAM_EPS, ADAM_WD, ADAM_STEP = 0.001, 0.9, 0.999, 1e-08, 0.01, 10
PACK_ROWS = 6688


def _pcall(body, **kw):
    return pl.pallas_call(body, **kw)


def _cparams(sem=None):
    kw = dict(vmem_limit_bytes=VMEM_LIMIT)
    if sem is not None:
        kw["dimension_semantics"] = sem
    return pltpu.CompilerParams(**kw)


def _sig(x):
    return jax.nn.sigmoid(x)


def _silu(x):
    return x * _sig(x)


def _dsilu(x):
    s = _sig(x)
    return s * (1.0 + x * (1.0 - s))


_GELU_C = math.sqrt(2.0 / math.pi)


def _gelu(x):
    return 0.5 * x * (1.0 + jnp.tanh(_GELU_C * (x + 0.044715 * x * x * x)))


def _dgelu(x):
    t = jnp.tanh(_GELU_C * (x + 0.044715 * x * x * x))
    return 0.5 * (1.0 + t) + 0.5 * x * (1.0 - t * t) * _GELU_C * (1.0 + 3 * 0.044715 * x * x)


def _softplus(x):
    return jnp.maximum(x, 0.0) + jnp.log1p(jnp.exp(-jnp.abs(x)))


def _dot(a, b, dims):
    return lax.dot_general(a.astype(BF16), b.astype(BF16), (dims, ((), ())), preferred_element_type=F32)


def _nn(a, b):
    return _dot(a, b, ((1,), (0,)))


def _nt(a, b):
    return _dot(a, b, ((1,), (1,)))


def _tn(a, b):
    return _dot(a, b, ((0,), (0,)))


def _tn_mxu(a, b):
    n = a.shape[1]
    eye = (lax.broadcasted_iota(jnp.int32, (n, n), 0) == lax.broadcasted_iota(jnp.int32, (n, n), 1)).astype(BF16)
    return _nn(_nt(eye, a), b)


def _mean(x):
    return jnp.mean(x, axis=-1, keepdims=True)


def _sum0(x):
    return jnp.sum(x, axis=0, keepdims=True)


def _sum1(x):
    return jnp.sum(x, axis=1, keepdims=True)


def _ln_fwd(x, g, b):
    xc = x - _mean(x)
    r = lax.rsqrt(_mean(xc * xc) + EPS)
    xh = xc * r
    return xh * g + b, xh, r


def _ln_bwd(dy, xh, r, g):
    dxh = dy * g
    return r * (dxh - _mean(dxh) - xh * _mean(dxh * xh)), _sum0(dy * xh), _sum0(dy)


def _rms_bwd(dxh, xh, r):
    return r * (dxh - xh * _mean(dxh * xh))


def _lane(x, idx):
    lanes = lax.broadcasted_iota(jnp.int32, x.shape, 1)
    return _sum1(jnp.where(lanes == idx, x, 0.0))


def _rows8(*rows):
    n = rows[0].shape[1]
    return jnp.concatenate(list(rows) + [jnp.zeros((8 - len(rows), n), F32)], axis=0)


def _a_fwd(val, glu, az, val_h, glu_h, w, bias, g, b):
    tb = val.shape[0]
    a_ext = jnp.concatenate([val_h * _sig(glu_h), val * _sig(glu)], axis=0)
    c = a_ext[2:2 + tb] * w[0:1]
    for j in range(1, CONV_K):
        c = c + a_ext[2 + j:2 + j + tb] * w[j:j + 1]
    ln, xh, r = _ln_fwd(c + bias, g, b)
    return _silu(ln) * _silu(az), (a_ext, ln, xh, r)


def _a_bwd(dy, val, glu, az, val_h, glu_h, w, bias, g, b, dc_next):
    tb = val.shape[0]
    _, (a_ext, ln, xh, r) = _a_fwd(val, glu, az, val_h, glu_h, w, bias, g, b)
    dln = dy * _silu(az) * _dsilu(ln)
    daz = dy * _silu(ln) * _dsilu(az)
    dc, dg, db = _ln_bwd(dln, xh, r, g)
    dc_ext = jnp.concatenate([dc, dc_next], axis=0)
    da = dc_ext[30:30 + tb] * w[0:1]
    dw_rows = [_sum0(dc * a_ext[2:2 + tb])]
    for j in range(1, CONV_K):
        da = da + dc_ext[30 - j:30 - j + tb] * w[j:j + 1]
        dw_rows.append(_sum0(dc * a_ext[2 + j:2 + j + tb]))
    dw = jnp.concatenate(dw_rows + [jnp.zeros((1, CONV_W), F32)], axis=0)
    sg = _sig(glu)
    return da * sg, da * val * sg * (1.0 - sg), daz, dw, _sum0(dc), dg, db, dc[:HALO_A]


def _tril(ws):
    ii = lax.broadcasted_iota(jnp.int32, (SG_C, SG_C), 0)
    jj = lax.broadcasted_iota(jnp.int32, (SG_C, SG_C), 1)
    return [jnp.where(jj <= ii, ws[gi], 0.0) for gi in range(SG_G)], jj <= ii


def _c_mix(wt, vs, bias_full):
    tb = vs.shape[0]
    rows = []
    for n in range(tb // SG_C):
        blks = [_nn(wt[gi], vs[n * SG_C:(n + 1) * SG_C, gi * SG_C:(gi + 1) * SG_C]) for gi in range(SG_G)]
        rows.append(jnp.concatenate(blks, axis=1) + bias_full)
    return jnp.concatenate(rows, axis=0)


def _c_fwd(cu, cv, cz, g, b, ws, bias_full):
    wt, _ = _tril(ws)
    vs, xh, r = _ln_fwd(_gelu(cv), g, b)
    mixed = _c_mix(wt, vs, bias_full)
    return _gelu(cu) * mixed * _silu(cz), (wt, vs, xh, r, mixed)


def _c_bwd(dy, cu, cv, cz, g, b, ws, bias_full):
    tb = cu.shape[0]
    _, (wt, vs, xh, r, mixed) = _c_fwd(cu, cv, cz, g, b, ws, bias_full)
    _, low = _tril(ws)
    u, sz = _gelu(cu), _silu(cz)
    dcu = dy * mixed * sz * _dgelu(cu)
    dcz = dy * u * mixed * _dsilu(cz)
    dmixed = dy * u * sz
    dbs = jnp.zeros((SG_C, SG_W), F32)
    dws = [jnp.zeros((SG_C, SG_C), F32) for _ in range(SG_G)]
    rows = []
    for n in range(tb // SG_C):
        dm_n = dmixed[n * SG_C:(n + 1) * SG_C]
        dbs = dbs + dm_n
        blks = []
        for gi in range(SG_G):
            dm = dm_n[:, gi * SG_C:(gi + 1) * SG_C]
            dws[gi] = dws[gi] + _nt(dm, vs[n * SG_C:(n + 1) * SG_C, gi * SG_C:(gi + 1) * SG_C])
            blks.append(_tn(wt[gi], dm))
        rows.append(jnp.concatenate(blks, axis=1))
    dvs = jnp.concatenate(rows, axis=0)
    dgv, dg, db = _ln_bwd(dvs, xh, r, g)
    dws = [jnp.where(low, d, 0.0) for d in dws]
    return dcu, dgv * _dgelu(cv), dcz, dws, dbs, dg, db


def _m_fwd(ya, yb, yc, g0, g1, g2, wa, wb, wc):
    pa, pb, pc = _nn(ya, wa), _nn(yb, wb), _nn(yc, wc)
    s0, s1, s2 = _sig(g0), _sig(g1), _sig(g2)
    return s0 * pa + s1 * pb + s2 * pc, (pa, pb, pc, s0, s1, s2)


def _m_bwd(dout, ya, yb, yc, g0, g1, g2, wa, wb, wc, wo):
    merged, (pa, pb, pc, s0, s1, s2) = _m_fwd(ya, yb, yc, g0, g1, g2, wa, wb, wc)
    dm = _nt(dout, wo)
    dpa, dpb, dpc = dm * s0, dm * s1, dm * s2
    dgs = (dm * pa * s0 * (1.0 - s0), dm * pb * s1 * (1.0 - s1), dm * pc * s2 * (1.0 - s2))
    return (_nt(dpa, wa), _nt(dpb, wb), _nt(dpc, wc)), dgs, merged, (dpa, dpb, dpc)


def _chunk_masks(c):
    ii = lax.broadcasted_iota(jnp.int32, (c, c), 0)
    jj = lax.broadcasted_iota(jnp.int32, (c, c), 1)
    return ii, jj


def _dn_decay(items):
    ii, jj = _chunk_masks(CHUNK)
    incl, strict, eye = jj <= ii, jj < ii, ii == jj
    for d in items:
        g = d["g"]
        grow = _sum0(jnp.where(eye, g, 0.0))
        gc_col = _sum1(jnp.where(incl, grow, 0.0))
        gc_row = _sum0(jnp.where(ii <= jj, g, 0.0))
        gam_i = jnp.where(incl, jnp.exp(jnp.where(incl, gc_col - gc_row, 0.0)), 0.0)
        gl = _sum0(g)
        egc = jnp.exp(gc_col)
        d.update(gam_i=gam_i, gam_s=jnp.where(strict, gam_i, 0.0), egc=egc, ekd=jnp.exp(gl - gc_col), dl=jnp.exp(gl),
                 gdiff=gc_col - gc_row, qd=d["q"] * egc, rhs_w=d["k"] * (d["b"] * egc))
        d["kd"] = d["k"] * d["ekd"]
    return ii, jj, strict, eye


def _dn_solve(items, ii, jj, eye):
    off = ((ii >> 1) == (jj >> 1)) & ((ii & 1) != 0) & ((jj & 1) == 0)
    for d in items:
        a1 = jnp.where(off, d["a"], 0.0)
        d["t"] = jnp.where(eye, 1.0, 0.0) - a1
        d["m"] = d["a"] - _nn(a1, d["a"])
    b, sh = 2, 2
    while b < CHUNK:
        off = ((ii >> sh) == (jj >> sh)) & ((ii & b) != 0) & ((jj & b) == 0)
        for d in items:
            mo = jnp.where(off, d["m"], 0.0)
            if 2 * b < CHUNK:
                r = _nn(mo, jnp.concatenate([d["m"], d["t"]], axis=1))
                d["m"], d["t"] = d["m"] - r[:, :CHUNK], d["t"] - r[:, CHUNK:]
            else:
                d["t"] = d["t"] - _nn(mo, d["t"])
        b, sh = 2 * b, sh + 1


def _dn_fwd_chunk(items, ss):
    ii, jj, strict, eye = _dn_decay(items)
    for d in items:
        kq = _nt(jnp.concatenate([d["k"], d["q"]], axis=0), d["k"])
        d["a"] = d["b"] * kq[:CHUNK] * d["gam_s"]
        d["qk"] = kq[CHUNK:] * d["gam_i"]
    _dn_solve(items, ii, jj, eye)
    for d in items:
        d["uw"] = _nn(d["t"], jnp.concatenate([d["v"] * d["b"], d["rhs_w"]], axis=1))
    ws = [_nn(jnp.concatenate([d["uw"][:, DK:], d["qd"]], axis=0), s) for d, s in zip(items, ss)]
    vnew = [d["uw"][:, :DK] - w[:CHUNK] for d, w in zip(items, ws)]
    outs = [w[CHUNK:] + _nn(d["qk"], vn) for d, w, vn in zip(items, ws, vnew)]
    ss = [d["dl"] * s + _tn(d["kd"], vn) for d, s, vn in zip(items, ss, vnew)]
    return outs, ss, vnew


def _dn_bwd_chunk(items, ss, dss):
    ii, jj, strict, eye = _dn_decay(items)
    for d in items:
        kq = jnp.concatenate([d["k"], d["q"]], axis=0)
        gram = _nt(kq, kq)
        d["kk"] = gram[:CHUNK, :CHUNK]
        d["a"] = d["b"] * d["kk"] * d["gam_s"]
        d["qk"] = gram[CHUNK:, :CHUNK] * d["gam_i"]
        d["qkt"] = gram[:CHUNK, CHUNK:] * jnp.where(ii <= jj, jnp.exp(jnp.where(ii <= jj, -d["gdiff"], 0.0)), 0.0)
    for d, s, ds2 in zip(items, ss, dss):
        d["dvnew"] = _nn(jnp.concatenate([d["qkt"], d["kd"]], axis=1), jnp.concatenate([d["do"], ds2], axis=0))
        r = _nt(d["do"], jnp.concatenate([d["vnew"], s], axis=0))
        d["dqk"], d["dqd"] = r[:, :CHUNK], r[:, CHUNK:]
        d["dkd"] = _nt(d["vnew"], ds2)
    new_dss = []
    for d, s, ds2 in zip(items, ss, dss):
        d["dw"] = -_nt(d["dvnew"], s)
        new_dss.append(_tn_mxu(jnp.concatenate([d["qd"], -d["uw"][:, DK:]], axis=0),
                               jnp.concatenate([d["do"], d["dvnew"]], axis=0)) + d["dl"] * ds2)
        d["ddl"] = _sum0(_sum1(s * ds2))
    for d in items:
        d["drhs"] = _tn_mxu(d["t"], jnp.concatenate([d["dvnew"], d["dw"]], axis=1))
    for d in items:
        d["da"] = jnp.where(strict, -_nt(d["drhs"], d["uw"]), 0.0)
    outs = []
    for d in items:
        q, k, v, b, egc = d["q"], d["k"], d["v"], d["b"], d["egc"]
        drhs_u, drhs_w, da = d["drhs"][:, :DK], d["drhs"][:, DK:], d["da"]
        dbeta = _sum1(da * d["kk"] * d["gam_s"]) + _sum1(drhs_u * v) + _sum1(drhs_w * k) * egc
        dkk = da * b * d["gam_s"]
        e = da * d["a"] + d["dqk"] * d["qk"]
        s_kd = _sum1(d["dkd"] * d["kd"])
        dgc_col = _sum1(e) + _sum1(drhs_w * d["rhs_w"]) + _sum1(d["dqd"] * d["qd"]) - s_kd
        dgc_row = _sum0(jnp.where(eye, dgc_col, 0.0)) - _sum0(e)
        dg = _sum1(jnp.where(jj >= ii, dgc_row, 0.0)) + (_sum0(s_kd) + d["ddl"] * d["dl"])
        x = jnp.concatenate([d["dqk"] * d["gam_i"], dkk], axis=0)
        x1 = _nn(x, k)
        x2 = _tn_mxu(x, jnp.concatenate([q, k], axis=0))
        dq = x1[:CHUNK] + d["dqd"] * egc
        dk = x2 + x1[CHUNK:] + drhs_w * (b * egc) + d["dkd"] * d["ekd"]
        outs.append((dq, dk, drhs_u * b, dbeta, dg))
    return outs, new_dss


def _short_conv(raw, halo, w):
    tb = raw.shape[0]
    ext = jnp.concatenate([halo, raw], axis=0)
    out = ext[5:5 + tb] * w[0:1]
    for j in range(1, SHORT_K):
        out = out + ext[5 + j:5 + j + tb] * w[j:j + 1]
    return out


def _dn_heads(pre, ba, alog8, dtb8):
    sp = _sig(pre)
    act = pre * sp
    bsig = _sig(ba)
    xg = ba + dtb8
    ea8 = jnp.exp(alog8)
    gfull = -ea8 * _softplus(xg)
    items = []
    for h in range(HEADS):
        qc, kc, vc = (act[:, n * D_MODEL + h * DK:n * D_MODEL + (h + 1) * DK] for n in range(3))
        nq = lax.rsqrt(_sum1(qc * qc) + EPS)
        nk = lax.rsqrt(_sum1(kc * kc) + EPS)
        items.append(dict(q=qc * nq * (DK ** -0.5), k=kc * nk, v=vc, b=bsig[:, h:h + 1],
                          g=gfull[:, HEADS + h:HEADS + h + 1], qc=qc, kc=kc, nq=nq, nk=nk))
    return items, sp, xg, ea8


def _dn_out(o, z, sz, og):
    r = lax.rsqrt(_mean(o * o) + EPS)
    xh = o * r
    return xh * og * (z * sz), xh, r


def _sds(shape, dtype):
    return jax.ShapeDtypeStruct(tuple(shape), dtype)


def _matmul(a, b, out_dtype, tm, tn, tk, name):
    m, kd = a.shape
    n = b.shape[1]
    tm, tn, tk = min(tm, m), min(tn, n), min(tk, kd)
    nk = kd // tk

    def body(a_ref, b_ref, o_ref, acc):
        @pl.when(pl.program_id(2) == 0)
        def _():
            acc[...] = jnp.zeros_like(acc)

        acc[...] += jnp.dot(a_ref[...], b_ref[...], preferred_element_type=F32)

        @pl.when(pl.program_id(2) == nk - 1)
        def _():
            o_ref[...] = acc[...].astype(o_ref.dtype)

    return _pcall(
        body, name=name, grid=(m // tm, n // tn, nk),
        in_specs=[pl.BlockSpec((tm, tk), lambda i, j, k: (i, k)), pl.BlockSpec((tk, tn), lambda i, j, k: (k, j))],
        out_specs=pl.BlockSpec((tm, tn), lambda i, j, k: (i, j)), out_shape=_sds((m, n), out_dtype),
        scratch_shapes=[pltpu.VMEM((tm, tn), F32)], compiler_params=_cparams(("parallel", "parallel", "arbitrary")),
    )(a, b)


def _rms_fwd_call(x, g, name):
    m = x.shape[0]
    tm = min(512, m)

    def body(x_ref, g_ref, o_ref):
        xv = x_ref[...]
        o_ref[...] = (xv * lax.rsqrt(_mean(xv * xv) + EPS) * g_ref[...]).astype(BF16)

    return _pcall(
        body, name=name, grid=(m // tm,),
        in_specs=[pl.BlockSpec((tm, D_MODEL), lambda i: (i, 0)), pl.BlockSpec((1, D_MODEL), lambda i: (0, 0))],
        out_specs=pl.BlockSpec((tm, D_MODEL), lambda i: (i, 0)), out_shape=_sds((m, D_MODEL), BF16),
        compiler_params=_cparams(("parallel",)),
    )(x, g)


def _rms_bwd_call(x, dh, g, dres, name):
    m = x.shape[0]
    tm = min(512, m)

    def body(x_ref, dh_ref, g_ref, dr_ref, dx_ref, dg_ref):
        @pl.when(pl.program_id(0) == 0)
        def _():
            dg_ref[...] = jnp.zeros_like(dg_ref)

        xv, dhv = x_ref[...], dh_ref[...]
        r = lax.rsqrt(_mean(xv * xv) + EPS)
        xh = xv * r
        dx_ref[...] = _rms_bwd(dhv * g_ref[...], xh, r) + dr_ref[...]
        dg_ref[...] += _rows8(_sum0(dhv * xh))

    row = pl.BlockSpec((tm, D_MODEL), lambda i: (i, 0))
    return _pcall(
        body, name=name, grid=(m // tm,),
        in_specs=[row, row, pl.BlockSpec((1, D_MODEL), lambda i: (0, 0)), row],
        out_specs=[row, pl.BlockSpec((8, D_MODEL), lambda i: (0, 0))],
        out_shape=[_sds((m, D_MODEL), F32), _sds((8, D_MODEL), F32)], compiler_params=_cparams(("arbitrary",)),
    )(x, dh, g, dres)


def _loss_call(x, tgt, g):
    m = x.shape[0]
    tm = min(512, m)

    def body(x_ref, t_ref, g_ref, dx_ref, dg_ref, l_ref):
        @pl.when(pl.program_id(0) == 0)
        def _():
            dg_ref[...] = jnp.zeros_like(dg_ref)
            l_ref[...] = jnp.zeros_like(l_ref)

        xv = x_ref[...]
        r = lax.rsqrt(_mean(xv * xv) + EPS)
        xh = xv * r
        err = xh * g_ref[...] - t_ref[...]
        dy = err * (1.0 / D_MODEL)
        dx_ref[...] = _rms_bwd(dy * g_ref[...], xh, r)
        dg_ref[...] += _rows8(_sum0(dy * xh))
        l_ref[...] += 0.5 * _sum0(_mean(err * err))

    row = pl.BlockSpec((tm, D_MODEL), lambda i: (i, 0))
    return _pcall(
        body, name="loss_head", grid=(m // tm,),
        in_specs=[row, row, pl.BlockSpec((1, D_MODEL), lambda i: (0, 0))],
        out_specs=[row, pl.BlockSpec((8, D_MODEL), lambda i: (0, 0)), pl.BlockSpec((8, 128), lambda i: (0, 0))],
        out_shape=[_sds((m, D_MODEL), F32), _sds((8, D_MODEL), F32), _sds((8, 128), F32)],
        compiler_params=_cparams(("arbitrary",)),
    )(x, tgt, g)


def _halo_idx(i, rows):
    return jnp.maximum(i * (TB // rows) - 1, 0)


def _a_specs(nt, rev):
    ti = (lambda i: nt - 1 - i) if rev else (lambda i: i)
    c0 = A_OFF // CONV_W
    tile = [pl.BlockSpec((1, TB, CONV_W), functools.partial(lambda b, i, c: (b, ti(i), c), c=c0 + c)) for c in range(3)]
    halo = [pl.BlockSpec((1, HALO_A, CONV_W),
                         functools.partial(lambda b, i, c: (b, _halo_idx(ti(i), HALO_A), c), c=c0 + c)) for c in range(2)]
    par = [pl.BlockSpec((HALO_A, CONV_W), lambda b, i: (0, 0))] + [pl.BlockSpec((1, CONV_W), lambda b, i: (0, 0))] * 3
    return tile + halo + par


def _a_fwd_call(proj, w, bias, g, b, name):
    bsz, t, _ = proj.shape
    nt = t // TB

    def body(val_ref, glu_ref, az_ref, vh_ref, gh_ref, w_ref, bias_ref, g_ref, b_ref, y_ref):
        keep = jnp.where(pl.program_id(1) > 0, 1.0, 0.0)
        y, _ = _a_fwd(val_ref[0], glu_ref[0], az_ref[0], vh_ref[0] * keep, gh_ref[0], w_ref[...], bias_ref[...],
                      g_ref[...], b_ref[...])
        y_ref[0] = y.astype(BF16)

    return _pcall(
        body, name=name, grid=(bsz, nt), in_specs=_a_specs(nt, False),
        out_specs=pl.BlockSpec((1, TB, CONV_W), lambda b, i: (b, i, 0)), out_shape=_sds((bsz, t, CONV_W), BF16),
        compiler_params=_cparams(("parallel", "parallel")),
    )(proj, proj, proj, proj, proj, w, bias, g, b)


def _a_bwd_call(dy, proj, w, bias, g, b, name):
    bsz, t, _ = proj.shape
    nt = t // TB

    def body(dy_ref, val_ref, glu_ref, az_ref, vh_ref, gh_ref, w_ref, bias_ref, g_ref, b_ref, da_ref, dw_ref, ds_ref,
             carry):
        ip = pl.program_id(1)

        @pl.when((pl.program_id(0) == 0) & (ip == 0))
        def _():
            dw_ref[...] = jnp.zeros_like(dw_ref)
            ds_ref[...] = jnp.zeros_like(ds_ref)

        @pl.when(ip == 0)
        def _():
            carry[...] = jnp.zeros_like(carry)

        keep = jnp.where(ip < nt - 1, 1.0, 0.0)
        dval, dglu, daz, dw, dbias, dg, db, head = _a_bwd(
            dy_ref[0], val_ref[0], glu_ref[0], az_ref[0], vh_ref[0] * keep, gh_ref[0], w_ref[...], bias_ref[...],
            g_ref[...], b_ref[...], carry[...])
        carry[...] = head
        da_ref[0] = jnp.concatenate([dval, dglu, daz], axis=1).astype(BF16)
        dw_ref[...] += dw
        ds_ref[...] += _rows8(dbias, dg, db)

    rtile = lambda b, i: (b, nt - 1 - i, 0)
    return _pcall(
        body, name=name, grid=(bsz, nt),
        in_specs=[pl.BlockSpec((1, TB, CONV_W), rtile)] + _a_specs(nt, True),
        out_specs=[pl.BlockSpec((1, TB, 3 * CONV_W), rtile), pl.BlockSpec((HALO_A, CONV_W), lambda b, i: (0, 0)),
                   pl.BlockSpec((8, CONV_W), lambda b, i: (0, 0))],
        out_shape=[_sds((bsz, t, 3 * CONV_W), BF16), _sds((HALO_A, CONV_W), F32), _sds((8, CONV_W), F32)],
        scratch_shapes=[pltpu.VMEM((HALO_A, CONV_W), F32)], compiler_params=_cparams(("arbitrary", "arbitrary")),
    )(dy, proj, proj, proj, proj, proj, w, bias, g, b)


def _c_specs():
    c0 = C_OFF // SG_W
    tile = [pl.BlockSpec((1, TB, SG_W), functools.partial(lambda b, i, c: (b, i, c), c=c0 + c)) for c in range(3)]
    par = [pl.BlockSpec((1, SG_W), lambda b, i: (0, 0))] * 2 + [
        pl.BlockSpec((SG_G, SG_C, SG_C), lambda b, i: (0, 0, 0)), pl.BlockSpec((SG_C, SG_W), lambda b, i: (0, 0))]
    return tile + par


def _c_fwd_call(proj, g, b, ws, bias_full, name):
    bsz, t, _ = proj.shape

    def body(cu_ref, cv_ref, cz_ref, g_ref, b_ref, ws_ref, bf_ref, y_ref):
        y, _ = _c_fwd(cu_ref[0], cv_ref[0], cz_ref[0], g_ref[...], b_ref[...], ws_ref[...], bf_ref[...])
        y_ref[0] = y.astype(BF16)

    return _pcall(
        body, name=name, grid=(bsz, t // TB), in_specs=_c_specs(),
        out_specs=pl.BlockSpec((1, TB, SG_W), lambda b, i: (b, i, 0)), out_shape=_sds((bsz, t, SG_W), BF16),
        compiler_params=_cparams(("parallel", "parallel")),
    )(proj, proj, proj, g, b, ws, bias_full)


def _c_bwd_call(dy, proj, g, b, ws, bias_full, name):
    bsz, t, _ = proj.shape

    def body(dy_ref, cu_ref, cv_ref, cz_ref, g_ref, b_ref, ws_ref, bf_ref, dc_ref, dws_ref, dbs_ref, ds_ref):
        @pl.when((pl.program_id(0) == 0) & (pl.program_id(1) == 0))
        def _():
            dws_ref[...] = jnp.zeros_like(dws_ref)
            dbs_ref[...] = jnp.zeros_like(dbs_ref)
            ds_ref[...] = jnp.zeros_like(ds_ref)

        dcu, dcv, dcz, dws, dbs, dg, db = _c_bwd(dy_ref[0], cu_ref[0], cv_ref[0], cz_ref[0], g_ref[...], b_ref[...],
                                                 ws_ref[...], bf_ref[...])
        dc_ref[0] = jnp.concatenate([dcu, dcv, dcz], axis=1).astype(BF16)
        for gi in range(SG_G):
            dws_ref[gi] += dws[gi]
        dbs_ref[...] += dbs
        ds_ref[...] += _rows8(dg, db)

    tile = lambda b, i: (b, i, 0)
    return _pcall(
        body, name=name, grid=(bsz, t // TB), in_specs=[pl.BlockSpec((1, TB, SG_W), tile)] + _c_specs(),
        out_specs=[pl.BlockSpec((1, TB, 3 * SG_W), tile), pl.BlockSpec((SG_G, SG_C, SG_C), lambda b, i: (0, 0, 0)),
                   pl.BlockSpec((SG_C, SG_W), lambda b, i: (0, 0)), pl.BlockSpec((8, SG_W), lambda b, i: (0, 0))],
        out_shape=[_sds((bsz, t, 3 * SG_W), BF16), _sds((SG_G, SG_C, SG_C), F32), _sds((SG_C, SG_W), F32),
                   _sds((8, SG_W), F32)],
        compiler_params=_cparams(("arbitrary", "arbitrary")),
    )(dy, proj, proj, proj, g, b, ws, bias_full)


def _m_specs():
    g0 = G_OFF // D_MODEL
    y = [pl.BlockSpec((1, TB, n), lambda b, i: (b, i, 0)) for n in (CONV_W, D_MODEL, SG_W)]
    gates = [pl.BlockSpec((1, TB, D_MODEL), functools.partial(lambda b, i, c: (b, i, c), c=g0 + c)) for c in range(3)]
    return y + gates


def _w_specs(*shapes):
    return [pl.BlockSpec(s, lambda b, i: (0, 0)) for s in shapes]


def _m_fwd_call(ya, yb, yc, proj, x, wa, wb, wc, wo, name):
    bsz, t, _ = x.shape

    def body(ya_ref, yb_ref, yc_ref, g0_ref, g1_ref, g2_ref, x_ref, wa_ref, wb_ref, wc_ref, wo_ref, o_ref):
        merged, _ = _m_fwd(ya_ref[0], yb_ref[0], yc_ref[0], g0_ref[0], g1_ref[0], g2_ref[0], wa_ref[...], wb_ref[...],
                           wc_ref[...])
        o_ref[0] = x_ref[0] + _nn(merged, wo_ref[...])

    tile = pl.BlockSpec((1, TB, D_MODEL), lambda b, i: (b, i, 0))
    return _pcall(
        body, name=name, grid=(bsz, t // TB),
        in_specs=_m_specs() + [tile] + _w_specs(wa.shape, wb.shape, wc.shape, wo.shape),
        out_specs=tile, out_shape=_sds(x.shape, F32), compiler_params=_cparams(("parallel", "parallel")),
    )(ya, yb, yc, proj, proj, proj, x, wa, wb, wc, wo)


def _m_bwd_call(dout, ya, yb, yc, proj, wa, wb, wc, wo, name):
    bsz, t, _ = dout.shape

    def body(do_ref, ya_ref, yb_ref, yc_ref, g0_ref, g1_ref, g2_ref, wa_ref, wb_ref, wc_ref, wo_ref, dya_ref, dyb_ref,
             dyc_ref, dg_ref, mg_ref, dp_ref):
        dys, dgs, merged, dps = _m_bwd(do_ref[0], ya_ref[0], yb_ref[0], yc_ref[0], g0_ref[0], g1_ref[0], g2_ref[0],
                                       wa_ref[...], wb_ref[...], wc_ref[...], wo_ref[...])
        dya_ref[0], dyb_ref[0], dyc_ref[0] = dys
        dg_ref[0] = jnp.concatenate(dgs, axis=1).astype(BF16)
        mg_ref[0] = merged.astype(BF16)
        dp_ref[0] = jnp.concatenate(dps, axis=1).astype(BF16)

    tile = lambda n: pl.BlockSpec((1, TB, n), lambda b, i: (b, i, 0))
    widths = (CONV_W, D_MODEL, SG_W, 3 * D_MODEL, D_MODEL, 3 * D_MODEL)
    dts = (F32, F32, F32, BF16, BF16, BF16)
    return _pcall(
        body, name=name, grid=(bsz, t // TB),
        in_specs=[tile(D_MODEL)] + _m_specs() + _w_specs(wa.shape, wb.shape, wc.shape, wo.shape),
        out_specs=[tile(n) for n in widths], out_shape=[_sds((bsz, t, n), d) for n, d in zip(widths, dts)],
        compiler_params=_cparams(("parallel", "parallel")),
    )(dout, ya, yb, yc, proj, proj, proj, wa, wb, wc, wo)


def _m_wgrad_call(ya, yb, yc, merged, dps, dout, name):
    bsz, t, _ = dout.shape

    def body(ya_ref, yb_ref, yc_ref, mg_ref, dp_ref, do_ref, dwa_ref, dwb_ref, dwc_ref, dwo_ref):
        @pl.when((pl.program_id(0) == 0) & (pl.program_id(1) == 0))
        def _():
            for r in (dwa_ref, dwb_ref, dwc_ref, dwo_ref):
                r[...] = jnp.zeros_like(r)

        dp = dp_ref[0]
        dwa_ref[...] += _tn(ya_ref[0], dp[:, :D_MODEL])
        dwb_ref[...] += _tn(yb_ref[0], dp[:, D_MODEL:2 * D_MODEL])
        dwc_ref[...] += _tn(yc_ref[0], dp[:, 2 * D_MODEL:])
        dwo_ref[...] += _tn(mg_ref[0], do_ref[0])

    tile = lambda n: pl.BlockSpec((1, TB, n), lambda b, i: (b, i, 0))
    shapes = ((CONV_W, D_MODEL), (D_MODEL, D_MODEL), (SG_W, D_MODEL), (D_MODEL, D_MODEL))
    return _pcall(
        body, name=name, grid=(bsz, t // TB),
        in_specs=[tile(CONV_W), tile(D_MODEL), tile(SG_W), tile(D_MODEL), tile(3 * D_MODEL), tile(D_MODEL)],
        out_specs=_w_specs(*shapes), out_shape=[_sds(s, F32) for s in shapes],
        compiler_params=_cparams(("arbitrary", "arbitrary")),
    )(ya, yb, yc, merged, dps, dout)


def _dn_specs(nc, rev):
    ti = (lambda i: nc - 1 - i) if rev else (lambda i: i)
    return [pl.BlockSpec((1, CHUNK, B_W), lambda b, i: (b, ti(i), 0)),
            pl.BlockSpec((1, CHUNK, DK), lambda b, i: (b, ti(i), BA_OFF // DK)),
            pl.BlockSpec((SHORT_K, 3 * D_MODEL), lambda b, i: (0, 0))] + [pl.BlockSpec((1, DK), lambda b, i: (0, 0))] * 3


def _dn_saved_specs(nc, rev):
    ti = (lambda i: nc - 1 - i) if rev else (lambda i: i)
    return [pl.BlockSpec((1, CHUNK, D_MODEL), lambda b, i: (b, ti(i), 0)),
            pl.BlockSpec((1, CHUNK, 3 * D_MODEL), lambda b, i: (b, ti(i), 0)),
            pl.BlockSpec((1, HEADS, 1, DK, DK), lambda b, i: (b, 0, ti(i), 0, 0)),
            pl.BlockSpec((1, 1, CHUNK, HEADS * CHUNK), lambda b, i: (b, ti(i), 0, 0)),
            pl.BlockSpec((1, CHUNK, 2 * D_MODEL), lambda b, i: (b, ti(i), 0)),
            pl.BlockSpec((1, CHUNK, D_MODEL), lambda b, i: (b, ti(i), 0))]


def _dn_fwd_call(proj, wconv, alog, dtb, og, name):
    bsz, t, _ = proj.shape
    nc = t // CHUNK

    def body(x_ref, ba_ref, w_ref, al_ref, dt_ref, og_ref, halo_ref, y_ref, o_ref, pre_ref, st_ref, t_ref, uw_ref, vn_ref,
             s_scr):
        i = pl.program_id(1)

        @pl.when(i == 0)
        def _():
            s_scr[...] = jnp.zeros_like(s_scr)

        x = x_ref[0]
        keep = jnp.where(i > 0, 1.0, 0.0)
        pre = _short_conv(x[:, :3 * D_MODEL], halo_ref[0] * keep, w_ref[...])
        pre_ref[0] = pre
        items, _, _, _ = _dn_heads(pre, ba_ref[0], al_ref[...], dt_ref[...])
        ss = [s_scr[h] for h in range(HEADS)]
        for h in range(HEADS):
            st_ref[0, h, 0] = ss[h]
        outs, ss, vnew = _dn_fwd_chunk(items, ss)
        for h in range(HEADS):
            s_scr[h] = ss[h]
        og_v = og_ref[...]
        z = x[:, 3 * D_MODEL:]
        sz = _sig(z)
        ys = [_dn_out(outs[h], z[:, h * DK:(h + 1) * DK], sz[:, h * DK:(h + 1) * DK], og_v)[0] for h in range(HEADS)]
        o_ref[0] = jnp.concatenate(outs, axis=1)
        y_ref[0] = jnp.concatenate(ys, axis=1).astype(BF16)
        t_ref[0, 0] = jnp.concatenate([d["t"] for d in items], axis=1).astype(BF16)
        uw_ref[0] = jnp.concatenate([d["uw"] for d in items], axis=1).astype(BF16)
        vn_ref[0] = jnp.concatenate(vnew, axis=1).astype(BF16)

    halo = pl.BlockSpec((1, HALO_B, 3 * D_MODEL), lambda b, i: (b, jnp.maximum(i * (CHUNK // HALO_B) - 1, 0), 0))
    return _pcall(
        body, name=name, grid=(bsz, nc), in_specs=_dn_specs(nc, False) + [halo],
        out_specs=[pl.BlockSpec((1, CHUNK, D_MODEL), lambda b, i: (b, i, 0))] + _dn_saved_specs(nc, False),
        out_shape=[_sds((bsz, t, D_MODEL), BF16), _sds((bsz, t, D_MODEL), F32), _sds((bsz, t, 3 * D_MODEL), F32),
                   _sds((bsz, HEADS, nc, DK, DK), F32), _sds((bsz, nc, CHUNK, HEADS * CHUNK), BF16),
                   _sds((bsz, t, 2 * D_MODEL), BF16), _sds((bsz, t, D_MODEL), BF16)],
        scratch_shapes=[pltpu.VMEM((HEADS, DK, DK), F32)], compiler_params=_cparams(("arbitrary", "arbitrary")),
    )(proj, proj, wconv, alog, dtb, og, proj)


def _dn_bwd_call(dy, saved, proj, wconv, alog, dtb, og, name):
    bsz, t, _ = proj.shape
    nc = t // CHUNK

    def body(dy_ref, o_ref, pre_ref, st_ref, t_ref, uw_ref, vn_ref, x_ref, ba_ref, w_ref, al_ref, dt_ref, og_ref,
             dx_ref, dba_ref, dwc_ref, dsm_ref, ds_scr, dpre_scr):
        ip = pl.program_id(1)

        @pl.when((pl.program_id(0) == 0) & (ip == 0))
        def _():
            dwc_ref[...] = jnp.zeros_like(dwc_ref)
            dsm_ref[...] = jnp.zeros_like(dsm_ref)

        @pl.when(ip == 0)
        def _():
            ds_scr[...] = jnp.zeros_like(ds_scr)
            dpre_scr[...] = jnp.zeros_like(dpre_scr)

        x, w, pre = x_ref[0], w_ref[...], pre_ref[0]
        items, sp, xg, ea8 = _dn_heads(pre, ba_ref[0], al_ref[...], dt_ref[...])
        og_v, dyv, ov, tv, uwv, vnv = og_ref[...], dy_ref[0], o_ref[0], t_ref[0, 0], uw_ref[0], vn_ref[0]
        z = x[:, 3 * D_MODEL:]
        sz = _sig(z)
        dsz = sz * (1.0 + z * (1.0 - sz))
        dzs, dog = [], jnp.zeros((1, DK), F32)
        for h, d in enumerate(items):
            hs = slice(h * DK, (h + 1) * DK)
            _, xh, r = _dn_out(ov[:, hs], z[:, hs], sz[:, hs], og_v)
            dzs.append(dyv[:, hs] * xh * og_v * dsz[:, hs])
            don = dyv[:, hs] * (z[:, hs] * sz[:, hs])
            dog = dog + _sum0(don * xh)
            d.update(do=_rms_bwd(don * og_v, xh, r), t=tv[:, h * CHUNK:(h + 1) * CHUNK],
                     uw=uwv[:, 2 * h * DK:2 * (h + 1) * DK], vnew=vnv[:, hs])
        grads, dss = _dn_bwd_chunk(items, [st_ref[0, h, 0] for h in range(HEADS)], [ds_scr[h] for h in range(HEADS)])
        for h in range(HEADS):
            ds_scr[h] = dss[h]
        dqc, dkc, dvc = [], [], []
        lanes = lax.broadcasted_iota(jnp.int32, (CHUNK, DK), 1)
        lane8 = lax.broadcasted_iota(jnp.int32, (8, DK), 1)
        row8 = lax.broadcasted_iota(jnp.int32, (8, DK), 0)
        dgdx = -ea8 * _sig(xg)
        dba = jnp.zeros((CHUNK, DK), F32)
        dsm = jnp.where(row8 == 2, dog, 0.0)
        for h, (d, (dq, dk, dv, dbeta, dg)) in enumerate(zip(items, grads)):
            qc, kc, nq, nk = d["qc"], d["kc"], d["nq"], d["nk"]
            dqc.append((DK ** -0.5) * nq * (dq - qc * (nq * nq * _sum1(dq * qc))))
            dkc.append(nk * (dk - kc * (nk * nk * _sum1(dk * kc))))
            dvc.append(dv)
            dal = dg * dgdx[:, HEADS + h:HEADS + h + 1]
            dba = dba + jnp.where(lanes == h, dbeta * d["b"] * (1.0 - d["b"]), 0.0) + jnp.where(lanes == HEADS + h, dal, 0.0)
            dsm = dsm + (jnp.where((row8 == 0) & (lane8 == h), _sum0(dg * d["g"]), 0.0)
                         + jnp.where((row8 == 1) & (lane8 == h), _sum0(dal), 0.0))
        dpre = jnp.concatenate(dqc + dkc + dvc, axis=1) * (sp * (1.0 + pre * (1.0 - sp)))
        dext = jnp.concatenate([dpre, dpre_scr[...]], axis=0)
        raw = x[:, :3 * D_MODEL]
        draw = dext[3:3 + CHUNK] * w[0:1]
        rows = [_sum0(dext[3:3 + CHUNK] * raw)]
        for j in range(1, SHORT_K):
            shifted = dext[3 - j:3 - j + CHUNK]
            draw = draw + shifted * w[j:j + 1]
            rows.append(_sum0(shifted * raw))
        dpre_scr[...] = dpre[:HALO_B]
        dx_ref[0] = jnp.concatenate([draw] + dzs, axis=1).astype(BF16)
        dba_ref[0] = dba.astype(BF16)
        dwc_ref[...] += _rows8(*rows)
        dsm_ref[...] += dsm

    rrow = lambda n: pl.BlockSpec((1, CHUNK, n), lambda b, i: (b, nc - 1 - i, 0))
    return _pcall(
        body, name=name, grid=(bsz, nc), in_specs=[rrow(D_MODEL)] + _dn_saved_specs(nc, True) + _dn_specs(nc, True),
        out_specs=[rrow(B_W), rrow(DK), pl.BlockSpec((8, 3 * D_MODEL), lambda b, i: (0, 0)),
                   pl.BlockSpec((8, DK), lambda b, i: (0, 0))],
        out_shape=[_sds((bsz, t, B_W), BF16), _sds((bsz, t, DK), BF16), _sds((8, 3 * D_MODEL), F32), _sds((8, DK), F32)],
        scratch_shapes=[pltpu.VMEM((HEADS, DK, DK), F32), pltpu.VMEM((HALO_B, 3 * D_MODEL), F32)],
        compiler_params=_cparams(("arbitrary", "arbitrary")),
    )(dy, *saved, proj, proj, wconv, alog, dtb, og)


def _adamw_call(w, g, m, v, name):
    shape = w.shape
    cols = shape[-1]
    rows = math.prod(shape[:-1]) if len(shape) > 1 else 1
    tr = 128 if rows % 128 == 0 else rows
    c1, c2 = 1.0 - ADAM_B1 ** ADAM_STEP, 1.0 - ADAM_B2 ** ADAM_STEP

    def body(w_ref, g_ref, m_ref, v_ref, d_ref, nm_ref, nv_ref):
        gv = g_ref[...]
        nm = ADAM_B1 * m_ref[...] + (1.0 - ADAM_B1) * gv
        nv = ADAM_B2 * v_ref[...] + (1.0 - ADAM_B2) * (gv * gv)
        d_ref[...] = -ADAM_LR * ((nm / c1) / (jnp.sqrt(nv / c2) + ADAM_EPS) + ADAM_WD * w_ref[...])
        nm_ref[...] = nm
        nv_ref[...] = nv

    blk = pl.BlockSpec((tr, cols), lambda i: (i, 0))
    outs = _pcall(
        body, name=name, grid=(rows // tr,), in_specs=[blk] * 4, out_specs=[blk] * 3,
        out_shape=[_sds((rows, cols), F32)] * 3, compiler_params=_cparams(("parallel",)),
    )(*(a.reshape(rows, cols) for a in (w, g, m, v)))
    return tuple(o.reshape(shape) for o in outs)


def _row_tile(rows, cap=512):
    for tr in range(cap, 15, -16):
        if rows % tr == 0:
            return tr
    return rows


def _add_pair_call(by_chip, recv, where, name):
    _, n, rows, cols = by_chip.shape
    tr = _row_tile(rows, 256)

    def body(where_ref, a_ref, b_ref, sb_ref, own_ref):
        s = a_ref[0, 0] + b_ref[0]
        sb_ref[0] = s.astype(BF16)

        @pl.when(pl.program_id(1) == where_ref[1])
        def _():
            own_ref[...] = s

    grid_spec = pltpu.PrefetchScalarGridSpec(
        num_scalar_prefetch=1, grid=(rows // tr, n),
        in_specs=[pl.BlockSpec((1, 1, tr, cols), lambda i, j, wr: (wr[0], j, i, 0)),
                  pl.BlockSpec((1, tr, cols), lambda i, j, wr: (j, i, 0))],
        out_specs=[pl.BlockSpec((1, tr, cols), lambda i, j, wr: (j, i, 0)),
                   pl.BlockSpec((tr, cols), lambda i, j, wr: (i, 0))])
    return _pcall(
        body, name=name, grid_spec=grid_spec, out_shape=[_sds((n, rows, cols), BF16), _sds((rows, cols), F32)],
        compiler_params=_cparams(("parallel", "arbitrary")),
    )(where, by_chip, recv)


def _add_recv_call(own, recv, name):
    rows, cols = own.shape
    tr = _row_tile(rows, 256)

    def body(o_ref, r_ref, s_ref):
        s_ref[...] = ((o_ref[...] + r_ref[0].astype(F32)) + r_ref[1].astype(F32)) + r_ref[2].astype(F32)

    return _pcall(
        body, name=name, grid=(rows // tr,),
        in_specs=[pl.BlockSpec((tr, cols), lambda i: (i, 0)), pl.BlockSpec((3, tr, cols), lambda i: (0, i, 0))],
        out_specs=pl.BlockSpec((tr, cols), lambda i: (i, 0)), out_shape=_sds(own.shape, F32),
        compiler_params=_cparams(("parallel",)),
    )(own, recv)


def _axes():
    return lax.axis_index("x"), lax.axis_index("y"), lax.axis_index("c")


def _chip_peers(x, y):
    return [(x, 1 - y), (1 - x, y), (1 - x, 1 - y)]


_ANY = pl.BlockSpec(memory_space=pl.ANY)
_VMEM = pl.BlockSpec(memory_space=pltpu.VMEM)


def _remote(src, dst, send_sems, recv_sems, k, dev):
    return pltpu.make_async_remote_copy(src_ref=src, dst_ref=dst, send_sem=send_sems.at[k], recv_sem=recv_sems.at[k],
                                        device_id=dev, device_id_type=MESH)


def _gather_weights_call(shards):
    n = len(shards)

    def body(*refs):
        x_refs, o_refs, (send_sems, recv_sems, local_sems) = refs[:n], refs[n:2 * n], refs[2 * n:]
        x, y, c = _axes()
        chip, sib, peers = 2 * x + y, (x, y, 1 - c), _chip_peers(x, y)
        pchip = [2 * px + py for px, py in peers]
        mine = [pltpu.make_async_copy(x_refs[a], o_refs[a].at[chip], local_sems.at[a]) for a in range(n)]
        for cp in mine:
            cp.start()
        first = [_remote(x_refs[a].at[c], o_refs[a].at[chip, c], send_sems, recv_sems, 6 * a + k, (*peers[k], c))
                 for k in range(3) for a in range(n)]
        for cp in first:
            cp.start()
        passed = []
        for k in range(3):
            for a in range(n):
                land = o_refs[a].at[pchip[k], c]
                _remote(land, land, send_sems, recv_sems, 6 * a + k, (*peers[k], c)).wait_recv()
                passed.append(_remote(land, land, send_sems, recv_sems, 6 * a + 3 + k, sib))
                passed[-1].start()
        for k in range(3):
            for a in range(n):
                land = o_refs[a].at[pchip[k], 1 - c]
                _remote(land, land, send_sems, recv_sems, 6 * a + 3 + k, sib).wait_recv()
        for cp in first + passed:
            cp.wait_send()
        for cp in mine:
            cp.wait()

    return _pcall(
        body, name="gather_weights", in_specs=[_ANY] * n, out_specs=[_ANY] * n,
        out_shape=[_sds((4,) + s.shape, s.dtype) for s in shards],
        scratch_shapes=[pltpu.SemaphoreType.DMA((6 * n,)), pltpu.SemaphoreType.DMA((6 * n,)),
                        pltpu.SemaphoreType.DMA((n,))],
    )(*shards)


def _pair_partials_call(by_chip):
    n = len(by_chip)

    def body(*refs):
        v_refs, o_refs, (send_sems, recv_sems) = refs[:n], refs[n:2 * n], refs[2 * n:]
        x, y, c = _axes()
        cps = [_remote(v_refs[a].at[1 - c], o_refs[a], send_sems, recv_sems, a, (x, y, 1 - c)) for a in range(n)]
        for cp in cps:
            cp.start()
        for cp in cps:
            cp.wait()

    return _pcall(
        body, name="pair_partials", in_specs=[_ANY] * n, out_specs=[_ANY] * n,
        out_shape=[_sds(v.shape[1:], v.dtype) for v in by_chip],
        scratch_shapes=[pltpu.SemaphoreType.DMA((n,)), pltpu.SemaphoreType.DMA((n,))],
    )(*by_chip)


def _scatter_partials_call(parts):
    n = len(parts)

    def body(*refs):
        v_refs, o_refs, (send_sems, recv_sems) = refs[:n], refs[n:2 * n], refs[2 * n:]
        x, y, c = _axes()
        peers = _chip_peers(x, y)
        cps = [_remote(v_refs[a].at[2 * peers[k][0] + peers[k][1]], o_refs[a].at[k], send_sems, recv_sems, 3 * a + k,
                       (*peers[k], c)) for k in range(3) for a in range(n)]
        for cp in cps:
            cp.start()
        for cp in cps:
            cp.wait()

    return _pcall(
        body, name="scatter_partials", in_specs=[_ANY] * n, out_specs=[_ANY] * n,
        out_shape=[_sds((3,) + v.shape[1:], v.dtype) for v in parts],
        scratch_shapes=[pltpu.SemaphoreType.DMA((3 * n,)), pltpu.SemaphoreType.DMA((3 * n,))],
    )(*parts)


def _pair_result_call(fins):
    n = len(fins)

    def body(*refs):
        v_refs, o_refs, (send_sems, recv_sems, local_sems) = refs[:n], refs[n:2 * n], refs[2 * n:]
        x, y, c = _axes()
        mine = [pltpu.make_async_copy(v_refs[a], o_refs[a].at[c], local_sems.at[a]) for a in range(n)]
        cps = [_remote(v_refs[a], o_refs[a].at[c], send_sems, recv_sems, a, (x, y, 1 - c)) for a in range(n)]
        for cp in mine + cps:
            cp.start()
        for a in range(n):
            cps[a].wait_send()
            _remote(v_refs[a], o_refs[a].at[1 - c], send_sems, recv_sems, a, (x, y, 1 - c)).wait_recv()
        for cp in mine:
            cp.wait()

    return _pcall(
        body, name="pair_result", in_specs=[_ANY] * n, out_specs=[_ANY] * n,
        out_shape=[_sds((2,) + v.shape, v.dtype) for v in fins],
        scratch_shapes=[pltpu.SemaphoreType.DMA((n,)), pltpu.SemaphoreType.DMA((n,)), pltpu.SemaphoreType.DMA((n,))],
    )(*fins)


def _allreduce_small_call(v):
    rows, cols = v.shape

    def body(v_ref, o_ref, buf, send_sems, recv_sems):
        x, y, c = _axes()
        me = 4 * x + 2 * y + c
        buf[0] = v_ref[...]
        cps = []
        for m in range(1, 8):
            dev = (1 - x if m & 4 else x, 1 - y if m & 2 else y, 1 - c if m & 1 else c)
            cps.append(_remote(v_ref, buf.at[m], send_sems, recv_sems, m - 1, dev))
        for cp in cps:
            cp.start()
        for cp in cps:
            cp.wait()
        acc = buf[me]
        for d in range(1, 8):
            acc = acc + buf[lax.bitwise_xor(me, d)]
        o_ref[...] = acc

    return _pcall(
        body, name="allreduce_small", in_specs=[_VMEM], out_specs=_VMEM, out_shape=_sds(v.shape, F32),
        scratch_shapes=[pltpu.VMEM((8, rows, cols), F32), pltpu.SemaphoreType.DMA((7,)), pltpu.SemaphoreType.DMA((7,))],
        compiler_params=_cparams(),
    )(v)


def _pack_rows(arrays, dtype, total_rows=None):
    parts = []
    for a in arrays:
        flat = a.astype(dtype).reshape(-1)
        parts.append(jnp.pad(flat, (0, -flat.shape[0] % D_MODEL)).reshape(-1, D_MODEL))
    out = jnp.concatenate(parts, axis=0)
    total_rows = total_rows or out.shape[0] + (-out.shape[0] % 8)
    return jnp.pad(out, ((0, total_rows - out.shape[0]), (0, 0)))


def _unpack_rows(packed, shapes):
    out, r = [], 0
    for s in shapes:
        n = math.prod(s)
        nr = -(-n // D_MODEL)
        out.append(packed[r:r + nr].reshape(-1)[:n].reshape(s))
        r += nr
    return out


def _to_r(w):
    pad = jnp.zeros(w.shape[:-1] + (N_R - N_ORIG,), w.dtype)
    return jnp.concatenate([w[..., A_ORIG_W:BA_ORIG], w[..., :A_ORIG_W], w[..., BA_ORIG + 2 * HEADS:],
                            w[..., BA_ORIG:BA_ORIG + 2 * HEADS], pad], axis=-1)


def _from_r(w):
    return jnp.concatenate([w[..., A_OFF:C_OFF], w[..., :A_OFF], w[..., BA_OFF:BA_OFF + 2 * HEADS],
                            w[..., C_OFF:BA_OFF]], axis=-1)


_BIG = ("w_in", "a_proj", "b_proj", "c_proj", "w_out")
_SHARD_AXIS = {"w_in": 2, "a_proj": 2, "b_proj": 1, "c_proj": 2, "w_out": 1, "a_dw": 2, "b_conv": 2}
_BIG_AXIS = _SHARD_AXIS


def _join_chips(g, axis):
    g = jnp.moveaxis(g, 0, axis)
    return g.reshape(g.shape[:axis] + (4 * g.shape[axis + 1],) + g.shape[axis + 2:])


def _split_chips(a, axis):
    n = a.shape[axis] // 4
    return jnp.moveaxis(a.reshape(a.shape[:axis] + (4, n) + a.shape[axis + 1:]), axis, 1)
_ORDER = ("norm_g", "w_in", "a_dw", "a_dw_b", "a_ln_g", "a_ln_b", "a_proj", "b_conv", "b_a_log", "b_dt_bias",
          "b_onorm_g", "b_proj", "c_ln_g", "c_ln_b", "c_ws", "c_bs", "c_proj", "w_out", "final_g")


def _local_step(x, tgt, w):
    bsz, t, _ = x.shape
    m = bsz * t
    depth = w["norm_g"].shape[0]
    row = lambda v: v.reshape(1, -1)
    w_r = _to_r(w["w_in"])
    saved, xl = [], x
    for l in range(depth):
        n = f"l{l}_"
        par = dict(
            adw=jnp.pad(w["a_dw"][l], ((0, 1), (0, 0))), adb=row(w["a_dw_b"][l]), alg=row(w["a_ln_g"][l]),
            alb=row(w["a_ln_b"][l]), bconv=w["b_conv"][l],
            alog=jnp.pad(row(w["b_a_log"][l]), ((0, 0), (HEADS, DK - 2 * HEADS))),
            dtb=jnp.pad(row(w["b_dt_bias"][l]), ((0, 0), (HEADS, DK - 2 * HEADS))), og=row(w["b_onorm_g"][l]),
            clg=row(w["c_ln_g"][l]), clb=row(w["c_ln_b"][l]), cws=w["c_ws"][l],
            cbias=jnp.repeat(w["c_bs"][l].T, SG_C, axis=1), ng=row(w["norm_g"][l]),
            wa=w["a_proj"][l], wb=w["b_proj"][l], wc=w["c_proj"][l], wo=w["w_out"][l], wr=w_r[l])
        h = _rms_fwd_call(xl.reshape(m, D_MODEL), par["ng"], n + "norm")
        proj = _matmul(h, par["wr"], F32, 1024, 1152, 1024, n + "in_proj").reshape(bsz, t, N_R)
        ya = _a_fwd_call(proj, par["adw"], par["adb"], par["alg"], par["alb"], n + "conv_fwd")
        yb, *dn_saved = _dn_fwd_call(proj, par["bconv"], par["alog"], par["dtb"], par["og"], n + "delta_fwd")
        yc = _c_fwd_call(proj, par["clg"], par["clb"], par["cws"], par["cbias"], n + "gmlp_fwd")
        x_next = _m_fwd_call(ya, yb, yc, proj, xl, par["wa"], par["wb"], par["wc"], par["wo"], n + "merge_fwd")
        saved.append((par, xl, h, proj, ya, yb, yc, dn_saved))
        xl = x_next
    dout, dfg, loss = _loss_call(xl.reshape(m, D_MODEL), tgt.reshape(m, D_MODEL), row(w["final_g"]))
    dout = dout.reshape(bsz, t, D_MODEL)
    g = {k: [None] * depth for k in _ORDER if k != "final_g"}
    for l in reversed(range(depth)):
        n = f"l{l}_"
        par, xl, h, proj, ya, yb, yc, dn_saved = saved[l]
        dya, dyb, dyc, dgate, merged, dps = _m_bwd_call(dout, ya, yb, yc, proj, par["wa"], par["wb"], par["wc"],
                                                        par["wo"], n + "merge_bwd")
        g["a_proj"][l], g["b_proj"][l], g["c_proj"][l], g["w_out"][l] = _m_wgrad_call(ya, yb, yc, merged, dps, dout,
                                                                                      n + "merge_wgrad")
        da, dadw, dasm = _a_bwd_call(dya, proj, par["adw"], par["adb"], par["alg"], par["alb"], n + "conv_bwd")
        db, dba, dbconv, dbsm = _dn_bwd_call(dyb, dn_saved, proj, par["bconv"], par["alog"], par["dtb"], par["og"],
                                             n + "delta_bwd")
        dc, dcws, dcbs, dcsm = _c_bwd_call(dyc, proj, par["clg"], par["clb"], par["cws"], par["cbias"], n + "gmlp_bwd")
        dproj = jnp.concatenate([db, da, dc, dgate, dba], axis=-1).reshape(m, N_R)
        dh = _matmul(dproj, par["wr"].T, F32, 1024, 1024, 1152, n + "in_proj_dx")
        g["w_in"][l] = _from_r(_matmul(h.T, dproj, F32, 1024, 1152, 1024, n + "in_proj_dw"))
        dx, dng = _rms_bwd_call(xl.reshape(m, D_MODEL), dh, par["ng"], dout.reshape(m, D_MODEL), n + "norm_bwd")
        dout = dx.reshape(bsz, t, D_MODEL)
        g["norm_g"][l] = dng[0]
        g["a_dw"][l], g["a_dw_b"][l], g["a_ln_g"][l], g["a_ln_b"][l] = dadw[:CONV_K], dasm[0], dasm[1], dasm[2]
        g["b_conv"][l], g["b_a_log"][l], g["b_dt_bias"][l] = dbconv[:SHORT_K], dbsm[0, :HEADS], dbsm[1, :HEADS]
        g["b_onorm_g"][l] = dbsm[2]
        g["c_ln_g"][l], g["c_ln_b"][l], g["c_ws"][l] = dcsm[0], dcsm[1], dcws
        g["c_bs"][l] = dcbs.reshape(SG_C, SG_G, SG_C).sum(-1).T
    grads = {k: jnp.stack(v) for k, v in g.items()}
    grads["final_g"] = dfg[0]
    return loss[0, 0], dout, grads


def kernel(x, norm_g, w_in, a_dw, a_dw_b, a_ln_g, a_ln_b, a_proj, b_conv, b_a_log, b_dt_bias, b_onorm_g, b_proj, c_ln_g, c_ln_b, c_ws, c_bs, c_proj, w_out, final_g, loss_target, m_norm_g, m_w_in, m_a_dw, m_a_dw_b, m_a_ln_g, m_a_ln_b, m_a_proj, m_b_conv, m_b_a_log, m_b_dt_bias, m_b_onorm_g, m_b_proj, m_c_ln_g, m_c_ln_b, m_c_ws, m_c_bs, m_c_proj, m_w_out, m_final_g, v_norm_g, v_w_in, v_a_dw, v_a_dw_b, v_a_ln_g, v_a_ln_b, v_a_proj, v_b_conv, v_b_a_log, v_b_dt_bias, v_b_onorm_g, v_b_proj, v_c_ln_g, v_c_ln_b, v_c_ws, v_c_bs, v_c_proj, v_w_out, v_final_g):
    given = dict(locals())
    ws = {k: given[k] for k in _ORDER}
    xi, yi, ci = _axes()
    chip = 2 * xi + yi
    where = jnp.stack([ci, chip]).astype(jnp.int32)

    sharded = _BIG + ("a_dw", "b_conv")
    gathered = _gather_weights_call([ws[k].astype(BF16) for k in _BIG] + [a_dw, b_conv])
    full = dict(ws)
    for k, g in zip(sharded, gathered):
        full[k] = _join_chips(g, _SHARD_AXIS[k])

    loss, grad_x, grads = _local_step(x, loss_target, full)

    by_chip = [_split_chips(grads[k], _BIG_AXIS[k]) for k in _BIG]
    theirs = _pair_partials_call(by_chip)
    sums = [_add_pair_call(b, r, where, "pair_sum_" + k) for k, b, r in zip(_BIG, by_chip, theirs)]
    got = _scatter_partials_call([s[0] for s in sums])
    fins = [_add_recv_call(s[1], r, "chip_sum_" + k) for k, s, r in zip(_BIG, sums, got)]
    out_g = dict(zip(_BIG, _pair_result_call(fins)))

    rest = [k for k in _ORDER if k not in _BIG]
    rest_shapes = [grads[k].shape for k in rest] + [(1,)]
    summed = _unpack_rows(_allreduce_small_call(_pack_rows([grads[k] for k in rest] + [loss.reshape(1)], F32)),
                          rest_shapes)
    out_g.update(zip(rest, summed[:-1]))
    out_g["a_dw"] = lax.dynamic_slice_in_dim(out_g["a_dw"], chip * a_dw.shape[2], a_dw.shape[2], axis=2)
    out_g["b_conv"] = lax.dynamic_slice_in_dim(out_g["b_conv"], chip * b_conv.shape[2], b_conv.shape[2], axis=2)

    upd = {k: _adamw_call(ws[k], out_g[k], given["m_" + k], given["v_" + k], "adamw_" + k) for k in _ORDER}
    return (summed[-1][0], grad_x, *[out_g[k] for k in _ORDER], *[upd[k][0] for k in _ORDER],
            *[upd[k][1] for k in _ORDER], *[upd[k][2] for k in _ORDER])
```

```python
import functools
import math

import jax
import jax.numpy as jnp
from jax import lax
from jax.experimental import pallas as pl
from jax.experimental.pallas import tpu as pltpu

F32 = jnp.float32
BF16 = jnp.bfloat16
MESH = pl.DeviceIdType.MESH

D_MODEL = 1024
CONV_W = 512
CONV_K = 31
HEADS = 8
DK = 128
SHORT_K = 4
CHUNK = 64
SG_W = 512
SG_G = 4
SG_C = 128
EPS = 1e-6
N_ORIG = 10256
BA_ORIG = 5632
A_ORIG_W = 3 * CONV_W
B_W = 4 * D_MODEL
B_OFF, A_OFF, C_OFF, G_OFF, BA_OFF, N_R = 0, 4096, 5632, 7168, 10240, 10368
TB = 256
HALO_A = 32
HALO_B = 8
VMEM_LIMIT = 56 * 1024 * 1024
ADAM_LR, ADAM_B1, ADAM_B2, ADAM_EPS, ADAM_WD, ADAM_STEP = 0.001, 0.9, 0.999, 1e-08, 0.01, 10


def _pcall(body, **kw):
    return pl.pallas_call(body, **kw)


def _cparams(sem=None):
    kw = dict(vmem_limit_bytes=VMEM_LIMIT)
    if sem is not None:
        kw["dimension_semantics"] = sem
    return pltpu.CompilerParams(**kw)


def _sig(x):
    return jax.nn.sigmoid(x)


def _silu(x):
    return x * _sig(x)


def _dsilu(x):
    s = _sig(x)
    return s * (1.0 + x * (1.0 - s))


_GELU_C = math.sqrt(2.0 / math.pi)


def _gelu(x):
    return 0.5 * x * (1.0 + jnp.tanh(_GELU_C * (x + 0.044715 * x * x * x)))


def _dgelu(x):
    t = jnp.tanh(_GELU_C * (x + 0.044715 * x * x * x))
    return 0.5 * (1.0 + t) + 0.5 * x * (1.0 - t * t) * _GELU_C * (1.0 + 3 * 0.044715 * x * x)


def _softplus(x):
    return jnp.maximum(x, 0.0) + jnp.log1p(jnp.exp(-jnp.abs(x)))


def _dot(a, b, dims):
    return lax.dot_general(a.astype(BF16), b.astype(BF16), (dims, ((), ())), preferred_element_type=F32)


def _nn(a, b):
    return _dot(a, b, ((1,), (0,)))


def _nt(a, b):
    return _dot(a, b, ((1,), (1,)))


def _tn(a, b):
    return _dot(a, b, ((0,), (0,)))


def _tn_mxu(a, b):
    n = a.shape[1]
    eye = (lax.broadcasted_iota(jnp.int32, (n, n), 0) == lax.broadcasted_iota(jnp.int32, (n, n), 1)).astype(BF16)
    return _nn(_nt(eye, a), b)


def _mean(x):
    return jnp.mean(x, axis=-1, keepdims=True)


def _sum0(x):
    return jnp.sum(x, axis=0, keepdims=True)


def _sum1(x):
    return jnp.sum(x, axis=1, keepdims=True)


def _ln_fwd(x, g, b):
    xc = x - _mean(x)
    r = lax.rsqrt(_mean(xc * xc) + EPS)
    xh = xc * r
    return xh * g + b, xh, r


def _ln_bwd(dy, xh, r, g):
    dxh = dy * g
    return r * (dxh - _mean(dxh) - xh * _mean(dxh * xh)), _sum0(dy * xh), _sum0(dy)


def _rms_bwd(dxh, xh, r):
    return r * (dxh - xh * _mean(dxh * xh))


def _rows8(*rows):
    n = rows[0].shape[1]
    return jnp.concatenate(list(rows) + [jnp.zeros((8 - len(rows), n), F32)], axis=0)


def _windows(ext, tb):
    n = ext.shape[0] - 8
    shifted = {0: ext}

    def window(off):
        r = off % 8
        if r not in shifted:
            shifted[r] = ext[r:r + n]
        return shifted[r][off - r:off - r + tb]

    return window


def _a_fwd(val, glu, az, val_h, glu_h, w, bias, g, b):
    tb = val.shape[0]
    win = _windows(jnp.concatenate([val_h * _sig(glu_h), val * _sig(glu)], axis=0), tb)
    c = win(2) * w[0:1]
    for j in range(1, CONV_K):
        c = c + win(2 + j) * w[j:j + 1]
    c = c + bias
    ln, _, _ = _ln_fwd(c, g, b)
    return _silu(ln) * _silu(az), c


def _a_bwd(dy, val, glu, az, c, w, g, b, dc_next):
    tb = val.shape[0]
    ln, xh, r = _ln_fwd(c, g, b)
    sl, sz = _sig(ln), _sig(az)
    dln = dy * (az * sz) * (sl * (1.0 + ln * (1.0 - sl)))
    daz = dy * (ln * sl) * (sz * (1.0 + az * (1.0 - sz)))
    dc, dg, db = _ln_bwd(dln, xh, r, g)
    win = _windows(jnp.concatenate([dc, dc_next], axis=0), tb)
    sg = _sig(glu)
    a = val * sg
    da = win(30) * w[0:1]
    dw_rows = [_sum0(win(30) * a)]
    for j in range(1, CONV_K):
        shifted = win(30 - j)
        da = da + shifted * w[j:j + 1]
        dw_rows.append(_sum0(shifted * a))
    dw = jnp.concatenate(dw_rows + [jnp.zeros((1, CONV_W), F32)], axis=0)
    return da * sg, da * val * sg * (1.0 - sg), daz, dw, _sum0(dc), dg, db, dc[:HALO_A]


def _tril(ws):
    ii = lax.broadcasted_iota(jnp.int32, (SG_C, SG_C), 0)
    jj = lax.broadcasted_iota(jnp.int32, (SG_C, SG_C), 1)
    return [jnp.where(jj <= ii, ws[gi], 0.0) for gi in range(SG_G)], jj <= ii


def _c_mix(wt, vs, bias_full):
    tb = vs.shape[0]
    rows = []
    for n in range(tb // SG_C):
        blks = [_nn(wt[gi], vs[n * SG_C:(n + 1) * SG_C, gi * SG_C:(gi + 1) * SG_C]) for gi in range(SG_G)]
        rows.append(jnp.concatenate(blks, axis=1) + bias_full)
    return jnp.concatenate(rows, axis=0)


def _c_fwd(cu, cv, cz, g, b, ws, bias_full):
    wt, _ = _tril(ws)
    vs, xh, r = _ln_fwd(_gelu(cv), g, b)
    mixed = _c_mix(wt, vs, bias_full)
    return _gelu(cu) * mixed * _silu(cz), (wt, vs, xh, r, mixed)


def _c_bwd(dy, cu, cv, cz, g, b, ws, bias_full):
    tb = cu.shape[0]
    _, (wt, vs, xh, r, mixed) = _c_fwd(cu, cv, cz, g, b, ws, bias_full)
    _, low = _tril(ws)
    u, sz = _gelu(cu), _silu(cz)
    dcu = dy * mixed * sz * _dgelu(cu)
    dcz = dy * u * mixed * _dsilu(cz)
    dmixed = dy * u * sz
    dbs = jnp.zeros((SG_C, SG_W), F32)
    dws = [jnp.zeros((SG_C, SG_C), F32) for _ in range(SG_G)]
    rows = []
    for n in range(tb // SG_C):
        dm_n = dmixed[n * SG_C:(n + 1) * SG_C]
        dbs = dbs + dm_n
        blks = []
        for gi in range(SG_G):
            dm = dm_n[:, gi * SG_C:(gi + 1) * SG_C]
            dws[gi] = dws[gi] + _nt(dm, vs[n * SG_C:(n + 1) * SG_C, gi * SG_C:(gi + 1) * SG_C])
            blks.append(_tn(wt[gi], dm))
        rows.append(jnp.concatenate(blks, axis=1))
    dvs = jnp.concatenate(rows, axis=0)
    dgv, dg, db = _ln_bwd(dvs, xh, r, g)
    dws = [jnp.where(low, d, 0.0) for d in dws]
    return dcu, dgv * _dgelu(cv), dcz, dws, dbs, dg, db


def _m_fwd(ya, yb, yc, g0, g1, g2, wa, wb, wc):
    pa, pb, pc = _nn(ya, wa), _nn(yb, wb), _nn(yc, wc)
    s0, s1, s2 = _sig(g0), _sig(g1), _sig(g2)
    return s0 * pa + s1 * pb + s2 * pc, (pa, pb, pc, s0, s1, s2)


def _m_bwd(dout, ya, yb, yc, g0, g1, g2, wa, wb, wc, wo):
    merged, (pa, pb, pc, s0, s1, s2) = _m_fwd(ya, yb, yc, g0, g1, g2, wa, wb, wc)
    dm = _nt(dout, wo)
    dpa, dpb, dpc = dm * s0, dm * s1, dm * s2
    dgs = (dm * pa * s0 * (1.0 - s0), dm * pb * s1 * (1.0 - s1), dm * pc * s2 * (1.0 - s2))
    return (_nt(dpa, wa), _nt(dpb, wb), _nt(dpc, wc)), dgs, merged, (dpa, dpb, dpc)


def _chunk_masks(c):
    ii = lax.broadcasted_iota(jnp.int32, (c, c), 0)
    jj = lax.broadcasted_iota(jnp.int32, (c, c), 1)
    return ii, jj


def _dn_decay(items):
    ii, jj = _chunk_masks(CHUNK)
    incl, strict, eye = jj <= ii, jj < ii, ii == jj
    for d in items:
        g = d["g"]
        grow = _sum0(jnp.where(eye, g, 0.0))
        gc_col = _sum1(jnp.where(incl, grow, 0.0))
        gc_row = _sum0(jnp.where(ii <= jj, g, 0.0))
        gam_i = jnp.where(incl, jnp.exp(jnp.where(incl, gc_col - gc_row, 0.0)), 0.0)
        gl = _sum0(g)
        egc = jnp.exp(gc_col)
        d.update(gam_i=gam_i, gam_s=jnp.where(strict, gam_i, 0.0), egc=egc, ekd=jnp.exp(gl - gc_col), dl=jnp.exp(gl),
                 gdiff=gc_col - gc_row, qd=d["q"] * egc, rhs_w=d["k"] * (d["b"] * egc))
        d["kd"] = d["k"] * d["ekd"]
    return ii, jj, strict, eye


def _dn_solve(items, ii, jj, eye):
    off = ((ii >> 1) == (jj >> 1)) & ((ii & 1) != 0) & ((jj & 1) == 0)
    for d in items:
        a1 = jnp.where(off, d["a"], 0.0)
        d["t"] = jnp.where(eye, 1.0, 0.0) - a1
        d["m"] = d["a"] - _nn(a1, d["a"])
    b, sh = 2, 2
    while b < CHUNK:
        off = ((ii >> sh) == (jj >> sh)) & ((ii & b) != 0) & ((jj & b) == 0)
        for d in items:
            mo = jnp.where(off, d["m"], 0.0)
            if 2 * b < CHUNK:
                r = _nn(mo, jnp.concatenate([d["m"], d["t"]], axis=1))
                d["m"], d["t"] = d["m"] - r[:, :CHUNK], d["t"] - r[:, CHUNK:]
            else:
                d["t"] = d["t"] - _nn(mo, d["t"])
        b, sh = 2 * b, sh + 1


def _dn_fwd_chunk(items, ss):
    ii, jj, strict, eye = _dn_decay(items)
    for d in items:
        kq = _nt(jnp.concatenate([d["k"], d["q"]], axis=0), d["k"])
        d["a"] = d["b"] * kq[:CHUNK] * d["gam_s"]
        d["qk"] = kq[CHUNK:] * d["gam_i"]
    _dn_solve(items, ii, jj, eye)
    for d in items:
        d["uw"] = _nn(d["t"], jnp.concatenate([d["v"] * d["b"], d["rhs_w"]], axis=1))
    ws = [_nn(jnp.concatenate([d["uw"][:, DK:], d["qd"]], axis=0), s) for d, s in zip(items, ss)]
    vnew = [d["uw"][:, :DK] - w[:CHUNK] for d, w in zip(items, ws)]
    outs = [w[CHUNK:] + _nn(d["qk"], vn) for d, w, vn in zip(items, ws, vnew)]
    ss = [d["dl"] * s + _tn(d["kd"], vn) for d, s, vn in zip(items, ss, vnew)]
    return outs, ss, vnew


def _dn_bwd_chunk(items, ss, dss):
    ii, jj, strict, eye = _dn_decay(items)
    for d in items:
        kq = jnp.concatenate([d["k"], d["q"]], axis=0)
        gram = _nt(kq, kq)
        d["kk"] = gram[:CHUNK, :CHUNK]
        d["a"] = d["b"] * d["kk"] * d["gam_s"]
        d["qk"] = gram[CHUNK:, :CHUNK] * d["gam_i"]
        d["qkt"] = gram[:CHUNK, CHUNK:] * jnp.where(ii <= jj, jnp.exp(jnp.where(ii <= jj, -d["gdiff"], 0.0)), 0.0)
    for d, s, ds2 in zip(items, ss, dss):
        d["dvnew"] = _nn(jnp.concatenate([d["qkt"], d["kd"]], axis=1), jnp.concatenate([d["do"], ds2], axis=0))
        r = _nt(d["do"], jnp.concatenate([d["vnew"], s], axis=0))
        d["dqk"], d["dqd"] = r[:, :CHUNK], r[:, CHUNK:]
        d["dkd"] = _nt(d["vnew"], ds2)
    new_dss = []
    for d, s, ds2 in zip(items, ss, dss):
        d["dw"] = -_nt(d["dvnew"], s)
        new_dss.append(_tn_mxu(jnp.concatenate([d["qd"], -d["uw"][:, DK:]], axis=0),
                               jnp.concatenate([d["do"], d["dvnew"]], axis=0)) + d["dl"] * ds2)
        d["ddl"] = _sum0(_sum1(s * ds2))
    for d in items:
        d["drhs"] = _tn_mxu(d["t"], jnp.concatenate([d["dvnew"], d["dw"]], axis=1))
    for d in items:
        d["da"] = jnp.where(strict, -_nt(d["drhs"], d["uw"]), 0.0)
    outs = []
    for d in items:
        q, k, v, b, egc = d["q"], d["k"], d["v"], d["b"], d["egc"]
        drhs_u, drhs_w, da = d["drhs"][:, :DK], d["drhs"][:, DK:], d["da"]
        dbeta = _sum1(da * d["kk"] * d["gam_s"]) + _sum1(drhs_u * v) + _sum1(drhs_w * k) * egc
        dkk = da * b * d["gam_s"]
        e = da * d["a"] + d["dqk"] * d["qk"]
        s_kd = _sum1(d["dkd"] * d["kd"])
        dgc_col = _sum1(e) + _sum1(drhs_w * d["rhs_w"]) + _sum1(d["dqd"] * d["qd"]) - s_kd
        dgc_row = _sum0(jnp.where(eye, dgc_col, 0.0)) - _sum0(e)
        dg = _sum1(jnp.where(jj >= ii, dgc_row, 0.0)) + (_sum0(s_kd) + d["ddl"] * d["dl"])
        x = jnp.concatenate([d["dqk"] * d["gam_i"], dkk], axis=0)
        x1 = _nn(x, k)
        x2 = _tn_mxu(x, jnp.concatenate([q, k], axis=0))
        dq = x1[:CHUNK] + d["dqd"] * egc
        dk = x2 + x1[CHUNK:] + drhs_w * (b * egc) + d["dkd"] * d["ekd"]
        outs.append((dq, dk, drhs_u * b, dbeta, dg))
    return outs, new_dss


def _short_conv(raw, halo, w):
    tb = raw.shape[0]
    ext = jnp.concatenate([halo, raw], axis=0)
    out = ext[5:5 + tb] * w[0:1]
    for j in range(1, SHORT_K):
        out = out + ext[5 + j:5 + j + tb] * w[j:j + 1]
    return out


def _dn_heads(pre, ba, alog8, dtb8):
    sp = _sig(pre)
    act = pre * sp
    bsig = _sig(ba)
    xg = ba + dtb8
    ea8 = jnp.exp(alog8)
    gfull = -ea8 * _softplus(xg)
    items = []
    for h in range(HEADS):
        qc, kc, vc = (act[:, n * D_MODEL + h * DK:n * D_MODEL + (h + 1) * DK] for n in range(3))
        nq = lax.rsqrt(_sum1(qc * qc) + EPS)
        nk = lax.rsqrt(_sum1(kc * kc) + EPS)
        items.append(dict(q=qc * nq * (DK ** -0.5), k=kc * nk, v=vc, b=bsig[:, h:h + 1],
                          g=gfull[:, HEADS + h:HEADS + h + 1], qc=qc, kc=kc, nq=nq, nk=nk))
    return items, sp, xg, ea8


def _dn_out(o, z, sz, og):
    r = lax.rsqrt(_mean(o * o) + EPS)
    xh = o * r
    return xh * og * (z * sz), xh, r


def _sds(shape, dtype):
    return jax.ShapeDtypeStruct(tuple(shape), dtype)


def _matmul(a, b, out_dtype, tm, tn, tk, name):
    m, kd = a.shape
    n = b.shape[1]
    tm, tn, tk = min(tm, m), min(tn, n), min(tk, kd)
    nk = kd // tk

    def body(a_ref, b_ref, o_ref, acc):
        @pl.when(pl.program_id(2) == 0)
        def _():
            acc[...] = jnp.zeros_like(acc)

        acc[...] += jnp.dot(a_ref[...], b_ref[...], preferred_element_type=F32)

        @pl.when(pl.program_id(2) == nk - 1)
        def _():
            o_ref[...] = acc[...].astype(o_ref.dtype)

    return _pcall(
        body, name=name, grid=(m // tm, n // tn, nk),
        in_specs=[pl.BlockSpec((tm, tk), lambda i, j, k: (i, k)), pl.BlockSpec((tk, tn), lambda i, j, k: (k, j))],
        out_specs=pl.BlockSpec((tm, tn), lambda i, j, k: (i, j)), out_shape=_sds((m, n), out_dtype),
        scratch_shapes=[pltpu.VMEM((tm, tn), F32)], compiler_params=_cparams(("parallel", "parallel", "arbitrary")),
    )(a, b)


def _matmul_ksegs(a_segs, b, tm, tn, tk, name):
    m, n = a_segs[0].shape[0], b.shape[1]
    tm, tn = min(tm, m), min(tn, n)
    main, tail = a_segs[:-1], a_segs[-1]
    steps = [s.shape[1] // tk for s in main]
    starts = [sum(steps[:i]) for i in range(len(main))]
    nk = sum(steps)
    wt = tail.shape[1]

    def body(*refs):
        a_refs, at_ref, b_ref, bt_ref, o_ref, acc = refs[:len(main)], *refs[len(main):]
        k = pl.program_id(2)

        @pl.when(k == 0)
        def _():
            acc[...] = jnp.zeros_like(acc)

        for a_ref, k0, ns in zip(a_refs, starts, steps):
            @pl.when((k >= k0) & (k < k0 + ns))
            def _():
                acc[...] += jnp.dot(a_ref[...], b_ref[...], preferred_element_type=F32)

        @pl.when(k == nk)
        def _():
            o_ref[...] = acc[...] + jnp.dot(at_ref[...], bt_ref[...], preferred_element_type=F32)

    seg_specs = [pl.BlockSpec((tm, tk), functools.partial(lambda i, j, k, k0, ns: (i, jnp.clip(k - k0, 0, ns - 1)), k0=k0, ns=ns))
                 for k0, ns in zip(starts, steps)]
    return _pcall(
        body, name=name, grid=(m // tm, n // tn, nk + 1),
        in_specs=seg_specs + [pl.BlockSpec((tm, wt), lambda i, j, k: (i, 0)),
                              pl.BlockSpec((tk, tn), lambda i, j, k: (jnp.minimum(k, nk - 1), j)),
                              pl.BlockSpec((wt, tn), lambda i, j, k: (nk * tk // wt, j))],
        out_specs=pl.BlockSpec((tm, tn), lambda i, j, k: (i, j)), out_shape=_sds((m, n), F32),
        scratch_shapes=[pltpu.VMEM((tm, tn), F32)], compiler_params=_cparams(("parallel", "parallel", "arbitrary")),
    )(*main, tail, b, b)


def _rms_fwd_call(x, g, name):
    m = x.shape[0]
    tm = min(512, m)

    def body(x_ref, g_ref, o_ref):
        xv = x_ref[...]
        o_ref[...] = (xv * lax.rsqrt(_mean(xv * xv) + EPS) * g_ref[...]).astype(BF16)

    return _pcall(
        body, name=name, grid=(m // tm,),
        in_specs=[pl.BlockSpec((tm, D_MODEL), lambda i: (i, 0)), pl.BlockSpec((1, D_MODEL), lambda i: (0, 0))],
        out_specs=pl.BlockSpec((tm, D_MODEL), lambda i: (i, 0)), out_shape=_sds((m, D_MODEL), BF16),
        compiler_params=_cparams(("parallel",)),
    )(x, g)


def _rms_bwd_call(x, dh, g, dres, name):
    m = x.shape[0]
    tm = min(512, m)

    def body(x_ref, dh_ref, g_ref, dr_ref, dx_ref, dg_ref):
        @pl.when(pl.program_id(0) == 0)
        def _():
            dg_ref[...] = jnp.zeros_like(dg_ref)

        xv, dhv = x_ref[...], dh_ref[...]
        r = lax.rsqrt(_mean(xv * xv) + EPS)
        xh = xv * r
        dx_ref[...] = _rms_bwd(dhv * g_ref[...], xh, r) + dr_ref[...]
        dg_ref[...] += _rows8(_sum0(dhv * xh))

    row = pl.BlockSpec((tm, D_MODEL), lambda i: (i, 0))
    return _pcall(
        body, name=name, grid=(m // tm,),
        in_specs=[row, row, pl.BlockSpec((1, D_MODEL), lambda i: (0, 0)), row],
        out_specs=[row, pl.BlockSpec((8, D_MODEL), lambda i: (0, 0))],
        out_shape=[_sds((m, D_MODEL), F32), _sds((8, D_MODEL), F32)], compiler_params=_cparams(("arbitrary",)),
    )(x, dh, g, dres)


def _loss_call(x, tgt, g):
    m = x.shape[0]
    tm = min(512, m)

    def body(x_ref, t_ref, g_ref, dx_ref, dg_ref, l_ref):
        @pl.when(pl.program_id(0) == 0)
        def _():
            dg_ref[...] = jnp.zeros_like(dg_ref)
            l_ref[...] = jnp.zeros_like(l_ref)

        xv = x_ref[...]
        r = lax.rsqrt(_mean(xv * xv) + EPS)
        xh = xv * r
        err = xh * g_ref[...] - t_ref[...]
        dy = err * (1.0 / D_MODEL)
        dx_ref[...] = _rms_bwd(dy * g_ref[...], xh, r)
        dg_ref[...] += _rows8(_sum0(dy * xh))
        l_ref[...] += 0.5 * _sum0(_mean(err * err))

    row = pl.BlockSpec((tm, D_MODEL), lambda i: (i, 0))
    return _pcall(
        body, name="loss_head", grid=(m // tm,),
        in_specs=[row, row, pl.BlockSpec((1, D_MODEL), lambda i: (0, 0))],
        out_specs=[row, pl.BlockSpec((8, D_MODEL), lambda i: (0, 0)), pl.BlockSpec((8, 128), lambda i: (0, 0))],
        out_shape=[_sds((m, D_MODEL), F32), _sds((8, D_MODEL), F32), _sds((8, 128), F32)],
        compiler_params=_cparams(("arbitrary",)),
    )(x, tgt, g)


def _halo_idx(i, rows):
    return jnp.maximum(i * (TB // rows) - 1, 0)


def _a_tiles(nt, rev):
    ti = (lambda i: nt - 1 - i) if rev else (lambda i: i)
    c0 = A_OFF // CONV_W
    return [pl.BlockSpec((1, TB, CONV_W), functools.partial(lambda b, i, c: (b, ti(i), c), c=c0 + c)) for c in range(3)]


def _a_fwd_call(proj, w, bias, g, b, name):
    bsz, t, _ = proj.shape
    nt = t // TB
    c0 = A_OFF // CONV_W

    def body(val_ref, glu_ref, az_ref, vh_ref, gh_ref, w_ref, bias_ref, g_ref, b_ref, y_ref, c_ref):
        keep = jnp.where(pl.program_id(1) > 0, 1.0, 0.0)
        y, c = _a_fwd(val_ref[0], glu_ref[0], az_ref[0], vh_ref[0] * keep, gh_ref[0], w_ref[...], bias_ref[...],
                      g_ref[...], b_ref[...])
        y_ref[0] = y.astype(BF16)
        c_ref[0] = c

    halo = [pl.BlockSpec((1, HALO_A, CONV_W), functools.partial(lambda b, i, c: (b, _halo_idx(i, HALO_A), c), c=c0 + c))
            for c in range(2)]
    par = [pl.BlockSpec((HALO_A, CONV_W), lambda b, i: (0, 0))] + [pl.BlockSpec((1, CONV_W), lambda b, i: (0, 0))] * 3
    tile = pl.BlockSpec((1, TB, CONV_W), lambda b, i: (b, i, 0))
    return _pcall(
        body, name=name, grid=(bsz, nt), in_specs=_a_tiles(nt, False) + halo + par, out_specs=[tile, tile],
        out_shape=[_sds((bsz, t, CONV_W), BF16), _sds((bsz, t, CONV_W), F32)],
        compiler_params=_cparams(("parallel", "parallel")),
    )(proj, proj, proj, proj, proj, w, bias, g, b)


def _a_bwd_call(dy, conv, proj, w, g, b, name):
    bsz, t, _ = proj.shape
    nt = t // TB

    def body(dy_ref, c_ref, val_ref, glu_ref, az_ref, w_ref, g_ref, b_ref, da_ref, dw_ref, ds_ref, carry):
        ip = pl.program_id(1)

        @pl.when((pl.program_id(0) == 0) & (ip == 0))
        def _():
            dw_ref[...] = jnp.zeros_like(dw_ref)
            ds_ref[...] = jnp.zeros_like(ds_ref)

        @pl.when(ip == 0)
        def _():
            carry[...] = jnp.zeros_like(carry)

        dval, dglu, daz, dw, dbias, dg, db, head = _a_bwd(dy_ref[0], val_ref[0], glu_ref[0], az_ref[0], c_ref[0],
                                                          w_ref[...], g_ref[...], b_ref[...], carry[...])
        carry[...] = head
        da_ref[0] = jnp.concatenate([dval, dglu, daz], axis=1).astype(BF16)
        dw_ref[...] += dw
        ds_ref[...] += _rows8(dbias, dg, db)

    rtile = lambda b, i: (b, nt - 1 - i, 0)
    par = [pl.BlockSpec((HALO_A, CONV_W), lambda b, i: (0, 0))] + [pl.BlockSpec((1, CONV_W), lambda b, i: (0, 0))] * 2
    return _pcall(
        body, name=name, grid=(bsz, nt),
        in_specs=[pl.BlockSpec((1, TB, CONV_W), rtile)] * 2 + _a_tiles(nt, True) + par,
        out_specs=[pl.BlockSpec((1, TB, 3 * CONV_W), rtile), pl.BlockSpec((HALO_A, CONV_W), lambda b, i: (0, 0)),
                   pl.BlockSpec((8, CONV_W), lambda b, i: (0, 0))],
        out_shape=[_sds((bsz, t, 3 * CONV_W), BF16), _sds((HALO_A, CONV_W), F32), _sds((8, CONV_W), F32)],
        scratch_shapes=[pltpu.VMEM((HALO_A, CONV_W), F32)], compiler_params=_cparams(("arbitrary", "arbitrary")),
    )(dy, conv, proj, proj, proj, w, g, b)


def _c_specs():
    c0 = C_OFF // SG_W
    tile = [pl.BlockSpec((1, TB, SG_W), functools.partial(lambda b, i, c: (b, i, c), c=c0 + c)) for c in range(3)]
    par = [pl.BlockSpec((1, SG_W), lambda b, i: (0, 0))] * 2 + [
        pl.BlockSpec((SG_G, SG_C, SG_C), lambda b, i: (0, 0, 0)), pl.BlockSpec((SG_C, SG_W), lambda b, i: (0, 0))]
    return tile + par


def _c_fwd_call(proj, g, b, ws, bias_full, name):
    bsz, t, _ = proj.shape

    def body(cu_ref, cv_ref, cz_ref, g_ref, b_ref, ws_ref, bf_ref, y_ref):
        y, _ = _c_fwd(cu_ref[0], cv_ref[0], cz_ref[0], g_ref[...], b_ref[...], ws_ref[...], bf_ref[...])
        y_ref[0] = y.astype(BF16)

    return _pcall(
        body, name=name, grid=(bsz, t // TB), in_specs=_c_specs(),
        out_specs=pl.BlockSpec((1, TB, SG_W), lambda b, i: (b, i, 0)), out_shape=_sds((bsz, t, SG_W), BF16),
        compiler_params=_cparams(("parallel", "parallel")),
    )(proj, proj, proj, g, b, ws, bias_full)


def _c_bwd_call(dy, proj, g, b, ws, bias_full, name):
    bsz, t, _ = proj.shape

    def body(dy_ref, cu_ref, cv_ref, cz_ref, g_ref, b_ref, ws_ref, bf_ref, dc_ref, dws_ref, dbs_ref, ds_ref):
        @pl.when((pl.program_id(0) == 0) & (pl.program_id(1) == 0))
        def _():
            dws_ref[...] = jnp.zeros_like(dws_ref)
            dbs_ref[...] = jnp.zeros_like(dbs_ref)
            ds_ref[...] = jnp.zeros_like(ds_ref)

        dcu, dcv, dcz, dws, dbs, dg, db = _c_bwd(dy_ref[0], cu_ref[0], cv_ref[0], cz_ref[0], g_ref[...], b_ref[...],
                                                 ws_ref[...], bf_ref[...])
        dc_ref[0] = jnp.concatenate([dcu, dcv, dcz], axis=1).astype(BF16)
        for gi in range(SG_G):
            dws_ref[gi] += dws[gi]
        dbs_ref[...] += dbs
        ds_ref[...] += _rows8(dg, db)

    tile = lambda b, i: (b, i, 0)
    return _pcall(
        body, name=name, grid=(bsz, t // TB), in_specs=[pl.BlockSpec((1, TB, SG_W), tile)] + _c_specs(),
        out_specs=[pl.BlockSpec((1, TB, 3 * SG_W), tile), pl.BlockSpec((SG_G, SG_C, SG_C), lambda b, i: (0, 0, 0)),
                   pl.BlockSpec((SG_C, SG_W), lambda b, i: (0, 0)), pl.BlockSpec((8, SG_W), lambda b, i: (0, 0))],
        out_shape=[_sds((bsz, t, 3 * SG_W), BF16), _sds((SG_G, SG_C, SG_C), F32), _sds((SG_C, SG_W), F32),
                   _sds((8, SG_W), F32)],
        compiler_params=_cparams(("arbitrary", "arbitrary")),
    )(dy, proj, proj, proj, g, b, ws, bias_full)


def _m_specs():
    g0 = G_OFF // D_MODEL
    y = [pl.BlockSpec((1, TB, n), lambda b, i: (b, i, 0)) for n in (CONV_W, D_MODEL, SG_W)]
    gates = [pl.BlockSpec((1, TB, D_MODEL), functools.partial(lambda b, i, c: (b, i, c), c=g0 + c)) for c in range(3)]
    return y + gates


def _w_specs(*shapes):
    return [pl.BlockSpec(s, lambda b, i: (0, 0)) for s in shapes]


def _m_fwd_call(ya, yb, yc, proj, x, wa, wb, wc, wo, name):
    bsz, t, _ = x.shape

    def body(ya_ref, yb_ref, yc_ref, g0_ref, g1_ref, g2_ref, x_ref, wa_ref, wb_ref, wc_ref, wo_ref, o_ref):
        merged, _ = _m_fwd(ya_ref[0], yb_ref[0], yc_ref[0], g0_ref[0], g1_ref[0], g2_ref[0], wa_ref[...], wb_ref[...],
                           wc_ref[...])
        o_ref[0] = x_ref[0] + _nn(merged, wo_ref[...])

    tile = pl.BlockSpec((1, TB, D_MODEL), lambda b, i: (b, i, 0))
    return _pcall(
        body, name=name, grid=(bsz, t // TB),
        in_specs=_m_specs() + [tile] + _w_specs(wa.shape, wb.shape, wc.shape, wo.shape),
        out_specs=tile, out_shape=_sds(x.shape, F32), compiler_params=_cparams(("parallel", "parallel")),
    )(ya, yb, yc, proj, proj, proj, x, wa, wb, wc, wo)


def _m_bwd_call(dout, ya, yb, yc, proj, wa, wb, wc, wo, name):
    bsz, t, _ = dout.shape

    def body(do_ref, ya_ref, yb_ref, yc_ref, g0_ref, g1_ref, g2_ref, wa_ref, wb_ref, wc_ref, wo_ref, dya_ref, dyb_ref,
             dyc_ref, dg_ref, mg_ref, dp_ref):
        dys, dgs, merged, dps = _m_bwd(do_ref[0], ya_ref[0], yb_ref[0], yc_ref[0], g0_ref[0], g1_ref[0], g2_ref[0],
                                       wa_ref[...], wb_ref[...], wc_ref[...], wo_ref[...])
        dya_ref[0], dyb_ref[0], dyc_ref[0] = dys
        dg_ref[0] = jnp.concatenate(dgs, axis=1).astype(BF16)
        mg_ref[0] = merged.astype(BF16)
        dp_ref[0] = jnp.concatenate(dps, axis=1).astype(BF16)

    tile = lambda n: pl.BlockSpec((1, TB, n), lambda b, i: (b, i, 0))
    widths = (CONV_W, D_MODEL, SG_W, 3 * D_MODEL, D_MODEL, 3 * D_MODEL)
    dts = (F32, F32, F32, BF16, BF16, BF16)
    return _pcall(
        body, name=name, grid=(bsz, t // TB),
        in_specs=[tile(D_MODEL)] + _m_specs() + _w_specs(wa.shape, wb.shape, wc.shape, wo.shape),
        out_specs=[tile(n) for n in widths], out_shape=[_sds((bsz, t, n), d) for n, d in zip(widths, dts)],
        compiler_params=_cparams(("parallel", "parallel")),
    )(dout, ya, yb, yc, proj, proj, proj, wa, wb, wc, wo)


def _m_wgrad_call(ya, yb, yc, merged, dps, dout, name):
    bsz, t, _ = dout.shape

    def body(ya_ref, yb_ref, yc_ref, mg_ref, dp_ref, do_ref, dwa_ref, dwb_ref, dwc_ref, dwo_ref):
        @pl.when((pl.program_id(0) == 0) & (pl.program_id(1) == 0))
        def _():
            for r in (dwa_ref, dwb_ref, dwc_ref, dwo_ref):
                r[...] = jnp.zeros_like(r)

        dp = dp_ref[0]
        dwa_ref[...] += _tn(ya_ref[0], dp[:, :D_MODEL])
        dwb_ref[...] += _tn(yb_ref[0], dp[:, D_MODEL:2 * D_MODEL])
        dwc_ref[...] += _tn(yc_ref[0], dp[:, 2 * D_MODEL:])
        dwo_ref[...] += _tn(mg_ref[0], do_ref[0])

    tile = lambda n: pl.BlockSpec((1, TB, n), lambda b, i: (b, i, 0))
    shapes = ((CONV_W, D_MODEL), (D_MODEL, D_MODEL), (SG_W, D_MODEL), (D_MODEL, D_MODEL))
    return _pcall(
        body, name=name, grid=(bsz, t // TB),
        in_specs=[tile(CONV_W), tile(D_MODEL), tile(SG_W), tile(D_MODEL), tile(3 * D_MODEL), tile(D_MODEL)],
        out_specs=_w_specs(*shapes), out_shape=[_sds(s, F32) for s in shapes],
        compiler_params=_cparams(("arbitrary", "arbitrary")),
    )(ya, yb, yc, merged, dps, dout)


def _dn_specs(nc, rev):
    ti = (lambda i: nc - 1 - i) if rev else (lambda i: i)
    return [pl.BlockSpec((1, CHUNK, B_W), lambda b, i: (b, ti(i), 0)),
            pl.BlockSpec((1, CHUNK, DK), lambda b, i: (b, ti(i), BA_OFF // DK)),
            pl.BlockSpec((SHORT_K, 3 * D_MODEL), lambda b, i: (0, 0))] + [pl.BlockSpec((1, DK), lambda b, i: (0, 0))] * 3


def _dn_saved_specs(nc, rev):
    ti = (lambda i: nc - 1 - i) if rev else (lambda i: i)
    return [pl.BlockSpec((1, CHUNK, D_MODEL), lambda b, i: (b, ti(i), 0)),
            pl.BlockSpec((1, CHUNK, 3 * D_MODEL), lambda b, i: (b, ti(i), 0)),
            pl.BlockSpec((1, HEADS, 1, DK, DK), lambda b, i: (b, 0, ti(i), 0, 0)),
            pl.BlockSpec((1, 1, CHUNK, HEADS * CHUNK), lambda b, i: (b, ti(i), 0, 0)),
            pl.BlockSpec((1, CHUNK, 2 * D_MODEL), lambda b, i: (b, ti(i), 0)),
            pl.BlockSpec((1, CHUNK, D_MODEL), lambda b, i: (b, ti(i), 0))]


def _dn_fwd_call(proj, wconv, alog, dtb, og, name):
    bsz, t, _ = proj.shape
    nc = t // CHUNK

    def body(x_ref, ba_ref, w_ref, al_ref, dt_ref, og_ref, halo_ref, y_ref, o_ref, pre_ref, st_ref, t_ref, uw_ref, vn_ref,
             s_scr):
        i = pl.program_id(1)

        @pl.when(i == 0)
        def _():
            s_scr[...] = jnp.zeros_like(s_scr)

        x = x_ref[0]
        keep = jnp.where(i > 0, 1.0, 0.0)
        pre = _short_conv(x[:, :3 * D_MODEL], halo_ref[0] * keep, w_ref[...])
        pre_ref[0] = pre
        items, _, _, _ = _dn_heads(pre, ba_ref[0], al_ref[...], dt_ref[...])
        ss = [s_scr[h] for h in range(HEADS)]
        for h in range(HEADS):
            st_ref[0, h, 0] = ss[h]
        outs, ss, vnew = _dn_fwd_chunk(items, ss)
        for h in range(HEADS):
            s_scr[h] = ss[h]
        og_v = og_ref[...]
        z = x[:, 3 * D_MODEL:]
        sz = _sig(z)
        ys = [_dn_out(outs[h], z[:, h * DK:(h + 1) * DK], sz[:, h * DK:(h + 1) * DK], og_v)[0] for h in range(HEADS)]
        o_ref[0] = jnp.concatenate(outs, axis=1)
        y_ref[0] = jnp.concatenate(ys, axis=1).astype(BF16)
        t_ref[0, 0] = jnp.concatenate([d["t"] for d in items], axis=1).astype(BF16)
        uw_ref[0] = jnp.concatenate([d["uw"] for d in items], axis=1).astype(BF16)
        vn_ref[0] = jnp.concatenate(vnew, axis=1).astype(BF16)

    halo = pl.BlockSpec((1, HALO_B, 3 * D_MODEL), lambda b, i: (b, jnp.maximum(i * (CHUNK // HALO_B) - 1, 0), 0))
    return _pcall(
        body, name=name, grid=(bsz, nc), in_specs=_dn_specs(nc, False) + [halo],
        out_specs=[pl.BlockSpec((1, CHUNK, D_MODEL), lambda b, i: (b, i, 0))] + _dn_saved_specs(nc, False),
        out_shape=[_sds((bsz, t, D_MODEL), BF16), _sds((bsz, t, D_MODEL), F32), _sds((bsz, t, 3 * D_MODEL), F32),
                   _sds((bsz, HEADS, nc, DK, DK), F32), _sds((bsz, nc, CHUNK, HEADS * CHUNK), BF16),
                   _sds((bsz, t, 2 * D_MODEL), BF16), _sds((bsz, t, D_MODEL), BF16)],
        scratch_shapes=[pltpu.VMEM((HEADS, DK, DK), F32)], compiler_params=_cparams(("arbitrary", "arbitrary")),
    )(proj, proj, wconv, alog, dtb, og, proj)


def _dn_bwd_call(dy, saved, proj, wconv, alog, dtb, og, name):
    bsz, t, _ = proj.shape
    nc = t // CHUNK

    def body(dy_ref, o_ref, pre_ref, st_ref, t_ref, uw_ref, vn_ref, x_ref, ba_ref, w_ref, al_ref, dt_ref, og_ref,
             dx_ref, dba_ref, dwc_ref, dsm_ref, ds_scr, dpre_scr):
        ip = pl.program_id(1)

        @pl.when((pl.program_id(0) == 0) & (ip == 0))
        def _():
            dwc_ref[...] = jnp.zeros_like(dwc_ref)
            dsm_ref[...] = jnp.zeros_like(dsm_ref)

        @pl.when(ip == 0)
        def _():
            ds_scr[...] = jnp.zeros_like(ds_scr)
            dpre_scr[...] = jnp.zeros_like(dpre_scr)

        x, w, pre = x_ref[0], w_ref[...], pre_ref[0]
        items, sp, xg, ea8 = _dn_heads(pre, ba_ref[0], al_ref[...], dt_ref[...])
        og_v, dyv, ov, tv, uwv, vnv = og_ref[...], dy_ref[0], o_ref[0], t_ref[0, 0], uw_ref[0], vn_ref[0]
        z = x[:, 3 * D_MODEL:]
        sz = _sig(z)
        dsz = sz * (1.0 + z * (1.0 - sz))
        dzs, dog = [], jnp.zeros((1, DK), F32)
        for h, d in enumerate(items):
            hs = slice(h * DK, (h + 1) * DK)
            _, xh, r = _dn_out(ov[:, hs], z[:, hs], sz[:, hs], og_v)
            dzs.append(dyv[:, hs] * xh * og_v * dsz[:, hs])
            don = dyv[:, hs] * (z[:, hs] * sz[:, hs])
            dog = dog + _sum0(don * xh)
            d.update(do=_rms_bwd(don * og_v, xh, r), t=tv[:, h * CHUNK:(h + 1) * CHUNK],
                     uw=uwv[:, 2 * h * DK:2 * (h + 1) * DK], vnew=vnv[:, hs])
        grads, dss = _dn_bwd_chunk(items, [st_ref[0, h, 0] for h in range(HEADS)], [ds_scr[h] for h in range(HEADS)])
        for h in range(HEADS):
            ds_scr[h] = dss[h]
        dqc, dkc, dvc = [], [], []
        lanes = lax.broadcasted_iota(jnp.int32, (CHUNK, DK), 1)
        lane8 = lax.broadcasted_iota(jnp.int32, (8, DK), 1)
        row8 = lax.broadcasted_iota(jnp.int32, (8, DK), 0)
        dgdx = -ea8 * _sig(xg)
        dba = jnp.zeros((CHUNK, DK), F32)
        dsm = jnp.where(row8 == 2, dog, 0.0)
        for h, (d, (dq, dk, dv, dbeta, dg)) in enumerate(zip(items, grads)):
            qc, kc, nq, nk = d["qc"], d["kc"], d["nq"], d["nk"]
            dqc.append((DK ** -0.5) * nq * (dq - qc * (nq * nq * _sum1(dq * qc))))
            dkc.append(nk * (dk - kc * (nk * nk * _sum1(dk * kc))))
            dvc.append(dv)
            dal = dg * dgdx[:, HEADS + h:HEADS + h + 1]
            dba = dba + jnp.where(lanes == h, dbeta * d["b"] * (1.0 - d["b"]), 0.0) + jnp.where(lanes == HEADS + h, dal, 0.0)
            dsm = dsm + (jnp.where((row8 == 0) & (lane8 == h), _sum0(dg * d["g"]), 0.0)
                         + jnp.where((row8 == 1) & (lane8 == h), _sum0(dal), 0.0))
        dpre = jnp.concatenate(dqc + dkc + dvc, axis=1) * (sp * (1.0 + pre * (1.0 - sp)))
        dext = jnp.concatenate([dpre, dpre_scr[...]], axis=0)
        raw = x[:, :3 * D_MODEL]
        draw = dext[3:3 + CHUNK] * w[0:1]
        rows = [_sum0(dext[3:3 + CHUNK] * raw)]
        for j in range(1, SHORT_K):
            shifted = dext[3 - j:3 - j + CHUNK]
            draw = draw + shifted * w[j:j + 1]
            rows.append(_sum0(shifted * raw))
        dpre_scr[...] = dpre[:HALO_B]
        dx_ref[0] = jnp.concatenate([draw] + dzs, axis=1).astype(BF16)
        dba_ref[0] = dba.astype(BF16)
        dwc_ref[...] += _rows8(*rows)
        dsm_ref[...] += dsm

    rrow = lambda n: pl.BlockSpec((1, CHUNK, n), lambda b, i: (b, nc - 1 - i, 0))
    return _pcall(
        body, name=name, grid=(bsz, nc), in_specs=[rrow(D_MODEL)] + _dn_saved_specs(nc, True) + _dn_specs(nc, True),
        out_specs=[rrow(B_W), rrow(DK), pl.BlockSpec((8, 3 * D_MODEL), lambda b, i: (0, 0)),
                   pl.BlockSpec((8, DK), lambda b, i: (0, 0))],
        out_shape=[_sds((bsz, t, B_W), BF16), _sds((bsz, t, DK), BF16), _sds((8, 3 * D_MODEL), F32), _sds((8, DK), F32)],
        scratch_shapes=[pltpu.VMEM((HEADS, DK, DK), F32), pltpu.VMEM((HALO_B, 3 * D_MODEL), F32)],
        compiler_params=_cparams(("arbitrary", "arbitrary")),
    )(dy, *saved, proj, proj, wconv, alog, dtb, og)


def _adamw_call(w, g, m, v, name):
    shape = w.shape
    view = (math.prod(shape[:-2]),) + shape[-2:] if len(shape) >= 2 else (1, 1) + shape
    lead, rows, cols = view
    tr = 128 if rows % 128 == 0 else rows
    c1, c2 = 1.0 - ADAM_B1 ** ADAM_STEP, 1.0 - ADAM_B2 ** ADAM_STEP

    def body(w_ref, g_ref, m_ref, v_ref, d_ref, nm_ref, nv_ref):
        gv = g_ref[...]
        nm = ADAM_B1 * m_ref[...] + (1.0 - ADAM_B1) * gv
        nv = ADAM_B2 * v_ref[...] + (1.0 - ADAM_B2) * (gv * gv)
        d_ref[...] = -ADAM_LR * ((nm / c1) / (jnp.sqrt(nv / c2) + ADAM_EPS) + ADAM_WD * w_ref[...])
        nm_ref[...] = nm
        nv_ref[...] = nv

    blk = pl.BlockSpec((1, tr, cols), lambda l, i: (l, i, 0))
    outs = _pcall(
        body, name=name, grid=(lead, rows // tr), in_specs=[blk] * 4, out_specs=[blk] * 3,
        out_shape=[_sds(view, F32)] * 3, compiler_params=_cparams(("parallel", "parallel")),
    )(*(a.reshape(view) for a in (w, g, m, v)))
    return tuple(o.reshape(shape) for o in outs)


def _row_tile(rows, cap=512):
    for tr in range(cap, 15, -16):
        if rows % tr == 0:
            return tr
    return rows


def _add_pair_call(by_chip, recv, where, name):
    _, n, rows, cols = by_chip.shape
    tr = _row_tile(rows, 256)

    def body(where_ref, a_ref, b_ref, sb_ref, own_ref):
        s = a_ref[0, 0] + b_ref[0]
        sb_ref[0] = s.astype(BF16)

        @pl.when(pl.program_id(1) == where_ref[1])
        def _():
            own_ref[...] = s

    grid_spec = pltpu.PrefetchScalarGridSpec(
        num_scalar_prefetch=1, grid=(rows // tr, n),
        in_specs=[pl.BlockSpec((1, 1, tr, cols), lambda i, j, wr: (wr[0], j, i, 0)),
                  pl.BlockSpec((1, tr, cols), lambda i, j, wr: (j, i, 0))],
        out_specs=[pl.BlockSpec((1, tr, cols), lambda i, j, wr: (j, i, 0)),
                   pl.BlockSpec((tr, cols), lambda i, j, wr: (i, 0))])
    return _pcall(
        body, name=name, grid_spec=grid_spec, out_shape=[_sds((n, rows, cols), BF16), _sds((rows, cols), F32)],
        compiler_params=_cparams(("parallel", "arbitrary")),
    )(where, by_chip, recv)


def _add_recv_call(own, recv, where, name):
    rows, cols = own.shape
    tr = _row_tile(rows, 256)

    def body(where_ref, o_ref, r_ref, s_ref):
        s_ref[0] = ((o_ref[...] + r_ref[0].astype(F32)) + r_ref[1].astype(F32)) + r_ref[2].astype(F32)

    grid_spec = pltpu.PrefetchScalarGridSpec(
        num_scalar_prefetch=1, grid=(rows // tr,),
        in_specs=[pl.BlockSpec((tr, cols), lambda i, wr: (i, 0)), pl.BlockSpec((3, tr, cols), lambda i, wr: (0, i, 0))],
        out_specs=pl.BlockSpec((1, tr, cols), lambda i, wr: (wr[0], i, 0)))
    return _pcall(
        body, name=name, grid_spec=grid_spec, out_shape=_sds((2, rows, cols), F32),
        compiler_params=_cparams(("parallel",)),
    )(where, own, recv)


def _axes():
    return lax.axis_index("x"), lax.axis_index("y"), lax.axis_index("c")


def _chip_peers(x, y):
    return [(x, 1 - y), (1 - x, y), (1 - x, 1 - y)]


_ANY = pl.BlockSpec(memory_space=pl.ANY)
_VMEM = pl.BlockSpec(memory_space=pltpu.VMEM)


def _remote(src, dst, send_sems, recv_sems, k, dev):
    return pltpu.make_async_remote_copy(src_ref=src, dst_ref=dst, send_sem=send_sems.at[k], recv_sem=recv_sems.at[k],
                                        device_id=dev, device_id_type=MESH)


def _gather_weights_call(shards):
    n = len(shards)

    def body(*refs):
        x_refs, o_refs, (send_sems, recv_sems, local_sems) = refs[:n], refs[n:2 * n], refs[2 * n:]
        x, y, c = _axes()
        chip, sib, peers = 2 * x + y, (x, y, 1 - c), _chip_peers(x, y)
        pchip = [2 * px + py for px, py in peers]
        mine = [pltpu.make_async_copy(x_refs[a], o_refs[a].at[chip], local_sems.at[a]) for a in range(n)]
        for cp in mine:
            cp.start()
        first = [_remote(x_refs[a].at[c], o_refs[a].at[chip, c], send_sems, recv_sems, 6 * a + k, (*peers[k], c))
                 for k in range(3) for a in range(n)]
        for cp in first:
            cp.start()
        passed = []
        for k in range(3):
            for a in range(n):
                land = o_refs[a].at[pchip[k], c]
                _remote(land, land, send_sems, recv_sems, 6 * a + k, (*peers[k], c)).wait_recv()
                passed.append(_remote(land, land, send_sems, recv_sems, 6 * a + 3 + k, sib))
                passed[-1].start()
        for k in range(3):
            for a in range(n):
                land = o_refs[a].at[pchip[k], 1 - c]
                _remote(land, land, send_sems, recv_sems, 6 * a + 3 + k, sib).wait_recv()
        for cp in first + passed:
            cp.wait_send()
        for cp in mine:
            cp.wait()

    return _pcall(
        body, name="gather_weights", in_specs=[_ANY] * n, out_specs=[_ANY] * n,
        out_shape=[_sds((4,) + s.shape, s.dtype) for s in shards],
        scratch_shapes=[pltpu.SemaphoreType.DMA((6 * n,)), pltpu.SemaphoreType.DMA((6 * n,)),
                        pltpu.SemaphoreType.DMA((n,))],
    )(*shards)


def _pair_partials_call(by_chip):
    n = len(by_chip)

    def body(*refs):
        v_refs, o_refs, (send_sems, recv_sems) = refs[:n], refs[n:2 * n], refs[2 * n:]
        x, y, c = _axes()
        cps = [_remote(v_refs[a].at[1 - c], o_refs[a], send_sems, recv_sems, a, (x, y, 1 - c)) for a in range(n)]
        for cp in cps:
            cp.start()
        for cp in cps:
            cp.wait()

    return _pcall(
        body, name="pair_partials", in_specs=[_ANY] * n, out_specs=[_ANY] * n,
        out_shape=[_sds(v.shape[1:], v.dtype) for v in by_chip],
        scratch_shapes=[pltpu.SemaphoreType.DMA((n,)), pltpu.SemaphoreType.DMA((n,))],
    )(*by_chip)


def _scatter_partials_call(parts):
    n = len(parts)

    def body(*refs):
        v_refs, o_refs, (send_sems, recv_sems) = refs[:n], refs[n:2 * n], refs[2 * n:]
        x, y, c = _axes()
        peers = _chip_peers(x, y)
        cps = [_remote(v_refs[a].at[2 * peers[k][0] + peers[k][1]], o_refs[a].at[k], send_sems, recv_sems, 3 * a + k,
                       (*peers[k], c)) for k in range(3) for a in range(n)]
        for cp in cps:
            cp.start()
        for cp in cps:
            cp.wait()

    return _pcall(
        body, name="scatter_partials", in_specs=[_ANY] * n, out_specs=[_ANY] * n,
        out_shape=[_sds((3,) + v.shape[1:], v.dtype) for v in parts],
        scratch_shapes=[pltpu.SemaphoreType.DMA((3 * n,)), pltpu.SemaphoreType.DMA((3 * n,))],
    )(*parts)


def _pair_result_call(fins):
    n = len(fins)

    def body(*refs):
        v_refs, o_refs, (send_sems, recv_sems) = refs[:n], refs[n:2 * n], refs[2 * n:]
        x, y, c = _axes()
        cps = [_remote(v_refs[a].at[c], o_refs[a].at[c], send_sems, recv_sems, a, (x, y, 1 - c)) for a in range(n)]
        for cp in cps:
            cp.start()
        for a in range(n):
            cps[a].wait_send()
            _remote(v_refs[a].at[c], o_refs[a].at[1 - c], send_sems, recv_sems, a, (x, y, 1 - c)).wait_recv()

    return _pcall(
        body, name="pair_result", in_specs=[_ANY] * n, out_specs=[_ANY] * n,
        out_shape=[_sds(v.shape, v.dtype) for v in fins], input_output_aliases={a: a for a in range(n)},
        scratch_shapes=[pltpu.SemaphoreType.DMA((n,)), pltpu.SemaphoreType.DMA((n,))],
    )(*fins)


def _allreduce_small_call(v):
    rows, cols = v.shape

    def body(v_ref, o_ref, buf, send_sems, recv_sems):
        x, y, c = _axes()
        me = 4 * x + 2 * y + c
        buf[0] = v_ref[...]
        cps = []
        for m in range(1, 8):
            dev = (1 - x if m & 4 else x, 1 - y if m & 2 else y, 1 - c if m & 1 else c)
            cps.append(_remote(v_ref, buf.at[m], send_sems, recv_sems, m - 1, dev))
        for cp in cps:
            cp.start()
        for cp in cps:
            cp.wait()
        acc = buf[me]
        for d in range(1, 8):
            acc = acc + buf[lax.bitwise_xor(me, d)]
        o_ref[...] = acc

    return _pcall(
        body, name="allreduce_small", in_specs=[_VMEM], out_specs=_VMEM, out_shape=_sds(v.shape, F32),
        scratch_shapes=[pltpu.VMEM((8, rows, cols), F32), pltpu.SemaphoreType.DMA((7,)), pltpu.SemaphoreType.DMA((7,))],
        compiler_params=_cparams(),
    )(v)


def _pack_rows(arrays, dtype, total_rows=None):
    parts = []
    for a in arrays:
        flat = a.astype(dtype).reshape(-1)
        parts.append(jnp.pad(flat, (0, -flat.shape[0] % D_MODEL)).reshape(-1, D_MODEL))
    out = jnp.concatenate(parts, axis=0)
    total_rows = total_rows or out.shape[0] + (-out.shape[0] % 8)
    return jnp.pad(out, ((0, total_rows - out.shape[0]), (0, 0)))


def _unpack_rows(packed, shapes):
    out, r = [], 0
    for s in shapes:
        n = math.prod(s)
        nr = -(-n // D_MODEL)
        out.append(packed[r:r + nr].reshape(-1)[:n].reshape(s))
        r += nr
    return out


def _to_r(w):
    pad = jnp.zeros(w.shape[:-1] + (N_R - N_ORIG,), w.dtype)
    return jnp.concatenate([w[..., A_ORIG_W:BA_ORIG], w[..., :A_ORIG_W], w[..., BA_ORIG + 2 * HEADS:],
                            w[..., BA_ORIG:BA_ORIG + 2 * HEADS], pad], axis=-1)


_BIG = ("w_in", "a_proj", "b_proj", "c_proj", "w_out")
_SHARD_AXIS = {"w_in": 2, "a_proj": 2, "b_proj": 1, "c_proj": 2, "w_out": 1, "a_dw": 2, "b_conv": 2}
_BIG_AXIS = _SHARD_AXIS


def _join_chips(g, axis):
    g = jnp.moveaxis(g, 0, axis)
    return g.reshape(g.shape[:axis] + (4 * g.shape[axis + 1],) + g.shape[axis + 2:])


def _split_chips(a, axis):
    n = a.shape[axis] // 4
    return jnp.moveaxis(a.reshape(a.shape[:axis] + (4, n) + a.shape[axis + 1:]), axis, 1)
_ORDER = ("norm_g", "w_in", "a_dw", "a_dw_b", "a_ln_g", "a_ln_b", "a_proj", "b_conv", "b_a_log", "b_dt_bias",
          "b_onorm_g", "b_proj", "c_ln_g", "c_ln_b", "c_ws", "c_bs", "c_proj", "w_out", "final_g")


def _local_step(x, tgt, w):
    bsz, t, _ = x.shape
    m = bsz * t
    depth = w["norm_g"].shape[0]
    row = lambda v: v.reshape(1, -1)
    w_r = _to_r(w["w_in"])
    saved, xl = [], x
    for l in range(depth):
        n = f"l{l}_"
        par = dict(
            adw=jnp.pad(w["a_dw"][l], ((0, 1), (0, 0))), adb=row(w["a_dw_b"][l]), alg=row(w["a_ln_g"][l]),
            alb=row(w["a_ln_b"][l]), bconv=w["b_conv"][l],
            alog=jnp.pad(row(w["b_a_log"][l]), ((0, 0), (HEADS, DK - 2 * HEADS))),
            dtb=jnp.pad(row(w["b_dt_bias"][l]), ((0, 0), (HEADS, DK - 2 * HEADS))), og=row(w["b_onorm_g"][l]),
            clg=row(w["c_ln_g"][l]), clb=row(w["c_ln_b"][l]), cws=w["c_ws"][l],
            cbias=jnp.repeat(w["c_bs"][l].T, SG_C, axis=1), ng=row(w["norm_g"][l]),
            wa=w["a_proj"][l], wb=w["b_proj"][l], wc=w["c_proj"][l], wo=w["w_out"][l], wr=w_r[l])
        h = _rms_fwd_call(xl.reshape(m, D_MODEL), par["ng"], n + "norm")
        proj = _matmul(h, par["wr"], F32, 1024, 1152, 1024, n + "in_proj").reshape(bsz, t, N_R)
        ya, conv = _a_fwd_call(proj, par["adw"], par["adb"], par["alg"], par["alb"], n + "conv_fwd")
        yb, *dn_saved = _dn_fwd_call(proj, par["bconv"], par["alog"], par["dtb"], par["og"], n + "delta_fwd")
        yc = _c_fwd_call(proj, par["clg"], par["clb"], par["cws"], par["cbias"], n + "gmlp_fwd")
        x_next = _m_fwd_call(ya, yb, yc, proj, xl, par["wa"], par["wb"], par["wc"], par["wo"], n + "merge_fwd")
        saved.append((par, xl, h, proj, ya, yb, yc, conv, dn_saved))
        xl = x_next
    dout, dfg, loss = _loss_call(xl.reshape(m, D_MODEL), tgt.reshape(m, D_MODEL), row(w["final_g"]))
    dout = dout.reshape(bsz, t, D_MODEL)
    g = {k: [None] * depth for k in _ORDER if k != "final_g"}
    for l in reversed(range(depth)):
        n = f"l{l}_"
        par, xl, h, proj, ya, yb, yc, conv, dn_saved = saved[l]
        dya, dyb, dyc, dgate, merged, dps = _m_bwd_call(dout, ya, yb, yc, proj, par["wa"], par["wb"], par["wc"],
                                                        par["wo"], n + "merge_bwd")
        g["a_proj"][l], g["b_proj"][l], g["c_proj"][l], g["w_out"][l] = _m_wgrad_call(ya, yb, yc, merged, dps, dout,
                                                                                      n + "merge_wgrad")
        da, dadw, dasm = _a_bwd_call(dya, conv, proj, par["adw"], par["alg"], par["alb"], n + "conv_bwd")
        db, dba, dbconv, dbsm = _dn_bwd_call(dyb, dn_saved, proj, par["bconv"], par["alog"], par["dtb"], par["og"],
                                             n + "delta_bwd")
        dc, dcws, dcbs, dcsm = _c_bwd_call(dyc, proj, par["clg"], par["clb"], par["cws"], par["cbias"], n + "gmlp_bwd")
        segs = [s.reshape(m, s.shape[-1]) for s in (db, da, dc, dgate, dba)]
        dh = _matmul_ksegs(segs, par["wr"].T, 1024, 1024, 512, n + "in_proj_dx")
        ht = h.T
        dwb, dwa, dwc, dwg, dwba = [_matmul(ht, s, F32, 1024, 1024 if s.shape[1] % 1024 == 0 else 768, 1024,
                                            n + "in_proj_dw_" + tag) for s, tag in zip(segs, "bacgs")]
        g["w_in"][l] = jnp.concatenate([dwa, dwb, dwba[:, :2 * HEADS], dwc, dwg], axis=1)
        dx, dng = _rms_bwd_call(xl.reshape(m, D_MODEL), dh, par["ng"], dout.reshape(m, D_MODEL), n + "norm_bwd")
        dout = dx.reshape(bsz, t, D_MODEL)
        g["norm_g"][l] = dng[0]
        g["a_dw"][l], g["a_dw_b"][l], g["a_ln_g"][l], g["a_ln_b"][l] = dadw[:CONV_K], dasm[0], dasm[1], dasm[2]
        g["b_conv"][l], g["b_a_log"][l], g["b_dt_bias"][l] = dbconv[:SHORT_K], dbsm[0, :HEADS], dbsm[1, :HEADS]
        g["b_onorm_g"][l] = dbsm[2]
        g["c_ln_g"][l], g["c_ln_b"][l], g["c_ws"][l] = dcsm[0], dcsm[1], dcws
        g["c_bs"][l] = dcbs.reshape(SG_C, SG_G, SG_C).sum(-1).T
    grads = {k: jnp.stack(v) for k, v in g.items()}
    grads["final_g"] = dfg[0]
    return loss[0, 0], dout, grads


def kernel(x, norm_g, w_in, a_dw, a_dw_b, a_ln_g, a_ln_b, a_proj, b_conv, b_a_log, b_dt_bias, b_onorm_g, b_proj, c_ln_g, c_ln_b, c_ws, c_bs, c_proj, w_out, final_g, loss_target, m_norm_g, m_w_in, m_a_dw, m_a_dw_b, m_a_ln_g, m_a_ln_b, m_a_proj, m_b_conv, m_b_a_log, m_b_dt_bias, m_b_onorm_g, m_b_proj, m_c_ln_g, m_c_ln_b, m_c_ws, m_c_bs, m_c_proj, m_w_out, m_final_g, v_norm_g, v_w_in, v_a_dw, v_a_dw_b, v_a_ln_g, v_a_ln_b, v_a_proj, v_b_conv, v_b_a_log, v_b_dt_bias, v_b_onorm_g, v_b_proj, v_c_ln_g, v_c_ln_b, v_c_ws, v_c_bs, v_c_proj, v_w_out, v_final_g):
    given = dict(locals())
    ws = {k: given[k] for k in _ORDER}
    xi, yi, ci = _axes()
    chip = 2 * xi + yi
    where = jnp.stack([ci, chip]).astype(jnp.int32)

    sharded = _BIG + ("a_dw", "b_conv")
    gathered = _gather_weights_call([ws[k].astype(BF16) for k in _BIG] + [a_dw, b_conv])
    full = dict(ws)
    for k, g in zip(sharded, gathered):
        full[k] = _join_chips(g, _SHARD_AXIS[k])

    loss, grad_x, grads = _local_step(x, loss_target, full)

    by_chip = [_split_chips(grads[k], _BIG_AXIS[k]) for k in _BIG]
    theirs = _pair_partials_call(by_chip)
    sums = [_add_pair_call(b, r, where, "pair_sum_" + k) for k, b, r in zip(_BIG, by_chip, theirs)]
    got = _scatter_partials_call([s[0] for s in sums])
    fins = [_add_recv_call(s[1], r, where, "chip_sum_" + k) for k, s, r in zip(_BIG, sums, got)]
    out_g = dict(zip(_BIG, _pair_result_call(fins)))

    rest = [k for k in _ORDER if k not in _BIG]
    rest_shapes = [grads[k].shape for k in rest] + [(1,)]
    summed = _unpack_rows(_allreduce_small_call(_pack_rows([grads[k] for k in rest] + [loss.reshape(1)], F32)),
                          rest_shapes)
    out_g.update(zip(rest, summed[:-1]))
    out_g["a_dw"] = lax.dynamic_slice_in_dim(out_g["a_dw"], chip * a_dw.shape[2], a_dw.shape[2], axis=2)
    out_g["b_conv"] = lax.dynamic_slice_in_dim(out_g["b_conv"], chip * b_conv.shape[2], b_conv.shape[2], axis=2)

    upd = {k: _adamw_call(ws[k], out_g[k], given["m_" + k], given["v_" + k], "adamw_" + k) for k in _ORDER}
    return (summed[-1][0], grad_x, *[out_g[k] for k in _ORDER], *[upd[k][0] for k in _ORDER],
            *[upd[k][1] for k in _ORDER], *[upd[k][2] for k in _ORDER])
```

```python
import functools
import math

import jax
import jax.numpy as jnp
from jax import lax
from jax.experimental import pallas as pl
from jax.experimental.pallas import tpu as pltpu

F32 = jnp.float32
BF16 = jnp.bfloat16
MESH = pl.DeviceIdType.MESH

D_MODEL = 1024
CONV_W = 512
CONV_K = 31
HEADS = 8
DK = 128
SHORT_K = 4
CHUNK = 64
SG_W = 512
SG_G = 4
SG_C = 128
EPS = 1e-6
N_ORIG = 10256
BA_ORIG = 5632
A_ORIG_W = 3 * CONV_W
B_W = 4 * D_MODEL
B_OFF, A_OFF, C_OFF, G_OFF, BA_OFF, N_R = 0, 4096, 5632, 7168, 10240, 10368
TB = 256
HALO_A = 32
HALO_B = 8
VMEM_LIMIT = 56 * 1024 * 1024
ADAM_LR, ADAM_B1, ADAM_B2, ADAM_EPS, ADAM_WD, ADAM_STEP = 0.001, 0.9, 0.999, 1e-08, 0.01, 10


def _pcall(body, **kw):
    return pl.pallas_call(body, **kw)


def _cparams(sem=None):
    kw = dict(vmem_limit_bytes=VMEM_LIMIT)
    if sem is not None:
        kw["dimension_semantics"] = sem
    return pltpu.CompilerParams(**kw)


def _sig(x):
    return jax.nn.sigmoid(x)


def _silu(x):
    return x * _sig(x)


def _dsilu(x):
    s = _sig(x)
    return s * (1.0 + x * (1.0 - s))


_GELU_C = math.sqrt(2.0 / math.pi)


def _gelu(x):
    return 0.5 * x * (1.0 + jnp.tanh(_GELU_C * (x + 0.044715 * x * x * x)))


def _dgelu(x):
    t = jnp.tanh(_GELU_C * (x + 0.044715 * x * x * x))
    return 0.5 * (1.0 + t) + 0.5 * x * (1.0 - t * t) * _GELU_C * (1.0 + 3 * 0.044715 * x * x)


def _softplus(x):
    return jnp.maximum(x, 0.0) + jnp.log1p(jnp.exp(-jnp.abs(x)))


def _dot(a, b, dims):
    return lax.dot_general(a.astype(BF16), b.astype(BF16), (dims, ((), ())), preferred_element_type=F32)


def _nn(a, b):
    return _dot(a, b, ((1,), (0,)))


def _nt(a, b):
    return _dot(a, b, ((1,), (1,)))


def _tn(a, b):
    return _dot(a, b, ((0,), (0,)))


def _tn_mxu(a, b):
    n = a.shape[1]
    eye = (lax.broadcasted_iota(jnp.int32, (n, n), 0) == lax.broadcasted_iota(jnp.int32, (n, n), 1)).astype(BF16)
    return _nn(_nt(eye, a), b)


def _mean(x):
    return jnp.mean(x, axis=-1, keepdims=True)


def _sum0(x):
    return jnp.sum(x, axis=0, keepdims=True)


def _sum1(x):
    return jnp.sum(x, axis=1, keepdims=True)


def _ln_fwd(x, g, b):
    xc = x - _mean(x)
    r = lax.rsqrt(_mean(xc * xc) + EPS)
    xh = xc * r
    return xh * g + b, xh, r


def _ln_bwd(dy, xh, r, g):
    dxh = dy * g
    return r * (dxh - _mean(dxh) - xh * _mean(dxh * xh)), _sum0(dy * xh), _sum0(dy)


def _rms_bwd(dxh, xh, r):
    return r * (dxh - xh * _mean(dxh * xh))


def _rows8(*rows):
    n = rows[0].shape[1]
    return jnp.concatenate(list(rows) + [jnp.zeros((8 - len(rows), n), F32)], axis=0)


def _windows(ext, tb):
    n = ext.shape[0] - 8
    shifted = {0: ext}

    def window(off):
        r = off % 8
        if r not in shifted:
            shifted[r] = ext[r:r + n]
        return shifted[r][off - r:off - r + tb]

    return window


def _a_fwd(val, glu, az, val_h, glu_h, w, bias, g, b):
    tb = val.shape[0]
    win = _windows(jnp.concatenate([val_h * _sig(glu_h), val * _sig(glu)], axis=0), tb)
    c = win(2) * w[0:1]
    for j in range(1, CONV_K):
        c = c + win(2 + j) * w[j:j + 1]
    c = c + bias
    ln, _, _ = _ln_fwd(c, g, b)
    return _silu(ln) * _silu(az), c


def _a_bwd(dy, val, glu, az, c, w, g, b, dc_next):
    tb = val.shape[0]
    ln, xh, r = _ln_fwd(c, g, b)
    sl, sz = _sig(ln), _sig(az)
    dln = dy * (az * sz) * (sl * (1.0 + ln * (1.0 - sl)))
    daz = dy * (ln * sl) * (sz * (1.0 + az * (1.0 - sz)))
    dc, dg, db = _ln_bwd(dln, xh, r, g)
    win = _windows(jnp.concatenate([dc, dc_next], axis=0), tb)
    sg = _sig(glu)
    a = val * sg
    da = win(30) * w[0:1]
    dw_rows = [_sum0(win(30) * a)]
    for j in range(1, CONV_K):
        shifted = win(30 - j)
        da = da + shifted * w[j:j + 1]
        dw_rows.append(_sum0(shifted * a))
    dw = jnp.concatenate(dw_rows + [jnp.zeros((1, CONV_W), F32)], axis=0)
    return da * sg, da * val * sg * (1.0 - sg), daz, dw, _sum0(dc), dg, db, dc[:HALO_A]


def _tril(ws):
    ii = lax.broadcasted_iota(jnp.int32, (SG_C, SG_C), 0)
    jj = lax.broadcasted_iota(jnp.int32, (SG_C, SG_C), 1)
    return [jnp.where(jj <= ii, ws[gi], 0.0) for gi in range(SG_G)], jj <= ii


def _c_mix(wt, vs, bias_full):
    tb = vs.shape[0]
    rows = []
    for n in range(tb // SG_C):
        blks = [_nn(wt[gi], vs[n * SG_C:(n + 1) * SG_C, gi * SG_C:(gi + 1) * SG_C]) for gi in range(SG_G)]
        rows.append(jnp.concatenate(blks, axis=1) + bias_full)
    return jnp.concatenate(rows, axis=0)


def _c_fwd(cu, cv, cz, g, b, ws, bias_full):
    wt, _ = _tril(ws)
    vs, xh, r = _ln_fwd(_gelu(cv), g, b)
    mixed = _c_mix(wt, vs, bias_full)
    return _gelu(cu) * mixed * _silu(cz), (wt, vs, xh, r, mixed)


def _c_bwd(dy, cu, cv, cz, g, b, ws, bias_full):
    tb = cu.shape[0]
    _, (wt, vs, xh, r, mixed) = _c_fwd(cu, cv, cz, g, b, ws, bias_full)
    _, low = _tril(ws)
    u, sz = _gelu(cu), _silu(cz)
    dcu = dy * mixed * sz * _dgelu(cu)
    dcz = dy * u * mixed * _dsilu(cz)
    dmixed = dy * u * sz
    dbs = jnp.zeros((SG_C, SG_W), F32)
    dws = [jnp.zeros((SG_C, SG_C), F32) for _ in range(SG_G)]
    rows = []
    for n in range(tb // SG_C):
        dm_n = dmixed[n * SG_C:(n + 1) * SG_C]
        dbs = dbs + dm_n
        blks = []
        for gi in range(SG_G):
            dm = dm_n[:, gi * SG_C:(gi + 1) * SG_C]
            dws[gi] = dws[gi] + _nt(dm, vs[n * SG_C:(n + 1) * SG_C, gi * SG_C:(gi + 1) * SG_C])
            blks.append(_tn(wt[gi], dm))
        rows.append(jnp.concatenate(blks, axis=1))
    dvs = jnp.concatenate(rows, axis=0)
    dgv, dg, db = _ln_bwd(dvs, xh, r, g)
    dws = [jnp.where(low, d, 0.0) for d in dws]
    return dcu, dgv * _dgelu(cv), dcz, dws, dbs, dg, db


def _m_fwd(ya, yb, yc, g0, g1, g2, wa, wb, wc):
    pa, pb, pc = _nn(ya, wa), _nn(yb, wb), _nn(yc, wc)
    s0, s1, s2 = _sig(g0), _sig(g1), _sig(g2)
    return s0 * pa + s1 * pb + s2 * pc, (pa, pb, pc, s0, s1, s2)


def _m_bwd(dout, ya, yb, yc, g0, g1, g2, wa, wb, wc, wo):
    merged, (pa, pb, pc, s0, s1, s2) = _m_fwd(ya, yb, yc, g0, g1, g2, wa, wb, wc)
    dm = _nt(dout, wo)
    dpa, dpb, dpc = dm * s0, dm * s1, dm * s2
    dgs = (dm * pa * s0 * (1.0 - s0), dm * pb * s1 * (1.0 - s1), dm * pc * s2 * (1.0 - s2))
    return (_nt(dpa, wa), _nt(dpb, wb), _nt(dpc, wc)), dgs, merged, (dpa, dpb, dpc)


def _chunk_masks(c):
    ii = lax.broadcasted_iota(jnp.int32, (c, c), 0)
    jj = lax.broadcasted_iota(jnp.int32, (c, c), 1)
    return ii, jj


def _dn_decay(items):
    ii, jj = _chunk_masks(CHUNK)
    incl, strict, eye = jj <= ii, jj < ii, ii == jj
    for d in items:
        g = d["g"]
        grow = _sum0(jnp.where(eye, g, 0.0))
        gc_col = _sum1(jnp.where(incl, grow, 0.0))
        gc_row = _sum0(jnp.where(ii <= jj, g, 0.0))
        gam_i = jnp.where(incl, jnp.exp(jnp.where(incl, gc_col - gc_row, 0.0)), 0.0)
        gl = _sum0(g)
        egc = jnp.exp(gc_col)
        d.update(gam_i=gam_i, gam_s=jnp.where(strict, gam_i, 0.0), egc=egc, ekd=jnp.exp(gl - gc_col), dl=jnp.exp(gl),
                 gdiff=gc_col - gc_row, qd=d["q"] * egc, rhs_w=d["k"] * (d["b"] * egc))
        d["kd"] = d["k"] * d["ekd"]
    return ii, jj, strict, eye


def _dn_solve(items, ii, jj, eye):
    off = ((ii >> 1) == (jj >> 1)) & ((ii & 1) != 0) & ((jj & 1) == 0)
    for d in items:
        a1 = jnp.where(off, d["a"], 0.0)
        d["t"] = jnp.where(eye, 1.0, 0.0) - a1
        d["m"] = d["a"] - _nn(a1, d["a"])
    b, sh = 2, 2
    while b < CHUNK:
        off = ((ii >> sh) == (jj >> sh)) & ((ii & b) != 0) & ((jj & b) == 0)
        for d in items:
            mo = jnp.where(off, d["m"], 0.0)
            if 2 * b < CHUNK:
                r = _nn(mo, jnp.concatenate([d["m"], d["t"]], axis=1))
                d["m"], d["t"] = d["m"] - r[:, :CHUNK], d["t"] - r[:, CHUNK:]
            else:
                d["t"] = d["t"] - _nn(mo, d["t"])
        b, sh = 2 * b, sh + 1


def _dn_fwd_chunk(items, ss):
    ii, jj, strict, eye = _dn_decay(items)
    for d in items:
        kq = _nt(jnp.concatenate([d["k"], d["q"]], axis=0), d["k"])
        d["a"] = d["b"] * kq[:CHUNK] * d["gam_s"]
        d["qk"] = kq[CHUNK:] * d["gam_i"]
    _dn_solve(items, ii, jj, eye)
    for d in items:
        d["uw"] = _nn(d["t"], jnp.concatenate([d["v"] * d["b"], d["rhs_w"]], axis=1))
    ws = [_nn(jnp.concatenate([d["uw"][:, DK:], d["qd"]], axis=0), s) for d, s in zip(items, ss)]
    vnew = [d["uw"][:, :DK] - w[:CHUNK] for d, w in zip(items, ws)]
    outs = [w[CHUNK:] + _nn(d["qk"], vn) for d, w, vn in zip(items, ws, vnew)]
    ss = [d["dl"] * s + _tn(d["kd"], vn) for d, s, vn in zip(items, ss, vnew)]
    return outs, ss, vnew


def _dn_bwd_chunk(items, ss, dss):
    ii, jj, strict, eye = _dn_decay(items)
    for d in items:
        kq = jnp.concatenate([d["k"], d["q"]], axis=0)
        gram = _nt(kq, kq)
        d["kk"] = gram[:CHUNK, :CHUNK]
        d["a"] = d["b"] * d["kk"] * d["gam_s"]
        d["qk"] = gram[CHUNK:, :CHUNK] * d["gam_i"]
        d["qkt"] = gram[:CHUNK, CHUNK:] * jnp.where(ii <= jj, jnp.exp(jnp.where(ii <= jj, -d["gdiff"], 0.0)), 0.0)
    for d, s, ds2 in zip(items, ss, dss):
        d["dvnew"] = _nn(jnp.concatenate([d["qkt"], d["kd"]], axis=1), jnp.concatenate([d["do"], ds2], axis=0))
        r = _nt(d["do"], jnp.concatenate([d["vnew"], s], axis=0))
        d["dqk"], d["dqd"] = r[:, :CHUNK], r[:, CHUNK:]
        d["dkd"] = _nt(d["vnew"], ds2)
    new_dss = []
    for d, s, ds2 in zip(items, ss, dss):
        d["dw"] = -_nt(d["dvnew"], s)
        new_dss.append(_tn_mxu(jnp.concatenate([d["qd"], -d["uw"][:, DK:]], axis=0),
                               jnp.concatenate([d["do"], d["dvnew"]], axis=0)) + d["dl"] * ds2)
        d["ddl"] = _sum0(_sum1(s * ds2))
    for d in items:
        d["drhs"] = _tn_mxu(d["t"], jnp.concatenate([d["dvnew"], d["dw"]], axis=1))
    for d in items:
        d["da"] = jnp.where(strict, -_nt(d["drhs"], d["uw"]), 0.0)
    outs = []
    for d in items:
        q, k, v, b, egc = d["q"], d["k"], d["v"], d["b"], d["egc"]
        drhs_u, drhs_w, da = d["drhs"][:, :DK], d["drhs"][:, DK:], d["da"]
        dbeta = _sum1(da * d["kk"] * d["gam_s"]) + _sum1(drhs_u * v) + _sum1(drhs_w * k) * egc
        dkk = da * b * d["gam_s"]
        e = da * d["a"] + d["dqk"] * d["qk"]
        s_kd = _sum1(d["dkd"] * d["kd"])
        dgc_col = _sum1(e) + _sum1(drhs_w * d["rhs_w"]) + _sum1(d["dqd"] * d["qd"]) - s_kd
        dgc_row = _sum0(jnp.where(eye, dgc_col, 0.0)) - _sum0(e)
        dg = _sum1(jnp.where(jj >= ii, dgc_row, 0.0)) + (_sum0(s_kd) + d["ddl"] * d["dl"])
        x = jnp.concatenate([d["dqk"] * d["gam_i"], dkk], axis=0)
        x1 = _nn(x, k)
        x2 = _tn_mxu(x, jnp.concatenate([q, k], axis=0))
        dq = x1[:CHUNK] + d["dqd"] * egc
        dk = x2 + x1[CHUNK:] + drhs_w * (b * egc) + d["dkd"] * d["ekd"]
        outs.append((dq, dk, drhs_u * b, dbeta, dg))
    return outs, new_dss


def _short_conv(raw, halo, w):
    tb = raw.shape[0]
    ext = jnp.concatenate([halo, raw], axis=0)
    out = ext[5:5 + tb] * w[0:1]
    for j in range(1, SHORT_K):
        out = out + ext[5 + j:5 + j + tb] * w[j:j + 1]
    return out


def _dn_heads(pre, ba, alog8, dtb8):
    sp = _sig(pre)
    act = pre * sp
    bsig = _sig(ba)
    xg = ba + dtb8
    ea8 = jnp.exp(alog8)
    gfull = -ea8 * _softplus(xg)
    items = []
    for h in range(HEADS):
        qc, kc, vc = (act[:, n * D_MODEL + h * DK:n * D_MODEL + (h + 1) * DK] for n in range(3))
        nq = lax.rsqrt(_sum1(qc * qc) + EPS)
        nk = lax.rsqrt(_sum1(kc * kc) + EPS)
        items.append(dict(q=qc * nq * (DK ** -0.5), k=kc * nk, v=vc, b=bsig[:, h:h + 1],
                          g=gfull[:, HEADS + h:HEADS + h + 1], qc=qc, kc=kc, nq=nq, nk=nk))
    return items, sp, xg, ea8


def _dn_out(o, z, sz, og):
    r = lax.rsqrt(_mean(o * o) + EPS)
    xh = o * r
    return xh * og * (z * sz), xh, r


def _sds(shape, dtype):
    return jax.ShapeDtypeStruct(tuple(shape), dtype)


def _matmul(a, b, out_dtype, tm, tn, tk, name):
    m, kd = a.shape
    n = b.shape[1]
    tm, tn, tk = min(tm, m), min(tn, n), min(tk, kd)
    nk = kd // tk

    def body(a_ref, b_ref, o_ref, acc):
        @pl.when(pl.program_id(2) == 0)
        def _():
            acc[...] = jnp.zeros_like(acc)

        acc[...] += jnp.dot(a_ref[...], b_ref[...], preferred_element_type=F32)

        @pl.when(pl.program_id(2) == nk - 1)
        def _():
            o_ref[...] = acc[...].astype(o_ref.dtype)

    return _pcall(
        body, name=name, grid=(m // tm, n // tn, nk),
        in_specs=[pl.BlockSpec((tm, tk), lambda i, j, k: (i, k)), pl.BlockSpec((tk, tn), lambda i, j, k: (k, j))],
        out_specs=pl.BlockSpec((tm, tn), lambda i, j, k: (i, j)), out_shape=_sds((m, n), out_dtype),
        scratch_shapes=[pltpu.VMEM((tm, tn), F32)], compiler_params=_cparams(("parallel", "parallel", "arbitrary")),
    )(a, b)


def _matmul_ksegs(a_segs, b, tm, tn, tk, name):
    m, n = a_segs[0].shape[0], b.shape[1]
    tm, tn = min(tm, m), min(tn, n)
    main, tail = a_segs[:-1], a_segs[-1]
    steps = [s.shape[1] // tk for s in main]
    starts = [sum(steps[:i]) for i in range(len(main))]
    nk = sum(steps)
    wt = tail.shape[1]

    def body(*refs):
        a_refs, at_ref, b_ref, bt_ref, o_ref, acc = refs[:len(main)], *refs[len(main):]
        k = pl.program_id(2)

        @pl.when(k == 0)
        def _():
            acc[...] = jnp.zeros_like(acc)

        for a_ref, k0, ns in zip(a_refs, starts, steps):
            @pl.when((k >= k0) & (k < k0 + ns))
            def _():
                acc[...] += jnp.dot(a_ref[...], b_ref[...], preferred_element_type=F32)

        @pl.when(k == nk)
        def _():
            o_ref[...] = acc[...] + jnp.dot(at_ref[...], bt_ref[...], preferred_element_type=F32)

    seg_specs = [pl.BlockSpec((tm, tk), functools.partial(lambda i, j, k, k0, ns: (i, jnp.clip(k - k0, 0, ns - 1)), k0=k0, ns=ns))
                 for k0, ns in zip(starts, steps)]
    return _pcall(
        body, name=name, grid=(m // tm, n // tn, nk + 1),
        in_specs=seg_specs + [pl.BlockSpec((tm, wt), lambda i, j, k: (i, 0)),
                              pl.BlockSpec((tk, tn), lambda i, j, k: (jnp.minimum(k, nk - 1), j)),
                              pl.BlockSpec((wt, tn), lambda i, j, k: (nk * tk // wt, j))],
        out_specs=pl.BlockSpec((tm, tn), lambda i, j, k: (i, j)), out_shape=_sds((m, n), F32),
        scratch_shapes=[pltpu.VMEM((tm, tn), F32)], compiler_params=_cparams(("parallel", "parallel", "arbitrary")),
    )(*main, tail, b, b)


def _rms_fwd_call(x, g, name):
    m = x.shape[0]
    tm = min(512, m)

    def body(x_ref, g_ref, o_ref):
        xv = x_ref[...]
        o_ref[...] = (xv * lax.rsqrt(_mean(xv * xv) + EPS) * g_ref[...]).astype(BF16)

    return _pcall(
        body, name=name, grid=(m // tm,),
        in_specs=[pl.BlockSpec((tm, D_MODEL), lambda i: (i, 0)), pl.BlockSpec((1, D_MODEL), lambda i: (0, 0))],
        out_specs=pl.BlockSpec((tm, D_MODEL), lambda i: (i, 0)), out_shape=_sds((m, D_MODEL), BF16),
        compiler_params=_cparams(("parallel",)),
    )(x, g)


def _rms_bwd_call(x, dh, g, dres, name):
    m = x.shape[0]
    tm = min(512, m)

    def body(x_ref, dh_ref, g_ref, dr_ref, dx_ref, dg_ref):
        @pl.when(pl.program_id(0) == 0)
        def _():
            dg_ref[...] = jnp.zeros_like(dg_ref)

        xv, dhv = x_ref[...], dh_ref[...]
        r = lax.rsqrt(_mean(xv * xv) + EPS)
        xh = xv * r
        dx_ref[...] = _rms_bwd(dhv * g_ref[...], xh, r) + dr_ref[...]
        dg_ref[...] += _rows8(_sum0(dhv * xh))

    row = pl.BlockSpec((tm, D_MODEL), lambda i: (i, 0))
    return _pcall(
        body, name=name, grid=(m // tm,),
        in_specs=[row, row, pl.BlockSpec((1, D_MODEL), lambda i: (0, 0)), row],
        out_specs=[row, pl.BlockSpec((8, D_MODEL), lambda i: (0, 0))],
        out_shape=[_sds((m, D_MODEL), F32), _sds((8, D_MODEL), F32)], compiler_params=_cparams(("arbitrary",)),
    )(x, dh, g, dres)


def _loss_call(x, tgt, g):
    m = x.shape[0]
    tm = min(512, m)

    def body(x_ref, t_ref, g_ref, dx_ref, dg_ref, l_ref):
        @pl.when(pl.program_id(0) == 0)
        def _():
            dg_ref[...] = jnp.zeros_like(dg_ref)
            l_ref[...] = jnp.zeros_like(l_ref)

        xv = x_ref[...]
        r = lax.rsqrt(_mean(xv * xv) + EPS)
        xh = xv * r
        err = xh * g_ref[...] - t_ref[...]
        dy = err * (1.0 / D_MODEL)
        dx_ref[...] = _rms_bwd(dy * g_ref[...], xh, r)
        dg_ref[...] += _rows8(_sum0(dy * xh))
        l_ref[...] += 0.5 * _sum0(_mean(err * err))

    row = pl.BlockSpec((tm, D_MODEL), lambda i: (i, 0))
    return _pcall(
        body, name="loss_head", grid=(m // tm,),
        in_specs=[row, row, pl.BlockSpec((1, D_MODEL), lambda i: (0, 0))],
        out_specs=[row, pl.BlockSpec((8, D_MODEL), lambda i: (0, 0)), pl.BlockSpec((8, 128), lambda i: (0, 0))],
        out_shape=[_sds((m, D_MODEL), F32), _sds((8, D_MODEL), F32), _sds((8, 128), F32)],
        compiler_params=_cparams(("arbitrary",)),
    )(x, tgt, g)


def _halo_idx(i, rows):
    return jnp.maximum(i * (TB // rows) - 1, 0)


def _a_tiles(nt, rev):
    ti = (lambda i: nt - 1 - i) if rev else (lambda i: i)
    c0 = A_OFF // CONV_W
    return [pl.BlockSpec((1, TB, CONV_W), functools.partial(lambda b, i, c: (b, ti(i), c), c=c0 + c)) for c in range(3)]


def _a_fwd_call(proj, w, bias, g, b, name):
    bsz, t, _ = proj.shape
    nt = t // TB
    c0 = A_OFF // CONV_W

    def body(val_ref, glu_ref, az_ref, vh_ref, gh_ref, w_ref, bias_ref, g_ref, b_ref, y_ref, c_ref):
        keep = jnp.where(pl.program_id(1) > 0, 1.0, 0.0)
        y, c = _a_fwd(val_ref[0], glu_ref[0], az_ref[0], vh_ref[0] * keep, gh_ref[0], w_ref[...], bias_ref[...],
                      g_ref[...], b_ref[...])
        y_ref[0] = y.astype(BF16)
        c_ref[0] = c

    halo = [pl.BlockSpec((1, HALO_A, CONV_W), functools.partial(lambda b, i, c: (b, _halo_idx(i, HALO_A), c), c=c0 + c))
            for c in range(2)]
    par = [pl.BlockSpec((HALO_A, CONV_W), lambda b, i: (0, 0))] + [pl.BlockSpec((1, CONV_W), lambda b, i: (0, 0))] * 3
    tile = pl.BlockSpec((1, TB, CONV_W), lambda b, i: (b, i, 0))
    return _pcall(
        body, name=name, grid=(bsz, nt), in_specs=_a_tiles(nt, False) + halo + par, out_specs=[tile, tile],
        out_shape=[_sds((bsz, t, CONV_W), BF16), _sds((bsz, t, CONV_W), F32)],
        compiler_params=_cparams(("parallel", "parallel")),
    )(proj, proj, proj, proj, proj, w, bias, g, b)


def _a_bwd_call(dy, conv, proj, w, g, b, name):
    bsz, t, _ = proj.shape
    nt = t // TB

    def body(dy_ref, c_ref, val_ref, glu_ref, az_ref, w_ref, g_ref, b_ref, da_ref, dw_ref, ds_ref, carry):
        ip = pl.program_id(1)

        @pl.when((pl.program_id(0) == 0) & (ip == 0))
        def _():
            dw_ref[...] = jnp.zeros_like(dw_ref)
            ds_ref[...] = jnp.zeros_like(ds_ref)

        @pl.when(ip == 0)
        def _():
            carry[...] = jnp.zeros_like(carry)

        dval, dglu, daz, dw, dbias, dg, db, head = _a_bwd(dy_ref[0], val_ref[0], glu_ref[0], az_ref[0], c_ref[0],
                                                          w_ref[...], g_ref[...], b_ref[...], carry[...])
        carry[...] = head
        da_ref[0] = jnp.concatenate([dval, dglu, daz], axis=1).astype(BF16)
        dw_ref[...] += dw
        ds_ref[...] += _rows8(dbias, dg, db)

    rtile = lambda b, i: (b, nt - 1 - i, 0)
    par = [pl.BlockSpec((HALO_A, CONV_W), lambda b, i: (0, 0))] + [pl.BlockSpec((1, CONV_W), lambda b, i: (0, 0))] * 2
    return _pcall(
        body, name=name, grid=(bsz, nt),
        in_specs=[pl.BlockSpec((1, TB, CONV_W), rtile)] * 2 + _a_tiles(nt, True) + par,
        out_specs=[pl.BlockSpec((1, TB, 3 * CONV_W), rtile), pl.BlockSpec((HALO_A, CONV_W), lambda b, i: (0, 0)),
                   pl.BlockSpec((8, CONV_W), lambda b, i: (0, 0))],
        out_shape=[_sds((bsz, t, 3 * CONV_W), BF16), _sds((HALO_A, CONV_W), F32), _sds((8, CONV_W), F32)],
        scratch_shapes=[pltpu.VMEM((HALO_A, CONV_W), F32)], compiler_params=_cparams(("arbitrary", "arbitrary")),
    )(dy, conv, proj, proj, proj, w, g, b)


def _c_specs():
    c0 = C_OFF // SG_W
    tile = [pl.BlockSpec((1, TB, SG_W), functools.partial(lambda b, i, c: (b, i, c), c=c0 + c)) for c in range(3)]
    par = [pl.BlockSpec((1, SG_W), lambda b, i: (0, 0))] * 2 + [
        pl.BlockSpec((SG_G, SG_C, SG_C), lambda b, i: (0, 0, 0)), pl.BlockSpec((SG_C, SG_W), lambda b, i: (0, 0))]
    return tile + par


def _c_fwd_call(proj, g, b, ws, bias_full, name):
    bsz, t, _ = proj.shape

    def body(cu_ref, cv_ref, cz_ref, g_ref, b_ref, ws_ref, bf_ref, y_ref):
        y, _ = _c_fwd(cu_ref[0], cv_ref[0], cz_ref[0], g_ref[...], b_ref[...], ws_ref[...], bf_ref[...])
        y_ref[0] = y.astype(BF16)

    return _pcall(
        body, name=name, grid=(bsz, t // TB), in_specs=_c_specs(),
        out_specs=pl.BlockSpec((1, TB, SG_W), lambda b, i: (b, i, 0)), out_shape=_sds((bsz, t, SG_W), BF16),
        compiler_params=_cparams(("parallel", "parallel")),
    )(proj, proj, proj, g, b, ws, bias_full)


def _c_bwd_call(dy, proj, g, b, ws, bias_full, name):
    bsz, t, _ = proj.shape

    def body(dy_ref, cu_ref, cv_ref, cz_ref, g_ref, b_ref, ws_ref, bf_ref, dc_ref, dws_ref, dbs_ref, ds_ref):
        @pl.when((pl.program_id(0) == 0) & (pl.program_id(1) == 0))
        def _():
            dws_ref[...] = jnp.zeros_like(dws_ref)
            dbs_ref[...] = jnp.zeros_like(dbs_ref)
            ds_ref[...] = jnp.zeros_like(ds_ref)

        dcu, dcv, dcz, dws, dbs, dg, db = _c_bwd(dy_ref[0], cu_ref[0], cv_ref[0], cz_ref[0], g_ref[...], b_ref[...],
                                                 ws_ref[...], bf_ref[...])
        dc_ref[0] = jnp.concatenate([dcu, dcv, dcz], axis=1).astype(BF16)
        for gi in range(SG_G):
            dws_ref[gi] += dws[gi]
        dbs_ref[...] += dbs
        ds_ref[...] += _rows8(dg, db)

    tile = lambda b, i: (b, i, 0)
    return _pcall(
        body, name=name, grid=(bsz, t // TB), in_specs=[pl.BlockSpec((1, TB, SG_W), tile)] + _c_specs(),
        out_specs=[pl.BlockSpec((1, TB, 3 * SG_W), tile), pl.BlockSpec((SG_G, SG_C, SG_C), lambda b, i: (0, 0, 0)),
                   pl.BlockSpec((SG_C, SG_W), lambda b, i: (0, 0)), pl.BlockSpec((8, SG_W), lambda b, i: (0, 0))],
        out_shape=[_sds((bsz, t, 3 * SG_W), BF16), _sds((SG_G, SG_C, SG_C), F32), _sds((SG_C, SG_W), F32),
                   _sds((8, SG_W), F32)],
        compiler_params=_cparams(("arbitrary", "arbitrary")),
    )(dy, proj, proj, proj, g, b, ws, bias_full)


def _m_specs():
    g0 = G_OFF // D_MODEL
    y = [pl.BlockSpec((1, TB, n), lambda b, i: (b, i, 0)) for n in (CONV_W, D_MODEL, SG_W)]
    gates = [pl.BlockSpec((1, TB, D_MODEL), functools.partial(lambda b, i, c: (b, i, c), c=g0 + c)) for c in range(3)]
    return y + gates


def _w_specs(*shapes):
    return [pl.BlockSpec(s, lambda b, i: (0, 0)) for s in shapes]


def _m_fwd_call(ya, yb, yc, proj, x, wa, wb, wc, wo, name):
    bsz, t, _ = x.shape

    def body(ya_ref, yb_ref, yc_ref, g0_ref, g1_ref, g2_ref, x_ref, wa_ref, wb_ref, wc_ref, wo_ref, o_ref):
        merged, _ = _m_fwd(ya_ref[0], yb_ref[0], yc_ref[0], g0_ref[0], g1_ref[0], g2_ref[0], wa_ref[...], wb_ref[...],
                           wc_ref[...])
        o_ref[0] = x_ref[0] + _nn(merged, wo_ref[...])

    tile = pl.BlockSpec((1, TB, D_MODEL), lambda b, i: (b, i, 0))
    return _pcall(
        body, name=name, grid=(bsz, t // TB),
        in_specs=_m_specs() + [tile] + _w_specs(wa.shape, wb.shape, wc.shape, wo.shape),
        out_specs=tile, out_shape=_sds(x.shape, F32), compiler_params=_cparams(("parallel", "parallel")),
    )(ya, yb, yc, proj, proj, proj, x, wa, wb, wc, wo)


def _m_bwd_call(dout, ya, yb, yc, proj, wa, wb, wc, wo, name):
    bsz, t, _ = dout.shape

    def body(do_ref, ya_ref, yb_ref, yc_ref, g0_ref, g1_ref, g2_ref, wa_ref, wb_ref, wc_ref, wo_ref, dya_ref, dyb_ref,
             dyc_ref, dg_ref, mg_ref, dp_ref):
        dys, dgs, merged, dps = _m_bwd(do_ref[0], ya_ref[0], yb_ref[0], yc_ref[0], g0_ref[0], g1_ref[0], g2_ref[0],
                                       wa_ref[...], wb_ref[...], wc_ref[...], wo_ref[...])
        dya_ref[0], dyb_ref[0], dyc_ref[0] = dys
        dg_ref[0] = jnp.concatenate(dgs, axis=1).astype(BF16)
        mg_ref[0] = merged.astype(BF16)
        dp_ref[0] = jnp.concatenate(dps, axis=1).astype(BF16)

    tile = lambda n: pl.BlockSpec((1, TB, n), lambda b, i: (b, i, 0))
    widths = (CONV_W, D_MODEL, SG_W, 3 * D_MODEL, D_MODEL, 3 * D_MODEL)
    dts = (F32, F32, F32, BF16, BF16, BF16)
    return _pcall(
        body, name=name, grid=(bsz, t // TB),
        in_specs=[tile(D_MODEL)] + _m_specs() + _w_specs(wa.shape, wb.shape, wc.shape, wo.shape),
        out_specs=[tile(n) for n in widths], out_shape=[_sds((bsz, t, n), d) for n, d in zip(widths, dts)],
        compiler_params=_cparams(("parallel", "parallel")),
    )(dout, ya, yb, yc, proj, proj, proj, wa, wb, wc, wo)


def _m_wgrad_call(ya, yb, yc, merged, dps, dout, name):
    bsz, t, _ = dout.shape

    def body(ya_ref, yb_ref, yc_ref, mg_ref, dp_ref, do_ref, dwa_ref, dwb_ref, dwc_ref, dwo_ref):
        @pl.when((pl.program_id(0) == 0) & (pl.program_id(1) == 0))
        def _():
            for r in (dwa_ref, dwb_ref, dwc_ref, dwo_ref):
                r[...] = jnp.zeros_like(r)

        dp = dp_ref[0]
        dwa_ref[...] += _tn(ya_ref[0], dp[:, :D_MODEL])
        dwb_ref[...] += _tn(yb_ref[0], dp[:, D_MODEL:2 * D_MODEL])
        dwc_ref[...] += _tn(yc_ref[0], dp[:, 2 * D_MODEL:])
        dwo_ref[...] += _tn(mg_ref[0], do_ref[0])

    tile = lambda n: pl.BlockSpec((1, TB, n), lambda b, i: (b, i, 0))
    shapes = ((CONV_W, D_MODEL), (D_MODEL, D_MODEL), (SG_W, D_MODEL), (D_MODEL, D_MODEL))
    return _pcall(
        body, name=name, grid=(bsz, t // TB),
        in_specs=[tile(CONV_W), tile(D_MODEL), tile(SG_W), tile(D_MODEL), tile(3 * D_MODEL), tile(D_MODEL)],
        out_specs=_w_specs(*shapes), out_shape=[_sds(s, F32) for s in shapes],
        compiler_params=_cparams(("arbitrary", "arbitrary")),
    )(ya, yb, yc, merged, dps, dout)


def _dn_specs(nc, rev):
    ti = (lambda i: nc - 1 - i) if rev else (lambda i: i)
    return [pl.BlockSpec((1, CHUNK, B_W), lambda b, i: (b, ti(i), 0)),
            pl.BlockSpec((1, CHUNK, DK), lambda b, i: (b, ti(i), BA_OFF // DK)),
            pl.BlockSpec((SHORT_K, 3 * D_MODEL), lambda b, i: (0, 0))] + [pl.BlockSpec((1, DK), lambda b, i: (0, 0))] * 3


def _dn_saved_specs(nc, rev):
    ti = (lambda i: nc - 1 - i) if rev else (lambda i: i)
    return [pl.BlockSpec((1, CHUNK, D_MODEL), lambda b, i: (b, ti(i), 0)),
            pl.BlockSpec((1, CHUNK, 3 * D_MODEL), lambda b, i: (b, ti(i), 0)),
            pl.BlockSpec((1, HEADS, 1, DK, DK), lambda b, i: (b, 0, ti(i), 0, 0)),
            pl.BlockSpec((1, 1, CHUNK, HEADS * CHUNK), lambda b, i: (b, ti(i), 0, 0)),
            pl.BlockSpec((1, CHUNK, 2 * D_MODEL), lambda b, i: (b, ti(i), 0)),
            pl.BlockSpec((1, CHUNK, D_MODEL), lambda b, i: (b, ti(i), 0))]


def _dn_fwd_call(proj, wconv, alog, dtb, og, name):
    bsz, t, _ = proj.shape
    nc = t // CHUNK

    def body(x_ref, ba_ref, w_ref, al_ref, dt_ref, og_ref, halo_ref, y_ref, o_ref, pre_ref, st_ref, t_ref, uw_ref, vn_ref,
             s_scr):
        i = pl.program_id(1)

        @pl.when(i == 0)
        def _():
            s_scr[...] = jnp.zeros_like(s_scr)

        x = x_ref[0]
        keep = jnp.where(i > 0, 1.0, 0.0)
        pre = _short_conv(x[:, :3 * D_MODEL], halo_ref[0] * keep, w_ref[...])
        pre_ref[0] = pre
        items, _, _, _ = _dn_heads(pre, ba_ref[0], al_ref[...], dt_ref[...])
        ss = [s_scr[h] for h in range(HEADS)]
        for h in range(HEADS):
            st_ref[0, h, 0] = ss[h]
        outs, ss, vnew = _dn_fwd_chunk(items, ss)
        for h in range(HEADS):
            s_scr[h] = ss[h]
        og_v = og_ref[...]
        z = x[:, 3 * D_MODEL:]
        sz = _sig(z)
        ys = [_dn_out(outs[h], z[:, h * DK:(h + 1) * DK], sz[:, h * DK:(h + 1) * DK], og_v)[0] for h in range(HEADS)]
        o_ref[0] = jnp.concatenate(outs, axis=1)
        y_ref[0] = jnp.concatenate(ys, axis=1).astype(BF16)
        t_ref[0, 0] = jnp.concatenate([d["t"] for d in items], axis=1).astype(BF16)
        uw_ref[0] = jnp.concatenate([d["uw"] for d in items], axis=1).astype(BF16)
        vn_ref[0] = jnp.concatenate(vnew, axis=1).astype(BF16)

    halo = pl.BlockSpec((1, HALO_B, 3 * D_MODEL), lambda b, i: (b, jnp.maximum(i * (CHUNK // HALO_B) - 1, 0), 0))
    return _pcall(
        body, name=name, grid=(bsz, nc), in_specs=_dn_specs(nc, False) + [halo],
        out_specs=[pl.BlockSpec((1, CHUNK, D_MODEL), lambda b, i: (b, i, 0))] + _dn_saved_specs(nc, False),
        out_shape=[_sds((bsz, t, D_MODEL), BF16), _sds((bsz, t, D_MODEL), F32), _sds((bsz, t, 3 * D_MODEL), F32),
                   _sds((bsz, HEADS, nc, DK, DK), F32), _sds((bsz, nc, CHUNK, HEADS * CHUNK), BF16),
                   _sds((bsz, t, 2 * D_MODEL), BF16), _sds((bsz, t, D_MODEL), BF16)],
        scratch_shapes=[pltpu.VMEM((HEADS, DK, DK), F32)], compiler_params=_cparams(("arbitrary", "arbitrary")),
    )(proj, proj, wconv, alog, dtb, og, proj)


def _dn_bwd_call(dy, saved, proj, wconv, alog, dtb, og, name):
    bsz, t, _ = proj.shape
    nc = t // CHUNK

    def body(dy_ref, o_ref, pre_ref, st_ref, t_ref, uw_ref, vn_ref, x_ref, ba_ref, w_ref, al_ref, dt_ref, og_ref,
             dx_ref, dba_ref, dwc_ref, dsm_ref, ds_scr, dpre_scr):
        ip = pl.program_id(1)

        @pl.when((pl.program_id(0) == 0) & (ip == 0))
        def _():
            dwc_ref[...] = jnp.zeros_like(dwc_ref)
            dsm_ref[...] = jnp.zeros_like(dsm_ref)

        @pl.when(ip == 0)
        def _():
            ds_scr[...] = jnp.zeros_like(ds_scr)
            dpre_scr[...] = jnp.zeros_like(dpre_scr)

        x, w, pre = x_ref[0], w_ref[...], pre_ref[0]
        items, sp, xg, ea8 = _dn_heads(pre, ba_ref[0], al_ref[...], dt_ref[...])
        og_v, dyv, ov, tv, uwv, vnv = og_ref[...], dy_ref[0], o_ref[0], t_ref[0, 0], uw_ref[0], vn_ref[0]
        z = x[:, 3 * D_MODEL:]
        sz = _sig(z)
        dsz = sz * (1.0 + z * (1.0 - sz))
        dzs, dog = [], jnp.zeros((1, DK), F32)
        for h, d in enumerate(items):
            hs = slice(h * DK, (h + 1) * DK)
            _, xh, r = _dn_out(ov[:, hs], z[:, hs], sz[:, hs], og_v)
            dzs.append(dyv[:, hs] * xh * og_v * dsz[:, hs])
            don = dyv[:, hs] * (z[:, hs] * sz[:, hs])
            dog = dog + _sum0(don * xh)
            d.update(do=_rms_bwd(don * og_v, xh, r), t=tv[:, h * CHUNK:(h + 1) * CHUNK],
                     uw=uwv[:, 2 * h * DK:2 * (h + 1) * DK], vnew=vnv[:, hs])
        grads, dss = _dn_bwd_chunk(items, [st_ref[0, h, 0] for h in range(HEADS)], [ds_scr[h] for h in range(HEADS)])
        for h in range(HEADS):
            ds_scr[h] = dss[h]
        dqc, dkc, dvc = [], [], []
        lanes = lax.broadcasted_iota(jnp.int32, (CHUNK, DK), 1)
        lane8 = lax.broadcasted_iota(jnp.int32, (8, DK), 1)
        row8 = lax.broadcasted_iota(jnp.int32, (8, DK), 0)
        dgdx = -ea8 * _sig(xg)
        dba = jnp.zeros((CHUNK, DK), F32)
        dsm = jnp.where(row8 == 2, dog, 0.0)
        for h, (d, (dq, dk, dv, dbeta, dg)) in enumerate(zip(items, grads)):
            qc, kc, nq, nk = d["qc"], d["kc"], d["nq"], d["nk"]
            dqc.append((DK ** -0.5) * nq * (dq - qc * (nq * nq * _sum1(dq * qc))))
            dkc.append(nk * (dk - kc * (nk * nk * _sum1(dk * kc))))
            dvc.append(dv)
            dal = dg * dgdx[:, HEADS + h:HEADS + h + 1]
            dba = dba + jnp.where(lanes == h, dbeta * d["b"] * (1.0 - d["b"]), 0.0) + jnp.where(lanes == HEADS + h, dal, 0.0)
            dsm = dsm + (jnp.where((row8 == 0) & (lane8 == h), _sum0(dg * d["g"]), 0.0)
                         + jnp.where((row8 == 1) & (lane8 == h), _sum0(dal), 0.0))
        dpre = jnp.concatenate(dqc + dkc + dvc, axis=1) * (sp * (1.0 + pre * (1.0 - sp)))
        dext = jnp.concatenate([dpre, dpre_scr[...]], axis=0)
        raw = x[:, :3 * D_MODEL]
        draw = dext[3:3 + CHUNK] * w[0:1]
        rows = [_sum0(dext[3:3 + CHUNK] * raw)]
        for j in range(1, SHORT_K):
            shifted = dext[3 - j:3 - j + CHUNK]
            draw = draw + shifted * w[j:j + 1]
            rows.append(_sum0(shifted * raw))
        dpre_scr[...] = dpre[:HALO_B]
        dx_ref[0] = jnp.concatenate([draw] + dzs, axis=1).astype(BF16)
        dba_ref[0] = dba.astype(BF16)
        dwc_ref[...] += _rows8(*rows)
        dsm_ref[...] += dsm

    rrow = lambda n: pl.BlockSpec((1, CHUNK, n), lambda b, i: (b, nc - 1 - i, 0))
    return _pcall(
        body, name=name, grid=(bsz, nc), in_specs=[rrow(D_MODEL)] + _dn_saved_specs(nc, True) + _dn_specs(nc, True),
        out_specs=[rrow(B_W), rrow(DK), pl.BlockSpec((8, 3 * D_MODEL), lambda b, i: (0, 0)),
                   pl.BlockSpec((8, DK), lambda b, i: (0, 0))],
        out_shape=[_sds((bsz, t, B_W), BF16), _sds((bsz, t, DK), BF16), _sds((8, 3 * D_MODEL), F32), _sds((8, DK), F32)],
        scratch_shapes=[pltpu.VMEM((HEADS, DK, DK), F32), pltpu.VMEM((HALO_B, 3 * D_MODEL), F32)],
        compiler_params=_cparams(("arbitrary", "arbitrary")),
    )(dy, *saved, proj, proj, wconv, alog, dtb, og)


def _adamw_call(w, g, m, v, name):
    shape = w.shape
    view = (math.prod(shape[:-2]),) + shape[-2:] if len(shape) >= 2 else (1, 1) + shape
    lead, rows, cols = view
    tr, tl = (128, 1) if rows % 128 == 0 else (rows, min(lead, 64))
    c1, c2 = 1.0 - ADAM_B1 ** ADAM_STEP, 1.0 - ADAM_B2 ** ADAM_STEP

    def body(w_ref, g_ref, m_ref, v_ref, d_ref, nm_ref, nv_ref):
        gv = g_ref[...]
        nm = ADAM_B1 * m_ref[...] + (1.0 - ADAM_B1) * gv
        nv = ADAM_B2 * v_ref[...] + (1.0 - ADAM_B2) * (gv * gv)
        d_ref[...] = -ADAM_LR * ((nm / c1) / (jnp.sqrt(nv / c2) + ADAM_EPS) + ADAM_WD * w_ref[...])
        nm_ref[...] = nm
        nv_ref[...] = nv

    blk = pl.BlockSpec((tl, tr, cols), lambda l, i: (l, i, 0))
    outs = _pcall(
        body, name=name, grid=(pl.cdiv(lead, tl), rows // tr), in_specs=[blk] * 4, out_specs=[blk] * 3,
        out_shape=[_sds(view, F32)] * 3, compiler_params=_cparams(("parallel", "parallel")),
    )(*(a.reshape(view) for a in (w, g, m, v)))
    return tuple(o.reshape(shape) for o in outs)


def _row_tile(rows, cap=512):
    for tr in range(cap, 15, -16):
        if rows % tr == 0:
            return tr
    return rows


def _add_pair_call(by_chip, recv, where, name):
    _, n, rows, cols = by_chip.shape
    tr = _row_tile(rows, 256)

    def body(where_ref, a_ref, b_ref, sb_ref, own_ref):
        s = a_ref[0, 0] + b_ref[0]
        sb_ref[0] = s.astype(BF16)

        @pl.when(pl.program_id(1) == where_ref[1])
        def _():
            own_ref[...] = s

    grid_spec = pltpu.PrefetchScalarGridSpec(
        num_scalar_prefetch=1, grid=(rows // tr, n),
        in_specs=[pl.BlockSpec((1, 1, tr, cols), lambda i, j, wr: (wr[0], j, i, 0)),
                  pl.BlockSpec((1, tr, cols), lambda i, j, wr: (j, i, 0))],
        out_specs=[pl.BlockSpec((1, tr, cols), lambda i, j, wr: (j, i, 0)),
                   pl.BlockSpec((tr, cols), lambda i, j, wr: (i, 0))])
    return _pcall(
        body, name=name, grid_spec=grid_spec, out_shape=[_sds((n, rows, cols), BF16), _sds((rows, cols), F32)],
        compiler_params=_cparams(("parallel", "arbitrary")),
    )(where, by_chip, recv)


def _add_recv_call(own, recv, where, name):
    rows, cols = own.shape
    tr = _row_tile(rows, 256)

    def body(where_ref, o_ref, r_ref, s_ref):
        s_ref[0] = ((o_ref[...] + r_ref[0].astype(F32)) + r_ref[1].astype(F32)) + r_ref[2].astype(F32)

    grid_spec = pltpu.PrefetchScalarGridSpec(
        num_scalar_prefetch=1, grid=(rows // tr,),
        in_specs=[pl.BlockSpec((tr, cols), lambda i, wr: (i, 0)), pl.BlockSpec((3, tr, cols), lambda i, wr: (0, i, 0))],
        out_specs=pl.BlockSpec((1, tr, cols), lambda i, wr: (wr[0], i, 0)))
    return _pcall(
        body, name=name, grid_spec=grid_spec, out_shape=_sds((2, rows, cols), F32),
        compiler_params=_cparams(("parallel",)),
    )(where, own, recv)


def _axes():
    return lax.axis_index("x"), lax.axis_index("y"), lax.axis_index("c")


def _chip_peers(x, y):
    return [(x, 1 - y), (1 - x, y), (1 - x, 1 - y)]


_ANY = pl.BlockSpec(memory_space=pl.ANY)
_VMEM = pl.BlockSpec(memory_space=pltpu.VMEM)


def _remote(src, dst, send_sems, recv_sems, k, dev):
    return pltpu.make_async_remote_copy(src_ref=src, dst_ref=dst, send_sem=send_sems.at[k], recv_sem=recv_sems.at[k],
                                        device_id=dev, device_id_type=MESH)


def _gather_weights_call(shards):
    n = len(shards)
    halves = [s.shape[0] // 2 for s in shards]

    def body(*refs):
        x_refs, o_refs, (send_sems, recv_sems, local_sems) = refs[:n], refs[n:2 * n], refs[2 * n:]
        x, y, c = _axes()
        chip, sib, peers = 2 * x + y, (x, y, 1 - c), _chip_peers(x, y)
        pchip = [2 * px + py for px, py in peers]

        def half(a, cc):
            return pl.ds(cc * halves[a], halves[a])

        mine = [pltpu.make_async_copy(x_refs[a], o_refs[a].at[chip], local_sems.at[a]) for a in range(n)]
        for cp in mine:
            cp.start()
        first = [_remote(x_refs[a].at[half(a, c)], o_refs[a].at[chip, half(a, c)], send_sems, recv_sems, 6 * a + k,
                         (*peers[k], c)) for k in range(3) for a in range(n)]
        for cp in first:
            cp.start()
        passed = []
        for k in range(3):
            for a in range(n):
                land = o_refs[a].at[pchip[k], half(a, c)]
                _remote(land, land, send_sems, recv_sems, 6 * a + k, (*peers[k], c)).wait_recv()
                passed.append(_remote(land, land, send_sems, recv_sems, 6 * a + 3 + k, sib))
                passed[-1].start()
        for k in range(3):
            for a in range(n):
                land = o_refs[a].at[pchip[k], half(a, 1 - c)]
                _remote(land, land, send_sems, recv_sems, 6 * a + 3 + k, sib).wait_recv()
        for cp in first + passed:
            cp.wait_send()
        for cp in mine:
            cp.wait()

    return _pcall(
        body, name="gather_weights", in_specs=[_ANY] * n, out_specs=[_ANY] * n,
        out_shape=[_sds((4,) + s.shape, s.dtype) for s in shards],
        scratch_shapes=[pltpu.SemaphoreType.DMA((6 * n,)), pltpu.SemaphoreType.DMA((6 * n,)),
                        pltpu.SemaphoreType.DMA((n,))],
    )(*shards)


def _pair_partials_call(by_chip):
    n = len(by_chip)

    def body(*refs):
        v_refs, o_refs, (send_sems, recv_sems) = refs[:n], refs[n:2 * n], refs[2 * n:]
        x, y, c = _axes()
        cps = [_remote(v_refs[a].at[1 - c], o_refs[a], send_sems, recv_sems, a, (x, y, 1 - c)) for a in range(n)]
        for cp in cps:
            cp.start()
        for cp in cps:
            cp.wait()

    return _pcall(
        body, name="pair_partials", in_specs=[_ANY] * n, out_specs=[_ANY] * n,
        out_shape=[_sds(v.shape[1:], v.dtype) for v in by_chip],
        scratch_shapes=[pltpu.SemaphoreType.DMA((n,)), pltpu.SemaphoreType.DMA((n,))],
    )(*by_chip)


def _scatter_partials_call(parts):
    n = len(parts)

    def body(*refs):
        v_refs, o_refs, (send_sems, recv_sems) = refs[:n], refs[n:2 * n], refs[2 * n:]
        x, y, c = _axes()
        peers = _chip_peers(x, y)
        cps = [_remote(v_refs[a].at[2 * peers[k][0] + peers[k][1]], o_refs[a].at[k], send_sems, recv_sems, 3 * a + k,
                       (*peers[k], c)) for k in range(3) for a in range(n)]
        for cp in cps:
            cp.start()
        for cp in cps:
            cp.wait()

    return _pcall(
        body, name="scatter_partials", in_specs=[_ANY] * n, out_specs=[_ANY] * n,
        out_shape=[_sds((3,) + v.shape[1:], v.dtype) for v in parts],
        scratch_shapes=[pltpu.SemaphoreType.DMA((3 * n,)), pltpu.SemaphoreType.DMA((3 * n,))],
    )(*parts)


def _pair_result_call(fins):
    n = len(fins)

    def body(*refs):
        v_refs, o_refs, (send_sems, recv_sems) = refs[:n], refs[n:2 * n], refs[2 * n:]
        x, y, c = _axes()
        cps = [_remote(v_refs[a].at[c], o_refs[a].at[c], send_sems, recv_sems, a, (x, y, 1 - c)) for a in range(n)]
        for cp in cps:
            cp.start()
        for a in range(n):
            cps[a].wait_send()
            _remote(v_refs[a].at[c], o_refs[a].at[1 - c], send_sems, recv_sems, a, (x, y, 1 - c)).wait_recv()

    return _pcall(
        body, name="pair_result", in_specs=[_ANY] * n, out_specs=[_ANY] * n,
        out_shape=[_sds(v.shape, v.dtype) for v in fins], input_output_aliases={a: a for a in range(n)},
        scratch_shapes=[pltpu.SemaphoreType.DMA((n,)), pltpu.SemaphoreType.DMA((n,))],
    )(*fins)


def _allreduce_small_call(v):
    rows, cols = v.shape

    def body(v_ref, o_ref, buf, send_sems, recv_sems):
        x, y, c = _axes()
        me = 4 * x + 2 * y + c
        buf[0] = v_ref[...]
        cps = []
        for m in range(1, 8):
            dev = (1 - x if m & 4 else x, 1 - y if m & 2 else y, 1 - c if m & 1 else c)
            cps.append(_remote(v_ref, buf.at[m], send_sems, recv_sems, m - 1, dev))
        for cp in cps:
            cp.start()
        for cp in cps:
            cp.wait()
        acc = buf[me]
        for d in range(1, 8):
            acc = acc + buf[lax.bitwise_xor(me, d)]
        o_ref[...] = acc

    return _pcall(
        body, name="allreduce_small", in_specs=[_VMEM], out_specs=_VMEM, out_shape=_sds(v.shape, F32),
        scratch_shapes=[pltpu.VMEM((8, rows, cols), F32), pltpu.SemaphoreType.DMA((7,)), pltpu.SemaphoreType.DMA((7,))],
        compiler_params=_cparams(),
    )(v)


def _pack_rows(arrays, dtype, total_rows=None):
    parts = []
    for a in arrays:
        flat = a.astype(dtype).reshape(-1)
        parts.append(jnp.pad(flat, (0, -flat.shape[0] % D_MODEL)).reshape(-1, D_MODEL))
    out = jnp.concatenate(parts, axis=0)
    total_rows = total_rows or out.shape[0] + (-out.shape[0] % 8)
    return jnp.pad(out, ((0, total_rows - out.shape[0]), (0, 0)))


def _unpack_rows(packed, shapes):
    out, r = [], 0
    for s in shapes:
        n = math.prod(s)
        nr = -(-n // D_MODEL)
        out.append(packed[r:r + nr].reshape(-1)[:n].reshape(s))
        r += nr
    return out


def _to_r(w_t):
    pad = jnp.zeros((N_R - N_ORIG,) + w_t.shape[1:], w_t.dtype)
    return jnp.concatenate([w_t[A_ORIG_W:BA_ORIG], w_t[:A_ORIG_W], w_t[BA_ORIG + 2 * HEADS:],
                            w_t[BA_ORIG:BA_ORIG + 2 * HEADS], pad], axis=0)


_BIG = ("w_in", "a_proj", "b_proj", "c_proj", "w_out")
_SHARD_AXIS = {"w_in": 2, "a_proj": 2, "b_proj": 1, "c_proj": 2, "w_out": 1, "a_dw": 2, "b_conv": 2}
_BIG_AXIS = _SHARD_AXIS


def _join_chips(g, axis):
    g = jnp.moveaxis(g, 0, axis)
    return g.reshape(g.shape[:axis] + (4 * g.shape[axis + 1],) + g.shape[axis + 2:])


def _split_chips(a, axis):
    n = a.shape[axis] // 4
    return jnp.moveaxis(a.reshape(a.shape[:axis] + (4, n) + a.shape[axis + 1:]), axis, 1)
_ORDER = ("norm_g", "w_in", "a_dw", "a_dw_b", "a_ln_g", "a_ln_b", "a_proj", "b_conv", "b_a_log", "b_dt_bias",
          "b_onorm_g", "b_proj", "c_ln_g", "c_ln_b", "c_ws", "c_bs", "c_proj", "w_out", "final_g")


def _local_step(x, tgt, w):
    bsz, t, _ = x.shape
    m = bsz * t
    depth = w["norm_g"].shape[0]
    row = lambda v: v.reshape(1, -1)
    w_rt = _to_r(w["w_in_t"])
    saved, xl = [], x
    for l in range(depth):
        n = f"l{l}_"
        par = dict(
            adw=jnp.pad(w["a_dw"][l], ((0, 1), (0, 0))), adb=row(w["a_dw_b"][l]), alg=row(w["a_ln_g"][l]),
            alb=row(w["a_ln_b"][l]), bconv=w["b_conv"][l],
            alog=jnp.pad(row(w["b_a_log"][l]), ((0, 0), (HEADS, DK - 2 * HEADS))),
            dtb=jnp.pad(row(w["b_dt_bias"][l]), ((0, 0), (HEADS, DK - 2 * HEADS))), og=row(w["b_onorm_g"][l]),
            clg=row(w["c_ln_g"][l]), clb=row(w["c_ln_b"][l]), cws=w["c_ws"][l],
            cbias=jnp.repeat(w["c_bs"][l].T, SG_C, axis=1), ng=row(w["norm_g"][l]),
            wa=w["a_proj"][l], wb=w["b_proj"][l], wc=w["c_proj"][l], wo=w["w_out"][l], wrt=w_rt[:, l], wr=w_rt[:, l].T)
        h = _rms_fwd_call(xl.reshape(m, D_MODEL), par["ng"], n + "norm")
        proj = _matmul(h, par["wr"], F32, 1024, 1152, 1024, n + "in_proj").reshape(bsz, t, N_R)
        ya, conv = _a_fwd_call(proj, par["adw"], par["adb"], par["alg"], par["alb"], n + "conv_fwd")
        yb, *dn_saved = _dn_fwd_call(proj, par["bconv"], par["alog"], par["dtb"], par["og"], n + "delta_fwd")
        yc = _c_fwd_call(proj, par["clg"], par["clb"], par["cws"], par["cbias"], n + "gmlp_fwd")
        x_next = _m_fwd_call(ya, yb, yc, proj, xl, par["wa"], par["wb"], par["wc"], par["wo"], n + "merge_fwd")
        saved.append((par, xl, h, proj, ya, yb, yc, conv, dn_saved))
        xl = x_next
    dout, dfg, loss = _loss_call(xl.reshape(m, D_MODEL), tgt.reshape(m, D_MODEL), row(w["final_g"]))
    dout = dout.reshape(bsz, t, D_MODEL)
    g = {k: [None] * depth for k in _ORDER if k != "final_g"}
    for l in reversed(range(depth)):
        n = f"l{l}_"
        par, xl, h, proj, ya, yb, yc, conv, dn_saved = saved[l]
        dya, dyb, dyc, dgate, merged, dps = _m_bwd_call(dout, ya, yb, yc, proj, par["wa"], par["wb"], par["wc"],
                                                        par["wo"], n + "merge_bwd")
        g["a_proj"][l], g["b_proj"][l], g["c_proj"][l], g["w_out"][l] = _m_wgrad_call(ya, yb, yc, merged, dps, dout,
                                                                                      n + "merge_wgrad")
        da, dadw, dasm = _a_bwd_call(dya, conv, proj, par["adw"], par["alg"], par["alb"], n + "conv_bwd")
        db, dba, dbconv, dbsm = _dn_bwd_call(dyb, dn_saved, proj, par["bconv"], par["alog"], par["dtb"], par["og"],
                                             n + "delta_bwd")
        dc, dcws, dcbs, dcsm = _c_bwd_call(dyc, proj, par["clg"], par["clb"], par["cws"], par["cbias"], n + "gmlp_bwd")
        segs = [s.reshape(m, s.shape[-1]) for s in (db, da, dc, dgate, dba)]
        dh = _matmul_ksegs(segs, par["wrt"], 2048, 1024, 512, n + "in_proj_dx")
        ht = h.T
        dwb, dwa, dwc, dwg, dwba = [_matmul(ht, s, F32, 1024, 1024 if s.shape[1] % 1024 == 0 else 768, 1024,
                                            n + "in_proj_dw_" + tag) for s, tag in zip(segs, "bacgs")]
        g["w_in"][l] = jnp.concatenate([dwa, dwb, dwba[:, :2 * HEADS], dwc, dwg], axis=1)
        dx, dng = _rms_bwd_call(xl.reshape(m, D_MODEL), dh, par["ng"], dout.reshape(m, D_MODEL), n + "norm_bwd")
        dout = dx.reshape(bsz, t, D_MODEL)
        g["norm_g"][l] = dng[0]
        g["a_dw"][l], g["a_dw_b"][l], g["a_ln_g"][l], g["a_ln_b"][l] = dadw[:CONV_K], dasm[0], dasm[1], dasm[2]
        g["b_conv"][l], g["b_a_log"][l], g["b_dt_bias"][l] = dbconv[:SHORT_K], dbsm[0, :HEADS], dbsm[1, :HEADS]
        g["b_onorm_g"][l] = dbsm[2]
        g["c_ln_g"][l], g["c_ln_b"][l], g["c_ws"][l] = dcsm[0], dcsm[1], dcws
        g["c_bs"][l] = dcbs.reshape(SG_C, SG_G, SG_C).sum(-1).T
    grads = {k: jnp.stack(v) for k, v in g.items()}
    grads["final_g"] = dfg[0]
    return loss[0, 0], dout, grads


def kernel(x, norm_g, w_in, a_dw, a_dw_b, a_ln_g, a_ln_b, a_proj, b_conv, b_a_log, b_dt_bias, b_onorm_g, b_proj, c_ln_g, c_ln_b, c_ws, c_bs, c_proj, w_out, final_g, loss_target, m_norm_g, m_w_in, m_a_dw, m_a_dw_b, m_a_ln_g, m_a_ln_b, m_a_proj, m_b_conv, m_b_a_log, m_b_dt_bias, m_b_onorm_g, m_b_proj, m_c_ln_g, m_c_ln_b, m_c_ws, m_c_bs, m_c_proj, m_w_out, m_final_g, v_norm_g, v_w_in, v_a_dw, v_a_dw_b, v_a_ln_g, v_a_ln_b, v_a_proj, v_b_conv, v_b_a_log, v_b_dt_bias, v_b_onorm_g, v_b_proj, v_c_ln_g, v_c_ln_b, v_c_ws, v_c_bs, v_c_proj, v_w_out, v_final_g):
    given = dict(locals())
    ws = {k: given[k] for k in _ORDER}
    xi, yi, ci = _axes()
    chip = 2 * xi + yi
    where = jnp.stack([ci, chip]).astype(jnp.int32)

    others = _BIG[1:] + ("a_dw", "b_conv")
    gathered = _gather_weights_call([jnp.transpose(w_in, (2, 0, 1)).astype(BF16)] + [ws[k].astype(BF16) for k in _BIG[1:]]
                                    + [a_dw, b_conv])
    full = {k: v for k, v in ws.items() if k != "w_in"}
    full["w_in_t"] = gathered[0].reshape((N_ORIG,) + gathered[0].shape[2:])
    for k, g in zip(others, gathered[1:]):
        full[k] = _join_chips(g, _SHARD_AXIS[k])

    loss, grad_x, grads = _local_step(x, loss_target, full)

    by_chip = [_split_chips(grads[k], _BIG_AXIS[k]) for k in _BIG]
    theirs = _pair_partials_call(by_chip)
    sums = [_add_pair_call(b, r, where, "pair_sum_" + k) for k, b, r in zip(_BIG, by_chip, theirs)]
    got = _scatter_partials_call([s[0] for s in sums])
    fins = [_add_recv_call(s[1], r, where, "chip_sum_" + k) for k, s, r in zip(_BIG, sums, got)]
    out_g = dict(zip(_BIG, _pair_result_call(fins)))

    rest = [k for k in _ORDER if k not in _BIG]
    rest_shapes = [grads[k].shape for k in rest] + [(1,)]
    summed = _unpack_rows(_allreduce_small_call(_pack_rows([grads[k] for k in rest] + [loss.reshape(1)], F32)),
                          rest_shapes)
    out_g.update(zip(rest, summed[:-1]))
    out_g["a_dw"] = lax.dynamic_slice_in_dim(out_g["a_dw"], chip * a_dw.shape[2], a_dw.shape[2], axis=2)
    out_g["b_conv"] = lax.dynamic_slice_in_dim(out_g["b_conv"], chip * b_conv.shape[2], b_conv.shape[2], axis=2)

    def adam(k):
        operands = (ws[k], out_g[k], given["m_" + k], given["v_" + k])
        if k != "w_in":
            return _adamw_call(*operands, "adamw_" + k)
        w_t, g_t, m_t, v_t = (jnp.transpose(a, (2, 0, 1)) for a in operands)
        out_g[k] = jnp.transpose(g_t, (1, 2, 0))
        return tuple(jnp.transpose(o, (1, 2, 0)) for o in _adamw_call(w_t, g_t, m_t, v_t, "adamw_" + k))

    upd = {k: adam(k) for k in _ORDER}
    return (summed[-1][0], grad_x, *[out_g[k] for k in _ORDER], *[upd[k][0] for k in _ORDER],
            *[upd[k][1] for k in _ORDER], *[upd[k][2] for k in _ORDER])
```

```python
import functools
import math

import jax
import jax.numpy as jnp
from jax import lax
from jax.experimental import pallas as pl
from jax.experimental.pallas import tpu as pltpu

F32 = jnp.float32
BF16 = jnp.bfloat16
MESH = pl.DeviceIdType.MESH

D_MODEL = 1024
CONV_W = 512
CONV_K = 31
HEADS = 8
DK = 128
SHORT_K = 4
CHUNK = 64
SG_W = 512
SG_G = 4
SG_C = 128
EPS = 1e-6
N_ORIG = 10256
BA_ORIG = 5632
A_ORIG_W = 3 * CONV_W
B_W = 4 * D_MODEL
B_OFF, A_OFF, C_OFF, G_OFF, BA_OFF, N_R = 0, 4096, 5632, 7168, 10240, 10368
TB = 256
HALO_A = 32
HALO_B = 8
VMEM_LIMIT = 56 * 1024 * 1024
ADAM_LR, ADAM_B1, ADAM_B2, ADAM_EPS, ADAM_WD, ADAM_STEP = 0.001, 0.9, 0.999, 1e-08, 0.01, 10


def _pcall(body, **kw):
    return pl.pallas_call(body, **kw)


def _cparams(sem=None):
    kw = dict(vmem_limit_bytes=VMEM_LIMIT)
    if sem is not None:
        kw["dimension_semantics"] = sem
    return pltpu.CompilerParams(**kw)


def _sig(x):
    return jax.nn.sigmoid(x)


def _silu(x):
    return x * _sig(x)


def _dsilu(x):
    s = _sig(x)
    return s * (1.0 + x * (1.0 - s))


_GELU_C = math.sqrt(2.0 / math.pi)


def _gelu(x):
    return 0.5 * x * (1.0 + jnp.tanh(_GELU_C * (x + 0.044715 * x * x * x)))


def _dgelu(x):
    t = jnp.tanh(_GELU_C * (x + 0.044715 * x * x * x))
    return 0.5 * (1.0 + t) + 0.5 * x * (1.0 - t * t) * _GELU_C * (1.0 + 3 * 0.044715 * x * x)


def _softplus(x):
    return jnp.maximum(x, 0.0) + jnp.log1p(jnp.exp(-jnp.abs(x)))


def _dot(a, b, dims):
    return lax.dot_general(a.astype(BF16), b.astype(BF16), (dims, ((), ())), preferred_element_type=F32)


def _nn(a, b):
    return _dot(a, b, ((1,), (0,)))


def _nt(a, b):
    return _dot(a, b, ((1,), (1,)))


def _tn(a, b):
    return _dot(a, b, ((0,), (0,)))


def _tn_mxu(a, b):
    n = a.shape[1]
    eye = (lax.broadcasted_iota(jnp.int32, (n, n), 0) == lax.broadcasted_iota(jnp.int32, (n, n), 1)).astype(BF16)
    return _nn(_nt(eye, a), b)


def _mean(x):
    return jnp.mean(x, axis=-1, keepdims=True)


def _sum0(x):
    return jnp.sum(x, axis=0, keepdims=True)


def _sum1(x):
    return jnp.sum(x, axis=1, keepdims=True)


def _ln_fwd(x, g, b):
    xc = x - _mean(x)
    r = lax.rsqrt(_mean(xc * xc) + EPS)
    xh = xc * r
    return xh * g + b, xh, r


def _ln_bwd(dy, xh, r, g):
    dxh = dy * g
    return r * (dxh - _mean(dxh) - xh * _mean(dxh * xh)), _sum0(dy * xh), _sum0(dy)


def _rms_bwd(dxh, xh, r):
    return r * (dxh - xh * _mean(dxh * xh))


def _rows8(*rows):
    n = rows[0].shape[1]
    return jnp.concatenate(list(rows) + [jnp.zeros((8 - len(rows), n), F32)], axis=0)


def _windows(ext, tb):
    n = ext.shape[0] - 8
    shifted = {0: ext}

    def window(off):
        r = off % 8
        if r not in shifted:
            shifted[r] = ext[r:r + n]
        return shifted[r][off - r:off - r + tb]

    return window


def _a_fwd(val, glu, az, val_h, glu_h, w, bias, g, b):
    tb = val.shape[0]
    win = _windows(jnp.concatenate([val_h * _sig(glu_h), val * _sig(glu)], axis=0), tb)
    c = win(2) * w[0:1]
    for j in range(1, CONV_K):
        c = c + win(2 + j) * w[j:j + 1]
    c = c + bias
    ln, _, _ = _ln_fwd(c, g, b)
    return _silu(ln) * _silu(az), c


def _a_bwd(dy, val, glu, az, c, w, g, b, dc_next):
    tb = val.shape[0]
    ln, xh, r = _ln_fwd(c, g, b)
    sl, sz = _sig(ln), _sig(az)
    dln = dy * (az * sz) * (sl * (1.0 + ln * (1.0 - sl)))
    daz = dy * (ln * sl) * (sz * (1.0 + az * (1.0 - sz)))
    dc, dg, db = _ln_bwd(dln, xh, r, g)
    win = _windows(jnp.concatenate([dc, dc_next], axis=0), tb)
    sg = _sig(glu)
    a = val * sg
    da = win(30) * w[0:1]
    dw_rows = [_sum0(win(30) * a)]
    for j in range(1, CONV_K):
        shifted = win(30 - j)
        da = da + shifted * w[j:j + 1]
        dw_rows.append(_sum0(shifted * a))
    dw = jnp.concatenate(dw_rows + [jnp.zeros((1, CONV_W), F32)], axis=0)
    return da * sg, da * val * sg * (1.0 - sg), daz, dw, _sum0(dc), dg, db, dc[:HALO_A]


def _tril(ws):
    ii = lax.broadcasted_iota(jnp.int32, (SG_C, SG_C), 0)
    jj = lax.broadcasted_iota(jnp.int32, (SG_C, SG_C), 1)
    return [jnp.where(jj <= ii, ws[gi], 0.0) for gi in range(SG_G)], jj <= ii


def _c_mix(wt, vs, bias_full):
    tb = vs.shape[0]
    rows = []
    for n in range(tb // SG_C):
        blks = [_nn(wt[gi], vs[n * SG_C:(n + 1) * SG_C, gi * SG_C:(gi + 1) * SG_C]) for gi in range(SG_G)]
        rows.append(jnp.concatenate(blks, axis=1) + bias_full)
    return jnp.concatenate(rows, axis=0)


def _c_fwd(cu, cv, cz, g, b, ws, bias_full):
    wt, _ = _tril(ws)
    vs, xh, r = _ln_fwd(_gelu(cv), g, b)
    mixed = _c_mix(wt, vs, bias_full)
    return _gelu(cu) * mixed * _silu(cz), (wt, vs, xh, r, mixed)


def _c_bwd(dy, cu, cv, cz, g, b, ws, bias_full):
    tb = cu.shape[0]
    _, (wt, vs, xh, r, mixed) = _c_fwd(cu, cv, cz, g, b, ws, bias_full)
    _, low = _tril(ws)
    u, sz = _gelu(cu), _silu(cz)
    dcu = dy * mixed * sz * _dgelu(cu)
    dcz = dy * u * mixed * _dsilu(cz)
    dmixed = dy * u * sz
    dbs = jnp.zeros((SG_C, SG_W), F32)
    dws = [jnp.zeros((SG_C, SG_C), F32) for _ in range(SG_G)]
    rows = []
    for n in range(tb // SG_C):
        dm_n = dmixed[n * SG_C:(n + 1) * SG_C]
        dbs = dbs + dm_n
        blks = []
        for gi in range(SG_G):
            dm = dm_n[:, gi * SG_C:(gi + 1) * SG_C]
            dws[gi] = dws[gi] + _nt(dm, vs[n * SG_C:(n + 1) * SG_C, gi * SG_C:(gi + 1) * SG_C])
            blks.append(_tn(wt[gi], dm))
        rows.append(jnp.concatenate(blks, axis=1))
    dvs = jnp.concatenate(rows, axis=0)
    dgv, dg, db = _ln_bwd(dvs, xh, r, g)
    dws = [jnp.where(low, d, 0.0) for d in dws]
    return dcu, dgv * _dgelu(cv), dcz, dws, dbs, dg, db


def _m_fwd(ya, yb, yc, g0, g1, g2, wa, wb, wc):
    pa, pb, pc = _nn(ya, wa), _nn(yb, wb), _nn(yc, wc)
    s0, s1, s2 = _sig(g0), _sig(g1), _sig(g2)
    return s0 * pa + s1 * pb + s2 * pc, (pa, pb, pc, s0, s1, s2)


def _m_bwd(dout, ya, yb, yc, g0, g1, g2, wa, wb, wc, wo):
    merged, (pa, pb, pc, s0, s1, s2) = _m_fwd(ya, yb, yc, g0, g1, g2, wa, wb, wc)
    dm = _nt(dout, wo)
    dpa, dpb, dpc = dm * s0, dm * s1, dm * s2
    dgs = (dm * pa * s0 * (1.0 - s0), dm * pb * s1 * (1.0 - s1), dm * pc * s2 * (1.0 - s2))
    return (_nt(dpa, wa), _nt(dpb, wb), _nt(dpc, wc)), dgs, merged, (dpa, dpb, dpc)


def _chunk_masks(c):
    ii = lax.broadcasted_iota(jnp.int32, (c, c), 0)
    jj = lax.broadcasted_iota(jnp.int32, (c, c), 1)
    return ii, jj


def _dn_decay(items):
    ii, jj = _chunk_masks(CHUNK)
    incl, strict, eye = jj <= ii, jj < ii, ii == jj
    for d in items:
        g = d["g"]
        grow = _sum0(jnp.where(eye, g, 0.0))
        gc_col = _sum1(jnp.where(incl, grow, 0.0))
        gc_row = _sum0(jnp.where(ii <= jj, g, 0.0))
        gam_i = jnp.where(incl, jnp.exp(jnp.where(incl, gc_col - gc_row, 0.0)), 0.0)
        gl = _sum0(g)
        egc = jnp.exp(gc_col)
        d.update(gam_i=gam_i, gam_s=jnp.where(strict, gam_i, 0.0), egc=egc, ekd=jnp.exp(gl - gc_col), dl=jnp.exp(gl),
                 gdiff=gc_col - gc_row, qd=d["q"] * egc, rhs_w=d["k"] * (d["b"] * egc))
        d["kd"] = d["k"] * d["ekd"]
    return ii, jj, strict, eye


def _dn_solve(items, ii, jj, eye):
    off = ((ii >> 1) == (jj >> 1)) & ((ii & 1) != 0) & ((jj & 1) == 0)
    for d in items:
        a1 = jnp.where(off, d["a"], 0.0)
        d["t"] = jnp.where(eye, 1.0, 0.0) - a1
        d["m"] = d["a"] - _nn(a1, d["a"])
    b, sh = 2, 2
    while b < CHUNK:
        off = ((ii >> sh) == (jj >> sh)) & ((ii & b) != 0) & ((jj & b) == 0)
        for d in items:
            mo = jnp.where(off, d["m"], 0.0)
            if 2 * b < CHUNK:
                d["m"], d["t"] = d["m"] - _nn(mo, d["m"]), d["t"] - _nn(mo, d["t"])
            else:
                d["t"] = d["t"] - _nn(mo, d["t"])
        b, sh = 2 * b, sh + 1


def _dn_fwd_chunk(items, ss):
    ii, jj, strict, eye = _dn_decay(items)
    for d in items:
        d["a"] = d["b"] * _nt(d["k"], d["k"]) * d["gam_s"]
        d["qk"] = _nt(d["q"], d["k"]) * d["gam_i"]
    _dn_solve(items, ii, jj, eye)
    for d in items:
        d["u"], d["w"] = _nn(d["t"], d["v"] * d["b"]), _nn(d["t"], d["rhs_w"])
    ws = [_nn(d["w"], s) for d, s in zip(items, ss)]
    qs = [_nn(d["qd"], s) for d, s in zip(items, ss)]
    vnew = [d["u"] - w for d, w in zip(items, ws)]
    outs = [q + _nn(d["qk"], vn) for d, q, vn in zip(items, qs, vnew)]
    ss = [d["dl"] * s + _tn(d["kd"], vn) for d, s, vn in zip(items, ss, vnew)]
    return outs, ss, vnew


def _dn_bwd_chunk(items, ss, dss):
    ii, jj, strict, eye = _dn_decay(items)
    eye_b = eye.astype(BF16)
    for d in items:
        k, q = d["k"], d["q"]
        d["kk"] = _nt(k, k)
        d["a"] = d["b"] * d["kk"] * d["gam_s"]
        d["qk"] = _nt(q, k) * d["gam_i"]
        d["qkt"] = _nt(k, q) * jnp.where(ii <= jj, jnp.exp(jnp.where(ii <= jj, -d["gdiff"], 0.0)), 0.0)
    for d, s, ds2 in zip(items, ss, dss):
        d["dvnew"] = _nn(d["qkt"], d["do"]) + _nn(d["kd"], ds2)
        d["dqk"] = _nt(d["do"], d["vnew"])
        d["dqd"] = _nt(d["do"], s)
        d["dkd"] = _nt(d["vnew"], ds2)
    new_dss = []
    for d, s, ds2 in zip(items, ss, dss):
        d["dw"] = -_nt(d["dvnew"], s)
        new_dss.append(_tn_mxu(d["qd"], d["do"]) - _tn_mxu(d["w"], d["dvnew"]) + d["dl"] * ds2)
        d["ddl"] = _sum0(_sum1(s * ds2))
    for d in items:
        tt = _nt(eye_b, d["t"])
        d["drhs_u"], d["drhs_w"] = _nn(tt, d["dvnew"]), _nn(tt, d["dw"])
    for d in items:
        d["da"] = jnp.where(strict, -(_nt(d["drhs_u"], d["u"]) + _nt(d["drhs_w"], d["w"])), 0.0)
    outs = []
    for d in items:
        q, k, v, b, egc = d["q"], d["k"], d["v"], d["b"], d["egc"]
        drhs_u, drhs_w, da = d["drhs_u"], d["drhs_w"], d["da"]
        dbeta = _sum1(da * d["kk"] * d["gam_s"]) + _sum1(drhs_u * v) + _sum1(drhs_w * k) * egc
        dkk = da * b * d["gam_s"]
        e = da * d["a"] + d["dqk"] * d["qk"]
        s_kd = _sum1(d["dkd"] * d["kd"])
        dgc_col = _sum1(e) + _sum1(drhs_w * d["rhs_w"]) + _sum1(d["dqd"] * d["qd"]) - s_kd
        dgc_row = _sum0(jnp.where(eye, dgc_col, 0.0)) - _sum0(e)
        dg = _sum1(jnp.where(jj >= ii, dgc_row, 0.0)) + (_sum0(s_kd) + d["ddl"] * d["dl"])
        dqkg = d["dqk"] * d["gam_i"]
        dq = _nn(dqkg, k) + d["dqd"] * egc
        dk = _tn_mxu(dqkg, q) + _nn(dkk, k) + _tn_mxu(dkk, k) + drhs_w * (b * egc) + d["dkd"] * d["ekd"]
        outs.append((dq, dk, drhs_u * b, dbeta, dg))
    return outs, new_dss


def _short_conv(raw, halo, w):
    tb = raw.shape[0]
    ext = jnp.concatenate([halo, raw], axis=0)
    out = ext[5:5 + tb] * w[0:1]
    for j in range(1, SHORT_K):
        out = out + ext[5 + j:5 + j + tb] * w[j:j + 1]
    return out


def _dn_gates(ba, alog8, dtb8):
    xg = ba + dtb8
    ea8 = jnp.exp(alog8)
    return _sig(ba), -ea8 * _softplus(xg), xg, ea8


def _dn_item(pre_q, pre_k, pre_v, b, g):
    qc, kc, vc = (p * _sig(p) for p in (pre_q, pre_k, pre_v))
    nq = lax.rsqrt(_sum1(qc * qc) + EPS)
    nk = lax.rsqrt(_sum1(kc * kc) + EPS)
    return dict(q=qc * nq * (DK ** -0.5), k=kc * nk, v=vc, b=b, g=g, qc=qc, kc=kc, nq=nq, nk=nk)


def _dn_out(o, z, sz, og):
    r = lax.rsqrt(_mean(o * o) + EPS)
    xh = o * r
    return xh * og * (z * sz), xh, r


def _sds(shape, dtype):
    return jax.ShapeDtypeStruct(tuple(shape), dtype)


def _matmul(a, b, out_dtype, tm, tn, tk, name):
    m, kd = a.shape
    n = b.shape[1]
    tm, tn, tk = min(tm, m), min(tn, n), min(tk, kd)
    nk = kd // tk

    def body(a_ref, b_ref, o_ref, acc):
        @pl.when(pl.program_id(2) == 0)
        def _():
            acc[...] = jnp.zeros_like(acc)

        acc[...] += jnp.dot(a_ref[...], b_ref[...], preferred_element_type=F32)

        @pl.when(pl.program_id(2) == nk - 1)
        def _():
            o_ref[...] = acc[...].astype(o_ref.dtype)

    return _pcall(
        body, name=name, grid=(m // tm, n // tn, nk),
        in_specs=[pl.BlockSpec((tm, tk), lambda i, j, k: (i, k)), pl.BlockSpec((tk, tn), lambda i, j, k: (k, j))],
        out_specs=pl.BlockSpec((tm, tn), lambda i, j, k: (i, j)), out_shape=_sds((m, n), out_dtype),
        scratch_shapes=[pltpu.VMEM((tm, tn), F32)], compiler_params=_cparams(("parallel", "parallel", "arbitrary")),
    )(a, b)


def _matmul_ksegs(a_segs, b, tm, tn, tk, name):
    m, n = a_segs[0].shape[0], b.shape[1]
    tm, tn = min(tm, m), min(tn, n)
    main, tail = a_segs[:-1], a_segs[-1]
    steps = [s.shape[1] // tk for s in main]
    starts = [sum(steps[:i]) for i in range(len(main))]
    nk = sum(steps)
    wt = tail.shape[1]

    def body(*refs):
        a_refs, at_ref, b_ref, bt_ref, o_ref, acc = refs[:len(main)], *refs[len(main):]
        k = pl.program_id(2)

        @pl.when(k == 0)
        def _():
            acc[...] = jnp.zeros_like(acc)

        for a_ref, k0, ns in zip(a_refs, starts, steps):
            @pl.when((k >= k0) & (k < k0 + ns))
            def _():
                acc[...] += jnp.dot(a_ref[...], b_ref[...], preferred_element_type=F32)

        @pl.when(k == nk)
        def _():
            o_ref[...] = acc[...] + jnp.dot(at_ref[...], bt_ref[...], preferred_element_type=F32)

    seg_specs = [pl.BlockSpec((tm, tk), functools.partial(lambda i, j, k, k0, ns: (i, jnp.clip(k - k0, 0, ns - 1)), k0=k0, ns=ns))
                 for k0, ns in zip(starts, steps)]
    return _pcall(
        body, name=name, grid=(m // tm, n // tn, nk + 1),
        in_specs=seg_specs + [pl.BlockSpec((tm, wt), lambda i, j, k: (i, 0)),
                              pl.BlockSpec((tk, tn), lambda i, j, k: (jnp.minimum(k, nk - 1), j)),
                              pl.BlockSpec((wt, tn), lambda i, j, k: (nk * tk // wt, j))],
        out_specs=pl.BlockSpec((tm, tn), lambda i, j, k: (i, j)), out_shape=_sds((m, n), F32),
        scratch_shapes=[pltpu.VMEM((tm, tn), F32)], compiler_params=_cparams(("parallel", "parallel", "arbitrary")),
    )(*main, tail, b, b)


def _rms_fwd_call(x, g, name):
    m = x.shape[0]
    tm = min(512, m)

    def body(x_ref, g_ref, o_ref):
        xv = x_ref[...]
        o_ref[...] = (xv * lax.rsqrt(_mean(xv * xv) + EPS) * g_ref[...]).astype(BF16)

    return _pcall(
        body, name=name, grid=(m // tm,),
        in_specs=[pl.BlockSpec((tm, D_MODEL), lambda i: (i, 0)), pl.BlockSpec((1, D_MODEL), lambda i: (0, 0))],
        out_specs=pl.BlockSpec((tm, D_MODEL), lambda i: (i, 0)), out_shape=_sds((m, D_MODEL), BF16),
        compiler_params=_cparams(("parallel",)),
    )(x, g)


def _rms_bwd_call(x, dh, g, dres, name):
    m = x.shape[0]
    tm = min(512, m)

    def body(x_ref, dh_ref, g_ref, dr_ref, dx_ref, dg_ref):
        @pl.when(pl.program_id(0) == 0)
        def _():
            dg_ref[...] = jnp.zeros_like(dg_ref)

        xv, dhv = x_ref[...], dh_ref[...]
        r = lax.rsqrt(_mean(xv * xv) + EPS)
        xh = xv * r
        dx_ref[...] = _rms_bwd(dhv * g_ref[...], xh, r) + dr_ref[...]
        dg_ref[...] += _rows8(_sum0(dhv * xh))

    row = pl.BlockSpec((tm, D_MODEL), lambda i: (i, 0))
    return _pcall(
        body, name=name, grid=(m // tm,),
        in_specs=[row, row, pl.BlockSpec((1, D_MODEL), lambda i: (0, 0)), row],
        out_specs=[row, pl.BlockSpec((8, D_MODEL), lambda i: (0, 0))],
        out_shape=[_sds((m, D_MODEL), F32), _sds((8, D_MODEL), F32)], compiler_params=_cparams(("arbitrary",)),
    )(x, dh, g, dres)


def _loss_call(x, tgt, g):
    m = x.shape[0]
    tm = min(512, m)

    def body(x_ref, t_ref, g_ref, dx_ref, dg_ref, l_ref):
        @pl.when(pl.program_id(0) == 0)
        def _():
            dg_ref[...] = jnp.zeros_like(dg_ref)
            l_ref[...] = jnp.zeros_like(l_ref)

        xv = x_ref[...]
        r = lax.rsqrt(_mean(xv * xv) + EPS)
        xh = xv * r
        err = xh * g_ref[...] - t_ref[...]
        dy = err * (1.0 / D_MODEL)
        dx_ref[...] = _rms_bwd(dy * g_ref[...], xh, r)
        dg_ref[...] += _rows8(_sum0(dy * xh))
        l_ref[...] += 0.5 * _sum0(_mean(err * err))

    row = pl.BlockSpec((tm, D_MODEL), lambda i: (i, 0))
    return _pcall(
        body, name="loss_head", grid=(m // tm,),
        in_specs=[row, row, pl.BlockSpec((1, D_MODEL), lambda i: (0, 0))],
        out_specs=[row, pl.BlockSpec((8, D_MODEL), lambda i: (0, 0)), pl.BlockSpec((8, 128), lambda i: (0, 0))],
        out_shape=[_sds((m, D_MODEL), F32), _sds((8, D_MODEL), F32), _sds((8, 128), F32)],
        compiler_params=_cparams(("arbitrary",)),
    )(x, tgt, g)


def _halo_idx(i, rows):
    return jnp.maximum(i * (TB // rows) - 1, 0)


def _a_tiles(nt, rev):
    ti = (lambda i: nt - 1 - i) if rev else (lambda i: i)
    c0 = A_OFF // CONV_W
    return [pl.BlockSpec((1, TB, CONV_W), functools.partial(lambda b, i, c: (b, ti(i), c), c=c0 + c)) for c in range(3)]


def _a_fwd_call(proj, w, bias, g, b, name):
    bsz, t, _ = proj.shape
    nt = t // TB
    c0 = A_OFF // CONV_W

    def body(val_ref, glu_ref, az_ref, vh_ref, gh_ref, w_ref, bias_ref, g_ref, b_ref, y_ref, c_ref):
        keep = jnp.where(pl.program_id(1) > 0, 1.0, 0.0)
        y, c = _a_fwd(val_ref[0], glu_ref[0], az_ref[0], vh_ref[0] * keep, gh_ref[0], w_ref[...], bias_ref[...],
                      g_ref[...], b_ref[...])
        y_ref[0] = y.astype(BF16)
        c_ref[0] = c

    halo = [pl.BlockSpec((1, HALO_A, CONV_W), functools.partial(lambda b, i, c: (b, _halo_idx(i, HALO_A), c), c=c0 + c))
            for c in range(2)]
    par = [pl.BlockSpec((HALO_A, CONV_W), lambda b, i: (0, 0))] + [pl.BlockSpec((1, CONV_W), lambda b, i: (0, 0))] * 3
    tile = pl.BlockSpec((1, TB, CONV_W), lambda b, i: (b, i, 0))
    return _pcall(
        body, name=name, grid=(bsz, nt), in_specs=_a_tiles(nt, False) + halo + par, out_specs=[tile, tile],
        out_shape=[_sds((bsz, t, CONV_W), BF16), _sds((bsz, t, CONV_W), F32)],
        compiler_params=_cparams(("parallel", "parallel")),
    )(proj, proj, proj, proj, proj, w, bias, g, b)


def _a_bwd_call(dy, conv, proj, w, g, b, name):
    bsz, t, _ = proj.shape
    nt = t // TB

    def body(dy_ref, c_ref, val_ref, glu_ref, az_ref, w_ref, g_ref, b_ref, da_ref, dw_ref, ds_ref, carry):
        ip = pl.program_id(1)

        @pl.when((pl.program_id(0) == 0) & (ip == 0))
        def _():
            dw_ref[...] = jnp.zeros_like(dw_ref)
            ds_ref[...] = jnp.zeros_like(ds_ref)

        @pl.when(ip == 0)
        def _():
            carry[...] = jnp.zeros_like(carry)

        dval, dglu, daz, dw, dbias, dg, db, head = _a_bwd(dy_ref[0], val_ref[0], glu_ref[0], az_ref[0], c_ref[0],
                                                          w_ref[...], g_ref[...], b_ref[...], carry[...])
        carry[...] = head
        for n, dcol in enumerate((dval, dglu, daz)):
            da_ref[0, :, n * CONV_W:(n + 1) * CONV_W] = dcol.astype(BF16)
        dw_ref[...] += dw
        ds_ref[...] += _rows8(dbias, dg, db)

    rtile = lambda b, i: (b, nt - 1 - i, 0)
    par = [pl.BlockSpec((HALO_A, CONV_W), lambda b, i: (0, 0))] + [pl.BlockSpec((1, CONV_W), lambda b, i: (0, 0))] * 2
    return _pcall(
        body, name=name, grid=(bsz, nt),
        in_specs=[pl.BlockSpec((1, TB, CONV_W), rtile)] * 2 + _a_tiles(nt, True) + par,
        out_specs=[pl.BlockSpec((1, TB, 3 * CONV_W), rtile), pl.BlockSpec((HALO_A, CONV_W), lambda b, i: (0, 0)),
                   pl.BlockSpec((8, CONV_W), lambda b, i: (0, 0))],
        out_shape=[_sds((bsz, t, 3 * CONV_W), BF16), _sds((HALO_A, CONV_W), F32), _sds((8, CONV_W), F32)],
        scratch_shapes=[pltpu.VMEM((HALO_A, CONV_W), F32)], compiler_params=_cparams(("arbitrary", "arbitrary")),
    )(dy, conv, proj, proj, proj, w, g, b)


def _c_specs():
    c0 = C_OFF // SG_W
    tile = [pl.BlockSpec((1, TB, SG_W), functools.partial(lambda b, i, c: (b, i, c), c=c0 + c)) for c in range(3)]
    par = [pl.BlockSpec((1, SG_W), lambda b, i: (0, 0))] * 2 + [
        pl.BlockSpec((SG_G, SG_C, SG_C), lambda b, i: (0, 0, 0)), pl.BlockSpec((SG_C, SG_W), lambda b, i: (0, 0))]
    return tile + par


def _c_fwd_call(proj, g, b, ws, bias_full, name):
    bsz, t, _ = proj.shape

    def body(cu_ref, cv_ref, cz_ref, g_ref, b_ref, ws_ref, bf_ref, y_ref):
        y, _ = _c_fwd(cu_ref[0], cv_ref[0], cz_ref[0], g_ref[...], b_ref[...], ws_ref[...], bf_ref[...])
        y_ref[0] = y.astype(BF16)

    return _pcall(
        body, name=name, grid=(bsz, t // TB), in_specs=_c_specs(),
        out_specs=pl.BlockSpec((1, TB, SG_W), lambda b, i: (b, i, 0)), out_shape=_sds((bsz, t, SG_W), BF16),
        compiler_params=_cparams(("parallel", "parallel")),
    )(proj, proj, proj, g, b, ws, bias_full)


def _c_bwd_call(dy, proj, g, b, ws, bias_full, name):
    bsz, t, _ = proj.shape

    def body(dy_ref, cu_ref, cv_ref, cz_ref, g_ref, b_ref, ws_ref, bf_ref, dc_ref, dws_ref, dbs_ref, ds_ref):
        @pl.when((pl.program_id(0) == 0) & (pl.program_id(1) == 0))
        def _():
            dws_ref[...] = jnp.zeros_like(dws_ref)
            dbs_ref[...] = jnp.zeros_like(dbs_ref)
            ds_ref[...] = jnp.zeros_like(ds_ref)

        dcu, dcv, dcz, dws, dbs, dg, db = _c_bwd(dy_ref[0], cu_ref[0], cv_ref[0], cz_ref[0], g_ref[...], b_ref[...],
                                                 ws_ref[...], bf_ref[...])
        for n, dcol in enumerate((dcu, dcv, dcz)):
            dc_ref[0, :, n * SG_W:(n + 1) * SG_W] = dcol.astype(BF16)
        for gi in range(SG_G):
            dws_ref[gi] += dws[gi]
        dbs_ref[...] += dbs
        ds_ref[...] += _rows8(dg, db)

    tile = lambda b, i: (b, i, 0)
    return _pcall(
        body, name=name, grid=(bsz, t // TB), in_specs=[pl.BlockSpec((1, TB, SG_W), tile)] + _c_specs(),
        out_specs=[pl.BlockSpec((1, TB, 3 * SG_W), tile), pl.BlockSpec((SG_G, SG_C, SG_C), lambda b, i: (0, 0, 0)),
                   pl.BlockSpec((SG_C, SG_W), lambda b, i: (0, 0)), pl.BlockSpec((8, SG_W), lambda b, i: (0, 0))],
        out_shape=[_sds((bsz, t, 3 * SG_W), BF16), _sds((SG_G, SG_C, SG_C), F32), _sds((SG_C, SG_W), F32),
                   _sds((8, SG_W), F32)],
        compiler_params=_cparams(("arbitrary", "arbitrary")),
    )(dy, proj, proj, proj, g, b, ws, bias_full)


def _m_specs():
    g0 = G_OFF // D_MODEL
    y = [pl.BlockSpec((1, TB, n), lambda b, i: (b, i, 0)) for n in (CONV_W, D_MODEL, SG_W)]
    gates = [pl.BlockSpec((1, TB, D_MODEL), functools.partial(lambda b, i, c: (b, i, c), c=g0 + c)) for c in range(3)]
    return y + gates


def _w_specs(*shapes):
    return [pl.BlockSpec(s, lambda b, i: (0, 0)) for s in shapes]


def _m_fwd_call(ya, yb, yc, proj, x, wa, wb, wc, wo, name):
    bsz, t, _ = x.shape

    def body(ya_ref, yb_ref, yc_ref, g0_ref, g1_ref, g2_ref, x_ref, wa_ref, wb_ref, wc_ref, wo_ref, o_ref):
        merged, _ = _m_fwd(ya_ref[0], yb_ref[0], yc_ref[0], g0_ref[0], g1_ref[0], g2_ref[0], wa_ref[...], wb_ref[...],
                           wc_ref[...])
        o_ref[0] = x_ref[0] + _nn(merged, wo_ref[...])

    tile = pl.BlockSpec((1, TB, D_MODEL), lambda b, i: (b, i, 0))
    return _pcall(
        body, name=name, grid=(bsz, t // TB),
        in_specs=_m_specs() + [tile] + _w_specs(wa.shape, wb.shape, wc.shape, wo.shape),
        out_specs=tile, out_shape=_sds(x.shape, F32), compiler_params=_cparams(("parallel", "parallel")),
    )(ya, yb, yc, proj, proj, proj, x, wa, wb, wc, wo)


def _m_bwd_call(dout, ya, yb, yc, proj, wa, wb, wc, wo, name):
    bsz, t, _ = dout.shape

    def body(do_ref, ya_ref, yb_ref, yc_ref, g0_ref, g1_ref, g2_ref, wa_ref, wb_ref, wc_ref, wo_ref, dya_ref, dyb_ref,
             dyc_ref, dg_ref, mg_ref, dp_ref):
        dys, dgs, merged, dps = _m_bwd(do_ref[0], ya_ref[0], yb_ref[0], yc_ref[0], g0_ref[0], g1_ref[0], g2_ref[0],
                                       wa_ref[...], wb_ref[...], wc_ref[...], wo_ref[...])
        dya_ref[0], dyb_ref[0], dyc_ref[0] = dys
        mg_ref[0] = merged.astype(BF16)
        for n in range(3):
            dg_ref[0, :, n * D_MODEL:(n + 1) * D_MODEL] = dgs[n].astype(BF16)
            dp_ref[0, :, n * D_MODEL:(n + 1) * D_MODEL] = dps[n].astype(BF16)

    tile = lambda n: pl.BlockSpec((1, TB, n), lambda b, i: (b, i, 0))
    widths = (CONV_W, D_MODEL, SG_W, 3 * D_MODEL, D_MODEL, 3 * D_MODEL)
    dts = (F32, F32, F32, BF16, BF16, BF16)
    return _pcall(
        body, name=name, grid=(bsz, t // TB),
        in_specs=[tile(D_MODEL)] + _m_specs() + _w_specs(wa.shape, wb.shape, wc.shape, wo.shape),
        out_specs=[tile(n) for n in widths], out_shape=[_sds((bsz, t, n), d) for n, d in zip(widths, dts)],
        compiler_params=_cparams(("parallel", "parallel")),
    )(dout, ya, yb, yc, proj, proj, proj, wa, wb, wc, wo)


def _m_wgrad_call(ya, yb, yc, merged, dps, dout, name):
    bsz, t, _ = dout.shape

    def body(ya_ref, yb_ref, yc_ref, mg_ref, dp_ref, do_ref, dwa_ref, dwb_ref, dwc_ref, dwo_ref):
        @pl.when((pl.program_id(0) == 0) & (pl.program_id(1) == 0))
        def _():
            for r in (dwa_ref, dwb_ref, dwc_ref, dwo_ref):
                r[...] = jnp.zeros_like(r)

        dp = dp_ref[0]
        dwa_ref[...] += _tn(ya_ref[0], dp[:, :D_MODEL])
        dwb_ref[...] += _tn(yb_ref[0], dp[:, D_MODEL:2 * D_MODEL])
        dwc_ref[...] += _tn(yc_ref[0], dp[:, 2 * D_MODEL:])
        dwo_ref[...] += _tn(mg_ref[0], do_ref[0])

    tile = lambda n: pl.BlockSpec((1, TB, n), lambda b, i: (b, i, 0))
    shapes = ((CONV_W, D_MODEL), (D_MODEL, D_MODEL), (SG_W, D_MODEL), (D_MODEL, D_MODEL))
    return _pcall(
        body, name=name, grid=(bsz, t // TB),
        in_specs=[tile(CONV_W), tile(D_MODEL), tile(SG_W), tile(D_MODEL), tile(3 * D_MODEL), tile(D_MODEL)],
        out_specs=_w_specs(*shapes), out_shape=[_sds(s, F32) for s in shapes],
        compiler_params=_cparams(("arbitrary", "arbitrary")),
    )(ya, yb, yc, merged, dps, dout)


def _dn_specs(nc, rev):
    ti = (lambda i: nc - 1 - i) if rev else (lambda i: i)
    return [pl.BlockSpec((1, CHUNK, B_W), lambda b, i: (b, ti(i), 0)),
            pl.BlockSpec((1, CHUNK, DK), lambda b, i: (b, ti(i), BA_OFF // DK)),
            pl.BlockSpec((SHORT_K, 3 * D_MODEL), lambda b, i: (0, 0))] + [pl.BlockSpec((1, DK), lambda b, i: (0, 0))] * 3


def _dn_saved_specs(nc, rev):
    ti = (lambda i: nc - 1 - i) if rev else (lambda i: i)
    return [pl.BlockSpec((1, CHUNK, D_MODEL), lambda b, i: (b, ti(i), 0)),
            pl.BlockSpec((1, CHUNK, 3 * D_MODEL), lambda b, i: (b, ti(i), 0)),
            pl.BlockSpec((1, HEADS, 1, DK, DK), lambda b, i: (b, 0, ti(i), 0, 0)),
            pl.BlockSpec((1, 1, CHUNK, HEADS * CHUNK), lambda b, i: (b, ti(i), 0, 0)),
            pl.BlockSpec((1, CHUNK, 2 * D_MODEL), lambda b, i: (b, ti(i), 0)),
            pl.BlockSpec((1, CHUNK, D_MODEL), lambda b, i: (b, ti(i), 0))]


def _dn_fwd_call(proj, wconv, alog, dtb, og, name):
    bsz, t, _ = proj.shape
    nc = t // CHUNK

    def body(x_ref, ba_ref, w_ref, al_ref, dt_ref, og_ref, halo_ref, y_ref, o_ref, pre_ref, st_ref, t_ref, uw_ref, vn_ref,
             s_scr):
        i = pl.program_id(1)

        @pl.when(i == 0)
        def _():
            s_scr[...] = jnp.zeros_like(s_scr)

        keep = jnp.where(i > 0, 1.0, 0.0)
        bsig, gfull, _, _ = _dn_gates(ba_ref[0], al_ref[...], dt_ref[...])
        items = []
        for h in range(HEADS):
            pres = []
            for n in range(3):
                cs = slice(n * D_MODEL + h * DK, n * D_MODEL + (h + 1) * DK)
                pres.append(_short_conv(x_ref[0, :, cs], halo_ref[0, :, cs] * keep, w_ref[:, cs]))
                pre_ref[0, :, cs] = pres[-1]
            items.append(_dn_item(*pres, bsig[:, h:h + 1], gfull[:, HEADS + h:HEADS + h + 1]))
        ss = [s_scr[h] for h in range(HEADS)]
        for h in range(HEADS):
            st_ref[0, h, 0] = ss[h]
        outs, ss, vnew = _dn_fwd_chunk(items, ss)
        og_v = og_ref[...]
        for h, d in enumerate(items):
            s_scr[h] = ss[h]
            hs = slice(h * DK, (h + 1) * DK)
            z = x_ref[0, :, 3 * D_MODEL + h * DK:3 * D_MODEL + (h + 1) * DK]
            o_ref[0, :, hs] = outs[h]
            y_ref[0, :, hs] = _dn_out(outs[h], z, _sig(z), og_v)[0].astype(BF16)
            t_ref[0, 0, :, h * CHUNK:(h + 1) * CHUNK] = d["t"].astype(BF16)
            uw_ref[0, :, 2 * h * DK:(2 * h + 1) * DK] = d["u"].astype(BF16)
            uw_ref[0, :, (2 * h + 1) * DK:2 * (h + 1) * DK] = d["w"].astype(BF16)
            vn_ref[0, :, hs] = vnew[h].astype(BF16)

    halo = pl.BlockSpec((1, HALO_B, 3 * D_MODEL), lambda b, i: (b, jnp.maximum(i * (CHUNK // HALO_B) - 1, 0), 0))
    return _pcall(
        body, name=name, grid=(bsz, nc), in_specs=_dn_specs(nc, False) + [halo],
        out_specs=[pl.BlockSpec((1, CHUNK, D_MODEL), lambda b, i: (b, i, 0))] + _dn_saved_specs(nc, False),
        out_shape=[_sds((bsz, t, D_MODEL), BF16), _sds((bsz, t, D_MODEL), F32), _sds((bsz, t, 3 * D_MODEL), F32),
                   _sds((bsz, HEADS, nc, DK, DK), F32), _sds((bsz, nc, CHUNK, HEADS * CHUNK), BF16),
                   _sds((bsz, t, 2 * D_MODEL), BF16), _sds((bsz, t, D_MODEL), BF16)],
        scratch_shapes=[pltpu.VMEM((HEADS, DK, DK), F32)], compiler_params=_cparams(("arbitrary", "arbitrary")),
    )(proj, proj, wconv, alog, dtb, og, proj)


def _dn_bwd_call(dy, saved, proj, wconv, alog, dtb, og, name):
    bsz, t, _ = proj.shape
    nc = t // CHUNK

    def body(dy_ref, o_ref, pre_ref, st_ref, t_ref, uw_ref, vn_ref, x_ref, ba_ref, w_ref, al_ref, dt_ref, og_ref,
             dx_ref, dba_ref, dwc_ref, dsm_ref, ds_scr, dpre_scr):
        ip = pl.program_id(1)

        @pl.when((pl.program_id(0) == 0) & (ip == 0))
        def _():
            dwc_ref[...] = jnp.zeros_like(dwc_ref)
            dsm_ref[...] = jnp.zeros_like(dsm_ref)

        @pl.when(ip == 0)
        def _():
            ds_scr[...] = jnp.zeros_like(ds_scr)
            dpre_scr[...] = jnp.zeros_like(dpre_scr)

        bsig, gfull, xg, ea8 = _dn_gates(ba_ref[0], al_ref[...], dt_ref[...])
        og_v = og_ref[...]
        items, dog = [], jnp.zeros((1, DK), F32)
        for h in range(HEADS):
            hs = slice(h * DK, (h + 1) * DK)
            zs = slice(3 * D_MODEL + h * DK, 3 * D_MODEL + (h + 1) * DK)
            d = _dn_item(*(pre_ref[0, :, n * D_MODEL + h * DK:n * D_MODEL + (h + 1) * DK] for n in range(3)),
                         bsig[:, h:h + 1], gfull[:, HEADS + h:HEADS + h + 1])
            z, dyv = x_ref[0, :, zs], dy_ref[0, :, hs]
            sz = _sig(z)
            _, xh, r = _dn_out(o_ref[0, :, hs], z, sz, og_v)
            dx_ref[0, :, zs] = (dyv * xh * og_v * (sz * (1.0 + z * (1.0 - sz)))).astype(BF16)
            don = dyv * (z * sz)
            dog = dog + _sum0(don * xh)
            d.update(do=_rms_bwd(don * og_v, xh, r), t=t_ref[0, 0, :, h * CHUNK:(h + 1) * CHUNK],
                     u=uw_ref[0, :, 2 * h * DK:(2 * h + 1) * DK], w=uw_ref[0, :, (2 * h + 1) * DK:2 * (h + 1) * DK],
                     vnew=vn_ref[0, :, hs])
            items.append(d)
        grads, dss = _dn_bwd_chunk(items, [st_ref[0, h, 0] for h in range(HEADS)], [ds_scr[h] for h in range(HEADS)])
        lanes = lax.broadcasted_iota(jnp.int32, (CHUNK, DK), 1)
        lane8 = lax.broadcasted_iota(jnp.int32, (8, DK), 1)
        row8 = lax.broadcasted_iota(jnp.int32, (8, DK), 0)
        dgdx = -ea8 * _sig(xg)
        dba = jnp.zeros((CHUNK, DK), F32)
        dsm = jnp.where(row8 == 2, dog, 0.0)
        for h, (d, (dq, dk, dv, dbeta, dg)) in enumerate(zip(items, grads)):
            ds_scr[h] = dss[h]
            qc, kc, nq, nk = d["qc"], d["kc"], d["nq"], d["nk"]
            dacts = ((DK ** -0.5) * nq * (dq - qc * (nq * nq * _sum1(dq * qc))),
                     nk * (dk - kc * (nk * nk * _sum1(dk * kc))), dv)
            dal = dg * dgdx[:, HEADS + h:HEADS + h + 1]
            dba = dba + jnp.where(lanes == h, dbeta * d["b"] * (1.0 - d["b"]), 0.0) + jnp.where(lanes == HEADS + h, dal, 0.0)
            dsm = dsm + (jnp.where((row8 == 0) & (lane8 == h), _sum0(dg * d["g"]), 0.0)
                         + jnp.where((row8 == 1) & (lane8 == h), _sum0(dal), 0.0))
            for n in range(3):
                cs = slice(n * D_MODEL + h * DK, n * D_MODEL + (h + 1) * DK)
                pre, raw, w = pre_ref[0, :, cs], x_ref[0, :, cs], w_ref[:, cs]
                sp = _sig(pre)
                dpre = dacts[n] * (sp * (1.0 + pre * (1.0 - sp)))
                dext = jnp.concatenate([dpre, dpre_scr[:, cs]], axis=0)
                draw = dext[3:3 + CHUNK] * w[0:1]
                rows = [_sum0(dext[3:3 + CHUNK] * raw)]
                for j in range(1, SHORT_K):
                    shifted = dext[3 - j:3 - j + CHUNK]
                    draw = draw + shifted * w[j:j + 1]
                    rows.append(_sum0(shifted * raw))
                dpre_scr[:, cs] = dpre[:HALO_B]
                dx_ref[0, :, cs] = draw.astype(BF16)
                dwc_ref[:, cs] += _rows8(*rows)
        dba_ref[0] = dba.astype(BF16)
        dsm_ref[...] += dsm

    rrow = lambda n: pl.BlockSpec((1, CHUNK, n), lambda b, i: (b, nc - 1 - i, 0))
    return _pcall(
        body, name=name, grid=(bsz, nc), in_specs=[rrow(D_MODEL)] + _dn_saved_specs(nc, True) + _dn_specs(nc, True),
        out_specs=[rrow(B_W), rrow(DK), pl.BlockSpec((8, 3 * D_MODEL), lambda b, i: (0, 0)),
                   pl.BlockSpec((8, DK), lambda b, i: (0, 0))],
        out_shape=[_sds((bsz, t, B_W), BF16), _sds((bsz, t, DK), BF16), _sds((8, 3 * D_MODEL), F32), _sds((8, DK), F32)],
        scratch_shapes=[pltpu.VMEM((HEADS, DK, DK), F32), pltpu.VMEM((HALO_B, 3 * D_MODEL), F32)],
        compiler_params=_cparams(("arbitrary", "arbitrary")),
    )(dy, *saved, proj, proj, wconv, alog, dtb, og)


def _adamw_call(w, g, m, v, name):
    shape = w.shape
    view = (math.prod(shape[:-2]),) + shape[-2:] if len(shape) >= 2 else (1, 1) + shape
    lead, rows, cols = view
    tr, tl = (128, 1) if rows % 128 == 0 else (rows, min(lead, 64))
    c1, c2 = 1.0 - ADAM_B1 ** ADAM_STEP, 1.0 - ADAM_B2 ** ADAM_STEP

    def body(w_ref, g_ref, m_ref, v_ref, d_ref, nm_ref, nv_ref):
        gv = g_ref[...]
        nm = ADAM_B1 * m_ref[...] + (1.0 - ADAM_B1) * gv
        nv = ADAM_B2 * v_ref[...] + (1.0 - ADAM_B2) * (gv * gv)
        d_ref[...] = -ADAM_LR * ((nm / c1) / (jnp.sqrt(nv / c2) + ADAM_EPS) + ADAM_WD * w_ref[...])
        nm_ref[...] = nm
        nv_ref[...] = nv

    blk = pl.BlockSpec((tl, tr, cols), lambda l, i: (l, i, 0))
    outs = _pcall(
        body, name=name, grid=(pl.cdiv(lead, tl), rows // tr), in_specs=[blk] * 4, out_specs=[blk] * 3,
        out_shape=[_sds(view, F32)] * 3, compiler_params=_cparams(("parallel", "parallel")),
    )(*(a.reshape(view) for a in (w, g, m, v)))
    return tuple(o.reshape(shape) for o in outs)


def _row_tile(rows, cap=512):
    for tr in range(cap, 15, -16):
        if rows % tr == 0:
            return tr
    return rows


def _add_pair_call(by_chip, recv, where, name):
    _, n, rows, cols = by_chip.shape
    tr = _row_tile(rows, 256)

    def body(where_ref, a_ref, b_ref, sb_ref, own_ref):
        s = a_ref[0, 0] + b_ref[0]
        sb_ref[0] = s.astype(BF16)

        @pl.when(pl.program_id(1) == where_ref[1])
        def _():
            own_ref[...] = s

    grid_spec = pltpu.PrefetchScalarGridSpec(
        num_scalar_prefetch=1, grid=(rows // tr, n),
        in_specs=[pl.BlockSpec((1, 1, tr, cols), lambda i, j, wr: (wr[0], j, i, 0)),
                  pl.BlockSpec((1, tr, cols), lambda i, j, wr: (j, i, 0))],
        out_specs=[pl.BlockSpec((1, tr, cols), lambda i, j, wr: (j, i, 0)),
                   pl.BlockSpec((tr, cols), lambda i, j, wr: (i, 0))])
    return _pcall(
        body, name=name, grid_spec=grid_spec, out_shape=[_sds((n, rows, cols), BF16), _sds((rows, cols), F32)],
        compiler_params=_cparams(("parallel", "arbitrary")),
    )(where, by_chip, recv)


def _add_recv_call(own, recv, where, name):
    rows, cols = own.shape
    tr = _row_tile(rows, 256)

    def body(where_ref, o_ref, r_ref, s_ref):
        s_ref[0] = ((o_ref[...] + r_ref[0].astype(F32)) + r_ref[1].astype(F32)) + r_ref[2].astype(F32)

    grid_spec = pltpu.PrefetchScalarGridSpec(
        num_scalar_prefetch=1, grid=(rows // tr,),
        in_specs=[pl.BlockSpec((tr, cols), lambda i, wr: (i, 0)), pl.BlockSpec((3, tr, cols), lambda i, wr: (0, i, 0))],
        out_specs=pl.BlockSpec((1, tr, cols), lambda i, wr: (wr[0], i, 0)))
    return _pcall(
        body, name=name, grid_spec=grid_spec, out_shape=_sds((2, rows, cols), F32),
        compiler_params=_cparams(("parallel",)),
    )(where, own, recv)


def _axes():
    return lax.axis_index("x"), lax.axis_index("y"), lax.axis_index("c")


def _chip_peers(x, y):
    return [(x, 1 - y), (1 - x, y), (1 - x, 1 - y)]


_ANY = pl.BlockSpec(memory_space=pl.ANY)
_VMEM = pl.BlockSpec(memory_space=pltpu.VMEM)


def _remote(src, dst, send_sems, recv_sems, k, dev):
    return pltpu.make_async_remote_copy(src_ref=src, dst_ref=dst, send_sem=send_sems.at[k], recv_sem=recv_sems.at[k],
                                        device_id=dev, device_id_type=MESH)


def _gather_weights_call(shards):
    n = len(shards)
    halves = [s.shape[0] // 2 for s in shards]

    def body(*refs):
        x_refs, o_refs, (send_sems, recv_sems, local_sems) = refs[:n], refs[n:2 * n], refs[2 * n:]
        x, y, c = _axes()
        chip, sib, peers = 2 * x + y, (x, y, 1 - c), _chip_peers(x, y)
        pchip = [2 * px + py for px, py in peers]

        def half(a, cc):
            return pl.ds(cc * halves[a], halves[a])

        mine = [pltpu.make_async_copy(x_refs[a], o_refs[a].at[chip], local_sems.at[a]) for a in range(n)]
        for cp in mine:
            cp.start()
        first = [_remote(x_refs[a].at[half(a, c)], o_refs[a].at[chip, half(a, c)], send_sems, recv_sems, 6 * a + k,
                         (*peers[k], c)) for k in range(3) for a in range(n)]
        for cp in first:
            cp.start()
        passed = []
        for k in range(3):
            for a in range(n):
                land = o_refs[a].at[pchip[k], half(a, c)]
                _remote(land, land, send_sems, recv_sems, 6 * a + k, (*peers[k], c)).wait_recv()
                passed.append(_remote(land, land, send_sems, recv_sems, 6 * a + 3 + k, sib))
                passed[-1].start()
        for k in range(3):
            for a in range(n):
                land = o_refs[a].at[pchip[k], half(a, 1 - c)]
                _remote(land, land, send_sems, recv_sems, 6 * a + 3 + k, sib).wait_recv()
        for cp in first + passed:
            cp.wait_send()
        for cp in mine:
            cp.wait()

    return _pcall(
        body, name="gather_weights", in_specs=[_ANY] * n, out_specs=[_ANY] * n,
        out_shape=[_sds((4,) + s.shape, s.dtype) for s in shards],
        scratch_shapes=[pltpu.SemaphoreType.DMA((6 * n,)), pltpu.SemaphoreType.DMA((6 * n,)),
                        pltpu.SemaphoreType.DMA((n,))],
    )(*shards)


def _pair_partials_call(by_chip):
    n = len(by_chip)

    def body(*refs):
        v_refs, o_refs, (send_sems, recv_sems) = refs[:n], refs[n:2 * n], refs[2 * n:]
        x, y, c = _axes()
        cps = [_remote(v_refs[a].at[1 - c], o_refs[a], send_sems, recv_sems, a, (x, y, 1 - c)) for a in range(n)]
        for cp in cps:
            cp.start()
        for cp in cps:
            cp.wait()

    return _pcall(
        body, name="pair_partials", in_specs=[_ANY] * n, out_specs=[_ANY] * n,
        out_shape=[_sds(v.shape[1:], v.dtype) for v in by_chip],
        scratch_shapes=[pltpu.SemaphoreType.DMA((n,)), pltpu.SemaphoreType.DMA((n,))],
    )(*by_chip)


def _scatter_partials_call(parts):
    n = len(parts)

    def body(*refs):
        v_refs, o_refs, (send_sems, recv_sems) = refs[:n], refs[n:2 * n], refs[2 * n:]
        x, y, c = _axes()
        peers = _chip_peers(x, y)
        cps = [_remote(v_refs[a].at[2 * peers[k][0] + peers[k][1]], o_refs[a].at[k], send_sems, recv_sems, 3 * a + k,
                       (*peers[k], c)) for k in range(3) for a in range(n)]
        for cp in cps:
            cp.start()
        for cp in cps:
            cp.wait()

    return _pcall(
        body, name="scatter_partials", in_specs=[_ANY] * n, out_specs=[_ANY] * n,
        out_shape=[_sds((3,) + v.shape[1:], v.dtype) for v in parts],
        scratch_shapes=[pltpu.SemaphoreType.DMA((3 * n,)), pltpu.SemaphoreType.DMA((3 * n,))],
    )(*parts)


def _pair_result_call(fins):
    n = len(fins)

    def body(*refs):
        v_refs, o_refs, (send_sems, recv_sems) = refs[:n], refs[n:2 * n], refs[2 * n:]
        x, y, c = _axes()
        cps = [_remote(v_refs[a].at[c], o_refs[a].at[c], send_sems, recv_sems, a, (x, y, 1 - c)) for a in range(n)]
        for cp in cps:
            cp.start()
        for a in range(n):
            cps[a].wait_send()
            _remote(v_refs[a].at[c], o_refs[a].at[1 - c], send_sems, recv_sems, a, (x, y, 1 - c)).wait_recv()

    return _pcall(
        body, name="pair_result", in_specs=[_ANY] * n, out_specs=[_ANY] * n,
        out_shape=[_sds(v.shape, v.dtype) for v in fins], input_output_aliases={a: a for a in range(n)},
        scratch_shapes=[pltpu.SemaphoreType.DMA((n,)), pltpu.SemaphoreType.DMA((n,))],
    )(*fins)


def _allreduce_small_call(v):
    rows, cols = v.shape

    def body(v_ref, o_ref, buf, send_sems, recv_sems):
        x, y, c = _axes()
        me = 4 * x + 2 * y + c
        buf[0] = v_ref[...]
        cps = []
        for m in range(1, 8):
            dev = (1 - x if m & 4 else x, 1 - y if m & 2 else y, 1 - c if m & 1 else c)
            cps.append(_remote(v_ref, buf.at[m], send_sems, recv_sems, m - 1, dev))
        for cp in cps:
            cp.start()
        for cp in cps:
            cp.wait()
        acc = buf[me]
        for d in range(1, 8):
            acc = acc + buf[lax.bitwise_xor(me, d)]
        o_ref[...] = acc

    return _pcall(
        body, name="allreduce_small", in_specs=[_VMEM], out_specs=_VMEM, out_shape=_sds(v.shape, F32),
        scratch_shapes=[pltpu.VMEM((8, rows, cols), F32), pltpu.SemaphoreType.DMA((7,)), pltpu.SemaphoreType.DMA((7,))],
        compiler_params=_cparams(),
    )(v)


def _pack_rows(arrays, dtype, total_rows=None):
    parts = []
    for a in arrays:
        flat = a.astype(dtype).reshape(-1)
        parts.append(jnp.pad(flat, (0, -flat.shape[0] % D_MODEL)).reshape(-1, D_MODEL))
    out = jnp.concatenate(parts, axis=0)
    total_rows = total_rows or out.shape[0] + (-out.shape[0] % 8)
    return jnp.pad(out, ((0, total_rows - out.shape[0]), (0, 0)))


def _unpack_rows(packed, shapes):
    out, r = [], 0
    for s in shapes:
        n = math.prod(s)
        nr = -(-n // D_MODEL)
        out.append(packed[r:r + nr].reshape(-1)[:n].reshape(s))
        r += nr
    return out


def _to_r(w_t):
    pad = jnp.zeros((N_R - N_ORIG,) + w_t.shape[1:], w_t.dtype)
    return jnp.concatenate([w_t[A_ORIG_W:BA_ORIG], w_t[:A_ORIG_W], w_t[BA_ORIG + 2 * HEADS:],
                            w_t[BA_ORIG:BA_ORIG + 2 * HEADS], pad], axis=0)


_BIG = ("w_in", "a_proj", "b_proj", "c_proj", "w_out")
_SHARD_AXIS = {"w_in": 2, "a_proj": 2, "b_proj": 1, "c_proj": 2, "w_out": 1, "a_dw": 2, "b_conv": 2}
_BIG_AXIS = _SHARD_AXIS


def _join_chips(g, axis):
    g = jnp.moveaxis(g, 0, axis)
    return g.reshape(g.shape[:axis] + (4 * g.shape[axis + 1],) + g.shape[axis + 2:])


def _split_chips(a, axis):
    n = a.shape[axis] // 4
    return jnp.moveaxis(a.reshape(a.shape[:axis] + (4, n) + a.shape[axis + 1:]), axis, 1)
_ORDER = ("norm_g", "w_in", "a_dw", "a_dw_b", "a_ln_g", "a_ln_b", "a_proj", "b_conv", "b_a_log", "b_dt_bias",
          "b_onorm_g", "b_proj", "c_ln_g", "c_ln_b", "c_ws", "c_bs", "c_proj", "w_out", "final_g")


def _local_step(x, tgt, w):
    bsz, t, _ = x.shape
    m = bsz * t
    depth = w["norm_g"].shape[0]
    row = lambda v: v.reshape(1, -1)
    w_rt = _to_r(w["w_in_t"])
    saved, xl = [], x
    for l in range(depth):
        n = f"l{l}_"
        par = dict(
            adw=jnp.pad(w["a_dw"][l], ((0, 1), (0, 0))), adb=row(w["a_dw_b"][l]), alg=row(w["a_ln_g"][l]),
            alb=row(w["a_ln_b"][l]), bconv=w["b_conv"][l],
            alog=jnp.pad(row(w["b_a_log"][l]), ((0, 0), (HEADS, DK - 2 * HEADS))),
            dtb=jnp.pad(row(w["b_dt_bias"][l]), ((0, 0), (HEADS, DK - 2 * HEADS))), og=row(w["b_onorm_g"][l]),
            clg=row(w["c_ln_g"][l]), clb=row(w["c_ln_b"][l]), cws=w["c_ws"][l],
            cbias=jnp.repeat(w["c_bs"][l].T, SG_C, axis=1), ng=row(w["norm_g"][l]),
            wa=w["a_proj"][l], wb=w["b_proj"][l], wc=w["c_proj"][l], wo=w["w_out"][l], wrt=w_rt[:, l], wr=w_rt[:, l].T)
        h = _rms_fwd_call(xl.reshape(m, D_MODEL), par["ng"], n + "norm")
        proj = _matmul(h, par["wr"], F32, 1024, 1152, 1024, n + "in_proj").reshape(bsz, t, N_R)
        ya, conv = _a_fwd_call(proj, par["adw"], par["adb"], par["alg"], par["alb"], n + "conv_fwd")
        yb, *dn_saved = _dn_fwd_call(proj, par["bconv"], par["alog"], par["dtb"], par["og"], n + "delta_fwd")
        yc = _c_fwd_call(proj, par["clg"], par["clb"], par["cws"], par["cbias"], n + "gmlp_fwd")
        x_next = _m_fwd_call(ya, yb, yc, proj, xl, par["wa"], par["wb"], par["wc"], par["wo"], n + "merge_fwd")
        saved.append((par, xl, h, proj, ya, yb, yc, conv, dn_saved))
        xl = x_next
    dout, dfg, loss = _loss_call(xl.reshape(m, D_MODEL), tgt.reshape(m, D_MODEL), row(w["final_g"]))
    dout = dout.reshape(bsz, t, D_MODEL)
    g = {k: [None] * depth for k in _ORDER if k != "final_g"}
    for l in reversed(range(depth)):
        n = f"l{l}_"
        par, xl, h, proj, ya, yb, yc, conv, dn_saved = saved[l]
        dya, dyb, dyc, dgate, merged, dps = _m_bwd_call(dout, ya, yb, yc, proj, par["wa"], par["wb"], par["wc"],
                                                        par["wo"], n + "merge_bwd")
        g["a_proj"][l], g["b_proj"][l], g["c_proj"][l], g["w_out"][l] = _m_wgrad_call(ya, yb, yc, merged, dps, dout,
                                                                                      n + "merge_wgrad")
        da, dadw, dasm = _a_bwd_call(dya, conv, proj, par["adw"], par["alg"], par["alb"], n + "conv_bwd")
        db, dba, dbconv, dbsm = _dn_bwd_call(dyb, dn_saved, proj, par["bconv"], par["alog"], par["dtb"], par["og"],
                                             n + "delta_bwd")
        dc, dcws, dcbs, dcsm = _c_bwd_call(dyc, proj, par["clg"], par["clb"], par["cws"], par["cbias"], n + "gmlp_bwd")
        segs = [s.reshape(m, s.shape[-1]) for s in (db, da, dc, dgate, dba)]
        dh = _matmul_ksegs(segs, par["wrt"], 2048, 1024, 512, n + "in_proj_dx")
        ht = h.T
        dwb, dwa, dwc, dwg, dwba = [_matmul(ht, s, F32, 1024, 1024 if s.shape[1] % 1024 == 0 else 768, 1024,
                                            n + "in_proj_dw_" + tag) for s, tag in zip(segs, "bacgs")]
        g["w_in"][l] = jnp.concatenate([dwa, dwb, dwba[:, :2 * HEADS], dwc, dwg], axis=1)
        dx, dng = _rms_bwd_call(xl.reshape(m, D_MODEL), dh, par["ng"], dout.reshape(m, D_MODEL), n + "norm_bwd")
        dout = dx.reshape(bsz, t, D_MODEL)
        g["norm_g"][l] = dng[0]
        g["a_dw"][l], g["a_dw_b"][l], g["a_ln_g"][l], g["a_ln_b"][l] = dadw[:CONV_K], dasm[0], dasm[1], dasm[2]
        g["b_conv"][l], g["b_a_log"][l], g["b_dt_bias"][l] = dbconv[:SHORT_K], dbsm[0, :HEADS], dbsm[1, :HEADS]
        g["b_onorm_g"][l] = dbsm[2]
        g["c_ln_g"][l], g["c_ln_b"][l], g["c_ws"][l] = dcsm[0], dcsm[1], dcws
        g["c_bs"][l] = dcbs.reshape(SG_C, SG_G, SG_C).sum(-1).T
    grads = {k: jnp.stack(v) for k, v in g.items()}
    grads["final_g"] = dfg[0]
    return loss[0, 0], dout, grads


def kernel(x, norm_g, w_in, a_dw, a_dw_b, a_ln_g, a_ln_b, a_proj, b_conv, b_a_log, b_dt_bias, b_onorm_g, b_proj, c_ln_g, c_ln_b, c_ws, c_bs, c_proj, w_out, final_g, loss_target, m_norm_g, m_w_in, m_a_dw, m_a_dw_b, m_a_ln_g, m_a_ln_b, m_a_proj, m_b_conv, m_b_a_log, m_b_dt_bias, m_b_onorm_g, m_b_proj, m_c_ln_g, m_c_ln_b, m_c_ws, m_c_bs, m_c_proj, m_w_out, m_final_g, v_norm_g, v_w_in, v_a_dw, v_a_dw_b, v_a_ln_g, v_a_ln_b, v_a_proj, v_b_conv, v_b_a_log, v_b_dt_bias, v_b_onorm_g, v_b_proj, v_c_ln_g, v_c_ln_b, v_c_ws, v_c_bs, v_c_proj, v_w_out, v_final_g):
    given = dict(locals())
    ws = {k: given[k] for k in _ORDER}
    xi, yi, ci = _axes()
    chip = 2 * xi + yi
    where = jnp.stack([ci, chip]).astype(jnp.int32)

    others = _BIG[1:] + ("a_dw", "b_conv")
    gathered = _gather_weights_call([jnp.transpose(w_in, (2, 0, 1)).astype(BF16)] + [ws[k].astype(BF16) for k in _BIG[1:]]
                                    + [a_dw, b_conv])
    full = {k: v for k, v in ws.items() if k != "w_in"}
    full["w_in_t"] = gathered[0].reshape((N_ORIG,) + gathered[0].shape[2:])
    for k, g in zip(others, gathered[1:]):
        full[k] = _join_chips(g, _SHARD_AXIS[k])

    loss, grad_x, grads = _local_step(x, loss_target, full)

    by_chip = [_split_chips(grads[k], _BIG_AXIS[k]) for k in _BIG]
    theirs = _pair_partials_call(by_chip)
    sums = [_add_pair_call(b, r, where, "pair_sum_" + k) for k, b, r in zip(_BIG, by_chip, theirs)]
    got = _scatter_partials_call([s[0] for s in sums])
    fins = [_add_recv_call(s[1], r, where, "chip_sum_" + k) for k, s, r in zip(_BIG, sums, got)]
    out_g = dict(zip(_BIG, _pair_result_call(fins)))

    rest = [k for k in _ORDER if k not in _BIG]
    rest_shapes = [grads[k].shape for k in rest] + [(1,)]
    summed = _unpack_rows(_allreduce_small_call(_pack_rows([grads[k] for k in rest] + [loss.reshape(1)], F32)),
                          rest_shapes)
    out_g.update(zip(rest, summed[:-1]))
    out_g["a_dw"] = lax.dynamic_slice_in_dim(out_g["a_dw"], chip * a_dw.shape[2], a_dw.shape[2], axis=2)
    out_g["b_conv"] = lax.dynamic_slice_in_dim(out_g["b_conv"], chip * b_conv.shape[2], b_conv.shape[2], axis=2)

    def adam(k):
        operands = (ws[k], out_g[k], given["m_" + k], given["v_" + k])
        if k != "w_in":
            return _adamw_call(*operands, "adamw_" + k)
        w_t, g_t, m_t, v_t = (jnp.transpose(a, (2, 0, 1)) for a in operands)
        out_g[k] = jnp.transpose(g_t, (1, 2, 0))
        return tuple(jnp.transpose(o, (1, 2, 0)) for o in _adamw_call(w_t, g_t, m_t, v_t, "adamw_" + k))

    upd = {k: adam(k) for k in _ORDER}
    return (summed[-1][0], grad_x, *[out_g[k] for k in _ORDER], *[upd[k][0] for k in _ORDER],
            *[upd[k][1] for k in _ORDER], *[upd[k][2] for k in _ORDER])
```

```python
import functools
import math

import jax
import jax.numpy as jnp
from jax import lax
from jax.experimental import pallas as pl
from jax.experimental.pallas import tpu as pltpu

F32 = jnp.float32
BF16 = jnp.bfloat16
MESH = pl.DeviceIdType.MESH

D_MODEL = 1024
CONV_W = 512
CONV_K = 31
HEADS = 8
DK = 128
SHORT_K = 4
CHUNK = 64
SG_W = 512
SG_G = 4
SG_C = 128
EPS = 1e-6
N_ORIG = 10256
BA_ORIG = 5632
A_ORIG_W = 3 * CONV_W
B_W = 4 * D_MODEL
B_OFF, A_OFF, C_OFF, G_OFF, BA_OFF, N_R = 0, 4096, 5632, 7168, 10240, 10368
TB = 256
HALO_A = 32
HALO_B = 8
VMEM_LIMIT = 56 * 1024 * 1024
ADAM_LR, ADAM_B1, ADAM_B2, ADAM_EPS, ADAM_WD, ADAM_STEP = 0.001, 0.9, 0.999, 1e-08, 0.01, 10


def _pcall(body, **kw):
    return pl.pallas_call(body, **kw)


def _cparams(sem=None):
    kw = dict(vmem_limit_bytes=VMEM_LIMIT)
    if sem is not None:
        kw["dimension_semantics"] = sem
    return pltpu.CompilerParams(**kw)


def _sig(x):
    return jax.nn.sigmoid(x)


def _silu(x):
    return x * _sig(x)


def _dsilu(x):
    s = _sig(x)
    return s * (1.0 + x * (1.0 - s))


_GELU_C = math.sqrt(2.0 / math.pi)


def _gelu(x):
    return 0.5 * x * (1.0 + jnp.tanh(_GELU_C * (x + 0.044715 * x * x * x)))


def _dgelu(x):
    t = jnp.tanh(_GELU_C * (x + 0.044715 * x * x * x))
    return 0.5 * (1.0 + t) + 0.5 * x * (1.0 - t * t) * _GELU_C * (1.0 + 3 * 0.044715 * x * x)


def _softplus(x):
    return jnp.maximum(x, 0.0) + jnp.log1p(jnp.exp(-jnp.abs(x)))


def _dot(a, b, dims):
    return lax.dot_general(a.astype(BF16), b.astype(BF16), (dims, ((), ())), preferred_element_type=F32)


def _nn(a, b):
    return _dot(a, b, ((1,), (0,)))


def _nt(a, b):
    return _dot(a, b, ((1,), (1,)))


def _tn(a, b):
    return _dot(a, b, ((0,), (0,)))


def _tn_mxu(a, b):
    n = a.shape[1]
    eye = (lax.broadcasted_iota(jnp.int32, (n, n), 0) == lax.broadcasted_iota(jnp.int32, (n, n), 1)).astype(BF16)
    return _nn(_nt(eye, a), b)


def _mean(x):
    return jnp.mean(x, axis=-1, keepdims=True)


def _sum0(x):
    return jnp.sum(x, axis=0, keepdims=True)


def _sum1(x):
    return jnp.sum(x, axis=1, keepdims=True)


def _ln_fwd(x, g, b):
    xc = x - _mean(x)
    r = lax.rsqrt(_mean(xc * xc) + EPS)
    xh = xc * r
    return xh * g + b, xh, r


def _ln_bwd(dy, xh, r, g):
    dxh = dy * g
    return r * (dxh - _mean(dxh) - xh * _mean(dxh * xh)), _sum0(dy * xh), _sum0(dy)


def _rms_bwd(dxh, xh, r):
    return r * (dxh - xh * _mean(dxh * xh))


def _rows8(*rows):
    n = rows[0].shape[1]
    return jnp.concatenate(list(rows) + [jnp.zeros((8 - len(rows), n), F32)], axis=0)


def _windows(ext, tb):
    n = ext.shape[0] - 8
    shifted = {0: ext}

    def window(off):
        r = off % 8
        if r not in shifted:
            shifted[r] = ext[r:r + n]
        return shifted[r][off - r:off - r + tb]

    return window


def _a_fwd(val, glu, az, val_h, glu_h, w, bias, g, b):
    tb = val.shape[0]
    win = _windows(jnp.concatenate([val_h * _sig(glu_h), val * _sig(glu)], axis=0), tb)
    c = win(2) * w[0:1]
    for j in range(1, CONV_K):
        c = c + win(2 + j) * w[j:j + 1]
    c = c + bias
    ln, _, _ = _ln_fwd(c, g, b)
    return _silu(ln) * _silu(az), c


def _a_bwd(dy, val, glu, az, c, w, g, b, dc_next):
    tb = val.shape[0]
    ln, xh, r = _ln_fwd(c, g, b)
    sl, sz = _sig(ln), _sig(az)
    dln = dy * (az * sz) * (sl * (1.0 + ln * (1.0 - sl)))
    daz = dy * (ln * sl) * (sz * (1.0 + az * (1.0 - sz)))
    dc, dg, db = _ln_bwd(dln, xh, r, g)
    win = _windows(jnp.concatenate([dc, dc_next], axis=0), tb)
    sg = _sig(glu)
    a = val * sg
    da = win(30) * w[0:1]
    dw_rows = [_sum0(win(30) * a)]
    for j in range(1, CONV_K):
        shifted = win(30 - j)
        da = da + shifted * w[j:j + 1]
        dw_rows.append(_sum0(shifted * a))
    dw = jnp.concatenate(dw_rows + [jnp.zeros((1, CONV_W), F32)], axis=0)
    return da * sg, da * val * sg * (1.0 - sg), daz, dw, _sum0(dc), dg, db, dc[:HALO_A]


def _tril(ws):
    ii = lax.broadcasted_iota(jnp.int32, (SG_C, SG_C), 0)
    jj = lax.broadcasted_iota(jnp.int32, (SG_C, SG_C), 1)
    return [jnp.where(jj <= ii, ws[gi], 0.0) for gi in range(SG_G)], jj <= ii


def _c_mix(wt, vs, bias_full):
    tb = vs.shape[0]
    rows = []
    for n in range(tb // SG_C):
        blks = [_nn(wt[gi], vs[n * SG_C:(n + 1) * SG_C, gi * SG_C:(gi + 1) * SG_C]) for gi in range(SG_G)]
        rows.append(jnp.concatenate(blks, axis=1) + bias_full)
    return jnp.concatenate(rows, axis=0)


def _c_fwd(cu, cv, cz, g, b, ws, bias_full):
    wt, _ = _tril(ws)
    vs, xh, r = _ln_fwd(_gelu(cv), g, b)
    mixed = _c_mix(wt, vs, bias_full)
    return _gelu(cu) * mixed * _silu(cz), (wt, vs, xh, r, mixed)


def _c_bwd(dy, cu, cv, cz, g, b, ws, bias_full):
    tb = cu.shape[0]
    _, (wt, vs, xh, r, mixed) = _c_fwd(cu, cv, cz, g, b, ws, bias_full)
    _, low = _tril(ws)
    u, sz = _gelu(cu), _silu(cz)
    dcu = dy * mixed * sz * _dgelu(cu)
    dcz = dy * u * mixed * _dsilu(cz)
    dmixed = dy * u * sz
    dbs = jnp.zeros((SG_C, SG_W), F32)
    dws = [jnp.zeros((SG_C, SG_C), F32) for _ in range(SG_G)]
    rows = []
    for n in range(tb // SG_C):
        dm_n = dmixed[n * SG_C:(n + 1) * SG_C]
        dbs = dbs + dm_n
        blks = []
        for gi in range(SG_G):
            dm = dm_n[:, gi * SG_C:(gi + 1) * SG_C]
            dws[gi] = dws[gi] + _nt(dm, vs[n * SG_C:(n + 1) * SG_C, gi * SG_C:(gi + 1) * SG_C])
            blks.append(_tn(wt[gi], dm))
        rows.append(jnp.concatenate(blks, axis=1))
    dvs = jnp.concatenate(rows, axis=0)
    dgv, dg, db = _ln_bwd(dvs, xh, r, g)
    dws = [jnp.where(low, d, 0.0) for d in dws]
    return dcu, dgv * _dgelu(cv), dcz, dws, dbs, dg, db


def _m_fwd(ya, yb, yc, g0, g1, g2, wa, wb, wc):
    pa, pb, pc = _nn(ya, wa), _nn(yb, wb), _nn(yc, wc)
    s0, s1, s2 = _sig(g0), _sig(g1), _sig(g2)
    return s0 * pa + s1 * pb + s2 * pc, (pa, pb, pc, s0, s1, s2)


def _m_bwd(dout, ya, yb, yc, g0, g1, g2, wa, wb, wc, wo):
    merged, (pa, pb, pc, s0, s1, s2) = _m_fwd(ya, yb, yc, g0, g1, g2, wa, wb, wc)
    dm = _nt(dout, wo)
    dpa, dpb, dpc = dm * s0, dm * s1, dm * s2
    dgs = (dm * pa * s0 * (1.0 - s0), dm * pb * s1 * (1.0 - s1), dm * pc * s2 * (1.0 - s2))
    return (_nt(dpa, wa), _nt(dpb, wb), _nt(dpc, wc)), dgs, merged, (dpa, dpb, dpc)


def _chunk_masks(c):
    ii = lax.broadcasted_iota(jnp.int32, (c, c), 0)
    jj = lax.broadcasted_iota(jnp.int32, (c, c), 1)
    return ii, jj


def _dn_decay(items):
    ii, jj = _chunk_masks(CHUNK)
    incl, strict, eye = jj <= ii, jj < ii, ii == jj
    for d in items:
        g = d["g"]
        grow = _sum0(jnp.where(eye, g, 0.0))
        gc_col = _sum1(jnp.where(incl, grow, 0.0))
        gc_row = _sum0(jnp.where(ii <= jj, g, 0.0))
        gam_i = jnp.where(incl, jnp.exp(jnp.where(incl, gc_col - gc_row, 0.0)), 0.0)
        gl = _sum0(g)
        egc = jnp.exp(gc_col)
        d.update(gam_i=gam_i, gam_s=jnp.where(strict, gam_i, 0.0), egc=egc, ekd=jnp.exp(gl - gc_col), dl=jnp.exp(gl),
                 gdiff=gc_col - gc_row, qd=d["q"] * egc, rhs_w=d["k"] * (d["b"] * egc))
        d["kd"] = d["k"] * d["ekd"]
    return ii, jj, strict, eye


def _dn_solve(items, ii, jj, eye):
    off = ((ii >> 1) == (jj >> 1)) & ((ii & 1) != 0) & ((jj & 1) == 0)
    for d in items:
        a1 = jnp.where(off, d["a"], 0.0)
        d["t"] = jnp.where(eye, 1.0, 0.0) - a1
        d["m"] = d["a"] - _nn(a1, d["a"])
    b, sh = 2, 2
    while b < CHUNK:
        off = ((ii >> sh) == (jj >> sh)) & ((ii & b) != 0) & ((jj & b) == 0)
        for d in items:
            mo = jnp.where(off, d["m"], 0.0)
            if 2 * b < CHUNK:
                d["m"], d["t"] = d["m"] - _nn(mo, d["m"]), d["t"] - _nn(mo, d["t"])
            else:
                d["t"] = d["t"] - _nn(mo, d["t"])
        b, sh = 2 * b, sh + 1


def _dn_fwd_chunk(items, ss):
    ii, jj, strict, eye = _dn_decay(items)
    for d in items:
        d["a"] = d["b"] * _nt(d["k"], d["k"]) * d["gam_s"]
        d["qk"] = _nt(d["q"], d["k"]) * d["gam_i"]
    _dn_solve(items, ii, jj, eye)
    for d in items:
        d["u"], d["w"] = _nn(d["t"], d["v"] * d["b"]), _nn(d["t"], d["rhs_w"])
    ws = [_nn(d["w"], s) for d, s in zip(items, ss)]
    qs = [_nn(d["qd"], s) for d, s in zip(items, ss)]
    vnew = [d["u"] - w for d, w in zip(items, ws)]
    outs = [q + _nn(d["qk"], vn) for d, q, vn in zip(items, qs, vnew)]
    ss = [d["dl"] * s + _tn(d["kd"], vn) for d, s, vn in zip(items, ss, vnew)]
    return outs, ss, vnew


def _dn_bwd_chunk(items, ss, dss):
    ii, jj, strict, eye = _dn_decay(items)
    eye_b = eye.astype(BF16)
    for d in items:
        k, q = d["k"], d["q"]
        d["kk"] = _nt(k, k)
        d["a"] = d["b"] * d["kk"] * d["gam_s"]
        d["qk"] = _nt(q, k) * d["gam_i"]
        d["qkt"] = _nt(k, q) * jnp.where(ii <= jj, jnp.exp(jnp.where(ii <= jj, -d["gdiff"], 0.0)), 0.0)
    for d, s, ds2 in zip(items, ss, dss):
        d["dvnew"] = _nn(d["qkt"], d["do"]) + _nn(d["kd"], ds2)
        d["dqk"] = _nt(d["do"], d["vnew"])
        d["dqd"] = _nt(d["do"], s)
        d["dkd"] = _nt(d["vnew"], ds2)
    new_dss = []
    for d, s, ds2 in zip(items, ss, dss):
        d["dw"] = -_nt(d["dvnew"], s)
        new_dss.append(_tn_mxu(d["qd"], d["do"]) - _tn_mxu(d["w"], d["dvnew"]) + d["dl"] * ds2)
        d["ddl"] = _sum0(_sum1(s * ds2))
    for d in items:
        tt = _nt(eye_b, d["t"])
        d["drhs_u"], d["drhs_w"] = _nn(tt, d["dvnew"]), _nn(tt, d["dw"])
    for d in items:
        d["da"] = jnp.where(strict, -(_nt(d["drhs_u"], d["u"]) + _nt(d["drhs_w"], d["w"])), 0.0)
    outs = []
    for d in items:
        q, k, v, b, egc = d["q"], d["k"], d["v"], d["b"], d["egc"]
        drhs_u, drhs_w, da = d["drhs_u"], d["drhs_w"], d["da"]
        dbeta = _sum1(da * d["kk"] * d["gam_s"]) + _sum1(drhs_u * v) + _sum1(drhs_w * k) * egc
        dkk = da * b * d["gam_s"]
        e = da * d["a"] + d["dqk"] * d["qk"]
        s_kd = _sum1(d["dkd"] * d["kd"])
        dgc_col = _sum1(e) + _sum1(drhs_w * d["rhs_w"]) + _sum1(d["dqd"] * d["qd"]) - s_kd
        dgc_row = _sum0(jnp.where(eye, dgc_col, 0.0)) - _sum0(e)
        dg = _sum1(jnp.where(jj >= ii, dgc_row, 0.0)) + (_sum0(s_kd) + d["ddl"] * d["dl"])
        dqkg = d["dqk"] * d["gam_i"]
        dq = _nn(dqkg, k) + d["dqd"] * egc
        dk = _tn_mxu(dqkg, q) + _nn(dkk, k) + _tn_mxu(dkk, k) + drhs_w * (b * egc) + d["dkd"] * d["ekd"]
        outs.append((dq, dk, drhs_u * b, dbeta, dg))
    return outs, new_dss


def _short_conv(raw, halo, w):
    tb = raw.shape[0]
    ext = jnp.concatenate([halo, raw], axis=0)
    out = ext[5:5 + tb] * w[0:1]
    for j in range(1, SHORT_K):
        out = out + ext[5 + j:5 + j + tb] * w[j:j + 1]
    return out


def _dn_gates(ba, alog8, dtb8):
    xg = ba + dtb8
    ea8 = jnp.exp(alog8)
    return _sig(ba), -ea8 * _softplus(xg), xg, ea8


def _dn_item(pre_q, pre_k, pre_v, b, g):
    qc, kc, vc = (p * _sig(p) for p in (pre_q, pre_k, pre_v))
    nq = lax.rsqrt(_sum1(qc * qc) + EPS)
    nk = lax.rsqrt(_sum1(kc * kc) + EPS)
    return dict(q=qc * nq * (DK ** -0.5), k=kc * nk, v=vc, b=b, g=g, qc=qc, kc=kc, nq=nq, nk=nk)


def _dn_out(o, z, sz, og):
    r = lax.rsqrt(_mean(o * o) + EPS)
    xh = o * r
    return xh * og * (z * sz), xh, r


def _sds(shape, dtype):
    return jax.ShapeDtypeStruct(tuple(shape), dtype)


def _matmul(a, b, out_dtype, tm, tn, tk, name):
    m, kd = a.shape
    n = b.shape[1]
    tm, tn, tk = min(tm, m), min(tn, n), min(tk, kd)
    nk = kd // tk

    def body(a_ref, b_ref, o_ref, acc):
        @pl.when(pl.program_id(2) == 0)
        def _():
            acc[...] = jnp.zeros_like(acc)

        acc[...] += jnp.dot(a_ref[...], b_ref[...], preferred_element_type=F32)

        @pl.when(pl.program_id(2) == nk - 1)
        def _():
            o_ref[...] = acc[...].astype(o_ref.dtype)

    return _pcall(
        body, name=name, grid=(m // tm, n // tn, nk),
        in_specs=[pl.BlockSpec((tm, tk), lambda i, j, k: (i, k)), pl.BlockSpec((tk, tn), lambda i, j, k: (k, j))],
        out_specs=pl.BlockSpec((tm, tn), lambda i, j, k: (i, j)), out_shape=_sds((m, n), out_dtype),
        scratch_shapes=[pltpu.VMEM((tm, tn), F32)], compiler_params=_cparams(("parallel", "parallel", "arbitrary")),
    )(a, b)


def _matmul_ksegs(a_segs, b, tm, tn, tk, name):
    m, n = a_segs[0].shape[0], b.shape[1]
    tm, tn = min(tm, m), min(tn, n)
    main, tail = a_segs[:-1], a_segs[-1]
    steps = [s.shape[1] // tk for s in main]
    starts = [sum(steps[:i]) for i in range(len(main))]
    nk = sum(steps)
    wt = tail.shape[1]

    def body(*refs):
        a_refs, at_ref, b_ref, bt_ref, o_ref, acc = refs[:len(main)], *refs[len(main):]
        k = pl.program_id(2)

        @pl.when(k == 0)
        def _():
            acc[...] = jnp.zeros_like(acc)

        for a_ref, k0, ns in zip(a_refs, starts, steps):
            @pl.when((k >= k0) & (k < k0 + ns))
            def _():
                acc[...] += jnp.dot(a_ref[...], b_ref[...], preferred_element_type=F32)

        @pl.when(k == nk)
        def _():
            o_ref[...] = acc[...] + jnp.dot(at_ref[...], bt_ref[...], preferred_element_type=F32)

    seg_specs = [pl.BlockSpec((tm, tk), functools.partial(lambda i, j, k, k0, ns: (i, jnp.clip(k - k0, 0, ns - 1)), k0=k0, ns=ns))
                 for k0, ns in zip(starts, steps)]
    return _pcall(
        body, name=name, grid=(m // tm, n // tn, nk + 1),
        in_specs=seg_specs + [pl.BlockSpec((tm, wt), lambda i, j, k: (i, 0)),
                              pl.BlockSpec((tk, tn), lambda i, j, k: (jnp.minimum(k, nk - 1), j)),
                              pl.BlockSpec((wt, tn), lambda i, j, k: (nk * tk // wt, j))],
        out_specs=pl.BlockSpec((tm, tn), lambda i, j, k: (i, j)), out_shape=_sds((m, n), F32),
        scratch_shapes=[pltpu.VMEM((tm, tn), F32)], compiler_params=_cparams(("parallel", "parallel", "arbitrary")),
    )(*main, tail, b, b)


def _rms_fwd_call(x, g, name):
    m = x.shape[0]
    tm = min(512, m)

    def body(x_ref, g_ref, o_ref):
        xv = x_ref[...]
        o_ref[...] = (xv * lax.rsqrt(_mean(xv * xv) + EPS) * g_ref[...]).astype(BF16)

    return _pcall(
        body, name=name, grid=(m // tm,),
        in_specs=[pl.BlockSpec((tm, D_MODEL), lambda i: (i, 0)), pl.BlockSpec((1, D_MODEL), lambda i: (0, 0))],
        out_specs=pl.BlockSpec((tm, D_MODEL), lambda i: (i, 0)), out_shape=_sds((m, D_MODEL), BF16),
        compiler_params=_cparams(("parallel",)),
    )(x, g)


def _rms_bwd_call(x, dh, g, dres, name):
    m = x.shape[0]
    tm = min(512, m)

    def body(x_ref, dh_ref, g_ref, dr_ref, dx_ref, dg_ref):
        @pl.when(pl.program_id(0) == 0)
        def _():
            dg_ref[...] = jnp.zeros_like(dg_ref)

        xv, dhv = x_ref[...], dh_ref[...]
        r = lax.rsqrt(_mean(xv * xv) + EPS)
        xh = xv * r
        dx_ref[...] = _rms_bwd(dhv * g_ref[...], xh, r) + dr_ref[...]
        dg_ref[...] += _rows8(_sum0(dhv * xh))

    row = pl.BlockSpec((tm, D_MODEL), lambda i: (i, 0))
    return _pcall(
        body, name=name, grid=(m // tm,),
        in_specs=[row, row, pl.BlockSpec((1, D_MODEL), lambda i: (0, 0)), row],
        out_specs=[row, pl.BlockSpec((8, D_MODEL), lambda i: (0, 0))],
        out_shape=[_sds((m, D_MODEL), F32), _sds((8, D_MODEL), F32)], compiler_params=_cparams(("arbitrary",)),
    )(x, dh, g, dres)


def _loss_call(x, tgt, g):
    m = x.shape[0]
    tm = min(512, m)

    def body(x_ref, t_ref, g_ref, dx_ref, dg_ref, l_ref):
        @pl.when(pl.program_id(0) == 0)
        def _():
            dg_ref[...] = jnp.zeros_like(dg_ref)
            l_ref[...] = jnp.zeros_like(l_ref)

        xv = x_ref[...]
        r = lax.rsqrt(_mean(xv * xv) + EPS)
        xh = xv * r
        err = xh * g_ref[...] - t_ref[...]
        dy = err * (1.0 / D_MODEL)
        dx_ref[...] = _rms_bwd(dy * g_ref[...], xh, r)
        dg_ref[...] += _rows8(_sum0(dy * xh))
        l_ref[...] += 0.5 * _sum0(_mean(err * err))

    row = pl.BlockSpec((tm, D_MODEL), lambda i: (i, 0))
    return _pcall(
        body, name="loss_head", grid=(m // tm,),
        in_specs=[row, row, pl.BlockSpec((1, D_MODEL), lambda i: (0, 0))],
        out_specs=[row, pl.BlockSpec((8, D_MODEL), lambda i: (0, 0)), pl.BlockSpec((8, 128), lambda i: (0, 0))],
        out_shape=[_sds((m, D_MODEL), F32), _sds((8, D_MODEL), F32), _sds((8, 128), F32)],
        compiler_params=_cparams(("arbitrary",)),
    )(x, tgt, g)


def _halo_idx(i, rows):
    return jnp.maximum(i * (TB // rows) - 1, 0)


def _a_tiles(nt, rev):
    ti = (lambda i: nt - 1 - i) if rev else (lambda i: i)
    c0 = A_OFF // CONV_W
    return [pl.BlockSpec((1, TB, CONV_W), functools.partial(lambda b, i, c: (b, ti(i), c), c=c0 + c)) for c in range(3)]


def _a_fwd_call(proj, w, bias, g, b, name):
    bsz, t, _ = proj.shape
    nt = t // TB
    c0 = A_OFF // CONV_W

    def body(val_ref, glu_ref, az_ref, vh_ref, gh_ref, w_ref, bias_ref, g_ref, b_ref, y_ref, c_ref):
        keep = jnp.where(pl.program_id(1) > 0, 1.0, 0.0)
        y, c = _a_fwd(val_ref[0], glu_ref[0], az_ref[0], vh_ref[0] * keep, gh_ref[0], w_ref[...], bias_ref[...],
                      g_ref[...], b_ref[...])
        y_ref[0] = y.astype(BF16)
        c_ref[0] = c

    halo = [pl.BlockSpec((1, HALO_A, CONV_W), functools.partial(lambda b, i, c: (b, _halo_idx(i, HALO_A), c), c=c0 + c))
            for c in range(2)]
    par = [pl.BlockSpec((HALO_A, CONV_W), lambda b, i: (0, 0))] + [pl.BlockSpec((1, CONV_W), lambda b, i: (0, 0))] * 3
    tile = pl.BlockSpec((1, TB, CONV_W), lambda b, i: (b, i, 0))
    return _pcall(
        body, name=name, grid=(bsz, nt), in_specs=_a_tiles(nt, False) + halo + par, out_specs=[tile, tile],
        out_shape=[_sds((bsz, t, CONV_W), BF16), _sds((bsz, t, CONV_W), F32)],
        compiler_params=_cparams(("parallel", "parallel")),
    )(proj, proj, proj, proj, proj, w, bias, g, b)


def _a_bwd_call(dy, conv, proj, w, g, b, name):
    bsz, t, _ = proj.shape
    nt = t // TB

    def body(dy_ref, c_ref, val_ref, glu_ref, az_ref, w_ref, g_ref, b_ref, da_ref, dw_ref, ds_ref, carry):
        ip = pl.program_id(1)

        @pl.when((pl.program_id(0) == 0) & (ip == 0))
        def _():
            dw_ref[...] = jnp.zeros_like(dw_ref)
            ds_ref[...] = jnp.zeros_like(ds_ref)

        @pl.when(ip == 0)
        def _():
            carry[...] = jnp.zeros_like(carry)

        dval, dglu, daz, dw, dbias, dg, db, head = _a_bwd(dy_ref[0], val_ref[0], glu_ref[0], az_ref[0], c_ref[0],
                                                          w_ref[...], g_ref[...], b_ref[...], carry[...])
        carry[...] = head
        for n, dcol in enumerate((dval, dglu, daz)):
            da_ref[0, :, n * CONV_W:(n + 1) * CONV_W] = dcol.astype(BF16)
        dw_ref[...] += dw
        ds_ref[...] += _rows8(dbias, dg, db)

    rtile = lambda b, i: (b, nt - 1 - i, 0)
    par = [pl.BlockSpec((HALO_A, CONV_W), lambda b, i: (0, 0))] + [pl.BlockSpec((1, CONV_W), lambda b, i: (0, 0))] * 2
    return _pcall(
        body, name=name, grid=(bsz, nt),
        in_specs=[pl.BlockSpec((1, TB, CONV_W), rtile)] * 2 + _a_tiles(nt, True) + par,
        out_specs=[pl.BlockSpec((1, TB, 3 * CONV_W), rtile), pl.BlockSpec((HALO_A, CONV_W), lambda b, i: (0, 0)),
                   pl.BlockSpec((8, CONV_W), lambda b, i: (0, 0))],
        out_shape=[_sds((bsz, t, 3 * CONV_W), BF16), _sds((HALO_A, CONV_W), F32), _sds((8, CONV_W), F32)],
        scratch_shapes=[pltpu.VMEM((HALO_A, CONV_W), F32)], compiler_params=_cparams(("arbitrary", "arbitrary")),
    )(dy, conv, proj, proj, proj, w, g, b)


def _c_specs():
    c0 = C_OFF // SG_W
    tile = [pl.BlockSpec((1, TB, SG_W), functools.partial(lambda b, i, c: (b, i, c), c=c0 + c)) for c in range(3)]
    par = [pl.BlockSpec((1, SG_W), lambda b, i: (0, 0))] * 2 + [
        pl.BlockSpec((SG_G, SG_C, SG_C), lambda b, i: (0, 0, 0)), pl.BlockSpec((SG_C, SG_W), lambda b, i: (0, 0))]
    return tile + par


def _c_fwd_call(proj, g, b, ws, bias_full, name):
    bsz, t, _ = proj.shape

    def body(cu_ref, cv_ref, cz_ref, g_ref, b_ref, ws_ref, bf_ref, y_ref):
        y, _ = _c_fwd(cu_ref[0], cv_ref[0], cz_ref[0], g_ref[...], b_ref[...], ws_ref[...], bf_ref[...])
        y_ref[0] = y.astype(BF16)

    return _pcall(
        body, name=name, grid=(bsz, t // TB), in_specs=_c_specs(),
        out_specs=pl.BlockSpec((1, TB, SG_W), lambda b, i: (b, i, 0)), out_shape=_sds((bsz, t, SG_W), BF16),
        compiler_params=_cparams(("parallel", "parallel")),
    )(proj, proj, proj, g, b, ws, bias_full)


def _c_bwd_call(dy, proj, g, b, ws, bias_full, name):
    bsz, t, _ = proj.shape

    def body(dy_ref, cu_ref, cv_ref, cz_ref, g_ref, b_ref, ws_ref, bf_ref, dc_ref, dws_ref, dbs_ref, ds_ref):
        @pl.when((pl.program_id(0) == 0) & (pl.program_id(1) == 0))
        def _():
            dws_ref[...] = jnp.zeros_like(dws_ref)
            dbs_ref[...] = jnp.zeros_like(dbs_ref)
            ds_ref[...] = jnp.zeros_like(ds_ref)

        dcu, dcv, dcz, dws, dbs, dg, db = _c_bwd(dy_ref[0], cu_ref[0], cv_ref[0], cz_ref[0], g_ref[...], b_ref[...],
                                                 ws_ref[...], bf_ref[...])
        for n, dcol in enumerate((dcu, dcv, dcz)):
            dc_ref[0, :, n * SG_W:(n + 1) * SG_W] = dcol.astype(BF16)
        for gi in range(SG_G):
            dws_ref[gi] += dws[gi]
        dbs_ref[...] += dbs
        ds_ref[...] += _rows8(dg, db)

    tile = lambda b, i: (b, i, 0)
    return _pcall(
        body, name=name, grid=(bsz, t // TB), in_specs=[pl.BlockSpec((1, TB, SG_W), tile)] + _c_specs(),
        out_specs=[pl.BlockSpec((1, TB, 3 * SG_W), tile), pl.BlockSpec((SG_G, SG_C, SG_C), lambda b, i: (0, 0, 0)),
                   pl.BlockSpec((SG_C, SG_W), lambda b, i: (0, 0)), pl.BlockSpec((8, SG_W), lambda b, i: (0, 0))],
        out_shape=[_sds((bsz, t, 3 * SG_W), BF16), _sds((SG_G, SG_C, SG_C), F32), _sds((SG_C, SG_W), F32),
                   _sds((8, SG_W), F32)],
        compiler_params=_cparams(("arbitrary", "arbitrary")),
    )(dy, proj, proj, proj, g, b, ws, bias_full)


def _m_specs():
    g0 = G_OFF // D_MODEL
    y = [pl.BlockSpec((1, TB, n), lambda b, i: (b, i, 0)) for n in (CONV_W, D_MODEL, SG_W)]
    gates = [pl.BlockSpec((1, TB, D_MODEL), functools.partial(lambda b, i, c: (b, i, c), c=g0 + c)) for c in range(3)]
    return y + gates


def _w_specs(*shapes):
    return [pl.BlockSpec(s, lambda b, i: (0, 0)) for s in shapes]


def _m_fwd_call(ya, yb, yc, proj, x, wa, wb, wc, wo, name):
    bsz, t, _ = x.shape

    def body(ya_ref, yb_ref, yc_ref, g0_ref, g1_ref, g2_ref, x_ref, wa_ref, wb_ref, wc_ref, wo_ref, o_ref):
        merged, _ = _m_fwd(ya_ref[0], yb_ref[0], yc_ref[0], g0_ref[0], g1_ref[0], g2_ref[0], wa_ref[...], wb_ref[...],
                           wc_ref[...])
        o_ref[0] = x_ref[0] + _nn(merged, wo_ref[...])

    tile = pl.BlockSpec((1, TB, D_MODEL), lambda b, i: (b, i, 0))
    return _pcall(
        body, name=name, grid=(bsz, t // TB),
        in_specs=_m_specs() + [tile] + _w_specs(wa.shape, wb.shape, wc.shape, wo.shape),
        out_specs=tile, out_shape=_sds(x.shape, F32), compiler_params=_cparams(("parallel", "parallel")),
    )(ya, yb, yc, proj, proj, proj, x, wa, wb, wc, wo)


def _m_bwd_call(dout, ya, yb, yc, proj, wa, wb, wc, wo, name):
    bsz, t, _ = dout.shape

    def body(do_ref, ya_ref, yb_ref, yc_ref, g0_ref, g1_ref, g2_ref, wa_ref, wb_ref, wc_ref, wo_ref, dya_ref, dyb_ref,
             dyc_ref, dg_ref, mg_ref, dp_ref):
        dys, dgs, merged, dps = _m_bwd(do_ref[0], ya_ref[0], yb_ref[0], yc_ref[0], g0_ref[0], g1_ref[0], g2_ref[0],
                                       wa_ref[...], wb_ref[...], wc_ref[...], wo_ref[...])
        dya_ref[0], dyb_ref[0], dyc_ref[0] = dys
        mg_ref[0] = merged.astype(BF16)
        for n in range(3):
            dg_ref[0, :, n * D_MODEL:(n + 1) * D_MODEL] = dgs[n].astype(BF16)
            dp_ref[0, :, n * D_MODEL:(n + 1) * D_MODEL] = dps[n].astype(BF16)

    tile = lambda n: pl.BlockSpec((1, TB, n), lambda b, i: (b, i, 0))
    widths = (CONV_W, D_MODEL, SG_W, 3 * D_MODEL, D_MODEL, 3 * D_MODEL)
    dts = (F32, F32, F32, BF16, BF16, BF16)
    return _pcall(
        body, name=name, grid=(bsz, t // TB),
        in_specs=[tile(D_MODEL)] + _m_specs() + _w_specs(wa.shape, wb.shape, wc.shape, wo.shape),
        out_specs=[tile(n) for n in widths], out_shape=[_sds((bsz, t, n), d) for n, d in zip(widths, dts)],
        compiler_params=_cparams(("parallel", "parallel")),
    )(dout, ya, yb, yc, proj, proj, proj, wa, wb, wc, wo)


def _m_wgrad_call(ya, yb, yc, merged, dps, dout, name):
    bsz, t, _ = dout.shape

    def body(ya_ref, yb_ref, yc_ref, mg_ref, dp_ref, do_ref, dwa_ref, dwb_ref, dwc_ref, dwo_ref):
        @pl.when((pl.program_id(0) == 0) & (pl.program_id(1) == 0))
        def _():
            for r in (dwa_ref, dwb_ref, dwc_ref, dwo_ref):
                r[...] = jnp.zeros_like(r)

        dp = dp_ref[0]
        dwa_ref[...] += _tn(ya_ref[0], dp[:, :D_MODEL])
        dwb_ref[...] += _tn(yb_ref[0], dp[:, D_MODEL:2 * D_MODEL])
        dwc_ref[...] += _tn(yc_ref[0], dp[:, 2 * D_MODEL:])
        dwo_ref[...] += _tn(mg_ref[0], do_ref[0])

    tile = lambda n: pl.BlockSpec((1, TB, n), lambda b, i: (b, i, 0))
    shapes = ((CONV_W, D_MODEL), (D_MODEL, D_MODEL), (SG_W, D_MODEL), (D_MODEL, D_MODEL))
    return _pcall(
        body, name=name, grid=(bsz, t // TB),
        in_specs=[tile(CONV_W), tile(D_MODEL), tile(SG_W), tile(D_MODEL), tile(3 * D_MODEL), tile(D_MODEL)],
        out_specs=_w_specs(*shapes), out_shape=[_sds(s, F32) for s in shapes],
        compiler_params=_cparams(("arbitrary", "arbitrary")),
    )(ya, yb, yc, merged, dps, dout)


def _dn_specs(nc, rev):
    ti = (lambda i: nc - 1 - i) if rev else (lambda i: i)
    return [pl.BlockSpec((1, CHUNK, B_W), lambda b, i: (b, ti(i), 0)),
            pl.BlockSpec((1, CHUNK, DK), lambda b, i: (b, ti(i), BA_OFF // DK)),
            pl.BlockSpec((SHORT_K, 3 * D_MODEL), lambda b, i: (0, 0))] + [pl.BlockSpec((1, DK), lambda b, i: (0, 0))] * 3


def _dn_saved_specs(nc, rev):
    ti = (lambda i: nc - 1 - i) if rev else (lambda i: i)
    return [pl.BlockSpec((1, CHUNK, D_MODEL), lambda b, i: (b, ti(i), 0)),
            pl.BlockSpec((1, CHUNK, 3 * D_MODEL), lambda b, i: (b, ti(i), 0)),
            pl.BlockSpec((1, HEADS, 1, DK, DK), lambda b, i: (b, 0, ti(i), 0, 0)),
            pl.BlockSpec((1, 1, CHUNK, HEADS * CHUNK), lambda b, i: (b, ti(i), 0, 0)),
            pl.BlockSpec((1, CHUNK, 2 * D_MODEL), lambda b, i: (b, ti(i), 0)),
            pl.BlockSpec((1, CHUNK, D_MODEL), lambda b, i: (b, ti(i), 0))]


def _dn_fwd_call(proj, wconv, alog, dtb, og, name):
    bsz, t, _ = proj.shape
    nc = t // CHUNK

    def body(x_ref, ba_ref, w_ref, al_ref, dt_ref, og_ref, halo_ref, y_ref, o_ref, pre_ref, st_ref, t_ref, uw_ref, vn_ref,
             s_scr):
        i = pl.program_id(1)

        @pl.when(i == 0)
        def _():
            s_scr[...] = jnp.zeros_like(s_scr)

        keep = jnp.where(i > 0, 1.0, 0.0)
        bsig, gfull, _, _ = _dn_gates(ba_ref[0], al_ref[...], dt_ref[...])
        items = []
        for h in range(HEADS):
            pres = []
            for n in range(3):
                cs = slice(n * D_MODEL + h * DK, n * D_MODEL + (h + 1) * DK)
                pres.append(_short_conv(x_ref[0, :, cs], halo_ref[0, :, cs] * keep, w_ref[:, cs]))
                pre_ref[0, :, cs] = pres[-1]
            items.append(_dn_item(*pres, bsig[:, h:h + 1], gfull[:, HEADS + h:HEADS + h + 1]))
        ss = [s_scr[h] for h in range(HEADS)]
        for h in range(HEADS):
            st_ref[0, h, 0] = ss[h]
        outs, ss, vnew = _dn_fwd_chunk(items, ss)
        og_v = og_ref[...]
        for h, d in enumerate(items):
            s_scr[h] = ss[h]
            hs = slice(h * DK, (h + 1) * DK)
            z = x_ref[0, :, 3 * D_MODEL + h * DK:3 * D_MODEL + (h + 1) * DK]
            o_ref[0, :, hs] = outs[h]
            y_ref[0, :, hs] = _dn_out(outs[h], z, _sig(z), og_v)[0].astype(BF16)
            t_ref[0, 0, :, h * CHUNK:(h + 1) * CHUNK] = d["t"].astype(BF16)
            uw_ref[0, :, 2 * h * DK:(2 * h + 1) * DK] = d["u"].astype(BF16)
            uw_ref[0, :, (2 * h + 1) * DK:2 * (h + 1) * DK] = d["w"].astype(BF16)
            vn_ref[0, :, hs] = vnew[h].astype(BF16)

    halo = pl.BlockSpec((1, HALO_B, 3 * D_MODEL), lambda b, i: (b, jnp.maximum(i * (CHUNK // HALO_B) - 1, 0), 0))
    return _pcall(
        body, name=name, grid=(bsz, nc), in_specs=_dn_specs(nc, False) + [halo],
        out_specs=[pl.BlockSpec((1, CHUNK, D_MODEL), lambda b, i: (b, i, 0))] + _dn_saved_specs(nc, False),
        out_shape=[_sds((bsz, t, D_MODEL), BF16), _sds((bsz, t, D_MODEL), F32), _sds((bsz, t, 3 * D_MODEL), F32),
                   _sds((bsz, HEADS, nc, DK, DK), F32), _sds((bsz, nc, CHUNK, HEADS * CHUNK), BF16),
                   _sds((bsz, t, 2 * D_MODEL), BF16), _sds((bsz, t, D_MODEL), BF16)],
        scratch_shapes=[pltpu.VMEM((HEADS, DK, DK), F32)], compiler_params=_cparams(("arbitrary", "arbitrary")),
    )(proj, proj, wconv, alog, dtb, og, proj)


def _dn_bwd_call(dy, saved, proj, wconv, alog, dtb, og, name):
    bsz, t, _ = proj.shape
    nc = t // CHUNK

    def body(dy_ref, o_ref, pre_ref, st_ref, t_ref, uw_ref, vn_ref, x_ref, ba_ref, w_ref, al_ref, dt_ref, og_ref,
             dx_ref, dba_ref, dwc_ref, dsm_ref, ds_scr, dpre_scr):
        ip = pl.program_id(1)

        @pl.when((pl.program_id(0) == 0) & (ip == 0))
        def _():
            dwc_ref[...] = jnp.zeros_like(dwc_ref)
            dsm_ref[...] = jnp.zeros_like(dsm_ref)

        @pl.when(ip == 0)
        def _():
            ds_scr[...] = jnp.zeros_like(ds_scr)
            dpre_scr[...] = jnp.zeros_like(dpre_scr)

        bsig, gfull, xg, ea8 = _dn_gates(ba_ref[0], al_ref[...], dt_ref[...])
        og_v = og_ref[...]
        items, dog = [], jnp.zeros((1, DK), F32)
        for h in range(HEADS):
            hs = slice(h * DK, (h + 1) * DK)
            zs = slice(3 * D_MODEL + h * DK, 3 * D_MODEL + (h + 1) * DK)
            d = _dn_item(*(pre_ref[0, :, n * D_MODEL + h * DK:n * D_MODEL + (h + 1) * DK] for n in range(3)),
                         bsig[:, h:h + 1], gfull[:, HEADS + h:HEADS + h + 1])
            z, dyv = x_ref[0, :, zs], dy_ref[0, :, hs]
            sz = _sig(z)
            _, xh, r = _dn_out(o_ref[0, :, hs], z, sz, og_v)
            dx_ref[0, :, zs] = (dyv * xh * og_v * (sz * (1.0 + z * (1.0 - sz)))).astype(BF16)
            don = dyv * (z * sz)
            dog = dog + _sum0(don * xh)
            d.update(do=_rms_bwd(don * og_v, xh, r), t=t_ref[0, 0, :, h * CHUNK:(h + 1) * CHUNK],
                     u=uw_ref[0, :, 2 * h * DK:(2 * h + 1) * DK], w=uw_ref[0, :, (2 * h + 1) * DK:2 * (h + 1) * DK],
                     vnew=vn_ref[0, :, hs])
            items.append(d)
        grads, dss = _dn_bwd_chunk(items, [st_ref[0, h, 0] for h in range(HEADS)], [ds_scr[h] for h in range(HEADS)])
        lanes = lax.broadcasted_iota(jnp.int32, (CHUNK, DK), 1)
        lane8 = lax.broadcasted_iota(jnp.int32, (8, DK), 1)
        row8 = lax.broadcasted_iota(jnp.int32, (8, DK), 0)
        dgdx = -ea8 * _sig(xg)
        dba = jnp.zeros((CHUNK, DK), F32)
        dsm = jnp.where(row8 == 2, dog, 0.0)
        for h, (d, (dq, dk, dv, dbeta, dg)) in enumerate(zip(items, grads)):
            ds_scr[h] = dss[h]
            qc, kc, nq, nk = d["qc"], d["kc"], d["nq"], d["nk"]
            dacts = ((DK ** -0.5) * nq * (dq - qc * (nq * nq * _sum1(dq * qc))),
                     nk * (dk - kc * (nk * nk * _sum1(dk * kc))), dv)
            dal = dg * dgdx[:, HEADS + h:HEADS + h + 1]
            dba = dba + jnp.where(lanes == h, dbeta * d["b"] * (1.0 - d["b"]), 0.0) + jnp.where(lanes == HEADS + h, dal, 0.0)
            dsm = dsm + (jnp.where((row8 == 0) & (lane8 == h), _sum0(dg * d["g"]), 0.0)
                         + jnp.where((row8 == 1) & (lane8 == h), _sum0(dal), 0.0))
            for n in range(3):
                cs = slice(n * D_MODEL + h * DK, n * D_MODEL + (h + 1) * DK)
                pre, raw, w = pre_ref[0, :, cs], x_ref[0, :, cs], w_ref[:, cs]
                sp = _sig(pre)
                dpre = dacts[n] * (sp * (1.0 + pre * (1.0 - sp)))
                dext = jnp.concatenate([dpre, dpre_scr[:, cs]], axis=0)
                draw = dext[3:3 + CHUNK] * w[0:1]
                rows = [_sum0(dext[3:3 + CHUNK] * raw)]
                for j in range(1, SHORT_K):
                    shifted = dext[3 - j:3 - j + CHUNK]
                    draw = draw + shifted * w[j:j + 1]
                    rows.append(_sum0(shifted * raw))
                dpre_scr[:, cs] = dpre[:HALO_B]
                dx_ref[0, :, cs] = draw.astype(BF16)
                dwc_ref[:, cs] += _rows8(*rows)
        dba_ref[0] = dba.astype(BF16)
        dsm_ref[...] += dsm

    rrow = lambda n: pl.BlockSpec((1, CHUNK, n), lambda b, i: (b, nc - 1 - i, 0))
    return _pcall(
        body, name=name, grid=(bsz, nc), in_specs=[rrow(D_MODEL)] + _dn_saved_specs(nc, True) + _dn_specs(nc, True),
        out_specs=[rrow(B_W), rrow(DK), pl.BlockSpec((8, 3 * D_MODEL), lambda b, i: (0, 0)),
                   pl.BlockSpec((8, DK), lambda b, i: (0, 0))],
        out_shape=[_sds((bsz, t, B_W), BF16), _sds((bsz, t, DK), BF16), _sds((8, 3 * D_MODEL), F32), _sds((8, DK), F32)],
        scratch_shapes=[pltpu.VMEM((HEADS, DK, DK), F32), pltpu.VMEM((HALO_B, 3 * D_MODEL), F32)],
        compiler_params=_cparams(("arbitrary", "arbitrary")),
    )(dy, *saved, proj, proj, wconv, alog, dtb, og)


def _adamw_call(w, g, m, v, name):
    shape = w.shape
    view = (math.prod(shape[:-2]),) + shape[-2:] if len(shape) >= 2 else (1, 1) + shape
    lead, rows, cols = view
    tr, tl = (128, 1) if rows % 128 == 0 else (rows, min(lead, 64))
    c1, c2 = 1.0 - ADAM_B1 ** ADAM_STEP, 1.0 - ADAM_B2 ** ADAM_STEP

    def body(w_ref, g_ref, m_ref, v_ref, d_ref, nm_ref, nv_ref):
        gv = g_ref[...]
        nm = ADAM_B1 * m_ref[...] + (1.0 - ADAM_B1) * gv
        nv = ADAM_B2 * v_ref[...] + (1.0 - ADAM_B2) * (gv * gv)
        d_ref[...] = -ADAM_LR * ((nm / c1) / (jnp.sqrt(nv / c2) + ADAM_EPS) + ADAM_WD * w_ref[...])
        nm_ref[...] = nm
        nv_ref[...] = nv

    blk = pl.BlockSpec((tl, tr, cols), lambda l, i: (l, i, 0))
    outs = _pcall(
        body, name=name, grid=(pl.cdiv(lead, tl), rows // tr), in_specs=[blk] * 4, out_specs=[blk] * 3,
        out_shape=[_sds(view, F32)] * 3, compiler_params=_cparams(("parallel", "parallel")),
    )(*(a.reshape(view) for a in (w, g, m, v)))
    return tuple(o.reshape(shape) for o in outs)


def _row_tile(rows, cap=512):
    for tr in range(cap, 15, -16):
        if rows % tr == 0:
            return tr
    return rows


def _add_pair_call(by_chip, recv, where, name):
    _, n, rows, cols = by_chip.shape
    tr = _row_tile(rows, 256)

    def body(where_ref, a_ref, b_ref, sb_ref, own_ref):
        s = a_ref[0, 0] + b_ref[0]
        sb_ref[0] = s.astype(BF16)

        @pl.when(pl.program_id(1) == where_ref[1])
        def _():
            own_ref[...] = s

    grid_spec = pltpu.PrefetchScalarGridSpec(
        num_scalar_prefetch=1, grid=(rows // tr, n),
        in_specs=[pl.BlockSpec((1, 1, tr, cols), lambda i, j, wr: (wr[0], j, i, 0)),
                  pl.BlockSpec((1, tr, cols), lambda i, j, wr: (j, i, 0))],
        out_specs=[pl.BlockSpec((1, tr, cols), lambda i, j, wr: (j, i, 0)),
                   pl.BlockSpec((tr, cols), lambda i, j, wr: (i, 0))])
    return _pcall(
        body, name=name, grid_spec=grid_spec, out_shape=[_sds((n, rows, cols), BF16), _sds((rows, cols), F32)],
        compiler_params=_cparams(("parallel", "arbitrary")),
    )(where, by_chip, recv)


def _add_recv_call(own, recv, where, name):
    rows, cols = own.shape
    tr = _row_tile(rows, 256)

    def body(where_ref, o_ref, r_ref, s_ref):
        s_ref[0] = ((o_ref[...] + r_ref[0].astype(F32)) + r_ref[1].astype(F32)) + r_ref[2].astype(F32)

    grid_spec = pltpu.PrefetchScalarGridSpec(
        num_scalar_prefetch=1, grid=(rows // tr,),
        in_specs=[pl.BlockSpec((tr, cols), lambda i, wr: (i, 0)), pl.BlockSpec((3, tr, cols), lambda i, wr: (0, i, 0))],
        out_specs=pl.BlockSpec((1, tr, cols), lambda i, wr: (wr[0], i, 0)))
    return _pcall(
        body, name=name, grid_spec=grid_spec, out_shape=_sds((2, rows, cols), F32),
        compiler_params=_cparams(("parallel",)),
    )(where, own, recv)


def _axes():
    return lax.axis_index("x"), lax.axis_index("y"), lax.axis_index("c")


def _chip_peers(x, y):
    return [(x, 1 - y), (1 - x, y), (1 - x, 1 - y)]


_ANY = pl.BlockSpec(memory_space=pl.ANY)
_VMEM = pl.BlockSpec(memory_space=pltpu.VMEM)


def _remote(src, dst, send_sems, recv_sems, k, dev):
    return pltpu.make_async_remote_copy(src_ref=src, dst_ref=dst, send_sem=send_sems.at[k], recv_sem=recv_sems.at[k],
                                        device_id=dev, device_id_type=MESH)


def _gather_weights_call(shards):
    n = len(shards)
    halves = [s.shape[0] // 2 for s in shards]
    quarter = halves[0] // 2
    base = [0] + [8 + 6 * (a - 1) for a in range(1, n)]

    def body(*refs):
        x_refs, o_refs, (send_sems, recv_sems, local_sems) = refs[:n], refs[n:2 * n], refs[2 * n:]
        x, y, c = _axes()
        chip, sib, peers = 2 * x + y, (x, y, 1 - c), _chip_peers(x, y)
        pchip = [2 * px + py for px, py in peers]

        def half(a, cc):
            return pl.ds(cc * halves[a], halves[a])

        def part(cc, q):
            return pl.ds(cc * halves[0] + q * quarter, quarter)

        def copy(ref, k, dev):
            return _remote(ref, ref, send_sems, recv_sems, k, dev)

        mine = [pltpu.make_async_copy(x_refs[a], o_refs[a].at[chip], local_sems.at[a]) for a in range(n)]
        for cp in mine:
            cp.start()
        sends = [_remote(x_refs[a].at[half(a, c)], o_refs[a].at[chip, half(a, c)], send_sems, recv_sems, base[a] + k,
                         (*peers[k], c)) for k in range(3) for a in range(n) if a > 0 or k < 2]
        for cp in sends:
            cp.start()
        for k in range(2):
            copy(o_refs[0].at[pchip[k], half(0, c)], k, (*peers[k], c)).wait_recv()
            sends.append(copy(o_refs[0].at[pchip[k], part(c, k)], 2 + k, (*peers[1 - k], c)))
            sends.append(copy(o_refs[0].at[pchip[k], half(0, c)], 4 + k, sib))
            sends[-2].start()
            sends[-1].start()
        for k in range(3):
            for a in range(1, n):
                land = o_refs[a].at[pchip[k], half(a, c)]
                copy(land, base[a] + k, (*peers[k], c)).wait_recv()
                sends.append(copy(land, base[a] + 3 + k, sib))
                sends[-1].start()
        for q in range(2):
            land = o_refs[0].at[pchip[2], part(c, q)]
            copy(land, 2 + q, (*peers[1 - q], c)).wait_recv()
            sends.append(copy(land, 6 + q, sib))
            sends[-1].start()
        for k in range(2):
            copy(o_refs[0].at[pchip[k], half(0, 1 - c)], 4 + k, sib).wait_recv()
            copy(o_refs[0].at[pchip[2], part(1 - c, k)], 6 + k, sib).wait_recv()
        for k in range(3):
            for a in range(1, n):
                copy(o_refs[a].at[pchip[k], half(a, 1 - c)], base[a] + 3 + k, sib).wait_recv()
        for cp in sends:
            cp.wait_send()
        for cp in mine:
            cp.wait()

    n_sems = 8 + 6 * (n - 1)
    return _pcall(
        body, name="gather_weights", in_specs=[_ANY] * n, out_specs=[_ANY] * n,
        out_shape=[_sds((4,) + s.shape, s.dtype) for s in shards],
        scratch_shapes=[pltpu.SemaphoreType.DMA((n_sems,)), pltpu.SemaphoreType.DMA((n_sems,)),
                        pltpu.SemaphoreType.DMA((n,))],
    )(*shards)


def _pair_partials_call(by_chip):
    n = len(by_chip)

    def body(*refs):
        v_refs, o_refs, (send_sems, recv_sems) = refs[:n], refs[n:2 * n], refs[2 * n:]
        x, y, c = _axes()
        cps = [_remote(v_refs[a].at[1 - c], o_refs[a], send_sems, recv_sems, a, (x, y, 1 - c)) for a in range(n)]
        for cp in cps:
            cp.start()
        for cp in cps:
            cp.wait()

    return _pcall(
        body, name="pair_partials", in_specs=[_ANY] * n, out_specs=[_ANY] * n,
        out_shape=[_sds(v.shape[1:], v.dtype) for v in by_chip],
        scratch_shapes=[pltpu.SemaphoreType.DMA((n,)), pltpu.SemaphoreType.DMA((n,))],
    )(*by_chip)


def _scatter_partials_call(parts):
    n = len(parts)

    def body(*refs):
        v_refs, o_refs, (send_sems, recv_sems) = refs[:n], refs[n:2 * n], refs[2 * n:]
        x, y, c = _axes()
        peers = _chip_peers(x, y)
        cps = [_remote(v_refs[a].at[2 * peers[k][0] + peers[k][1]], o_refs[a].at[k], send_sems, recv_sems, 3 * a + k,
                       (*peers[k], c)) for k in range(3) for a in range(n)]
        for cp in cps:
            cp.start()
        for cp in cps:
            cp.wait()

    return _pcall(
        body, name="scatter_partials", in_specs=[_ANY] * n, out_specs=[_ANY] * n,
        out_shape=[_sds((3,) + v.shape[1:], v.dtype) for v in parts],
        scratch_shapes=[pltpu.SemaphoreType.DMA((3 * n,)), pltpu.SemaphoreType.DMA((3 * n,))],
    )(*parts)


def _pair_result_call(fins):
    n = len(fins)

    def body(*refs):
        v_refs, o_refs, (send_sems, recv_sems) = refs[:n], refs[n:2 * n], refs[2 * n:]
        x, y, c = _axes()
        cps = [_remote(v_refs[a].at[c], o_refs[a].at[c], send_sems, recv_sems, a, (x, y, 1 - c)) for a in range(n)]
        for cp in cps:
            cp.start()
        for a in range(n):
            cps[a].wait_send()
            _remote(v_refs[a].at[c], o_refs[a].at[1 - c], send_sems, recv_sems, a, (x, y, 1 - c)).wait_recv()

    return _pcall(
        body, name="pair_result", in_specs=[_ANY] * n, out_specs=[_ANY] * n,
        out_shape=[_sds(v.shape, v.dtype) for v in fins], input_output_aliases={a: a for a in range(n)},
        scratch_shapes=[pltpu.SemaphoreType.DMA((n,)), pltpu.SemaphoreType.DMA((n,))],
    )(*fins)


def _allreduce_small_call(v):
    rows, cols = v.shape

    def body(v_ref, o_ref, buf, send_sems, recv_sems):
        x, y, c = _axes()
        me = 4 * x + 2 * y + c
        buf[0] = v_ref[...]
        cps = []
        for m in range(1, 8):
            dev = (1 - x if m & 4 else x, 1 - y if m & 2 else y, 1 - c if m & 1 else c)
            cps.append(_remote(v_ref, buf.at[m], send_sems, recv_sems, m - 1, dev))
        for cp in cps:
            cp.start()
        for cp in cps:
            cp.wait()
        acc = buf[me]
        for d in range(1, 8):
            acc = acc + buf[lax.bitwise_xor(me, d)]
        o_ref[...] = acc

    return _pcall(
        body, name="allreduce_small", in_specs=[_VMEM], out_specs=_VMEM, out_shape=_sds(v.shape, F32),
        scratch_shapes=[pltpu.VMEM((8, rows, cols), F32), pltpu.SemaphoreType.DMA((7,)), pltpu.SemaphoreType.DMA((7,))],
        compiler_params=_cparams(),
    )(v)


def _pack_rows(arrays, dtype, total_rows=None):
    parts = []
    for a in arrays:
        flat = a.astype(dtype).reshape(-1)
        parts.append(jnp.pad(flat, (0, -flat.shape[0] % D_MODEL)).reshape(-1, D_MODEL))
    out = jnp.concatenate(parts, axis=0)
    total_rows = total_rows or out.shape[0] + (-out.shape[0] % 8)
    return jnp.pad(out, ((0, total_rows - out.shape[0]), (0, 0)))


def _unpack_rows(packed, shapes):
    out, r = [], 0
    for s in shapes:
        n = math.prod(s)
        nr = -(-n // D_MODEL)
        out.append(packed[r:r + nr].reshape(-1)[:n].reshape(s))
        r += nr
    return out


def _to_r(w_t):
    pad = jnp.zeros((N_R - N_ORIG,) + w_t.shape[1:], w_t.dtype)
    return jnp.concatenate([w_t[A_ORIG_W:BA_ORIG], w_t[:A_ORIG_W], w_t[BA_ORIG + 2 * HEADS:],
                            w_t[BA_ORIG:BA_ORIG + 2 * HEADS], pad], axis=0)


_BIG = ("w_in", "a_proj", "b_proj", "c_proj", "w_out")
_SHARD_AXIS = {"w_in": 2, "a_proj": 2, "b_proj": 1, "c_proj": 2, "w_out": 1, "a_dw": 2, "b_conv": 2}
_BIG_AXIS = _SHARD_AXIS


def _join_chips(g, axis):
    g = jnp.moveaxis(g, 0, axis)
    return g.reshape(g.shape[:axis] + (4 * g.shape[axis + 1],) + g.shape[axis + 2:])


def _split_chips(a, axis):
    n = a.shape[axis] // 4
    return jnp.moveaxis(a.reshape(a.shape[:axis] + (4, n) + a.shape[axis + 1:]), axis, 1)
_ORDER = ("norm_g", "w_in", "a_dw", "a_dw_b", "a_ln_g", "a_ln_b", "a_proj", "b_conv", "b_a_log", "b_dt_bias",
          "b_onorm_g", "b_proj", "c_ln_g", "c_ln_b", "c_ws", "c_bs", "c_proj", "w_out", "final_g")


def _local_step(x, tgt, w):
    bsz, t, _ = x.shape
    m = bsz * t
    depth = w["norm_g"].shape[0]
    row = lambda v: v.reshape(1, -1)
    w_rt = _to_r(w["w_in_t"])
    saved, xl = [], x
    for l in range(depth):
        n = f"l{l}_"
        par = dict(
            adw=jnp.pad(w["a_dw"][l], ((0, 1), (0, 0))), adb=row(w["a_dw_b"][l]), alg=row(w["a_ln_g"][l]),
            alb=row(w["a_ln_b"][l]), bconv=w["b_conv"][l],
            alog=jnp.pad(row(w["b_a_log"][l]), ((0, 0), (HEADS, DK - 2 * HEADS))),
            dtb=jnp.pad(row(w["b_dt_bias"][l]), ((0, 0), (HEADS, DK - 2 * HEADS))), og=row(w["b_onorm_g"][l]),
            clg=row(w["c_ln_g"][l]), clb=row(w["c_ln_b"][l]), cws=w["c_ws"][l],
            cbias=jnp.repeat(w["c_bs"][l].T, SG_C, axis=1), ng=row(w["norm_g"][l]),
            wa=w["a_proj"][l], wb=w["b_proj"][l], wc=w["c_proj"][l], wo=w["w_out"][l], wrt=w_rt[:, l], wr=w_rt[:, l].T)
        h = _rms_fwd_call(xl.reshape(m, D_MODEL), par["ng"], n + "norm")
        proj = _matmul(h, par["wr"], F32, 2048, 1152, 1024, n + "in_proj").reshape(bsz, t, N_R)
        ya, conv = _a_fwd_call(proj, par["adw"], par["adb"], par["alg"], par["alb"], n + "conv_fwd")
        yb, *dn_saved = _dn_fwd_call(proj, par["bconv"], par["alog"], par["dtb"], par["og"], n + "delta_fwd")
        yc = _c_fwd_call(proj, par["clg"], par["clb"], par["cws"], par["cbias"], n + "gmlp_fwd")
        x_next = _m_fwd_call(ya, yb, yc, proj, xl, par["wa"], par["wb"], par["wc"], par["wo"], n + "merge_fwd")
        saved.append((par, xl, h, proj, ya, yb, yc, conv, dn_saved))
        xl = x_next
    dout, dfg, loss = _loss_call(xl.reshape(m, D_MODEL), tgt.reshape(m, D_MODEL), row(w["final_g"]))
    dout = dout.reshape(bsz, t, D_MODEL)
    g = {k: [None] * depth for k in _ORDER if k != "final_g"}
    for l in reversed(range(depth)):
        n = f"l{l}_"
        par, xl, h, proj, ya, yb, yc, conv, dn_saved = saved[l]
        dya, dyb, dyc, dgate, merged, dps = _m_bwd_call(dout, ya, yb, yc, proj, par["wa"], par["wb"], par["wc"],
                                                        par["wo"], n + "merge_bwd")
        g["a_proj"][l], g["b_proj"][l], g["c_proj"][l], g["w_out"][l] = _m_wgrad_call(ya, yb, yc, merged, dps, dout,
                                                                                      n + "merge_wgrad")
        da, dadw, dasm = _a_bwd_call(dya, conv, proj, par["adw"], par["alg"], par["alb"], n + "conv_bwd")
        db, dba, dbconv, dbsm = _dn_bwd_call(dyb, dn_saved, proj, par["bconv"], par["alog"], par["dtb"], par["og"],
                                             n + "delta_bwd")
        dc, dcws, dcbs, dcsm = _c_bwd_call(dyc, proj, par["clg"], par["clb"], par["cws"], par["cbias"], n + "gmlp_bwd")
        segs = [s.reshape(m, s.shape[-1]) for s in (db, da, dc, dgate, dba)]
        dh = _matmul_ksegs(segs, par["wrt"], 2048, 1024, 512, n + "in_proj_dx")
        ht = h.T
        dwb, dwa, dwc, dwg, dwba = [_matmul(ht, s, F32, 1024, 1024 if s.shape[1] % 1024 == 0 else 768, 2048,
                                            n + "in_proj_dw_" + tag) for s, tag in zip(segs, "bacgs")]
        g["w_in"][l] = jnp.concatenate([dwa, dwb, dwba[:, :2 * HEADS], dwc, dwg], axis=1)
        dx, dng = _rms_bwd_call(xl.reshape(m, D_MODEL), dh, par["ng"], dout.reshape(m, D_MODEL), n + "norm_bwd")
        dout = dx.reshape(bsz, t, D_MODEL)
        g["norm_g"][l] = dng[0]
        g["a_dw"][l], g["a_dw_b"][l], g["a_ln_g"][l], g["a_ln_b"][l] = dadw[:CONV_K], dasm[0], dasm[1], dasm[2]
        g["b_conv"][l], g["b_a_log"][l], g["b_dt_bias"][l] = dbconv[:SHORT_K], dbsm[0, :HEADS], dbsm[1, :HEADS]
        g["b_onorm_g"][l] = dbsm[2]
        g["c_ln_g"][l], g["c_ln_b"][l], g["c_ws"][l] = dcsm[0], dcsm[1], dcws
        g["c_bs"][l] = dcbs.reshape(SG_C, SG_G, SG_C).sum(-1).T
    grads = {k: jnp.stack(v) for k, v in g.items()}
    grads["final_g"] = dfg[0]
    return loss[0, 0], dout, grads


def kernel(x, norm_g, w_in, a_dw, a_dw_b, a_ln_g, a_ln_b, a_proj, b_conv, b_a_log, b_dt_bias, b_onorm_g, b_proj, c_ln_g, c_ln_b, c_ws, c_bs, c_proj, w_out, final_g, loss_target, m_norm_g, m_w_in, m_a_dw, m_a_dw_b, m_a_ln_g, m_a_ln_b, m_a_proj, m_b_conv, m_b_a_log, m_b_dt_bias, m_b_onorm_g, m_b_proj, m_c_ln_g, m_c_ln_b, m_c_ws, m_c_bs, m_c_proj, m_w_out, m_final_g, v_norm_g, v_w_in, v_a_dw, v_a_dw_b, v_a_ln_g, v_a_ln_b, v_a_proj, v_b_conv, v_b_a_log, v_b_dt_bias, v_b_onorm_g, v_b_proj, v_c_ln_g, v_c_ln_b, v_c_ws, v_c_bs, v_c_proj, v_w_out, v_final_g):
    given = dict(locals())
    ws = {k: given[k] for k in _ORDER}
    xi, yi, ci = _axes()
    chip = 2 * xi + yi
    where = jnp.stack([ci, chip]).astype(jnp.int32)

    others = _BIG[1:] + ("a_dw", "b_conv")
    gathered = _gather_weights_call([jnp.transpose(w_in, (2, 0, 1)).astype(BF16)] + [ws[k].astype(BF16) for k in _BIG[1:]]
                                    + [a_dw, b_conv])
    full = {k: v for k, v in ws.items() if k != "w_in"}
    full["w_in_t"] = gathered[0].reshape((N_ORIG,) + gathered[0].shape[2:])
    for k, g in zip(others, gathered[1:]):
        full[k] = _join_chips(g, _SHARD_AXIS[k])

    loss, grad_x, grads = _local_step(x, loss_target, full)

    by_chip = [_split_chips(grads[k], _BIG_AXIS[k]) for k in _BIG]
    theirs = _pair_partials_call(by_chip)
    sums = [_add_pair_call(b, r, where, "pair_sum_" + k) for k, b, r in zip(_BIG, by_chip, theirs)]
    got = _scatter_partials_call([s[0] for s in sums])
    fins = [_add_recv_call(s[1], r, where, "chip_sum_" + k) for k, s, r in zip(_BIG, sums, got)]
    out_g = dict(zip(_BIG, _pair_result_call(fins)))

    rest = [k for k in _ORDER if k not in _BIG]
    rest_shapes = [grads[k].shape for k in rest] + [(1,)]
    summed = _unpack_rows(_allreduce_small_call(_pack_rows([grads[k] for k in rest] + [loss.reshape(1)], F32)),
                          rest_shapes)
    out_g.update(zip(rest, summed[:-1]))
    out_g["a_dw"] = lax.dynamic_slice_in_dim(out_g["a_dw"], chip * a_dw.shape[2], a_dw.shape[2], axis=2)
    out_g["b_conv"] = lax.dynamic_slice_in_dim(out_g["b_conv"], chip * b_conv.shape[2], b_conv.shape[2], axis=2)

    def adam(k):
        operands = (ws[k], out_g[k], given["m_" + k], given["v_" + k])
        if k != "w_in":
            return _adamw_call(*operands, "adamw_" + k)
        w_t, g_t, m_t, v_t = (jnp.transpose(a, (2, 0, 1)) for a in operands)
        out_g[k] = jnp.transpose(g_t, (1, 2, 0))
        return tuple(jnp.transpose(o, (1, 2, 0)) for o in _adamw_call(w_t, g_t, m_t, v_t, "adamw_" + k))

    upd = {k: adam(k) for k in _ORDER}
    return (summed[-1][0], grad_x, *[out_g[k] for k in _ORDER], *[upd[k][0] for k in _ORDER],
            *[upd[k][1] for k in _ORDER], *[upd[k][2] for k in _ORDER])
```

```python
import functools
import math

import jax
import jax.numpy as jnp
from jax import lax
from jax.experimental import pallas as pl
from jax.experimental.pallas import tpu as pltpu

F32 = jnp.float32
BF16 = jnp.bfloat16
MESH = pl.DeviceIdType.MESH

D_MODEL = 1024
CONV_W = 512
CONV_K = 31
HEADS = 8
DK = 128
SHORT_K = 4
CHUNK = 64
SG_W = 512
SG_G = 4
SG_C = 128
EPS = 1e-6
N_ORIG = 10256
BA_ORIG = 5632
A_ORIG_W = 3 * CONV_W
B_W = 4 * D_MODEL
B_OFF, A_OFF, C_OFF, G_OFF, BA_OFF, N_R = 0, 4096, 5632, 7168, 10240, 10368
TB = 256
HALO_A = 32
HALO_B = 8
VMEM_LIMIT = 56 * 1024 * 1024
ADAM_LR, ADAM_B1, ADAM_B2, ADAM_EPS, ADAM_WD, ADAM_STEP = 0.001, 0.9, 0.999, 1e-08, 0.01, 10


def _pcall(body, **kw):
    return pl.pallas_call(body, **kw)


def _cparams(sem=None):
    kw = dict(vmem_limit_bytes=VMEM_LIMIT)
    if sem is not None:
        kw["dimension_semantics"] = sem
    return pltpu.CompilerParams(**kw)


def _sig(x):
    return jax.nn.sigmoid(x)


def _silu(x):
    return x * _sig(x)


def _dsilu(x):
    s = _sig(x)
    return s * (1.0 + x * (1.0 - s))


_GELU_C = math.sqrt(2.0 / math.pi)


def _gelu(x):
    return 0.5 * x * (1.0 + jnp.tanh(_GELU_C * (x + 0.044715 * x * x * x)))


def _dgelu(x):
    t = jnp.tanh(_GELU_C * (x + 0.044715 * x * x * x))
    return 0.5 * (1.0 + t) + 0.5 * x * (1.0 - t * t) * _GELU_C * (1.0 + 3 * 0.044715 * x * x)


def _softplus(x):
    return jnp.maximum(x, 0.0) + jnp.log1p(jnp.exp(-jnp.abs(x)))


def _dot(a, b, dims):
    return lax.dot_general(a.astype(BF16), b.astype(BF16), (dims, ((), ())), preferred_element_type=F32)


def _nn(a, b):
    return _dot(a, b, ((1,), (0,)))


def _nt(a, b):
    return _dot(a, b, ((1,), (1,)))


def _tn(a, b):
    return _dot(a, b, ((0,), (0,)))


def _tn_mxu(a, b):
    n = a.shape[1]
    eye = (lax.broadcasted_iota(jnp.int32, (n, n), 0) == lax.broadcasted_iota(jnp.int32, (n, n), 1)).astype(BF16)
    return _nn(_nt(eye, a), b)


def _mean(x):
    return jnp.mean(x, axis=-1, keepdims=True)


def _sum0(x):
    return jnp.sum(x, axis=0, keepdims=True)


def _sum1(x):
    return jnp.sum(x, axis=1, keepdims=True)


def _ln_fwd(x, g, b):
    xc = x - _mean(x)
    r = lax.rsqrt(_mean(xc * xc) + EPS)
    xh = xc * r
    return xh * g + b, xh, r


def _ln_bwd(dy, xh, r, g):
    dxh = dy * g
    return r * (dxh - _mean(dxh) - xh * _mean(dxh * xh)), _sum0(dy * xh), _sum0(dy)


def _rms_bwd(dxh, xh, r):
    return r * (dxh - xh * _mean(dxh * xh))


def _rows8(*rows):
    n = rows[0].shape[1]
    return jnp.concatenate(list(rows) + [jnp.zeros((8 - len(rows), n), F32)], axis=0)


def _windows(ext, tb):
    n = ext.shape[0] - 8
    shifted = {0: ext}

    def window(off):
        r = off % 8
        if r not in shifted:
            shifted[r] = ext[r:r + n]
        return shifted[r][off - r:off - r + tb]

    return window


def _a_fwd(val, glu, az, val_h, glu_h, w, bias, g, b):
    tb = val.shape[0]
    win = _windows(jnp.concatenate([val_h * _sig(glu_h), val * _sig(glu)], axis=0), tb)
    c = win(2) * w[0:1]
    for j in range(1, CONV_K):
        c = c + win(2 + j) * w[j:j + 1]
    c = c + bias
    ln, _, _ = _ln_fwd(c, g, b)
    return _silu(ln) * _silu(az), c


def _a_bwd(dy, val, glu, az, c, w, g, b, dc_next):
    tb = val.shape[0]
    ln, xh, r = _ln_fwd(c, g, b)
    sl, sz = _sig(ln), _sig(az)
    dln = dy * (az * sz) * (sl * (1.0 + ln * (1.0 - sl)))
    daz = dy * (ln * sl) * (sz * (1.0 + az * (1.0 - sz)))
    dc, dg, db = _ln_bwd(dln, xh, r, g)
    win = _windows(jnp.concatenate([dc, dc_next], axis=0), tb)
    sg = _sig(glu)
    a = val * sg
    da = win(30) * w[0:1]
    dw_rows = [_sum0(win(30) * a)]
    for j in range(1, CONV_K):
        shifted = win(30 - j)
        da = da + shifted * w[j:j + 1]
        dw_rows.append(_sum0(shifted * a))
    dw = jnp.concatenate(dw_rows + [jnp.zeros((1, CONV_W), F32)], axis=0)
    return da * sg, da * val * sg * (1.0 - sg), daz, dw, _sum0(dc), dg, db, dc[:HALO_A]


def _tril(ws):
    ii = lax.broadcasted_iota(jnp.int32, (SG_C, SG_C), 0)
    jj = lax.broadcasted_iota(jnp.int32, (SG_C, SG_C), 1)
    return [jnp.where(jj <= ii, ws[gi], 0.0) for gi in range(SG_G)], jj <= ii


def _c_mix(wt, vs, bias_full):
    tb = vs.shape[0]
    rows = []
    for n in range(tb // SG_C):
        blks = [_nn(wt[gi], vs[n * SG_C:(n + 1) * SG_C, gi * SG_C:(gi + 1) * SG_C]) for gi in range(SG_G)]
        rows.append(jnp.concatenate(blks, axis=1) + bias_full)
    return jnp.concatenate(rows, axis=0)


def _c_fwd(cu, cv, cz, g, b, ws, bias_full):
    wt, _ = _tril(ws)
    vs, xh, r = _ln_fwd(_gelu(cv), g, b)
    mixed = _c_mix(wt, vs, bias_full)
    return _gelu(cu) * mixed * _silu(cz), (wt, vs, xh, r, mixed)


def _c_bwd(dy, cu, cv, cz, g, b, ws, bias_full):
    tb = cu.shape[0]
    _, (wt, vs, xh, r, mixed) = _c_fwd(cu, cv, cz, g, b, ws, bias_full)
    _, low = _tril(ws)
    u, sz = _gelu(cu), _silu(cz)
    dcu = dy * mixed * sz * _dgelu(cu)
    dcz = dy * u * mixed * _dsilu(cz)
    dmixed = dy * u * sz
    dbs = jnp.zeros((SG_C, SG_W), F32)
    dws = [jnp.zeros((SG_C, SG_C), F32) for _ in range(SG_G)]
    rows = []
    for n in range(tb // SG_C):
        dm_n = dmixed[n * SG_C:(n + 1) * SG_C]
        dbs = dbs + dm_n
        blks = []
        for gi in range(SG_G):
            dm = dm_n[:, gi * SG_C:(gi + 1) * SG_C]
            dws[gi] = dws[gi] + _nt(dm, vs[n * SG_C:(n + 1) * SG_C, gi * SG_C:(gi + 1) * SG_C])
            blks.append(_tn(wt[gi], dm))
        rows.append(jnp.concatenate(blks, axis=1))
    dvs = jnp.concatenate(rows, axis=0)
    dgv, dg, db = _ln_bwd(dvs, xh, r, g)
    dws = [jnp.where(low, d, 0.0) for d in dws]
    return dcu, dgv * _dgelu(cv), dcz, dws, dbs, dg, db


def _m_fwd(ya, yb, yc, g0, g1, g2, wa, wb, wc):
    pa, pb, pc = _nn(ya, wa), _nn(yb, wb), _nn(yc, wc)
    s0, s1, s2 = _sig(g0), _sig(g1), _sig(g2)
    return s0 * pa + s1 * pb + s2 * pc, (pa, pb, pc, s0, s1, s2)


def _m_bwd(dout, ya, yb, yc, g0, g1, g2, wa, wb, wc, wo):
    merged, (pa, pb, pc, s0, s1, s2) = _m_fwd(ya, yb, yc, g0, g1, g2, wa, wb, wc)
    dm = _nt(dout, wo)
    dpa, dpb, dpc = dm * s0, dm * s1, dm * s2
    dgs = (dm * pa * s0 * (1.0 - s0), dm * pb * s1 * (1.0 - s1), dm * pc * s2 * (1.0 - s2))
    return (_nt(dpa, wa), _nt(dpb, wb), _nt(dpc, wc)), dgs, merged, (dpa, dpb, dpc)


def _chunk_masks(c):
    ii = lax.broadcasted_iota(jnp.int32, (c, c), 0)
    jj = lax.broadcasted_iota(jnp.int32, (c, c), 1)
    return ii, jj


def _dn_decay(items):
    ii, jj = _chunk_masks(CHUNK)
    incl, strict, eye = jj <= ii, jj < ii, ii == jj
    for d in items:
        g = d["g"]
        grow = _sum0(jnp.where(eye, g, 0.0))
        gc_col = _sum1(jnp.where(incl, grow, 0.0))
        gc_row = _sum0(jnp.where(ii <= jj, g, 0.0))
        gam_i = jnp.where(incl, jnp.exp(jnp.where(incl, gc_col - gc_row, 0.0)), 0.0)
        gl = _sum0(g)
        egc = jnp.exp(gc_col)
        d.update(gam_i=gam_i, gam_s=jnp.where(strict, gam_i, 0.0), egc=egc, ekd=jnp.exp(gl - gc_col), dl=jnp.exp(gl),
                 gdiff=gc_col - gc_row, qd=d["q"] * egc, rhs_w=d["k"] * (d["b"] * egc))
        d["kd"] = d["k"] * d["ekd"]
    return ii, jj, strict, eye


def _dn_solve(items, ii, jj, eye):
    off = ((ii >> 1) == (jj >> 1)) & ((ii & 1) != 0) & ((jj & 1) == 0)
    for d in items:
        a1 = jnp.where(off, d["a"], 0.0)
        d["t"] = jnp.where(eye, 1.0, 0.0) - a1
        d["m"] = d["a"] - _nn(a1, d["a"])
    b, sh = 2, 2
    while b < CHUNK:
        off = ((ii >> sh) == (jj >> sh)) & ((ii & b) != 0) & ((jj & b) == 0)
        for d in items:
            mo = jnp.where(off, d["m"], 0.0)
            if 2 * b < CHUNK:
                d["m"], d["t"] = d["m"] - _nn(mo, d["m"]), d["t"] - _nn(mo, d["t"])
            else:
                d["t"] = d["t"] - _nn(mo, d["t"])
        b, sh = 2 * b, sh + 1


def _dn_fwd_chunk(items, ss):
    ii, jj, strict, eye = _dn_decay(items)
    for d in items:
        d["a"] = d["b"] * _nt(d["k"], d["k"]) * d["gam_s"]
        d["qk"] = _nt(d["q"], d["k"]) * d["gam_i"]
    _dn_solve(items, ii, jj, eye)
    for d in items:
        d["u"], d["w"] = _nn(d["t"], d["v"] * d["b"]), _nn(d["t"], d["rhs_w"])
    ws = [_nn(d["w"], s) for d, s in zip(items, ss)]
    qs = [_nn(d["qd"], s) for d, s in zip(items, ss)]
    vnew = [d["u"] - w for d, w in zip(items, ws)]
    outs = [q + _nn(d["qk"], vn) for d, q, vn in zip(items, qs, vnew)]
    ss = [d["dl"] * s + _tn(d["kd"], vn) for d, s, vn in zip(items, ss, vnew)]
    return outs, ss, vnew


def _dn_bwd_chunk(items, ss, dss):
    ii, jj, strict, eye = _dn_decay(items)
    eye_b = eye.astype(BF16)
    for d in items:
        k, q = d["k"], d["q"]
        d["kk"] = _nt(k, k)
        d["a"] = d["b"] * d["kk"] * d["gam_s"]
        d["qk"] = _nt(q, k) * d["gam_i"]
        d["qkt"] = _nt(k, q) * jnp.where(ii <= jj, jnp.exp(jnp.where(ii <= jj, -d["gdiff"], 0.0)), 0.0)
    for d, s, ds2 in zip(items, ss, dss):
        d["dvnew"] = _nn(d["qkt"], d["do"]) + _nn(d["kd"], ds2)
        d["dqk"] = _nt(d["do"], d["vnew"])
        d["dqd"] = _nt(d["do"], s)
        d["dkd"] = _nt(d["vnew"], ds2)
    new_dss = []
    for d, s, ds2 in zip(items, ss, dss):
        d["dw"] = -_nt(d["dvnew"], s)
        new_dss.append(_tn_mxu(d["qd"], d["do"]) - _tn_mxu(d["w"], d["dvnew"]) + d["dl"] * ds2)
        d["ddl"] = _sum0(_sum1(s * ds2))
    for d in items:
        tt = _nt(eye_b, d["t"])
        d["drhs_u"], d["drhs_w"] = _nn(tt, d["dvnew"]), _nn(tt, d["dw"])
    for d in items:
        d["da"] = jnp.where(strict, -(_nt(d["drhs_u"], d["u"]) + _nt(d["drhs_w"], d["w"])), 0.0)
    outs = []
    for d in items:
        q, k, v, b, egc = d["q"], d["k"], d["v"], d["b"], d["egc"]
        drhs_u, drhs_w, da = d["drhs_u"], d["drhs_w"], d["da"]
        dbeta = _sum1(da * d["kk"] * d["gam_s"]) + _sum1(drhs_u * v) + _sum1(drhs_w * k) * egc
        dkk = da * b * d["gam_s"]
        e = da * d["a"] + d["dqk"] * d["qk"]
        s_kd = _sum1(d["dkd"] * d["kd"])
        dgc_col = _sum1(e) + _sum1(drhs_w * d["rhs_w"]) + _sum1(d["dqd"] * d["qd"]) - s_kd
        dgc_row = _sum0(jnp.where(eye, dgc_col, 0.0)) - _sum0(e)
        dg = _sum1(jnp.where(jj >= ii, dgc_row, 0.0)) + (_sum0(s_kd) + d["ddl"] * d["dl"])
        dqkg = d["dqk"] * d["gam_i"]
        dq = _nn(dqkg, k) + d["dqd"] * egc
        dk = _tn_mxu(dqkg, q) + _nn(dkk, k) + _tn_mxu(dkk, k) + drhs_w * (b * egc) + d["dkd"] * d["ekd"]
        outs.append((dq, dk, drhs_u * b, dbeta, dg))
    return outs, new_dss


def _short_conv(raw, halo, w):
    tb = raw.shape[0]
    ext = jnp.concatenate([halo, raw], axis=0)
    out = ext[5:5 + tb] * w[0:1]
    for j in range(1, SHORT_K):
        out = out + ext[5 + j:5 + j + tb] * w[j:j + 1]
    return out


def _dn_gates(ba, alog8, dtb8):
    xg = ba + dtb8
    ea8 = jnp.exp(alog8)
    return _sig(ba), -ea8 * _softplus(xg), xg, ea8


def _dn_item(pre_q, pre_k, pre_v, b, g):
    qc, kc, vc = (p * _sig(p) for p in (pre_q, pre_k, pre_v))
    nq = lax.rsqrt(_sum1(qc * qc) + EPS)
    nk = lax.rsqrt(_sum1(kc * kc) + EPS)
    return dict(q=qc * nq * (DK ** -0.5), k=kc * nk, v=vc, b=b, g=g, qc=qc, kc=kc, nq=nq, nk=nk)


def _dn_out(o, z, sz, og):
    r = lax.rsqrt(_mean(o * o) + EPS)
    xh = o * r
    return xh * og * (z * sz), xh, r


def _sds(shape, dtype):
    return jax.ShapeDtypeStruct(tuple(shape), dtype)


def _matmul(a, b, out_dtype, tm, tn, tk, name):
    m, kd = a.shape
    n = b.shape[1]
    tm, tn, tk = min(tm, m), min(tn, n), min(tk, kd)
    nk = kd // tk

    def body(a_ref, b_ref, o_ref, acc):
        @pl.when(pl.program_id(2) == 0)
        def _():
            acc[...] = jnp.zeros_like(acc)

        acc[...] += jnp.dot(a_ref[...], b_ref[...], preferred_element_type=F32)

        @pl.when(pl.program_id(2) == nk - 1)
        def _():
            o_ref[...] = acc[...].astype(o_ref.dtype)

    return _pcall(
        body, name=name, grid=(m // tm, n // tn, nk),
        in_specs=[pl.BlockSpec((tm, tk), lambda i, j, k: (i, k)), pl.BlockSpec((tk, tn), lambda i, j, k: (k, j))],
        out_specs=pl.BlockSpec((tm, tn), lambda i, j, k: (i, j)), out_shape=_sds((m, n), out_dtype),
        scratch_shapes=[pltpu.VMEM((tm, tn), F32)], compiler_params=_cparams(("parallel", "parallel", "arbitrary")),
    )(a, b)


def _matmul_ksegs(a_segs, b, tm, tn, tk, name):
    m, n = a_segs[0].shape[0], b.shape[1]
    tm, tn = min(tm, m), min(tn, n)
    main, tail = a_segs[:-1], a_segs[-1]
    steps = [s.shape[1] // tk for s in main]
    starts = [sum(steps[:i]) for i in range(len(main))]
    nk = sum(steps)
    wt = tail.shape[1]

    def body(*refs):
        a_refs, at_ref, b_ref, bt_ref, o_ref, acc = refs[:len(main)], *refs[len(main):]
        k = pl.program_id(2)

        @pl.when(k == 0)
        def _():
            acc[...] = jnp.zeros_like(acc)

        for a_ref, k0, ns in zip(a_refs, starts, steps):
            @pl.when((k >= k0) & (k < k0 + ns))
            def _():
                acc[...] += jnp.dot(a_ref[...], b_ref[...], preferred_element_type=F32)

        @pl.when(k == nk)
        def _():
            o_ref[...] = acc[...] + jnp.dot(at_ref[...], bt_ref[...], preferred_element_type=F32)

    seg_specs = [pl.BlockSpec((tm, tk), functools.partial(lambda i, j, k, k0, ns: (i, jnp.clip(k - k0, 0, ns - 1)), k0=k0, ns=ns))
                 for k0, ns in zip(starts, steps)]
    return _pcall(
        body, name=name, grid=(m // tm, n // tn, nk + 1),
        in_specs=seg_specs + [pl.BlockSpec((tm, wt), lambda i, j, k: (i, 0)),
                              pl.BlockSpec((tk, tn), lambda i, j, k: (jnp.minimum(k, nk - 1), j)),
                              pl.BlockSpec((wt, tn), lambda i, j, k: (nk * tk // wt, j))],
        out_specs=pl.BlockSpec((tm, tn), lambda i, j, k: (i, j)), out_shape=_sds((m, n), F32),
        scratch_shapes=[pltpu.VMEM((tm, tn), F32)], compiler_params=_cparams(("parallel", "parallel", "arbitrary")),
    )(*main, tail, b, b)


def _rms_fwd_call(x, g, name):
    m = x.shape[0]
    tm = min(512, m)

    def body(x_ref, g_ref, o_ref):
        xv = x_ref[...]
        o_ref[...] = (xv * lax.rsqrt(_mean(xv * xv) + EPS) * g_ref[...]).astype(BF16)

    return _pcall(
        body, name=name, grid=(m // tm,),
        in_specs=[pl.BlockSpec((tm, D_MODEL), lambda i: (i, 0)), pl.BlockSpec((1, D_MODEL), lambda i: (0, 0))],
        out_specs=pl.BlockSpec((tm, D_MODEL), lambda i: (i, 0)), out_shape=_sds((m, D_MODEL), BF16),
        compiler_params=_cparams(("parallel",)),
    )(x, g)


def _rms_bwd_call(x, dh, g, dres, name):
    m = x.shape[0]
    tm = min(512, m)

    def body(x_ref, dh_ref, g_ref, dr_ref, dx_ref, dg_ref):
        @pl.when(pl.program_id(0) == 0)
        def _():
            dg_ref[...] = jnp.zeros_like(dg_ref)

        xv, dhv = x_ref[...], dh_ref[...]
        r = lax.rsqrt(_mean(xv * xv) + EPS)
        xh = xv * r
        dx_ref[...] = _rms_bwd(dhv * g_ref[...], xh, r) + dr_ref[...]
        dg_ref[...] += _rows8(_sum0(dhv * xh))

    row = pl.BlockSpec((tm, D_MODEL), lambda i: (i, 0))
    return _pcall(
        body, name=name, grid=(m // tm,),
        in_specs=[row, row, pl.BlockSpec((1, D_MODEL), lambda i: (0, 0)), row],
        out_specs=[row, pl.BlockSpec((8, D_MODEL), lambda i: (0, 0))],
        out_shape=[_sds((m, D_MODEL), F32), _sds((8, D_MODEL), F32)], compiler_params=_cparams(("arbitrary",)),
    )(x, dh, g, dres)


def _loss_call(x, tgt, g):
    m = x.shape[0]
    tm = min(512, m)

    def body(x_ref, t_ref, g_ref, dx_ref, dg_ref, l_ref):
        @pl.when(pl.program_id(0) == 0)
        def _():
            dg_ref[...] = jnp.zeros_like(dg_ref)
            l_ref[...] = jnp.zeros_like(l_ref)

        xv = x_ref[...]
        r = lax.rsqrt(_mean(xv * xv) + EPS)
        xh = xv * r
        err = xh * g_ref[...] - t_ref[...]
        dy = err * (1.0 / D_MODEL)
        dx_ref[...] = _rms_bwd(dy * g_ref[...], xh, r)
        dg_ref[...] += _rows8(_sum0(dy * xh))
        l_ref[...] += 0.5 * _sum0(_mean(err * err))

    row = pl.BlockSpec((tm, D_MODEL), lambda i: (i, 0))
    return _pcall(
        body, name="loss_head", grid=(m // tm,),
        in_specs=[row, row, pl.BlockSpec((1, D_MODEL), lambda i: (0, 0))],
        out_specs=[row, pl.BlockSpec((8, D_MODEL), lambda i: (0, 0)), pl.BlockSpec((8, 128), lambda i: (0, 0))],
        out_shape=[_sds((m, D_MODEL), F32), _sds((8, D_MODEL), F32), _sds((8, 128), F32)],
        compiler_params=_cparams(("arbitrary",)),
    )(x, tgt, g)


def _halo_idx(i, rows):
    return jnp.maximum(i * (TB // rows) - 1, 0)


def _a_tiles(nt, rev):
    ti = (lambda i: nt - 1 - i) if rev else (lambda i: i)
    c0 = A_OFF // CONV_W
    return [pl.BlockSpec((1, TB, CONV_W), functools.partial(lambda b, i, c: (b, ti(i), c), c=c0 + c)) for c in range(3)]


def _a_fwd_call(proj, w, bias, g, b, name):
    bsz, t, _ = proj.shape
    nt = t // TB
    c0 = A_OFF // CONV_W

    def body(val_ref, glu_ref, az_ref, vh_ref, gh_ref, w_ref, bias_ref, g_ref, b_ref, y_ref, c_ref):
        keep = jnp.where(pl.program_id(1) > 0, 1.0, 0.0)
        y, c = _a_fwd(val_ref[0], glu_ref[0], az_ref[0], vh_ref[0] * keep, gh_ref[0], w_ref[...], bias_ref[...],
                      g_ref[...], b_ref[...])
        y_ref[0] = y.astype(BF16)
        c_ref[0] = c

    halo = [pl.BlockSpec((1, HALO_A, CONV_W), functools.partial(lambda b, i, c: (b, _halo_idx(i, HALO_A), c), c=c0 + c))
            for c in range(2)]
    par = [pl.BlockSpec((HALO_A, CONV_W), lambda b, i: (0, 0))] + [pl.BlockSpec((1, CONV_W), lambda b, i: (0, 0))] * 3
    tile = pl.BlockSpec((1, TB, CONV_W), lambda b, i: (b, i, 0))
    return _pcall(
        body, name=name, grid=(bsz, nt), in_specs=_a_tiles(nt, False) + halo + par, out_specs=[tile, tile],
        out_shape=[_sds((bsz, t, CONV_W), BF16), _sds((bsz, t, CONV_W), F32)],
        compiler_params=_cparams(("parallel", "parallel")),
    )(proj, proj, proj, proj, proj, w, bias, g, b)


def _a_bwd_call(dy, conv, proj, w, g, b, name):
    bsz, t, _ = proj.shape
    nt = t // TB

    def body(dy_ref, c_ref, val_ref, glu_ref, az_ref, w_ref, g_ref, b_ref, da_ref, dw_ref, ds_ref, carry):
        ip = pl.program_id(1)

        @pl.when((pl.program_id(0) == 0) & (ip == 0))
        def _():
            dw_ref[...] = jnp.zeros_like(dw_ref)
            ds_ref[...] = jnp.zeros_like(ds_ref)

        @pl.when(ip == 0)
        def _():
            carry[...] = jnp.zeros_like(carry)

        dval, dglu, daz, dw, dbias, dg, db, head = _a_bwd(dy_ref[0], val_ref[0], glu_ref[0], az_ref[0], c_ref[0],
                                                          w_ref[...], g_ref[...], b_ref[...], carry[...])
        carry[...] = head
        for n, dcol in enumerate((dval, dglu, daz)):
            da_ref[0, :, n * CONV_W:(n + 1) * CONV_W] = dcol.astype(BF16)
        dw_ref[...] += dw
        ds_ref[...] += _rows8(dbias, dg, db)

    rtile = lambda b, i: (b, nt - 1 - i, 0)
    par = [pl.BlockSpec((HALO_A, CONV_W), lambda b, i: (0, 0))] + [pl.BlockSpec((1, CONV_W), lambda b, i: (0, 0))] * 2
    return _pcall(
        body, name=name, grid=(bsz, nt),
        in_specs=[pl.BlockSpec((1, TB, CONV_W), rtile)] * 2 + _a_tiles(nt, True) + par,
        out_specs=[pl.BlockSpec((1, TB, 3 * CONV_W), rtile), pl.BlockSpec((HALO_A, CONV_W), lambda b, i: (0, 0)),
                   pl.BlockSpec((8, CONV_W), lambda b, i: (0, 0))],
        out_shape=[_sds((bsz, t, 3 * CONV_W), BF16), _sds((HALO_A, CONV_W), F32), _sds((8, CONV_W), F32)],
        scratch_shapes=[pltpu.VMEM((HALO_A, CONV_W), F32)], compiler_params=_cparams(("arbitrary", "arbitrary")),
    )(dy, conv, proj, proj, proj, w, g, b)


def _c_specs():
    c0 = C_OFF // SG_W
    tile = [pl.BlockSpec((1, TB, SG_W), functools.partial(lambda b, i, c: (b, i, c), c=c0 + c)) for c in range(3)]
    par = [pl.BlockSpec((1, SG_W), lambda b, i: (0, 0))] * 2 + [
        pl.BlockSpec((SG_G, SG_C, SG_C), lambda b, i: (0, 0, 0)), pl.BlockSpec((SG_C, SG_W), lambda b, i: (0, 0))]
    return tile + par


def _c_fwd_call(proj, g, b, ws, bias_full, name):
    bsz, t, _ = proj.shape

    def body(cu_ref, cv_ref, cz_ref, g_ref, b_ref, ws_ref, bf_ref, y_ref):
        y, _ = _c_fwd(cu_ref[0], cv_ref[0], cz_ref[0], g_ref[...], b_ref[...], ws_ref[...], bf_ref[...])
        y_ref[0] = y.astype(BF16)

    return _pcall(
        body, name=name, grid=(bsz, t // TB), in_specs=_c_specs(),
        out_specs=pl.BlockSpec((1, TB, SG_W), lambda b, i: (b, i, 0)), out_shape=_sds((bsz, t, SG_W), BF16),
        compiler_params=_cparams(("parallel", "parallel")),
    )(proj, proj, proj, g, b, ws, bias_full)


def _c_bwd_call(dy, proj, g, b, ws, bias_full, name):
    bsz, t, _ = proj.shape

    def body(dy_ref, cu_ref, cv_ref, cz_ref, g_ref, b_ref, ws_ref, bf_ref, dc_ref, dws_ref, dbs_ref, ds_ref):
        @pl.when((pl.program_id(0) == 0) & (pl.program_id(1) == 0))
        def _():
            dws_ref[...] = jnp.zeros_like(dws_ref)
            dbs_ref[...] = jnp.zeros_like(dbs_ref)
            ds_ref[...] = jnp.zeros_like(ds_ref)

        dcu, dcv, dcz, dws, dbs, dg, db = _c_bwd(dy_ref[0], cu_ref[0], cv_ref[0], cz_ref[0], g_ref[...], b_ref[...],
                                                 ws_ref[...], bf_ref[...])
        for n, dcol in enumerate((dcu, dcv, dcz)):
            dc_ref[0, :, n * SG_W:(n + 1) * SG_W] = dcol.astype(BF16)
        for gi in range(SG_G):
            dws_ref[gi] += dws[gi]
        dbs_ref[...] += dbs
        ds_ref[...] += _rows8(dg, db)

    tile = lambda b, i: (b, i, 0)
    return _pcall(
        body, name=name, grid=(bsz, t // TB), in_specs=[pl.BlockSpec((1, TB, SG_W), tile)] + _c_specs(),
        out_specs=[pl.BlockSpec((1, TB, 3 * SG_W), tile), pl.BlockSpec((SG_G, SG_C, SG_C), lambda b, i: (0, 0, 0)),
                   pl.BlockSpec((SG_C, SG_W), lambda b, i: (0, 0)), pl.BlockSpec((8, SG_W), lambda b, i: (0, 0))],
        out_shape=[_sds((bsz, t, 3 * SG_W), BF16), _sds((SG_G, SG_C, SG_C), F32), _sds((SG_C, SG_W), F32),
                   _sds((8, SG_W), F32)],
        compiler_params=_cparams(("arbitrary", "arbitrary")),
    )(dy, proj, proj, proj, g, b, ws, bias_full)


def _m_specs():
    g0 = G_OFF // D_MODEL
    y = [pl.BlockSpec((1, TB, n), lambda b, i: (b, i, 0)) for n in (CONV_W, D_MODEL, SG_W)]
    gates = [pl.BlockSpec((1, TB, D_MODEL), functools.partial(lambda b, i, c: (b, i, c), c=g0 + c)) for c in range(3)]
    return y + gates


def _w_specs(*shapes):
    return [pl.BlockSpec(s, lambda b, i: (0, 0)) for s in shapes]


def _m_fwd_call(ya, yb, yc, proj, x, wa, wb, wc, wo, name):
    bsz, t, _ = x.shape

    def body(ya_ref, yb_ref, yc_ref, g0_ref, g1_ref, g2_ref, x_ref, wa_ref, wb_ref, wc_ref, wo_ref, o_ref):
        merged, _ = _m_fwd(ya_ref[0], yb_ref[0], yc_ref[0], g0_ref[0], g1_ref[0], g2_ref[0], wa_ref[...], wb_ref[...],
                           wc_ref[...])
        o_ref[0] = x_ref[0] + _nn(merged, wo_ref[...])

    tile = pl.BlockSpec((1, TB, D_MODEL), lambda b, i: (b, i, 0))
    return _pcall(
        body, name=name, grid=(bsz, t // TB),
        in_specs=_m_specs() + [tile] + _w_specs(wa.shape, wb.shape, wc.shape, wo.shape),
        out_specs=tile, out_shape=_sds(x.shape, F32), compiler_params=_cparams(("parallel", "parallel")),
    )(ya, yb, yc, proj, proj, proj, x, wa, wb, wc, wo)


def _m_bwd_call(dout, ya, yb, yc, proj, wa, wb, wc, wo, name):
    bsz, t, _ = dout.shape

    def body(do_ref, ya_ref, yb_ref, yc_ref, g0_ref, g1_ref, g2_ref, wa_ref, wb_ref, wc_ref, wo_ref, dya_ref, dyb_ref,
             dyc_ref, dg_ref, mg_ref, dp_ref):
        dys, dgs, merged, dps = _m_bwd(do_ref[0], ya_ref[0], yb_ref[0], yc_ref[0], g0_ref[0], g1_ref[0], g2_ref[0],
                                       wa_ref[...], wb_ref[...], wc_ref[...], wo_ref[...])
        dya_ref[0], dyb_ref[0], dyc_ref[0] = dys
        mg_ref[0] = merged.astype(BF16)
        for n in range(3):
            dg_ref[0, :, n * D_MODEL:(n + 1) * D_MODEL] = dgs[n].astype(BF16)
            dp_ref[0, :, n * D_MODEL:(n + 1) * D_MODEL] = dps[n].astype(BF16)

    tile = lambda n: pl.BlockSpec((1, TB, n), lambda b, i: (b, i, 0))
    widths = (CONV_W, D_MODEL, SG_W, 3 * D_MODEL, D_MODEL, 3 * D_MODEL)
    dts = (F32, F32, F32, BF16, BF16, BF16)
    return _pcall(
        body, name=name, grid=(bsz, t // TB),
        in_specs=[tile(D_MODEL)] + _m_specs() + _w_specs(wa.shape, wb.shape, wc.shape, wo.shape),
        out_specs=[tile(n) for n in widths], out_shape=[_sds((bsz, t, n), d) for n, d in zip(widths, dts)],
        compiler_params=_cparams(("parallel", "parallel")),
    )(dout, ya, yb, yc, proj, proj, proj, wa, wb, wc, wo)


def _m_wgrad_call(ya, yb, yc, merged, dps, dout, name):
    bsz, t, _ = dout.shape

    def body(ya_ref, yb_ref, yc_ref, mg_ref, dp_ref, do_ref, dwa_ref, dwb_ref, dwc_ref, dwo_ref):
        @pl.when((pl.program_id(0) == 0) & (pl.program_id(1) == 0))
        def _():
            for r in (dwa_ref, dwb_ref, dwc_ref, dwo_ref):
                r[...] = jnp.zeros_like(r)

        dp = dp_ref[0]
        dwa_ref[...] += _tn(ya_ref[0], dp[:, :D_MODEL])
        dwb_ref[...] += _tn(yb_ref[0], dp[:, D_MODEL:2 * D_MODEL])
        dwc_ref[...] += _tn(yc_ref[0], dp[:, 2 * D_MODEL:])
        dwo_ref[...] += _tn(mg_ref[0], do_ref[0])

    tile = lambda n: pl.BlockSpec((1, TB, n), lambda b, i: (b, i, 0))
    shapes = ((CONV_W, D_MODEL), (D_MODEL, D_MODEL), (SG_W, D_MODEL), (D_MODEL, D_MODEL))
    return _pcall(
        body, name=name, grid=(bsz, t // TB),
        in_specs=[tile(CONV_W), tile(D_MODEL), tile(SG_W), tile(D_MODEL), tile(3 * D_MODEL), tile(D_MODEL)],
        out_specs=_w_specs(*shapes), out_shape=[_sds(s, F32) for s in shapes],
        compiler_params=_cparams(("arbitrary", "arbitrary")),
    )(ya, yb, yc, merged, dps, dout)


def _dn_specs(nc, rev):
    ti = (lambda i: nc - 1 - i) if rev else (lambda i: i)
    return [pl.BlockSpec((1, CHUNK, B_W), lambda b, i: (b, ti(i), 0)),
            pl.BlockSpec((1, CHUNK, DK), lambda b, i: (b, ti(i), BA_OFF // DK)),
            pl.BlockSpec((SHORT_K, 3 * D_MODEL), lambda b, i: (0, 0))] + [pl.BlockSpec((1, DK), lambda b, i: (0, 0))] * 3


def _dn_saved_specs(nc, rev):
    ti = (lambda i: nc - 1 - i) if rev else (lambda i: i)
    return [pl.BlockSpec((1, CHUNK, D_MODEL), lambda b, i: (b, ti(i), 0)),
            pl.BlockSpec((1, CHUNK, 3 * D_MODEL), lambda b, i: (b, ti(i), 0)),
            pl.BlockSpec((1, HEADS, 1, DK, DK), lambda b, i: (b, 0, ti(i), 0, 0)),
            pl.BlockSpec((1, 1, CHUNK, HEADS * CHUNK), lambda b, i: (b, ti(i), 0, 0)),
            pl.BlockSpec((1, CHUNK, 2 * D_MODEL), lambda b, i: (b, ti(i), 0)),
            pl.BlockSpec((1, CHUNK, D_MODEL), lambda b, i: (b, ti(i), 0))]


def _dn_fwd_call(proj, wconv, alog, dtb, og, name):
    bsz, t, _ = proj.shape
    nc = t // CHUNK

    def body(x_ref, ba_ref, w_ref, al_ref, dt_ref, og_ref, halo_ref, y_ref, o_ref, pre_ref, st_ref, t_ref, uw_ref, vn_ref,
             s_scr):
        i = pl.program_id(1)

        @pl.when(i == 0)
        def _():
            s_scr[...] = jnp.zeros_like(s_scr)

        keep = jnp.where(i > 0, 1.0, 0.0)
        bsig, gfull, _, _ = _dn_gates(ba_ref[0], al_ref[...], dt_ref[...])
        items = []
        for h in range(HEADS):
            pres = []
            for n in range(3):
                cs = slice(n * D_MODEL + h * DK, n * D_MODEL + (h + 1) * DK)
                pres.append(_short_conv(x_ref[0, :, cs], halo_ref[0, :, cs] * keep, w_ref[:, cs]))
                pre_ref[0, :, cs] = pres[-1]
            items.append(_dn_item(*pres, bsig[:, h:h + 1], gfull[:, HEADS + h:HEADS + h + 1]))
        ss = [s_scr[h] for h in range(HEADS)]
        for h in range(HEADS):
            st_ref[0, h, 0] = ss[h]
        outs, ss, vnew = _dn_fwd_chunk(items, ss)
        og_v = og_ref[...]
        for h, d in enumerate(items):
            s_scr[h] = ss[h]
            hs = slice(h * DK, (h + 1) * DK)
            z = x_ref[0, :, 3 * D_MODEL + h * DK:3 * D_MODEL + (h + 1) * DK]
            o_ref[0, :, hs] = outs[h]
            y_ref[0, :, hs] = _dn_out(outs[h], z, _sig(z), og_v)[0].astype(BF16)
            t_ref[0, 0, :, h * CHUNK:(h + 1) * CHUNK] = d["t"].astype(BF16)
            uw_ref[0, :, 2 * h * DK:(2 * h + 1) * DK] = d["u"].astype(BF16)
            uw_ref[0, :, (2 * h + 1) * DK:2 * (h + 1) * DK] = d["w"].astype(BF16)
            vn_ref[0, :, hs] = vnew[h].astype(BF16)

    halo = pl.BlockSpec((1, HALO_B, 3 * D_MODEL), lambda b, i: (b, jnp.maximum(i * (CHUNK // HALO_B) - 1, 0), 0))
    return _pcall(
        body, name=name, grid=(bsz, nc), in_specs=_dn_specs(nc, False) + [halo],
        out_specs=[pl.BlockSpec((1, CHUNK, D_MODEL), lambda b, i: (b, i, 0))] + _dn_saved_specs(nc, False),
        out_shape=[_sds((bsz, t, D_MODEL), BF16), _sds((bsz, t, D_MODEL), F32), _sds((bsz, t, 3 * D_MODEL), F32),
                   _sds((bsz, HEADS, nc, DK, DK), F32), _sds((bsz, nc, CHUNK, HEADS * CHUNK), BF16),
                   _sds((bsz, t, 2 * D_MODEL), BF16), _sds((bsz, t, D_MODEL), BF16)],
        scratch_shapes=[pltpu.VMEM((HEADS, DK, DK), F32)], compiler_params=_cparams(("arbitrary", "arbitrary")),
    )(proj, proj, wconv, alog, dtb, og, proj)


def _dn_bwd_call(dy, saved, proj, wconv, alog, dtb, og, name):
    bsz, t, _ = proj.shape
    nc = t // CHUNK

    def body(dy_ref, o_ref, pre_ref, st_ref, t_ref, uw_ref, vn_ref, x_ref, ba_ref, w_ref, al_ref, dt_ref, og_ref,
             dx_ref, dba_ref, dwc_ref, dsm_ref, ds_scr, dpre_scr):
        ip = pl.program_id(1)

        @pl.when((pl.program_id(0) == 0) & (ip == 0))
        def _():
            dwc_ref[...] = jnp.zeros_like(dwc_ref)
            dsm_ref[...] = jnp.zeros_like(dsm_ref)

        @pl.when(ip == 0)
        def _():
            ds_scr[...] = jnp.zeros_like(ds_scr)
            dpre_scr[...] = jnp.zeros_like(dpre_scr)

        bsig, gfull, xg, ea8 = _dn_gates(ba_ref[0], al_ref[...], dt_ref[...])
        og_v = og_ref[...]
        items, dog = [], jnp.zeros((1, DK), F32)
        for h in range(HEADS):
            hs = slice(h * DK, (h + 1) * DK)
            zs = slice(3 * D_MODEL + h * DK, 3 * D_MODEL + (h + 1) * DK)
            d = _dn_item(*(pre_ref[0, :, n * D_MODEL + h * DK:n * D_MODEL + (h + 1) * DK] for n in range(3)),
                         bsig[:, h:h + 1], gfull[:, HEADS + h:HEADS + h + 1])
            z, dyv = x_ref[0, :, zs], dy_ref[0, :, hs]
            sz = _sig(z)
            _, xh, r = _dn_out(o_ref[0, :, hs], z, sz, og_v)
            dx_ref[0, :, zs] = (dyv * xh * og_v * (sz * (1.0 + z * (1.0 - sz)))).astype(BF16)
            don = dyv * (z * sz)
            dog = dog + _sum0(don * xh)
            d.update(do=_rms_bwd(don * og_v, xh, r), t=t_ref[0, 0, :, h * CHUNK:(h + 1) * CHUNK],
                     u=uw_ref[0, :, 2 * h * DK:(2 * h + 1) * DK], w=uw_ref[0, :, (2 * h + 1) * DK:2 * (h + 1) * DK],
                     vnew=vn_ref[0, :, hs])
            items.append(d)
        grads, dss = _dn_bwd_chunk(items, [st_ref[0, h, 0] for h in range(HEADS)], [ds_scr[h] for h in range(HEADS)])
        lanes = lax.broadcasted_iota(jnp.int32, (CHUNK, DK), 1)
        lane8 = lax.broadcasted_iota(jnp.int32, (8, DK), 1)
        row8 = lax.broadcasted_iota(jnp.int32, (8, DK), 0)
        dgdx = -ea8 * _sig(xg)
        dba = jnp.zeros((CHUNK, DK), F32)
        dsm = jnp.where(row8 == 2, dog, 0.0)
        for h, (d, (dq, dk, dv, dbeta, dg)) in enumerate(zip(items, grads)):
            ds_scr[h] = dss[h]
            qc, kc, nq, nk = d["qc"], d["kc"], d["nq"], d["nk"]
            dacts = ((DK ** -0.5) * nq * (dq - qc * (nq * nq * _sum1(dq * qc))),
                     nk * (dk - kc * (nk * nk * _sum1(dk * kc))), dv)
            dal = dg * dgdx[:, HEADS + h:HEADS + h + 1]
            dba = dba + jnp.where(lanes == h, dbeta * d["b"] * (1.0 - d["b"]), 0.0) + jnp.where(lanes == HEADS + h, dal, 0.0)
            dsm = dsm + (jnp.where((row8 == 0) & (lane8 == h), _sum0(dg * d["g"]), 0.0)
                         + jnp.where((row8 == 1) & (lane8 == h), _sum0(dal), 0.0))
            for n in range(3):
                cs = slice(n * D_MODEL + h * DK, n * D_MODEL + (h + 1) * DK)
                pre, raw, w = pre_ref[0, :, cs], x_ref[0, :, cs], w_ref[:, cs]
                sp = _sig(pre)
                dpre = dacts[n] * (sp * (1.0 + pre * (1.0 - sp)))
                dext = jnp.concatenate([dpre, dpre_scr[:, cs]], axis=0)
                draw = dext[3:3 + CHUNK] * w[0:1]
                rows = [_sum0(dext[3:3 + CHUNK] * raw)]
                for j in range(1, SHORT_K):
                    shifted = dext[3 - j:3 - j + CHUNK]
                    draw = draw + shifted * w[j:j + 1]
                    rows.append(_sum0(shifted * raw))
                dpre_scr[:, cs] = dpre[:HALO_B]
                dx_ref[0, :, cs] = draw.astype(BF16)
                dwc_ref[:, cs] += _rows8(*rows)
        dba_ref[0] = dba.astype(BF16)
        dsm_ref[...] += dsm

    rrow = lambda n: pl.BlockSpec((1, CHUNK, n), lambda b, i: (b, nc - 1 - i, 0))
    return _pcall(
        body, name=name, grid=(bsz, nc), in_specs=[rrow(D_MODEL)] + _dn_saved_specs(nc, True) + _dn_specs(nc, True),
        out_specs=[rrow(B_W), rrow(DK), pl.BlockSpec((8, 3 * D_MODEL), lambda b, i: (0, 0)),
                   pl.BlockSpec((8, DK), lambda b, i: (0, 0))],
        out_shape=[_sds((bsz, t, B_W), BF16), _sds((bsz, t, DK), BF16), _sds((8, 3 * D_MODEL), F32), _sds((8, DK), F32)],
        scratch_shapes=[pltpu.VMEM((HEADS, DK, DK), F32), pltpu.VMEM((HALO_B, 3 * D_MODEL), F32)],
        compiler_params=_cparams(("arbitrary", "arbitrary")),
    )(dy, *saved, proj, proj, wconv, alog, dtb, og)


def _adamw_call(w, g, m, v, name):
    shape = w.shape
    view = (math.prod(shape[:-2]),) + shape[-2:] if len(shape) >= 2 else (1, 1) + shape
    lead, rows, cols = view
    tr, tl = (128, 1) if rows % 128 == 0 else (rows, min(lead, 64))
    c1, c2 = 1.0 - ADAM_B1 ** ADAM_STEP, 1.0 - ADAM_B2 ** ADAM_STEP

    def body(w_ref, g_ref, m_ref, v_ref, d_ref, nm_ref, nv_ref):
        gv = g_ref[...]
        nm = ADAM_B1 * m_ref[...] + (1.0 - ADAM_B1) * gv
        nv = ADAM_B2 * v_ref[...] + (1.0 - ADAM_B2) * (gv * gv)
        d_ref[...] = -ADAM_LR * ((nm / c1) / (jnp.sqrt(nv / c2) + ADAM_EPS) + ADAM_WD * w_ref[...])
        nm_ref[...] = nm
        nv_ref[...] = nv

    blk = pl.BlockSpec((tl, tr, cols), lambda l, i: (l, i, 0))
    outs = _pcall(
        body, name=name, grid=(pl.cdiv(lead, tl), rows // tr), in_specs=[blk] * 4, out_specs=[blk] * 3,
        out_shape=[_sds(view, F32)] * 3, compiler_params=_cparams(("parallel", "parallel")),
    )(*(a.reshape(view) for a in (w, g, m, v)))
    return tuple(o.reshape(shape) for o in outs)


def _row_tile(rows, cap=512):
    for tr in range(cap, 15, -16):
        if rows % tr == 0:
            return tr
    return rows


def _add_pair_call(by_chip, recv, where, name):
    _, n, rows, cols = by_chip.shape
    tr = _row_tile(rows, 256)

    def body(where_ref, a_ref, b_ref, sb_ref, own_ref):
        s = a_ref[0, 0] + b_ref[0]
        sb_ref[0] = s.astype(BF16)

        @pl.when(pl.program_id(1) == where_ref[1])
        def _():
            own_ref[...] = s

    grid_spec = pltpu.PrefetchScalarGridSpec(
        num_scalar_prefetch=1, grid=(rows // tr, n),
        in_specs=[pl.BlockSpec((1, 1, tr, cols), lambda i, j, wr: (wr[0], j, i, 0)),
                  pl.BlockSpec((1, tr, cols), lambda i, j, wr: (j, i, 0))],
        out_specs=[pl.BlockSpec((1, tr, cols), lambda i, j, wr: (j, i, 0)),
                   pl.BlockSpec((tr, cols), lambda i, j, wr: (i, 0))])
    return _pcall(
        body, name=name, grid_spec=grid_spec, out_shape=[_sds((n, rows, cols), BF16), _sds((rows, cols), F32)],
        compiler_params=_cparams(("parallel", "arbitrary")),
    )(where, by_chip, recv)


def _add_recv_call(own, recv, where, name):
    rows, cols = own.shape
    tr = _row_tile(rows, 256)

    def body(where_ref, o_ref, r_ref, s_ref):
        s_ref[0] = ((o_ref[...] + r_ref[0].astype(F32)) + r_ref[1].astype(F32)) + r_ref[2].astype(F32)

    grid_spec = pltpu.PrefetchScalarGridSpec(
        num_scalar_prefetch=1, grid=(rows // tr,),
        in_specs=[pl.BlockSpec((tr, cols), lambda i, wr: (i, 0)), pl.BlockSpec((3, tr, cols), lambda i, wr: (0, i, 0))],
        out_specs=pl.BlockSpec((1, tr, cols), lambda i, wr: (wr[0], i, 0)))
    return _pcall(
        body, name=name, grid_spec=grid_spec, out_shape=_sds((2, rows, cols), F32),
        compiler_params=_cparams(("parallel",)),
    )(where, own, recv)


def _axes():
    return lax.axis_index("x"), lax.axis_index("y"), lax.axis_index("c")


def _chip_peers(x, y):
    return [(x, 1 - y), (1 - x, y), (1 - x, 1 - y)]


_ANY = pl.BlockSpec(memory_space=pl.ANY)
_VMEM = pl.BlockSpec(memory_space=pltpu.VMEM)


def _remote(src, dst, send_sems, recv_sems, k, dev):
    return pltpu.make_async_remote_copy(src_ref=src, dst_ref=dst, send_sem=send_sems.at[k], recv_sem=recv_sems.at[k],
                                        device_id=dev, device_id_type=MESH)


def _gather_weights_call(shards):
    n = len(shards)
    halves = [s.shape[0] // 2 for s in shards]
    quarter = halves[0] // 2
    base = [0] + [8 + 6 * (a - 1) for a in range(1, n)]

    def body(*refs):
        x_refs, o_refs, (send_sems, recv_sems) = refs[:n], refs[n:2 * n], refs[2 * n:]
        x, y, c = _axes()
        chip, sib, peers = 2 * x + y, (x, y, 1 - c), _chip_peers(x, y)
        pchip = [2 * px + py for px, py in peers]

        def half(a, cc):
            return pl.ds(cc * halves[a], halves[a])

        def part(cc, q):
            return pl.ds(cc * halves[0] + q * quarter, quarter)

        def copy(ref, k, dev):
            return _remote(ref, ref, send_sems, recv_sems, k, dev)

        sends = [_remote(x_refs[a].at[half(a, c)], o_refs[a].at[chip, half(a, c)], send_sems, recv_sems, base[a] + k,
                         (*peers[k], c)) for k in range(3) for a in range(n) if a > 0 or k < 2]
        for cp in sends:
            cp.start()
        for k in range(2):
            copy(o_refs[0].at[pchip[k], half(0, c)], k, (*peers[k], c)).wait_recv()
            sends.append(copy(o_refs[0].at[pchip[k], part(c, k)], 2 + k, (*peers[1 - k], c)))
            sends.append(copy(o_refs[0].at[pchip[k], half(0, c)], 4 + k, sib))
            sends[-2].start()
            sends[-1].start()
        for k in range(3):
            for a in range(1, n):
                land = o_refs[a].at[pchip[k], half(a, c)]
                copy(land, base[a] + k, (*peers[k], c)).wait_recv()
                sends.append(copy(land, base[a] + 3 + k, sib))
                sends[-1].start()
        for q in range(2):
            land = o_refs[0].at[pchip[2], part(c, q)]
            copy(land, 2 + q, (*peers[1 - q], c)).wait_recv()
            sends.append(copy(land, 6 + q, sib))
            sends[-1].start()
        for k in range(2):
            copy(o_refs[0].at[pchip[k], half(0, 1 - c)], 4 + k, sib).wait_recv()
            copy(o_refs[0].at[pchip[2], part(1 - c, k)], 6 + k, sib).wait_recv()
        for k in range(3):
            for a in range(1, n):
                copy(o_refs[a].at[pchip[k], half(a, 1 - c)], base[a] + 3 + k, sib).wait_recv()
        for cp in sends:
            cp.wait_send()

    n_sems = 8 + 6 * (n - 1)
    outs = _pcall(
        body, name="gather_weights", in_specs=[_ANY] * n, out_specs=[_ANY] * n,
        out_shape=[_sds((4,) + s.shape, s.dtype) for s in shards],
        scratch_shapes=[pltpu.SemaphoreType.DMA((n_sems,)), pltpu.SemaphoreType.DMA((n_sems,))],
    )(*shards)
    chip = 2 * lax.axis_index("x") + lax.axis_index("y")
    return [lax.dynamic_update_slice_in_dim(o, s[None], chip, axis=0) for o, s in zip(outs, shards)]


def _pair_partials_call(by_chip):
    n = len(by_chip)

    def body(*refs):
        v_refs, o_refs, (send_sems, recv_sems) = refs[:n], refs[n:2 * n], refs[2 * n:]
        x, y, c = _axes()
        cps = [_remote(v_refs[a].at[1 - c], o_refs[a], send_sems, recv_sems, a, (x, y, 1 - c)) for a in range(n)]
        for cp in cps:
            cp.start()
        for cp in cps:
            cp.wait()

    return _pcall(
        body, name="pair_partials", in_specs=[_ANY] * n, out_specs=[_ANY] * n,
        out_shape=[_sds(v.shape[1:], v.dtype) for v in by_chip],
        scratch_shapes=[pltpu.SemaphoreType.DMA((n,)), pltpu.SemaphoreType.DMA((n,))],
    )(*by_chip)


def _scatter_partials_call(parts):
    n = len(parts)

    def body(*refs):
        v_refs, o_refs, (send_sems, recv_sems) = refs[:n], refs[n:2 * n], refs[2 * n:]
        x, y, c = _axes()
        peers = _chip_peers(x, y)
        cps = [_remote(v_refs[a].at[2 * peers[k][0] + peers[k][1]], o_refs[a].at[k], send_sems, recv_sems, 3 * a + k,
                       (*peers[k], c)) for k in range(3) for a in range(n)]
        for cp in cps:
            cp.start()
        for cp in cps:
            cp.wait()

    return _pcall(
        body, name="scatter_partials", in_specs=[_ANY] * n, out_specs=[_ANY] * n,
        out_shape=[_sds((3,) + v.shape[1:], v.dtype) for v in parts],
        scratch_shapes=[pltpu.SemaphoreType.DMA((3 * n,)), pltpu.SemaphoreType.DMA((3 * n,))],
    )(*parts)


def _pair_result_call(fins):
    n = len(fins)

    def body(*refs):
        v_refs, o_refs, (send_sems, recv_sems) = refs[:n], refs[n:2 * n], refs[2 * n:]
        x, y, c = _axes()
        cps = [_remote(v_refs[a].at[c], o_refs[a].at[c], send_sems, recv_sems, a, (x, y, 1 - c)) for a in range(n)]
        for cp in cps:
            cp.start()
        for a in range(n):
            cps[a].wait_send()
            _remote(v_refs[a].at[c], o_refs[a].at[1 - c], send_sems, recv_sems, a, (x, y, 1 - c)).wait_recv()

    return _pcall(
        body, name="pair_result", in_specs=[_ANY] * n, out_specs=[_ANY] * n,
        out_shape=[_sds(v.shape, v.dtype) for v in fins], input_output_aliases={a: a for a in range(n)},
        scratch_shapes=[pltpu.SemaphoreType.DMA((n,)), pltpu.SemaphoreType.DMA((n,))],
    )(*fins)


def _allreduce_small_call(v):
    rows, cols = v.shape

    def body(v_ref, o_ref, buf, send_sems, recv_sems):
        x, y, c = _axes()
        chip, peers = 2 * x + y, _chip_peers(x, y)
        pair = _remote(v_ref, buf.at[0], send_sems, recv_sems, 0, (x, y, 1 - c))
        pair.start()
        pair.wait()
        buf[0] = v_ref[...] + buf[0]
        cps = [_remote(buf.at[0], buf.at[1 + k], send_sems, recv_sems, 1 + k, (*peers[k], c)) for k in range(3)]
        for cp in cps:
            cp.start()
        for cp in cps:
            cp.wait()
        acc = buf[chip]
        for d in range(1, 4):
            acc = acc + buf[lax.bitwise_xor(chip, d)]
        o_ref[...] = acc

    return _pcall(
        body, name="allreduce_small", in_specs=[_VMEM], out_specs=_VMEM, out_shape=_sds(v.shape, F32),
        scratch_shapes=[pltpu.VMEM((4, rows, cols), F32), pltpu.SemaphoreType.DMA((4,)), pltpu.SemaphoreType.DMA((4,))],
        compiler_params=_cparams(),
    )(v)


def _pack_rows(arrays, dtype, total_rows=None):
    parts = []
    for a in arrays:
        flat = a.astype(dtype).reshape(-1)
        parts.append(jnp.pad(flat, (0, -flat.shape[0] % D_MODEL)).reshape(-1, D_MODEL))
    out = jnp.concatenate(parts, axis=0)
    total_rows = total_rows or out.shape[0] + (-out.shape[0] % 8)
    return jnp.pad(out, ((0, total_rows - out.shape[0]), (0, 0)))


def _unpack_rows(packed, shapes):
    out, r = [], 0
    for s in shapes:
        n = math.prod(s)
        nr = -(-n // D_MODEL)
        out.append(packed[r:r + nr].reshape(-1)[:n].reshape(s))
        r += nr
    return out


def _to_r(w_t):
    pad = jnp.zeros((N_R - N_ORIG,) + w_t.shape[1:], w_t.dtype)
    return jnp.concatenate([w_t[A_ORIG_W:BA_ORIG], w_t[:A_ORIG_W], w_t[BA_ORIG + 2 * HEADS:],
                            w_t[BA_ORIG:BA_ORIG + 2 * HEADS], pad], axis=0)


_BIG = ("w_in", "a_proj", "b_proj", "c_proj", "w_out")
_SHARD_AXIS = {"w_in": 2, "a_proj": 2, "b_proj": 1, "c_proj": 2, "w_out": 1, "a_dw": 2, "b_conv": 2}
_BIG_AXIS = _SHARD_AXIS


def _join_chips(g, axis):
    g = jnp.moveaxis(g, 0, axis)
    return g.reshape(g.shape[:axis] + (4 * g.shape[axis + 1],) + g.shape[axis + 2:])


def _split_chips(a, axis):
    n = a.shape[axis] // 4
    return jnp.moveaxis(a.reshape(a.shape[:axis] + (4, n) + a.shape[axis + 1:]), axis, 1)
_ORDER = ("norm_g", "w_in", "a_dw", "a_dw_b", "a_ln_g", "a_ln_b", "a_proj", "b_conv", "b_a_log", "b_dt_bias",
          "b_onorm_g", "b_proj", "c_ln_g", "c_ln_b", "c_ws", "c_bs", "c_proj", "w_out", "final_g")


def _local_step(x, tgt, w):
    bsz, t, _ = x.shape
    m = bsz * t
    depth = w["norm_g"].shape[0]
    row = lambda v: v.reshape(1, -1)
    w_rt = _to_r(w["w_in_t"])
    saved, xl = [], x
    for l in range(depth):
        n = f"l{l}_"
        par = dict(
            adw=jnp.pad(w["a_dw"][l], ((0, 1), (0, 0))), adb=row(w["a_dw_b"][l]), alg=row(w["a_ln_g"][l]),
            alb=row(w["a_ln_b"][l]), bconv=w["b_conv"][l],
            alog=jnp.pad(row(w["b_a_log"][l]), ((0, 0), (HEADS, DK - 2 * HEADS))),
            dtb=jnp.pad(row(w["b_dt_bias"][l]), ((0, 0), (HEADS, DK - 2 * HEADS))), og=row(w["b_onorm_g"][l]),
            clg=row(w["c_ln_g"][l]), clb=row(w["c_ln_b"][l]), cws=w["c_ws"][l],
            cbias=jnp.repeat(w["c_bs"][l].T, SG_C, axis=1), ng=row(w["norm_g"][l]),
            wa=w["a_proj"][l], wb=w["b_proj"][l], wc=w["c_proj"][l], wo=w["w_out"][l], wrt=w_rt[:, l], wr=w_rt[:, l].T)
        h = _rms_fwd_call(xl.reshape(m, D_MODEL), par["ng"], n + "norm")
        proj = _matmul(h, par["wr"], F32, 2048, 1152, 1024, n + "in_proj").reshape(bsz, t, N_R)
        ya, conv = _a_fwd_call(proj, par["adw"], par["adb"], par["alg"], par["alb"], n + "conv_fwd")
        yb, *dn_saved = _dn_fwd_call(proj, par["bconv"], par["alog"], par["dtb"], par["og"], n + "delta_fwd")
        yc = _c_fwd_call(proj, par["clg"], par["clb"], par["cws"], par["cbias"], n + "gmlp_fwd")
        x_next = _m_fwd_call(ya, yb, yc, proj, xl, par["wa"], par["wb"], par["wc"], par["wo"], n + "merge_fwd")
        saved.append((par, xl, h, proj, ya, yb, yc, conv, dn_saved))
        xl = x_next
    dout, dfg, loss = _loss_call(xl.reshape(m, D_MODEL), tgt.reshape(m, D_MODEL), row(w["final_g"]))
    dout = dout.reshape(bsz, t, D_MODEL)
    g = {k: [None] * depth for k in _ORDER if k != "final_g"}
    for l in reversed(range(depth)):
        n = f"l{l}_"
        par, xl, h, proj, ya, yb, yc, conv, dn_saved = saved[l]
        dya, dyb, dyc, dgate, merged, dps = _m_bwd_call(dout, ya, yb, yc, proj, par["wa"], par["wb"], par["wc"],
                                                        par["wo"], n + "merge_bwd")
        g["a_proj"][l], g["b_proj"][l], g["c_proj"][l], g["w_out"][l] = _m_wgrad_call(ya, yb, yc, merged, dps, dout,
                                                                                      n + "merge_wgrad")
        da, dadw, dasm = _a_bwd_call(dya, conv, proj, par["adw"], par["alg"], par["alb"], n + "conv_bwd")
        db, dba, dbconv, dbsm = _dn_bwd_call(dyb, dn_saved, proj, par["bconv"], par["alog"], par["dtb"], par["og"],
                                             n + "delta_bwd")
        dc, dcws, dcbs, dcsm = _c_bwd_call(dyc, proj, par["clg"], par["clb"], par["cws"], par["cbias"], n + "gmlp_bwd")
        segs = [s.reshape(m, s.shape[-1]) for s in (db, da, dc, dgate, dba)]
        dh = _matmul_ksegs(segs, par["wrt"], 2048, 1024, 512, n + "in_proj_dx")
        ht = h.T
        dwb, dwa, dwc, dwg, dwba = [_matmul(ht, s, F32, 1024, 1024 if s.shape[1] % 1024 == 0 else 768, 2048,
                                            n + "in_proj_dw_" + tag) for s, tag in zip(segs, "bacgs")]
        g["w_in"][l] = jnp.concatenate([dwa, dwb, dwba[:, :2 * HEADS], dwc, dwg], axis=1)
        dx, dng = _rms_bwd_call(xl.reshape(m, D_MODEL), dh, par["ng"], dout.reshape(m, D_MODEL), n + "norm_bwd")
        dout = dx.reshape(bsz, t, D_MODEL)
        g["norm_g"][l] = dng[0]
        g["a_dw"][l], g["a_dw_b"][l], g["a_ln_g"][l], g["a_ln_b"][l] = dadw[:CONV_K], dasm[0], dasm[1], dasm[2]
        g["b_conv"][l], g["b_a_log"][l], g["b_dt_bias"][l] = dbconv[:SHORT_K], dbsm[0, :HEADS], dbsm[1, :HEADS]
        g["b_onorm_g"][l] = dbsm[2]
        g["c_ln_g"][l], g["c_ln_b"][l], g["c_ws"][l] = dcsm[0], dcsm[1], dcws
        g["c_bs"][l] = dcbs.reshape(SG_C, SG_G, SG_C).sum(-1).T
    grads = {k: jnp.stack(v) for k, v in g.items()}
    grads["final_g"] = dfg[0]
    return loss[0, 0], dout, grads


def kernel(x, norm_g, w_in, a_dw, a_dw_b, a_ln_g, a_ln_b, a_proj, b_conv, b_a_log, b_dt_bias, b_onorm_g, b_proj, c_ln_g, c_ln_b, c_ws, c_bs, c_proj, w_out, final_g, loss_target, m_norm_g, m_w_in, m_a_dw, m_a_dw_b, m_a_ln_g, m_a_ln_b, m_a_proj, m_b_conv, m_b_a_log, m_b_dt_bias, m_b_onorm_g, m_b_proj, m_c_ln_g, m_c_ln_b, m_c_ws, m_c_bs, m_c_proj, m_w_out, m_final_g, v_norm_g, v_w_in, v_a_dw, v_a_dw_b, v_a_ln_g, v_a_ln_b, v_a_proj, v_b_conv, v_b_a_log, v_b_dt_bias, v_b_onorm_g, v_b_proj, v_c_ln_g, v_c_ln_b, v_c_ws, v_c_bs, v_c_proj, v_w_out, v_final_g):
    given = dict(locals())
    ws = {k: given[k] for k in _ORDER}
    xi, yi, ci = _axes()
    chip = 2 * xi + yi
    where = jnp.stack([ci, chip]).astype(jnp.int32)

    others = _BIG[1:] + ("a_dw", "b_conv")
    gathered = _gather_weights_call([jnp.transpose(w_in, (2, 0, 1)).astype(BF16)] + [ws[k].astype(BF16) for k in _BIG[1:]]
                                    + [a_dw, b_conv])
    full = {k: v for k, v in ws.items() if k != "w_in"}
    full["w_in_t"] = gathered[0].reshape((N_ORIG,) + gathered[0].shape[2:])
    for k, g in zip(others, gathered[1:]):
        full[k] = _join_chips(g, _SHARD_AXIS[k])

    loss, grad_x, grads = _local_step(x, loss_target, full)

    by_chip = [_split_chips(grads[k], _BIG_AXIS[k]) for k in _BIG]
    theirs = _pair_partials_call(by_chip)
    sums = [_add_pair_call(b, r, where, "pair_sum_" + k) for k, b, r in zip(_BIG, by_chip, theirs)]
    got = _scatter_partials_call([s[0] for s in sums])
    fins = [_add_recv_call(s[1], r, where, "chip_sum_" + k) for k, s, r in zip(_BIG, sums, got)]
    out_g = dict(zip(_BIG, _pair_result_call(fins)))

    rest = [k for k in _ORDER if k not in _BIG]
    rest_shapes = [grads[k].shape for k in rest] + [(1,)]
    summed = _unpack_rows(_allreduce_small_call(_pack_rows([grads[k] for k in rest] + [loss.reshape(1)], F32)),
                          rest_shapes)
    out_g.update(zip(rest, summed[:-1]))
    out_g["a_dw"] = lax.dynamic_slice_in_dim(out_g["a_dw"], chip * a_dw.shape[2], a_dw.shape[2], axis=2)
    out_g["b_conv"] = lax.dynamic_slice_in_dim(out_g["b_conv"], chip * b_conv.shape[2], b_conv.shape[2], axis=2)

    def adam(k):
        operands = (ws[k], out_g[k], given["m_" + k], given["v_" + k])
        if k != "w_in":
            return _adamw_call(*operands, "adamw_" + k)
        w_t, g_t, m_t, v_t = (jnp.transpose(a, (2, 0, 1)) for a in operands)
        out_g[k] = jnp.transpose(g_t, (1, 2, 0))
        return tuple(jnp.transpose(o, (1, 2, 0)) for o in _adamw_call(w_t, g_t, m_t, v_t, "adamw_" + k))

    upd = {k: adam(k) for k in _ORDER}
    return (summed[-1][0], grad_x, *[out_g[k] for k in _ORDER], *[upd[k][0] for k in _ORDER],
            *[upd[k][1] for k in _ORDER], *[upd[k][2] for k in _ORDER])
```

```python
import functools
import math

import jax
import jax.numpy as jnp
from jax import lax
from jax.experimental import pallas as pl
from jax.experimental.pallas import tpu as pltpu

F32 = jnp.float32
BF16 = jnp.bfloat16
MESH = pl.DeviceIdType.MESH

D_MODEL = 1024
CONV_W = 512
CONV_K = 31
HEADS = 8
DK = 128
SHORT_K = 4
CHUNK = 64
SG_W = 512
SG_G = 4
SG_C = 128
EPS = 1e-6
N_ORIG = 10256
BA_ORIG = 5632
A_ORIG_W = 3 * CONV_W
B_W = 4 * D_MODEL
B_OFF, A_OFF, C_OFF, G_OFF, BA_OFF, N_R = 0, 4096, 5632, 7168, 10240, 10368
TB = 256
HALO_A = 32
HALO_B = 8
VMEM_LIMIT = 56 * 1024 * 1024
ADAM_LR, ADAM_B1, ADAM_B2, ADAM_EPS, ADAM_WD, ADAM_STEP = 0.001, 0.9, 0.999, 1e-08, 0.01, 10


def _pcall(body, **kw):
    return pl.pallas_call(body, **kw)


def _cparams(sem=None):
    kw = dict(vmem_limit_bytes=VMEM_LIMIT)
    if sem is not None:
        kw["dimension_semantics"] = sem
    return pltpu.CompilerParams(**kw)


def _sig(x):
    return jax.nn.sigmoid(x)


def _silu(x):
    return x * _sig(x)


def _dsilu(x):
    s = _sig(x)
    return s * (1.0 + x * (1.0 - s))


_GELU_C = math.sqrt(2.0 / math.pi)


def _gelu(x):
    return 0.5 * x * (1.0 + jnp.tanh(_GELU_C * (x + 0.044715 * x * x * x)))


def _dgelu(x):
    t = jnp.tanh(_GELU_C * (x + 0.044715 * x * x * x))
    return 0.5 * (1.0 + t) + 0.5 * x * (1.0 - t * t) * _GELU_C * (1.0 + 3 * 0.044715 * x * x)


def _softplus(x):
    return jnp.maximum(x, 0.0) + jnp.log1p(jnp.exp(-jnp.abs(x)))


def _dot(a, b, dims):
    return lax.dot_general(a.astype(BF16), b.astype(BF16), (dims, ((), ())), preferred_element_type=F32)


def _nn(a, b):
    return _dot(a, b, ((1,), (0,)))


def _nt(a, b):
    return _dot(a, b, ((1,), (1,)))


def _tn(a, b):
    return _dot(a, b, ((0,), (0,)))


def _tn_mxu(a, b):
    n = a.shape[1]
    eye = (lax.broadcasted_iota(jnp.int32, (n, n), 0) == lax.broadcasted_iota(jnp.int32, (n, n), 1)).astype(BF16)
    return _nn(_nt(eye, a), b)


def _mean(x):
    return jnp.mean(x, axis=-1, keepdims=True)


def _sum0(x):
    return jnp.sum(x, axis=0, keepdims=True)


def _sum1(x):
    return jnp.sum(x, axis=1, keepdims=True)


def _ln_fwd(x, g, b):
    xc = x - _mean(x)
    r = lax.rsqrt(_mean(xc * xc) + EPS)
    xh = xc * r
    return xh * g + b, xh, r


def _ln_bwd(dy, xh, r, g):
    dxh = dy * g
    return r * (dxh - _mean(dxh) - xh * _mean(dxh * xh)), _sum0(dy * xh), _sum0(dy)


def _rms_bwd(dxh, xh, r):
    return r * (dxh - xh * _mean(dxh * xh))


def _rows8(*rows):
    n = rows[0].shape[1]
    return jnp.concatenate(list(rows) + [jnp.zeros((8 - len(rows), n), F32)], axis=0)


def _windows(ext, tb):
    n = ext.shape[0] - 8
    shifted = {0: ext}

    def window(off):
        r = off % 8
        if r not in shifted:
            shifted[r] = ext[r:r + n]
        return shifted[r][off - r:off - r + tb]

    return window


def _a_fwd(val, glu, az, val_h, glu_h, w, bias, g, b):
    tb = val.shape[0]
    win = _windows(jnp.concatenate([val_h * _sig(glu_h), val * _sig(glu)], axis=0), tb)
    c = win(2) * w[0:1]
    for j in range(1, CONV_K):
        c = c + win(2 + j) * w[j:j + 1]
    c = c + bias
    ln, _, _ = _ln_fwd(c, g, b)
    return _silu(ln) * _silu(az), c


def _a_bwd(dy, val, glu, az, c, w, g, b, dc_next):
    tb = val.shape[0]
    ln, xh, r = _ln_fwd(c, g, b)
    sl, sz = _sig(ln), _sig(az)
    dln = dy * (az * sz) * (sl * (1.0 + ln * (1.0 - sl)))
    daz = dy * (ln * sl) * (sz * (1.0 + az * (1.0 - sz)))
    dc, dg, db = _ln_bwd(dln, xh, r, g)
    win = _windows(jnp.concatenate([dc, dc_next], axis=0), tb)
    sg = _sig(glu)
    a = val * sg
    da = win(30) * w[0:1]
    dw_rows = [_sum0(win(30) * a)]
    for j in range(1, CONV_K):
        shifted = win(30 - j)
        da = da + shifted * w[j:j + 1]
        dw_rows.append(_sum0(shifted * a))
    dw = jnp.concatenate(dw_rows + [jnp.zeros((1, CONV_W), F32)], axis=0)
    return da * sg, da * val * sg * (1.0 - sg), daz, dw, _sum0(dc), dg, db, dc[:HALO_A]


def _tril(ws):
    ii = lax.broadcasted_iota(jnp.int32, (SG_C, SG_C), 0)
    jj = lax.broadcasted_iota(jnp.int32, (SG_C, SG_C), 1)
    return [jnp.where(jj <= ii, ws[gi], 0.0) for gi in range(SG_G)], jj <= ii


def _c_mix(wt, vs, bias_full):
    tb = vs.shape[0]
    rows = []
    for n in range(tb // SG_C):
        blks = [_nn(wt[gi], vs[n * SG_C:(n + 1) * SG_C, gi * SG_C:(gi + 1) * SG_C]) for gi in range(SG_G)]
        rows.append(jnp.concatenate(blks, axis=1) + bias_full)
    return jnp.concatenate(rows, axis=0)


def _c_fwd(cu, cv, cz, g, b, ws, bias_full):
    wt, _ = _tril(ws)
    vs, xh, r = _ln_fwd(_gelu(cv), g, b)
    mixed = _c_mix(wt, vs, bias_full)
    return _gelu(cu) * mixed * _silu(cz), (wt, vs, xh, r, mixed)


def _c_bwd(dy, cu, cv, cz, g, b, ws, bias_full):
    tb = cu.shape[0]
    _, (wt, vs, xh, r, mixed) = _c_fwd(cu, cv, cz, g, b, ws, bias_full)
    _, low = _tril(ws)
    u, sz = _gelu(cu), _silu(cz)
    dcu = dy * mixed * sz * _dgelu(cu)
    dcz = dy * u * mixed * _dsilu(cz)
    dmixed = dy * u * sz
    dbs = jnp.zeros((SG_C, SG_W), F32)
    dws = [jnp.zeros((SG_C, SG_C), F32) for _ in range(SG_G)]
    rows = []
    for n in range(tb // SG_C):
        dm_n = dmixed[n * SG_C:(n + 1) * SG_C]
        dbs = dbs + dm_n
        blks = []
        for gi in range(SG_G):
            dm = dm_n[:, gi * SG_C:(gi + 1) * SG_C]
            dws[gi] = dws[gi] + _nt(dm, vs[n * SG_C:(n + 1) * SG_C, gi * SG_C:(gi + 1) * SG_C])
            blks.append(_tn(wt[gi], dm))
        rows.append(jnp.concatenate(blks, axis=1))
    dvs = jnp.concatenate(rows, axis=0)
    dgv, dg, db = _ln_bwd(dvs, xh, r, g)
    dws = [jnp.where(low, d, 0.0) for d in dws]
    return dcu, dgv * _dgelu(cv), dcz, dws, dbs, dg, db


def _m_fwd(ya, yb, yc, g0, g1, g2, wa, wb, wc):
    pa, pb, pc = _nn(ya, wa), _nn(yb, wb), _nn(yc, wc)
    s0, s1, s2 = _sig(g0), _sig(g1), _sig(g2)
    return s0 * pa + s1 * pb + s2 * pc, (pa, pb, pc, s0, s1, s2)


def _m_bwd(dout, ya, yb, yc, g0, g1, g2, wa, wb, wc, wo):
    merged, (pa, pb, pc, s0, s1, s2) = _m_fwd(ya, yb, yc, g0, g1, g2, wa, wb, wc)
    dm = _nt(dout, wo)
    dpa, dpb, dpc = dm * s0, dm * s1, dm * s2
    dgs = (dm * pa * s0 * (1.0 - s0), dm * pb * s1 * (1.0 - s1), dm * pc * s2 * (1.0 - s2))
    return (_nt(dpa, wa), _nt(dpb, wb), _nt(dpc, wc)), dgs, merged, (dpa, dpb, dpc)


def _chunk_masks(c):
    ii = lax.broadcasted_iota(jnp.int32, (c, c), 0)
    jj = lax.broadcasted_iota(jnp.int32, (c, c), 1)
    return ii, jj


def _dn_decay(items):
    ii, jj = _chunk_masks(CHUNK)
    incl, strict, eye = jj <= ii, jj < ii, ii == jj
    for d in items:
        g = d["g"]
        grow = _sum0(jnp.where(eye, g, 0.0))
        gc_col = _sum1(jnp.where(incl, grow, 0.0))
        gc_row = _sum0(jnp.where(ii <= jj, g, 0.0))
        gam_i = jnp.where(incl, jnp.exp(jnp.where(incl, gc_col - gc_row, 0.0)), 0.0)
        gl = _sum0(g)
        egc = jnp.exp(gc_col)
        d.update(gam_i=gam_i, gam_s=jnp.where(strict, gam_i, 0.0), egc=egc, ekd=jnp.exp(gl - gc_col), dl=jnp.exp(gl),
                 gdiff=gc_col - gc_row, qd=d["q"] * egc, rhs_w=d["k"] * (d["b"] * egc))
        d["kd"] = d["k"] * d["ekd"]
    return ii, jj, strict, eye


def _dn_solve(items, ii, jj, eye):
    off = ((ii >> 1) == (jj >> 1)) & ((ii & 1) != 0) & ((jj & 1) == 0)
    for d in items:
        a1 = jnp.where(off, d["a"], 0.0)
        d["t"] = jnp.where(eye, 1.0, 0.0) - a1
        d["m"] = d["a"] - _nn(a1, d["a"])
    b, sh = 2, 2
    while b < CHUNK:
        off = ((ii >> sh) == (jj >> sh)) & ((ii & b) != 0) & ((jj & b) == 0)
        for d in items:
            mo = jnp.where(off, d["m"], 0.0)
            if 2 * b < CHUNK:
                d["m"], d["t"] = d["m"] - _nn(mo, d["m"]), d["t"] - _nn(mo, d["t"])
            else:
                d["t"] = d["t"] - _nn(mo, d["t"])
        b, sh = 2 * b, sh + 1


def _dn_fwd_chunk(items, ss):
    ii, jj, strict, eye = _dn_decay(items)
    for d in items:
        d["a"] = d["b"] * _nt(d["k"], d["k"]) * d["gam_s"]
        d["qk"] = _nt(d["q"], d["k"]) * d["gam_i"]
    _dn_solve(items, ii, jj, eye)
    for d in items:
        d["u"], d["w"] = _nn(d["t"], d["v"] * d["b"]), _nn(d["t"], d["rhs_w"])
    ws = [_nn(d["w"], s) for d, s in zip(items, ss)]
    qs = [_nn(d["qd"], s) for d, s in zip(items, ss)]
    vnew = [d["u"] - w for d, w in zip(items, ws)]
    outs = [q + _nn(d["qk"], vn) for d, q, vn in zip(items, qs, vnew)]
    ss = [d["dl"] * s + _tn(d["kd"], vn) for d, s, vn in zip(items, ss, vnew)]
    return outs, ss, vnew


def _dn_bwd_chunk(items, ss, dss):
    ii, jj, strict, eye = _dn_decay(items)
    eye_b = eye.astype(BF16)
    for d in items:
        k, q = d["k"], d["q"]
        d["kk"] = _nt(k, k)
        d["a"] = d["b"] * d["kk"] * d["gam_s"]
        d["qk"] = _nt(q, k) * d["gam_i"]
        d["qkt"] = _nt(k, q) * jnp.where(ii <= jj, jnp.exp(jnp.where(ii <= jj, -d["gdiff"], 0.0)), 0.0)
    for d, s, ds2 in zip(items, ss, dss):
        d["dvnew"] = _nn(d["qkt"], d["do"]) + _nn(d["kd"], ds2)
        d["dqk"] = _nt(d["do"], d["vnew"])
        d["dqd"] = _nt(d["do"], s)
        d["dkd"] = _nt(d["vnew"], ds2)
    new_dss = []
    for d, s, ds2 in zip(items, ss, dss):
        d["dw"] = -_nt(d["dvnew"], s)
        new_dss.append(_tn_mxu(d["qd"], d["do"]) - _tn_mxu(d["w"], d["dvnew"]) + d["dl"] * ds2)
        d["ddl"] = _sum0(_sum1(s * ds2))
    for d in items:
        tt = _nt(eye_b, d["t"])
        d["drhs_u"], d["drhs_w"] = _nn(tt, d["dvnew"]), _nn(tt, d["dw"])
    for d in items:
        d["da"] = jnp.where(strict, -(_nt(d["drhs_u"], d["u"]) + _nt(d["drhs_w"], d["w"])), 0.0)
    outs = []
    for d in items:
        q, k, v, b, egc = d["q"], d["k"], d["v"], d["b"], d["egc"]
        drhs_u, drhs_w, da = d["drhs_u"], d["drhs_w"], d["da"]
        dbeta = _sum1(da * d["kk"] * d["gam_s"]) + _sum1(drhs_u * v) + _sum1(drhs_w * k) * egc
        dkk = da * b * d["gam_s"]
        e = da * d["a"] + d["dqk"] * d["qk"]
        s_kd = _sum1(d["dkd"] * d["kd"])
        dgc_col = _sum1(e) + _sum1(drhs_w * d["rhs_w"]) + _sum1(d["dqd"] * d["qd"]) - s_kd
        dgc_row = _sum0(jnp.where(eye, dgc_col, 0.0)) - _sum0(e)
        dg = _sum1(jnp.where(jj >= ii, dgc_row, 0.0)) + (_sum0(s_kd) + d["ddl"] * d["dl"])
        dqkg = d["dqk"] * d["gam_i"]
        dq = _nn(dqkg, k) + d["dqd"] * egc
        dk = _tn_mxu(dqkg, q) + _nn(dkk, k) + _tn_mxu(dkk, k) + drhs_w * (b * egc) + d["dkd"] * d["ekd"]
        outs.append((dq, dk, drhs_u * b, dbeta, dg))
    return outs, new_dss


def _short_conv(raw, halo, w):
    tb = raw.shape[0]
    ext = jnp.concatenate([halo, raw], axis=0)
    out = ext[5:5 + tb] * w[0:1]
    for j in range(1, SHORT_K):
        out = out + ext[5 + j:5 + j + tb] * w[j:j + 1]
    return out


def _dn_gates(ba, alog8, dtb8):
    xg = ba + dtb8
    ea8 = jnp.exp(alog8)
    return _sig(ba), -ea8 * _softplus(xg), xg, ea8


def _dn_item(pre_q, pre_k, pre_v, b, g):
    qc, kc, vc = (p * _sig(p) for p in (pre_q, pre_k, pre_v))
    nq = lax.rsqrt(_sum1(qc * qc) + EPS)
    nk = lax.rsqrt(_sum1(kc * kc) + EPS)
    return dict(q=qc * nq * (DK ** -0.5), k=kc * nk, v=vc, b=b, g=g, qc=qc, kc=kc, nq=nq, nk=nk)


def _dn_out(o, z, sz, og):
    r = lax.rsqrt(_mean(o * o) + EPS)
    xh = o * r
    return xh * og * (z * sz), xh, r


def _sds(shape, dtype):
    return jax.ShapeDtypeStruct(tuple(shape), dtype)


def _matmul(a, b, out_dtype, tm, tn, tk, name):
    m, kd = a.shape
    n = b.shape[1]
    tm, tn, tk = min(tm, m), min(tn, n), min(tk, kd)
    nk = kd // tk

    def body(a_ref, b_ref, o_ref, acc):
        @pl.when(pl.program_id(2) == 0)
        def _():
            acc[...] = jnp.zeros_like(acc)

        acc[...] += jnp.dot(a_ref[...], b_ref[...], preferred_element_type=F32)

        @pl.when(pl.program_id(2) == nk - 1)
        def _():
            o_ref[...] = acc[...].astype(o_ref.dtype)

    return _pcall(
        body, name=name, grid=(m // tm, n // tn, nk),
        in_specs=[pl.BlockSpec((tm, tk), lambda i, j, k: (i, k)), pl.BlockSpec((tk, tn), lambda i, j, k: (k, j))],
        out_specs=pl.BlockSpec((tm, tn), lambda i, j, k: (i, j)), out_shape=_sds((m, n), out_dtype),
        scratch_shapes=[pltpu.VMEM((tm, tn), F32)], compiler_params=_cparams(("parallel", "parallel", "arbitrary")),
    )(a, b)


def _matmul_ksegs(a_segs, b, tm, tn, tk, name):
    m, n = a_segs[0].shape[0], b.shape[1]
    tm, tn = min(tm, m), min(tn, n)
    main, tail = a_segs[:-1], a_segs[-1]
    steps = [s.shape[1] // tk for s in main]
    starts = [sum(steps[:i]) for i in range(len(main))]
    nk = sum(steps)
    wt = tail.shape[1]

    def body(*refs):
        a_refs, at_ref, b_ref, bt_ref, o_ref, acc = refs[:len(main)], *refs[len(main):]
        k = pl.program_id(2)

        @pl.when(k == 0)
        def _():
            acc[...] = jnp.zeros_like(acc)

        for a_ref, k0, ns in zip(a_refs, starts, steps):
            @pl.when((k >= k0) & (k < k0 + ns))
            def _():
                acc[...] += jnp.dot(a_ref[...], b_ref[...], preferred_element_type=F32)

        @pl.when(k == nk)
        def _():
            o_ref[...] = acc[...] + jnp.dot(at_ref[...], bt_ref[...], preferred_element_type=F32)

    seg_specs = [pl.BlockSpec((tm, tk), functools.partial(lambda i, j, k, k0, ns: (i, jnp.clip(k - k0, 0, ns - 1)), k0=k0, ns=ns))
                 for k0, ns in zip(starts, steps)]
    return _pcall(
        body, name=name, grid=(m // tm, n // tn, nk + 1),
        in_specs=seg_specs + [pl.BlockSpec((tm, wt), lambda i, j, k: (i, 0)),
                              pl.BlockSpec((tk, tn), lambda i, j, k: (jnp.minimum(k, nk - 1), j)),
                              pl.BlockSpec((wt, tn), lambda i, j, k: (nk * tk // wt, j))],
        out_specs=pl.BlockSpec((tm, tn), lambda i, j, k: (i, j)), out_shape=_sds((m, n), F32),
        scratch_shapes=[pltpu.VMEM((tm, tn), F32)], compiler_params=_cparams(("parallel", "parallel", "arbitrary")),
    )(*main, tail, b, b)


def _rms_fwd_call(x, g, name):
    m = x.shape[0]
    tm = min(512, m)

    def body(x_ref, g_ref, o_ref):
        xv = x_ref[...]
        o_ref[...] = (xv * lax.rsqrt(_mean(xv * xv) + EPS) * g_ref[...]).astype(BF16)

    return _pcall(
        body, name=name, grid=(m // tm,),
        in_specs=[pl.BlockSpec((tm, D_MODEL), lambda i: (i, 0)), pl.BlockSpec((1, D_MODEL), lambda i: (0, 0))],
        out_specs=pl.BlockSpec((tm, D_MODEL), lambda i: (i, 0)), out_shape=_sds((m, D_MODEL), BF16),
        compiler_params=_cparams(("parallel",)),
    )(x, g)


def _rms_bwd_call(x, dh, g, dres, name):
    m = x.shape[0]
    tm = min(512, m)

    def body(x_ref, dh_ref, g_ref, dr_ref, dx_ref, dg_ref):
        @pl.when(pl.program_id(0) == 0)
        def _():
            dg_ref[...] = jnp.zeros_like(dg_ref)

        xv, dhv = x_ref[...], dh_ref[...]
        r = lax.rsqrt(_mean(xv * xv) + EPS)
        xh = xv * r
        dx_ref[...] = _rms_bwd(dhv * g_ref[...], xh, r) + dr_ref[...]
        dg_ref[...] += _rows8(_sum0(dhv * xh))

    row = pl.BlockSpec((tm, D_MODEL), lambda i: (i, 0))
    return _pcall(
        body, name=name, grid=(m // tm,),
        in_specs=[row, row, pl.BlockSpec((1, D_MODEL), lambda i: (0, 0)), row],
        out_specs=[row, pl.BlockSpec((8, D_MODEL), lambda i: (0, 0))],
        out_shape=[_sds((m, D_MODEL), F32), _sds((8, D_MODEL), F32)], compiler_params=_cparams(("arbitrary",)),
    )(x, dh, g, dres)


def _loss_call(x, tgt, g):
    m = x.shape[0]
    tm = min(512, m)

    def body(x_ref, t_ref, g_ref, dx_ref, dg_ref, l_ref):
        @pl.when(pl.program_id(0) == 0)
        def _():
            dg_ref[...] = jnp.zeros_like(dg_ref)
            l_ref[...] = jnp.zeros_like(l_ref)

        xv = x_ref[...]
        r = lax.rsqrt(_mean(xv * xv) + EPS)
        xh = xv * r
        err = xh * g_ref[...] - t_ref[...]
        dy = err * (1.0 / D_MODEL)
        dx_ref[...] = _rms_bwd(dy * g_ref[...], xh, r)
        dg_ref[...] += _rows8(_sum0(dy * xh))
        l_ref[...] += 0.5 * _sum0(_mean(err * err))

    row = pl.BlockSpec((tm, D_MODEL), lambda i: (i, 0))
    return _pcall(
        body, name="loss_head", grid=(m // tm,),
        in_specs=[row, row, pl.BlockSpec((1, D_MODEL), lambda i: (0, 0))],
        out_specs=[row, pl.BlockSpec((8, D_MODEL), lambda i: (0, 0)), pl.BlockSpec((8, 128), lambda i: (0, 0))],
        out_shape=[_sds((m, D_MODEL), F32), _sds((8, D_MODEL), F32), _sds((8, 128), F32)],
        compiler_params=_cparams(("arbitrary",)),
    )(x, tgt, g)


def _halo_idx(i, rows):
    return jnp.maximum(i * (TB // rows) - 1, 0)


def _a_tiles(nt, rev):
    ti = (lambda i: nt - 1 - i) if rev else (lambda i: i)
    c0 = A_OFF // CONV_W
    return [pl.BlockSpec((1, TB, CONV_W), functools.partial(lambda b, i, c: (b, ti(i), c), c=c0 + c)) for c in range(3)]


def _a_fwd_call(proj, w, bias, g, b, name):
    bsz, t, _ = proj.shape
    nt = t // TB
    c0 = A_OFF // CONV_W

    def body(val_ref, glu_ref, az_ref, vh_ref, gh_ref, w_ref, bias_ref, g_ref, b_ref, y_ref, c_ref):
        keep = jnp.where(pl.program_id(1) > 0, 1.0, 0.0)
        y, c = _a_fwd(val_ref[0], glu_ref[0], az_ref[0], vh_ref[0] * keep, gh_ref[0], w_ref[...], bias_ref[...],
                      g_ref[...], b_ref[...])
        y_ref[0] = y.astype(BF16)
        c_ref[0] = c

    halo = [pl.BlockSpec((1, HALO_A, CONV_W), functools.partial(lambda b, i, c: (b, _halo_idx(i, HALO_A), c), c=c0 + c))
            for c in range(2)]
    par = [pl.BlockSpec((HALO_A, CONV_W), lambda b, i: (0, 0))] + [pl.BlockSpec((1, CONV_W), lambda b, i: (0, 0))] * 3
    tile = pl.BlockSpec((1, TB, CONV_W), lambda b, i: (b, i, 0))
    return _pcall(
        body, name=name, grid=(bsz, nt), in_specs=_a_tiles(nt, False) + halo + par, out_specs=[tile, tile],
        out_shape=[_sds((bsz, t, CONV_W), BF16), _sds((bsz, t, CONV_W), F32)],
        compiler_params=_cparams(("parallel", "parallel")),
    )(proj, proj, proj, proj, proj, w, bias, g, b)


def _a_bwd_call(dy, conv, proj, w, g, b, name, travel=()):
    bsz, t, _ = proj.shape
    nt = t // TB
    nv = len(travel)

    def body(*refs):
        (dy_ref, c_ref, val_ref, glu_ref, az_ref, w_ref, g_ref, b_ref), refs = refs[:8], refs[8:]
        v_refs, (da_ref, dw_ref, ds_ref), refs = refs[:nv], refs[nv:nv + 3], refs[nv + 3:]
        o_refs, carry, sems = refs[:nv], refs[nv], refs[nv + 1:]
        ip = pl.program_id(1)

        def exchange():
            x, y, c = _axes()
            peers = _chip_peers(x, y)
            return [_remote(v_refs[a].at[2 * peers[k][0] + peers[k][1]], o_refs[a].at[k], sems[0], sems[1], 3 * a + k,
                            (*peers[k], c)) for k in range(3) for a in range(nv)]

        @pl.when((pl.program_id(0) == 0) & (ip == 0))
        def _():
            dw_ref[...] = jnp.zeros_like(dw_ref)
            ds_ref[...] = jnp.zeros_like(ds_ref)
            if nv:
                for cp in exchange():
                    cp.start()

        @pl.when(ip == 0)
        def _():
            carry[...] = jnp.zeros_like(carry)

        dval, dglu, daz, dw, dbias, dg, db, head = _a_bwd(dy_ref[0], val_ref[0], glu_ref[0], az_ref[0], c_ref[0],
                                                          w_ref[...], g_ref[...], b_ref[...], carry[...])
        carry[...] = head
        for n, dcol in enumerate((dval, dglu, daz)):
            da_ref[0, :, n * CONV_W:(n + 1) * CONV_W] = dcol.astype(BF16)
        dw_ref[...] += dw
        ds_ref[...] += _rows8(dbias, dg, db)

        if nv:
            @pl.when((pl.program_id(0) == bsz - 1) & (ip == nt - 1))
            def _():
                for cp in exchange():
                    cp.wait()

    rtile = lambda b, i: (b, nt - 1 - i, 0)
    par = [pl.BlockSpec((HALO_A, CONV_W), lambda b, i: (0, 0))] + [pl.BlockSpec((1, CONV_W), lambda b, i: (0, 0))] * 2
    return _pcall(
        body, name=name, grid=(bsz, nt),
        in_specs=[pl.BlockSpec((1, TB, CONV_W), rtile)] * 2 + _a_tiles(nt, True) + par + [_ANY] * nv,
        out_specs=[pl.BlockSpec((1, TB, 3 * CONV_W), rtile), pl.BlockSpec((HALO_A, CONV_W), lambda b, i: (0, 0)),
                   pl.BlockSpec((8, CONV_W), lambda b, i: (0, 0))] + [_ANY] * nv,
        out_shape=[_sds((bsz, t, 3 * CONV_W), BF16), _sds((HALO_A, CONV_W), F32), _sds((8, CONV_W), F32)]
        + [_sds((3,) + v.shape[1:], v.dtype) for v in travel],
        scratch_shapes=[pltpu.VMEM((HALO_A, CONV_W), F32)] + [pltpu.SemaphoreType.DMA((3 * nv,))] * (2 if nv else 0),
        compiler_params=_cparams(("arbitrary", "arbitrary")),
    )(dy, conv, proj, proj, proj, w, g, b, *travel)


def _c_specs():
    c0 = C_OFF // SG_W
    tile = [pl.BlockSpec((1, TB, SG_W), functools.partial(lambda b, i, c: (b, i, c), c=c0 + c)) for c in range(3)]
    par = [pl.BlockSpec((1, SG_W), lambda b, i: (0, 0))] * 2 + [
        pl.BlockSpec((SG_G, SG_C, SG_C), lambda b, i: (0, 0, 0)), pl.BlockSpec((SG_C, SG_W), lambda b, i: (0, 0))]
    return tile + par


def _c_fwd_call(proj, g, b, ws, bias_full, name):
    bsz, t, _ = proj.shape

    def body(cu_ref, cv_ref, cz_ref, g_ref, b_ref, ws_ref, bf_ref, y_ref):
        y, _ = _c_fwd(cu_ref[0], cv_ref[0], cz_ref[0], g_ref[...], b_ref[...], ws_ref[...], bf_ref[...])
        y_ref[0] = y.astype(BF16)

    return _pcall(
        body, name=name, grid=(bsz, t // TB), in_specs=_c_specs(),
        out_specs=pl.BlockSpec((1, TB, SG_W), lambda b, i: (b, i, 0)), out_shape=_sds((bsz, t, SG_W), BF16),
        compiler_params=_cparams(("parallel", "parallel")),
    )(proj, proj, proj, g, b, ws, bias_full)


def _c_bwd_call(dy, proj, g, b, ws, bias_full, name):
    bsz, t, _ = proj.shape

    def body(dy_ref, cu_ref, cv_ref, cz_ref, g_ref, b_ref, ws_ref, bf_ref, dc_ref, dws_ref, dbs_ref, ds_ref):
        @pl.when((pl.program_id(0) == 0) & (pl.program_id(1) == 0))
        def _():
            dws_ref[...] = jnp.zeros_like(dws_ref)
            dbs_ref[...] = jnp.zeros_like(dbs_ref)
            ds_ref[...] = jnp.zeros_like(ds_ref)

        dcu, dcv, dcz, dws, dbs, dg, db = _c_bwd(dy_ref[0], cu_ref[0], cv_ref[0], cz_ref[0], g_ref[...], b_ref[...],
                                                 ws_ref[...], bf_ref[...])
        for n, dcol in enumerate((dcu, dcv, dcz)):
            dc_ref[0, :, n * SG_W:(n + 1) * SG_W] = dcol.astype(BF16)
        for gi in range(SG_G):
            dws_ref[gi] += dws[gi]
        dbs_ref[...] += dbs
        ds_ref[...] += _rows8(dg, db)

    tile = lambda b, i: (b, i, 0)
    return _pcall(
        body, name=name, grid=(bsz, t // TB), in_specs=[pl.BlockSpec((1, TB, SG_W), tile)] + _c_specs(),
        out_specs=[pl.BlockSpec((1, TB, 3 * SG_W), tile), pl.BlockSpec((SG_G, SG_C, SG_C), lambda b, i: (0, 0, 0)),
                   pl.BlockSpec((SG_C, SG_W), lambda b, i: (0, 0)), pl.BlockSpec((8, SG_W), lambda b, i: (0, 0))],
        out_shape=[_sds((bsz, t, 3 * SG_W), BF16), _sds((SG_G, SG_C, SG_C), F32), _sds((SG_C, SG_W), F32),
                   _sds((8, SG_W), F32)],
        compiler_params=_cparams(("arbitrary", "arbitrary")),
    )(dy, proj, proj, proj, g, b, ws, bias_full)


def _m_specs():
    g0 = G_OFF // D_MODEL
    y = [pl.BlockSpec((1, TB, n), lambda b, i: (b, i, 0)) for n in (CONV_W, D_MODEL, SG_W)]
    gates = [pl.BlockSpec((1, TB, D_MODEL), functools.partial(lambda b, i, c: (b, i, c), c=g0 + c)) for c in range(3)]
    return y + gates


def _w_specs(*shapes):
    return [pl.BlockSpec(s, lambda b, i: (0, 0)) for s in shapes]


def _m_fwd_call(ya, yb, yc, proj, x, wa, wb, wc, wo, name):
    bsz, t, _ = x.shape

    def body(ya_ref, yb_ref, yc_ref, g0_ref, g1_ref, g2_ref, x_ref, wa_ref, wb_ref, wc_ref, wo_ref, o_ref):
        merged, _ = _m_fwd(ya_ref[0], yb_ref[0], yc_ref[0], g0_ref[0], g1_ref[0], g2_ref[0], wa_ref[...], wb_ref[...],
                           wc_ref[...])
        o_ref[0] = x_ref[0] + _nn(merged, wo_ref[...])

    tile = pl.BlockSpec((1, TB, D_MODEL), lambda b, i: (b, i, 0))
    return _pcall(
        body, name=name, grid=(bsz, t // TB),
        in_specs=_m_specs() + [tile] + _w_specs(wa.shape, wb.shape, wc.shape, wo.shape),
        out_specs=tile, out_shape=_sds(x.shape, F32), compiler_params=_cparams(("parallel", "parallel")),
    )(ya, yb, yc, proj, proj, proj, x, wa, wb, wc, wo)


def _m_bwd_call(dout, ya, yb, yc, proj, wa, wb, wc, wo, name):
    bsz, t, _ = dout.shape

    def body(do_ref, ya_ref, yb_ref, yc_ref, g0_ref, g1_ref, g2_ref, wa_ref, wb_ref, wc_ref, wo_ref, dya_ref, dyb_ref,
             dyc_ref, dg_ref, mg_ref, dp_ref):
        dys, dgs, merged, dps = _m_bwd(do_ref[0], ya_ref[0], yb_ref[0], yc_ref[0], g0_ref[0], g1_ref[0], g2_ref[0],
                                       wa_ref[...], wb_ref[...], wc_ref[...], wo_ref[...])
        dya_ref[0], dyb_ref[0], dyc_ref[0] = dys
        mg_ref[0] = merged.astype(BF16)
        for n in range(3):
            dg_ref[0, :, n * D_MODEL:(n + 1) * D_MODEL] = dgs[n].astype(BF16)
            dp_ref[0, :, n * D_MODEL:(n + 1) * D_MODEL] = dps[n].astype(BF16)

    tile = lambda n: pl.BlockSpec((1, TB, n), lambda b, i: (b, i, 0))
    widths = (CONV_W, D_MODEL, SG_W, 3 * D_MODEL, D_MODEL, 3 * D_MODEL)
    dts = (F32, F32, F32, BF16, BF16, BF16)
    return _pcall(
        body, name=name, grid=(bsz, t // TB),
        in_specs=[tile(D_MODEL)] + _m_specs() + _w_specs(wa.shape, wb.shape, wc.shape, wo.shape),
        out_specs=[tile(n) for n in widths], out_shape=[_sds((bsz, t, n), d) for n, d in zip(widths, dts)],
        compiler_params=_cparams(("parallel", "parallel")),
    )(dout, ya, yb, yc, proj, proj, proj, wa, wb, wc, wo)


def _m_wgrad_call(ya, yb, yc, merged, dps, dout, name):
    bsz, t, _ = dout.shape

    def body(ya_ref, yb_ref, yc_ref, mg_ref, dp_ref, do_ref, dwa_ref, dwb_ref, dwc_ref, dwo_ref):
        @pl.when((pl.program_id(0) == 0) & (pl.program_id(1) == 0))
        def _():
            for r in (dwa_ref, dwb_ref, dwc_ref, dwo_ref):
                r[...] = jnp.zeros_like(r)

        dp = dp_ref[0]
        dwa_ref[...] += _tn(ya_ref[0], dp[:, :D_MODEL])
        dwb_ref[...] += _tn(yb_ref[0], dp[:, D_MODEL:2 * D_MODEL])
        dwc_ref[...] += _tn(yc_ref[0], dp[:, 2 * D_MODEL:])
        dwo_ref[...] += _tn(mg_ref[0], do_ref[0])

    tile = lambda n: pl.BlockSpec((1, TB, n), lambda b, i: (b, i, 0))
    shapes = ((CONV_W, D_MODEL), (D_MODEL, D_MODEL), (SG_W, D_MODEL), (D_MODEL, D_MODEL))
    return _pcall(
        body, name=name, grid=(bsz, t // TB),
        in_specs=[tile(CONV_W), tile(D_MODEL), tile(SG_W), tile(D_MODEL), tile(3 * D_MODEL), tile(D_MODEL)],
        out_specs=_w_specs(*shapes), out_shape=[_sds(s, F32) for s in shapes],
        compiler_params=_cparams(("arbitrary", "arbitrary")),
    )(ya, yb, yc, merged, dps, dout)


def _dn_specs(nc, rev):
    ti = (lambda i: nc - 1 - i) if rev else (lambda i: i)
    return [pl.BlockSpec((1, CHUNK, B_W), lambda b, i: (b, ti(i), 0)),
            pl.BlockSpec((1, CHUNK, DK), lambda b, i: (b, ti(i), BA_OFF // DK)),
            pl.BlockSpec((SHORT_K, 3 * D_MODEL), lambda b, i: (0, 0))] + [pl.BlockSpec((1, DK), lambda b, i: (0, 0))] * 3


def _dn_saved_specs(nc, rev):
    ti = (lambda i: nc - 1 - i) if rev else (lambda i: i)
    return [pl.BlockSpec((1, CHUNK, D_MODEL), lambda b, i: (b, ti(i), 0)),
            pl.BlockSpec((1, CHUNK, 3 * D_MODEL), lambda b, i: (b, ti(i), 0)),
            pl.BlockSpec((1, HEADS, 1, DK, DK), lambda b, i: (b, 0, ti(i), 0, 0)),
            pl.BlockSpec((1, 1, CHUNK, HEADS * CHUNK), lambda b, i: (b, ti(i), 0, 0)),
            pl.BlockSpec((1, CHUNK, 2 * D_MODEL), lambda b, i: (b, ti(i), 0)),
            pl.BlockSpec((1, CHUNK, D_MODEL), lambda b, i: (b, ti(i), 0))]


def _dn_fwd_call(proj, wconv, alog, dtb, og, name):
    bsz, t, _ = proj.shape
    nc = t // CHUNK

    def body(x_ref, ba_ref, w_ref, al_ref, dt_ref, og_ref, halo_ref, y_ref, o_ref, pre_ref, st_ref, t_ref, uw_ref, vn_ref,
             s_scr):
        i = pl.program_id(1)

        @pl.when(i == 0)
        def _():
            s_scr[...] = jnp.zeros_like(s_scr)

        keep = jnp.where(i > 0, 1.0, 0.0)
        bsig, gfull, _, _ = _dn_gates(ba_ref[0], al_ref[...], dt_ref[...])
        items = []
        for h in range(HEADS):
            pres = []
            for n in range(3):
                cs = slice(n * D_MODEL + h * DK, n * D_MODEL + (h + 1) * DK)
                pres.append(_short_conv(x_ref[0, :, cs], halo_ref[0, :, cs] * keep, w_ref[:, cs]))
                pre_ref[0, :, cs] = pres[-1]
            items.append(_dn_item(*pres, bsig[:, h:h + 1], gfull[:, HEADS + h:HEADS + h + 1]))
        ss = [s_scr[h] for h in range(HEADS)]
        for h in range(HEADS):
            st_ref[0, h, 0] = ss[h]
        outs, ss, vnew = _dn_fwd_chunk(items, ss)
        og_v = og_ref[...]
        for h, d in enumerate(items):
            s_scr[h] = ss[h]
            hs = slice(h * DK, (h + 1) * DK)
            z = x_ref[0, :, 3 * D_MODEL + h * DK:3 * D_MODEL + (h + 1) * DK]
            o_ref[0, :, hs] = outs[h]
            y_ref[0, :, hs] = _dn_out(outs[h], z, _sig(z), og_v)[0].astype(BF16)
            t_ref[0, 0, :, h * CHUNK:(h + 1) * CHUNK] = d["t"].astype(BF16)
            uw_ref[0, :, 2 * h * DK:(2 * h + 1) * DK] = d["u"].astype(BF16)
            uw_ref[0, :, (2 * h + 1) * DK:2 * (h + 1) * DK] = d["w"].astype(BF16)
            vn_ref[0, :, hs] = vnew[h].astype(BF16)

    halo = pl.BlockSpec((1, HALO_B, 3 * D_MODEL), lambda b, i: (b, jnp.maximum(i * (CHUNK // HALO_B) - 1, 0), 0))
    return _pcall(
        body, name=name, grid=(bsz, nc), in_specs=_dn_specs(nc, False) + [halo],
        out_specs=[pl.BlockSpec((1, CHUNK, D_MODEL), lambda b, i: (b, i, 0))] + _dn_saved_specs(nc, False),
        out_shape=[_sds((bsz, t, D_MODEL), BF16), _sds((bsz, t, D_MODEL), F32), _sds((bsz, t, 3 * D_MODEL), F32),
                   _sds((bsz, HEADS, nc, DK, DK), F32), _sds((bsz, nc, CHUNK, HEADS * CHUNK), BF16),
                   _sds((bsz, t, 2 * D_MODEL), BF16), _sds((bsz, t, D_MODEL), BF16)],
        scratch_shapes=[pltpu.VMEM((HEADS, DK, DK), F32)], compiler_params=_cparams(("arbitrary", "arbitrary")),
    )(proj, proj, wconv, alog, dtb, og, proj)


def _dn_bwd_call(dy, saved, proj, wconv, alog, dtb, og, name):
    bsz, t, _ = proj.shape
    nc = t // CHUNK

    def body(dy_ref, o_ref, pre_ref, st_ref, t_ref, uw_ref, vn_ref, x_ref, ba_ref, w_ref, al_ref, dt_ref, og_ref,
             dx_ref, dba_ref, dwc_ref, dsm_ref, ds_scr, dpre_scr):
        ip = pl.program_id(1)

        @pl.when((pl.program_id(0) == 0) & (ip == 0))
        def _():
            dwc_ref[...] = jnp.zeros_like(dwc_ref)
            dsm_ref[...] = jnp.zeros_like(dsm_ref)

        @pl.when(ip == 0)
        def _():
            ds_scr[...] = jnp.zeros_like(ds_scr)
            dpre_scr[...] = jnp.zeros_like(dpre_scr)

        bsig, gfull, xg, ea8 = _dn_gates(ba_ref[0], al_ref[...], dt_ref[...])
        og_v = og_ref[...]
        items, dog = [], jnp.zeros((1, DK), F32)
        for h in range(HEADS):
            hs = slice(h * DK, (h + 1) * DK)
            zs = slice(3 * D_MODEL + h * DK, 3 * D_MODEL + (h + 1) * DK)
            d = _dn_item(*(pre_ref[0, :, n * D_MODEL + h * DK:n * D_MODEL + (h + 1) * DK] for n in range(3)),
                         bsig[:, h:h + 1], gfull[:, HEADS + h:HEADS + h + 1])
            z, dyv = x_ref[0, :, zs], dy_ref[0, :, hs]
            sz = _sig(z)
            _, xh, r = _dn_out(o_ref[0, :, hs], z, sz, og_v)
            dx_ref[0, :, zs] = (dyv * xh * og_v * (sz * (1.0 + z * (1.0 - sz)))).astype(BF16)
            don = dyv * (z * sz)
            dog = dog + _sum0(don * xh)
            d.update(do=_rms_bwd(don * og_v, xh, r), t=t_ref[0, 0, :, h * CHUNK:(h + 1) * CHUNK],
                     u=uw_ref[0, :, 2 * h * DK:(2 * h + 1) * DK], w=uw_ref[0, :, (2 * h + 1) * DK:2 * (h + 1) * DK],
                     vnew=vn_ref[0, :, hs])
            items.append(d)
        grads, dss = _dn_bwd_chunk(items, [st_ref[0, h, 0] for h in range(HEADS)], [ds_scr[h] for h in range(HEADS)])
        lanes = lax.broadcasted_iota(jnp.int32, (CHUNK, DK), 1)
        lane8 = lax.broadcasted_iota(jnp.int32, (8, DK), 1)
        row8 = lax.broadcasted_iota(jnp.int32, (8, DK), 0)
        dgdx = -ea8 * _sig(xg)
        dba = jnp.zeros((CHUNK, DK), F32)
        dsm = jnp.where(row8 == 2, dog, 0.0)
        for h, (d, (dq, dk, dv, dbeta, dg)) in enumerate(zip(items, grads)):
            ds_scr[h] = dss[h]
            qc, kc, nq, nk = d["qc"], d["kc"], d["nq"], d["nk"]
            dacts = ((DK ** -0.5) * nq * (dq - qc * (nq * nq * _sum1(dq * qc))),
                     nk * (dk - kc * (nk * nk * _sum1(dk * kc))), dv)
            dal = dg * dgdx[:, HEADS + h:HEADS + h + 1]
            dba = dba + jnp.where(lanes == h, dbeta * d["b"] * (1.0 - d["b"]), 0.0) + jnp.where(lanes == HEADS + h, dal, 0.0)
            dsm = dsm + (jnp.where((row8 == 0) & (lane8 == h), _sum0(dg * d["g"]), 0.0)
                         + jnp.where((row8 == 1) & (lane8 == h), _sum0(dal), 0.0))
            for n in range(3):
                cs = slice(n * D_MODEL + h * DK, n * D_MODEL + (h + 1) * DK)
                pre, raw, w = pre_ref[0, :, cs], x_ref[0, :, cs], w_ref[:, cs]
                sp = _sig(pre)
                dpre = dacts[n] * (sp * (1.0 + pre * (1.0 - sp)))
                dext = jnp.concatenate([dpre, dpre_scr[:, cs]], axis=0)
                draw = dext[3:3 + CHUNK] * w[0:1]
                rows = [_sum0(dext[3:3 + CHUNK] * raw)]
                for j in range(1, SHORT_K):
                    shifted = dext[3 - j:3 - j + CHUNK]
                    draw = draw + shifted * w[j:j + 1]
                    rows.append(_sum0(shifted * raw))
                dpre_scr[:, cs] = dpre[:HALO_B]
                dx_ref[0, :, cs] = draw.astype(BF16)
                dwc_ref[:, cs] += _rows8(*rows)
        dba_ref[0] = dba.astype(BF16)
        dsm_ref[...] += dsm

    rrow = lambda n: pl.BlockSpec((1, CHUNK, n), lambda b, i: (b, nc - 1 - i, 0))
    return _pcall(
        body, name=name, grid=(bsz, nc), in_specs=[rrow(D_MODEL)] + _dn_saved_specs(nc, True) + _dn_specs(nc, True),
        out_specs=[rrow(B_W), rrow(DK), pl.BlockSpec((8, 3 * D_MODEL), lambda b, i: (0, 0)),
                   pl.BlockSpec((8, DK), lambda b, i: (0, 0))],
        out_shape=[_sds((bsz, t, B_W), BF16), _sds((bsz, t, DK), BF16), _sds((8, 3 * D_MODEL), F32), _sds((8, DK), F32)],
        scratch_shapes=[pltpu.VMEM((HEADS, DK, DK), F32), pltpu.VMEM((HALO_B, 3 * D_MODEL), F32)],
        compiler_params=_cparams(("arbitrary", "arbitrary")),
    )(dy, *saved, proj, proj, wconv, alog, dtb, og)


def _adamw_call(w, g, m, v, name):
    shape = w.shape
    view = (math.prod(shape[:-2]),) + shape[-2:] if len(shape) >= 2 else (1, 1) + shape
    lead, rows, cols = view
    tr, tl = (128, 1) if rows % 128 == 0 else (rows, min(lead, 64))
    c1, c2 = 1.0 - ADAM_B1 ** ADAM_STEP, 1.0 - ADAM_B2 ** ADAM_STEP

    def body(w_ref, g_ref, m_ref, v_ref, d_ref, nm_ref, nv_ref):
        gv = g_ref[...]
        nm = ADAM_B1 * m_ref[...] + (1.0 - ADAM_B1) * gv
        nv = ADAM_B2 * v_ref[...] + (1.0 - ADAM_B2) * (gv * gv)
        d_ref[...] = -ADAM_LR * ((nm / c1) / (jnp.sqrt(nv / c2) + ADAM_EPS) + ADAM_WD * w_ref[...])
        nm_ref[...] = nm
        nv_ref[...] = nv

    blk = pl.BlockSpec((tl, tr, cols), lambda l, i: (l, i, 0))
    outs = _pcall(
        body, name=name, grid=(pl.cdiv(lead, tl), rows // tr), in_specs=[blk] * 4, out_specs=[blk] * 3,
        out_shape=[_sds(view, F32)] * 3, compiler_params=_cparams(("parallel", "parallel")),
    )(*(a.reshape(view) for a in (w, g, m, v)))
    return tuple(o.reshape(shape) for o in outs)


def _row_tile(rows, cap=512):
    for tr in range(cap, 15, -16):
        if rows % tr == 0:
            return tr
    return rows


def _add_pair_call(by_chip, recv, where, name):
    _, n, rows, cols = by_chip.shape
    tr = _row_tile(rows, 256)

    def body(where_ref, a_ref, b_ref, sb_ref, own_ref):
        s = a_ref[0, 0] + b_ref[0]
        sb_ref[0] = s.astype(BF16)

        @pl.when(pl.program_id(1) == where_ref[1])
        def _():
            own_ref[...] = s

    grid_spec = pltpu.PrefetchScalarGridSpec(
        num_scalar_prefetch=1, grid=(rows // tr, n),
        in_specs=[pl.BlockSpec((1, 1, tr, cols), lambda i, j, wr: (wr[0], j, i, 0)),
                  pl.BlockSpec((1, tr, cols), lambda i, j, wr: (j, i, 0))],
        out_specs=[pl.BlockSpec((1, tr, cols), lambda i, j, wr: (j, i, 0)),
                   pl.BlockSpec((tr, cols), lambda i, j, wr: (i, 0))])
    return _pcall(
        body, name=name, grid_spec=grid_spec, out_shape=[_sds((n, rows, cols), BF16), _sds((rows, cols), F32)],
        compiler_params=_cparams(("parallel", "arbitrary")),
    )(where, by_chip, recv)


def _add_recv_call(own, recv, where, name):
    rows, cols = own.shape
    tr = _row_tile(rows, 256)

    def body(where_ref, o_ref, r_ref, s_ref):
        s_ref[0] = ((o_ref[...] + r_ref[0].astype(F32)) + r_ref[1].astype(F32)) + r_ref[2].astype(F32)

    grid_spec = pltpu.PrefetchScalarGridSpec(
        num_scalar_prefetch=1, grid=(rows // tr,),
        in_specs=[pl.BlockSpec((tr, cols), lambda i, wr: (i, 0)), pl.BlockSpec((3, tr, cols), lambda i, wr: (0, i, 0))],
        out_specs=pl.BlockSpec((1, tr, cols), lambda i, wr: (wr[0], i, 0)))
    return _pcall(
        body, name=name, grid_spec=grid_spec, out_shape=_sds((2, rows, cols), F32),
        compiler_params=_cparams(("parallel",)),
    )(where, own, recv)


def _axes():
    return lax.axis_index("x"), lax.axis_index("y"), lax.axis_index("c")


def _chip_peers(x, y):
    return [(x, 1 - y), (1 - x, y), (1 - x, 1 - y)]


_ANY = pl.BlockSpec(memory_space=pl.ANY)
_VMEM = pl.BlockSpec(memory_space=pltpu.VMEM)


def _remote(src, dst, send_sems, recv_sems, k, dev):
    return pltpu.make_async_remote_copy(src_ref=src, dst_ref=dst, send_sem=send_sems.at[k], recv_sem=recv_sems.at[k],
                                        device_id=dev, device_id_type=MESH)


def _gather_weights_call(shards):
    n = len(shards)
    halves = [s.shape[0] // 2 for s in shards]
    quarter = halves[0] // 2
    base = [0] + [8 + 6 * (a - 1) for a in range(1, n)]

    def body(*refs):
        x_refs, o_refs, (send_sems, recv_sems) = refs[:n], refs[n:2 * n], refs[2 * n:]
        x, y, c = _axes()
        chip, sib, peers = 2 * x + y, (x, y, 1 - c), _chip_peers(x, y)
        pchip = [2 * px + py for px, py in peers]

        def half(a, cc):
            return pl.ds(cc * halves[a], halves[a])

        def part(cc, q):
            return pl.ds(cc * halves[0] + q * quarter, quarter)

        def copy(ref, k, dev):
            return _remote(ref, ref, send_sems, recv_sems, k, dev)

        sends = [_remote(x_refs[a].at[half(a, c)], o_refs[a].at[chip, half(a, c)], send_sems, recv_sems, base[a] + k,
                         (*peers[k], c)) for k in range(3) for a in range(n) if a > 0 or k < 2]
        for cp in sends:
            cp.start()
        for k in range(2):
            copy(o_refs[0].at[pchip[k], half(0, c)], k, (*peers[k], c)).wait_recv()
            sends.append(copy(o_refs[0].at[pchip[k], part(c, k)], 2 + k, (*peers[1 - k], c)))
            sends.append(copy(o_refs[0].at[pchip[k], half(0, c)], 4 + k, sib))
            sends[-2].start()
            sends[-1].start()
        for k in range(3):
            for a in range(1, n):
                land = o_refs[a].at[pchip[k], half(a, c)]
                copy(land, base[a] + k, (*peers[k], c)).wait_recv()
                sends.append(copy(land, base[a] + 3 + k, sib))
                sends[-1].start()
        for q in range(2):
            land = o_refs[0].at[pchip[2], part(c, q)]
            copy(land, 2 + q, (*peers[1 - q], c)).wait_recv()
            sends.append(copy(land, 6 + q, sib))
            sends[-1].start()
        for k in range(2):
            copy(o_refs[0].at[pchip[k], half(0, 1 - c)], 4 + k, sib).wait_recv()
            copy(o_refs[0].at[pchip[2], part(1 - c, k)], 6 + k, sib).wait_recv()
        for k in range(3):
            for a in range(1, n):
                copy(o_refs[a].at[pchip[k], half(a, 1 - c)], base[a] + 3 + k, sib).wait_recv()
        for cp in sends:
            cp.wait_send()

    n_sems = 8 + 6 * (n - 1)
    outs = _pcall(
        body, name="gather_weights", in_specs=[_ANY] * n, out_specs=[_ANY] * n,
        out_shape=[_sds((4,) + s.shape, s.dtype) for s in shards],
        scratch_shapes=[pltpu.SemaphoreType.DMA((n_sems,)), pltpu.SemaphoreType.DMA((n_sems,))],
    )(*shards)
    chip = 2 * lax.axis_index("x") + lax.axis_index("y")
    return [lax.dynamic_update_slice_in_dim(o, s[None], chip, axis=0) for o, s in zip(outs, shards)]


def _pair_partials_call(by_chip, name):
    n = len(by_chip)

    def body(*refs):
        v_refs, o_refs, (send_sems, recv_sems) = refs[:n], refs[n:2 * n], refs[2 * n:]
        x, y, c = _axes()
        cps = [_remote(v_refs[a].at[1 - c], o_refs[a], send_sems, recv_sems, a, (x, y, 1 - c)) for a in range(n)]
        for cp in cps:
            cp.start()
        for cp in cps:
            cp.wait()

    return _pcall(
        body, name=name, in_specs=[_ANY] * n, out_specs=[_ANY] * n,
        out_shape=[_sds(v.shape[1:], v.dtype) for v in by_chip],
        scratch_shapes=[pltpu.SemaphoreType.DMA((n,)), pltpu.SemaphoreType.DMA((n,))],
    )(*by_chip)


def _scatter_partials_call(parts, name):
    n = len(parts)

    def body(*refs):
        v_refs, o_refs, (send_sems, recv_sems) = refs[:n], refs[n:2 * n], refs[2 * n:]
        x, y, c = _axes()
        peers = _chip_peers(x, y)
        cps = [_remote(v_refs[a].at[2 * peers[k][0] + peers[k][1]], o_refs[a].at[k], send_sems, recv_sems, 3 * a + k,
                       (*peers[k], c)) for k in range(3) for a in range(n)]
        for cp in cps:
            cp.start()
        for cp in cps:
            cp.wait()

    return _pcall(
        body, name=name, in_specs=[_ANY] * n, out_specs=[_ANY] * n,
        out_shape=[_sds((3,) + v.shape[1:], v.dtype) for v in parts],
        scratch_shapes=[pltpu.SemaphoreType.DMA((3 * n,)), pltpu.SemaphoreType.DMA((3 * n,))],
    )(*parts)


def _pair_result_call(fins, name):
    n = len(fins)

    def body(*refs):
        v_refs, o_refs, (send_sems, recv_sems) = refs[:n], refs[n:2 * n], refs[2 * n:]
        x, y, c = _axes()
        cps = [_remote(v_refs[a].at[c], o_refs[a].at[c], send_sems, recv_sems, a, (x, y, 1 - c)) for a in range(n)]
        for cp in cps:
            cp.start()
        for a in range(n):
            cps[a].wait_send()
            _remote(v_refs[a].at[c], o_refs[a].at[1 - c], send_sems, recv_sems, a, (x, y, 1 - c)).wait_recv()

    return _pcall(
        body, name=name, in_specs=[_ANY] * n, out_specs=[_ANY] * n,
        out_shape=[_sds(v.shape, v.dtype) for v in fins], input_output_aliases={a: a for a in range(n)},
        scratch_shapes=[pltpu.SemaphoreType.DMA((n,)), pltpu.SemaphoreType.DMA((n,))],
    )(*fins)


def _allreduce_small_call(v):
    rows, cols = v.shape

    def body(v_ref, o_ref, buf, send_sems, recv_sems):
        x, y, c = _axes()
        chip, peers = 2 * x + y, _chip_peers(x, y)
        pair = _remote(v_ref, buf.at[0], send_sems, recv_sems, 0, (x, y, 1 - c))
        pair.start()
        pair.wait()
        buf[0] = v_ref[...] + buf[0]
        cps = [_remote(buf.at[0], buf.at[1 + k], send_sems, recv_sems, 1 + k, (*peers[k], c)) for k in range(3)]
        for cp in cps:
            cp.start()
        for cp in cps:
            cp.wait()
        acc = buf[chip]
        for d in range(1, 4):
            acc = acc + buf[lax.bitwise_xor(chip, d)]
        o_ref[...] = acc

    return _pcall(
        body, name="allreduce_small", in_specs=[_VMEM], out_specs=_VMEM, out_shape=_sds(v.shape, F32),
        scratch_shapes=[pltpu.VMEM((4, rows, cols), F32), pltpu.SemaphoreType.DMA((4,)), pltpu.SemaphoreType.DMA((4,))],
        compiler_params=_cparams(),
    )(v)


def _pack_rows(arrays, dtype, total_rows=None):
    parts = []
    for a in arrays:
        flat = a.astype(dtype).reshape(-1)
        parts.append(jnp.pad(flat, (0, -flat.shape[0] % D_MODEL)).reshape(-1, D_MODEL))
    out = jnp.concatenate(parts, axis=0)
    total_rows = total_rows or out.shape[0] + (-out.shape[0] % 8)
    return jnp.pad(out, ((0, total_rows - out.shape[0]), (0, 0)))


def _unpack_rows(packed, shapes):
    out, r = [], 0
    for s in shapes:
        n = math.prod(s)
        nr = -(-n // D_MODEL)
        out.append(packed[r:r + nr].reshape(-1)[:n].reshape(s))
        r += nr
    return out


def _to_r(w_t):
    pad = jnp.zeros((N_R - N_ORIG,) + w_t.shape[1:], w_t.dtype)
    return jnp.concatenate([w_t[A_ORIG_W:BA_ORIG], w_t[:A_ORIG_W], w_t[BA_ORIG + 2 * HEADS:],
                            w_t[BA_ORIG:BA_ORIG + 2 * HEADS], pad], axis=0)


_BIG = ("w_in", "a_proj", "b_proj", "c_proj", "w_out")
_SHARD_AXIS = {"w_in": 2, "a_proj": 2, "b_proj": 1, "c_proj": 2, "w_out": 1, "a_dw": 2, "b_conv": 2}
_BIG_AXIS = _SHARD_AXIS


def _join_chips(g, axis):
    g = jnp.moveaxis(g, 0, axis)
    return g.reshape(g.shape[:axis] + (4 * g.shape[axis + 1],) + g.shape[axis + 2:])


def _split_chips(a, axis):
    n = a.shape[axis] // 4
    by_chip = jnp.moveaxis(a.reshape(a.shape[:axis] + (4, n) + a.shape[axis + 1:]), axis, 0)
    return jnp.moveaxis(by_chip.reshape(4, 2, by_chip.shape[1] // 2, by_chip.shape[2]), 1, 0)
_ORDER = ("norm_g", "w_in", "a_dw", "a_dw_b", "a_ln_g", "a_ln_b", "a_proj", "b_conv", "b_a_log", "b_dt_bias",
          "b_onorm_g", "b_proj", "c_ln_g", "c_ln_b", "c_ws", "c_bs", "c_proj", "w_out", "final_g")


def _reduce_begin(layer_grads, where, name):
    by_chip = [_split_chips(layer_grads[k], _SHARD_AXIS[k] - 1) for k in _BIG]
    theirs = _pair_partials_call(by_chip, name + "pair_partials")
    return [_add_pair_call(b, r, where, name + "pair_sum_" + k) for k, b, r in zip(_BIG, by_chip, theirs)]


def _reduce_end(sums, got, where, name):
    fins = [_add_recv_call(s[1], r, where, name + "chip_sum_" + k) for k, s, r in zip(_BIG, sums, got)]
    return [o.reshape((-1,) + o.shape[2:]) for o in _pair_result_call(fins, name + "pair_result")]


def _local_step(x, tgt, w, where=None):
    bsz, t, _ = x.shape
    m = bsz * t
    depth = w["norm_g"].shape[0]
    row = lambda v: v.reshape(1, -1)
    w_rt = _to_r(w["w_in_t"])
    saved, xl = [], x
    for l in range(depth):
        n = f"l{l}_"
        par = dict(
            adw=jnp.pad(w["a_dw"][l], ((0, 1), (0, 0))), adb=row(w["a_dw_b"][l]), alg=row(w["a_ln_g"][l]),
            alb=row(w["a_ln_b"][l]), bconv=w["b_conv"][l],
            alog=jnp.pad(row(w["b_a_log"][l]), ((0, 0), (HEADS, DK - 2 * HEADS))),
            dtb=jnp.pad(row(w["b_dt_bias"][l]), ((0, 0), (HEADS, DK - 2 * HEADS))), og=row(w["b_onorm_g"][l]),
            clg=row(w["c_ln_g"][l]), clb=row(w["c_ln_b"][l]), cws=w["c_ws"][l],
            cbias=jnp.repeat(w["c_bs"][l].T, SG_C, axis=1), ng=row(w["norm_g"][l]),
            wa=w["a_proj"][l], wb=w["b_proj"][l], wc=w["c_proj"][l], wo=w["w_out"][l], wrt=w_rt[:, l], wr=w_rt[:, l].T)
        h = _rms_fwd_call(xl.reshape(m, D_MODEL), par["ng"], n + "norm")
        proj = _matmul(h, par["wr"], F32, 2048, 1152, 1024, n + "in_proj").reshape(bsz, t, N_R)
        ya, conv = _a_fwd_call(proj, par["adw"], par["adb"], par["alg"], par["alb"], n + "conv_fwd")
        yb, *dn_saved = _dn_fwd_call(proj, par["bconv"], par["alog"], par["dtb"], par["og"], n + "delta_fwd")
        yc = _c_fwd_call(proj, par["clg"], par["clb"], par["cws"], par["cbias"], n + "gmlp_fwd")
        x_next = _m_fwd_call(ya, yb, yc, proj, xl, par["wa"], par["wb"], par["wc"], par["wo"], n + "merge_fwd")
        saved.append((par, xl, h, proj, ya, yb, yc, conv, dn_saved))
        xl = x_next
    dout, dfg, loss = _loss_call(xl.reshape(m, D_MODEL), tgt.reshape(m, D_MODEL), row(w["final_g"]))
    dout = dout.reshape(bsz, t, D_MODEL)
    g = {k: [None] * depth for k in _ORDER if k != "final_g"}
    pending = None
    for l in reversed(range(depth)):
        n = f"l{l}_"
        par, xl, h, proj, ya, yb, yc, conv, dn_saved = saved[l]
        dya, dyb, dyc, dgate, merged, dps = _m_bwd_call(dout, ya, yb, yc, proj, par["wa"], par["wb"], par["wc"],
                                                        par["wo"], n + "merge_bwd")
        g["a_proj"][l], g["b_proj"][l], g["c_proj"][l], g["w_out"][l] = _m_wgrad_call(ya, yb, yc, merged, dps, dout,
                                                                                      n + "merge_wgrad")
        travel = [s[0] for s in pending[1]] if pending else []
        da, dadw, dasm, *got = _a_bwd_call(dya, conv, proj, par["adw"], par["alg"], par["alb"], n + "conv_bwd", travel)
        if pending:
            for k, r in zip(_BIG, _reduce_end(pending[1], got, where, f"l{pending[0]}_")):
                g[k][pending[0]] = r
            pending = None
        db, dba, dbconv, dbsm = _dn_bwd_call(dyb, dn_saved, proj, par["bconv"], par["alog"], par["dtb"], par["og"],
                                             n + "delta_bwd")
        dc, dcws, dcbs, dcsm = _c_bwd_call(dyc, proj, par["clg"], par["clb"], par["cws"], par["cbias"], n + "gmlp_bwd")
        segs = [s.reshape(m, s.shape[-1]) for s in (db, da, dc, dgate, dba)]
        dh = _matmul_ksegs(segs, par["wrt"], 2048, 1024, 512, n + "in_proj_dx")
        ht = h.T
        dwb, dwa, dwc, dwg, dwba = [_matmul(ht, s, F32, 1024, 1024 if s.shape[1] % 1024 == 0 else 768, 2048,
                                            n + "in_proj_dw_" + tag) for s, tag in zip(segs, "bacgs")]
        g["w_in"][l] = jnp.concatenate([dwa, dwb, dwba[:, :2 * HEADS], dwc, dwg], axis=1)
        dx, dng = _rms_bwd_call(xl.reshape(m, D_MODEL), dh, par["ng"], dout.reshape(m, D_MODEL), n + "norm_bwd")
        dout = dx.reshape(bsz, t, D_MODEL)
        g["norm_g"][l] = dng[0]
        g["a_dw"][l], g["a_dw_b"][l], g["a_ln_g"][l], g["a_ln_b"][l] = dadw[:CONV_K], dasm[0], dasm[1], dasm[2]
        g["b_conv"][l], g["b_a_log"][l], g["b_dt_bias"][l] = dbconv[:SHORT_K], dbsm[0, :HEADS], dbsm[1, :HEADS]
        g["b_onorm_g"][l] = dbsm[2]
        g["c_ln_g"][l], g["c_ln_b"][l], g["c_ws"][l] = dcsm[0], dcsm[1], dcws
        g["c_bs"][l] = dcbs.reshape(SG_C, SG_G, SG_C).sum(-1).T
        if where is not None:
            sums = _reduce_begin({k: g[k][l] for k in _BIG}, where, n)
            if l > 0:
                pending = (l, sums)
            else:
                got = _scatter_partials_call([s[0] for s in sums], n + "scatter_partials")
                for k, r in zip(_BIG, _reduce_end(sums, got, where, n)):
                    g[k][l] = r
    grads = {k: jnp.stack(v) for k, v in g.items()}
    grads["final_g"] = dfg[0]
    return loss[0, 0], dout, grads


def kernel(x, norm_g, w_in, a_dw, a_dw_b, a_ln_g, a_ln_b, a_proj, b_conv, b_a_log, b_dt_bias, b_onorm_g, b_proj, c_ln_g, c_ln_b, c_ws, c_bs, c_proj, w_out, final_g, loss_target, m_norm_g, m_w_in, m_a_dw, m_a_dw_b, m_a_ln_g, m_a_ln_b, m_a_proj, m_b_conv, m_b_a_log, m_b_dt_bias, m_b_onorm_g, m_b_proj, m_c_ln_g, m_c_ln_b, m_c_ws, m_c_bs, m_c_proj, m_w_out, m_final_g, v_norm_g, v_w_in, v_a_dw, v_a_dw_b, v_a_ln_g, v_a_ln_b, v_a_proj, v_b_conv, v_b_a_log, v_b_dt_bias, v_b_onorm_g, v_b_proj, v_c_ln_g, v_c_ln_b, v_c_ws, v_c_bs, v_c_proj, v_w_out, v_final_g):
    given = dict(locals())
    ws = {k: given[k] for k in _ORDER}
    xi, yi, ci = _axes()
    chip = 2 * xi + yi
    where = jnp.stack([ci, chip]).astype(jnp.int32)

    others = _BIG[1:] + ("a_dw", "b_conv")
    gathered = _gather_weights_call([jnp.transpose(w_in, (2, 0, 1)).astype(BF16)] + [ws[k].astype(BF16) for k in _BIG[1:]]
                                    + [a_dw, b_conv])
    full = {k: v for k, v in ws.items() if k != "w_in"}
    full["w_in_t"] = gathered[0].reshape((N_ORIG,) + gathered[0].shape[2:])
    for k, g in zip(others, gathered[1:]):
        full[k] = _join_chips(g, _SHARD_AXIS[k])

    loss, grad_x, grads = _local_step(x, loss_target, full, where)
    out_g = {k: grads[k] for k in _BIG}

    rest = [k for k in _ORDER if k not in _BIG]
    rest_shapes = [grads[k].shape for k in rest] + [(1,)]
    summed = _unpack_rows(_allreduce_small_call(_pack_rows([grads[k] for k in rest] + [loss.reshape(1)], F32)),
                          rest_shapes)
    out_g.update(zip(rest, summed[:-1]))
    out_g["a_dw"] = lax.dynamic_slice_in_dim(out_g["a_dw"], chip * a_dw.shape[2], a_dw.shape[2], axis=2)
    out_g["b_conv"] = lax.dynamic_slice_in_dim(out_g["b_conv"], chip * b_conv.shape[2], b_conv.shape[2], axis=2)

    def adam(k):
        operands = (ws[k], out_g[k], given["m_" + k], given["v_" + k])
        if k != "w_in":
            return _adamw_call(*operands, "adamw_" + k)
        w_t, g_t, m_t, v_t = (jnp.transpose(a, (2, 0, 1)) for a in operands)
        out_g[k] = jnp.transpose(g_t, (1, 2, 0))
        return tuple(jnp.transpose(o, (1, 2, 0)) for o in _adamw_call(w_t, g_t, m_t, v_t, "adamw_" + k))

    upd = {k: adam(k) for k in _ORDER}
    return (summed[-1][0], grad_x, *[out_g[k] for k in _ORDER], *[upd[k][0] for k in _ORDER],
            *[upd[k][1] for k in _ORDER], *[upd[k][2] for k in _ORDER])
```

```python
import functools
import math

import jax
import jax.numpy as jnp
from jax import lax
from jax.experimental import pallas as pl
from jax.experimental.pallas import tpu as pltpu

F32 = jnp.float32
BF16 = jnp.bfloat16
MESH = pl.DeviceIdType.MESH

D_MODEL = 1024
CONV_W = 512
CONV_K = 31
HEADS = 8
DK = 128
SHORT_K = 4
CHUNK = 64
SG_W = 512
SG_G = 4
SG_C = 128
EPS = 1e-6
N_ORIG = 10256
BA_ORIG = 5632
A_ORIG_W = 3 * CONV_W
B_W = 4 * D_MODEL
B_OFF, A_OFF, C_OFF, G_OFF, BA_OFF, N_R = 0, 4096, 5632, 7168, 10240, 10368
TB = 256
HALO_A = 32
HALO_B = 8
VMEM_LIMIT = 56 * 1024 * 1024
ADAM_LR, ADAM_B1, ADAM_B2, ADAM_EPS, ADAM_WD, ADAM_STEP = 0.001, 0.9, 0.999, 1e-08, 0.01, 10


def _pcall(body, **kw):
    return pl.pallas_call(body, **kw)


def _cparams(sem=None):
    kw = dict(vmem_limit_bytes=VMEM_LIMIT)
    if sem is not None:
        kw["dimension_semantics"] = sem
    return pltpu.CompilerParams(**kw)


def _sig(x):
    return jax.nn.sigmoid(x)


def _silu(x):
    return x * _sig(x)


def _dsilu(x):
    s = _sig(x)
    return s * (1.0 + x * (1.0 - s))


_GELU_C = math.sqrt(2.0 / math.pi)


def _gelu(x):
    return 0.5 * x * (1.0 + jnp.tanh(_GELU_C * (x + 0.044715 * x * x * x)))


def _dgelu(x):
    t = jnp.tanh(_GELU_C * (x + 0.044715 * x * x * x))
    return 0.5 * (1.0 + t) + 0.5 * x * (1.0 - t * t) * _GELU_C * (1.0 + 3 * 0.044715 * x * x)


def _softplus(x):
    return jnp.maximum(x, 0.0) + jnp.log1p(jnp.exp(-jnp.abs(x)))


def _dot(a, b, dims):
    return lax.dot_general(a.astype(BF16), b.astype(BF16), (dims, ((), ())), preferred_element_type=F32)


def _nn(a, b):
    return _dot(a, b, ((1,), (0,)))


def _nt(a, b):
    return _dot(a, b, ((1,), (1,)))


def _tn(a, b):
    return _dot(a, b, ((0,), (0,)))


def _tn_mxu(a, b):
    n = a.shape[1]
    eye = (lax.broadcasted_iota(jnp.int32, (n, n), 0) == lax.broadcasted_iota(jnp.int32, (n, n), 1)).astype(BF16)
    return _nn(_nt(eye, a), b)


def _mean(x):
    return jnp.mean(x, axis=-1, keepdims=True)


def _sum0(x):
    return jnp.sum(x, axis=0, keepdims=True)


def _sum1(x):
    return jnp.sum(x, axis=1, keepdims=True)


def _ln_fwd(x, g, b):
    xc = x - _mean(x)
    r = lax.rsqrt(_mean(xc * xc) + EPS)
    xh = xc * r
    return xh * g + b, xh, r


def _ln_bwd(dy, xh, r, g):
    dxh = dy * g
    return r * (dxh - _mean(dxh) - xh * _mean(dxh * xh)), _sum0(dy * xh), _sum0(dy)


def _rms_bwd(dxh, xh, r):
    return r * (dxh - xh * _mean(dxh * xh))


def _rows8(*rows):
    n = rows[0].shape[1]
    return jnp.concatenate(list(rows) + [jnp.zeros((8 - len(rows), n), F32)], axis=0)


def _windows(ext, tb):
    n = ext.shape[0] - 8
    shifted = {0: ext}

    def window(off):
        r = off % 8
        if r not in shifted:
            shifted[r] = ext[r:r + n]
        return shifted[r][off - r:off - r + tb]

    return window


def _a_fwd(val, glu, az, val_h, glu_h, w, bias, g, b):
    tb = val.shape[0]
    win = _windows(jnp.concatenate([val_h * _sig(glu_h), val * _sig(glu)], axis=0), tb)
    c = win(2) * w[0:1]
    for j in range(1, CONV_K):
        c = c + win(2 + j) * w[j:j + 1]
    c = c + bias
    ln, _, _ = _ln_fwd(c, g, b)
    return _silu(ln) * _silu(az), c


def _a_bwd(dy, val, glu, az, c, w, g, b, dc_next):
    tb = val.shape[0]
    ln, xh, r = _ln_fwd(c, g, b)
    sl, sz = _sig(ln), _sig(az)
    dln = dy * (az * sz) * (sl * (1.0 + ln * (1.0 - sl)))
    daz = dy * (ln * sl) * (sz * (1.0 + az * (1.0 - sz)))
    dc, dg, db = _ln_bwd(dln, xh, r, g)
    win = _windows(jnp.concatenate([dc, dc_next], axis=0), tb)
    sg = _sig(glu)
    a = val * sg
    da = win(30) * w[0:1]
    dw_rows = [_sum0(win(30) * a)]
    for j in range(1, CONV_K):
        shifted = win(30 - j)
        da = da + shifted * w[j:j + 1]
        dw_rows.append(_sum0(shifted * a))
    dw = jnp.concatenate(dw_rows + [jnp.zeros((1, CONV_W), F32)], axis=0)
    return da * sg, da * val * sg * (1.0 - sg), daz, dw, _sum0(dc), dg, db, dc[:HALO_A]


def _tril(ws):
    ii = lax.broadcasted_iota(jnp.int32, (SG_C, SG_C), 0)
    jj = lax.broadcasted_iota(jnp.int32, (SG_C, SG_C), 1)
    return [jnp.where(jj <= ii, ws[gi], 0.0) for gi in range(SG_G)], jj <= ii


def _c_mix(wt, vs, bias_full):
    tb = vs.shape[0]
    rows = []
    for n in range(tb // SG_C):
        blks = [_nn(wt[gi], vs[n * SG_C:(n + 1) * SG_C, gi * SG_C:(gi + 1) * SG_C]) for gi in range(SG_G)]
        rows.append(jnp.concatenate(blks, axis=1) + bias_full)
    return jnp.concatenate(rows, axis=0)


def _c_fwd(cu, cv, cz, g, b, ws, bias_full):
    wt, _ = _tril(ws)
    vs, xh, r = _ln_fwd(_gelu(cv), g, b)
    mixed = _c_mix(wt, vs, bias_full)
    return _gelu(cu) * mixed * _silu(cz), (wt, vs, xh, r, mixed)


def _c_bwd(dy, cu, cv, cz, g, b, ws, bias_full):
    tb = cu.shape[0]
    _, (wt, vs, xh, r, mixed) = _c_fwd(cu, cv, cz, g, b, ws, bias_full)
    _, low = _tril(ws)
    u, sz = _gelu(cu), _silu(cz)
    dcu = dy * mixed * sz * _dgelu(cu)
    dcz = dy * u * mixed * _dsilu(cz)
    dmixed = dy * u * sz
    dbs = jnp.zeros((SG_C, SG_W), F32)
    dws = [jnp.zeros((SG_C, SG_C), F32) for _ in range(SG_G)]
    rows = []
    for n in range(tb // SG_C):
        dm_n = dmixed[n * SG_C:(n + 1) * SG_C]
        dbs = dbs + dm_n
        blks = []
        for gi in range(SG_G):
            dm = dm_n[:, gi * SG_C:(gi + 1) * SG_C]
            dws[gi] = dws[gi] + _nt(dm, vs[n * SG_C:(n + 1) * SG_C, gi * SG_C:(gi + 1) * SG_C])
            blks.append(_tn(wt[gi], dm))
        rows.append(jnp.concatenate(blks, axis=1))
    dvs = jnp.concatenate(rows, axis=0)
    dgv, dg, db = _ln_bwd(dvs, xh, r, g)
    dws = [jnp.where(low, d, 0.0) for d in dws]
    return dcu, dgv * _dgelu(cv), dcz, dws, dbs, dg, db


def _m_fwd(ya, yb, yc, g0, g1, g2, wa, wb, wc):
    pa, pb, pc = _nn(ya, wa), _nn(yb, wb), _nn(yc, wc)
    s0, s1, s2 = _sig(g0), _sig(g1), _sig(g2)
    return s0 * pa + s1 * pb + s2 * pc, (pa, pb, pc, s0, s1, s2)


def _m_bwd(dout, ya, yb, yc, g0, g1, g2, wa, wb, wc, wo):
    merged, (pa, pb, pc, s0, s1, s2) = _m_fwd(ya, yb, yc, g0, g1, g2, wa, wb, wc)
    dm = _nt(dout, wo)
    dpa, dpb, dpc = dm * s0, dm * s1, dm * s2
    dgs = (dm * pa * s0 * (1.0 - s0), dm * pb * s1 * (1.0 - s1), dm * pc * s2 * (1.0 - s2))
    return (_nt(dpa, wa), _nt(dpb, wb), _nt(dpc, wc)), dgs, merged, (dpa, dpb, dpc)


def _chunk_masks(c):
    ii = lax.broadcasted_iota(jnp.int32, (c, c), 0)
    jj = lax.broadcasted_iota(jnp.int32, (c, c), 1)
    return ii, jj


def _dn_decay(items):
    ii, jj = _chunk_masks(CHUNK)
    incl, strict, eye = jj <= ii, jj < ii, ii == jj
    for d in items:
        g = d["g"]
        grow = _sum0(jnp.where(eye, g, 0.0))
        gc_col = _sum1(jnp.where(incl, grow, 0.0))
        gc_row = _sum0(jnp.where(ii <= jj, g, 0.0))
        gam_i = jnp.where(incl, jnp.exp(jnp.where(incl, gc_col - gc_row, 0.0)), 0.0)
        gl = _sum0(g)
        egc = jnp.exp(gc_col)
        d.update(gam_i=gam_i, gam_s=jnp.where(strict, gam_i, 0.0), egc=egc, ekd=jnp.exp(gl - gc_col), dl=jnp.exp(gl),
                 gdiff=gc_col - gc_row, qd=d["q"] * egc, rhs_w=d["k"] * (d["b"] * egc))
        d["kd"] = d["k"] * d["ekd"]
    return ii, jj, strict, eye


def _dn_solve(items, ii, jj, eye):
    off = ((ii >> 1) == (jj >> 1)) & ((ii & 1) != 0) & ((jj & 1) == 0)
    for d in items:
        a1 = jnp.where(off, d["a"], 0.0)
        d["t"] = jnp.where(eye, 1.0, 0.0) - a1
        d["m"] = d["a"] - _nn(a1, d["a"])
    b, sh = 2, 2
    while b < CHUNK:
        off = ((ii >> sh) == (jj >> sh)) & ((ii & b) != 0) & ((jj & b) == 0)
        for d in items:
            mo = jnp.where(off, d["m"], 0.0)
            if 2 * b < CHUNK:
                d["m"], d["t"] = d["m"] - _nn(mo, d["m"]), d["t"] - _nn(mo, d["t"])
            else:
                d["t"] = d["t"] - _nn(mo, d["t"])
        b, sh = 2 * b, sh + 1


def _dn_fwd_chunk(items, ss):
    ii, jj, strict, eye = _dn_decay(items)
    for d in items:
        d["a"] = d["b"] * _nt(d["k"], d["k"]) * d["gam_s"]
        d["qk"] = _nt(d["q"], d["k"]) * d["gam_i"]
    _dn_solve(items, ii, jj, eye)
    for d in items:
        d["u"], d["w"] = _nn(d["t"], d["v"] * d["b"]), _nn(d["t"], d["rhs_w"])
    ws = [_nn(d["w"], s) for d, s in zip(items, ss)]
    qs = [_nn(d["qd"], s) for d, s in zip(items, ss)]
    vnew = [d["u"] - w for d, w in zip(items, ws)]
    outs = [q + _nn(d["qk"], vn) for d, q, vn in zip(items, qs, vnew)]
    ss = [d["dl"] * s + _tn(d["kd"], vn) for d, s, vn in zip(items, ss, vnew)]
    return outs, ss, vnew


def _dn_bwd_chunk(items, ss, dss):
    ii, jj, strict, eye = _dn_decay(items)
    eye_b = eye.astype(BF16)
    for d in items:
        k, q = d["k"], d["q"]
        d["kk"] = _nt(k, k)
        d["a"] = d["b"] * d["kk"] * d["gam_s"]
        d["qk"] = _nt(q, k) * d["gam_i"]
        d["qkt"] = _nt(k, q) * jnp.where(ii <= jj, jnp.exp(jnp.where(ii <= jj, -d["gdiff"], 0.0)), 0.0)
    for d, s, ds2 in zip(items, ss, dss):
        d["dvnew"] = _nn(d["qkt"], d["do"]) + _nn(d["kd"], ds2)
        d["dqk"] = _nt(d["do"], d["vnew"])
        d["dqd"] = _nt(d["do"], s)
        d["dkd"] = _nt(d["vnew"], ds2)
    new_dss = []
    for d, s, ds2 in zip(items, ss, dss):
        d["dw"] = -_nt(d["dvnew"], s)
        new_dss.append(_tn_mxu(d["qd"], d["do"]) - _tn_mxu(d["w"], d["dvnew"]) + d["dl"] * ds2)
        d["ddl"] = _sum0(_sum1(s * ds2))
    for d in items:
        tt = _nt(eye_b, d["t"])
        d["drhs_u"], d["drhs_w"] = _nn(tt, d["dvnew"]), _nn(tt, d["dw"])
    for d in items:
        d["da"] = jnp.where(strict, -(_nt(d["drhs_u"], d["u"]) + _nt(d["drhs_w"], d["w"])), 0.0)
    outs = []
    for d in items:
        q, k, v, b, egc = d["q"], d["k"], d["v"], d["b"], d["egc"]
        drhs_u, drhs_w, da = d["drhs_u"], d["drhs_w"], d["da"]
        dbeta = _sum1(da * d["kk"] * d["gam_s"]) + _sum1(drhs_u * v) + _sum1(drhs_w * k) * egc
        dkk = da * b * d["gam_s"]
        e = da * d["a"] + d["dqk"] * d["qk"]
        s_kd = _sum1(d["dkd"] * d["kd"])
        dgc_col = _sum1(e) + _sum1(drhs_w * d["rhs_w"]) + _sum1(d["dqd"] * d["qd"]) - s_kd
        dgc_row = _sum0(jnp.where(eye, dgc_col, 0.0)) - _sum0(e)
        dg = _sum1(jnp.where(jj >= ii, dgc_row, 0.0)) + (_sum0(s_kd) + d["ddl"] * d["dl"])
        dqkg = d["dqk"] * d["gam_i"]
        dq = _nn(dqkg, k) + d["dqd"] * egc
        dk = _tn_mxu(dqkg, q) + _nn(dkk, k) + _tn_mxu(dkk, k) + drhs_w * (b * egc) + d["dkd"] * d["ekd"]
        outs.append((dq, dk, drhs_u * b, dbeta, dg))
    return outs, new_dss


def _short_conv(raw, halo, w):
    tb = raw.shape[0]
    ext = jnp.concatenate([halo, raw], axis=0)
    out = ext[5:5 + tb] * w[0:1]
    for j in range(1, SHORT_K):
        out = out + ext[5 + j:5 + j + tb] * w[j:j + 1]
    return out


def _dn_gates(ba, alog8, dtb8):
    xg = ba + dtb8
    ea8 = jnp.exp(alog8)
    return _sig(ba), -ea8 * _softplus(xg), xg, ea8


def _dn_item(pre_q, pre_k, pre_v, b, g):
    qc, kc, vc = (p * _sig(p) for p in (pre_q, pre_k, pre_v))
    nq = lax.rsqrt(_sum1(qc * qc) + EPS)
    nk = lax.rsqrt(_sum1(kc * kc) + EPS)
    return dict(q=qc * nq * (DK ** -0.5), k=kc * nk, v=vc, b=b, g=g, qc=qc, kc=kc, nq=nq, nk=nk)


def _dn_out(o, z, sz, og):
    r = lax.rsqrt(_mean(o * o) + EPS)
    xh = o * r
    return xh * og * (z * sz), xh, r


def _sds(shape, dtype):
    return jax.ShapeDtypeStruct(tuple(shape), dtype)


def _matmul(a, b, out_dtype, tm, tn, tk, name):
    m, kd = a.shape
    n = b.shape[1]
    tm, tn, tk = min(tm, m), min(tn, n), min(tk, kd)
    nk = kd // tk

    def body(a_ref, b_ref, o_ref, acc):
        @pl.when(pl.program_id(2) == 0)
        def _():
            acc[...] = jnp.zeros_like(acc)

        acc[...] += jnp.dot(a_ref[...], b_ref[...], preferred_element_type=F32)

        @pl.when(pl.program_id(2) == nk - 1)
        def _():
            o_ref[...] = acc[...].astype(o_ref.dtype)

    return _pcall(
        body, name=name, grid=(m // tm, n // tn, nk),
        in_specs=[pl.BlockSpec((tm, tk), lambda i, j, k: (i, k)), pl.BlockSpec((tk, tn), lambda i, j, k: (k, j))],
        out_specs=pl.BlockSpec((tm, tn), lambda i, j, k: (i, j)), out_shape=_sds((m, n), out_dtype),
        scratch_shapes=[pltpu.VMEM((tm, tn), F32)], compiler_params=_cparams(("parallel", "parallel", "arbitrary")),
    )(a, b)


def _travel_copies(v_refs, o_refs, send_sems, recv_sems):
    x, y, c = _axes()
    peers = _chip_peers(x, y)
    return [_remote(v_refs[a].at[2 * peers[k][0] + peers[k][1]], o_refs[a].at[k], send_sems, recv_sems, 3 * a + k,
                    (*peers[k], c)) for k in range(3) for a in range(len(v_refs))]


def _matmul_ksegs(a_segs, b, tm, tn, tk, name, travel=()):
    m, n = a_segs[0].shape[0], b.shape[1]
    tm, tn = min(tm, m), min(tn, n)
    main, tail = a_segs[:-1], a_segs[-1]
    steps = [s.shape[1] // tk for s in main]
    starts = [sum(steps[:i]) for i in range(len(main))]
    nk = sum(steps)
    wt = tail.shape[1]
    n_main, nv = len(main), len(travel)
    grid = (m // tm, n // tn, nk + 1)

    def body(*refs):
        a_refs, (at_ref, b_ref, bt_ref), refs = refs[:n_main], refs[n_main:n_main + 3], refs[n_main + 3:]
        v_refs, o_ref, refs = refs[:nv], refs[nv], refs[nv + 1:]
        got_refs, acc, sems = refs[:nv], refs[nv], refs[nv + 1:]
        k = pl.program_id(2)
        if nv:
            first = (pl.program_id(0) == 0) & (pl.program_id(1) == 0) & (k == 0)
            last = (pl.program_id(0) == grid[0] - 1) & (pl.program_id(1) == grid[1] - 1) & (k == nk)

            @pl.when(first)
            def _():
                for cp in _travel_copies(v_refs, got_refs, *sems):
                    cp.start()

        @pl.when(k == 0)
        def _():
            acc[...] = jnp.zeros_like(acc)

        for a_ref, k0, ns in zip(a_refs, starts, steps):
            @pl.when((k >= k0) & (k < k0 + ns))
            def _():
                acc[...] += jnp.dot(a_ref[...], b_ref[...], preferred_element_type=F32)

        @pl.when(k == nk)
        def _():
            o_ref[...] = acc[...] + jnp.dot(at_ref[...], bt_ref[...], preferred_element_type=F32)

        if nv:
            @pl.when(last)
            def _():
                for cp in _travel_copies(v_refs, got_refs, *sems):
                    cp.wait()

    seg_specs = [pl.BlockSpec((tm, tk), functools.partial(lambda i, j, k, k0, n_s: (i, jnp.clip(k - k0, 0, n_s - 1)),
                                                          k0=k0, n_s=n_s)) for k0, n_s in zip(starts, steps)]
    out = _pcall(
        body, name=name, grid=grid,
        in_specs=seg_specs + [pl.BlockSpec((tm, wt), lambda i, j, k: (i, 0)),
                              pl.BlockSpec((tk, tn), lambda i, j, k: (jnp.minimum(k, nk - 1), j)),
                              pl.BlockSpec((wt, tn), lambda i, j, k: (nk * tk // wt, j))] + [_ANY] * nv,
        out_specs=[pl.BlockSpec((tm, tn), lambda i, j, k: (i, j))] + [_ANY] * nv,
        out_shape=[_sds((m, n), F32)] + [_sds((3,) + v.shape[1:], v.dtype) for v in travel],
        scratch_shapes=[pltpu.VMEM((tm, tn), F32)] + [pltpu.SemaphoreType.DMA((3 * nv,))] * (2 if nv else 0),
        compiler_params=_cparams(("arbitrary",) * 3 if nv else ("parallel", "parallel", "arbitrary")),
    )(*main, tail, b, b, *travel)
    return (out[0], out[1:]) if nv else out[0]


def _rms_fwd_call(x, g, name):
    m = x.shape[0]
    tm = min(512, m)

    def body(x_ref, g_ref, o_ref):
        xv = x_ref[...]
        o_ref[...] = (xv * lax.rsqrt(_mean(xv * xv) + EPS) * g_ref[...]).astype(BF16)

    return _pcall(
        body, name=name, grid=(m // tm,),
        in_specs=[pl.BlockSpec((tm, D_MODEL), lambda i: (i, 0)), pl.BlockSpec((1, D_MODEL), lambda i: (0, 0))],
        out_specs=pl.BlockSpec((tm, D_MODEL), lambda i: (i, 0)), out_shape=_sds((m, D_MODEL), BF16),
        compiler_params=_cparams(("parallel",)),
    )(x, g)


def _rms_bwd_call(x, dh, g, dres, name):
    m = x.shape[0]
    tm = min(512, m)

    def body(x_ref, dh_ref, g_ref, dr_ref, dx_ref, dg_ref):
        @pl.when(pl.program_id(0) == 0)
        def _():
            dg_ref[...] = jnp.zeros_like(dg_ref)

        xv, dhv = x_ref[...], dh_ref[...]
        r = lax.rsqrt(_mean(xv * xv) + EPS)
        xh = xv * r
        dx_ref[...] = _rms_bwd(dhv * g_ref[...], xh, r) + dr_ref[...]
        dg_ref[...] += _rows8(_sum0(dhv * xh))

    row = pl.BlockSpec((tm, D_MODEL), lambda i: (i, 0))
    return _pcall(
        body, name=name, grid=(m // tm,),
        in_specs=[row, row, pl.BlockSpec((1, D_MODEL), lambda i: (0, 0)), row],
        out_specs=[row, pl.BlockSpec((8, D_MODEL), lambda i: (0, 0))],
        out_shape=[_sds((m, D_MODEL), F32), _sds((8, D_MODEL), F32)], compiler_params=_cparams(("arbitrary",)),
    )(x, dh, g, dres)


def _loss_call(x, tgt, g):
    m = x.shape[0]
    tm = min(512, m)

    def body(x_ref, t_ref, g_ref, dx_ref, dg_ref, l_ref):
        @pl.when(pl.program_id(0) == 0)
        def _():
            dg_ref[...] = jnp.zeros_like(dg_ref)
            l_ref[...] = jnp.zeros_like(l_ref)

        xv = x_ref[...]
        r = lax.rsqrt(_mean(xv * xv) + EPS)
        xh = xv * r
        err = xh * g_ref[...] - t_ref[...]
        dy = err * (1.0 / D_MODEL)
        dx_ref[...] = _rms_bwd(dy * g_ref[...], xh, r)
        dg_ref[...] += _rows8(_sum0(dy * xh))
        l_ref[...] += 0.5 * _sum0(_mean(err * err))

    row = pl.BlockSpec((tm, D_MODEL), lambda i: (i, 0))
    return _pcall(
        body, name="loss_head", grid=(m // tm,),
        in_specs=[row, row, pl.BlockSpec((1, D_MODEL), lambda i: (0, 0))],
        out_specs=[row, pl.BlockSpec((8, D_MODEL), lambda i: (0, 0)), pl.BlockSpec((8, 128), lambda i: (0, 0))],
        out_shape=[_sds((m, D_MODEL), F32), _sds((8, D_MODEL), F32), _sds((8, 128), F32)],
        compiler_params=_cparams(("arbitrary",)),
    )(x, tgt, g)


def _halo_idx(i, rows):
    return jnp.maximum(i * (TB // rows) - 1, 0)


def _a_tiles(nt, rev):
    ti = (lambda i: nt - 1 - i) if rev else (lambda i: i)
    c0 = A_OFF // CONV_W
    return [pl.BlockSpec((1, TB, CONV_W), functools.partial(lambda b, i, c: (b, ti(i), c), c=c0 + c)) for c in range(3)]


def _a_fwd_call(proj, w, bias, g, b, name):
    bsz, t, _ = proj.shape
    nt = t // TB
    c0 = A_OFF // CONV_W

    def body(val_ref, glu_ref, az_ref, vh_ref, gh_ref, w_ref, bias_ref, g_ref, b_ref, y_ref, c_ref):
        keep = jnp.where(pl.program_id(1) > 0, 1.0, 0.0)
        y, c = _a_fwd(val_ref[0], glu_ref[0], az_ref[0], vh_ref[0] * keep, gh_ref[0], w_ref[...], bias_ref[...],
                      g_ref[...], b_ref[...])
        y_ref[0] = y.astype(BF16)
        c_ref[0] = c

    halo = [pl.BlockSpec((1, HALO_A, CONV_W), functools.partial(lambda b, i, c: (b, _halo_idx(i, HALO_A), c), c=c0 + c))
            for c in range(2)]
    par = [pl.BlockSpec((HALO_A, CONV_W), lambda b, i: (0, 0))] + [pl.BlockSpec((1, CONV_W), lambda b, i: (0, 0))] * 3
    tile = pl.BlockSpec((1, TB, CONV_W), lambda b, i: (b, i, 0))
    return _pcall(
        body, name=name, grid=(bsz, nt), in_specs=_a_tiles(nt, False) + halo + par, out_specs=[tile, tile],
        out_shape=[_sds((bsz, t, CONV_W), BF16), _sds((bsz, t, CONV_W), F32)],
        compiler_params=_cparams(("parallel", "parallel")),
    )(proj, proj, proj, proj, proj, w, bias, g, b)


def _a_bwd_call(dy, conv, proj, w, g, b, name, travel=()):
    bsz, t, _ = proj.shape
    nt = t // TB
    nv = len(travel)

    def body(*refs):
        (dy_ref, c_ref, val_ref, glu_ref, az_ref, w_ref, g_ref, b_ref), refs = refs[:8], refs[8:]
        v_refs, (da_ref, dw_ref, ds_ref), refs = refs[:nv], refs[nv:nv + 3], refs[nv + 3:]
        o_refs, carry, sems = refs[:nv], refs[nv], refs[nv + 1:]
        ip = pl.program_id(1)

        def exchange():
            return _travel_copies(v_refs, o_refs, *sems)

        @pl.when((pl.program_id(0) == 0) & (ip == 0))
        def _():
            dw_ref[...] = jnp.zeros_like(dw_ref)
            ds_ref[...] = jnp.zeros_like(ds_ref)
            if nv:
                for cp in exchange():
                    cp.start()

        @pl.when(ip == 0)
        def _():
            carry[...] = jnp.zeros_like(carry)

        dval, dglu, daz, dw, dbias, dg, db, head = _a_bwd(dy_ref[0], val_ref[0], glu_ref[0], az_ref[0], c_ref[0],
                                                          w_ref[...], g_ref[...], b_ref[...], carry[...])
        carry[...] = head
        for n, dcol in enumerate((dval, dglu, daz)):
            da_ref[0, :, n * CONV_W:(n + 1) * CONV_W] = dcol.astype(BF16)
        dw_ref[...] += dw
        ds_ref[...] += _rows8(dbias, dg, db)

        if nv:
            @pl.when((pl.program_id(0) == bsz - 1) & (ip == nt - 1))
            def _():
                for cp in exchange():
                    cp.wait()

    rtile = lambda b, i: (b, nt - 1 - i, 0)
    par = [pl.BlockSpec((HALO_A, CONV_W), lambda b, i: (0, 0))] + [pl.BlockSpec((1, CONV_W), lambda b, i: (0, 0))] * 2
    return _pcall(
        body, name=name, grid=(bsz, nt),
        in_specs=[pl.BlockSpec((1, TB, CONV_W), rtile)] * 2 + _a_tiles(nt, True) + par + [_ANY] * nv,
        out_specs=[pl.BlockSpec((1, TB, 3 * CONV_W), rtile), pl.BlockSpec((HALO_A, CONV_W), lambda b, i: (0, 0)),
                   pl.BlockSpec((8, CONV_W), lambda b, i: (0, 0))] + [_ANY] * nv,
        out_shape=[_sds((bsz, t, 3 * CONV_W), BF16), _sds((HALO_A, CONV_W), F32), _sds((8, CONV_W), F32)]
        + [_sds((3,) + v.shape[1:], v.dtype) for v in travel],
        scratch_shapes=[pltpu.VMEM((HALO_A, CONV_W), F32)] + [pltpu.SemaphoreType.DMA((3 * nv,))] * (2 if nv else 0),
        compiler_params=_cparams(("arbitrary", "arbitrary")),
    )(dy, conv, proj, proj, proj, w, g, b, *travel)


def _c_specs():
    c0 = C_OFF // SG_W
    tile = [pl.BlockSpec((1, TB, SG_W), functools.partial(lambda b, i, c: (b, i, c), c=c0 + c)) for c in range(3)]
    par = [pl.BlockSpec((1, SG_W), lambda b, i: (0, 0))] * 2 + [
        pl.BlockSpec((SG_G, SG_C, SG_C), lambda b, i: (0, 0, 0)), pl.BlockSpec((SG_C, SG_W), lambda b, i: (0, 0))]
    return tile + par


def _c_fwd_call(proj, g, b, ws, bias_full, name):
    bsz, t, _ = proj.shape

    def body(cu_ref, cv_ref, cz_ref, g_ref, b_ref, ws_ref, bf_ref, y_ref):
        y, _ = _c_fwd(cu_ref[0], cv_ref[0], cz_ref[0], g_ref[...], b_ref[...], ws_ref[...], bf_ref[...])
        y_ref[0] = y.astype(BF16)

    return _pcall(
        body, name=name, grid=(bsz, t // TB), in_specs=_c_specs(),
        out_specs=pl.BlockSpec((1, TB, SG_W), lambda b, i: (b, i, 0)), out_shape=_sds((bsz, t, SG_W), BF16),
        compiler_params=_cparams(("parallel", "parallel")),
    )(proj, proj, proj, g, b, ws, bias_full)


def _c_bwd_call(dy, proj, g, b, ws, bias_full, name):
    bsz, t, _ = proj.shape

    def body(dy_ref, cu_ref, cv_ref, cz_ref, g_ref, b_ref, ws_ref, bf_ref, dc_ref, dws_ref, dbs_ref, ds_ref):
        @pl.when((pl.program_id(0) == 0) & (pl.program_id(1) == 0))
        def _():
            dws_ref[...] = jnp.zeros_like(dws_ref)
            dbs_ref[...] = jnp.zeros_like(dbs_ref)
            ds_ref[...] = jnp.zeros_like(ds_ref)

        dcu, dcv, dcz, dws, dbs, dg, db = _c_bwd(dy_ref[0], cu_ref[0], cv_ref[0], cz_ref[0], g_ref[...], b_ref[...],
                                                 ws_ref[...], bf_ref[...])
        for n, dcol in enumerate((dcu, dcv, dcz)):
            dc_ref[0, :, n * SG_W:(n + 1) * SG_W] = dcol.astype(BF16)
        for gi in range(SG_G):
            dws_ref[gi] += dws[gi]
        dbs_ref[...] += dbs
        ds_ref[...] += _rows8(dg, db)

    tile = lambda b, i: (b, i, 0)
    return _pcall(
        body, name=name, grid=(bsz, t // TB), in_specs=[pl.BlockSpec((1, TB, SG_W), tile)] + _c_specs(),
        out_specs=[pl.BlockSpec((1, TB, 3 * SG_W), tile), pl.BlockSpec((SG_G, SG_C, SG_C), lambda b, i: (0, 0, 0)),
                   pl.BlockSpec((SG_C, SG_W), lambda b, i: (0, 0)), pl.BlockSpec((8, SG_W), lambda b, i: (0, 0))],
        out_shape=[_sds((bsz, t, 3 * SG_W), BF16), _sds((SG_G, SG_C, SG_C), F32), _sds((SG_C, SG_W), F32),
                   _sds((8, SG_W), F32)],
        compiler_params=_cparams(("arbitrary", "arbitrary")),
    )(dy, proj, proj, proj, g, b, ws, bias_full)


def _m_specs():
    g0 = G_OFF // D_MODEL
    y = [pl.BlockSpec((1, TB, n), lambda b, i: (b, i, 0)) for n in (CONV_W, D_MODEL, SG_W)]
    gates = [pl.BlockSpec((1, TB, D_MODEL), functools.partial(lambda b, i, c: (b, i, c), c=g0 + c)) for c in range(3)]
    return y + gates


def _w_specs(*shapes):
    return [pl.BlockSpec(s, lambda b, i: (0, 0)) for s in shapes]


def _m_fwd_call(ya, yb, yc, proj, x, wa, wb, wc, wo, name):
    bsz, t, _ = x.shape

    def body(ya_ref, yb_ref, yc_ref, g0_ref, g1_ref, g2_ref, x_ref, wa_ref, wb_ref, wc_ref, wo_ref, o_ref):
        merged, _ = _m_fwd(ya_ref[0], yb_ref[0], yc_ref[0], g0_ref[0], g1_ref[0], g2_ref[0], wa_ref[...], wb_ref[...],
                           wc_ref[...])
        o_ref[0] = x_ref[0] + _nn(merged, wo_ref[...])

    tile = pl.BlockSpec((1, TB, D_MODEL), lambda b, i: (b, i, 0))
    return _pcall(
        body, name=name, grid=(bsz, t // TB),
        in_specs=_m_specs() + [tile] + _w_specs(wa.shape, wb.shape, wc.shape, wo.shape),
        out_specs=tile, out_shape=_sds(x.shape, F32), compiler_params=_cparams(("parallel", "parallel")),
    )(ya, yb, yc, proj, proj, proj, x, wa, wb, wc, wo)


def _m_bwd_call(dout, ya, yb, yc, proj, wa, wb, wc, wo, name):
    bsz, t, _ = dout.shape

    def body(do_ref, ya_ref, yb_ref, yc_ref, g0_ref, g1_ref, g2_ref, wa_ref, wb_ref, wc_ref, wo_ref, dya_ref, dyb_ref,
             dyc_ref, dg_ref, mg_ref, dp_ref):
        dys, dgs, merged, dps = _m_bwd(do_ref[0], ya_ref[0], yb_ref[0], yc_ref[0], g0_ref[0], g1_ref[0], g2_ref[0],
                                       wa_ref[...], wb_ref[...], wc_ref[...], wo_ref[...])
        dya_ref[0], dyb_ref[0], dyc_ref[0] = dys
        mg_ref[0] = merged.astype(BF16)
        for n in range(3):
            dg_ref[0, :, n * D_MODEL:(n + 1) * D_MODEL] = dgs[n].astype(BF16)
            dp_ref[0, :, n * D_MODEL:(n + 1) * D_MODEL] = dps[n].astype(BF16)

    tile = lambda n: pl.BlockSpec((1, TB, n), lambda b, i: (b, i, 0))
    widths = (CONV_W, D_MODEL, SG_W, 3 * D_MODEL, D_MODEL, 3 * D_MODEL)
    dts = (F32, F32, F32, BF16, BF16, BF16)
    return _pcall(
        body, name=name, grid=(bsz, t // TB),
        in_specs=[tile(D_MODEL)] + _m_specs() + _w_specs(wa.shape, wb.shape, wc.shape, wo.shape),
        out_specs=[tile(n) for n in widths], out_shape=[_sds((bsz, t, n), d) for n, d in zip(widths, dts)],
        compiler_params=_cparams(("parallel", "parallel")),
    )(dout, ya, yb, yc, proj, proj, proj, wa, wb, wc, wo)


def _m_wgrad_call(ya, yb, yc, merged, dps, dout, name):
    bsz, t, _ = dout.shape

    def body(ya_ref, yb_ref, yc_ref, mg_ref, dp_ref, do_ref, dwa_ref, dwb_ref, dwc_ref, dwo_ref):
        @pl.when((pl.program_id(0) == 0) & (pl.program_id(1) == 0))
        def _():
            for r in (dwa_ref, dwb_ref, dwc_ref, dwo_ref):
                r[...] = jnp.zeros_like(r)

        dp = dp_ref[0]
        dwa_ref[...] += _tn(ya_ref[0], dp[:, :D_MODEL])
        dwb_ref[...] += _tn(yb_ref[0], dp[:, D_MODEL:2 * D_MODEL])
        dwc_ref[...] += _tn(yc_ref[0], dp[:, 2 * D_MODEL:])
        dwo_ref[...] += _tn(mg_ref[0], do_ref[0])

    tile = lambda n: pl.BlockSpec((1, TB, n), lambda b, i: (b, i, 0))
    shapes = ((CONV_W, D_MODEL), (D_MODEL, D_MODEL), (SG_W, D_MODEL), (D_MODEL, D_MODEL))
    return _pcall(
        body, name=name, grid=(bsz, t // TB),
        in_specs=[tile(CONV_W), tile(D_MODEL), tile(SG_W), tile(D_MODEL), tile(3 * D_MODEL), tile(D_MODEL)],
        out_specs=_w_specs(*shapes), out_shape=[_sds(s, F32) for s in shapes],
        compiler_params=_cparams(("arbitrary", "arbitrary")),
    )(ya, yb, yc, merged, dps, dout)


def _dn_specs(nc, rev):
    ti = (lambda i: nc - 1 - i) if rev else (lambda i: i)
    return [pl.BlockSpec((1, CHUNK, B_W), lambda b, i: (b, ti(i), 0)),
            pl.BlockSpec((1, CHUNK, DK), lambda b, i: (b, ti(i), BA_OFF // DK)),
            pl.BlockSpec((SHORT_K, 3 * D_MODEL), lambda b, i: (0, 0))] + [pl.BlockSpec((1, DK), lambda b, i: (0, 0))] * 3


def _dn_saved_specs(nc, rev):
    ti = (lambda i: nc - 1 - i) if rev else (lambda i: i)
    return [pl.BlockSpec((1, CHUNK, D_MODEL), lambda b, i: (b, ti(i), 0)),
            pl.BlockSpec((1, CHUNK, 3 * D_MODEL), lambda b, i: (b, ti(i), 0)),
            pl.BlockSpec((1, HEADS, 1, DK, DK), lambda b, i: (b, 0, ti(i), 0, 0)),
            pl.BlockSpec((1, 1, CHUNK, HEADS * CHUNK), lambda b, i: (b, ti(i), 0, 0)),
            pl.BlockSpec((1, CHUNK, 2 * D_MODEL), lambda b, i: (b, ti(i), 0)),
            pl.BlockSpec((1, CHUNK, D_MODEL), lambda b, i: (b, ti(i), 0))]


def _dn_fwd_call(proj, wconv, alog, dtb, og, name):
    bsz, t, _ = proj.shape
    nc = t // CHUNK

    def body(x_ref, ba_ref, w_ref, al_ref, dt_ref, og_ref, halo_ref, y_ref, o_ref, pre_ref, st_ref, t_ref, uw_ref, vn_ref,
             s_scr):
        i = pl.program_id(1)

        @pl.when(i == 0)
        def _():
            s_scr[...] = jnp.zeros_like(s_scr)

        keep = jnp.where(i > 0, 1.0, 0.0)
        bsig, gfull, _, _ = _dn_gates(ba_ref[0], al_ref[...], dt_ref[...])
        items = []
        for h in range(HEADS):
            pres = []
            for n in range(3):
                cs = slice(n * D_MODEL + h * DK, n * D_MODEL + (h + 1) * DK)
                pres.append(_short_conv(x_ref[0, :, cs], halo_ref[0, :, cs] * keep, w_ref[:, cs]))
                pre_ref[0, :, cs] = pres[-1]
            items.append(_dn_item(*pres, bsig[:, h:h + 1], gfull[:, HEADS + h:HEADS + h + 1]))
        ss = [s_scr[h] for h in range(HEADS)]
        for h in range(HEADS):
            st_ref[0, h, 0] = ss[h]
        outs, ss, vnew = _dn_fwd_chunk(items, ss)
        og_v = og_ref[...]
        for h, d in enumerate(items):
            s_scr[h] = ss[h]
            hs = slice(h * DK, (h + 1) * DK)
            z = x_ref[0, :, 3 * D_MODEL + h * DK:3 * D_MODEL + (h + 1) * DK]
            o_ref[0, :, hs] = outs[h]
            y_ref[0, :, hs] = _dn_out(outs[h], z, _sig(z), og_v)[0].astype(BF16)
            t_ref[0, 0, :, h * CHUNK:(h + 1) * CHUNK] = d["t"].astype(BF16)
            uw_ref[0, :, 2 * h * DK:(2 * h + 1) * DK] = d["u"].astype(BF16)
            uw_ref[0, :, (2 * h + 1) * DK:2 * (h + 1) * DK] = d["w"].astype(BF16)
            vn_ref[0, :, hs] = vnew[h].astype(BF16)

    halo = pl.BlockSpec((1, HALO_B, 3 * D_MODEL), lambda b, i: (b, jnp.maximum(i * (CHUNK // HALO_B) - 1, 0), 0))
    return _pcall(
        body, name=name, grid=(bsz, nc), in_specs=_dn_specs(nc, False) + [halo],
        out_specs=[pl.BlockSpec((1, CHUNK, D_MODEL), lambda b, i: (b, i, 0))] + _dn_saved_specs(nc, False),
        out_shape=[_sds((bsz, t, D_MODEL), BF16), _sds((bsz, t, D_MODEL), F32), _sds((bsz, t, 3 * D_MODEL), F32),
                   _sds((bsz, HEADS, nc, DK, DK), F32), _sds((bsz, nc, CHUNK, HEADS * CHUNK), BF16),
                   _sds((bsz, t, 2 * D_MODEL), BF16), _sds((bsz, t, D_MODEL), BF16)],
        scratch_shapes=[pltpu.VMEM((HEADS, DK, DK), F32)], compiler_params=_cparams(("arbitrary", "arbitrary")),
    )(proj, proj, wconv, alog, dtb, og, proj)


def _dn_bwd_call(dy, saved, proj, wconv, alog, dtb, og, name):
    bsz, t, _ = proj.shape
    nc = t // CHUNK

    def body(dy_ref, o_ref, pre_ref, st_ref, t_ref, uw_ref, vn_ref, x_ref, ba_ref, w_ref, al_ref, dt_ref, og_ref,
             dx_ref, dba_ref, dwc_ref, dsm_ref, ds_scr, dpre_scr):
        ip = pl.program_id(1)

        @pl.when((pl.program_id(0) == 0) & (ip == 0))
        def _():
            dwc_ref[...] = jnp.zeros_like(dwc_ref)
            dsm_ref[...] = jnp.zeros_like(dsm_ref)

        @pl.when(ip == 0)
        def _():
            ds_scr[...] = jnp.zeros_like(ds_scr)
            dpre_scr[...] = jnp.zeros_like(dpre_scr)

        bsig, gfull, xg, ea8 = _dn_gates(ba_ref[0], al_ref[...], dt_ref[...])
        og_v = og_ref[...]
        items, dog = [], jnp.zeros((1, DK), F32)
        for h in range(HEADS):
            hs = slice(h * DK, (h + 1) * DK)
            zs = slice(3 * D_MODEL + h * DK, 3 * D_MODEL + (h + 1) * DK)
            d = _dn_item(*(pre_ref[0, :, n * D_MODEL + h * DK:n * D_MODEL + (h + 1) * DK] for n in range(3)),
                         bsig[:, h:h + 1], gfull[:, HEADS + h:HEADS + h + 1])
            z, dyv = x_ref[0, :, zs], dy_ref[0, :, hs]
            sz = _sig(z)
            _, xh, r = _dn_out(o_ref[0, :, hs], z, sz, og_v)
            dx_ref[0, :, zs] = (dyv * xh * og_v * (sz * (1.0 + z * (1.0 - sz)))).astype(BF16)
            don = dyv * (z * sz)
            dog = dog + _sum0(don * xh)
            d.update(do=_rms_bwd(don * og_v, xh, r), t=t_ref[0, 0, :, h * CHUNK:(h + 1) * CHUNK],
                     u=uw_ref[0, :, 2 * h * DK:(2 * h + 1) * DK], w=uw_ref[0, :, (2 * h + 1) * DK:2 * (h + 1) * DK],
                     vnew=vn_ref[0, :, hs])
            items.append(d)
        grads, dss = _dn_bwd_chunk(items, [st_ref[0, h, 0] for h in range(HEADS)], [ds_scr[h] for h in range(HEADS)])
        lanes = lax.broadcasted_iota(jnp.int32, (CHUNK, DK), 1)
        lane8 = lax.broadcasted_iota(jnp.int32, (8, DK), 1)
        row8 = lax.broadcasted_iota(jnp.int32, (8, DK), 0)
        dgdx = -ea8 * _sig(xg)
        dba = jnp.zeros((CHUNK, DK), F32)
        dsm = jnp.where(row8 == 2, dog, 0.0)
        for h, (d, (dq, dk, dv, dbeta, dg)) in enumerate(zip(items, grads)):
            ds_scr[h] = dss[h]
            qc, kc, nq, nk = d["qc"], d["kc"], d["nq"], d["nk"]
            dacts = ((DK ** -0.5) * nq * (dq - qc * (nq * nq * _sum1(dq * qc))),
                     nk * (dk - kc * (nk * nk * _sum1(dk * kc))), dv)
            dal = dg * dgdx[:, HEADS + h:HEADS + h + 1]
            dba = dba + jnp.where(lanes == h, dbeta * d["b"] * (1.0 - d["b"]), 0.0) + jnp.where(lanes == HEADS + h, dal, 0.0)
            dsm = dsm + (jnp.where((row8 == 0) & (lane8 == h), _sum0(dg * d["g"]), 0.0)
                         + jnp.where((row8 == 1) & (lane8 == h), _sum0(dal), 0.0))
            for n in range(3):
                cs = slice(n * D_MODEL + h * DK, n * D_MODEL + (h + 1) * DK)
                pre, raw, w = pre_ref[0, :, cs], x_ref[0, :, cs], w_ref[:, cs]
                sp = _sig(pre)
                dpre = dacts[n] * (sp * (1.0 + pre * (1.0 - sp)))
                dext = jnp.concatenate([dpre, dpre_scr[:, cs]], axis=0)
                draw = dext[3:3 + CHUNK] * w[0:1]
                rows = [_sum0(dext[3:3 + CHUNK] * raw)]
                for j in range(1, SHORT_K):
                    shifted = dext[3 - j:3 - j + CHUNK]
                    draw = draw + shifted * w[j:j + 1]
                    rows.append(_sum0(shifted * raw))
                dpre_scr[:, cs] = dpre[:HALO_B]
                dx_ref[0, :, cs] = draw.astype(BF16)
                dwc_ref[:, cs] += _rows8(*rows)
        dba_ref[0] = dba.astype(BF16)
        dsm_ref[...] += dsm

    rrow = lambda n: pl.BlockSpec((1, CHUNK, n), lambda b, i: (b, nc - 1 - i, 0))
    return _pcall(
        body, name=name, grid=(bsz, nc), in_specs=[rrow(D_MODEL)] + _dn_saved_specs(nc, True) + _dn_specs(nc, True),
        out_specs=[rrow(B_W), rrow(DK), pl.BlockSpec((8, 3 * D_MODEL), lambda b, i: (0, 0)),
                   pl.BlockSpec((8, DK), lambda b, i: (0, 0))],
        out_shape=[_sds((bsz, t, B_W), BF16), _sds((bsz, t, DK), BF16), _sds((8, 3 * D_MODEL), F32), _sds((8, DK), F32)],
        scratch_shapes=[pltpu.VMEM((HEADS, DK, DK), F32), pltpu.VMEM((HALO_B, 3 * D_MODEL), F32)],
        compiler_params=_cparams(("arbitrary", "arbitrary")),
    )(dy, *saved, proj, proj, wconv, alog, dtb, og)


def _adamw_call(w, g, m, v, name):
    shape = w.shape
    view = (math.prod(shape[:-2]),) + shape[-2:] if len(shape) >= 2 else (1, 1) + shape
    lead, rows, cols = view
    tr, tl = (128, 1) if rows % 128 == 0 else (rows, min(lead, 64))
    c1, c2 = 1.0 - ADAM_B1 ** ADAM_STEP, 1.0 - ADAM_B2 ** ADAM_STEP

    def body(w_ref, g_ref, m_ref, v_ref, d_ref, nm_ref, nv_ref):
        gv = g_ref[...]
        nm = ADAM_B1 * m_ref[...] + (1.0 - ADAM_B1) * gv
        nv = ADAM_B2 * v_ref[...] + (1.0 - ADAM_B2) * (gv * gv)
        d_ref[...] = -ADAM_LR * ((nm / c1) / (jnp.sqrt(nv / c2) + ADAM_EPS) + ADAM_WD * w_ref[...])
        nm_ref[...] = nm
        nv_ref[...] = nv

    blk = pl.BlockSpec((tl, tr, cols), lambda l, i: (l, i, 0))
    outs = _pcall(
        body, name=name, grid=(pl.cdiv(lead, tl), rows // tr), in_specs=[blk] * 4, out_specs=[blk] * 3,
        out_shape=[_sds(view, F32)] * 3, compiler_params=_cparams(("parallel", "parallel")),
    )(*(a.reshape(view) for a in (w, g, m, v)))
    return tuple(o.reshape(shape) for o in outs)


def _row_tile(rows, cap=512):
    for tr in range(cap, 15, -16):
        if rows % tr == 0:
            return tr
    return rows


def _add_pair_call(by_chip, recv, where, name):
    _, n, rows, cols = by_chip.shape
    tr = _row_tile(rows, 256)

    def body(where_ref, a_ref, b_ref, sb_ref, own_ref):
        s = a_ref[0, 0] + b_ref[0]
        sb_ref[0] = s.astype(BF16)

        @pl.when(pl.program_id(1) == where_ref[1])
        def _():
            own_ref[...] = s

    grid_spec = pltpu.PrefetchScalarGridSpec(
        num_scalar_prefetch=1, grid=(rows // tr, n),
        in_specs=[pl.BlockSpec((1, 1, tr, cols), lambda i, j, wr: (wr[0], j, i, 0)),
                  pl.BlockSpec((1, tr, cols), lambda i, j, wr: (j, i, 0))],
        out_specs=[pl.BlockSpec((1, tr, cols), lambda i, j, wr: (j, i, 0)),
                   pl.BlockSpec((tr, cols), lambda i, j, wr: (i, 0))])
    return _pcall(
        body, name=name, grid_spec=grid_spec, out_shape=[_sds((n, rows, cols), BF16), _sds((rows, cols), F32)],
        compiler_params=_cparams(("parallel", "arbitrary")),
    )(where, by_chip, recv)


def _add_recv_call(own, recv, where, name):
    rows, cols = own.shape
    tr = _row_tile(rows, 256)

    def body(where_ref, o_ref, r_ref, s_ref):
        s_ref[0] = ((o_ref[...] + r_ref[0].astype(F32)) + r_ref[1].astype(F32)) + r_ref[2].astype(F32)

    grid_spec = pltpu.PrefetchScalarGridSpec(
        num_scalar_prefetch=1, grid=(rows // tr,),
        in_specs=[pl.BlockSpec((tr, cols), lambda i, wr: (i, 0)), pl.BlockSpec((3, tr, cols), lambda i, wr: (0, i, 0))],
        out_specs=pl.BlockSpec((1, tr, cols), lambda i, wr: (wr[0], i, 0)))
    return _pcall(
        body, name=name, grid_spec=grid_spec, out_shape=_sds((2, rows, cols), F32),
        compiler_params=_cparams(("parallel",)),
    )(where, own, recv)


def _axes():
    return lax.axis_index("x"), lax.axis_index("y"), lax.axis_index("c")


def _chip_peers(x, y):
    return [(x, 1 - y), (1 - x, y), (1 - x, 1 - y)]


_ANY = pl.BlockSpec(memory_space=pl.ANY)
_VMEM = pl.BlockSpec(memory_space=pltpu.VMEM)


def _remote(src, dst, send_sems, recv_sems, k, dev):
    return pltpu.make_async_remote_copy(src_ref=src, dst_ref=dst, send_sem=send_sems.at[k], recv_sem=recv_sems.at[k],
                                        device_id=dev, device_id_type=MESH)


def _gather_weights_call(shards):
    n = len(shards)
    halves = [s.shape[0] // 2 for s in shards]
    quarter = halves[0] // 2
    base = [0] + [8 + 6 * (a - 1) for a in range(1, n)]

    def body(*refs):
        x_refs, o_refs, (send_sems, recv_sems) = refs[:n], refs[n:2 * n], refs[2 * n:]
        x, y, c = _axes()
        chip, sib, peers = 2 * x + y, (x, y, 1 - c), _chip_peers(x, y)
        pchip = [2 * px + py for px, py in peers]

        def half(a, cc):
            return pl.ds(cc * halves[a], halves[a])

        def part(cc, q):
            return pl.ds(cc * halves[0] + q * quarter, quarter)

        def copy(ref, k, dev):
            return _remote(ref, ref, send_sems, recv_sems, k, dev)

        sends = [_remote(x_refs[a].at[half(a, c)], o_refs[a].at[chip, half(a, c)], send_sems, recv_sems, base[a] + k,
                         (*peers[k], c)) for k in range(3) for a in range(n) if a > 0 or k < 2]
        for cp in sends:
            cp.start()
        for k in range(2):
            copy(o_refs[0].at[pchip[k], half(0, c)], k, (*peers[k], c)).wait_recv()
            sends.append(copy(o_refs[0].at[pchip[k], part(c, k)], 2 + k, (*peers[1 - k], c)))
            sends.append(copy(o_refs[0].at[pchip[k], half(0, c)], 4 + k, sib))
            sends[-2].start()
            sends[-1].start()
        for k in range(3):
            for a in range(1, n):
                land = o_refs[a].at[pchip[k], half(a, c)]
                copy(land, base[a] + k, (*peers[k], c)).wait_recv()
                sends.append(copy(land, base[a] + 3 + k, sib))
                sends[-1].start()
        for q in range(2):
            land = o_refs[0].at[pchip[2], part(c, q)]
            copy(land, 2 + q, (*peers[1 - q], c)).wait_recv()
            sends.append(copy(land, 6 + q, sib))
            sends[-1].start()
        for k in range(2):
            copy(o_refs[0].at[pchip[k], half(0, 1 - c)], 4 + k, sib).wait_recv()
            copy(o_refs[0].at[pchip[2], part(1 - c, k)], 6 + k, sib).wait_recv()
        for k in range(3):
            for a in range(1, n):
                copy(o_refs[a].at[pchip[k], half(a, 1 - c)], base[a] + 3 + k, sib).wait_recv()
        for cp in sends:
            cp.wait_send()

    n_sems = 8 + 6 * (n - 1)
    outs = _pcall(
        body, name="gather_weights", in_specs=[_ANY] * n, out_specs=[_ANY] * n,
        out_shape=[_sds((4,) + s.shape, s.dtype) for s in shards],
        scratch_shapes=[pltpu.SemaphoreType.DMA((n_sems,)), pltpu.SemaphoreType.DMA((n_sems,))],
    )(*shards)
    chip = 2 * lax.axis_index("x") + lax.axis_index("y")
    return [lax.dynamic_update_slice_in_dim(o, s[None], chip, axis=0) for o, s in zip(outs, shards)]


def _pair_partials_call(by_chip, name):
    n = len(by_chip)

    def body(*refs):
        v_refs, o_refs, (send_sems, recv_sems) = refs[:n], refs[n:2 * n], refs[2 * n:]
        x, y, c = _axes()
        cps = [_remote(v_refs[a].at[1 - c], o_refs[a], send_sems, recv_sems, a, (x, y, 1 - c)) for a in range(n)]
        for cp in cps:
            cp.start()
        for cp in cps:
            cp.wait()

    return _pcall(
        body, name=name, in_specs=[_ANY] * n, out_specs=[_ANY] * n,
        out_shape=[_sds(v.shape[1:], v.dtype) for v in by_chip],
        scratch_shapes=[pltpu.SemaphoreType.DMA((n,)), pltpu.SemaphoreType.DMA((n,))],
    )(*by_chip)


def _pair_result_call(fins, name):
    n = len(fins)

    def body(*refs):
        v_refs, o_refs, (send_sems, recv_sems) = refs[:n], refs[n:2 * n], refs[2 * n:]
        x, y, c = _axes()
        cps = [_remote(v_refs[a].at[c], o_refs[a].at[c], send_sems, recv_sems, a, (x, y, 1 - c)) for a in range(n)]
        for cp in cps:
            cp.start()
        for a in range(n):
            cps[a].wait_send()
            _remote(v_refs[a].at[c], o_refs[a].at[1 - c], send_sems, recv_sems, a, (x, y, 1 - c)).wait_recv()

    return _pcall(
        body, name=name, in_specs=[_ANY] * n, out_specs=[_ANY] * n,
        out_shape=[_sds(v.shape, v.dtype) for v in fins], input_output_aliases={a: a for a in range(n)},
        scratch_shapes=[pltpu.SemaphoreType.DMA((n,)), pltpu.SemaphoreType.DMA((n,))],
    )(*fins)


def _allreduce_small_call(v):
    rows, cols = v.shape

    def body(v_ref, o_ref, buf, send_sems, recv_sems):
        x, y, c = _axes()
        chip, peers = 2 * x + y, _chip_peers(x, y)
        pair = _remote(v_ref, buf.at[0], send_sems, recv_sems, 0, (x, y, 1 - c))
        pair.start()
        pair.wait()
        buf[0] = v_ref[...] + buf[0]
        cps = [_remote(buf.at[0], buf.at[1 + k], send_sems, recv_sems, 1 + k, (*peers[k], c)) for k in range(3)]
        for cp in cps:
            cp.start()
        for cp in cps:
            cp.wait()
        acc = buf[chip]
        for d in range(1, 4):
            acc = acc + buf[lax.bitwise_xor(chip, d)]
        o_ref[...] = acc

    return _pcall(
        body, name="allreduce_small", in_specs=[_VMEM], out_specs=_VMEM, out_shape=_sds(v.shape, F32),
        scratch_shapes=[pltpu.VMEM((4, rows, cols), F32), pltpu.SemaphoreType.DMA((4,)), pltpu.SemaphoreType.DMA((4,))],
        compiler_params=_cparams(),
    )(v)


def _pack_rows(arrays, dtype, total_rows=None):
    parts = []
    for a in arrays:
        flat = a.astype(dtype).reshape(-1)
        parts.append(jnp.pad(flat, (0, -flat.shape[0] % D_MODEL)).reshape(-1, D_MODEL))
    out = jnp.concatenate(parts, axis=0)
    total_rows = total_rows or out.shape[0] + (-out.shape[0] % 8)
    return jnp.pad(out, ((0, total_rows - out.shape[0]), (0, 0)))


def _unpack_rows(packed, shapes):
    out, r = [], 0
    for s in shapes:
        n = math.prod(s)
        nr = -(-n // D_MODEL)
        out.append(packed[r:r + nr].reshape(-1)[:n].reshape(s))
        r += nr
    return out


def _to_r(w_t):
    pad = jnp.zeros((N_R - N_ORIG,) + w_t.shape[1:], w_t.dtype)
    return jnp.concatenate([w_t[A_ORIG_W:BA_ORIG], w_t[:A_ORIG_W], w_t[BA_ORIG + 2 * HEADS:],
                            w_t[BA_ORIG:BA_ORIG + 2 * HEADS], pad], axis=0)


_BIG = ("w_in", "a_proj", "b_proj", "c_proj", "w_out")
_SHARD_AXIS = {"w_in": 2, "a_proj": 2, "b_proj": 1, "c_proj": 2, "w_out": 1, "a_dw": 2, "b_conv": 2}
_BIG_AXIS = _SHARD_AXIS


def _join_chips(g, axis):
    g = jnp.moveaxis(g, 0, axis)
    return g.reshape(g.shape[:axis] + (4 * g.shape[axis + 1],) + g.shape[axis + 2:])


def _split_chips(a, axis):
    n = a.shape[axis] // 4
    by_chip = jnp.moveaxis(a.reshape(a.shape[:axis] + (4, n) + a.shape[axis + 1:]), axis, 0)
    return jnp.moveaxis(by_chip.reshape(4, 2, by_chip.shape[1] // 2, by_chip.shape[2]), 1, 0)
_ORDER = ("norm_g", "w_in", "a_dw", "a_dw_b", "a_ln_g", "a_ln_b", "a_proj", "b_conv", "b_a_log", "b_dt_bias",
          "b_onorm_g", "b_proj", "c_ln_g", "c_ln_b", "c_ws", "c_bs", "c_proj", "w_out", "final_g")


def _reduce_begin(layer_grads, where, name):
    by_chip = [_split_chips(layer_grads[k], _SHARD_AXIS[k] - 1) for k in _BIG]
    theirs = _pair_partials_call(by_chip, name + "pair_partials")
    return [_add_pair_call(b, r, where, name + "pair_sum_" + k) for k, b, r in zip(_BIG, by_chip, theirs)]


def _reduce_end(sums, got, where, name):
    fins = [_add_recv_call(s[1], r, where, name + "chip_sum_" + k) for k, s, r in zip(_BIG, sums, got)]
    return [o.reshape((-1,) + o.shape[2:]) for o in _pair_result_call(fins, name + "pair_result")]


def _local_step(x, tgt, w, where=None):
    bsz, t, _ = x.shape
    m = bsz * t
    depth = w["norm_g"].shape[0]
    row = lambda v: v.reshape(1, -1)
    w_rt = _to_r(w["w_in_t"])
    saved, xl = [], x
    for l in range(depth):
        n = f"l{l}_"
        par = dict(
            adw=jnp.pad(w["a_dw"][l], ((0, 1), (0, 0))), adb=row(w["a_dw_b"][l]), alg=row(w["a_ln_g"][l]),
            alb=row(w["a_ln_b"][l]), bconv=w["b_conv"][l],
            alog=jnp.pad(row(w["b_a_log"][l]), ((0, 0), (HEADS, DK - 2 * HEADS))),
            dtb=jnp.pad(row(w["b_dt_bias"][l]), ((0, 0), (HEADS, DK - 2 * HEADS))), og=row(w["b_onorm_g"][l]),
            clg=row(w["c_ln_g"][l]), clb=row(w["c_ln_b"][l]), cws=w["c_ws"][l],
            cbias=jnp.repeat(w["c_bs"][l].T, SG_C, axis=1), ng=row(w["norm_g"][l]),
            wa=w["a_proj"][l], wb=w["b_proj"][l], wc=w["c_proj"][l], wo=w["w_out"][l], wrt=w_rt[:, l], wr=w_rt[:, l].T)
        h = _rms_fwd_call(xl.reshape(m, D_MODEL), par["ng"], n + "norm")
        proj = _matmul(h, par["wr"], F32, 2048, 1152, 1024, n + "in_proj").reshape(bsz, t, N_R)
        ya, conv = _a_fwd_call(proj, par["adw"], par["adb"], par["alg"], par["alb"], n + "conv_fwd")
        yb, *dn_saved = _dn_fwd_call(proj, par["bconv"], par["alog"], par["dtb"], par["og"], n + "delta_fwd")
        yc = _c_fwd_call(proj, par["clg"], par["clb"], par["cws"], par["cbias"], n + "gmlp_fwd")
        x_next = _m_fwd_call(ya, yb, yc, proj, xl, par["wa"], par["wb"], par["wc"], par["wo"], n + "merge_fwd")
        saved.append((par, xl, h, proj, ya, yb, yc, conv, dn_saved))
        xl = x_next
    dout, dfg, loss = _loss_call(xl.reshape(m, D_MODEL), tgt.reshape(m, D_MODEL), row(w["final_g"]))
    dout = dout.reshape(bsz, t, D_MODEL)
    g = {k: [None] * depth for k in _ORDER if k != "final_g"}
    pending = None
    for l in reversed(range(depth)):
        n = f"l{l}_"
        par, xl, h, proj, ya, yb, yc, conv, dn_saved = saved[l]
        dya, dyb, dyc, dgate, merged, dps = _m_bwd_call(dout, ya, yb, yc, proj, par["wa"], par["wb"], par["wc"],
                                                        par["wo"], n + "merge_bwd")
        g["a_proj"][l], g["b_proj"][l], g["c_proj"][l], g["w_out"][l] = _m_wgrad_call(ya, yb, yc, merged, dps, dout,
                                                                                      n + "merge_wgrad")
        travel = [s[0] for s in pending[1]] if pending else []
        da, dadw, dasm, *got = _a_bwd_call(dya, conv, proj, par["adw"], par["alg"], par["alb"], n + "conv_bwd", travel)
        if pending:
            for k, r in zip(_BIG, _reduce_end(pending[1], got, where, f"l{pending[0]}_")):
                g[k][pending[0]] = r
            pending = None
        db, dba, dbconv, dbsm = _dn_bwd_call(dyb, dn_saved, proj, par["bconv"], par["alog"], par["dtb"], par["og"],
                                             n + "delta_bwd")
        dc, dcws, dcbs, dcsm = _c_bwd_call(dyc, proj, par["clg"], par["clb"], par["cws"], par["cbias"], n + "gmlp_bwd")
        segs = [s.reshape(m, s.shape[-1]) for s in (db, da, dc, dgate, dba)]
        ht = h.T
        dwb, dwa, dwc, dwg, dwba = [_matmul(ht, s, F32, 1024, 1024 if s.shape[1] % 1024 == 0 else 768, 2048,
                                            n + "in_proj_dw_" + tag) for s, tag in zip(segs, "bacgs")]
        g["w_in"][l] = jnp.concatenate([dwa, dwb, dwba[:, :2 * HEADS], dwc, dwg], axis=1)
        if where is None:
            dh = _matmul_ksegs(segs, par["wrt"], 2048, 1024, 512, n + "in_proj_dx")
        else:
            sums = _reduce_begin({k: g[k][l] for k in _BIG}, where, n)
            if l > 0:
                pending = (l, sums)
                dh = _matmul_ksegs(segs, par["wrt"], 2048, 1024, 512, n + "in_proj_dx")
            else:
                dh, got = _matmul_ksegs(segs, par["wrt"], 2048, 1024, 512, n + "in_proj_dx", [s[0] for s in sums])
                for k, r in zip(_BIG, _reduce_end(sums, got, where, n)):
                    g[k][l] = r
        dx, dng = _rms_bwd_call(xl.reshape(m, D_MODEL), dh, par["ng"], dout.reshape(m, D_MODEL), n + "norm_bwd")
        dout = dx.reshape(bsz, t, D_MODEL)
        g["norm_g"][l] = dng[0]
        g["a_dw"][l], g["a_dw_b"][l], g["a_ln_g"][l], g["a_ln_b"][l] = dadw[:CONV_K], dasm[0], dasm[1], dasm[2]
        g["b_conv"][l], g["b_a_log"][l], g["b_dt_bias"][l] = dbconv[:SHORT_K], dbsm[0, :HEADS], dbsm[1, :HEADS]
        g["b_onorm_g"][l] = dbsm[2]
        g["c_ln_g"][l], g["c_ln_b"][l], g["c_ws"][l] = dcsm[0], dcsm[1], dcws
        g["c_bs"][l] = dcbs.reshape(SG_C, SG_G, SG_C).sum(-1).T
    grads = {k: jnp.stack(v) for k, v in g.items()}
    grads["final_g"] = dfg[0]
    return loss[0, 0], dout, grads


def kernel(x, norm_g, w_in, a_dw, a_dw_b, a_ln_g, a_ln_b, a_proj, b_conv, b_a_log, b_dt_bias, b_onorm_g, b_proj, c_ln_g, c_ln_b, c_ws, c_bs, c_proj, w_out, final_g, loss_target, m_norm_g, m_w_in, m_a_dw, m_a_dw_b, m_a_ln_g, m_a_ln_b, m_a_proj, m_b_conv, m_b_a_log, m_b_dt_bias, m_b_onorm_g, m_b_proj, m_c_ln_g, m_c_ln_b, m_c_ws, m_c_bs, m_c_proj, m_w_out, m_final_g, v_norm_g, v_w_in, v_a_dw, v_a_dw_b, v_a_ln_g, v_a_ln_b, v_a_proj, v_b_conv, v_b_a_log, v_b_dt_bias, v_b_onorm_g, v_b_proj, v_c_ln_g, v_c_ln_b, v_c_ws, v_c_bs, v_c_proj, v_w_out, v_final_g):
    given = dict(locals())
    ws = {k: given[k] for k in _ORDER}
    xi, yi, ci = _axes()
    chip = 2 * xi + yi
    where = jnp.stack([ci, chip]).astype(jnp.int32)

    others = _BIG[1:] + ("a_dw", "b_conv")
    gathered = _gather_weights_call([jnp.transpose(w_in, (2, 0, 1)).astype(BF16)] + [ws[k].astype(BF16) for k in _BIG[1:]]
                                    + [a_dw, b_conv])
    full = {k: v for k, v in ws.items() if k != "w_in"}
    full["w_in_t"] = gathered[0].reshape((N_ORIG,) + gathered[0].shape[2:])
    for k, g in zip(others, gathered[1:]):
        full[k] = _join_chips(g, _SHARD_AXIS[k])

    loss, grad_x, grads = _local_step(x, loss_target, full, where)
    out_g = {k: grads[k] for k in _BIG}

    rest = [k for k in _ORDER if k not in _BIG]
    rest_shapes = [grads[k].shape for k in rest] + [(1,)]
    summed = _unpack_rows(_allreduce_small_call(_pack_rows([grads[k] for k in rest] + [loss.reshape(1)], F32)),
                          rest_shapes)
    out_g.update(zip(rest, summed[:-1]))
    out_g["a_dw"] = lax.dynamic_slice_in_dim(out_g["a_dw"], chip * a_dw.shape[2], a_dw.shape[2], axis=2)
    out_g["b_conv"] = lax.dynamic_slice_in_dim(out_g["b_conv"], chip * b_conv.shape[2], b_conv.shape[2], axis=2)

    def adam(k):
        operands = (ws[k], out_g[k], given["m_" + k], given["v_" + k])
        if k != "w_in":
            return _adamw_call(*operands, "adamw_" + k)
        w_t, g_t, m_t, v_t = (jnp.transpose(a, (2, 0, 1)) for a in operands)
        out_g[k] = jnp.transpose(g_t, (1, 2, 0))
        return tuple(jnp.transpose(o, (1, 2, 0)) for o in _adamw_call(w_t, g_t, m_t, v_t, "adamw_" + k))

    upd = {k: adam(k) for k in _ORDER}
    return (summed[-1][0], grad_x, *[out_g[k] for k in _ORDER], *[upd[k][0] for k in _ORDER],
            *[upd[k][1] for k in _ORDER], *[upd[k][2] for k in _ORDER])
```

```python
import functools
import math

import jax
import jax.numpy as jnp
from jax import lax
from jax.experimental import pallas as pl
from jax.experimental.pallas import tpu as pltpu

F32 = jnp.float32
BF16 = jnp.bfloat16
MESH = pl.DeviceIdType.MESH

D_MODEL = 1024
CONV_W = 512
CONV_K = 31
HEADS = 8
DK = 128
SHORT_K = 4
CHUNK = 64
SG_W = 512
SG_G = 4
SG_C = 128
EPS = 1e-6
N_ORIG = 10256
BA_ORIG = 5632
A_ORIG_W = 3 * CONV_W
B_W = 4 * D_MODEL
B_OFF, A_OFF, C_OFF, G_OFF, BA_OFF, N_R = 0, 4096, 5632, 7168, 10240, 10368
TB = 256
HALO_A = 32
HALO_B = 8
VMEM_LIMIT = 56 * 1024 * 1024
GATHER_ROWS = 2624
ADAM_LR, ADAM_B1, ADAM_B2, ADAM_EPS, ADAM_WD, ADAM_STEP = 0.001, 0.9, 0.999, 1e-08, 0.01, 10


def _pcall(body, **kw):
    return pl.pallas_call(body, **kw)


def _cparams(sem=None):
    kw = dict(vmem_limit_bytes=VMEM_LIMIT)
    if sem is not None:
        kw["dimension_semantics"] = sem
    return pltpu.CompilerParams(**kw)


def _sig(x):
    return jax.nn.sigmoid(x)


def _silu(x):
    return x * _sig(x)


def _dsilu(x):
    s = _sig(x)
    return s * (1.0 + x * (1.0 - s))


_GELU_C = math.sqrt(2.0 / math.pi)


def _gelu(x):
    return 0.5 * x * (1.0 + jnp.tanh(_GELU_C * (x + 0.044715 * x * x * x)))


def _dgelu(x):
    t = jnp.tanh(_GELU_C * (x + 0.044715 * x * x * x))
    return 0.5 * (1.0 + t) + 0.5 * x * (1.0 - t * t) * _GELU_C * (1.0 + 3 * 0.044715 * x * x)


def _softplus(x):
    return jnp.maximum(x, 0.0) + jnp.log1p(jnp.exp(-jnp.abs(x)))


def _dot(a, b, dims):
    return lax.dot_general(a.astype(BF16), b.astype(BF16), (dims, ((), ())), preferred_element_type=F32)


def _nn(a, b):
    return _dot(a, b, ((1,), (0,)))


def _nt(a, b):
    return _dot(a, b, ((1,), (1,)))


def _tn(a, b):
    return _dot(a, b, ((0,), (0,)))


def _tn_mxu(a, b):
    n = a.shape[1]
    eye = (lax.broadcasted_iota(jnp.int32, (n, n), 0) == lax.broadcasted_iota(jnp.int32, (n, n), 1)).astype(BF16)
    return _nn(_nt(eye, a), b)


def _mean(x):
    return jnp.mean(x, axis=-1, keepdims=True)


def _sum0(x):
    return jnp.sum(x, axis=0, keepdims=True)


def _sum1(x):
    return jnp.sum(x, axis=1, keepdims=True)


def _ln_fwd(x, g, b):
    xc = x - _mean(x)
    r = lax.rsqrt(_mean(xc * xc) + EPS)
    xh = xc * r
    return xh * g + b, xh, r


def _ln_bwd(dy, xh, r, g):
    dxh = dy * g
    return r * (dxh - _mean(dxh) - xh * _mean(dxh * xh)), _sum0(dy * xh), _sum0(dy)


def _rms_bwd(dxh, xh, r):
    return r * (dxh - xh * _mean(dxh * xh))


def _rows8(*rows):
    n = rows[0].shape[1]
    return jnp.concatenate(list(rows) + [jnp.zeros((8 - len(rows), n), F32)], axis=0)


def _windows(ext, tb):
    n = ext.shape[0] - 8
    shifted = {0: ext}

    def window(off):
        r = off % 8
        if r not in shifted:
            shifted[r] = ext[r:r + n]
        return shifted[r][off - r:off - r + tb]

    return window


def _a_fwd(val, glu, az, val_h, glu_h, w, bias, g, b):
    tb = val.shape[0]
    win = _windows(jnp.concatenate([val_h * _sig(glu_h), val * _sig(glu)], axis=0), tb)
    c = win(2) * w[0:1]
    for j in range(1, CONV_K):
        c = c + win(2 + j) * w[j:j + 1]
    c = c + bias
    ln, _, _ = _ln_fwd(c, g, b)
    return _silu(ln) * _silu(az), c


def _a_bwd(dy, val, glu, az, c, w, g, b, dc_next):
    tb = val.shape[0]
    ln, xh, r = _ln_fwd(c, g, b)
    sl, sz = _sig(ln), _sig(az)
    dln = dy * (az * sz) * (sl * (1.0 + ln * (1.0 - sl)))
    daz = dy * (ln * sl) * (sz * (1.0 + az * (1.0 - sz)))
    dc, dg, db = _ln_bwd(dln, xh, r, g)
    win = _windows(jnp.concatenate([dc, dc_next], axis=0), tb)
    sg = _sig(glu)
    a = val * sg
    da = win(30) * w[0:1]
    dw_rows = [_sum0(win(30) * a)]
    for j in range(1, CONV_K):
        shifted = win(30 - j)
        da = da + shifted * w[j:j + 1]
        dw_rows.append(_sum0(shifted * a))
    dw = jnp.concatenate(dw_rows + [jnp.zeros((1, CONV_W), F32)], axis=0)
    return da * sg, da * val * sg * (1.0 - sg), daz, dw, _sum0(dc), dg, db, dc[:HALO_A]


def _tril(ws):
    ii = lax.broadcasted_iota(jnp.int32, (SG_C, SG_C), 0)
    jj = lax.broadcasted_iota(jnp.int32, (SG_C, SG_C), 1)
    return [jnp.where(jj <= ii, ws[gi], 0.0) for gi in range(SG_G)], jj <= ii


def _c_mix(wt, vs, bias_full):
    tb = vs.shape[0]
    rows = []
    for n in range(tb // SG_C):
        blks = [_nn(wt[gi], vs[n * SG_C:(n + 1) * SG_C, gi * SG_C:(gi + 1) * SG_C]) for gi in range(SG_G)]
        rows.append(jnp.concatenate(blks, axis=1) + bias_full)
    return jnp.concatenate(rows, axis=0)


def _c_fwd(cu, cv, cz, g, b, ws, bias_full):
    wt, _ = _tril(ws)
    vs, xh, r = _ln_fwd(_gelu(cv), g, b)
    mixed = _c_mix(wt, vs, bias_full)
    return _gelu(cu) * mixed * _silu(cz), (wt, vs, xh, r, mixed)


def _c_bwd(dy, cu, cv, cz, g, b, ws, bias_full):
    tb = cu.shape[0]
    _, (wt, vs, xh, r, mixed) = _c_fwd(cu, cv, cz, g, b, ws, bias_full)
    _, low = _tril(ws)
    u, sz = _gelu(cu), _silu(cz)
    dcu = dy * mixed * sz * _dgelu(cu)
    dcz = dy * u * mixed * _dsilu(cz)
    dmixed = dy * u * sz
    dbs = jnp.zeros((SG_C, SG_W), F32)
    dws = [jnp.zeros((SG_C, SG_C), F32) for _ in range(SG_G)]
    rows = []
    for n in range(tb // SG_C):
        dm_n = dmixed[n * SG_C:(n + 1) * SG_C]
        dbs = dbs + dm_n
        blks = []
        for gi in range(SG_G):
            dm = dm_n[:, gi * SG_C:(gi + 1) * SG_C]
            dws[gi] = dws[gi] + _nt(dm, vs[n * SG_C:(n + 1) * SG_C, gi * SG_C:(gi + 1) * SG_C])
            blks.append(_tn(wt[gi], dm))
        rows.append(jnp.concatenate(blks, axis=1))
    dvs = jnp.concatenate(rows, axis=0)
    dgv, dg, db = _ln_bwd(dvs, xh, r, g)
    dws = [jnp.where(low, d, 0.0) for d in dws]
    return dcu, dgv * _dgelu(cv), dcz, dws, dbs, dg, db


def _m_fwd(ya, yb, yc, g0, g1, g2, wa, wb, wc):
    pa, pb, pc = _nn(ya, wa), _nn(yb, wb), _nn(yc, wc)
    s0, s1, s2 = _sig(g0), _sig(g1), _sig(g2)
    return s0 * pa + s1 * pb + s2 * pc, (pa, pb, pc, s0, s1, s2)


def _m_bwd(dout, ya, yb, yc, g0, g1, g2, wa, wb, wc, wo):
    merged, (pa, pb, pc, s0, s1, s2) = _m_fwd(ya, yb, yc, g0, g1, g2, wa, wb, wc)
    dm = _nt(dout, wo)
    dpa, dpb, dpc = dm * s0, dm * s1, dm * s2
    dgs = (dm * pa * s0 * (1.0 - s0), dm * pb * s1 * (1.0 - s1), dm * pc * s2 * (1.0 - s2))
    return (_nt(dpa, wa), _nt(dpb, wb), _nt(dpc, wc)), dgs, merged, (dpa, dpb, dpc)


def _chunk_masks(c):
    ii = lax.broadcasted_iota(jnp.int32, (c, c), 0)
    jj = lax.broadcasted_iota(jnp.int32, (c, c), 1)
    return ii, jj


def _dn_decay(items):
    ii, jj = _chunk_masks(CHUNK)
    incl, strict, eye = jj <= ii, jj < ii, ii == jj
    for d in items:
        g = d["g"]
        grow = _sum0(jnp.where(eye, g, 0.0))
        gc_col = _sum1(jnp.where(incl, grow, 0.0))
        gc_row = _sum0(jnp.where(ii <= jj, g, 0.0))
        gam_i = jnp.where(incl, jnp.exp(jnp.where(incl, gc_col - gc_row, 0.0)), 0.0)
        gl = _sum0(g)
        egc = jnp.exp(gc_col)
        d.update(gam_i=gam_i, gam_s=jnp.where(strict, gam_i, 0.0), egc=egc, ekd=jnp.exp(gl - gc_col), dl=jnp.exp(gl),
                 gdiff=gc_col - gc_row, qd=d["q"] * egc, rhs_w=d["k"] * (d["b"] * egc))
        d["kd"] = d["k"] * d["ekd"]
    return ii, jj, strict, eye


def _dn_solve(items, ii, jj, eye):
    off = ((ii >> 1) == (jj >> 1)) & ((ii & 1) != 0) & ((jj & 1) == 0)
    for d in items:
        a1 = jnp.where(off, d["a"], 0.0)
        d["t"] = jnp.where(eye, 1.0, 0.0) - a1
        d["m"] = d["a"] - _nn(a1, d["a"])
    b, sh = 2, 2
    while b < CHUNK:
        off = ((ii >> sh) == (jj >> sh)) & ((ii & b) != 0) & ((jj & b) == 0)
        for d in items:
            mo = jnp.where(off, d["m"], 0.0)
            if 2 * b < CHUNK:
                d["m"], d["t"] = d["m"] - _nn(mo, d["m"]), d["t"] - _nn(mo, d["t"])
            else:
                d["t"] = d["t"] - _nn(mo, d["t"])
        b, sh = 2 * b, sh + 1


def _dn_fwd_chunk(items, ss):
    ii, jj, strict, eye = _dn_decay(items)
    for d in items:
        d["a"] = d["b"] * _nt(d["k"], d["k"]) * d["gam_s"]
        d["qk"] = _nt(d["q"], d["k"]) * d["gam_i"]
    _dn_solve(items, ii, jj, eye)
    for d in items:
        d["u"], d["w"] = _nn(d["t"], d["v"] * d["b"]), _nn(d["t"], d["rhs_w"])
    ws = [_nn(d["w"], s) for d, s in zip(items, ss)]
    qs = [_nn(d["qd"], s) for d, s in zip(items, ss)]
    vnew = [d["u"] - w for d, w in zip(items, ws)]
    outs = [q + _nn(d["qk"], vn) for d, q, vn in zip(items, qs, vnew)]
    ss = [d["dl"] * s + _tn(d["kd"], vn) for d, s, vn in zip(items, ss, vnew)]
    return outs, ss, vnew


def _dn_bwd_chunk(items, ss, dss):
    ii, jj, strict, eye = _dn_decay(items)
    eye_b = eye.astype(BF16)
    for d in items:
        k, q = d["k"], d["q"]
        d["kk"] = _nt(k, k)
        d["a"] = d["b"] * d["kk"] * d["gam_s"]
        d["qk"] = _nt(q, k) * d["gam_i"]
        d["qkt"] = _nt(k, q) * jnp.where(ii <= jj, jnp.exp(jnp.where(ii <= jj, -d["gdiff"], 0.0)), 0.0)
    for d, s, ds2 in zip(items, ss, dss):
        d["dvnew"] = _nn(d["qkt"], d["do"]) + _nn(d["kd"], ds2)
        d["dqk"] = _nt(d["do"], d["vnew"])
        d["dqd"] = _nt(d["do"], s)
        d["dkd"] = _nt(d["vnew"], ds2)
    new_dss = []
    for d, s, ds2 in zip(items, ss, dss):
        d["dw"] = -_nt(d["dvnew"], s)
        new_dss.append(_tn_mxu(d["qd"], d["do"]) - _tn_mxu(d["w"], d["dvnew"]) + d["dl"] * ds2)
        d["ddl"] = _sum0(_sum1(s * ds2))
    for d in items:
        tt = _nt(eye_b, d["t"])
        d["drhs_u"], d["drhs_w"] = _nn(tt, d["dvnew"]), _nn(tt, d["dw"])
    for d in items:
        d["da"] = jnp.where(strict, -(_nt(d["drhs_u"], d["u"]) + _nt(d["drhs_w"], d["w"])), 0.0)
    outs = []
    for d in items:
        q, k, v, b, egc = d["q"], d["k"], d["v"], d["b"], d["egc"]
        drhs_u, drhs_w, da = d["drhs_u"], d["drhs_w"], d["da"]
        dbeta = _sum1(da * d["kk"] * d["gam_s"]) + _sum1(drhs_u * v) + _sum1(drhs_w * k) * egc
        dkk = da * b * d["gam_s"]
        e = da * d["a"] + d["dqk"] * d["qk"]
        s_kd = _sum1(d["dkd"] * d["kd"])
        dgc_col = _sum1(e) + _sum1(drhs_w * d["rhs_w"]) + _sum1(d["dqd"] * d["qd"]) - s_kd
        dgc_row = _sum0(jnp.where(eye, dgc_col, 0.0)) - _sum0(e)
        dg = _sum1(jnp.where(jj >= ii, dgc_row, 0.0)) + (_sum0(s_kd) + d["ddl"] * d["dl"])
        dqkg = d["dqk"] * d["gam_i"]
        dq = _nn(dqkg, k) + d["dqd"] * egc
        dk = _tn_mxu(dqkg, q) + _nn(dkk, k) + _tn_mxu(dkk, k) + drhs_w * (b * egc) + d["dkd"] * d["ekd"]
        outs.append((dq, dk, drhs_u * b, dbeta, dg))
    return outs, new_dss


def _short_conv(raw, halo, w):
    tb = raw.shape[0]
    ext = jnp.concatenate([halo, raw], axis=0)
    out = ext[5:5 + tb] * w[0:1]
    for j in range(1, SHORT_K):
        out = out + ext[5 + j:5 + j + tb] * w[j:j + 1]
    return out


def _dn_gates(ba, alog8, dtb8):
    xg = ba + dtb8
    ea8 = jnp.exp(alog8)
    return _sig(ba), -ea8 * _softplus(xg), xg, ea8


def _dn_item(pre_q, pre_k, pre_v, b, g):
    qc, kc, vc = (p * _sig(p) for p in (pre_q, pre_k, pre_v))
    nq = lax.rsqrt(_sum1(qc * qc) + EPS)
    nk = lax.rsqrt(_sum1(kc * kc) + EPS)
    return dict(q=qc * nq * (DK ** -0.5), k=kc * nk, v=vc, b=b, g=g, qc=qc, kc=kc, nq=nq, nk=nk)


def _dn_out(o, z, sz, og):
    r = lax.rsqrt(_mean(o * o) + EPS)
    xh = o * r
    return xh * og * (z * sz), xh, r


def _sds(shape, dtype):
    return jax.ShapeDtypeStruct(tuple(shape), dtype)


def _matmul(a, b, out_dtype, tm, tn, tk, name):
    m, kd = a.shape
    n = b.shape[1]
    tm, tn, tk = min(tm, m), min(tn, n), min(tk, kd)
    nk = kd // tk

    def body(a_ref, b_ref, o_ref, acc):
        @pl.when(pl.program_id(2) == 0)
        def _():
            acc[...] = jnp.zeros_like(acc)

        acc[...] += jnp.dot(a_ref[...], b_ref[...], preferred_element_type=F32)

        @pl.when(pl.program_id(2) == nk - 1)
        def _():
            o_ref[...] = acc[...].astype(o_ref.dtype)

    return _pcall(
        body, name=name, grid=(m // tm, n // tn, nk),
        in_specs=[pl.BlockSpec((tm, tk), lambda i, j, k: (i, k)), pl.BlockSpec((tk, tn), lambda i, j, k: (k, j))],
        out_specs=pl.BlockSpec((tm, tn), lambda i, j, k: (i, j)), out_shape=_sds((m, n), out_dtype),
        scratch_shapes=[pltpu.VMEM((tm, tn), F32)], compiler_params=_cparams(("parallel", "parallel", "arbitrary")),
    )(a, b)


def _travel_copies(v_refs, o_refs, send_sems, recv_sems):
    x, y, c = _axes()
    peers = _chip_peers(x, y)
    return [_remote(v_refs[a].at[2 * peers[k][0] + peers[k][1]], o_refs[a].at[k], send_sems, recv_sems, 3 * a + k,
                    (*peers[k], c)) for k in range(3) for a in range(len(v_refs))]


def _matmul_ksegs(a_segs, b, tm, tn, tk, name, travel=()):
    m, n = a_segs[0].shape[0], b.shape[1]
    tm, tn = min(tm, m), min(tn, n)
    main, tail = a_segs[:-1], a_segs[-1]
    steps = [s.shape[1] // tk for s in main]
    starts = [sum(steps[:i]) for i in range(len(main))]
    nk = sum(steps)
    wt = tail.shape[1]
    n_main, nv = len(main), len(travel)
    grid = (m // tm, n // tn, nk + 1)

    def body(*refs):
        a_refs, (at_ref, b_ref, bt_ref), refs = refs[:n_main], refs[n_main:n_main + 3], refs[n_main + 3:]
        v_refs, o_ref, refs = refs[:nv], refs[nv], refs[nv + 1:]
        got_refs, acc, sems = refs[:nv], refs[nv], refs[nv + 1:]
        k = pl.program_id(2)
        if nv:
            first = (pl.program_id(0) == 0) & (pl.program_id(1) == 0) & (k == 0)
            last = (pl.program_id(0) == grid[0] - 1) & (pl.program_id(1) == grid[1] - 1) & (k == nk)

            @pl.when(first)
            def _():
                for cp in _travel_copies(v_refs, got_refs, *sems):
                    cp.start()

        @pl.when(k == 0)
        def _():
            acc[...] = jnp.zeros_like(acc)

        for a_ref, k0, ns in zip(a_refs, starts, steps):
            @pl.when((k >= k0) & (k < k0 + ns))
            def _():
                acc[...] += jnp.dot(a_ref[...], b_ref[...], preferred_element_type=F32)

        @pl.when(k == nk)
        def _():
            o_ref[...] = acc[...] + jnp.dot(at_ref[...], bt_ref[...], preferred_element_type=F32)

        if nv:
            @pl.when(last)
            def _():
                for cp in _travel_copies(v_refs, got_refs, *sems):
                    cp.wait()

    seg_specs = [pl.BlockSpec((tm, tk), functools.partial(lambda i, j, k, k0, n_s: (i, jnp.clip(k - k0, 0, n_s - 1)),
                                                          k0=k0, n_s=n_s)) for k0, n_s in zip(starts, steps)]
    out = _pcall(
        body, name=name, grid=grid,
        in_specs=seg_specs + [pl.BlockSpec((tm, wt), lambda i, j, k: (i, 0)),
                              pl.BlockSpec((tk, tn), lambda i, j, k: (jnp.minimum(k, nk - 1), j)),
                              pl.BlockSpec((wt, tn), lambda i, j, k: (nk * tk // wt, j))] + [_ANY] * nv,
        out_specs=[pl.BlockSpec((tm, tn), lambda i, j, k: (i, j))] + [_ANY] * nv,
        out_shape=[_sds((m, n), F32)] + [_sds((3,) + v.shape[1:], v.dtype) for v in travel],
        scratch_shapes=[pltpu.VMEM((tm, tn), F32)] + [pltpu.SemaphoreType.DMA((3 * nv,))] * (2 if nv else 0),
        compiler_params=_cparams(("arbitrary",) * 3 if nv else ("parallel", "parallel", "arbitrary")),
    )(*main, tail, b, b, *travel)
    return (out[0], out[1:]) if nv else out[0]


def _rms_fwd_call(x, g, name):
    m = x.shape[0]
    tm = min(512, m)

    def body(x_ref, g_ref, o_ref):
        xv = x_ref[...]
        o_ref[...] = (xv * lax.rsqrt(_mean(xv * xv) + EPS) * g_ref[...]).astype(BF16)

    return _pcall(
        body, name=name, grid=(m // tm,),
        in_specs=[pl.BlockSpec((tm, D_MODEL), lambda i: (i, 0)), pl.BlockSpec((1, D_MODEL), lambda i: (0, 0))],
        out_specs=pl.BlockSpec((tm, D_MODEL), lambda i: (i, 0)), out_shape=_sds((m, D_MODEL), BF16),
        compiler_params=_cparams(("parallel",)),
    )(x, g)


def _rms_bwd_call(x, dh, g, dres, name):
    m = x.shape[0]
    tm = min(512, m)

    def body(x_ref, dh_ref, g_ref, dr_ref, dx_ref, dg_ref):
        @pl.when(pl.program_id(0) == 0)
        def _():
            dg_ref[...] = jnp.zeros_like(dg_ref)

        xv, dhv = x_ref[...], dh_ref[...]
        r = lax.rsqrt(_mean(xv * xv) + EPS)
        xh = xv * r
        dx_ref[...] = _rms_bwd(dhv * g_ref[...], xh, r) + dr_ref[...]
        dg_ref[...] += _rows8(_sum0(dhv * xh))

    row = pl.BlockSpec((tm, D_MODEL), lambda i: (i, 0))
    return _pcall(
        body, name=name, grid=(m // tm,),
        in_specs=[row, row, pl.BlockSpec((1, D_MODEL), lambda i: (0, 0)), row],
        out_specs=[row, pl.BlockSpec((8, D_MODEL), lambda i: (0, 0))],
        out_shape=[_sds((m, D_MODEL), F32), _sds((8, D_MODEL), F32)], compiler_params=_cparams(("arbitrary",)),
    )(x, dh, g, dres)


def _loss_call(x, tgt, g):
    m = x.shape[0]
    tm = min(512, m)

    def body(x_ref, t_ref, g_ref, dx_ref, dg_ref, l_ref):
        @pl.when(pl.program_id(0) == 0)
        def _():
            dg_ref[...] = jnp.zeros_like(dg_ref)
            l_ref[...] = jnp.zeros_like(l_ref)

        xv = x_ref[...]
        r = lax.rsqrt(_mean(xv * xv) + EPS)
        xh = xv * r
        err = xh * g_ref[...] - t_ref[...]
        dy = err * (1.0 / D_MODEL)
        dx_ref[...] = _rms_bwd(dy * g_ref[...], xh, r)
        dg_ref[...] += _rows8(_sum0(dy * xh))
        l_ref[...] += 0.5 * _sum0(_mean(err * err))

    row = pl.BlockSpec((tm, D_MODEL), lambda i: (i, 0))
    return _pcall(
        body, name="loss_head", grid=(m // tm,),
        in_specs=[row, row, pl.BlockSpec((1, D_MODEL), lambda i: (0, 0))],
        out_specs=[row, pl.BlockSpec((8, D_MODEL), lambda i: (0, 0)), pl.BlockSpec((8, 128), lambda i: (0, 0))],
        out_shape=[_sds((m, D_MODEL), F32), _sds((8, D_MODEL), F32), _sds((8, 128), F32)],
        compiler_params=_cparams(("arbitrary",)),
    )(x, tgt, g)


def _halo_idx(i, rows):
    return jnp.maximum(i * (TB // rows) - 1, 0)


def _a_tiles(nt, rev):
    ti = (lambda i: nt - 1 - i) if rev else (lambda i: i)
    c0 = A_OFF // CONV_W
    return [pl.BlockSpec((1, TB, CONV_W), functools.partial(lambda b, i, c: (b, ti(i), c), c=c0 + c)) for c in range(3)]


def _a_fwd_call(proj, w, bias, g, b, name):
    bsz, t, _ = proj.shape
    nt = t // TB
    c0 = A_OFF // CONV_W

    def body(val_ref, glu_ref, az_ref, vh_ref, gh_ref, w_ref, bias_ref, g_ref, b_ref, y_ref, c_ref):
        keep = jnp.where(pl.program_id(1) > 0, 1.0, 0.0)
        y, c = _a_fwd(val_ref[0], glu_ref[0], az_ref[0], vh_ref[0] * keep, gh_ref[0], w_ref[...], bias_ref[...],
                      g_ref[...], b_ref[...])
        y_ref[0] = y.astype(BF16)
        c_ref[0] = c

    halo = [pl.BlockSpec((1, HALO_A, CONV_W), functools.partial(lambda b, i, c: (b, _halo_idx(i, HALO_A), c), c=c0 + c))
            for c in range(2)]
    par = [pl.BlockSpec((HALO_A, CONV_W), lambda b, i: (0, 0))] + [pl.BlockSpec((1, CONV_W), lambda b, i: (0, 0))] * 3
    tile = pl.BlockSpec((1, TB, CONV_W), lambda b, i: (b, i, 0))
    return _pcall(
        body, name=name, grid=(bsz, nt), in_specs=_a_tiles(nt, False) + halo + par, out_specs=[tile, tile],
        out_shape=[_sds((bsz, t, CONV_W), BF16), _sds((bsz, t, CONV_W), F32)],
        compiler_params=_cparams(("parallel", "parallel")),
    )(proj, proj, proj, proj, proj, w, bias, g, b)


def _a_bwd_call(dy, conv, proj, w, g, b, name, travel=()):
    bsz, t, _ = proj.shape
    nt = t // TB
    nv = len(travel)

    def body(*refs):
        (dy_ref, c_ref, val_ref, glu_ref, az_ref, w_ref, g_ref, b_ref), refs = refs[:8], refs[8:]
        v_refs, (da_ref, dw_ref, ds_ref), refs = refs[:nv], refs[nv:nv + 3], refs[nv + 3:]
        o_refs, carry, sems = refs[:nv], refs[nv], refs[nv + 1:]
        ip = pl.program_id(1)

        def exchange():
            return _travel_copies(v_refs, o_refs, *sems)

        @pl.when((pl.program_id(0) == 0) & (ip == 0))
        def _():
            dw_ref[...] = jnp.zeros_like(dw_ref)
            ds_ref[...] = jnp.zeros_like(ds_ref)
            if nv:
                for cp in exchange():
                    cp.start()

        @pl.when(ip == 0)
        def _():
            carry[...] = jnp.zeros_like(carry)

        dval, dglu, daz, dw, dbias, dg, db, head = _a_bwd(dy_ref[0], val_ref[0], glu_ref[0], az_ref[0], c_ref[0],
                                                          w_ref[...], g_ref[...], b_ref[...], carry[...])
        carry[...] = head
        for n, dcol in enumerate((dval, dglu, daz)):
            da_ref[0, :, n * CONV_W:(n + 1) * CONV_W] = dcol.astype(BF16)
        dw_ref[...] += dw
        ds_ref[...] += _rows8(dbias, dg, db)

        if nv:
            @pl.when((pl.program_id(0) == bsz - 1) & (ip == nt - 1))
            def _():
                for cp in exchange():
                    cp.wait()

    rtile = lambda b, i: (b, nt - 1 - i, 0)
    par = [pl.BlockSpec((HALO_A, CONV_W), lambda b, i: (0, 0))] + [pl.BlockSpec((1, CONV_W), lambda b, i: (0, 0))] * 2
    return _pcall(
        body, name=name, grid=(bsz, nt),
        in_specs=[pl.BlockSpec((1, TB, CONV_W), rtile)] * 2 + _a_tiles(nt, True) + par + [_ANY] * nv,
        out_specs=[pl.BlockSpec((1, TB, 3 * CONV_W), rtile), pl.BlockSpec((HALO_A, CONV_W), lambda b, i: (0, 0)),
                   pl.BlockSpec((8, CONV_W), lambda b, i: (0, 0))] + [_ANY] * nv,
        out_shape=[_sds((bsz, t, 3 * CONV_W), BF16), _sds((HALO_A, CONV_W), F32), _sds((8, CONV_W), F32)]
        + [_sds((3,) + v.shape[1:], v.dtype) for v in travel],
        scratch_shapes=[pltpu.VMEM((HALO_A, CONV_W), F32)] + [pltpu.SemaphoreType.DMA((3 * nv,))] * (2 if nv else 0),
        compiler_params=_cparams(("arbitrary", "arbitrary")),
    )(dy, conv, proj, proj, proj, w, g, b, *travel)


def _c_specs():
    c0 = C_OFF // SG_W
    tile = [pl.BlockSpec((1, TB, SG_W), functools.partial(lambda b, i, c: (b, i, c), c=c0 + c)) for c in range(3)]
    par = [pl.BlockSpec((1, SG_W), lambda b, i: (0, 0))] * 2 + [
        pl.BlockSpec((SG_G, SG_C, SG_C), lambda b, i: (0, 0, 0)), pl.BlockSpec((SG_C, SG_W), lambda b, i: (0, 0))]
    return tile + par


def _c_fwd_call(proj, g, b, ws, bias_full, name):
    bsz, t, _ = proj.shape

    def body(cu_ref, cv_ref, cz_ref, g_ref, b_ref, ws_ref, bf_ref, y_ref):
        y, _ = _c_fwd(cu_ref[0], cv_ref[0], cz_ref[0], g_ref[...], b_ref[...], ws_ref[...], bf_ref[...])
        y_ref[0] = y.astype(BF16)

    return _pcall(
        body, name=name, grid=(bsz, t // TB), in_specs=_c_specs(),
        out_specs=pl.BlockSpec((1, TB, SG_W), lambda b, i: (b, i, 0)), out_shape=_sds((bsz, t, SG_W), BF16),
        compiler_params=_cparams(("parallel", "parallel")),
    )(proj, proj, proj, g, b, ws, bias_full)


def _c_bwd_call(dy, proj, g, b, ws, bias_full, name):
    bsz, t, _ = proj.shape

    def body(dy_ref, cu_ref, cv_ref, cz_ref, g_ref, b_ref, ws_ref, bf_ref, dc_ref, dws_ref, dbs_ref, ds_ref):
        @pl.when((pl.program_id(0) == 0) & (pl.program_id(1) == 0))
        def _():
            dws_ref[...] = jnp.zeros_like(dws_ref)
            dbs_ref[...] = jnp.zeros_like(dbs_ref)
            ds_ref[...] = jnp.zeros_like(ds_ref)

        dcu, dcv, dcz, dws, dbs, dg, db = _c_bwd(dy_ref[0], cu_ref[0], cv_ref[0], cz_ref[0], g_ref[...], b_ref[...],
                                                 ws_ref[...], bf_ref[...])
        for n, dcol in enumerate((dcu, dcv, dcz)):
            dc_ref[0, :, n * SG_W:(n + 1) * SG_W] = dcol.astype(BF16)
        for gi in range(SG_G):
            dws_ref[gi] += dws[gi]
        dbs_ref[...] += dbs
        ds_ref[...] += _rows8(dg, db)

    tile = lambda b, i: (b, i, 0)
    return _pcall(
        body, name=name, grid=(bsz, t // TB), in_specs=[pl.BlockSpec((1, TB, SG_W), tile)] + _c_specs(),
        out_specs=[pl.BlockSpec((1, TB, 3 * SG_W), tile), pl.BlockSpec((SG_G, SG_C, SG_C), lambda b, i: (0, 0, 0)),
                   pl.BlockSpec((SG_C, SG_W), lambda b, i: (0, 0)), pl.BlockSpec((8, SG_W), lambda b, i: (0, 0))],
        out_shape=[_sds((bsz, t, 3 * SG_W), BF16), _sds((SG_G, SG_C, SG_C), F32), _sds((SG_C, SG_W), F32),
                   _sds((8, SG_W), F32)],
        compiler_params=_cparams(("arbitrary", "arbitrary")),
    )(dy, proj, proj, proj, g, b, ws, bias_full)


def _m_specs():
    g0 = G_OFF // D_MODEL
    y = [pl.BlockSpec((1, TB, n), lambda b, i: (b, i, 0)) for n in (CONV_W, D_MODEL, SG_W)]
    gates = [pl.BlockSpec((1, TB, D_MODEL), functools.partial(lambda b, i, c: (b, i, c), c=g0 + c)) for c in range(3)]
    return y + gates


def _w_specs(*shapes):
    return [pl.BlockSpec(s, lambda b, i: (0, 0)) for s in shapes]


def _m_fwd_call(ya, yb, yc, proj, x, wa, wb, wc, wo, name):
    bsz, t, _ = x.shape

    def body(ya_ref, yb_ref, yc_ref, g0_ref, g1_ref, g2_ref, x_ref, wa_ref, wb_ref, wc_ref, wo_ref, o_ref):
        merged, _ = _m_fwd(ya_ref[0], yb_ref[0], yc_ref[0], g0_ref[0], g1_ref[0], g2_ref[0], wa_ref[...], wb_ref[...],
                           wc_ref[...])
        o_ref[0] = x_ref[0] + _nn(merged, wo_ref[...])

    tile = pl.BlockSpec((1, TB, D_MODEL), lambda b, i: (b, i, 0))
    return _pcall(
        body, name=name, grid=(bsz, t // TB),
        in_specs=_m_specs() + [tile] + _w_specs(wa.shape, wb.shape, wc.shape, wo.shape),
        out_specs=tile, out_shape=_sds(x.shape, F32), compiler_params=_cparams(("parallel", "parallel")),
    )(ya, yb, yc, proj, proj, proj, x, wa, wb, wc, wo)


def _m_bwd_call(dout, ya, yb, yc, proj, wa, wb, wc, wo, name):
    bsz, t, _ = dout.shape

    def body(do_ref, ya_ref, yb_ref, yc_ref, g0_ref, g1_ref, g2_ref, wa_ref, wb_ref, wc_ref, wo_ref, dya_ref, dyb_ref,
             dyc_ref, dg_ref, mg_ref, dp_ref):
        dys, dgs, merged, dps = _m_bwd(do_ref[0], ya_ref[0], yb_ref[0], yc_ref[0], g0_ref[0], g1_ref[0], g2_ref[0],
                                       wa_ref[...], wb_ref[...], wc_ref[...], wo_ref[...])
        dya_ref[0], dyb_ref[0], dyc_ref[0] = dys
        mg_ref[0] = merged.astype(BF16)
        for n in range(3):
            dg_ref[0, :, n * D_MODEL:(n + 1) * D_MODEL] = dgs[n].astype(BF16)
            dp_ref[0, :, n * D_MODEL:(n + 1) * D_MODEL] = dps[n].astype(BF16)

    tile = lambda n: pl.BlockSpec((1, TB, n), lambda b, i: (b, i, 0))
    widths = (CONV_W, D_MODEL, SG_W, 3 * D_MODEL, D_MODEL, 3 * D_MODEL)
    dts = (F32, F32, F32, BF16, BF16, BF16)
    return _pcall(
        body, name=name, grid=(bsz, t // TB),
        in_specs=[tile(D_MODEL)] + _m_specs() + _w_specs(wa.shape, wb.shape, wc.shape, wo.shape),
        out_specs=[tile(n) for n in widths], out_shape=[_sds((bsz, t, n), d) for n, d in zip(widths, dts)],
        compiler_params=_cparams(("parallel", "parallel")),
    )(dout, ya, yb, yc, proj, proj, proj, wa, wb, wc, wo)


def _m_wgrad_call(ya, yb, yc, merged, dps, dout, name):
    bsz, t, _ = dout.shape

    def body(ya_ref, yb_ref, yc_ref, mg_ref, dp_ref, do_ref, dwa_ref, dwb_ref, dwc_ref, dwo_ref):
        @pl.when((pl.program_id(0) == 0) & (pl.program_id(1) == 0))
        def _():
            for r in (dwa_ref, dwb_ref, dwc_ref, dwo_ref):
                r[...] = jnp.zeros_like(r)

        dp = dp_ref[0]
        dwa_ref[...] += _tn(ya_ref[0], dp[:, :D_MODEL])
        dwb_ref[...] += _tn(yb_ref[0], dp[:, D_MODEL:2 * D_MODEL])
        dwc_ref[...] += _tn(yc_ref[0], dp[:, 2 * D_MODEL:])
        dwo_ref[...] += _tn(mg_ref[0], do_ref[0])

    tile = lambda n: pl.BlockSpec((1, TB, n), lambda b, i: (b, i, 0))
    shapes = ((CONV_W, D_MODEL), (D_MODEL, D_MODEL), (SG_W, D_MODEL), (D_MODEL, D_MODEL))
    return _pcall(
        body, name=name, grid=(bsz, t // TB),
        in_specs=[tile(CONV_W), tile(D_MODEL), tile(SG_W), tile(D_MODEL), tile(3 * D_MODEL), tile(D_MODEL)],
        out_specs=_w_specs(*shapes), out_shape=[_sds(s, F32) for s in shapes],
        compiler_params=_cparams(("arbitrary", "arbitrary")),
    )(ya, yb, yc, merged, dps, dout)


def _dn_specs(nc, rev):
    ti = (lambda i: nc - 1 - i) if rev else (lambda i: i)
    return [pl.BlockSpec((1, CHUNK, B_W), lambda b, i: (b, ti(i), 0)),
            pl.BlockSpec((1, CHUNK, DK), lambda b, i: (b, ti(i), BA_OFF // DK)),
            pl.BlockSpec((SHORT_K, 3 * D_MODEL), lambda b, i: (0, 0))] + [pl.BlockSpec((1, DK), lambda b, i: (0, 0))] * 3


def _dn_saved_specs(nc, rev):
    ti = (lambda i: nc - 1 - i) if rev else (lambda i: i)
    return [pl.BlockSpec((1, CHUNK, D_MODEL), lambda b, i: (b, ti(i), 0)),
            pl.BlockSpec((1, CHUNK, 3 * D_MODEL), lambda b, i: (b, ti(i), 0)),
            pl.BlockSpec((1, HEADS, 1, DK, DK), lambda b, i: (b, 0, ti(i), 0, 0)),
            pl.BlockSpec((1, 1, CHUNK, HEADS * CHUNK), lambda b, i: (b, ti(i), 0, 0)),
            pl.BlockSpec((1, CHUNK, 2 * D_MODEL), lambda b, i: (b, ti(i), 0)),
            pl.BlockSpec((1, CHUNK, D_MODEL), lambda b, i: (b, ti(i), 0))]


def _gather_fused(x_refs, o_refs, send_sems, recv_sems, finish):
    x, y, c = _axes()
    chip, sib, peers = 2 * x + y, (x, y, 1 - c), _chip_peers(x, y)
    pchip = [2 * px + py for px, py in peers]
    n = len(x_refs)
    halves = [r.shape[0] // 2 for r in x_refs]

    def half(a, cc):
        return pl.ds(cc * halves[a], halves[a])

    first = [_remote(x_refs[a].at[half(a, c)], o_refs[a].at[chip, half(a, c)], send_sems, recv_sems, 6 * a + k,
                     (*peers[k], c)) for k in range(3) for a in range(n)]
    if not finish:
        for cp in first:
            cp.start()
        return
    passed = []
    for k in range(3):
        for a in range(n):
            land = o_refs[a].at[pchip[k], half(a, c)]
            _remote(land, land, send_sems, recv_sems, 6 * a + k, (*peers[k], c)).wait_recv()
            passed.append(_remote(land, land, send_sems, recv_sems, 6 * a + 3 + k, sib))
            passed[-1].start()
    for k in range(3):
        for a in range(n):
            land = o_refs[a].at[pchip[k], half(a, 1 - c)]
            _remote(land, land, send_sems, recv_sems, 6 * a + 3 + k, sib).wait_recv()
    for cp in first + passed:
        cp.wait_send()


def _dn_fwd_call(proj, wconv, alog, dtb, og, name, gather=()):
    bsz, t, _ = proj.shape
    nc = t // CHUNK
    ng = len(gather)

    def body(*refs):
        (x_ref, ba_ref, w_ref, al_ref, dt_ref, og_ref, halo_ref), refs = refs[:7], refs[7:]
        g_refs, (y_ref, o_ref, pre_ref, st_ref, t_ref, uw_ref, vn_ref), refs = refs[:ng], refs[ng:ng + 7], refs[ng + 7:]
        go_refs, s_scr, sems = refs[:ng], refs[ng], refs[ng + 1:]
        i = pl.program_id(1)

        if ng:
            @pl.when((pl.program_id(0) == 0) & (i == 0))
            def _():
                _gather_fused(g_refs, go_refs, *sems, finish=False)

        @pl.when(i == 0)
        def _():
            s_scr[...] = jnp.zeros_like(s_scr)

        keep = jnp.where(i > 0, 1.0, 0.0)
        bsig, gfull, _, _ = _dn_gates(ba_ref[0], al_ref[...], dt_ref[...])
        items = []
        for h in range(HEADS):
            pres = []
            for n in range(3):
                cs = slice(n * D_MODEL + h * DK, n * D_MODEL + (h + 1) * DK)
                pres.append(_short_conv(x_ref[0, :, cs], halo_ref[0, :, cs] * keep, w_ref[:, cs]))
                pre_ref[0, :, cs] = pres[-1]
            items.append(_dn_item(*pres, bsig[:, h:h + 1], gfull[:, HEADS + h:HEADS + h + 1]))
        ss = [s_scr[h] for h in range(HEADS)]
        for h in range(HEADS):
            st_ref[0, h, 0] = ss[h]
        outs, ss, vnew = _dn_fwd_chunk(items, ss)
        og_v = og_ref[...]
        for h, d in enumerate(items):
            s_scr[h] = ss[h]
            hs = slice(h * DK, (h + 1) * DK)
            z = x_ref[0, :, 3 * D_MODEL + h * DK:3 * D_MODEL + (h + 1) * DK]
            o_ref[0, :, hs] = outs[h]
            y_ref[0, :, hs] = _dn_out(outs[h], z, _sig(z), og_v)[0].astype(BF16)
            t_ref[0, 0, :, h * CHUNK:(h + 1) * CHUNK] = d["t"].astype(BF16)
            uw_ref[0, :, 2 * h * DK:(2 * h + 1) * DK] = d["u"].astype(BF16)
            uw_ref[0, :, (2 * h + 1) * DK:2 * (h + 1) * DK] = d["w"].astype(BF16)
            vn_ref[0, :, hs] = vnew[h].astype(BF16)

        if ng:
            @pl.when((pl.program_id(0) == bsz - 1) & (i == nc - 1))
            def _():
                _gather_fused(g_refs, go_refs, *sems, finish=True)

    halo = pl.BlockSpec((1, HALO_B, 3 * D_MODEL), lambda b, i: (b, jnp.maximum(i * (CHUNK // HALO_B) - 1, 0), 0))
    outs = _pcall(
        body, name=name, grid=(bsz, nc), in_specs=_dn_specs(nc, False) + [halo] + [_ANY] * ng,
        out_specs=[pl.BlockSpec((1, CHUNK, D_MODEL), lambda b, i: (b, i, 0))] + _dn_saved_specs(nc, False) + [_ANY] * ng,
        out_shape=[_sds((bsz, t, D_MODEL), BF16), _sds((bsz, t, D_MODEL), F32), _sds((bsz, t, 3 * D_MODEL), F32),
                   _sds((bsz, HEADS, nc, DK, DK), F32), _sds((bsz, nc, CHUNK, HEADS * CHUNK), BF16),
                   _sds((bsz, t, 2 * D_MODEL), BF16), _sds((bsz, t, D_MODEL), BF16)]
        + [_sds((4,) + s.shape, s.dtype) for s in gather],
        scratch_shapes=[pltpu.VMEM((HEADS, DK, DK), F32)] + [pltpu.SemaphoreType.DMA((6 * ng,))] * (2 if ng else 0),
        compiler_params=_cparams(("arbitrary", "arbitrary")),
    )(proj, proj, wconv, alog, dtb, og, proj, *gather)
    if not ng:
        return outs
    chip = 2 * lax.axis_index("x") + lax.axis_index("y")
    return list(outs[:7]) + [lax.dynamic_update_slice_in_dim(o, s[None], chip, axis=0)
                             for o, s in zip(outs[7:], gather)]


def _dn_bwd_call(dy, saved, proj, wconv, alog, dtb, og, name):
    bsz, t, _ = proj.shape
    nc = t // CHUNK

    def body(dy_ref, o_ref, pre_ref, st_ref, t_ref, uw_ref, vn_ref, x_ref, ba_ref, w_ref, al_ref, dt_ref, og_ref,
             dx_ref, dba_ref, dwc_ref, dsm_ref, ds_scr, dpre_scr):
        ip = pl.program_id(1)

        @pl.when((pl.program_id(0) == 0) & (ip == 0))
        def _():
            dwc_ref[...] = jnp.zeros_like(dwc_ref)
            dsm_ref[...] = jnp.zeros_like(dsm_ref)

        @pl.when(ip == 0)
        def _():
            ds_scr[...] = jnp.zeros_like(ds_scr)
            dpre_scr[...] = jnp.zeros_like(dpre_scr)

        bsig, gfull, xg, ea8 = _dn_gates(ba_ref[0], al_ref[...], dt_ref[...])
        og_v = og_ref[...]
        items, dog = [], jnp.zeros((1, DK), F32)
        for h in range(HEADS):
            hs = slice(h * DK, (h + 1) * DK)
            zs = slice(3 * D_MODEL + h * DK, 3 * D_MODEL + (h + 1) * DK)
            d = _dn_item(*(pre_ref[0, :, n * D_MODEL + h * DK:n * D_MODEL + (h + 1) * DK] for n in range(3)),
                         bsig[:, h:h + 1], gfull[:, HEADS + h:HEADS + h + 1])
            z, dyv = x_ref[0, :, zs], dy_ref[0, :, hs]
            sz = _sig(z)
            _, xh, r = _dn_out(o_ref[0, :, hs], z, sz, og_v)
            dx_ref[0, :, zs] = (dyv * xh * og_v * (sz * (1.0 + z * (1.0 - sz)))).astype(BF16)
            don = dyv * (z * sz)
            dog = dog + _sum0(don * xh)
            d.update(do=_rms_bwd(don * og_v, xh, r), t=t_ref[0, 0, :, h * CHUNK:(h + 1) * CHUNK],
                     u=uw_ref[0, :, 2 * h * DK:(2 * h + 1) * DK], w=uw_ref[0, :, (2 * h + 1) * DK:2 * (h + 1) * DK],
                     vnew=vn_ref[0, :, hs])
            items.append(d)
        grads, dss = _dn_bwd_chunk(items, [st_ref[0, h, 0] for h in range(HEADS)], [ds_scr[h] for h in range(HEADS)])
        lanes = lax.broadcasted_iota(jnp.int32, (CHUNK, DK), 1)
        lane8 = lax.broadcasted_iota(jnp.int32, (8, DK), 1)
        row8 = lax.broadcasted_iota(jnp.int32, (8, DK), 0)
        dgdx = -ea8 * _sig(xg)
        dba = jnp.zeros((CHUNK, DK), F32)
        dsm = jnp.where(row8 == 2, dog, 0.0)
        for h, (d, (dq, dk, dv, dbeta, dg)) in enumerate(zip(items, grads)):
            ds_scr[h] = dss[h]
            qc, kc, nq, nk = d["qc"], d["kc"], d["nq"], d["nk"]
            dacts = ((DK ** -0.5) * nq * (dq - qc * (nq * nq * _sum1(dq * qc))),
                     nk * (dk - kc * (nk * nk * _sum1(dk * kc))), dv)
            dal = dg * dgdx[:, HEADS + h:HEADS + h + 1]
            dba = dba + jnp.where(lanes == h, dbeta * d["b"] * (1.0 - d["b"]), 0.0) + jnp.where(lanes == HEADS + h, dal, 0.0)
            dsm = dsm + (jnp.where((row8 == 0) & (lane8 == h), _sum0(dg * d["g"]), 0.0)
                         + jnp.where((row8 == 1) & (lane8 == h), _sum0(dal), 0.0))
            for n in range(3):
                cs = slice(n * D_MODEL + h * DK, n * D_MODEL + (h + 1) * DK)
                pre, raw, w = pre_ref[0, :, cs], x_ref[0, :, cs], w_ref[:, cs]
                sp = _sig(pre)
                dpre = dacts[n] * (sp * (1.0 + pre * (1.0 - sp)))
                dext = jnp.concatenate([dpre, dpre_scr[:, cs]], axis=0)
                draw = dext[3:3 + CHUNK] * w[0:1]
                rows = [_sum0(dext[3:3 + CHUNK] * raw)]
                for j in range(1, SHORT_K):
                    shifted = dext[3 - j:3 - j + CHUNK]
                    draw = draw + shifted * w[j:j + 1]
                    rows.append(_sum0(shifted * raw))
                dpre_scr[:, cs] = dpre[:HALO_B]
                dx_ref[0, :, cs] = draw.astype(BF16)
                dwc_ref[:, cs] += _rows8(*rows)
        dba_ref[0] = dba.astype(BF16)
        dsm_ref[...] += dsm

    rrow = lambda n: pl.BlockSpec((1, CHUNK, n), lambda b, i: (b, nc - 1 - i, 0))
    return _pcall(
        body, name=name, grid=(bsz, nc), in_specs=[rrow(D_MODEL)] + _dn_saved_specs(nc, True) + _dn_specs(nc, True),
        out_specs=[rrow(B_W), rrow(DK), pl.BlockSpec((8, 3 * D_MODEL), lambda b, i: (0, 0)),
                   pl.BlockSpec((8, DK), lambda b, i: (0, 0))],
        out_shape=[_sds((bsz, t, B_W), BF16), _sds((bsz, t, DK), BF16), _sds((8, 3 * D_MODEL), F32), _sds((8, DK), F32)],
        scratch_shapes=[pltpu.VMEM((HEADS, DK, DK), F32), pltpu.VMEM((HALO_B, 3 * D_MODEL), F32)],
        compiler_params=_cparams(("arbitrary", "arbitrary")),
    )(dy, *saved, proj, proj, wconv, alog, dtb, og)


def _adamw_call(w, g, m, v, name):
    shape = w.shape
    view = (math.prod(shape[:-2]),) + shape[-2:] if len(shape) >= 2 else (1, 1) + shape
    lead, rows, cols = view
    tr, tl = (128, 1) if rows % 128 == 0 else (rows, min(lead, 64))
    c1, c2 = 1.0 - ADAM_B1 ** ADAM_STEP, 1.0 - ADAM_B2 ** ADAM_STEP

    def body(w_ref, g_ref, m_ref, v_ref, d_ref, nm_ref, nv_ref):
        gv = g_ref[...]
        nm = ADAM_B1 * m_ref[...] + (1.0 - ADAM_B1) * gv
        nv = ADAM_B2 * v_ref[...] + (1.0 - ADAM_B2) * (gv * gv)
        d_ref[...] = -ADAM_LR * ((nm / c1) / (jnp.sqrt(nv / c2) + ADAM_EPS) + ADAM_WD * w_ref[...])
        nm_ref[...] = nm
        nv_ref[...] = nv

    blk = pl.BlockSpec((tl, tr, cols), lambda l, i: (l, i, 0))
    outs = _pcall(
        body, name=name, grid=(pl.cdiv(lead, tl), rows // tr), in_specs=[blk] * 4, out_specs=[blk] * 3,
        out_shape=[_sds(view, F32)] * 3, compiler_params=_cparams(("parallel", "parallel")),
    )(*(a.reshape(view) for a in (w, g, m, v)))
    return tuple(o.reshape(shape) for o in outs)


def _row_tile(rows, cap=512):
    for tr in range(cap, 15, -16):
        if rows % tr == 0:
            return tr
    return rows


def _add_pair_call(by_chip, recv, where, name):
    _, n, rows, cols = by_chip.shape
    tr = _row_tile(rows, 256)

    def body(where_ref, a_ref, b_ref, sb_ref, own_ref):
        s = a_ref[0, 0] + b_ref[0]
        sb_ref[0] = s.astype(BF16)

        @pl.when(pl.program_id(1) == where_ref[1])
        def _():
            own_ref[...] = s

    grid_spec = pltpu.PrefetchScalarGridSpec(
        num_scalar_prefetch=1, grid=(rows // tr, n),
        in_specs=[pl.BlockSpec((1, 1, tr, cols), lambda i, j, wr: (wr[0], j, i, 0)),
                  pl.BlockSpec((1, tr, cols), lambda i, j, wr: (j, i, 0))],
        out_specs=[pl.BlockSpec((1, tr, cols), lambda i, j, wr: (j, i, 0)),
                   pl.BlockSpec((tr, cols), lambda i, j, wr: (i, 0))])
    return _pcall(
        body, name=name, grid_spec=grid_spec, out_shape=[_sds((n, rows, cols), BF16), _sds((rows, cols), F32)],
        compiler_params=_cparams(("parallel", "arbitrary")),
    )(where, by_chip, recv)


def _add_recv_call(own, recv, where, name):
    rows, cols = own.shape
    tr = _row_tile(rows, 256)

    def body(where_ref, o_ref, r_ref, s_ref):
        s_ref[0] = ((o_ref[...] + r_ref[0].astype(F32)) + r_ref[1].astype(F32)) + r_ref[2].astype(F32)

    grid_spec = pltpu.PrefetchScalarGridSpec(
        num_scalar_prefetch=1, grid=(rows // tr,),
        in_specs=[pl.BlockSpec((tr, cols), lambda i, wr: (i, 0)), pl.BlockSpec((3, tr, cols), lambda i, wr: (0, i, 0))],
        out_specs=pl.BlockSpec((1, tr, cols), lambda i, wr: (wr[0], i, 0)))
    return _pcall(
        body, name=name, grid_spec=grid_spec, out_shape=_sds((2, rows, cols), F32),
        compiler_params=_cparams(("parallel",)),
    )(where, own, recv)


def _axes():
    return lax.axis_index("x"), lax.axis_index("y"), lax.axis_index("c")


def _chip_peers(x, y):
    return [(x, 1 - y), (1 - x, y), (1 - x, 1 - y)]


_ANY = pl.BlockSpec(memory_space=pl.ANY)
_VMEM = pl.BlockSpec(memory_space=pltpu.VMEM)


def _remote(src, dst, send_sems, recv_sems, k, dev):
    return pltpu.make_async_remote_copy(src_ref=src, dst_ref=dst, send_sem=send_sems.at[k], recv_sem=recv_sems.at[k],
                                        device_id=dev, device_id_type=MESH)


def _gather_weights_call(shards):
    n = len(shards)
    halves = [s.shape[0] // 2 for s in shards]
    quarter = halves[0] // 2
    base = [0] + [8 + 6 * (a - 1) for a in range(1, n)]

    def body(*refs):
        x_refs, o_refs, (send_sems, recv_sems) = refs[:n], refs[n:2 * n], refs[2 * n:]
        x, y, c = _axes()
        chip, sib, peers = 2 * x + y, (x, y, 1 - c), _chip_peers(x, y)
        pchip = [2 * px + py for px, py in peers]

        def half(a, cc):
            return pl.ds(cc * halves[a], halves[a])

        def part(cc, q):
            return pl.ds(cc * halves[0] + q * quarter, quarter)

        def copy(ref, k, dev):
            return _remote(ref, ref, send_sems, recv_sems, k, dev)

        sends = [_remote(x_refs[a].at[half(a, c)], o_refs[a].at[chip, half(a, c)], send_sems, recv_sems, base[a] + k,
                         (*peers[k], c)) for k in range(3) for a in range(n) if a > 0 or k < 2]
        for cp in sends:
            cp.start()
        for k in range(2):
            copy(o_refs[0].at[pchip[k], half(0, c)], k, (*peers[k], c)).wait_recv()
            sends.append(copy(o_refs[0].at[pchip[k], part(c, k)], 2 + k, (*peers[1 - k], c)))
            sends.append(copy(o_refs[0].at[pchip[k], half(0, c)], 4 + k, sib))
            sends[-2].start()
            sends[-1].start()
        for k in range(3):
            for a in range(1, n):
                land = o_refs[a].at[pchip[k], half(a, c)]
                copy(land, base[a] + k, (*peers[k], c)).wait_recv()
                sends.append(copy(land, base[a] + 3 + k, sib))
                sends[-1].start()
        for q in range(2):
            land = o_refs[0].at[pchip[2], part(c, q)]
            copy(land, 2 + q, (*peers[1 - q], c)).wait_recv()
            sends.append(copy(land, 6 + q, sib))
            sends[-1].start()
        for k in range(2):
            copy(o_refs[0].at[pchip[k], half(0, 1 - c)], 4 + k, sib).wait_recv()
            copy(o_refs[0].at[pchip[2], part(1 - c, k)], 6 + k, sib).wait_recv()
        for k in range(3):
            for a in range(1, n):
                copy(o_refs[a].at[pchip[k], half(a, 1 - c)], base[a] + 3 + k, sib).wait_recv()
        for cp in sends:
            cp.wait_send()

    n_sems = 8 + 6 * (n - 1)
    outs = _pcall(
        body, name="gather_weights", in_specs=[_ANY] * n, out_specs=[_ANY] * n,
        out_shape=[_sds((4,) + s.shape, s.dtype) for s in shards],
        scratch_shapes=[pltpu.SemaphoreType.DMA((n_sems,)), pltpu.SemaphoreType.DMA((n_sems,))],
    )(*shards)
    chip = 2 * lax.axis_index("x") + lax.axis_index("y")
    return [lax.dynamic_update_slice_in_dim(o, s[None], chip, axis=0) for o, s in zip(outs, shards)]


def _pair_partials_call(by_chip, name):
    n = len(by_chip)

    def body(*refs):
        v_refs, o_refs, (send_sems, recv_sems) = refs[:n], refs[n:2 * n], refs[2 * n:]
        x, y, c = _axes()
        cps = [_remote(v_refs[a].at[1 - c], o_refs[a], send_sems, recv_sems, a, (x, y, 1 - c)) for a in range(n)]
        for cp in cps:
            cp.start()
        for cp in cps:
            cp.wait()

    return _pcall(
        body, name=name, in_specs=[_ANY] * n, out_specs=[_ANY] * n,
        out_shape=[_sds(v.shape[1:], v.dtype) for v in by_chip],
        scratch_shapes=[pltpu.SemaphoreType.DMA((n,)), pltpu.SemaphoreType.DMA((n,))],
    )(*by_chip)


def _pair_result_call(fins, name):
    n = len(fins)

    def body(*refs):
        v_refs, o_refs, (send_sems, recv_sems) = refs[:n], refs[n:2 * n], refs[2 * n:]
        x, y, c = _axes()
        cps = [_remote(v_refs[a].at[c], o_refs[a].at[c], send_sems, recv_sems, a, (x, y, 1 - c)) for a in range(n)]
        for cp in cps:
            cp.start()
        for a in range(n):
            cps[a].wait_send()
            _remote(v_refs[a].at[c], o_refs[a].at[1 - c], send_sems, recv_sems, a, (x, y, 1 - c)).wait_recv()

    return _pcall(
        body, name=name, in_specs=[_ANY] * n, out_specs=[_ANY] * n,
        out_shape=[_sds(v.shape, v.dtype) for v in fins], input_output_aliases={a: a for a in range(n)},
        scratch_shapes=[pltpu.SemaphoreType.DMA((n,)), pltpu.SemaphoreType.DMA((n,))],
    )(*fins)


def _allreduce_small_call(v):
    rows, cols = v.shape

    def body(v_ref, o_ref, buf, send_sems, recv_sems):
        x, y, c = _axes()
        chip, peers = 2 * x + y, _chip_peers(x, y)
        pair = _remote(v_ref, buf.at[0], send_sems, recv_sems, 0, (x, y, 1 - c))
        pair.start()
        pair.wait()
        buf[0] = v_ref[...] + buf[0]
        cps = [_remote(buf.at[0], buf.at[1 + k], send_sems, recv_sems, 1 + k, (*peers[k], c)) for k in range(3)]
        for cp in cps:
            cp.start()
        for cp in cps:
            cp.wait()
        acc = buf[chip]
        for d in range(1, 4):
            acc = acc + buf[lax.bitwise_xor(chip, d)]
        o_ref[...] = acc

    return _pcall(
        body, name="allreduce_small", in_specs=[_VMEM], out_specs=_VMEM, out_shape=_sds(v.shape, F32),
        scratch_shapes=[pltpu.VMEM((4, rows, cols), F32), pltpu.SemaphoreType.DMA((4,)), pltpu.SemaphoreType.DMA((4,))],
        compiler_params=_cparams(),
    )(v)


def _pack_rows(arrays, dtype, total_rows=None):
    parts = []
    for a in arrays:
        flat = a.astype(dtype).reshape(-1)
        parts.append(jnp.pad(flat, (0, -flat.shape[0] % D_MODEL)).reshape(-1, D_MODEL))
    out = jnp.concatenate(parts, axis=0)
    total_rows = total_rows or out.shape[0] + (-out.shape[0] % 8)
    return jnp.pad(out, ((0, total_rows - out.shape[0]), (0, 0)))


def _unpack_rows(packed, shapes):
    out, r = [], 0
    for s in shapes:
        n = math.prod(s)
        nr = -(-n // D_MODEL)
        out.append(packed[r:r + nr].reshape(-1)[:n].reshape(s))
        r += nr
    return out


def _to_r(w_t):
    pad = jnp.zeros((N_R - N_ORIG,) + w_t.shape[1:], w_t.dtype)
    return jnp.concatenate([w_t[A_ORIG_W:BA_ORIG], w_t[:A_ORIG_W], w_t[BA_ORIG + 2 * HEADS:],
                            w_t[BA_ORIG:BA_ORIG + 2 * HEADS], pad], axis=0)


_BIG = ("w_in", "a_proj", "b_proj", "c_proj", "w_out")
_SHARD_AXIS = {"w_in": 2, "a_proj": 2, "b_proj": 1, "c_proj": 2, "w_out": 1, "a_dw": 2, "b_conv": 2}
_BIG_AXIS = _SHARD_AXIS


def _join_chips(g, axis):
    g = jnp.moveaxis(g, 0, axis)
    return g.reshape(g.shape[:axis] + (4 * g.shape[axis + 1],) + g.shape[axis + 2:])


def _split_chips(a, axis):
    n = a.shape[axis] // 4
    by_chip = jnp.moveaxis(a.reshape(a.shape[:axis] + (4, n) + a.shape[axis + 1:]), axis, 0)
    return jnp.moveaxis(by_chip.reshape(4, 2, by_chip.shape[1] // 2, by_chip.shape[2]), 1, 0)
_ORDER = ("norm_g", "w_in", "a_dw", "a_dw_b", "a_ln_g", "a_ln_b", "a_proj", "b_conv", "b_a_log", "b_dt_bias",
          "b_onorm_g", "b_proj", "c_ln_g", "c_ln_b", "c_ws", "c_bs", "c_proj", "w_out", "final_g")


def _reduce_begin(layer_grads, where, name):
    by_chip = [_split_chips(layer_grads[k], _SHARD_AXIS[k] - 1) for k in _BIG]
    theirs = _pair_partials_call(by_chip, name + "pair_partials")
    return [_add_pair_call(b, r, where, name + "pair_sum_" + k) for k, b, r in zip(_BIG, by_chip, theirs)]


def _reduce_end(sums, got, where, name):
    fins = [_add_recv_call(s[1], r, where, name + "chip_sum_" + k) for k, s, r in zip(_BIG, sums, got)]
    return [o.reshape((-1,) + o.shape[2:]) for o in _pair_result_call(fins, name + "pair_result")]


def _local_step(x, tgt, first, rest, final_g, where=None):
    bsz, t, _ = x.shape
    m = bsz * t
    depth = 1 + len(rest)
    row = lambda v: v.reshape(1, -1)
    saved, xl, w = [], x, first
    for l in range(depth):
        n = f"l{l}_"
        w_rt = _to_r(w["w_in_t"])
        par = dict(
            adw=w["a_dw"], adb=row(w["a_dw_b"]), alg=row(w["a_ln_g"]), alb=row(w["a_ln_b"]), bconv=w["b_conv"],
            alog=jnp.pad(row(w["b_a_log"]), ((0, 0), (HEADS, DK - 2 * HEADS))),
            dtb=jnp.pad(row(w["b_dt_bias"]), ((0, 0), (HEADS, DK - 2 * HEADS))), og=row(w["b_onorm_g"]),
            clg=row(w["c_ln_g"]), clb=row(w["c_ln_b"]), cws=w["c_ws"], cbias=jnp.repeat(w["c_bs"].T, SG_C, axis=1),
            ng=row(w["norm_g"]), wa=w["a_proj"], wb=w["b_proj"], wc=w["c_proj"], wo=w["w_out"], wrt=w_rt, wr=w_rt.T)
        h = _rms_fwd_call(xl.reshape(m, D_MODEL), par["ng"], n + "norm")
        proj = _matmul(h, par["wr"], F32, 2048, 1152, 1024, n + "in_proj").reshape(bsz, t, N_R)
        ya, conv = _a_fwd_call(proj, par["adw"], par["adb"], par["alg"], par["alb"], n + "conv_fwd")
        shards, assemble = rest[l] if l < len(rest) else ((), None)
        yb, *dn_saved = _dn_fwd_call(proj, par["bconv"], par["alog"], par["dtb"], par["og"], n + "delta_fwd", shards)
        dn_saved, gathered = dn_saved[:6], dn_saved[6:]
        if assemble is not None:
            w = assemble(gathered)
        yc = _c_fwd_call(proj, par["clg"], par["clb"], par["cws"], par["cbias"], n + "gmlp_fwd")
        x_next = _m_fwd_call(ya, yb, yc, proj, xl, par["wa"], par["wb"], par["wc"], par["wo"], n + "merge_fwd")
        saved.append((par, xl, h, proj, ya, yb, yc, conv, dn_saved))
        xl = x_next
    dout, dfg, loss = _loss_call(xl.reshape(m, D_MODEL), tgt.reshape(m, D_MODEL), row(final_g))
    dout = dout.reshape(bsz, t, D_MODEL)
    g = {k: [None] * depth for k in _ORDER if k != "final_g"}
    pending = None
    for l in reversed(range(depth)):
        n = f"l{l}_"
        par, xl, h, proj, ya, yb, yc, conv, dn_saved = saved[l]
        dya, dyb, dyc, dgate, merged, dps = _m_bwd_call(dout, ya, yb, yc, proj, par["wa"], par["wb"], par["wc"],
                                                        par["wo"], n + "merge_bwd")
        g["a_proj"][l], g["b_proj"][l], g["c_proj"][l], g["w_out"][l] = _m_wgrad_call(ya, yb, yc, merged, dps, dout,
                                                                                      n + "merge_wgrad")
        travel = [s[0] for s in pending[1]] if pending else []
        da, dadw, dasm, *got = _a_bwd_call(dya, conv, proj, par["adw"], par["alg"], par["alb"], n + "conv_bwd", travel)
        if pending:
            for k, r in zip(_BIG, _reduce_end(pending[1], got, where, f"l{pending[0]}_")):
                g[k][pending[0]] = r
            pending = None
        db, dba, dbconv, dbsm = _dn_bwd_call(dyb, dn_saved, proj, par["bconv"], par["alog"], par["dtb"], par["og"],
                                             n + "delta_bwd")
        dc, dcws, dcbs, dcsm = _c_bwd_call(dyc, proj, par["clg"], par["clb"], par["cws"], par["cbias"], n + "gmlp_bwd")
        segs = [s.reshape(m, s.shape[-1]) for s in (db, da, dc, dgate, dba)]
        ht = h.T
        dwb, dwa, dwc, dwg, dwba = [_matmul(ht, s, F32, 1024, 1024 if s.shape[1] % 1024 == 0 else 768, 2048,
                                            n + "in_proj_dw_" + tag) for s, tag in zip(segs, "bacgs")]
        g["w_in"][l] = jnp.concatenate([dwa, dwb, dwba[:, :2 * HEADS], dwc, dwg], axis=1)
        if where is None:
            dh = _matmul_ksegs(segs, par["wrt"], 2048, 1024, 512, n + "in_proj_dx")
        else:
            sums = _reduce_begin({k: g[k][l] for k in _BIG}, where, n)
            if l > 0:
                pending = (l, sums)
                dh = _matmul_ksegs(segs, par["wrt"], 2048, 1024, 512, n + "in_proj_dx")
            else:
                dh, got = _matmul_ksegs(segs, par["wrt"], 2048, 1024, 512, n + "in_proj_dx", [s[0] for s in sums])
                for k, r in zip(_BIG, _reduce_end(sums, got, where, n)):
                    g[k][l] = r
        dx, dng = _rms_bwd_call(xl.reshape(m, D_MODEL), dh, par["ng"], dout.reshape(m, D_MODEL), n + "norm_bwd")
        dout = dx.reshape(bsz, t, D_MODEL)
        g["norm_g"][l] = dng[0]
        g["a_dw"][l], g["a_dw_b"][l], g["a_ln_g"][l], g["a_ln_b"][l] = dadw[:CONV_K], dasm[0], dasm[1], dasm[2]
        g["b_conv"][l], g["b_a_log"][l], g["b_dt_bias"][l] = dbconv[:SHORT_K], dbsm[0, :HEADS], dbsm[1, :HEADS]
        g["b_onorm_g"][l] = dbsm[2]
        g["c_ln_g"][l], g["c_ln_b"][l], g["c_ws"][l] = dcsm[0], dcsm[1], dcws
        g["c_bs"][l] = dcbs.reshape(SG_C, SG_G, SG_C).sum(-1).T
    grads = {k: jnp.stack(v) for k, v in g.items()}
    grads["final_g"] = dfg[0]
    return loss[0, 0], dout, grads


def kernel(x, norm_g, w_in, a_dw, a_dw_b, a_ln_g, a_ln_b, a_proj, b_conv, b_a_log, b_dt_bias, b_onorm_g, b_proj, c_ln_g, c_ln_b, c_ws, c_bs, c_proj, w_out, final_g, loss_target, m_norm_g, m_w_in, m_a_dw, m_a_dw_b, m_a_ln_g, m_a_ln_b, m_a_proj, m_b_conv, m_b_a_log, m_b_dt_bias, m_b_onorm_g, m_b_proj, m_c_ln_g, m_c_ln_b, m_c_ws, m_c_bs, m_c_proj, m_w_out, m_final_g, v_norm_g, v_w_in, v_a_dw, v_a_dw_b, v_a_ln_g, v_a_ln_b, v_a_proj, v_b_conv, v_b_a_log, v_b_dt_bias, v_b_onorm_g, v_b_proj, v_c_ln_g, v_c_ln_b, v_c_ws, v_c_bs, v_c_proj, v_w_out, v_final_g):
    given = dict(locals())
    ws = {k: given[k] for k in _ORDER}
    xi, yi, ci = _axes()
    chip = 2 * xi + yi
    where = jnp.stack([ci, chip]).astype(jnp.int32)

    depth = norm_g.shape[0]
    w_t = jnp.transpose(w_in, (2, 0, 1)).astype(BF16)
    sharded = ("w_in_t",) + _BIG[1:] + ("a_dw", "b_conv")

    def rows_to(a, n):
        return jnp.pad(a, ((0, n - a.shape[0]), (0, 0)))

    def shards(l):
        return ([rows_to(w_t[:, l], GATHER_ROWS)] + [ws[k][l].astype(BF16) for k in _BIG[1:]]
                + [rows_to(a_dw[l], HALO_A), rows_to(b_conv[l], 16)])

    def assemble(l):
        def full_weights(gathered):
            lw = {k: ws[k][l] for k in _ORDER if k not in sharded and k not in ("w_in", "final_g")}
            for k, g in zip(sharded, gathered):
                lw[k] = _join_chips(g[:, :w_t.shape[0]] if k == "w_in_t" else g, 0 if k == "w_in_t" else _SHARD_AXIS[k] - 1)
            lw["b_conv"] = lw["b_conv"][:SHORT_K]
            return lw
        return full_weights

    first = assemble(0)(_gather_weights_call(shards(0)))

    loss, grad_x, grads = _local_step(x, loss_target, first, [(shards(l), assemble(l)) for l in range(1, depth)],
                                      final_g, where)
    out_g = {k: grads[k] for k in _BIG}

    rest = [k for k in _ORDER if k not in _BIG]
    rest_shapes = [grads[k].shape for k in rest] + [(1,)]
    summed = _unpack_rows(_allreduce_small_call(_pack_rows([grads[k] for k in rest] + [loss.reshape(1)], F32)),
                          rest_shapes)
    out_g.update(zip(rest, summed[:-1]))
    out_g["a_dw"] = lax.dynamic_slice_in_dim(out_g["a_dw"], chip * a_dw.shape[2], a_dw.shape[2], axis=2)
    out_g["b_conv"] = lax.dynamic_slice_in_dim(out_g["b_conv"], chip * b_conv.shape[2], b_conv.shape[2], axis=2)

    def adam(k):
        operands = (ws[k], out_g[k], given["m_" + k], given["v_" + k])
        if k != "w_in":
            return _adamw_call(*operands, "adamw_" + k)
        w_t, g_t, m_t, v_t = (jnp.transpose(a, (2, 0, 1)) for a in operands)
        out_g[k] = jnp.transpose(g_t, (1, 2, 0))
        return tuple(jnp.transpose(o, (1, 2, 0)) for o in _adamw_call(w_t, g_t, m_t, v_t, "adamw_" + k))

    upd = {k: adam(k) for k in _ORDER}
    return (summed[-1][0], grad_x, *[out_g[k] for k in _ORDER], *[upd[k][0] for k in _ORDER],
            *[upd[k][1] for k in _ORDER], *[upd[k][2] for k in _ORDER])
```

```python
import functools
import math

import jax
import jax.numpy as jnp
from jax import lax
from jax.experimental import pallas as pl
from jax.experimental.pallas import tpu as pltpu

F32 = jnp.float32
BF16 = jnp.bfloat16
MESH = pl.DeviceIdType.MESH

D_MODEL = 1024
CONV_W = 512
CONV_K = 31
HEADS = 8
DK = 128
SHORT_K = 4
CHUNK = 64
SG_W = 512
SG_G = 4
SG_C = 128
EPS = 1e-6
N_ORIG = 10256
BA_ORIG = 5632
A_ORIG_W = 3 * CONV_W
B_W = 4 * D_MODEL
B_OFF, A_OFF, C_OFF, G_OFF, BA_OFF, N_R = 0, 4096, 5632, 7168, 10240, 10368
TB = 256
HALO_A = 32
HALO_B = 8
VMEM_LIMIT = 56 * 1024 * 1024
GATHER_ROWS = 2624
ADAM_LR, ADAM_B1, ADAM_B2, ADAM_EPS, ADAM_WD, ADAM_STEP = 0.001, 0.9, 0.999, 1e-08, 0.01, 10


def _pcall(body, **kw):
    return pl.pallas_call(body, **kw)


def _cparams(sem=None):
    kw = dict(vmem_limit_bytes=VMEM_LIMIT)
    if sem is not None:
        kw["dimension_semantics"] = sem
    return pltpu.CompilerParams(**kw)


def _sig(x):
    return jax.nn.sigmoid(x)


def _silu(x):
    return x * _sig(x)


def _dsilu(x):
    s = _sig(x)
    return s * (1.0 + x * (1.0 - s))


_GELU_C = math.sqrt(2.0 / math.pi)


def _gelu(x):
    return 0.5 * x * (1.0 + jnp.tanh(_GELU_C * (x + 0.044715 * x * x * x)))


def _dgelu(x):
    t = jnp.tanh(_GELU_C * (x + 0.044715 * x * x * x))
    return 0.5 * (1.0 + t) + 0.5 * x * (1.0 - t * t) * _GELU_C * (1.0 + 3 * 0.044715 * x * x)


def _softplus(x):
    return jnp.maximum(x, 0.0) + jnp.log1p(jnp.exp(-jnp.abs(x)))


def _dot(a, b, dims):
    return lax.dot_general(a.astype(BF16), b.astype(BF16), (dims, ((), ())), preferred_element_type=F32)


def _nn(a, b):
    return _dot(a, b, ((1,), (0,)))


def _nt(a, b):
    return _dot(a, b, ((1,), (1,)))


def _tn(a, b):
    return _dot(a, b, ((0,), (0,)))


def _tn_mxu(a, b):
    n = a.shape[1]
    eye = (lax.broadcasted_iota(jnp.int32, (n, n), 0) == lax.broadcasted_iota(jnp.int32, (n, n), 1)).astype(BF16)
    return _nn(_nt(eye, a), b)


def _mean(x):
    return jnp.mean(x, axis=-1, keepdims=True)


def _sum0(x):
    return jnp.sum(x, axis=0, keepdims=True)


def _sum1(x):
    return jnp.sum(x, axis=1, keepdims=True)


def _ln_fwd(x, g, b):
    xc = x - _mean(x)
    r = lax.rsqrt(_mean(xc * xc) + EPS)
    xh = xc * r
    return xh * g + b, xh, r


def _ln_bwd(dy, xh, r, g):
    dxh = dy * g
    return r * (dxh - _mean(dxh) - xh * _mean(dxh * xh)), _sum0(dy * xh), _sum0(dy)


def _rms_bwd(dxh, xh, r):
    return r * (dxh - xh * _mean(dxh * xh))


def _rows8(*rows):
    n = rows[0].shape[1]
    return jnp.concatenate(list(rows) + [jnp.zeros((8 - len(rows), n), F32)], axis=0)


def _windows(ext, tb):
    n = ext.shape[0] - 8
    shifted = {0: ext}

    def window(off):
        r = off % 8
        if r not in shifted:
            shifted[r] = ext[r:r + n]
        return shifted[r][off - r:off - r + tb]

    return window


def _a_fwd(val, glu, az, val_h, glu_h, w, bias, g, b):
    tb = val.shape[0]
    win = _windows(jnp.concatenate([val_h * _sig(glu_h), val * _sig(glu)], axis=0), tb)
    c = win(2) * w[0:1]
    for j in range(1, CONV_K):
        c = c + win(2 + j) * w[j:j + 1]
    c = c + bias
    ln, _, _ = _ln_fwd(c, g, b)
    return _silu(ln) * _silu(az), c


def _a_bwd(dy, val, glu, az, c, w, g, b, dc_next):
    tb = val.shape[0]
    ln, xh, r = _ln_fwd(c, g, b)
    sl, sz = _sig(ln), _sig(az)
    dln = dy * (az * sz) * (sl * (1.0 + ln * (1.0 - sl)))
    daz = dy * (ln * sl) * (sz * (1.0 + az * (1.0 - sz)))
    dc, dg, db = _ln_bwd(dln, xh, r, g)
    win = _windows(jnp.concatenate([dc, dc_next], axis=0), tb)
    sg = _sig(glu)
    a = val * sg
    da = win(30) * w[0:1]
    dw_rows = [_sum0(win(30) * a)]
    for j in range(1, CONV_K):
        shifted = win(30 - j)
        da = da + shifted * w[j:j + 1]
        dw_rows.append(_sum0(shifted * a))
    dw = jnp.concatenate(dw_rows + [jnp.zeros((1, CONV_W), F32)], axis=0)
    return da * sg, da * val * sg * (1.0 - sg), daz, dw, _sum0(dc), dg, db, dc[:HALO_A]


def _tril(ws):
    ii = lax.broadcasted_iota(jnp.int32, (SG_C, SG_C), 0)
    jj = lax.broadcasted_iota(jnp.int32, (SG_C, SG_C), 1)
    return [jnp.where(jj <= ii, ws[gi], 0.0) for gi in range(SG_G)], jj <= ii


def _c_mix(wt, vs, bias_full):
    tb = vs.shape[0]
    rows = []
    for n in range(tb // SG_C):
        blks = [_nn(wt[gi], vs[n * SG_C:(n + 1) * SG_C, gi * SG_C:(gi + 1) * SG_C]) for gi in range(SG_G)]
        rows.append(jnp.concatenate(blks, axis=1) + bias_full)
    return jnp.concatenate(rows, axis=0)


def _c_fwd(cu, cv, cz, g, b, ws, bias_full):
    wt, _ = _tril(ws)
    vs, xh, r = _ln_fwd(_gelu(cv), g, b)
    mixed = _c_mix(wt, vs, bias_full)
    return _gelu(cu) * mixed * _silu(cz), (wt, vs, xh, r, mixed)


def _c_bwd(dy, cu, cv, cz, g, b, ws, bias_full):
    tb = cu.shape[0]
    _, (wt, vs, xh, r, mixed) = _c_fwd(cu, cv, cz, g, b, ws, bias_full)
    _, low = _tril(ws)
    u, sz = _gelu(cu), _silu(cz)
    dcu = dy * mixed * sz * _dgelu(cu)
    dcz = dy * u * mixed * _dsilu(cz)
    dmixed = dy * u * sz
    dbs = jnp.zeros((SG_C, SG_W), F32)
    dws = [jnp.zeros((SG_C, SG_C), F32) for _ in range(SG_G)]
    rows = []
    for n in range(tb // SG_C):
        dm_n = dmixed[n * SG_C:(n + 1) * SG_C]
        dbs = dbs + dm_n
        blks = []
        for gi in range(SG_G):
            dm = dm_n[:, gi * SG_C:(gi + 1) * SG_C]
            dws[gi] = dws[gi] + _nt(dm, vs[n * SG_C:(n + 1) * SG_C, gi * SG_C:(gi + 1) * SG_C])
            blks.append(_tn(wt[gi], dm))
        rows.append(jnp.concatenate(blks, axis=1))
    dvs = jnp.concatenate(rows, axis=0)
    dgv, dg, db = _ln_bwd(dvs, xh, r, g)
    dws = [jnp.where(low, d, 0.0) for d in dws]
    return dcu, dgv * _dgelu(cv), dcz, dws, dbs, dg, db


def _m_fwd(ya, yb, yc, g0, g1, g2, wa, wb, wc):
    pa, pb, pc = _nn(ya, wa), _nn(yb, wb), _nn(yc, wc)
    s0, s1, s2 = _sig(g0), _sig(g1), _sig(g2)
    return s0 * pa + s1 * pb + s2 * pc, (pa, pb, pc, s0, s1, s2)


def _m_bwd(dout, ya, yb, yc, g0, g1, g2, wa, wb, wc, wo):
    merged, (pa, pb, pc, s0, s1, s2) = _m_fwd(ya, yb, yc, g0, g1, g2, wa, wb, wc)
    dm = _nt(dout, wo)
    dpa, dpb, dpc = dm * s0, dm * s1, dm * s2
    dgs = (dm * pa * s0 * (1.0 - s0), dm * pb * s1 * (1.0 - s1), dm * pc * s2 * (1.0 - s2))
    return (_nt(dpa, wa), _nt(dpb, wb), _nt(dpc, wc)), dgs, merged, (dpa, dpb, dpc)


def _chunk_masks(c):
    ii = lax.broadcasted_iota(jnp.int32, (c, c), 0)
    jj = lax.broadcasted_iota(jnp.int32, (c, c), 1)
    return ii, jj


def _dn_decay(items):
    ii, jj = _chunk_masks(CHUNK)
    incl, strict, eye = jj <= ii, jj < ii, ii == jj
    for d in items:
        g = d["g"]
        grow = _sum0(jnp.where(eye, g, 0.0))
        gc_col = _sum1(jnp.where(incl, grow, 0.0))
        gc_row = _sum0(jnp.where(ii <= jj, g, 0.0))
        gam_i = jnp.where(incl, jnp.exp(jnp.where(incl, gc_col - gc_row, 0.0)), 0.0)
        gl = _sum0(g)
        egc = jnp.exp(gc_col)
        d.update(gam_i=gam_i, gam_s=jnp.where(strict, gam_i, 0.0), egc=egc, ekd=jnp.exp(gl - gc_col), dl=jnp.exp(gl),
                 gdiff=gc_col - gc_row, qd=d["q"] * egc, rhs_w=d["k"] * (d["b"] * egc))
        d["kd"] = d["k"] * d["ekd"]
    return ii, jj, strict, eye


def _dn_solve(items, ii, jj, eye):
    off = ((ii >> 1) == (jj >> 1)) & ((ii & 1) != 0) & ((jj & 1) == 0)
    for d in items:
        a1 = jnp.where(off, d["a"], 0.0)
        d["t"] = jnp.where(eye, 1.0, 0.0) - a1
        d["m"] = d["a"] - _nn(a1, d["a"])
    b, sh = 2, 2
    while b < CHUNK:
        off = ((ii >> sh) == (jj >> sh)) & ((ii & b) != 0) & ((jj & b) == 0)
        for d in items:
            mo = jnp.where(off, d["m"], 0.0)
            if 2 * b < CHUNK:
                d["m"], d["t"] = d["m"] - _nn(mo, d["m"]), d["t"] - _nn(mo, d["t"])
            else:
                d["t"] = d["t"] - _nn(mo, d["t"])
        b, sh = 2 * b, sh + 1


def _dn_fwd_chunk(items, ss):
    ii, jj, strict, eye = _dn_decay(items)
    for d in items:
        d["a"] = d["b"] * _nt(d["k"], d["k"]) * d["gam_s"]
        d["qk"] = _nt(d["q"], d["k"]) * d["gam_i"]
    _dn_solve(items, ii, jj, eye)
    for d in items:
        d["u"], d["w"] = _nn(d["t"], d["v"] * d["b"]), _nn(d["t"], d["rhs_w"])
    ws = [_nn(d["w"], s) for d, s in zip(items, ss)]
    qs = [_nn(d["qd"], s) for d, s in zip(items, ss)]
    vnew = [d["u"] - w for d, w in zip(items, ws)]
    outs = [q + _nn(d["qk"], vn) for d, q, vn in zip(items, qs, vnew)]
    ss = [d["dl"] * s + _tn(d["kd"], vn) for d, s, vn in zip(items, ss, vnew)]
    return outs, ss, vnew


def _dn_bwd_chunk(items, ss, dss):
    ii, jj, strict, eye = _dn_decay(items)
    eye_b = eye.astype(BF16)
    for d in items:
        k, q = d["k"], d["q"]
        d["kk"] = _nt(k, k)
        d["a"] = d["b"] * d["kk"] * d["gam_s"]
        d["qk"] = _nt(q, k) * d["gam_i"]
        d["qkt"] = _nt(k, q) * jnp.where(ii <= jj, jnp.exp(jnp.where(ii <= jj, -d["gdiff"], 0.0)), 0.0)
    for d, s, ds2 in zip(items, ss, dss):
        d["dvnew"] = _nn(d["qkt"], d["do"]) + _nn(d["kd"], ds2)
        d["dqk"] = _nt(d["do"], d["vnew"])
        d["dqd"] = _nt(d["do"], s)
        d["dkd"] = _nt(d["vnew"], ds2)
    new_dss = []
    for d, s, ds2 in zip(items, ss, dss):
        d["dw"] = -_nt(d["dvnew"], s)
        new_dss.append(_tn_mxu(d["qd"], d["do"]) - _tn_mxu(d["w"], d["dvnew"]) + d["dl"] * ds2)
        d["ddl"] = _sum0(_sum1(s * ds2))
    for d in items:
        tt = _nt(eye_b, d["t"])
        d["drhs_u"], d["drhs_w"] = _nn(tt, d["dvnew"]), _nn(tt, d["dw"])
    for d in items:
        d["da"] = jnp.where(strict, -(_nt(d["drhs_u"], d["u"]) + _nt(d["drhs_w"], d["w"])), 0.0)
    outs = []
    for d in items:
        q, k, v, b, egc = d["q"], d["k"], d["v"], d["b"], d["egc"]
        drhs_u, drhs_w, da = d["drhs_u"], d["drhs_w"], d["da"]
        dbeta = _sum1(da * d["kk"] * d["gam_s"]) + _sum1(drhs_u * v) + _sum1(drhs_w * k) * egc
        dkk = da * b * d["gam_s"]
        e = da * d["a"] + d["dqk"] * d["qk"]
        s_kd = _sum1(d["dkd"] * d["kd"])
        dgc_col = _sum1(e) + _sum1(drhs_w * d["rhs_w"]) + _sum1(d["dqd"] * d["qd"]) - s_kd
        dgc_row = _sum0(jnp.where(eye, dgc_col, 0.0)) - _sum0(e)
        dg = _sum1(jnp.where(jj >= ii, dgc_row, 0.0)) + (_sum0(s_kd) + d["ddl"] * d["dl"])
        dqkg = d["dqk"] * d["gam_i"]
        dq = _nn(dqkg, k) + d["dqd"] * egc
        dk = _tn_mxu(dqkg, q) + _nn(dkk, k) + _tn_mxu(dkk, k) + drhs_w * (b * egc) + d["dkd"] * d["ekd"]
        outs.append((dq, dk, drhs_u * b, dbeta, dg))
    return outs, new_dss


def _short_conv(raw, halo, w):
    tb = raw.shape[0]
    ext = jnp.concatenate([halo, raw], axis=0)
    out = ext[5:5 + tb] * w[0:1]
    for j in range(1, SHORT_K):
        out = out + ext[5 + j:5 + j + tb] * w[j:j + 1]
    return out


def _dn_gates(ba, alog8, dtb8):
    xg = ba + dtb8
    ea8 = jnp.exp(alog8)
    return _sig(ba), -ea8 * _softplus(xg), xg, ea8


def _dn_item(pre_q, pre_k, pre_v, b, g):
    qc, kc, vc = (p * _sig(p) for p in (pre_q, pre_k, pre_v))
    nq = lax.rsqrt(_sum1(qc * qc) + EPS)
    nk = lax.rsqrt(_sum1(kc * kc) + EPS)
    return dict(q=qc * nq * (DK ** -0.5), k=kc * nk, v=vc, b=b, g=g, qc=qc, kc=kc, nq=nq, nk=nk)


def _dn_out(o, z, sz, og):
    r = lax.rsqrt(_mean(o * o) + EPS)
    xh = o * r
    return xh * og * (z * sz), xh, r


def _sds(shape, dtype):
    return jax.ShapeDtypeStruct(tuple(shape), dtype)


def _matmul(a, b, out_dtype, tm, tn, tk, name):
    m, kd = a.shape
    n = b.shape[1]
    tm, tn, tk = min(tm, m), min(tn, n), min(tk, kd)
    nk = kd // tk

    def body(a_ref, b_ref, o_ref, acc):
        @pl.when(pl.program_id(2) == 0)
        def _():
            acc[...] = jnp.zeros_like(acc)

        acc[...] += jnp.dot(a_ref[...], b_ref[...], preferred_element_type=F32)

        @pl.when(pl.program_id(2) == nk - 1)
        def _():
            o_ref[...] = acc[...].astype(o_ref.dtype)

    return _pcall(
        body, name=name, grid=(m // tm, n // tn, nk),
        in_specs=[pl.BlockSpec((tm, tk), lambda i, j, k: (i, k)), pl.BlockSpec((tk, tn), lambda i, j, k: (k, j))],
        out_specs=pl.BlockSpec((tm, tn), lambda i, j, k: (i, j)), out_shape=_sds((m, n), out_dtype),
        scratch_shapes=[pltpu.VMEM((tm, tn), F32)], compiler_params=_cparams(("parallel", "parallel", "arbitrary")),
    )(a, b)


def _travel_copies(v_refs, o_refs, send_sems, recv_sems):
    x, y, c = _axes()
    peers = _chip_peers(x, y)
    return [_remote(v_refs[a].at[2 * peers[k][0] + peers[k][1]], o_refs[a].at[k], send_sems, recv_sems, 3 * a + k,
                    (*peers[k], c)) for k in range(3) for a in range(len(v_refs))]


def _matmul_ksegs(a_segs, b, tm, tn, tk, name, travel=()):
    m, n = a_segs[0].shape[0], b.shape[1]
    tm, tn = min(tm, m), min(tn, n)
    main, tail = a_segs[:-1], a_segs[-1]
    steps = [s.shape[1] // tk for s in main]
    starts = [sum(steps[:i]) for i in range(len(main))]
    nk = sum(steps)
    wt = tail.shape[1]
    n_main, nv = len(main), len(travel)
    grid = (m // tm, n // tn, nk + 1)

    def body(*refs):
        a_refs, (at_ref, b_ref, bt_ref), refs = refs[:n_main], refs[n_main:n_main + 3], refs[n_main + 3:]
        v_refs, o_ref, refs = refs[:nv], refs[nv], refs[nv + 1:]
        got_refs, acc, sems = refs[:nv], refs[nv], refs[nv + 1:]
        k = pl.program_id(2)
        if nv:
            first = (pl.program_id(0) == 0) & (pl.program_id(1) == 0) & (k == 0)
            last = (pl.program_id(0) == grid[0] - 1) & (pl.program_id(1) == grid[1] - 1) & (k == nk)

            @pl.when(first)
            def _():
                for cp in _travel_copies(v_refs, got_refs, *sems):
                    cp.start()

        @pl.when(k == 0)
        def _():
            acc[...] = jnp.zeros_like(acc)

        for a_ref, k0, ns in zip(a_refs, starts, steps):
            @pl.when((k >= k0) & (k < k0 + ns))
            def _():
                acc[...] += jnp.dot(a_ref[...], b_ref[...], preferred_element_type=F32)

        @pl.when(k == nk)
        def _():
            o_ref[...] = acc[...] + jnp.dot(at_ref[...], bt_ref[...], preferred_element_type=F32)

        if nv:
            @pl.when(last)
            def _():
                for cp in _travel_copies(v_refs, got_refs, *sems):
                    cp.wait()

    seg_specs = [pl.BlockSpec((tm, tk), functools.partial(lambda i, j, k, k0, n_s: (i, jnp.clip(k - k0, 0, n_s - 1)),
                                                          k0=k0, n_s=n_s)) for k0, n_s in zip(starts, steps)]
    out = _pcall(
        body, name=name, grid=grid,
        in_specs=seg_specs + [pl.BlockSpec((tm, wt), lambda i, j, k: (i, 0)),
                              pl.BlockSpec((tk, tn), lambda i, j, k: (jnp.minimum(k, nk - 1), j)),
                              pl.BlockSpec((wt, tn), lambda i, j, k: (nk * tk // wt, j))] + [_ANY] * nv,
        out_specs=[pl.BlockSpec((tm, tn), lambda i, j, k: (i, j))] + [_ANY] * nv,
        out_shape=[_sds((m, n), F32)] + [_sds((3,) + v.shape[1:], v.dtype) for v in travel],
        scratch_shapes=[pltpu.VMEM((tm, tn), F32)] + [pltpu.SemaphoreType.DMA((3 * nv,))] * (2 if nv else 0),
        compiler_params=_cparams(("arbitrary",) * 3 if nv else ("parallel", "parallel", "arbitrary")),
    )(*main, tail, b, b, *travel)
    return (out[0], out[1:]) if nv else out[0]


def _rms_fwd_call(x, g, name):
    m = x.shape[0]
    tm = min(512, m)

    def body(x_ref, g_ref, o_ref):
        xv = x_ref[...]
        o_ref[...] = (xv * lax.rsqrt(_mean(xv * xv) + EPS) * g_ref[...]).astype(BF16)

    return _pcall(
        body, name=name, grid=(m // tm,),
        in_specs=[pl.BlockSpec((tm, D_MODEL), lambda i: (i, 0)), pl.BlockSpec((1, D_MODEL), lambda i: (0, 0))],
        out_specs=pl.BlockSpec((tm, D_MODEL), lambda i: (i, 0)), out_shape=_sds((m, D_MODEL), BF16),
        compiler_params=_cparams(("parallel",)),
    )(x, g)


def _rms_bwd_call(x, dh, g, dres, name):
    m = x.shape[0]
    tm = min(512, m)

    def body(x_ref, dh_ref, g_ref, dr_ref, dx_ref, dg_ref):
        @pl.when(pl.program_id(0) == 0)
        def _():
            dg_ref[...] = jnp.zeros_like(dg_ref)

        xv, dhv = x_ref[...], dh_ref[...]
        r = lax.rsqrt(_mean(xv * xv) + EPS)
        xh = xv * r
        dx_ref[...] = _rms_bwd(dhv * g_ref[...], xh, r) + dr_ref[...]
        dg_ref[...] += _rows8(_sum0(dhv * xh))

    row = pl.BlockSpec((tm, D_MODEL), lambda i: (i, 0))
    return _pcall(
        body, name=name, grid=(m // tm,),
        in_specs=[row, row, pl.BlockSpec((1, D_MODEL), lambda i: (0, 0)), row],
        out_specs=[row, pl.BlockSpec((8, D_MODEL), lambda i: (0, 0))],
        out_shape=[_sds((m, D_MODEL), F32), _sds((8, D_MODEL), F32)], compiler_params=_cparams(("arbitrary",)),
    )(x, dh, g, dres)


def _loss_call(x, tgt, g):
    m = x.shape[0]
    tm = min(512, m)

    def body(x_ref, t_ref, g_ref, dx_ref, dg_ref, l_ref):
        @pl.when(pl.program_id(0) == 0)
        def _():
            dg_ref[...] = jnp.zeros_like(dg_ref)
            l_ref[...] = jnp.zeros_like(l_ref)

        xv = x_ref[...]
        r = lax.rsqrt(_mean(xv * xv) + EPS)
        xh = xv * r
        err = xh * g_ref[...] - t_ref[...]
        dy = err * (1.0 / D_MODEL)
        dx_ref[...] = _rms_bwd(dy * g_ref[...], xh, r)
        dg_ref[...] += _rows8(_sum0(dy * xh))
        l_ref[...] += 0.5 * _sum0(_mean(err * err))

    row = pl.BlockSpec((tm, D_MODEL), lambda i: (i, 0))
    return _pcall(
        body, name="loss_head", grid=(m // tm,),
        in_specs=[row, row, pl.BlockSpec((1, D_MODEL), lambda i: (0, 0))],
        out_specs=[row, pl.BlockSpec((8, D_MODEL), lambda i: (0, 0)), pl.BlockSpec((8, 128), lambda i: (0, 0))],
        out_shape=[_sds((m, D_MODEL), F32), _sds((8, D_MODEL), F32), _sds((8, 128), F32)],
        compiler_params=_cparams(("arbitrary",)),
    )(x, tgt, g)


def _halo_idx(i, rows):
    return jnp.maximum(i * (TB // rows) - 1, 0)


def _a_tiles(nt, rev):
    ti = (lambda i: nt - 1 - i) if rev else (lambda i: i)
    c0 = A_OFF // CONV_W
    return [pl.BlockSpec((1, TB, CONV_W), functools.partial(lambda b, i, c: (b, ti(i), c), c=c0 + c)) for c in range(3)]


def _a_fwd_call(proj, w, bias, g, b, name):
    bsz, t, _ = proj.shape
    nt = t // TB
    c0 = A_OFF // CONV_W

    def body(val_ref, glu_ref, az_ref, vh_ref, gh_ref, w_ref, bias_ref, g_ref, b_ref, y_ref, c_ref):
        keep = jnp.where(pl.program_id(1) > 0, 1.0, 0.0)
        y, c = _a_fwd(val_ref[0], glu_ref[0], az_ref[0], vh_ref[0] * keep, gh_ref[0], w_ref[...], bias_ref[...],
                      g_ref[...], b_ref[...])
        y_ref[0] = y.astype(BF16)
        c_ref[0] = c

    halo = [pl.BlockSpec((1, HALO_A, CONV_W), functools.partial(lambda b, i, c: (b, _halo_idx(i, HALO_A), c), c=c0 + c))
            for c in range(2)]
    par = [pl.BlockSpec((HALO_A, CONV_W), lambda b, i: (0, 0))] + [pl.BlockSpec((1, CONV_W), lambda b, i: (0, 0))] * 3
    tile = pl.BlockSpec((1, TB, CONV_W), lambda b, i: (b, i, 0))
    return _pcall(
        body, name=name, grid=(bsz, nt), in_specs=_a_tiles(nt, False) + halo + par, out_specs=[tile, tile],
        out_shape=[_sds((bsz, t, CONV_W), BF16), _sds((bsz, t, CONV_W), F32)],
        compiler_params=_cparams(("parallel", "parallel")),
    )(proj, proj, proj, proj, proj, w, bias, g, b)


def _a_bwd_call(dy, conv, proj, w, g, b, name, travel=()):
    bsz, t, _ = proj.shape
    nt = t // TB
    nv = len(travel)

    def body(*refs):
        (dy_ref, c_ref, val_ref, glu_ref, az_ref, w_ref, g_ref, b_ref), refs = refs[:8], refs[8:]
        v_refs, (da_ref, dw_ref, ds_ref), refs = refs[:nv], refs[nv:nv + 3], refs[nv + 3:]
        o_refs, carry, sems = refs[:nv], refs[nv], refs[nv + 1:]
        ip = pl.program_id(1)

        def exchange():
            return _travel_copies(v_refs, o_refs, *sems)

        @pl.when((pl.program_id(0) == 0) & (ip == 0))
        def _():
            dw_ref[...] = jnp.zeros_like(dw_ref)
            ds_ref[...] = jnp.zeros_like(ds_ref)
            if nv:
                for cp in exchange():
                    cp.start()

        @pl.when(ip == 0)
        def _():
            carry[...] = jnp.zeros_like(carry)

        dval, dglu, daz, dw, dbias, dg, db, head = _a_bwd(dy_ref[0], val_ref[0], glu_ref[0], az_ref[0], c_ref[0],
                                                          w_ref[...], g_ref[...], b_ref[...], carry[...])
        carry[...] = head
        for n, dcol in enumerate((dval, dglu, daz)):
            da_ref[0, :, n * CONV_W:(n + 1) * CONV_W] = dcol.astype(BF16)
        dw_ref[...] += dw
        ds_ref[...] += _rows8(dbias, dg, db)

        if nv:
            @pl.when((pl.program_id(0) == bsz - 1) & (ip == nt - 1))
            def _():
                for cp in exchange():
                    cp.wait()

    rtile = lambda b, i: (b, nt - 1 - i, 0)
    par = [pl.BlockSpec((HALO_A, CONV_W), lambda b, i: (0, 0))] + [pl.BlockSpec((1, CONV_W), lambda b, i: (0, 0))] * 2
    return _pcall(
        body, name=name, grid=(bsz, nt),
        in_specs=[pl.BlockSpec((1, TB, CONV_W), rtile)] * 2 + _a_tiles(nt, True) + par + [_ANY] * nv,
        out_specs=[pl.BlockSpec((1, TB, 3 * CONV_W), rtile), pl.BlockSpec((HALO_A, CONV_W), lambda b, i: (0, 0)),
                   pl.BlockSpec((8, CONV_W), lambda b, i: (0, 0))] + [_ANY] * nv,
        out_shape=[_sds((bsz, t, 3 * CONV_W), BF16), _sds((HALO_A, CONV_W), F32), _sds((8, CONV_W), F32)]
        + [_sds((3,) + v.shape[1:], v.dtype) for v in travel],
        scratch_shapes=[pltpu.VMEM((HALO_A, CONV_W), F32)] + [pltpu.SemaphoreType.DMA((3 * nv,))] * (2 if nv else 0),
        compiler_params=_cparams(("arbitrary", "arbitrary")),
    )(dy, conv, proj, proj, proj, w, g, b, *travel)


def _c_specs():
    c0 = C_OFF // SG_W
    tile = [pl.BlockSpec((1, TB, SG_W), functools.partial(lambda b, i, c: (b, i, c), c=c0 + c)) for c in range(3)]
    par = [pl.BlockSpec((1, SG_W), lambda b, i: (0, 0))] * 2 + [
        pl.BlockSpec((SG_G, SG_C, SG_C), lambda b, i: (0, 0, 0)), pl.BlockSpec((SG_C, SG_W), lambda b, i: (0, 0))]
    return tile + par


def _c_fwd_call(proj, g, b, ws, bias_full, name):
    bsz, t, _ = proj.shape

    def body(cu_ref, cv_ref, cz_ref, g_ref, b_ref, ws_ref, bf_ref, y_ref):
        y, _ = _c_fwd(cu_ref[0], cv_ref[0], cz_ref[0], g_ref[...], b_ref[...], ws_ref[...], bf_ref[...])
        y_ref[0] = y.astype(BF16)

    return _pcall(
        body, name=name, grid=(bsz, t // TB), in_specs=_c_specs(),
        out_specs=pl.BlockSpec((1, TB, SG_W), lambda b, i: (b, i, 0)), out_shape=_sds((bsz, t, SG_W), BF16),
        compiler_params=_cparams(("parallel", "parallel")),
    )(proj, proj, proj, g, b, ws, bias_full)


def _c_bwd_call(dy, proj, g, b, ws, bias_full, name):
    bsz, t, _ = proj.shape

    def body(dy_ref, cu_ref, cv_ref, cz_ref, g_ref, b_ref, ws_ref, bf_ref, dc_ref, dws_ref, dbs_ref, ds_ref):
        @pl.when((pl.program_id(0) == 0) & (pl.program_id(1) == 0))
        def _():
            dws_ref[...] = jnp.zeros_like(dws_ref)
            dbs_ref[...] = jnp.zeros_like(dbs_ref)
            ds_ref[...] = jnp.zeros_like(ds_ref)

        dcu, dcv, dcz, dws, dbs, dg, db = _c_bwd(dy_ref[0], cu_ref[0], cv_ref[0], cz_ref[0], g_ref[...], b_ref[...],
                                                 ws_ref[...], bf_ref[...])
        for n, dcol in enumerate((dcu, dcv, dcz)):
            dc_ref[0, :, n * SG_W:(n + 1) * SG_W] = dcol.astype(BF16)
        for gi in range(SG_G):
            dws_ref[gi] += dws[gi]
        dbs_ref[...] += dbs
        ds_ref[...] += _rows8(dg, db)

    tile = lambda b, i: (b, i, 0)
    return _pcall(
        body, name=name, grid=(bsz, t // TB), in_specs=[pl.BlockSpec((1, TB, SG_W), tile)] + _c_specs(),
        out_specs=[pl.BlockSpec((1, TB, 3 * SG_W), tile), pl.BlockSpec((SG_G, SG_C, SG_C), lambda b, i: (0, 0, 0)),
                   pl.BlockSpec((SG_C, SG_W), lambda b, i: (0, 0)), pl.BlockSpec((8, SG_W), lambda b, i: (0, 0))],
        out_shape=[_sds((bsz, t, 3 * SG_W), BF16), _sds((SG_G, SG_C, SG_C), F32), _sds((SG_C, SG_W), F32),
                   _sds((8, SG_W), F32)],
        compiler_params=_cparams(("arbitrary", "arbitrary")),
    )(dy, proj, proj, proj, g, b, ws, bias_full)


def _m_specs():
    g0 = G_OFF // D_MODEL
    y = [pl.BlockSpec((1, TB, n), lambda b, i: (b, i, 0)) for n in (CONV_W, D_MODEL, SG_W)]
    gates = [pl.BlockSpec((1, TB, D_MODEL), functools.partial(lambda b, i, c: (b, i, c), c=g0 + c)) for c in range(3)]
    return y + gates


def _w_specs(*shapes):
    return [pl.BlockSpec(s, lambda b, i: (0, 0)) for s in shapes]


def _m_fwd_call(ya, yb, yc, proj, x, wa, wb, wc, wo, name):
    bsz, t, _ = x.shape

    def body(ya_ref, yb_ref, yc_ref, g0_ref, g1_ref, g2_ref, x_ref, wa_ref, wb_ref, wc_ref, wo_ref, o_ref):
        merged, _ = _m_fwd(ya_ref[0], yb_ref[0], yc_ref[0], g0_ref[0], g1_ref[0], g2_ref[0], wa_ref[...], wb_ref[...],
                           wc_ref[...])
        o_ref[0] = x_ref[0] + _nn(merged, wo_ref[...])

    tile = pl.BlockSpec((1, TB, D_MODEL), lambda b, i: (b, i, 0))
    return _pcall(
        body, name=name, grid=(bsz, t // TB),
        in_specs=_m_specs() + [tile] + _w_specs(wa.shape, wb.shape, wc.shape, wo.shape),
        out_specs=tile, out_shape=_sds(x.shape, F32), compiler_params=_cparams(("parallel", "parallel")),
    )(ya, yb, yc, proj, proj, proj, x, wa, wb, wc, wo)


def _m_bwd_call(dout, ya, yb, yc, proj, wa, wb, wc, wo, name):
    bsz, t, _ = dout.shape

    def body(do_ref, ya_ref, yb_ref, yc_ref, g0_ref, g1_ref, g2_ref, wa_ref, wb_ref, wc_ref, wo_ref, dya_ref, dyb_ref,
             dyc_ref, dg_ref, mg_ref, dp_ref):
        dys, dgs, merged, dps = _m_bwd(do_ref[0], ya_ref[0], yb_ref[0], yc_ref[0], g0_ref[0], g1_ref[0], g2_ref[0],
                                       wa_ref[...], wb_ref[...], wc_ref[...], wo_ref[...])
        dya_ref[0], dyb_ref[0], dyc_ref[0] = dys
        mg_ref[0] = merged.astype(BF16)
        for n in range(3):
            dg_ref[0, :, n * D_MODEL:(n + 1) * D_MODEL] = dgs[n].astype(BF16)
            dp_ref[0, :, n * D_MODEL:(n + 1) * D_MODEL] = dps[n].astype(BF16)

    tile = lambda n: pl.BlockSpec((1, TB, n), lambda b, i: (b, i, 0))
    widths = (CONV_W, D_MODEL, SG_W, 3 * D_MODEL, D_MODEL, 3 * D_MODEL)
    dts = (F32, F32, F32, BF16, BF16, BF16)
    return _pcall(
        body, name=name, grid=(bsz, t // TB),
        in_specs=[tile(D_MODEL)] + _m_specs() + _w_specs(wa.shape, wb.shape, wc.shape, wo.shape),
        out_specs=[tile(n) for n in widths], out_shape=[_sds((bsz, t, n), d) for n, d in zip(widths, dts)],
        compiler_params=_cparams(("parallel", "parallel")),
    )(dout, ya, yb, yc, proj, proj, proj, wa, wb, wc, wo)


def _m_wgrad_call(ya, yb, yc, merged, dps, dout, name):
    bsz, t, _ = dout.shape

    def body(ya_ref, yb_ref, yc_ref, mg_ref, dp_ref, do_ref, dwa_ref, dwb_ref, dwc_ref, dwo_ref):
        @pl.when((pl.program_id(0) == 0) & (pl.program_id(1) == 0))
        def _():
            for r in (dwa_ref, dwb_ref, dwc_ref, dwo_ref):
                r[...] = jnp.zeros_like(r)

        dp = dp_ref[0]
        dwa_ref[...] += _tn(ya_ref[0], dp[:, :D_MODEL])
        dwb_ref[...] += _tn(yb_ref[0], dp[:, D_MODEL:2 * D_MODEL])
        dwc_ref[...] += _tn(yc_ref[0], dp[:, 2 * D_MODEL:])
        dwo_ref[...] += _tn(mg_ref[0], do_ref[0])

    tile = lambda n: pl.BlockSpec((1, TB, n), lambda b, i: (b, i, 0))
    shapes = ((CONV_W, D_MODEL), (D_MODEL, D_MODEL), (SG_W, D_MODEL), (D_MODEL, D_MODEL))
    return _pcall(
        body, name=name, grid=(bsz, t // TB),
        in_specs=[tile(CONV_W), tile(D_MODEL), tile(SG_W), tile(D_MODEL), tile(3 * D_MODEL), tile(D_MODEL)],
        out_specs=_w_specs(*shapes), out_shape=[_sds(s, F32) for s in shapes],
        compiler_params=_cparams(("arbitrary", "arbitrary")),
    )(ya, yb, yc, merged, dps, dout)


def _dn_specs(nc, rev):
    ti = (lambda i: nc - 1 - i) if rev else (lambda i: i)
    return [pl.BlockSpec((1, CHUNK, B_W), lambda b, i: (b, ti(i), 0)),
            pl.BlockSpec((1, CHUNK, DK), lambda b, i: (b, ti(i), BA_OFF // DK)),
            pl.BlockSpec((SHORT_K, 3 * D_MODEL), lambda b, i: (0, 0))] + [pl.BlockSpec((1, DK), lambda b, i: (0, 0))] * 3


def _dn_saved_specs(nc, rev):
    ti = (lambda i: nc - 1 - i) if rev else (lambda i: i)
    return [pl.BlockSpec((1, CHUNK, D_MODEL), lambda b, i: (b, ti(i), 0)),
            pl.BlockSpec((1, CHUNK, 3 * D_MODEL), lambda b, i: (b, ti(i), 0)),
            pl.BlockSpec((1, HEADS, 1, DK, DK), lambda b, i: (b, 0, ti(i), 0, 0)),
            pl.BlockSpec((1, 1, CHUNK, HEADS * CHUNK), lambda b, i: (b, ti(i), 0, 0)),
            pl.BlockSpec((1, CHUNK, 2 * D_MODEL), lambda b, i: (b, ti(i), 0)),
            pl.BlockSpec((1, CHUNK, D_MODEL), lambda b, i: (b, ti(i), 0))]


def _gather_fused(x_refs, o_refs, send_sems, recv_sems, finish):
    x, y, c = _axes()
    chip, sib, peers = 2 * x + y, (x, y, 1 - c), _chip_peers(x, y)
    pchip = [2 * px + py for px, py in peers]
    n = len(x_refs)
    halves = [r.shape[0] // 2 for r in x_refs]

    def half(a, cc):
        return pl.ds(cc * halves[a], halves[a])

    first = [_remote(x_refs[a].at[half(a, c)], o_refs[a].at[chip, half(a, c)], send_sems, recv_sems, 6 * a + k,
                     (*peers[k], c)) for k in range(3) for a in range(n)]
    if not finish:
        for cp in first:
            cp.start()
        return
    passed = []
    for k in range(3):
        for a in range(n):
            land = o_refs[a].at[pchip[k], half(a, c)]
            _remote(land, land, send_sems, recv_sems, 6 * a + k, (*peers[k], c)).wait_recv()
            passed.append(_remote(land, land, send_sems, recv_sems, 6 * a + 3 + k, sib))
            passed[-1].start()
    for k in range(3):
        for a in range(n):
            land = o_refs[a].at[pchip[k], half(a, 1 - c)]
            _remote(land, land, send_sems, recv_sems, 6 * a + 3 + k, sib).wait_recv()
    for cp in first + passed:
        cp.wait_send()


def _dn_fwd_call(proj, wconv, alog, dtb, og, name, gather=()):
    bsz, t, _ = proj.shape
    nc = t // CHUNK
    ng = len(gather)

    def body(*refs):
        (x_ref, ba_ref, w_ref, al_ref, dt_ref, og_ref, halo_ref), refs = refs[:7], refs[7:]
        g_refs, (y_ref, o_ref, pre_ref, st_ref, t_ref, uw_ref, vn_ref), refs = refs[:ng], refs[ng:ng + 7], refs[ng + 7:]
        go_refs, s_scr, sems = refs[:ng], refs[ng], refs[ng + 1:]
        i = pl.program_id(1)

        if ng:
            @pl.when((pl.program_id(0) == 0) & (i == 0))
            def _():
                _gather_fused(g_refs, go_refs, *sems, finish=False)

        @pl.when(i == 0)
        def _():
            s_scr[...] = jnp.zeros_like(s_scr)

        keep = jnp.where(i > 0, 1.0, 0.0)
        bsig, gfull, _, _ = _dn_gates(ba_ref[0], al_ref[...], dt_ref[...])
        items = []
        for h in range(HEADS):
            pres = []
            for n in range(3):
                cs = slice(n * D_MODEL + h * DK, n * D_MODEL + (h + 1) * DK)
                pres.append(_short_conv(x_ref[0, :, cs], halo_ref[0, :, cs] * keep, w_ref[:, cs]))
                pre_ref[0, :, cs] = pres[-1]
            items.append(_dn_item(*pres, bsig[:, h:h + 1], gfull[:, HEADS + h:HEADS + h + 1]))
        ss = [s_scr[h] for h in range(HEADS)]
        for h in range(HEADS):
            st_ref[0, h, 0] = ss[h]
        outs, ss, vnew = _dn_fwd_chunk(items, ss)
        og_v = og_ref[...]
        for h, d in enumerate(items):
            s_scr[h] = ss[h]
            hs = slice(h * DK, (h + 1) * DK)
            z = x_ref[0, :, 3 * D_MODEL + h * DK:3 * D_MODEL + (h + 1) * DK]
            o_ref[0, :, hs] = outs[h]
            y_ref[0, :, hs] = _dn_out(outs[h], z, _sig(z), og_v)[0].astype(BF16)
            t_ref[0, 0, :, h * CHUNK:(h + 1) * CHUNK] = d["t"].astype(BF16)
            uw_ref[0, :, 2 * h * DK:(2 * h + 1) * DK] = d["u"].astype(BF16)
            uw_ref[0, :, (2 * h + 1) * DK:2 * (h + 1) * DK] = d["w"].astype(BF16)
            vn_ref[0, :, hs] = vnew[h].astype(BF16)

        if ng:
            @pl.when((pl.program_id(0) == bsz - 1) & (i == nc - 1))
            def _():
                _gather_fused(g_refs, go_refs, *sems, finish=True)

    halo = pl.BlockSpec((1, HALO_B, 3 * D_MODEL), lambda b, i: (b, jnp.maximum(i * (CHUNK // HALO_B) - 1, 0), 0))
    outs = _pcall(
        body, name=name, grid=(bsz, nc), in_specs=_dn_specs(nc, False) + [halo] + [_ANY] * ng,
        out_specs=[pl.BlockSpec((1, CHUNK, D_MODEL), lambda b, i: (b, i, 0))] + _dn_saved_specs(nc, False) + [_ANY] * ng,
        out_shape=[_sds((bsz, t, D_MODEL), BF16), _sds((bsz, t, D_MODEL), F32), _sds((bsz, t, 3 * D_MODEL), F32),
                   _sds((bsz, HEADS, nc, DK, DK), F32), _sds((bsz, nc, CHUNK, HEADS * CHUNK), BF16),
                   _sds((bsz, t, 2 * D_MODEL), BF16), _sds((bsz, t, D_MODEL), BF16)]
        + [_sds((4,) + s.shape, s.dtype) for s in gather],
        scratch_shapes=[pltpu.VMEM((HEADS, DK, DK), F32)] + [pltpu.SemaphoreType.DMA((6 * ng,))] * (2 if ng else 0),
        compiler_params=_cparams(("arbitrary", "arbitrary")),
    )(proj, proj, wconv, alog, dtb, og, proj, *gather)
    if not ng:
        return outs
    chip = 2 * lax.axis_index("x") + lax.axis_index("y")
    return list(outs[:7]) + [lax.dynamic_update_slice_in_dim(o, s[None], chip, axis=0)
                             for o, s in zip(outs[7:], gather)]


def _dn_bwd_call(dy, saved, proj, wconv, alog, dtb, og, name):
    bsz, t, _ = proj.shape
    nc = t // CHUNK

    def body(dy_ref, o_ref, pre_ref, st_ref, t_ref, uw_ref, vn_ref, x_ref, ba_ref, w_ref, al_ref, dt_ref, og_ref,
             dx_ref, dba_ref, dwc_ref, dsm_ref, ds_scr, dpre_scr):
        ip = pl.program_id(1)

        @pl.when((pl.program_id(0) == 0) & (ip == 0))
        def _():
            dwc_ref[...] = jnp.zeros_like(dwc_ref)
            dsm_ref[...] = jnp.zeros_like(dsm_ref)

        @pl.when(ip == 0)
        def _():
            ds_scr[...] = jnp.zeros_like(ds_scr)
            dpre_scr[...] = jnp.zeros_like(dpre_scr)

        bsig, gfull, xg, ea8 = _dn_gates(ba_ref[0], al_ref[...], dt_ref[...])
        og_v = og_ref[...]
        items, dog = [], jnp.zeros((1, DK), F32)
        for h in range(HEADS):
            hs = slice(h * DK, (h + 1) * DK)
            zs = slice(3 * D_MODEL + h * DK, 3 * D_MODEL + (h + 1) * DK)
            d = _dn_item(*(pre_ref[0, :, n * D_MODEL + h * DK:n * D_MODEL + (h + 1) * DK] for n in range(3)),
                         bsig[:, h:h + 1], gfull[:, HEADS + h:HEADS + h + 1])
            z, dyv = x_ref[0, :, zs], dy_ref[0, :, hs]
            sz = _sig(z)
            _, xh, r = _dn_out(o_ref[0, :, hs], z, sz, og_v)
            dx_ref[0, :, zs] = (dyv * xh * og_v * (sz * (1.0 + z * (1.0 - sz)))).astype(BF16)
            don = dyv * (z * sz)
            dog = dog + _sum0(don * xh)
            d.update(do=_rms_bwd(don * og_v, xh, r), t=t_ref[0, 0, :, h * CHUNK:(h + 1) * CHUNK],
                     u=uw_ref[0, :, 2 * h * DK:(2 * h + 1) * DK], w=uw_ref[0, :, (2 * h + 1) * DK:2 * (h + 1) * DK],
                     vnew=vn_ref[0, :, hs])
            items.append(d)
        grads, dss = _dn_bwd_chunk(items, [st_ref[0, h, 0] for h in range(HEADS)], [ds_scr[h] for h in range(HEADS)])
        lanes = lax.broadcasted_iota(jnp.int32, (CHUNK, DK), 1)
        lane8 = lax.broadcasted_iota(jnp.int32, (8, DK), 1)
        row8 = lax.broadcasted_iota(jnp.int32, (8, DK), 0)
        dgdx = -ea8 * _sig(xg)
        dba = jnp.zeros((CHUNK, DK), F32)
        dsm = jnp.where(row8 == 2, dog, 0.0)
        for h, (d, (dq, dk, dv, dbeta, dg)) in enumerate(zip(items, grads)):
            ds_scr[h] = dss[h]
            qc, kc, nq, nk = d["qc"], d["kc"], d["nq"], d["nk"]
            dacts = ((DK ** -0.5) * nq * (dq - qc * (nq * nq * _sum1(dq * qc))),
                     nk * (dk - kc * (nk * nk * _sum1(dk * kc))), dv)
            dal = dg * dgdx[:, HEADS + h:HEADS + h + 1]
            dba = dba + jnp.where(lanes == h, dbeta * d["b"] * (1.0 - d["b"]), 0.0) + jnp.where(lanes == HEADS + h, dal, 0.0)
            dsm = dsm + (jnp.where((row8 == 0) & (lane8 == h), _sum0(dg * d["g"]), 0.0)
                         + jnp.where((row8 == 1) & (lane8 == h), _sum0(dal), 0.0))
            for n in range(3):
                cs = slice(n * D_MODEL + h * DK, n * D_MODEL + (h + 1) * DK)
                pre, raw, w = pre_ref[0, :, cs], x_ref[0, :, cs], w_ref[:, cs]
                sp = _sig(pre)
                dpre = dacts[n] * (sp * (1.0 + pre * (1.0 - sp)))
                dext = jnp.concatenate([dpre, dpre_scr[:, cs]], axis=0)
                draw = dext[3:3 + CHUNK] * w[0:1]
                rows = [_sum0(dext[3:3 + CHUNK] * raw)]
                for j in range(1, SHORT_K):
                    shifted = dext[3 - j:3 - j + CHUNK]
                    draw = draw + shifted * w[j:j + 1]
                    rows.append(_sum0(shifted * raw))
                dpre_scr[:, cs] = dpre[:HALO_B]
                dx_ref[0, :, cs] = draw.astype(BF16)
                dwc_ref[:, cs] += _rows8(*rows)
        dba_ref[0] = dba.astype(BF16)
        dsm_ref[...] += dsm

    rrow = lambda n: pl.BlockSpec((1, CHUNK, n), lambda b, i: (b, nc - 1 - i, 0))
    return _pcall(
        body, name=name, grid=(bsz, nc), in_specs=[rrow(D_MODEL)] + _dn_saved_specs(nc, True) + _dn_specs(nc, True),
        out_specs=[rrow(B_W), rrow(DK), pl.BlockSpec((8, 3 * D_MODEL), lambda b, i: (0, 0)),
                   pl.BlockSpec((8, DK), lambda b, i: (0, 0))],
        out_shape=[_sds((bsz, t, B_W), BF16), _sds((bsz, t, DK), BF16), _sds((8, 3 * D_MODEL), F32), _sds((8, DK), F32)],
        scratch_shapes=[pltpu.VMEM((HEADS, DK, DK), F32), pltpu.VMEM((HALO_B, 3 * D_MODEL), F32)],
        compiler_params=_cparams(("arbitrary", "arbitrary")),
    )(dy, *saved, proj, proj, wconv, alog, dtb, og)


def _adamw_call(w, g, m, v, name):
    shape = w.shape
    view = (math.prod(shape[:-2]),) + shape[-2:] if len(shape) >= 2 else (1, 1) + shape
    lead, rows, cols = view
    tr, tl = (128, 1) if rows % 128 == 0 else (rows, min(lead, 64))
    c1, c2 = 1.0 - ADAM_B1 ** ADAM_STEP, 1.0 - ADAM_B2 ** ADAM_STEP

    def body(w_ref, g_ref, m_ref, v_ref, d_ref, nm_ref, nv_ref):
        gv = g_ref[...]
        nm = ADAM_B1 * m_ref[...] + (1.0 - ADAM_B1) * gv
        nv = ADAM_B2 * v_ref[...] + (1.0 - ADAM_B2) * (gv * gv)
        d_ref[...] = -ADAM_LR * ((nm / c1) / (jnp.sqrt(nv / c2) + ADAM_EPS) + ADAM_WD * w_ref[...])
        nm_ref[...] = nm
        nv_ref[...] = nv

    blk = pl.BlockSpec((tl, tr, cols), lambda l, i: (l, i, 0))
    outs = _pcall(
        body, name=name, grid=(pl.cdiv(lead, tl), rows // tr), in_specs=[blk] * 4, out_specs=[blk] * 3,
        out_shape=[_sds(view, F32)] * 3, compiler_params=_cparams(("parallel", "parallel")),
    )(*(a.reshape(view) for a in (w, g, m, v)))
    return tuple(o.reshape(shape) for o in outs)


def _row_tile(rows, cap=512):
    for tr in range(cap, 15, -16):
        if rows % tr == 0:
            return tr
    return rows


def _add_pair_call(by_chip, recv, where, name):
    _, n, rows, cols = by_chip.shape
    tr = _row_tile(rows, 256)

    def body(where_ref, a_ref, b_ref, sb_ref, own_ref):
        s = a_ref[0, 0] + b_ref[0]
        sb_ref[0] = s.astype(BF16)

        @pl.when(pl.program_id(1) == where_ref[1])
        def _():
            own_ref[...] = s

    grid_spec = pltpu.PrefetchScalarGridSpec(
        num_scalar_prefetch=1, grid=(rows // tr, n),
        in_specs=[pl.BlockSpec((1, 1, tr, cols), lambda i, j, wr: (wr[0], j, i, 0)),
                  pl.BlockSpec((1, tr, cols), lambda i, j, wr: (j, i, 0))],
        out_specs=[pl.BlockSpec((1, tr, cols), lambda i, j, wr: (j, i, 0)),
                   pl.BlockSpec((tr, cols), lambda i, j, wr: (i, 0))])
    return _pcall(
        body, name=name, grid_spec=grid_spec, out_shape=[_sds((n, rows, cols), BF16), _sds((rows, cols), F32)],
        compiler_params=_cparams(("parallel", "arbitrary")),
    )(where, by_chip, recv)


def _add_recv_call(own, recv, where, name):
    rows, cols = own.shape
    tr = _row_tile(rows, 256)

    def body(where_ref, o_ref, r_ref, s_ref):
        s_ref[0] = ((o_ref[...] + r_ref[0].astype(F32)) + r_ref[1].astype(F32)) + r_ref[2].astype(F32)

    grid_spec = pltpu.PrefetchScalarGridSpec(
        num_scalar_prefetch=1, grid=(rows // tr,),
        in_specs=[pl.BlockSpec((tr, cols), lambda i, wr: (i, 0)), pl.BlockSpec((3, tr, cols), lambda i, wr: (0, i, 0))],
        out_specs=pl.BlockSpec((1, tr, cols), lambda i, wr: (wr[0], i, 0)))
    return _pcall(
        body, name=name, grid_spec=grid_spec, out_shape=_sds((2, rows, cols), F32),
        compiler_params=_cparams(("parallel",)),
    )(where, own, recv)


def _axes():
    return lax.axis_index("x"), lax.axis_index("y"), lax.axis_index("c")


def _chip_peers(x, y):
    return [(x, 1 - y), (1 - x, y), (1 - x, 1 - y)]


_ANY = pl.BlockSpec(memory_space=pl.ANY)
_VMEM = pl.BlockSpec(memory_space=pltpu.VMEM)


def _remote(src, dst, send_sems, recv_sems, k, dev):
    return pltpu.make_async_remote_copy(src_ref=src, dst_ref=dst, send_sem=send_sems.at[k], recv_sem=recv_sems.at[k],
                                        device_id=dev, device_id_type=MESH)


def _gather_weights_call(shards):
    n = len(shards)
    halves = [s.shape[0] // 2 for s in shards]
    quarter = halves[0] // 2
    base = [0] + [8 + 6 * (a - 1) for a in range(1, n)]

    def body(*refs):
        x_refs, o_refs, (send_sems, recv_sems) = refs[:n], refs[n:2 * n], refs[2 * n:]
        x, y, c = _axes()
        chip, sib, peers = 2 * x + y, (x, y, 1 - c), _chip_peers(x, y)
        pchip = [2 * px + py for px, py in peers]

        def half(a, cc):
            return pl.ds(cc * halves[a], halves[a])

        def part(cc, q):
            return pl.ds(cc * halves[0] + q * quarter, quarter)

        def copy(ref, k, dev):
            return _remote(ref, ref, send_sems, recv_sems, k, dev)

        sends = [_remote(x_refs[a].at[half(a, c)], o_refs[a].at[chip, half(a, c)], send_sems, recv_sems, base[a] + k,
                         (*peers[k], c)) for k in range(3) for a in range(n) if a > 0 or k < 2]
        for cp in sends:
            cp.start()
        for k in range(2):
            copy(o_refs[0].at[pchip[k], half(0, c)], k, (*peers[k], c)).wait_recv()
            sends.append(copy(o_refs[0].at[pchip[k], part(c, k)], 2 + k, (*peers[1 - k], c)))
            sends.append(copy(o_refs[0].at[pchip[k], half(0, c)], 4 + k, sib))
            sends[-2].start()
            sends[-1].start()
        for k in range(3):
            for a in range(1, n):
                land = o_refs[a].at[pchip[k], half(a, c)]
                copy(land, base[a] + k, (*peers[k], c)).wait_recv()
                sends.append(copy(land, base[a] + 3 + k, sib))
                sends[-1].start()
        for q in range(2):
            land = o_refs[0].at[pchip[2], part(c, q)]
            copy(land, 2 + q, (*peers[1 - q], c)).wait_recv()
            sends.append(copy(land, 6 + q, sib))
            sends[-1].start()
        for k in range(2):
            copy(o_refs[0].at[pchip[k], half(0, 1 - c)], 4 + k, sib).wait_recv()
            copy(o_refs[0].at[pchip[2], part(1 - c, k)], 6 + k, sib).wait_recv()
        for k in range(3):
            for a in range(1, n):
                copy(o_refs[a].at[pchip[k], half(a, 1 - c)], base[a] + 3 + k, sib).wait_recv()
        for cp in sends:
            cp.wait_send()

    n_sems = 8 + 6 * (n - 1)
    outs = _pcall(
        body, name="gather_weights", in_specs=[_ANY] * n, out_specs=[_ANY] * n,
        out_shape=[_sds((4,) + s.shape, s.dtype) for s in shards],
        scratch_shapes=[pltpu.SemaphoreType.DMA((n_sems,)), pltpu.SemaphoreType.DMA((n_sems,))],
    )(*shards)
    chip = 2 * lax.axis_index("x") + lax.axis_index("y")
    return [lax.dynamic_update_slice_in_dim(o, s[None], chip, axis=0) for o, s in zip(outs, shards)]


def _pair_partials_call(by_chip, name):
    n = len(by_chip)

    def body(*refs):
        v_refs, o_refs, (send_sems, recv_sems) = refs[:n], refs[n:2 * n], refs[2 * n:]
        x, y, c = _axes()
        cps = [_remote(v_refs[a].at[1 - c], o_refs[a], send_sems, recv_sems, a, (x, y, 1 - c)) for a in range(n)]
        for cp in cps:
            cp.start()
        for cp in cps:
            cp.wait()

    return _pcall(
        body, name=name, in_specs=[_ANY] * n, out_specs=[_ANY] * n,
        out_shape=[_sds(v.shape[1:], v.dtype) for v in by_chip],
        scratch_shapes=[pltpu.SemaphoreType.DMA((n,)), pltpu.SemaphoreType.DMA((n,))],
    )(*by_chip)


def _pair_result_call(fins, name):
    n = len(fins)

    def body(*refs):
        v_refs, o_refs, (send_sems, recv_sems) = refs[:n], refs[n:2 * n], refs[2 * n:]
        x, y, c = _axes()
        cps = [_remote(v_refs[a].at[c], o_refs[a].at[c], send_sems, recv_sems, a, (x, y, 1 - c)) for a in range(n)]
        for cp in cps:
            cp.start()
        for a in range(n):
            cps[a].wait_send()
            _remote(v_refs[a].at[c], o_refs[a].at[1 - c], send_sems, recv_sems, a, (x, y, 1 - c)).wait_recv()

    return _pcall(
        body, name=name, in_specs=[_ANY] * n, out_specs=[_ANY] * n,
        out_shape=[_sds(v.shape, v.dtype) for v in fins], input_output_aliases={a: a for a in range(n)},
        scratch_shapes=[pltpu.SemaphoreType.DMA((n,)), pltpu.SemaphoreType.DMA((n,))],
    )(*fins)


def _allreduce_small_call(v):
    rows, cols = v.shape

    def body(v_ref, o_ref, buf, send_sems, recv_sems):
        x, y, c = _axes()
        chip, peers = 2 * x + y, _chip_peers(x, y)
        pair = _remote(v_ref, buf.at[0], send_sems, recv_sems, 0, (x, y, 1 - c))
        pair.start()
        pair.wait()
        buf[0] = v_ref[...] + buf[0]
        cps = [_remote(buf.at[0], buf.at[1 + k], send_sems, recv_sems, 1 + k, (*peers[k], c)) for k in range(3)]
        for cp in cps:
            cp.start()
        for cp in cps:
            cp.wait()
        acc = buf[chip]
        for d in range(1, 4):
            acc = acc + buf[lax.bitwise_xor(chip, d)]
        o_ref[...] = acc

    return _pcall(
        body, name="allreduce_small", in_specs=[_VMEM], out_specs=_VMEM, out_shape=_sds(v.shape, F32),
        scratch_shapes=[pltpu.VMEM((4, rows, cols), F32), pltpu.SemaphoreType.DMA((4,)), pltpu.SemaphoreType.DMA((4,))],
        compiler_params=_cparams(),
    )(v)


def _pack_rows(arrays, dtype, total_rows=None):
    parts = []
    for a in arrays:
        flat = a.astype(dtype).reshape(-1)
        parts.append(jnp.pad(flat, (0, -flat.shape[0] % D_MODEL)).reshape(-1, D_MODEL))
    out = jnp.concatenate(parts, axis=0)
    total_rows = total_rows or out.shape[0] + (-out.shape[0] % 8)
    return jnp.pad(out, ((0, total_rows - out.shape[0]), (0, 0)))


def _unpack_rows(packed, shapes):
    out, r = [], 0
    for s in shapes:
        n = math.prod(s)
        nr = -(-n // D_MODEL)
        out.append(packed[r:r + nr].reshape(-1)[:n].reshape(s))
        r += nr
    return out


def _to_r(by_chip, n):
    def cols(lo, hi):
        return [by_chip[j, max(lo, j * n) - j * n:min(hi, (j + 1) * n) - j * n] for j in range(4)
                if max(lo, j * n) < min(hi, (j + 1) * n)]

    pad = jnp.zeros((N_R - N_ORIG, by_chip.shape[2]), by_chip.dtype)
    return jnp.concatenate(cols(A_ORIG_W, BA_ORIG) + cols(0, A_ORIG_W) + cols(BA_ORIG + 2 * HEADS, N_ORIG)
                           + cols(BA_ORIG, BA_ORIG + 2 * HEADS) + [pad], axis=0)


_BIG = ("w_in", "a_proj", "b_proj", "c_proj", "w_out")
_SHARD_AXIS = {"w_in": 2, "a_proj": 2, "b_proj": 1, "c_proj": 2, "w_out": 1, "a_dw": 2, "b_conv": 2}
_BIG_AXIS = _SHARD_AXIS


def _join_chips(g, axis):
    g = jnp.moveaxis(g, 0, axis)
    return g.reshape(g.shape[:axis] + (4 * g.shape[axis + 1],) + g.shape[axis + 2:])


def _split_chips(a, axis):
    n = a.shape[axis] // 4
    by_chip = jnp.moveaxis(a.reshape(a.shape[:axis] + (4, n) + a.shape[axis + 1:]), axis, 0)
    return jnp.moveaxis(by_chip.reshape(4, 2, by_chip.shape[1] // 2, by_chip.shape[2]), 1, 0)
_ORDER = ("norm_g", "w_in", "a_dw", "a_dw_b", "a_ln_g", "a_ln_b", "a_proj", "b_conv", "b_a_log", "b_dt_bias",
          "b_onorm_g", "b_proj", "c_ln_g", "c_ln_b", "c_ws", "c_bs", "c_proj", "w_out", "final_g")


def _reduce_begin(layer_grads, where, name):
    by_chip = [_split_chips(layer_grads[k], _SHARD_AXIS[k] - 1) for k in _BIG]
    theirs = _pair_partials_call(by_chip, name + "pair_partials")
    return [_add_pair_call(b, r, where, name + "pair_sum_" + k) for k, b, r in zip(_BIG, by_chip, theirs)]


def _reduce_end(sums, got, where, name):
    fins = [_add_recv_call(s[1], r, where, name + "chip_sum_" + k) for k, s, r in zip(_BIG, sums, got)]
    return [o.reshape((-1,) + o.shape[2:]) for o in _pair_result_call(fins, name + "pair_result")]


def _local_step(x, tgt, first, rest, final_g, where=None):
    bsz, t, _ = x.shape
    m = bsz * t
    depth = 1 + len(rest)
    row = lambda v: v.reshape(1, -1)
    saved, xl, w = [], x, first
    for l in range(depth):
        n = f"l{l}_"
        w_rt = w["w_in_rt"]
        par = dict(
            adw=w["a_dw"], adb=row(w["a_dw_b"]), alg=row(w["a_ln_g"]), alb=row(w["a_ln_b"]), bconv=w["b_conv"],
            alog=jnp.pad(row(w["b_a_log"]), ((0, 0), (HEADS, DK - 2 * HEADS))),
            dtb=jnp.pad(row(w["b_dt_bias"]), ((0, 0), (HEADS, DK - 2 * HEADS))), og=row(w["b_onorm_g"]),
            clg=row(w["c_ln_g"]), clb=row(w["c_ln_b"]), cws=w["c_ws"], cbias=jnp.repeat(w["c_bs"].T, SG_C, axis=1),
            ng=row(w["norm_g"]), wa=w["a_proj"], wb=w["b_proj"], wc=w["c_proj"], wo=w["w_out"], wrt=w_rt, wr=w_rt.T)
        h = _rms_fwd_call(xl.reshape(m, D_MODEL), par["ng"], n + "norm")
        proj = _matmul(h, par["wr"], F32, 2048, 1152, 1024, n + "in_proj").reshape(bsz, t, N_R)
        ya, conv = _a_fwd_call(proj, par["adw"], par["adb"], par["alg"], par["alb"], n + "conv_fwd")
        shards, assemble = rest[l] if l < len(rest) else ((), None)
        yb, *dn_saved = _dn_fwd_call(proj, par["bconv"], par["alog"], par["dtb"], par["og"], n + "delta_fwd", shards)
        dn_saved, gathered = dn_saved[:6], dn_saved[6:]
        if assemble is not None:
            w = assemble(gathered)
        yc = _c_fwd_call(proj, par["clg"], par["clb"], par["cws"], par["cbias"], n + "gmlp_fwd")
        x_next = _m_fwd_call(ya, yb, yc, proj, xl, par["wa"], par["wb"], par["wc"], par["wo"], n + "merge_fwd")
        saved.append((par, xl, h, proj, ya, yb, yc, conv, dn_saved))
        xl = x_next
    dout, dfg, loss = _loss_call(xl.reshape(m, D_MODEL), tgt.reshape(m, D_MODEL), row(final_g))
    dout = dout.reshape(bsz, t, D_MODEL)
    g = {k: [None] * depth for k in _ORDER if k != "final_g"}
    pending = None
    for l in reversed(range(depth)):
        n = f"l{l}_"
        par, xl, h, proj, ya, yb, yc, conv, dn_saved = saved[l]
        dya, dyb, dyc, dgate, merged, dps = _m_bwd_call(dout, ya, yb, yc, proj, par["wa"], par["wb"], par["wc"],
                                                        par["wo"], n + "merge_bwd")
        g["a_proj"][l], g["b_proj"][l], g["c_proj"][l], g["w_out"][l] = _m_wgrad_call(ya, yb, yc, merged, dps, dout,
                                                                                      n + "merge_wgrad")
        travel = [s[0] for s in pending[1]] if pending else []
        da, dadw, dasm, *got = _a_bwd_call(dya, conv, proj, par["adw"], par["alg"], par["alb"], n + "conv_bwd", travel)
        if pending:
            for k, r in zip(_BIG, _reduce_end(pending[1], got, where, f"l{pending[0]}_")):
                g[k][pending[0]] = r
            pending = None
        db, dba, dbconv, dbsm = _dn_bwd_call(dyb, dn_saved, proj, par["bconv"], par["alog"], par["dtb"], par["og"],
                                             n + "delta_bwd")
        dc, dcws, dcbs, dcsm = _c_bwd_call(dyc, proj, par["clg"], par["clb"], par["cws"], par["cbias"], n + "gmlp_bwd")
        segs = [s.reshape(m, s.shape[-1]) for s in (db, da, dc, dgate, dba)]
        ht = h.T
        dwb, dwa, dwc, dwg, dwba = [_matmul(ht, s, F32, 1024, 1024 if s.shape[1] % 1024 == 0 else 768, 2048,
                                            n + "in_proj_dw_" + tag) for s, tag in zip(segs, "bacgs")]
        g["w_in"][l] = jnp.concatenate([dwa, dwb, dwba[:, :2 * HEADS], dwc, dwg], axis=1)
        if where is None:
            dh = _matmul_ksegs(segs, par["wrt"], 2048, 1024, 512, n + "in_proj_dx")
        else:
            sums = _reduce_begin({k: g[k][l] for k in _BIG}, where, n)
            if l > 0:
                pending = (l, sums)
                dh = _matmul_ksegs(segs, par["wrt"], 2048, 1024, 512, n + "in_proj_dx")
            else:
                dh, got = _matmul_ksegs(segs, par["wrt"], 2048, 1024, 512, n + "in_proj_dx", [s[0] for s in sums])
                for k, r in zip(_BIG, _reduce_end(sums, got, where, n)):
                    g[k][l] = r
        dx, dng = _rms_bwd_call(xl.reshape(m, D_MODEL), dh, par["ng"], dout.reshape(m, D_MODEL), n + "norm_bwd")
        dout = dx.reshape(bsz, t, D_MODEL)
        g["norm_g"][l] = dng[0]
        g["a_dw"][l], g["a_dw_b"][l], g["a_ln_g"][l], g["a_ln_b"][l] = dadw[:CONV_K], dasm[0], dasm[1], dasm[2]
        g["b_conv"][l], g["b_a_log"][l], g["b_dt_bias"][l] = dbconv[:SHORT_K], dbsm[0, :HEADS], dbsm[1, :HEADS]
        g["b_onorm_g"][l] = dbsm[2]
        g["c_ln_g"][l], g["c_ln_b"][l], g["c_ws"][l] = dcsm[0], dcsm[1], dcws
        g["c_bs"][l] = dcbs.reshape(SG_C, SG_G, SG_C).sum(-1).T
    grads = {k: jnp.stack(v) for k, v in g.items()}
    grads["final_g"] = dfg[0]
    return loss[0, 0], dout, grads


def kernel(x, norm_g, w_in, a_dw, a_dw_b, a_ln_g, a_ln_b, a_proj, b_conv, b_a_log, b_dt_bias, b_onorm_g, b_proj, c_ln_g, c_ln_b, c_ws, c_bs, c_proj, w_out, final_g, loss_target, m_norm_g, m_w_in, m_a_dw, m_a_dw_b, m_a_ln_g, m_a_ln_b, m_a_proj, m_b_conv, m_b_a_log, m_b_dt_bias, m_b_onorm_g, m_b_proj, m_c_ln_g, m_c_ln_b, m_c_ws, m_c_bs, m_c_proj, m_w_out, m_final_g, v_norm_g, v_w_in, v_a_dw, v_a_dw_b, v_a_ln_g, v_a_ln_b, v_a_proj, v_b_conv, v_b_a_log, v_b_dt_bias, v_b_onorm_g, v_b_proj, v_c_ln_g, v_c_ln_b, v_c_ws, v_c_bs, v_c_proj, v_w_out, v_final_g):
    given = dict(locals())
    ws = {k: given[k] for k in _ORDER}
    xi, yi, ci = _axes()
    chip = 2 * xi + yi
    where = jnp.stack([ci, chip]).astype(jnp.int32)

    depth = norm_g.shape[0]
    w_t = jnp.transpose(w_in, (2, 0, 1)).astype(BF16)
    sharded = ("w_in_t",) + _BIG[1:] + ("a_dw", "b_conv")

    def rows_to(a, n):
        return jnp.pad(a, ((0, n - a.shape[0]), (0, 0)))

    def shards(l):
        return ([rows_to(w_t[:, l], GATHER_ROWS)] + [ws[k][l].astype(BF16) for k in _BIG[1:]]
                + [rows_to(a_dw[l], HALO_A), rows_to(b_conv[l], 16)])

    def assemble(l):
        def full_weights(gathered):
            lw = {k: ws[k][l] for k in _ORDER if k not in sharded and k not in ("w_in", "final_g")}
            lw["w_in_rt"] = _to_r(gathered[0], w_t.shape[0])
            for k, g in zip(sharded[1:], gathered[1:]):
                lw[k] = _join_chips(g, _SHARD_AXIS[k] - 1)
            lw["b_conv"] = lw["b_conv"][:SHORT_K]
            return lw
        return full_weights

    first = assemble(0)(_gather_weights_call(shards(0)))

    loss, grad_x, grads = _local_step(x, loss_target, first, [(shards(l), assemble(l)) for l in range(1, depth)],
                                      final_g, where)
    out_g = {k: grads[k] for k in _BIG}

    rest = [k for k in _ORDER if k not in _BIG]
    rest_shapes = [grads[k].shape for k in rest] + [(1,)]
    summed = _unpack_rows(_allreduce_small_call(_pack_rows([grads[k] for k in rest] + [loss.reshape(1)], F32)),
                          rest_shapes)
    out_g.update(zip(rest, summed[:-1]))
    out_g["a_dw"] = lax.dynamic_slice_in_dim(out_g["a_dw"], chip * a_dw.shape[2], a_dw.shape[2], axis=2)
    out_g["b_conv"] = lax.dynamic_slice_in_dim(out_g["b_conv"], chip * b_conv.shape[2], b_conv.shape[2], axis=2)

    def adam(k):
        operands = (ws[k], out_g[k], given["m_" + k], given["v_" + k])
        if k != "w_in":
            return _adamw_call(*operands, "adamw_" + k)
        w_t, g_t, m_t, v_t = (jnp.transpose(a, (2, 0, 1)) for a in operands)
        out_g[k] = jnp.transpose(g_t, (1, 2, 0))
        return tuple(jnp.transpose(o, (1, 2, 0)) for o in _adamw_call(w_t, g_t, m_t, v_t, "adamw_" + k))

    upd = {k: adam(k) for k in _ORDER}
    return (summed[-1][0], grad_x, *[out_g[k] for k in _ORDER], *[upd[k][0] for k in _ORDER],
            *[upd[k][1] for k in _ORDER], *[upd[k][2] for k in _ORDER])
```

```python
import functools
import math

import jax
import jax.numpy as jnp
from jax import lax
from jax.experimental import pallas as pl
from jax.experimental.pallas import tpu as pltpu

F32 = jnp.float32
BF16 = jnp.bfloat16
MESH = pl.DeviceIdType.MESH

D_MODEL = 1024
CONV_W = 512
CONV_K = 31
HEADS = 8
DK = 128
SHORT_K = 4
CHUNK = 64
SG_W = 512
SG_G = 4
SG_C = 128
EPS = 1e-6
N_ORIG = 10256
BA_ORIG = 5632
A_ORIG_W = 3 * CONV_W
B_W = 4 * D_MODEL
B_OFF, A_OFF, C_OFF, G_OFF, BA_OFF, N_R = 0, 4096, 5632, 7168, 10240, 10368
TB = 256
TB_L = 512
HALO_A = 32
HALO_B = 8
VMEM_LIMIT = 56 * 1024 * 1024
GATHER_ROWS = 2624
ADAM_LR, ADAM_B1, ADAM_B2, ADAM_EPS, ADAM_WD, ADAM_STEP = 0.001, 0.9, 0.999, 1e-08, 0.01, 10


def _pcall(body, **kw):
    return pl.pallas_call(body, **kw)


def _cparams(sem=None):
    kw = dict(vmem_limit_bytes=VMEM_LIMIT)
    if sem is not None:
        kw["dimension_semantics"] = sem
    return pltpu.CompilerParams(**kw)


def _sig(x):
    return jax.nn.sigmoid(x)


def _silu(x):
    return x * _sig(x)


def _dsilu(x):
    s = _sig(x)
    return s * (1.0 + x * (1.0 - s))


_GELU_C = math.sqrt(2.0 / math.pi)


def _gelu(x):
    return 0.5 * x * (1.0 + jnp.tanh(_GELU_C * (x + 0.044715 * x * x * x)))


def _dgelu(x):
    t = jnp.tanh(_GELU_C * (x + 0.044715 * x * x * x))
    return 0.5 * (1.0 + t) + 0.5 * x * (1.0 - t * t) * _GELU_C * (1.0 + 3 * 0.044715 * x * x)


def _softplus(x):
    return jnp.maximum(x, 0.0) + jnp.log1p(jnp.exp(-jnp.abs(x)))


def _dot(a, b, dims):
    return lax.dot_general(a.astype(BF16), b.astype(BF16), (dims, ((), ())), preferred_element_type=F32)


def _nn(a, b):
    return _dot(a, b, ((1,), (0,)))


def _nt(a, b):
    return _dot(a, b, ((1,), (1,)))


def _tn(a, b):
    return _dot(a, b, ((0,), (0,)))


def _tn_mxu(a, b):
    n = a.shape[1]
    eye = (lax.broadcasted_iota(jnp.int32, (n, n), 0) == lax.broadcasted_iota(jnp.int32, (n, n), 1)).astype(BF16)
    return _nn(_nt(eye, a), b)


def _mean(x):
    return jnp.mean(x, axis=-1, keepdims=True)


def _sum0(x):
    return jnp.sum(x, axis=0, keepdims=True)


def _sum1(x):
    return jnp.sum(x, axis=1, keepdims=True)


def _ln_fwd(x, g, b):
    xc = x - _mean(x)
    r = lax.rsqrt(_mean(xc * xc) + EPS)
    xh = xc * r
    return xh * g + b, xh, r


def _ln_bwd(dy, xh, r, g):
    dxh = dy * g
    return r * (dxh - _mean(dxh) - xh * _mean(dxh * xh)), _sum0(dy * xh), _sum0(dy)


def _rms_bwd(dxh, xh, r):
    return r * (dxh - xh * _mean(dxh * xh))


def _rows8(*rows):
    n = rows[0].shape[1]
    return jnp.concatenate(list(rows) + [jnp.zeros((8 - len(rows), n), F32)], axis=0)


def _windows(ext, tb):
    n = ext.shape[0] - 8
    shifted = {0: ext}

    def window(off):
        r = off % 8
        if r not in shifted:
            shifted[r] = ext[r:r + n]
        return shifted[r][off - r:off - r + tb]

    return window


def _a_fwd(val, glu, az, val_h, glu_h, w, bias, g, b):
    tb = val.shape[0]
    win = _windows(jnp.concatenate([val_h * _sig(glu_h), val * _sig(glu)], axis=0), tb)
    c = win(2) * w[0:1]
    for j in range(1, CONV_K):
        c = c + win(2 + j) * w[j:j + 1]
    c = c + bias
    ln, _, _ = _ln_fwd(c, g, b)
    return _silu(ln) * _silu(az), c


def _a_bwd(dy, val, glu, az, c, w, g, b, dc_next):
    tb = val.shape[0]
    ln, xh, r = _ln_fwd(c, g, b)
    sl, sz = _sig(ln), _sig(az)
    dln = dy * (az * sz) * (sl * (1.0 + ln * (1.0 - sl)))
    daz = dy * (ln * sl) * (sz * (1.0 + az * (1.0 - sz)))
    dc, dg, db = _ln_bwd(dln, xh, r, g)
    win = _windows(jnp.concatenate([dc, dc_next], axis=0), tb)
    sg = _sig(glu)
    a = val * sg
    da = win(30) * w[0:1]
    dw_rows = [_sum0(win(30) * a)]
    for j in range(1, CONV_K):
        shifted = win(30 - j)
        da = da + shifted * w[j:j + 1]
        dw_rows.append(_sum0(shifted * a))
    dw = jnp.concatenate(dw_rows + [jnp.zeros((1, CONV_W), F32)], axis=0)
    return da * sg, da * val * sg * (1.0 - sg), daz, dw, _sum0(dc), dg, db, dc[:HALO_A]


def _tril(ws):
    ii = lax.broadcasted_iota(jnp.int32, (SG_C, SG_C), 0)
    jj = lax.broadcasted_iota(jnp.int32, (SG_C, SG_C), 1)
    return [jnp.where(jj <= ii, ws[gi], 0.0) for gi in range(SG_G)], jj <= ii


def _c_mix(wt, vs, bias_full):
    tb = vs.shape[0]
    rows = []
    for n in range(tb // SG_C):
        blks = [_nn(wt[gi], vs[n * SG_C:(n + 1) * SG_C, gi * SG_C:(gi + 1) * SG_C]) for gi in range(SG_G)]
        rows.append(jnp.concatenate(blks, axis=1) + bias_full)
    return jnp.concatenate(rows, axis=0)


def _c_fwd(cu, cv, cz, g, b, ws, bias_full):
    wt, _ = _tril(ws)
    vs, xh, r = _ln_fwd(_gelu(cv), g, b)
    mixed = _c_mix(wt, vs, bias_full)
    return _gelu(cu) * mixed * _silu(cz), (wt, vs, xh, r, mixed)


def _c_bwd(dy, cu, cv, cz, g, b, ws, bias_full):
    tb = cu.shape[0]
    _, (wt, vs, xh, r, mixed) = _c_fwd(cu, cv, cz, g, b, ws, bias_full)
    _, low = _tril(ws)
    u, sz = _gelu(cu), _silu(cz)
    dcu = dy * mixed * sz * _dgelu(cu)
    dcz = dy * u * mixed * _dsilu(cz)
    dmixed = dy * u * sz
    dbs = jnp.zeros((SG_C, SG_W), F32)
    dws = [jnp.zeros((SG_C, SG_C), F32) for _ in range(SG_G)]
    rows = []
    for n in range(tb // SG_C):
        dm_n = dmixed[n * SG_C:(n + 1) * SG_C]
        dbs = dbs + dm_n
        blks = []
        for gi in range(SG_G):
            dm = dm_n[:, gi * SG_C:(gi + 1) * SG_C]
            dws[gi] = dws[gi] + _nt(dm, vs[n * SG_C:(n + 1) * SG_C, gi * SG_C:(gi + 1) * SG_C])
            blks.append(_tn(wt[gi], dm))
        rows.append(jnp.concatenate(blks, axis=1))
    dvs = jnp.concatenate(rows, axis=0)
    dgv, dg, db = _ln_bwd(dvs, xh, r, g)
    dws = [jnp.where(low, d, 0.0) for d in dws]
    return dcu, dgv * _dgelu(cv), dcz, dws, dbs, dg, db


def _m_fwd(ya, yb, yc, g0, g1, g2, wa, wb, wc):
    pa, pb, pc = _nn(ya, wa), _nn(yb, wb), _nn(yc, wc)
    s0, s1, s2 = _sig(g0), _sig(g1), _sig(g2)
    return s0 * pa + s1 * pb + s2 * pc, (pa, pb, pc, s0, s1, s2)


def _m_bwd(dout, ya, yb, yc, g0, g1, g2, wa, wb, wc, wo):
    merged, (pa, pb, pc, s0, s1, s2) = _m_fwd(ya, yb, yc, g0, g1, g2, wa, wb, wc)
    dm = _nt(dout, wo)
    dpa, dpb, dpc = dm * s0, dm * s1, dm * s2
    dgs = (dm * pa * s0 * (1.0 - s0), dm * pb * s1 * (1.0 - s1), dm * pc * s2 * (1.0 - s2))
    return (_nt(dpa, wa), _nt(dpb, wb), _nt(dpc, wc)), dgs, merged, (dpa, dpb, dpc)


def _chunk_masks(c):
    ii = lax.broadcasted_iota(jnp.int32, (c, c), 0)
    jj = lax.broadcasted_iota(jnp.int32, (c, c), 1)
    return ii, jj


def _dn_decay(items):
    ii, jj = _chunk_masks(CHUNK)
    incl, strict, eye = jj <= ii, jj < ii, ii == jj
    for d in items:
        g = d["g"]
        grow = _sum0(jnp.where(eye, g, 0.0))
        gc_col = _sum1(jnp.where(incl, grow, 0.0))
        gc_row = _sum0(jnp.where(ii <= jj, g, 0.0))
        gam_i = jnp.where(incl, jnp.exp(jnp.where(incl, gc_col - gc_row, 0.0)), 0.0)
        gl = _sum0(g)
        egc = jnp.exp(gc_col)
        d.update(gam_i=gam_i, gam_s=jnp.where(strict, gam_i, 0.0), egc=egc, ekd=jnp.exp(gl - gc_col), dl=jnp.exp(gl),
                 gdiff=gc_col - gc_row, qd=d["q"] * egc, rhs_w=d["k"] * (d["b"] * egc))
        d["kd"] = d["k"] * d["ekd"]
    return ii, jj, strict, eye


def _dn_solve(items, ii, jj, eye):
    off = ((ii >> 1) == (jj >> 1)) & ((ii & 1) != 0) & ((jj & 1) == 0)
    for d in items:
        a1 = jnp.where(off, d["a"], 0.0)
        d["t"] = jnp.where(eye, 1.0, 0.0) - a1
        d["m"] = d["a"] - _nn(a1, d["a"])
    b, sh = 2, 2
    while b < CHUNK:
        off = ((ii >> sh) == (jj >> sh)) & ((ii & b) != 0) & ((jj & b) == 0)
        for d in items:
            mo = jnp.where(off, d["m"], 0.0)
            if 2 * b < CHUNK:
                d["m"], d["t"] = d["m"] - _nn(mo, d["m"]), d["t"] - _nn(mo, d["t"])
            else:
                d["t"] = d["t"] - _nn(mo, d["t"])
        b, sh = 2 * b, sh + 1


def _dn_fwd_chunk(items, ss):
    ii, jj, strict, eye = _dn_decay(items)
    for d in items:
        d["a"] = d["b"] * _nt(d["k"], d["k"]) * d["gam_s"]
        d["qk"] = _nt(d["q"], d["k"]) * d["gam_i"]
    _dn_solve(items, ii, jj, eye)
    for d in items:
        d["u"], d["w"] = _nn(d["t"], d["v"] * d["b"]), _nn(d["t"], d["rhs_w"])
    ws = [_nn(d["w"], s) for d, s in zip(items, ss)]
    qs = [_nn(d["qd"], s) for d, s in zip(items, ss)]
    vnew = [d["u"] - w for d, w in zip(items, ws)]
    outs = [q + _nn(d["qk"], vn) for d, q, vn in zip(items, qs, vnew)]
    ss = [d["dl"] * s + _tn(d["kd"], vn) for d, s, vn in zip(items, ss, vnew)]
    return outs, ss, vnew


def _dn_bwd_chunk(items, ss, dss):
    ii, jj, strict, eye = _dn_decay(items)
    eye_b = eye.astype(BF16)
    for d in items:
        k, q = d["k"], d["q"]
        d["kk"] = _nt(k, k)
        d["a"] = d["b"] * d["kk"] * d["gam_s"]
        d["qk"] = _nt(q, k) * d["gam_i"]
        d["qkt"] = _nt(k, q) * jnp.where(ii <= jj, jnp.exp(jnp.where(ii <= jj, -d["gdiff"], 0.0)), 0.0)
    for d, s, ds2 in zip(items, ss, dss):
        d["dvnew"] = _nn(d["qkt"], d["do"]) + _nn(d["kd"], ds2)
        d["dqk"] = _nt(d["do"], d["vnew"])
        d["dqd"] = _nt(d["do"], s)
        d["dkd"] = _nt(d["vnew"], ds2)
    new_dss = []
    for d, s, ds2 in zip(items, ss, dss):
        d["dw"] = -_nt(d["dvnew"], s)
        new_dss.append(_tn_mxu(d["qd"], d["do"]) - _tn_mxu(d["w"], d["dvnew"]) + d["dl"] * ds2)
        d["ddl"] = _sum0(_sum1(s * ds2))
    for d in items:
        tt = _nt(eye_b, d["t"])
        d["drhs_u"], d["drhs_w"] = _nn(tt, d["dvnew"]), _nn(tt, d["dw"])
    for d in items:
        d["da"] = jnp.where(strict, -(_nt(d["drhs_u"], d["u"]) + _nt(d["drhs_w"], d["w"])), 0.0)
    outs = []
    for d in items:
        q, k, v, b, egc = d["q"], d["k"], d["v"], d["b"], d["egc"]
        drhs_u, drhs_w, da = d["drhs_u"], d["drhs_w"], d["da"]
        dbeta = _sum1(da * d["kk"] * d["gam_s"]) + _sum1(drhs_u * v) + _sum1(drhs_w * k) * egc
        dkk = da * b * d["gam_s"]
        e = da * d["a"] + d["dqk"] * d["qk"]
        s_kd = _sum1(d["dkd"] * d["kd"])
        dgc_col = _sum1(e) + _sum1(drhs_w * d["rhs_w"]) + _sum1(d["dqd"] * d["qd"]) - s_kd
        dgc_row = _sum0(jnp.where(eye, dgc_col, 0.0)) - _sum0(e)
        dg = _sum1(jnp.where(jj >= ii, dgc_row, 0.0)) + (_sum0(s_kd) + d["ddl"] * d["dl"])
        dqkg = d["dqk"] * d["gam_i"]
        dq = _nn(dqkg, k) + d["dqd"] * egc
        dk = _tn_mxu(dqkg, q) + _nn(dkk, k) + _tn_mxu(dkk, k) + drhs_w * (b * egc) + d["dkd"] * d["ekd"]
        outs.append((dq, dk, drhs_u * b, dbeta, dg))
    return outs, new_dss


def _short_conv(raw, halo, w):
    tb = raw.shape[0]
    ext = jnp.concatenate([halo, raw], axis=0)
    out = ext[5:5 + tb] * w[0:1]
    for j in range(1, SHORT_K):
        out = out + ext[5 + j:5 + j + tb] * w[j:j + 1]
    return out


def _dn_gates(ba, alog8, dtb8):
    xg = ba + dtb8
    ea8 = jnp.exp(alog8)
    return _sig(ba), -ea8 * _softplus(xg), xg, ea8


def _dn_item(pre_q, pre_k, pre_v, b, g):
    qc, kc, vc = (p * _sig(p) for p in (pre_q, pre_k, pre_v))
    nq = lax.rsqrt(_sum1(qc * qc) + EPS)
    nk = lax.rsqrt(_sum1(kc * kc) + EPS)
    return dict(q=qc * nq * (DK ** -0.5), k=kc * nk, v=vc, b=b, g=g, qc=qc, kc=kc, nq=nq, nk=nk)


def _dn_out(o, z, sz, og):
    r = lax.rsqrt(_mean(o * o) + EPS)
    xh = o * r
    return xh * og * (z * sz), xh, r


def _sds(shape, dtype):
    return jax.ShapeDtypeStruct(tuple(shape), dtype)


def _matmul(a, b, out_dtype, tm, tn, tk, name):
    m, kd = a.shape
    n = b.shape[1]
    tm, tn, tk = min(tm, m), min(tn, n), min(tk, kd)
    nk = kd // tk

    def body(a_ref, b_ref, o_ref, acc):
        @pl.when(pl.program_id(2) == 0)
        def _():
            acc[...] = jnp.zeros_like(acc)

        acc[...] += jnp.dot(a_ref[...], b_ref[...], preferred_element_type=F32)

        @pl.when(pl.program_id(2) == nk - 1)
        def _():
            o_ref[...] = acc[...].astype(o_ref.dtype)

    return _pcall(
        body, name=name, grid=(m // tm, n // tn, nk),
        in_specs=[pl.BlockSpec((tm, tk), lambda i, j, k: (i, k)), pl.BlockSpec((tk, tn), lambda i, j, k: (k, j))],
        out_specs=pl.BlockSpec((tm, tn), lambda i, j, k: (i, j)), out_shape=_sds((m, n), out_dtype),
        scratch_shapes=[pltpu.VMEM((tm, tn), F32)], compiler_params=_cparams(("parallel", "parallel", "arbitrary")),
    )(a, b)


def _travel_copies(v_refs, o_refs, send_sems, recv_sems):
    x, y, c = _axes()
    peers = _chip_peers(x, y)
    return [_remote(v_refs[a].at[2 * peers[k][0] + peers[k][1]], o_refs[a].at[k], send_sems, recv_sems, 3 * a + k,
                    (*peers[k], c)) for k in range(3) for a in range(len(v_refs))]


def _matmul_ksegs(a_segs, b, tm, tn, tk, name, travel=()):
    m, n = a_segs[0].shape[0], b.shape[1]
    tm, tn = min(tm, m), min(tn, n)
    main, tail = a_segs[:-1], a_segs[-1]
    steps = [s.shape[1] // tk for s in main]
    starts = [sum(steps[:i]) for i in range(len(main))]
    nk = sum(steps)
    wt = tail.shape[1]
    n_main, nv = len(main), len(travel)
    grid = (m // tm, n // tn, nk + 1)

    def body(*refs):
        a_refs, (at_ref, b_ref, bt_ref), refs = refs[:n_main], refs[n_main:n_main + 3], refs[n_main + 3:]
        v_refs, o_ref, refs = refs[:nv], refs[nv], refs[nv + 1:]
        got_refs, acc, sems = refs[:nv], refs[nv], refs[nv + 1:]
        k = pl.program_id(2)
        if nv:
            first = (pl.program_id(0) == 0) & (pl.program_id(1) == 0) & (k == 0)
            last = (pl.program_id(0) == grid[0] - 1) & (pl.program_id(1) == grid[1] - 1) & (k == nk)

            @pl.when(first)
            def _():
                for cp in _travel_copies(v_refs, got_refs, *sems):
                    cp.start()

        @pl.when(k == 0)
        def _():
            acc[...] = jnp.zeros_like(acc)

        for a_ref, k0, ns in zip(a_refs, starts, steps):
            @pl.when((k >= k0) & (k < k0 + ns))
            def _():
                acc[...] += jnp.dot(a_ref[...], b_ref[...], preferred_element_type=F32)

        @pl.when(k == nk)
        def _():
            o_ref[...] = acc[...] + jnp.dot(at_ref[...], bt_ref[...], preferred_element_type=F32)

        if nv:
            @pl.when(last)
            def _():
                for cp in _travel_copies(v_refs, got_refs, *sems):
                    cp.wait()

    seg_specs = [pl.BlockSpec((tm, tk), functools.partial(lambda i, j, k, k0, n_s: (i, jnp.clip(k - k0, 0, n_s - 1)),
                                                          k0=k0, n_s=n_s)) for k0, n_s in zip(starts, steps)]
    out = _pcall(
        body, name=name, grid=grid,
        in_specs=seg_specs + [pl.BlockSpec((tm, wt), lambda i, j, k: (i, 0)),
                              pl.BlockSpec((tk, tn), lambda i, j, k: (jnp.minimum(k, nk - 1), j)),
                              pl.BlockSpec((wt, tn), lambda i, j, k: (nk * tk // wt, j))] + [_ANY] * nv,
        out_specs=[pl.BlockSpec((tm, tn), lambda i, j, k: (i, j))] + [_ANY] * nv,
        out_shape=[_sds((m, n), F32)] + [_sds((3,) + v.shape[1:], v.dtype) for v in travel],
        scratch_shapes=[pltpu.VMEM((tm, tn), F32)] + [pltpu.SemaphoreType.DMA((3 * nv,))] * (2 if nv else 0),
        compiler_params=_cparams(("arbitrary",) * 3 if nv else ("parallel", "parallel", "arbitrary")),
    )(*main, tail, b, b, *travel)
    return (out[0], out[1:]) if nv else out[0]


def _rms_fwd_call(x, g, name):
    m = x.shape[0]
    tm = min(512, m)

    def body(x_ref, g_ref, o_ref):
        xv = x_ref[...]
        o_ref[...] = (xv * lax.rsqrt(_mean(xv * xv) + EPS) * g_ref[...]).astype(BF16)

    return _pcall(
        body, name=name, grid=(m // tm,),
        in_specs=[pl.BlockSpec((tm, D_MODEL), lambda i: (i, 0)), pl.BlockSpec((1, D_MODEL), lambda i: (0, 0))],
        out_specs=pl.BlockSpec((tm, D_MODEL), lambda i: (i, 0)), out_shape=_sds((m, D_MODEL), BF16),
        compiler_params=_cparams(("parallel",)),
    )(x, g)


def _rms_bwd_call(x, dh, g, dres, name):
    m = x.shape[0]
    tm = min(512, m)

    def body(x_ref, dh_ref, g_ref, dr_ref, dx_ref, dg_ref):
        @pl.when(pl.program_id(0) == 0)
        def _():
            dg_ref[...] = jnp.zeros_like(dg_ref)

        xv, dhv = x_ref[...], dh_ref[...]
        r = lax.rsqrt(_mean(xv * xv) + EPS)
        xh = xv * r
        dx_ref[...] = _rms_bwd(dhv * g_ref[...], xh, r) + dr_ref[...]
        dg_ref[...] += _rows8(_sum0(dhv * xh))

    row = pl.BlockSpec((tm, D_MODEL), lambda i: (i, 0))
    return _pcall(
        body, name=name, grid=(m // tm,),
        in_specs=[row, row, pl.BlockSpec((1, D_MODEL), lambda i: (0, 0)), row],
        out_specs=[row, pl.BlockSpec((8, D_MODEL), lambda i: (0, 0))],
        out_shape=[_sds((m, D_MODEL), F32), _sds((8, D_MODEL), F32)], compiler_params=_cparams(("arbitrary",)),
    )(x, dh, g, dres)


def _loss_call(x, tgt, g):
    m = x.shape[0]
    tm = min(512, m)

    def body(x_ref, t_ref, g_ref, dx_ref, dg_ref, l_ref):
        @pl.when(pl.program_id(0) == 0)
        def _():
            dg_ref[...] = jnp.zeros_like(dg_ref)
            l_ref[...] = jnp.zeros_like(l_ref)

        xv = x_ref[...]
        r = lax.rsqrt(_mean(xv * xv) + EPS)
        xh = xv * r
        err = xh * g_ref[...] - t_ref[...]
        dy = err * (1.0 / D_MODEL)
        dx_ref[...] = _rms_bwd(dy * g_ref[...], xh, r)
        dg_ref[...] += _rows8(_sum0(dy * xh))
        l_ref[...] += 0.5 * _sum0(_mean(err * err))

    row = pl.BlockSpec((tm, D_MODEL), lambda i: (i, 0))
    return _pcall(
        body, name="loss_head", grid=(m // tm,),
        in_specs=[row, row, pl.BlockSpec((1, D_MODEL), lambda i: (0, 0))],
        out_specs=[row, pl.BlockSpec((8, D_MODEL), lambda i: (0, 0)), pl.BlockSpec((8, 128), lambda i: (0, 0))],
        out_shape=[_sds((m, D_MODEL), F32), _sds((8, D_MODEL), F32), _sds((8, 128), F32)],
        compiler_params=_cparams(("arbitrary",)),
    )(x, tgt, g)


def _halo_idx(i, rows):
    return jnp.maximum(i * (TB // rows) - 1, 0)


def _a_tiles(nt, rev):
    ti = (lambda i: nt - 1 - i) if rev else (lambda i: i)
    c0 = A_OFF // CONV_W
    return [pl.BlockSpec((1, TB, CONV_W), functools.partial(lambda b, i, c: (b, ti(i), c), c=c0 + c)) for c in range(3)]


def _a_fwd_call(proj, w, bias, g, b, name):
    bsz, t, _ = proj.shape
    nt = t // TB
    c0 = A_OFF // CONV_W

    def body(val_ref, glu_ref, az_ref, vh_ref, gh_ref, w_ref, bias_ref, g_ref, b_ref, y_ref, c_ref):
        keep = jnp.where(pl.program_id(1) > 0, 1.0, 0.0)
        y, c = _a_fwd(val_ref[0], glu_ref[0], az_ref[0], vh_ref[0] * keep, gh_ref[0], w_ref[...], bias_ref[...],
                      g_ref[...], b_ref[...])
        y_ref[0] = y.astype(BF16)
        c_ref[0] = c

    halo = [pl.BlockSpec((1, HALO_A, CONV_W), functools.partial(lambda b, i, c: (b, _halo_idx(i, HALO_A), c), c=c0 + c))
            for c in range(2)]
    par = [pl.BlockSpec((HALO_A, CONV_W), lambda b, i: (0, 0))] + [pl.BlockSpec((1, CONV_W), lambda b, i: (0, 0))] * 3
    tile = pl.BlockSpec((1, TB, CONV_W), lambda b, i: (b, i, 0))
    return _pcall(
        body, name=name, grid=(bsz, nt), in_specs=_a_tiles(nt, False) + halo + par, out_specs=[tile, tile],
        out_shape=[_sds((bsz, t, CONV_W), BF16), _sds((bsz, t, CONV_W), F32)],
        compiler_params=_cparams(("parallel", "parallel")),
    )(proj, proj, proj, proj, proj, w, bias, g, b)


def _a_bwd_call(dy, conv, proj, w, g, b, name, travel=()):
    bsz, t, _ = proj.shape
    nt = t // TB
    nv = len(travel)

    def body(*refs):
        (dy_ref, c_ref, val_ref, glu_ref, az_ref, w_ref, g_ref, b_ref), refs = refs[:8], refs[8:]
        v_refs, (da_ref, dw_ref, ds_ref), refs = refs[:nv], refs[nv:nv + 3], refs[nv + 3:]
        o_refs, carry, sems = refs[:nv], refs[nv], refs[nv + 1:]
        ip = pl.program_id(1)

        def exchange():
            return _travel_copies(v_refs, o_refs, *sems)

        @pl.when((pl.program_id(0) == 0) & (ip == 0))
        def _():
            dw_ref[...] = jnp.zeros_like(dw_ref)
            ds_ref[...] = jnp.zeros_like(ds_ref)
            if nv:
                for cp in exchange():
                    cp.start()

        @pl.when(ip == 0)
        def _():
            carry[...] = jnp.zeros_like(carry)

        dval, dglu, daz, dw, dbias, dg, db, head = _a_bwd(dy_ref[0], val_ref[0], glu_ref[0], az_ref[0], c_ref[0],
                                                          w_ref[...], g_ref[...], b_ref[...], carry[...])
        carry[...] = head
        for n, dcol in enumerate((dval, dglu, daz)):
            da_ref[0, :, n * CONV_W:(n + 1) * CONV_W] = dcol.astype(BF16)
        dw_ref[...] += dw
        ds_ref[...] += _rows8(dbias, dg, db)

        if nv:
            @pl.when((pl.program_id(0) == bsz - 1) & (ip == nt - 1))
            def _():
                for cp in exchange():
                    cp.wait()

    rtile = lambda b, i: (b, nt - 1 - i, 0)
    par = [pl.BlockSpec((HALO_A, CONV_W), lambda b, i: (0, 0))] + [pl.BlockSpec((1, CONV_W), lambda b, i: (0, 0))] * 2
    return _pcall(
        body, name=name, grid=(bsz, nt),
        in_specs=[pl.BlockSpec((1, TB, CONV_W), rtile)] * 2 + _a_tiles(nt, True) + par + [_ANY] * nv,
        out_specs=[pl.BlockSpec((1, TB, 3 * CONV_W), rtile), pl.BlockSpec((HALO_A, CONV_W), lambda b, i: (0, 0)),
                   pl.BlockSpec((8, CONV_W), lambda b, i: (0, 0))] + [_ANY] * nv,
        out_shape=[_sds((bsz, t, 3 * CONV_W), BF16), _sds((HALO_A, CONV_W), F32), _sds((8, CONV_W), F32)]
        + [_sds((3,) + v.shape[1:], v.dtype) for v in travel],
        scratch_shapes=[pltpu.VMEM((HALO_A, CONV_W), F32)] + [pltpu.SemaphoreType.DMA((3 * nv,))] * (2 if nv else 0),
        compiler_params=_cparams(("arbitrary", "arbitrary")),
    )(dy, conv, proj, proj, proj, w, g, b, *travel)


def _c_specs():
    c0 = C_OFF // SG_W
    tile = [pl.BlockSpec((1, TB_L, SG_W), functools.partial(lambda b, i, c: (b, i, c), c=c0 + c)) for c in range(3)]
    par = [pl.BlockSpec((1, SG_W), lambda b, i: (0, 0))] * 2 + [
        pl.BlockSpec((SG_G, SG_C, SG_C), lambda b, i: (0, 0, 0)), pl.BlockSpec((SG_C, SG_W), lambda b, i: (0, 0))]
    return tile + par


def _c_fwd_call(proj, g, b, ws, bias_full, name):
    bsz, t, _ = proj.shape

    def body(cu_ref, cv_ref, cz_ref, g_ref, b_ref, ws_ref, bf_ref, y_ref):
        y, _ = _c_fwd(cu_ref[0], cv_ref[0], cz_ref[0], g_ref[...], b_ref[...], ws_ref[...], bf_ref[...])
        y_ref[0] = y.astype(BF16)

    return _pcall(
        body, name=name, grid=(bsz, t // TB_L), in_specs=_c_specs(),
        out_specs=pl.BlockSpec((1, TB_L, SG_W), lambda b, i: (b, i, 0)), out_shape=_sds((bsz, t, SG_W), BF16),
        compiler_params=_cparams(("parallel", "parallel")),
    )(proj, proj, proj, g, b, ws, bias_full)


def _c_bwd_call(dy, proj, g, b, ws, bias_full, name):
    bsz, t, _ = proj.shape

    def body(dy_ref, cu_ref, cv_ref, cz_ref, g_ref, b_ref, ws_ref, bf_ref, dc_ref, dws_ref, dbs_ref, ds_ref):
        @pl.when((pl.program_id(0) == 0) & (pl.program_id(1) == 0))
        def _():
            dws_ref[...] = jnp.zeros_like(dws_ref)
            dbs_ref[...] = jnp.zeros_like(dbs_ref)
            ds_ref[...] = jnp.zeros_like(ds_ref)

        dcu, dcv, dcz, dws, dbs, dg, db = _c_bwd(dy_ref[0], cu_ref[0], cv_ref[0], cz_ref[0], g_ref[...], b_ref[...],
                                                 ws_ref[...], bf_ref[...])
        for n, dcol in enumerate((dcu, dcv, dcz)):
            dc_ref[0, :, n * SG_W:(n + 1) * SG_W] = dcol.astype(BF16)
        for gi in range(SG_G):
            dws_ref[gi] += dws[gi]
        dbs_ref[...] += dbs
        ds_ref[...] += _rows8(dg, db)

    tile = lambda b, i: (b, i, 0)
    return _pcall(
        body, name=name, grid=(bsz, t // TB_L), in_specs=[pl.BlockSpec((1, TB_L, SG_W), tile)] + _c_specs(),
        out_specs=[pl.BlockSpec((1, TB_L, 3 * SG_W), tile), pl.BlockSpec((SG_G, SG_C, SG_C), lambda b, i: (0, 0, 0)),
                   pl.BlockSpec((SG_C, SG_W), lambda b, i: (0, 0)), pl.BlockSpec((8, SG_W), lambda b, i: (0, 0))],
        out_shape=[_sds((bsz, t, 3 * SG_W), BF16), _sds((SG_G, SG_C, SG_C), F32), _sds((SG_C, SG_W), F32),
                   _sds((8, SG_W), F32)],
        compiler_params=_cparams(("arbitrary", "arbitrary")),
    )(dy, proj, proj, proj, g, b, ws, bias_full)


def _m_specs(tb=TB):
    g0 = G_OFF // D_MODEL
    y = [pl.BlockSpec((1, tb, n), lambda b, i: (b, i, 0)) for n in (CONV_W, D_MODEL, SG_W)]
    gates = [pl.BlockSpec((1, tb, D_MODEL), functools.partial(lambda b, i, c: (b, i, c), c=g0 + c)) for c in range(3)]
    return y + gates


def _w_specs(*shapes):
    return [pl.BlockSpec(s, lambda b, i: (0, 0)) for s in shapes]


def _m_fwd_call(ya, yb, yc, proj, x, wa, wb, wc, wo, name):
    bsz, t, _ = x.shape

    def body(ya_ref, yb_ref, yc_ref, g0_ref, g1_ref, g2_ref, x_ref, wa_ref, wb_ref, wc_ref, wo_ref, o_ref):
        merged, _ = _m_fwd(ya_ref[0], yb_ref[0], yc_ref[0], g0_ref[0], g1_ref[0], g2_ref[0], wa_ref[...], wb_ref[...],
                           wc_ref[...])
        o_ref[0] = x_ref[0] + _nn(merged, wo_ref[...])

    tile = pl.BlockSpec((1, TB_L, D_MODEL), lambda b, i: (b, i, 0))
    return _pcall(
        body, name=name, grid=(bsz, t // TB_L),
        in_specs=_m_specs(TB_L) + [tile] + _w_specs(wa.shape, wb.shape, wc.shape, wo.shape),
        out_specs=tile, out_shape=_sds(x.shape, F32), compiler_params=_cparams(("parallel", "parallel")),
    )(ya, yb, yc, proj, proj, proj, x, wa, wb, wc, wo)


def _m_bwd_call(dout, ya, yb, yc, proj, wa, wb, wc, wo, name):
    bsz, t, _ = dout.shape

    def body(do_ref, ya_ref, yb_ref, yc_ref, g0_ref, g1_ref, g2_ref, wa_ref, wb_ref, wc_ref, wo_ref, dya_ref, dyb_ref,
             dyc_ref, dg_ref, mg_ref, dp_ref):
        dys, dgs, merged, dps = _m_bwd(do_ref[0], ya_ref[0], yb_ref[0], yc_ref[0], g0_ref[0], g1_ref[0], g2_ref[0],
                                       wa_ref[...], wb_ref[...], wc_ref[...], wo_ref[...])
        dya_ref[0], dyb_ref[0], dyc_ref[0] = dys
        mg_ref[0] = merged.astype(BF16)
        for n in range(3):
            dg_ref[0, :, n * D_MODEL:(n + 1) * D_MODEL] = dgs[n].astype(BF16)
            dp_ref[0, :, n * D_MODEL:(n + 1) * D_MODEL] = dps[n].astype(BF16)

    tile = lambda n: pl.BlockSpec((1, TB, n), lambda b, i: (b, i, 0))
    widths = (CONV_W, D_MODEL, SG_W, 3 * D_MODEL, D_MODEL, 3 * D_MODEL)
    dts = (F32, F32, F32, BF16, BF16, BF16)
    return _pcall(
        body, name=name, grid=(bsz, t // TB),
        in_specs=[tile(D_MODEL)] + _m_specs() + _w_specs(wa.shape, wb.shape, wc.shape, wo.shape),
        out_specs=[tile(n) for n in widths], out_shape=[_sds((bsz, t, n), d) for n, d in zip(widths, dts)],
        compiler_params=_cparams(("parallel", "parallel")),
    )(dout, ya, yb, yc, proj, proj, proj, wa, wb, wc, wo)


def _m_wgrad_call(ya, yb, yc, merged, dps, dout, name):
    bsz, t, _ = dout.shape

    def body(ya_ref, yb_ref, yc_ref, mg_ref, dp_ref, do_ref, dwa_ref, dwb_ref, dwc_ref, dwo_ref):
        @pl.when((pl.program_id(0) == 0) & (pl.program_id(1) == 0))
        def _():
            for r in (dwa_ref, dwb_ref, dwc_ref, dwo_ref):
                r[...] = jnp.zeros_like(r)

        dp = dp_ref[0]
        dwa_ref[...] += _tn(ya_ref[0], dp[:, :D_MODEL])
        dwb_ref[...] += _tn(yb_ref[0], dp[:, D_MODEL:2 * D_MODEL])
        dwc_ref[...] += _tn(yc_ref[0], dp[:, 2 * D_MODEL:])
        dwo_ref[...] += _tn(mg_ref[0], do_ref[0])

    tile = lambda n: pl.BlockSpec((1, TB, n), lambda b, i: (b, i, 0))
    shapes = ((CONV_W, D_MODEL), (D_MODEL, D_MODEL), (SG_W, D_MODEL), (D_MODEL, D_MODEL))
    return _pcall(
        body, name=name, grid=(bsz, t // TB),
        in_specs=[tile(CONV_W), tile(D_MODEL), tile(SG_W), tile(D_MODEL), tile(3 * D_MODEL), tile(D_MODEL)],
        out_specs=_w_specs(*shapes), out_shape=[_sds(s, F32) for s in shapes],
        compiler_params=_cparams(("arbitrary", "arbitrary")),
    )(ya, yb, yc, merged, dps, dout)


def _dn_specs(nc, rev):
    ti = (lambda i: nc - 1 - i) if rev else (lambda i: i)
    return [pl.BlockSpec((1, CHUNK, B_W), lambda b, i: (b, ti(i), 0)),
            pl.BlockSpec((1, CHUNK, DK), lambda b, i: (b, ti(i), BA_OFF // DK)),
            pl.BlockSpec((SHORT_K, 3 * D_MODEL), lambda b, i: (0, 0))] + [pl.BlockSpec((1, DK), lambda b, i: (0, 0))] * 3


def _dn_saved_specs(nc, rev):
    ti = (lambda i: nc - 1 - i) if rev else (lambda i: i)
    return [pl.BlockSpec((1, CHUNK, D_MODEL), lambda b, i: (b, ti(i), 0)),
            pl.BlockSpec((1, CHUNK, 3 * D_MODEL), lambda b, i: (b, ti(i), 0)),
            pl.BlockSpec((1, HEADS, 1, DK, DK), lambda b, i: (b, 0, ti(i), 0, 0)),
            pl.BlockSpec((1, 1, CHUNK, HEADS * CHUNK), lambda b, i: (b, ti(i), 0, 0)),
            pl.BlockSpec((1, CHUNK, 2 * D_MODEL), lambda b, i: (b, ti(i), 0)),
            pl.BlockSpec((1, CHUNK, D_MODEL), lambda b, i: (b, ti(i), 0))]


def _gather_fused(x_refs, o_refs, send_sems, recv_sems, finish):
    x, y, c = _axes()
    chip, sib, peers = 2 * x + y, (x, y, 1 - c), _chip_peers(x, y)
    pchip = [2 * px + py for px, py in peers]
    n = len(x_refs)
    halves = [r.shape[0] // 2 for r in x_refs]

    def half(a, cc):
        return pl.ds(cc * halves[a], halves[a])

    first = [_remote(x_refs[a].at[half(a, c)], o_refs[a].at[chip, half(a, c)], send_sems, recv_sems, 6 * a + k,
                     (*peers[k], c)) for k in range(3) for a in range(n)]
    if not finish:
        for cp in first:
            cp.start()
        return
    passed = []
    for k in range(3):
        for a in range(n):
            land = o_refs[a].at[pchip[k], half(a, c)]
            _remote(land, land, send_sems, recv_sems, 6 * a + k, (*peers[k], c)).wait_recv()
            passed.append(_remote(land, land, send_sems, recv_sems, 6 * a + 3 + k, sib))
            passed[-1].start()
    for k in range(3):
        for a in range(n):
            land = o_refs[a].at[pchip[k], half(a, 1 - c)]
            _remote(land, land, send_sems, recv_sems, 6 * a + 3 + k, sib).wait_recv()
    for cp in first + passed:
        cp.wait_send()


def _dn_fwd_call(proj, wconv, alog, dtb, og, name, gather=()):
    bsz, t, _ = proj.shape
    nc = t // CHUNK
    ng = len(gather)

    def body(*refs):
        (x_ref, ba_ref, w_ref, al_ref, dt_ref, og_ref, halo_ref), refs = refs[:7], refs[7:]
        g_refs, (y_ref, o_ref, pre_ref, st_ref, t_ref, uw_ref, vn_ref), refs = refs[:ng], refs[ng:ng + 7], refs[ng + 7:]
        go_refs, s_scr, sems = refs[:ng], refs[ng], refs[ng + 1:]
        i = pl.program_id(1)

        if ng:
            @pl.when((pl.program_id(0) == 0) & (i == 0))
            def _():
                _gather_fused(g_refs, go_refs, *sems, finish=False)

        @pl.when(i == 0)
        def _():
            s_scr[...] = jnp.zeros_like(s_scr)

        keep = jnp.where(i > 0, 1.0, 0.0)
        bsig, gfull, _, _ = _dn_gates(ba_ref[0], al_ref[...], dt_ref[...])
        items = []
        for h in range(HEADS):
            pres = []
            for n in range(3):
                cs = slice(n * D_MODEL + h * DK, n * D_MODEL + (h + 1) * DK)
                pres.append(_short_conv(x_ref[0, :, cs], halo_ref[0, :, cs] * keep, w_ref[:, cs]))
                pre_ref[0, :, cs] = pres[-1]
            items.append(_dn_item(*pres, bsig[:, h:h + 1], gfull[:, HEADS + h:HEADS + h + 1]))
        ss = [s_scr[h] for h in range(HEADS)]
        for h in range(HEADS):
            st_ref[0, h, 0] = ss[h]
        outs, ss, vnew = _dn_fwd_chunk(items, ss)
        og_v = og_ref[...]
        for h, d in enumerate(items):
            s_scr[h] = ss[h]
            hs = slice(h * DK, (h + 1) * DK)
            z = x_ref[0, :, 3 * D_MODEL + h * DK:3 * D_MODEL + (h + 1) * DK]
            o_ref[0, :, hs] = outs[h]
            y_ref[0, :, hs] = _dn_out(outs[h], z, _sig(z), og_v)[0].astype(BF16)
            t_ref[0, 0, :, h * CHUNK:(h + 1) * CHUNK] = d["t"].astype(BF16)
            uw_ref[0, :, 2 * h * DK:(2 * h + 1) * DK] = d["u"].astype(BF16)
            uw_ref[0, :, (2 * h + 1) * DK:2 * (h + 1) * DK] = d["w"].astype(BF16)
            vn_ref[0, :, hs] = vnew[h].astype(BF16)

        if ng:
            @pl.when((pl.program_id(0) == bsz - 1) & (i == nc - 1))
            def _():
                _gather_fused(g_refs, go_refs, *sems, finish=True)

    halo = pl.BlockSpec((1, HALO_B, 3 * D_MODEL), lambda b, i: (b, jnp.maximum(i * (CHUNK // HALO_B) - 1, 0), 0))
    outs = _pcall(
        body, name=name, grid=(bsz, nc), in_specs=_dn_specs(nc, False) + [halo] + [_ANY] * ng,
        out_specs=[pl.BlockSpec((1, CHUNK, D_MODEL), lambda b, i: (b, i, 0))] + _dn_saved_specs(nc, False) + [_ANY] * ng,
        out_shape=[_sds((bsz, t, D_MODEL), BF16), _sds((bsz, t, D_MODEL), F32), _sds((bsz, t, 3 * D_MODEL), F32),
                   _sds((bsz, HEADS, nc, DK, DK), F32), _sds((bsz, nc, CHUNK, HEADS * CHUNK), BF16),
                   _sds((bsz, t, 2 * D_MODEL), BF16), _sds((bsz, t, D_MODEL), BF16)]
        + [_sds((4,) + s.shape, s.dtype) for s in gather],
        scratch_shapes=[pltpu.VMEM((HEADS, DK, DK), F32)] + [pltpu.SemaphoreType.DMA((6 * ng,))] * (2 if ng else 0),
        compiler_params=_cparams(("arbitrary", "arbitrary")),
    )(proj, proj, wconv, alog, dtb, og, proj, *gather)
    if not ng:
        return outs
    chip = 2 * lax.axis_index("x") + lax.axis_index("y")
    return list(outs[:7]) + [lax.dynamic_update_slice_in_dim(o, s[None], chip, axis=0)
                             for o, s in zip(outs[7:], gather)]


def _dn_bwd_call(dy, saved, proj, wconv, alog, dtb, og, name):
    bsz, t, _ = proj.shape
    nc = t // CHUNK

    def body(dy_ref, o_ref, pre_ref, st_ref, t_ref, uw_ref, vn_ref, x_ref, ba_ref, w_ref, al_ref, dt_ref, og_ref,
             dx_ref, dba_ref, dwc_ref, dsm_ref, ds_scr, dpre_scr):
        ip = pl.program_id(1)

        @pl.when((pl.program_id(0) == 0) & (ip == 0))
        def _():
            dwc_ref[...] = jnp.zeros_like(dwc_ref)
            dsm_ref[...] = jnp.zeros_like(dsm_ref)

        @pl.when(ip == 0)
        def _():
            ds_scr[...] = jnp.zeros_like(ds_scr)
            dpre_scr[...] = jnp.zeros_like(dpre_scr)

        bsig, gfull, xg, ea8 = _dn_gates(ba_ref[0], al_ref[...], dt_ref[...])
        og_v = og_ref[...]
        items, dog = [], jnp.zeros((1, DK), F32)
        for h in range(HEADS):
            hs = slice(h * DK, (h + 1) * DK)
            zs = slice(3 * D_MODEL + h * DK, 3 * D_MODEL + (h + 1) * DK)
            d = _dn_item(*(pre_ref[0, :, n * D_MODEL + h * DK:n * D_MODEL + (h + 1) * DK] for n in range(3)),
                         bsig[:, h:h + 1], gfull[:, HEADS + h:HEADS + h + 1])
            z, dyv = x_ref[0, :, zs], dy_ref[0, :, hs]
            sz = _sig(z)
            _, xh, r = _dn_out(o_ref[0, :, hs], z, sz, og_v)
            dx_ref[0, :, zs] = (dyv * xh * og_v * (sz * (1.0 + z * (1.0 - sz)))).astype(BF16)
            don = dyv * (z * sz)
            dog = dog + _sum0(don * xh)
            d.update(do=_rms_bwd(don * og_v, xh, r), t=t_ref[0, 0, :, h * CHUNK:(h + 1) * CHUNK],
                     u=uw_ref[0, :, 2 * h * DK:(2 * h + 1) * DK], w=uw_ref[0, :, (2 * h + 1) * DK:2 * (h + 1) * DK],
                     vnew=vn_ref[0, :, hs])
            items.append(d)
        grads, dss = _dn_bwd_chunk(items, [st_ref[0, h, 0] for h in range(HEADS)], [ds_scr[h] for h in range(HEADS)])
        lanes = lax.broadcasted_iota(jnp.int32, (CHUNK, DK), 1)
        lane8 = lax.broadcasted_iota(jnp.int32, (8, DK), 1)
        row8 = lax.broadcasted_iota(jnp.int32, (8, DK), 0)
        dgdx = -ea8 * _sig(xg)
        dba = jnp.zeros((CHUNK, DK), F32)
        dsm = jnp.where(row8 == 2, dog, 0.0)
        for h, (d, (dq, dk, dv, dbeta, dg)) in enumerate(zip(items, grads)):
            ds_scr[h] = dss[h]
            qc, kc, nq, nk = d["qc"], d["kc"], d["nq"], d["nk"]
            dacts = ((DK ** -0.5) * nq * (dq - qc * (nq * nq * _sum1(dq * qc))),
                     nk * (dk - kc * (nk * nk * _sum1(dk * kc))), dv)
            dal = dg * dgdx[:, HEADS + h:HEADS + h + 1]
            dba = dba + jnp.where(lanes == h, dbeta * d["b"] * (1.0 - d["b"]), 0.0) + jnp.where(lanes == HEADS + h, dal, 0.0)
            dsm = dsm + (jnp.where((row8 == 0) & (lane8 == h), _sum0(dg * d["g"]), 0.0)
                         + jnp.where((row8 == 1) & (lane8 == h), _sum0(dal), 0.0))
            for n in range(3):
                cs = slice(n * D_MODEL + h * DK, n * D_MODEL + (h + 1) * DK)
                pre, raw, w = pre_ref[0, :, cs], x_ref[0, :, cs], w_ref[:, cs]
                sp = _sig(pre)
                dpre = dacts[n] * (sp * (1.0 + pre * (1.0 - sp)))
                dext = jnp.concatenate([dpre, dpre_scr[:, cs]], axis=0)
                draw = dext[3:3 + CHUNK] * w[0:1]
                rows = [_sum0(dext[3:3 + CHUNK] * raw)]
                for j in range(1, SHORT_K):
                    shifted = dext[3 - j:3 - j + CHUNK]
                    draw = draw + shifted * w[j:j + 1]
                    rows.append(_sum0(shifted * raw))
                dpre_scr[:, cs] = dpre[:HALO_B]
                dx_ref[0, :, cs] = draw.astype(BF16)
                dwc_ref[:, cs] += _rows8(*rows)
        dba_ref[0] = dba.astype(BF16)
        dsm_ref[...] += dsm

    rrow = lambda n: pl.BlockSpec((1, CHUNK, n), lambda b, i: (b, nc - 1 - i, 0))
    return _pcall(
        body, name=name, grid=(bsz, nc), in_specs=[rrow(D_MODEL)] + _dn_saved_specs(nc, True) + _dn_specs(nc, True),
        out_specs=[rrow(B_W), rrow(DK), pl.BlockSpec((8, 3 * D_MODEL), lambda b, i: (0, 0)),
                   pl.BlockSpec((8, DK), lambda b, i: (0, 0))],
        out_shape=[_sds((bsz, t, B_W), BF16), _sds((bsz, t, DK), BF16), _sds((8, 3 * D_MODEL), F32), _sds((8, DK), F32)],
        scratch_shapes=[pltpu.VMEM((HEADS, DK, DK), F32), pltpu.VMEM((HALO_B, 3 * D_MODEL), F32)],
        compiler_params=_cparams(("arbitrary", "arbitrary")),
    )(dy, *saved, proj, proj, wconv, alog, dtb, og)


def _adamw_call(w, g, m, v, name):
    shape = w.shape
    view = (math.prod(shape[:-2]),) + shape[-2:] if len(shape) >= 2 else (1, 1) + shape
    lead, rows, cols = view
    tr, tl = (128, 1) if rows % 128 == 0 else (rows, min(lead, 64))
    c1, c2 = 1.0 - ADAM_B1 ** ADAM_STEP, 1.0 - ADAM_B2 ** ADAM_STEP

    def body(w_ref, g_ref, m_ref, v_ref, d_ref, nm_ref, nv_ref):
        gv = g_ref[...]
        nm = ADAM_B1 * m_ref[...] + (1.0 - ADAM_B1) * gv
        nv = ADAM_B2 * v_ref[...] + (1.0 - ADAM_B2) * (gv * gv)
        d_ref[...] = -ADAM_LR * ((nm / c1) / (jnp.sqrt(nv / c2) + ADAM_EPS) + ADAM_WD * w_ref[...])
        nm_ref[...] = nm
        nv_ref[...] = nv

    blk = pl.BlockSpec((tl, tr, cols), lambda l, i: (l, i, 0))
    outs = _pcall(
        body, name=name, grid=(pl.cdiv(lead, tl), rows // tr), in_specs=[blk] * 4, out_specs=[blk] * 3,
        out_shape=[_sds(view, F32)] * 3, compiler_params=_cparams(("parallel", "parallel")),
    )(*(a.reshape(view) for a in (w, g, m, v)))
    return tuple(o.reshape(shape) for o in outs)


def _row_tile(rows, cap=512):
    for tr in range(cap, 15, -16):
        if rows % tr == 0:
            return tr
    return rows


def _add_pair_call(by_chip, recv, where, name):
    _, n, rows, cols = by_chip.shape
    tr = _row_tile(rows, 256)

    def body(where_ref, a_ref, b_ref, sb_ref, own_ref):
        s = a_ref[0, 0] + b_ref[0]
        sb_ref[0] = s.astype(BF16)

        @pl.when(pl.program_id(1) == where_ref[1])
        def _():
            own_ref[...] = s

    grid_spec = pltpu.PrefetchScalarGridSpec(
        num_scalar_prefetch=1, grid=(rows // tr, n),
        in_specs=[pl.BlockSpec((1, 1, tr, cols), lambda i, j, wr: (wr[0], j, i, 0)),
                  pl.BlockSpec((1, tr, cols), lambda i, j, wr: (j, i, 0))],
        out_specs=[pl.BlockSpec((1, tr, cols), lambda i, j, wr: (j, i, 0)),
                   pl.BlockSpec((tr, cols), lambda i, j, wr: (i, 0))])
    return _pcall(
        body, name=name, grid_spec=grid_spec, out_shape=[_sds((n, rows, cols), BF16), _sds((rows, cols), F32)],
        compiler_params=_cparams(("parallel", "arbitrary")),
    )(where, by_chip, recv)


def _add_recv_call(own, recv, where, name):
    rows, cols = own.shape
    tr = _row_tile(rows, 256)

    def body(where_ref, o_ref, r_ref, s_ref):
        s_ref[0] = ((o_ref[...] + r_ref[0].astype(F32)) + r_ref[1].astype(F32)) + r_ref[2].astype(F32)

    grid_spec = pltpu.PrefetchScalarGridSpec(
        num_scalar_prefetch=1, grid=(rows // tr,),
        in_specs=[pl.BlockSpec((tr, cols), lambda i, wr: (i, 0)), pl.BlockSpec((3, tr, cols), lambda i, wr: (0, i, 0))],
        out_specs=pl.BlockSpec((1, tr, cols), lambda i, wr: (wr[0], i, 0)))
    return _pcall(
        body, name=name, grid_spec=grid_spec, out_shape=_sds((2, rows, cols), F32),
        compiler_params=_cparams(("parallel",)),
    )(where, own, recv)


def _axes():
    return lax.axis_index("x"), lax.axis_index("y"), lax.axis_index("c")


def _chip_peers(x, y):
    return [(x, 1 - y), (1 - x, y), (1 - x, 1 - y)]


_ANY = pl.BlockSpec(memory_space=pl.ANY)
_VMEM = pl.BlockSpec(memory_space=pltpu.VMEM)


def _remote(src, dst, send_sems, recv_sems, k, dev):
    return pltpu.make_async_remote_copy(src_ref=src, dst_ref=dst, send_sem=send_sems.at[k], recv_sem=recv_sems.at[k],
                                        device_id=dev, device_id_type=MESH)


def _gather_weights_call(shards):
    n = len(shards)
    halves = [s.shape[0] // 2 for s in shards]
    quarter = halves[0] // 2
    base = [0] + [8 + 6 * (a - 1) for a in range(1, n)]

    def body(*refs):
        x_refs, o_refs, (send_sems, recv_sems) = refs[:n], refs[n:2 * n], refs[2 * n:]
        x, y, c = _axes()
        chip, sib, peers = 2 * x + y, (x, y, 1 - c), _chip_peers(x, y)
        pchip = [2 * px + py for px, py in peers]

        def half(a, cc):
            return pl.ds(cc * halves[a], halves[a])

        def part(cc, q):
            return pl.ds(cc * halves[0] + q * quarter, quarter)

        def copy(ref, k, dev):
            return _remote(ref, ref, send_sems, recv_sems, k, dev)

        sends = [_remote(x_refs[a].at[half(a, c)], o_refs[a].at[chip, half(a, c)], send_sems, recv_sems, base[a] + k,
                         (*peers[k], c)) for k in range(3) for a in range(n) if a > 0 or k < 2]
        for cp in sends:
            cp.start()
        for k in range(2):
            copy(o_refs[0].at[pchip[k], half(0, c)], k, (*peers[k], c)).wait_recv()
            sends.append(copy(o_refs[0].at[pchip[k], part(c, k)], 2 + k, (*peers[1 - k], c)))
            sends.append(copy(o_refs[0].at[pchip[k], half(0, c)], 4 + k, sib))
            sends[-2].start()
            sends[-1].start()
        for k in range(3):
            for a in range(1, n):
                land = o_refs[a].at[pchip[k], half(a, c)]
                copy(land, base[a] + k, (*peers[k], c)).wait_recv()
                sends.append(copy(land, base[a] + 3 + k, sib))
                sends[-1].start()
        for q in range(2):
            land = o_refs[0].at[pchip[2], part(c, q)]
            copy(land, 2 + q, (*peers[1 - q], c)).wait_recv()
            sends.append(copy(land, 6 + q, sib))
            sends[-1].start()
        for k in range(2):
            copy(o_refs[0].at[pchip[k], half(0, 1 - c)], 4 + k, sib).wait_recv()
            copy(o_refs[0].at[pchip[2], part(1 - c, k)], 6 + k, sib).wait_recv()
        for k in range(3):
            for a in range(1, n):
                copy(o_refs[a].at[pchip[k], half(a, 1 - c)], base[a] + 3 + k, sib).wait_recv()
        for cp in sends:
            cp.wait_send()

    n_sems = 8 + 6 * (n - 1)
    outs = _pcall(
        body, name="gather_weights", in_specs=[_ANY] * n, out_specs=[_ANY] * n,
        out_shape=[_sds((4,) + s.shape, s.dtype) for s in shards],
        scratch_shapes=[pltpu.SemaphoreType.DMA((n_sems,)), pltpu.SemaphoreType.DMA((n_sems,))],
    )(*shards)
    chip = 2 * lax.axis_index("x") + lax.axis_index("y")
    return [lax.dynamic_update_slice_in_dim(o, s[None], chip, axis=0) for o, s in zip(outs, shards)]


def _pair_partials_call(by_chip, name):
    n = len(by_chip)

    def body(*refs):
        v_refs, o_refs, (send_sems, recv_sems) = refs[:n], refs[n:2 * n], refs[2 * n:]
        x, y, c = _axes()
        cps = [_remote(v_refs[a].at[1 - c], o_refs[a], send_sems, recv_sems, a, (x, y, 1 - c)) for a in range(n)]
        for cp in cps:
            cp.start()
        for cp in cps:
            cp.wait()

    return _pcall(
        body, name=name, in_specs=[_ANY] * n, out_specs=[_ANY] * n,
        out_shape=[_sds(v.shape[1:], v.dtype) for v in by_chip],
        scratch_shapes=[pltpu.SemaphoreType.DMA((n,)), pltpu.SemaphoreType.DMA((n,))],
    )(*by_chip)


def _pair_result_call(fins, name):
    n = len(fins)

    def body(*refs):
        v_refs, o_refs, (send_sems, recv_sems) = refs[:n], refs[n:2 * n], refs[2 * n:]
        x, y, c = _axes()
        cps = [_remote(v_refs[a].at[c], o_refs[a].at[c], send_sems, recv_sems, a, (x, y, 1 - c)) for a in range(n)]
        for cp in cps:
            cp.start()
        for a in range(n):
            cps[a].wait_send()
            _remote(v_refs[a].at[c], o_refs[a].at[1 - c], send_sems, recv_sems, a, (x, y, 1 - c)).wait_recv()

    return _pcall(
        body, name=name, in_specs=[_ANY] * n, out_specs=[_ANY] * n,
        out_shape=[_sds(v.shape, v.dtype) for v in fins], input_output_aliases={a: a for a in range(n)},
        scratch_shapes=[pltpu.SemaphoreType.DMA((n,)), pltpu.SemaphoreType.DMA((n,))],
    )(*fins)


def _allreduce_small_call(v):
    rows, cols = v.shape

    def body(v_ref, o_ref, buf, send_sems, recv_sems):
        x, y, c = _axes()
        chip, peers = 2 * x + y, _chip_peers(x, y)
        pair = _remote(v_ref, buf.at[0], send_sems, recv_sems, 0, (x, y, 1 - c))
        pair.start()
        pair.wait()
        buf[0] = v_ref[...] + buf[0]
        cps = [_remote(buf.at[0], buf.at[1 + k], send_sems, recv_sems, 1 + k, (*peers[k], c)) for k in range(3)]
        for cp in cps:
            cp.start()
        for cp in cps:
            cp.wait()
        acc = buf[chip]
        for d in range(1, 4):
            acc = acc + buf[lax.bitwise_xor(chip, d)]
        o_ref[...] = acc

    return _pcall(
        body, name="allreduce_small", in_specs=[_VMEM], out_specs=_VMEM, out_shape=_sds(v.shape, F32),
        scratch_shapes=[pltpu.VMEM((4, rows, cols), F32), pltpu.SemaphoreType.DMA((4,)), pltpu.SemaphoreType.DMA((4,))],
        compiler_params=_cparams(),
    )(v)


def _pack_rows(arrays, dtype, total_rows=None):
    parts = []
    for a in arrays:
        flat = a.astype(dtype).reshape(-1)
        parts.append(jnp.pad(flat, (0, -flat.shape[0] % D_MODEL)).reshape(-1, D_MODEL))
    out = jnp.concatenate(parts, axis=0)
    total_rows = total_rows or out.shape[0] + (-out.shape[0] % 8)
    return jnp.pad(out, ((0, total_rows - out.shape[0]), (0, 0)))


def _unpack_rows(packed, shapes):
    out, r = [], 0
    for s in shapes:
        n = math.prod(s)
        nr = -(-n // D_MODEL)
        out.append(packed[r:r + nr].reshape(-1)[:n].reshape(s))
        r += nr
    return out


def _to_r(w_t):
    pad = jnp.zeros((N_R - N_ORIG,) + w_t.shape[1:], w_t.dtype)
    return jnp.concatenate([w_t[A_ORIG_W:BA_ORIG], w_t[:A_ORIG_W], w_t[BA_ORIG + 2 * HEADS:],
                            w_t[BA_ORIG:BA_ORIG + 2 * HEADS], pad], axis=0)


_BIG = ("w_in", "a_proj", "b_proj", "c_proj", "w_out")
_SHARD_AXIS = {"w_in": 2, "a_proj": 2, "b_proj": 1, "c_proj": 2, "w_out": 1, "a_dw": 2, "b_conv": 2}
_BIG_AXIS = _SHARD_AXIS


def _join_chips(g, axis):
    g = jnp.moveaxis(g, 0, axis)
    return g.reshape(g.shape[:axis] + (4 * g.shape[axis + 1],) + g.shape[axis + 2:])


def _split_chips(a, axis):
    n = a.shape[axis] // 4
    by_chip = jnp.moveaxis(a.reshape(a.shape[:axis] + (4, n) + a.shape[axis + 1:]), axis, 0)
    return jnp.moveaxis(by_chip.reshape(4, 2, by_chip.shape[1] // 2, by_chip.shape[2]), 1, 0)
_ORDER = ("norm_g", "w_in", "a_dw", "a_dw_b", "a_ln_g", "a_ln_b", "a_proj", "b_conv", "b_a_log", "b_dt_bias",
          "b_onorm_g", "b_proj", "c_ln_g", "c_ln_b", "c_ws", "c_bs", "c_proj", "w_out", "final_g")


def _reduce_begin(layer_grads, where, name):
    by_chip = [_split_chips(layer_grads[k], _SHARD_AXIS[k] - 1) for k in _BIG]
    theirs = _pair_partials_call(by_chip, name + "pair_partials")
    return [_add_pair_call(b, r, where, name + "pair_sum_" + k) for k, b, r in zip(_BIG, by_chip, theirs)]


def _reduce_end(sums, got, where, name):
    fins = [_add_recv_call(s[1], r, where, name + "chip_sum_" + k) for k, s, r in zip(_BIG, sums, got)]
    return [o.reshape((-1,) + o.shape[2:]) for o in _pair_result_call(fins, name + "pair_result")]


def _local_step(x, tgt, first, rest, final_g, where=None):
    bsz, t, _ = x.shape
    m = bsz * t
    depth = 1 + len(rest)
    row = lambda v: v.reshape(1, -1)
    saved, xl, w = [], x, first
    for l in range(depth):
        n = f"l{l}_"
        w_rt = _to_r(w["w_in_t"])
        par = dict(
            adw=w["a_dw"], adb=row(w["a_dw_b"]), alg=row(w["a_ln_g"]), alb=row(w["a_ln_b"]), bconv=w["b_conv"],
            alog=jnp.pad(row(w["b_a_log"]), ((0, 0), (HEADS, DK - 2 * HEADS))),
            dtb=jnp.pad(row(w["b_dt_bias"]), ((0, 0), (HEADS, DK - 2 * HEADS))), og=row(w["b_onorm_g"]),
            clg=row(w["c_ln_g"]), clb=row(w["c_ln_b"]), cws=w["c_ws"], cbias=jnp.repeat(w["c_bs"].T, SG_C, axis=1),
            ng=row(w["norm_g"]), wa=w["a_proj"], wb=w["b_proj"], wc=w["c_proj"], wo=w["w_out"], wrt=w_rt, wr=w_rt.T)
        h = _rms_fwd_call(xl.reshape(m, D_MODEL), par["ng"], n + "norm")
        proj = _matmul(h, par["wr"], F32, 2048, 1152, 1024, n + "in_proj").reshape(bsz, t, N_R)
        ya, conv = _a_fwd_call(proj, par["adw"], par["adb"], par["alg"], par["alb"], n + "conv_fwd")
        shards, assemble = rest[l] if l < len(rest) else ((), None)
        yb, *dn_saved = _dn_fwd_call(proj, par["bconv"], par["alog"], par["dtb"], par["og"], n + "delta_fwd", shards)
        dn_saved, gathered = dn_saved[:6], dn_saved[6:]
        if assemble is not None:
            w = assemble(gathered)
        yc = _c_fwd_call(proj, par["clg"], par["clb"], par["cws"], par["cbias"], n + "gmlp_fwd")
        x_next = _m_fwd_call(ya, yb, yc, proj, xl, par["wa"], par["wb"], par["wc"], par["wo"], n + "merge_fwd")
        saved.append((par, xl, h, proj, ya, yb, yc, conv, dn_saved))
        xl = x_next
    dout, dfg, loss = _loss_call(xl.reshape(m, D_MODEL), tgt.reshape(m, D_MODEL), row(final_g))
    dout = dout.reshape(bsz, t, D_MODEL)
    g = {k: [None] * depth for k in _ORDER if k != "final_g"}
    pending = None
    for l in reversed(range(depth)):
        n = f"l{l}_"
        par, xl, h, proj, ya, yb, yc, conv, dn_saved = saved[l]
        dya, dyb, dyc, dgate, merged, dps = _m_bwd_call(dout, ya, yb, yc, proj, par["wa"], par["wb"], par["wc"],
                                                        par["wo"], n + "merge_bwd")
        g["a_proj"][l], g["b_proj"][l], g["c_proj"][l], g["w_out"][l] = _m_wgrad_call(ya, yb, yc, merged, dps, dout,
                                                                                      n + "merge_wgrad")
        travel = [s[0] for s in pending[1]] if pending else []
        da, dadw, dasm, *got = _a_bwd_call(dya, conv, proj, par["adw"], par["alg"], par["alb"], n + "conv_bwd", travel)
        if pending:
            for k, r in zip(_BIG, _reduce_end(pending[1], got, where, f"l{pending[0]}_")):
                g[k][pending[0]] = r
            pending = None
        db, dba, dbconv, dbsm = _dn_bwd_call(dyb, dn_saved, proj, par["bconv"], par["alog"], par["dtb"], par["og"],
                                             n + "delta_bwd")
        dc, dcws, dcbs, dcsm = _c_bwd_call(dyc, proj, par["clg"], par["clb"], par["cws"], par["cbias"], n + "gmlp_bwd")
        segs = [s.reshape(m, s.shape[-1]) for s in (db, da, dc, dgate, dba)]
        ht = h.T
        dwb, dwa, dwc, dwg, dwba = [_matmul(ht, s, F32, 1024, 1024 if s.shape[1] % 1024 == 0 else 768, 2048,
                                            n + "in_proj_dw_" + tag) for s, tag in zip(segs, "bacgs")]
        g["w_in"][l] = jnp.concatenate([dwa, dwb, dwba[:, :2 * HEADS], dwc, dwg], axis=1)
        if where is None:
            dh = _matmul_ksegs(segs, par["wrt"], 2048, 1024, 512, n + "in_proj_dx")
        else:
            sums = _reduce_begin({k: g[k][l] for k in _BIG}, where, n)
            if l > 0:
                pending = (l, sums)
                dh = _matmul_ksegs(segs, par["wrt"], 2048, 1024, 512, n + "in_proj_dx")
            else:
                dh, got = _matmul_ksegs(segs, par["wrt"], 2048, 1024, 512, n + "in_proj_dx", [s[0] for s in sums])
                for k, r in zip(_BIG, _reduce_end(sums, got, where, n)):
                    g[k][l] = r
        dx, dng = _rms_bwd_call(xl.reshape(m, D_MODEL), dh, par["ng"], dout.reshape(m, D_MODEL), n + "norm_bwd")
        dout = dx.reshape(bsz, t, D_MODEL)
        g["norm_g"][l] = dng[0]
        g["a_dw"][l], g["a_dw_b"][l], g["a_ln_g"][l], g["a_ln_b"][l] = dadw[:CONV_K], dasm[0], dasm[1], dasm[2]
        g["b_conv"][l], g["b_a_log"][l], g["b_dt_bias"][l] = dbconv[:SHORT_K], dbsm[0, :HEADS], dbsm[1, :HEADS]
        g["b_onorm_g"][l] = dbsm[2]
        g["c_ln_g"][l], g["c_ln_b"][l], g["c_ws"][l] = dcsm[0], dcsm[1], dcws
        g["c_bs"][l] = dcbs.reshape(SG_C, SG_G, SG_C).sum(-1).T
    grads = {k: jnp.stack(v) for k, v in g.items()}
    grads["final_g"] = dfg[0]
    return loss[0, 0], dout, grads


def kernel(x, norm_g, w_in, a_dw, a_dw_b, a_ln_g, a_ln_b, a_proj, b_conv, b_a_log, b_dt_bias, b_onorm_g, b_proj, c_ln_g, c_ln_b, c_ws, c_bs, c_proj, w_out, final_g, loss_target, m_norm_g, m_w_in, m_a_dw, m_a_dw_b, m_a_ln_g, m_a_ln_b, m_a_proj, m_b_conv, m_b_a_log, m_b_dt_bias, m_b_onorm_g, m_b_proj, m_c_ln_g, m_c_ln_b, m_c_ws, m_c_bs, m_c_proj, m_w_out, m_final_g, v_norm_g, v_w_in, v_a_dw, v_a_dw_b, v_a_ln_g, v_a_ln_b, v_a_proj, v_b_conv, v_b_a_log, v_b_dt_bias, v_b_onorm_g, v_b_proj, v_c_ln_g, v_c_ln_b, v_c_ws, v_c_bs, v_c_proj, v_w_out, v_final_g):
    given = dict(locals())
    ws = {k: given[k] for k in _ORDER}
    xi, yi, ci = _axes()
    chip = 2 * xi + yi
    where = jnp.stack([ci, chip]).astype(jnp.int32)

    depth = norm_g.shape[0]
    w_t = jnp.transpose(w_in, (2, 0, 1)).astype(BF16)
    sharded = ("w_in_t",) + _BIG[1:] + ("a_dw", "b_conv")

    def rows_to(a, n):
        return jnp.pad(a, ((0, n - a.shape[0]), (0, 0)))

    def shards(l):
        return ([rows_to(w_t[:, l], GATHER_ROWS)] + [ws[k][l].astype(BF16) for k in _BIG[1:]]
                + [rows_to(a_dw[l], HALO_A), rows_to(b_conv[l], 16)])

    def assemble(l):
        def full_weights(gathered):
            lw = {k: ws[k][l] for k in _ORDER if k not in sharded and k not in ("w_in", "final_g")}
            for k, g in zip(sharded, gathered):
                lw[k] = _join_chips(g[:, :w_t.shape[0]] if k == "w_in_t" else g, 0 if k == "w_in_t" else _SHARD_AXIS[k] - 1)
            lw["b_conv"] = lw["b_conv"][:SHORT_K]
            return lw
        return full_weights

    first = assemble(0)(_gather_weights_call(shards(0)))

    loss, grad_x, grads = _local_step(x, loss_target, first, [(shards(l), assemble(l)) for l in range(1, depth)],
                                      final_g, where)
    out_g = {k: grads[k] for k in _BIG}

    rest = [k for k in _ORDER if k not in _BIG]
    rest_shapes = [grads[k].shape for k in rest] + [(1,)]
    summed = _unpack_rows(_allreduce_small_call(_pack_rows([grads[k] for k in rest] + [loss.reshape(1)], F32)),
                          rest_shapes)
    out_g.update(zip(rest, summed[:-1]))
    out_g["a_dw"] = lax.dynamic_slice_in_dim(out_g["a_dw"], chip * a_dw.shape[2], a_dw.shape[2], axis=2)
    out_g["b_conv"] = lax.dynamic_slice_in_dim(out_g["b_conv"], chip * b_conv.shape[2], b_conv.shape[2], axis=2)

    def adam(k):
        operands = (ws[k], out_g[k], given["m_" + k], given["v_" + k])
        if k != "w_in":
            return _adamw_call(*operands, "adamw_" + k)
        w_t, g_t, m_t, v_t = (jnp.transpose(a, (2, 0, 1)) for a in operands)
        out_g[k] = jnp.transpose(g_t, (1, 2, 0))
        return tuple(jnp.transpose(o, (1, 2, 0)) for o in _adamw_call(w_t, g_t, m_t, v_t, "adamw_" + k))

    upd = {k: adam(k) for k in _ORDER}
    return (summed[-1][0], grad_x, *[out_g[k] for k in _ORDER], *[upd[k][0] for k in _ORDER],
            *[upd[k][1] for k in _ORDER], *[upd[k][2] for k in _ORDER])
```
